```python
import math
import jax, jax.numpy as jnp
from jax import lax
import numpy as np

D_MODEL = 1024
BATCH = 8
SEQ = 8192
DEPTH = 1

HEAD_DIM = 128
ATTN_GROUPS = ((128, 1), (512, 4), (2048, 16))
N_GROUPS = 3
HEADS_PER_GROUP = 4
ATTN_QKV_W = N_GROUPS * HEADS_PER_GROUP * HEAD_DIM
ATTN_OUT = HEADS_PER_GROUP * HEAD_DIM
CONV_CH = 512
CONV_WIDTH = 31
N_MEM = 256
N_XHEADS = 4
XATTN_W = N_XHEADS * HEAD_DIM
N_BRANCH = 3
D_FF = 2816
FFN_CONV_WIDTH = 3
N_BUCKETS = 32
MAX_DISTANCE = 2048
N_REL_HEADS = N_GROUPS * HEADS_PER_GROUP
RMS_EPS = 1e-6
LN_EPS = 1e-5
N_IN = 3 * ATTN_QKV_W + 2 * CONV_CH + XATTN_W + N_BRANCH * D_MODEL
IN_SPLITS = (ATTN_QKV_W, 2 * ATTN_QKV_W, 3 * ATTN_QKV_W,
             3 * ATTN_QKV_W + CONV_CH, 3 * ATTN_QKV_W + 2 * CONV_CH,
             3 * ATTN_QKV_W + 2 * CONV_CH + XATTN_W)

kernel_name = "hybrid_gated_dilated_conformer_block"


def rms_norm(x, w, eps=RMS_EPS):
    xf = x.astype(jnp.float32)
    y = xf * lax.rsqrt(jnp.mean(xf * xf, axis=-1, keepdims=True) + eps)
    return (y * w.astype(jnp.float32)).astype(x.dtype)


def layer_norm(x, w, b, eps=LN_EPS):
    xf = x.astype(jnp.float32)
    mu = jnp.mean(xf, axis=-1, keepdims=True)
    xc = xf - mu
    y = xc * lax.rsqrt(jnp.mean(xc * xc, axis=-1, keepdims=True) + eps)
    return (y * w.astype(jnp.float32) + b.astype(jnp.float32)).astype(x.dtype)


def causal_depthwise_conv(x, w, b):
    K, C = w.shape
    y = lax.conv_general_dilated(
        x, w[:, None, :].astype(x.dtype), window_strides=(1,), padding=[(K - 1, 0)],
        dimension_numbers=("NWC", "WIO", "NWC"), feature_group_count=C)
    return y + b.astype(x.dtype)


def t5_bucket(dist):
    max_exact = N_BUCKETS // 2
    d = jnp.maximum(dist.astype(jnp.float32), 1.0)
    large = max_exact + (jnp.log(d / max_exact) / math.log(MAX_DISTANCE / max_exact)
                         * (N_BUCKETS - max_exact)).astype(jnp.int32)
    large = jnp.minimum(large, N_BUCKETS - 1)
    return jnp.where(dist < max_exact, dist, large)


def dilated_window_attention(q, k, v, bias_cols, window, dilation):
    B, S, H, Dh = q.shape
    n = window // dilation
    span = n * dilation
    s_pad = -(-S // span) * span
    nb = s_pad // span

    def to_blocks(a):
        a = jnp.pad(a, ((0, 0), (0, s_pad - S), (0, 0), (0, 0)))
        a = a.reshape(B, nb * n, dilation, H, Dh).transpose(0, 2, 1, 3, 4)
        return a.reshape(B, dilation, nb, n, H, Dh)

    def with_prev(a):
        prev = jnp.pad(a, ((0, 0), (0, 0), (1, 0), (0, 0), (0, 0), (0, 0)))[:, :, :-1]
        return jnp.concatenate([prev, a], axis=3)

    qb = to_blocks(q)
    kk = with_prev(to_blocks(k))
    vv = with_prev(to_blocks(v))

    qi = jnp.arange(n)[:, None]
    kj = jnp.arange(2 * n)[None, :]
    step = qi + n - kj
    band = (step >= 0) & (step <= n)
    not_before_start = (jnp.arange(nb)[:, None, None] > 0) | (kj[None] >= n)
    mask = band[None] & not_before_start
    bucket = t5_bucket(jnp.clip(step, 0, None) * dilation)
    bias = jnp.transpose(bias_cols[bucket].astype(jnp.float32), (2, 0, 1))

    scores = jnp.einsum("brnqhd,brnkhd->brnhqk", qb, kk).astype(jnp.float32) * (Dh ** -0.5) + bias
    scores = jnp.where(mask[:, None], scores, -jnp.inf)
    m = jnp.max(scores, axis=-1, keepdims=True)
    p = jnp.exp(scores - m)
    l = jnp.sum(p, axis=-1, keepdims=True)
    o = jnp.einsum("brnhqk,brnkhd->brnqhd", p.astype(v.dtype), vv).astype(jnp.float32)
    l_t = jnp.swapaxes(l[..., 0], 3, 4)
    lse_t = jnp.swapaxes((m + jnp.log(l))[..., 0], 3, 4)
    o = o / l_t[..., None]

    def from_blocks(a):
        a = a.reshape(B, dilation, nb * n, H, -1).transpose(0, 2, 1, 3, 4)
        return a.reshape(B, s_pad, H, -1)[:, :S]

    return from_blocks(o), from_blocks(lse_t[..., None])[..., 0]


def memory_cross_attention(zq, mem, mem_norm_w, w_mem_kv, xq_norm_w, xk_norm_w):
    B, S, _ = zq.shape
    mem_n = rms_norm(mem, mem_norm_w)
    kv = mem_n @ w_mem_kv
    mk, mv = jnp.split(kv, 2, axis=-1)
    mk = rms_norm(mk.reshape(B, -1, N_XHEADS, HEAD_DIM), xk_norm_w)
    mv = mv.reshape(B, -1, N_XHEADS, HEAD_DIM)
    q = rms_norm(zq.reshape(B, S, N_XHEADS, HEAD_DIM), xq_norm_w)
    s = jnp.einsum("bshd,bmhd->bhsm", q, mk).astype(jnp.float32) * (HEAD_DIM ** -0.5)
    p = jax.nn.softmax(s, axis=-1).astype(mv.dtype)
    o = jnp.einsum("bhsm,bmhd->bshd", p, mv)
    return o.reshape(B, S, XATTN_W)


def _fwd_setup_inputs(seed: int = 0) -> dict:
    key = jax.random.key(seed)
    ks = iter(jax.random.split(key, 32))
    nrm = lambda shape, scale: scale * jax.random.normal(next(ks), shape, jnp.float32)
    L = DEPTH
    return {
        "x": nrm((BATCH, SEQ, D_MODEL), 1.0),
        "mem": nrm((BATCH, N_MEM, D_MODEL), 1.0),
        "rel_bias_table": nrm((N_BUCKETS, N_REL_HEADS), 0.2),
        "attn_norm_w": 1.0 + nrm((L, D_MODEL), 0.02),
        "w_in": nrm((L, D_MODEL, N_IN), D_MODEL ** -0.5),
        "b_gate": nrm((L, N_BRANCH * D_MODEL), 0.02),
        "q_norm_w": 1.0 + nrm((L, N_GROUPS, HEAD_DIM), 0.02),
        "k_norm_w": 1.0 + nrm((L, N_GROUPS, HEAD_DIM), 0.02),
        "w_attn_o": nrm((L, ATTN_OUT, D_MODEL), ATTN_OUT ** -0.5),
        "conv_dw_w": nrm((L, CONV_WIDTH, CONV_CH), CONV_WIDTH ** -0.5),
        "conv_dw_b": nrm((L, CONV_CH), 0.02),
        "conv_ln_w": 1.0 + nrm((L, CONV_CH), 0.02),
        "conv_ln_b": nrm((L, CONV_CH), 0.02),
        "w_conv_o": nrm((L, CONV_CH, D_MODEL), CONV_CH ** -0.5),
        "mem_norm_w": 1.0 + nrm((L, D_MODEL), 0.02),
        "w_mem_kv": nrm((L, D_MODEL, 2 * XATTN_W), D_MODEL ** -0.5),
        "xq_norm_w": 1.0 + nrm((L, HEAD_DIM), 0.02),
        "xk_norm_w": 1.0 + nrm((L, HEAD_DIM), 0.02),
        "w_cross_o": nrm((L, XATTN_W, D_MODEL), XATTN_W ** -0.5),
        "w_out": nrm((L, D_MODEL, D_MODEL), D_MODEL ** -0.5),
        "ffn_norm_w": 1.0 + nrm((L, D_MODEL), 0.02),
        "w_up": nrm((L, D_MODEL, 2 * D_FF), D_MODEL ** -0.5),
        "ffn_conv_w": nrm((L, FFN_CONV_WIDTH, 2 * D_FF), FFN_CONV_WIDTH ** -0.5),
        "ffn_conv_b": nrm((L, 2 * D_FF), 0.02),
        "w_down": nrm((L, D_FF, D_MODEL), D_FF ** -0.5),
    }


def _fwd_reference(x, mem, rel_bias_table, attn_norm_w, w_in, b_gate, q_norm_w, k_norm_w, w_attn_o,
              conv_dw_w, conv_dw_b, conv_ln_w, conv_ln_b, w_conv_o, mem_norm_w, w_mem_kv,
              xq_norm_w, xk_norm_w, w_cross_o, w_out, ffn_norm_w, w_up, ffn_conv_w, ffn_conv_b,
              w_down):
    B, S, _ = x.shape
    h = x
    for l in range(DEPTH):
        xn = rms_norm(h, attn_norm_w[l])
        z = xn @ w_in[l]
        zq, zk, zv, zc_val, zc_gate, zx_q, zg = jnp.split(z, IN_SPLITS, axis=-1)

        q = rms_norm(zq.reshape(B, S, N_GROUPS, HEADS_PER_GROUP, HEAD_DIM), q_norm_w[l][:, None, :])
        k = rms_norm(zk.reshape(B, S, N_GROUPS, HEADS_PER_GROUP, HEAD_DIM), k_norm_w[l][:, None, :])
        v = zv.reshape(B, S, N_GROUPS, HEADS_PER_GROUP, HEAD_DIM)
        outs, lses = [], []
        for g, (win, dil) in enumerate(ATTN_GROUPS):
            cols = rel_bias_table[:, g * HEADS_PER_GROUP:(g + 1) * HEADS_PER_GROUP]
            o_g, lse_g = dilated_window_attention(q[:, :, g], k[:, :, g], v[:, :, g], cols, win, dil)
            outs.append(o_g)
            lses.append(lse_g)
        wts = jax.nn.softmax(jnp.stack(lses), axis=0)
        attn = jnp.sum(wts[..., None] * jnp.stack(outs), axis=0).astype(x.dtype)
        y_attn = attn.reshape(B, S, ATTN_OUT) @ w_attn_o[l]

        u = zc_val * jax.nn.sigmoid(zc_gate)
        u = causal_depthwise_conv(u, conv_dw_w[l], conv_dw_b[l])
        u = jax.nn.silu(layer_norm(u, conv_ln_w[l], conv_ln_b[l]))
        y_conv = u @ w_conv_o[l]

        y_cross = memory_cross_attention(zx_q, mem, mem_norm_w[l], w_mem_kv[l],
                                         xq_norm_w[l], xk_norm_w[l]) @ w_cross_o[l]

        gates = jax.nn.sigmoid((zg + b_gate[l]).astype(jnp.float32)).astype(x.dtype)
        gates = gates.reshape(B, S, N_BRANCH, D_MODEL)
        merged = gates[:, :, 0] * y_attn + gates[:, :, 1] * y_conv + gates[:, :, 2] * y_cross
        h = h + merged @ w_out[l]

        hn = rms_norm(h, ffn_norm_w[l])
        up = causal_depthwise_conv(hn @ w_up[l], ffn_conv_w[l], ffn_conv_b[l])
        a, gt = jnp.split(up, 2, axis=-1)
        h = h + (jax.nn.silu(gt) * a) @ w_down[l]
    return h


import jax as _jax
import jax.numpy as _jnp

TWIN_FORMAT = 'train_step'
FWD_PARAMS = ['x', 'mem', 'rel_bias_table', 'attn_norm_w', 'w_in', 'b_gate', 'q_norm_w', 'k_norm_w', 'w_attn_o', 'conv_dw_w', 'conv_dw_b', 'conv_ln_w', 'conv_ln_b', 'w_conv_o', 'mem_norm_w', 'w_mem_kv', 'xq_norm_w', 'xk_norm_w', 'w_cross_o', 'w_out', 'ffn_norm_w', 'w_up', 'ffn_conv_w', 'ffn_conv_b', 'w_down']
TWIN_WEIGHTS = ['rel_bias_table', 'attn_norm_w', 'w_in', 'b_gate', 'q_norm_w', 'k_norm_w', 'w_attn_o', 'conv_dw_w', 'conv_dw_b', 'conv_ln_w', 'conv_ln_b', 'w_conv_o', 'mem_norm_w', 'w_mem_kv', 'xq_norm_w', 'xk_norm_w', 'w_cross_o', 'w_out', 'ffn_norm_w', 'w_up', 'ffn_conv_w', 'ffn_conv_b', 'w_down']
TWIN_DIFF_INPUT = 'x'
TWIN_INPUTS = ['x', 'mem', 'rel_bias_table', 'attn_norm_w', 'w_in', 'b_gate', 'q_norm_w', 'k_norm_w', 'w_attn_o', 'conv_dw_w', 'conv_dw_b', 'conv_ln_w', 'conv_ln_b', 'w_conv_o', 'mem_norm_w', 'w_mem_kv', 'xq_norm_w', 'xk_norm_w', 'w_cross_o', 'w_out', 'ffn_norm_w', 'w_up', 'ffn_conv_w', 'ffn_conv_b', 'w_down', 'loss_target', 'm_rel_bias_table', 'm_attn_norm_w', 'm_w_in', 'm_b_gate', 'm_q_norm_w', 'm_k_norm_w', 'm_w_attn_o', 'm_conv_dw_w', 'm_conv_dw_b', 'm_conv_ln_w', 'm_conv_ln_b', 'm_w_conv_o', 'm_mem_norm_w', 'm_w_mem_kv', 'm_xq_norm_w', 'm_xk_norm_w', 'm_w_cross_o', 'm_w_out', 'm_ffn_norm_w', 'm_w_up', 'm_ffn_conv_w', 'm_ffn_conv_b', 'm_w_down', 'v_rel_bias_table', 'v_attn_norm_w', 'v_w_in', 'v_b_gate', 'v_q_norm_w', 'v_k_norm_w', 'v_w_attn_o', 'v_conv_dw_w', 'v_conv_dw_b', 'v_conv_ln_w', 'v_conv_ln_b', 'v_w_conv_o', 'v_mem_norm_w', 'v_w_mem_kv', 'v_xq_norm_w', 'v_xk_norm_w', 'v_w_cross_o', 'v_w_out', 'v_ffn_norm_w', 'v_w_up', 'v_ffn_conv_w', 'v_ffn_conv_b', 'v_w_down']
TWIN_OUTPUTS = ['loss', 'grad_x', 'grad_rel_bias_table', 'grad_attn_norm_w', 'grad_w_in', 'grad_b_gate', 'grad_q_norm_w', 'grad_k_norm_w', 'grad_w_attn_o', 'grad_conv_dw_w', 'grad_conv_dw_b', 'grad_conv_ln_w', 'grad_conv_ln_b', 'grad_w_conv_o', 'grad_mem_norm_w', 'grad_w_mem_kv', 'grad_xq_norm_w', 'grad_xk_norm_w', 'grad_w_cross_o', 'grad_w_out', 'grad_ffn_norm_w', 'grad_w_up', 'grad_ffn_conv_w', 'grad_ffn_conv_b', 'grad_w_down', 'delta_rel_bias_table', 'delta_attn_norm_w', 'delta_w_in', 'delta_b_gate', 'delta_q_norm_w', 'delta_k_norm_w', 'delta_w_attn_o', 'delta_conv_dw_w', 'delta_conv_dw_b', 'delta_conv_ln_w', 'delta_conv_ln_b', 'delta_w_conv_o', 'delta_mem_norm_w', 'delta_w_mem_kv', 'delta_xq_norm_w', 'delta_xk_norm_w', 'delta_w_cross_o', 'delta_w_out', 'delta_ffn_norm_w', 'delta_w_up', 'delta_ffn_conv_w', 'delta_ffn_conv_b', 'delta_w_down', 'new_m_rel_bias_table', 'new_m_attn_norm_w', 'new_m_w_in', 'new_m_b_gate', 'new_m_q_norm_w', 'new_m_k_norm_w', 'new_m_w_attn_o', 'new_m_conv_dw_w', 'new_m_conv_dw_b', 'new_m_conv_ln_w', 'new_m_conv_ln_b', 'new_m_w_conv_o', 'new_m_mem_norm_w', 'new_m_w_mem_kv', 'new_m_xq_norm_w', 'new_m_xk_norm_w', 'new_m_w_cross_o', 'new_m_w_out', 'new_m_ffn_norm_w', 'new_m_w_up', 'new_m_ffn_conv_w', 'new_m_ffn_conv_b', 'new_m_w_down', 'new_v_rel_bias_table', 'new_v_attn_norm_w', 'new_v_w_in', 'new_v_b_gate', 'new_v_q_norm_w', 'new_v_k_norm_w', 'new_v_w_attn_o', 'new_v_conv_dw_w', 'new_v_conv_dw_b', 'new_v_conv_ln_w', 'new_v_conv_ln_b', 'new_v_w_conv_o', 'new_v_mem_norm_w', 'new_v_w_mem_kv', 'new_v_xq_norm_w', 'new_v_xk_norm_w', 'new_v_w_cross_o', 'new_v_w_out', 'new_v_ffn_norm_w', 'new_v_w_up', 'new_v_ffn_conv_w', 'new_v_ffn_conv_b', 'new_v_w_down']
TWIN_LEAF_KINDS = {'loss': 'loss', 'grad_x': 'grad_x', 'grad_rel_bias_table': 'grad_w', 'grad_attn_norm_w': 'grad_w', 'grad_w_in': 'grad_w', 'grad_b_gate': 'grad_w', 'grad_q_norm_w': 'grad_w', 'grad_k_norm_w': 'grad_w', 'grad_w_attn_o': 'grad_w', 'grad_conv_dw_w': 'grad_w', 'grad_conv_dw_b': 'grad_w', 'grad_conv_ln_w': 'grad_w', 'grad_conv_ln_b': 'grad_w', 'grad_w_conv_o': 'grad_w', 'grad_mem_norm_w': 'grad_w', 'grad_w_mem_kv': 'grad_w', 'grad_xq_norm_w': 'grad_w', 'grad_xk_norm_w': 'grad_w', 'grad_w_cross_o': 'grad_w', 'grad_w_out': 'grad_w', 'grad_ffn_norm_w': 'grad_w', 'grad_w_up': 'grad_w', 'grad_ffn_conv_w': 'grad_w', 'grad_ffn_conv_b': 'grad_w', 'grad_w_down': 'grad_w', 'delta_rel_bias_table': 'delta_w', 'delta_attn_norm_w': 'delta_w', 'delta_w_in': 'delta_w', 'delta_b_gate': 'delta_w', 'delta_q_norm_w': 'delta_w', 'delta_k_norm_w': 'delta_w', 'delta_w_attn_o': 'delta_w', 'delta_conv_dw_w': 'delta_w', 'delta_conv_dw_b': 'delta_w', 'delta_conv_ln_w': 'delta_w', 'delta_conv_ln_b': 'delta_w', 'delta_w_conv_o': 'delta_w', 'delta_mem_norm_w': 'delta_w', 'delta_w_mem_kv': 'delta_w', 'delta_xq_norm_w': 'delta_w', 'delta_xk_norm_w': 'delta_w', 'delta_w_cross_o': 'delta_w', 'delta_w_out': 'delta_w', 'delta_ffn_norm_w': 'delta_w', 'delta_w_up': 'delta_w', 'delta_ffn_conv_w': 'delta_w', 'delta_ffn_conv_b': 'delta_w', 'delta_w_down': 'delta_w', 'new_m_rel_bias_table': 'new_m', 'new_m_attn_norm_w': 'new_m', 'new_m_w_in': 'new_m', 'new_m_b_gate': 'new_m', 'new_m_q_norm_w': 'new_m', 'new_m_k_norm_w': 'new_m', 'new_m_w_attn_o': 'new_m', 'new_m_conv_dw_w': 'new_m', 'new_m_conv_dw_b': 'new_m', 'new_m_conv_ln_w': 'new_m', 'new_m_conv_ln_b': 'new_m', 'new_m_w_conv_o': 'new_m', 'new_m_mem_norm_w': 'new_m', 'new_m_w_mem_kv': 'new_m', 'new_m_xq_norm_w': 'new_m', 'new_m_xk_norm_w': 'new_m', 'new_m_w_cross_o': 'new_m', 'new_m_w_out': 'new_m', 'new_m_ffn_norm_w': 'new_m', 'new_m_w_up': 'new_m', 'new_m_ffn_conv_w': 'new_m', 'new_m_ffn_conv_b': 'new_m', 'new_m_w_down': 'new_m', 'new_v_rel_bias_table': 'new_v', 'new_v_attn_norm_w': 'new_v', 'new_v_w_in': 'new_v', 'new_v_b_gate': 'new_v', 'new_v_q_norm_w': 'new_v', 'new_v_k_norm_w': 'new_v', 'new_v_w_attn_o': 'new_v', 'new_v_conv_dw_w': 'new_v', 'new_v_conv_dw_b': 'new_v', 'new_v_conv_ln_w': 'new_v', 'new_v_conv_ln_b': 'new_v', 'new_v_w_conv_o': 'new_v', 'new_v_mem_norm_w': 'new_v', 'new_v_w_mem_kv': 'new_v', 'new_v_xq_norm_w': 'new_v', 'new_v_xk_norm_w': 'new_v', 'new_v_w_cross_o': 'new_v', 'new_v_w_out': 'new_v', 'new_v_ffn_norm_w': 'new_v', 'new_v_w_up': 'new_v', 'new_v_ffn_conv_w': 'new_v', 'new_v_ffn_conv_b': 'new_v', 'new_v_w_down': 'new_v'}


def _forward(args):
    return _fwd_reference(*[args[k] for k in FWD_PARAMS])


def _output_shape():
    def fwd():
        inp = _fwd_setup_inputs(0)
        return _fwd_reference(*[inp[k] for k in FWD_PARAMS])
    out = _jax.eval_shape(fwd)
    return out.shape, out.dtype

N_MICROBATCH = 1
ADAM_LR = 0.001
ADAM_B1 = 0.9
ADAM_B2 = 0.999
ADAM_EPS = 1e-08
ADAM_WD = 0.01
ADAM_STEP = 10
PER_EXAMPLE_BATCH_AXIS = {'x': 0, 'mem': 0, 'loss_target': 0}
SHARED_INPUTS = []
_WEIGHT_DTYPES = {'rel_bias_table': _jnp.float32, 'attn_norm_w': _jnp.float32, 'w_in': _jnp.float32, 'b_gate': _jnp.float32, 'q_norm_w': _jnp.float32, 'k_norm_w': _jnp.float32, 'w_attn_o': _jnp.float32, 'conv_dw_w': _jnp.float32, 'conv_dw_b': _jnp.float32, 'conv_ln_w': _jnp.float32, 'conv_ln_b': _jnp.float32, 'w_conv_o': _jnp.float32, 'mem_norm_w': _jnp.float32, 'w_mem_kv': _jnp.float32, 'xq_norm_w': _jnp.float32, 'xk_norm_w': _jnp.float32, 'w_cross_o': _jnp.float32, 'w_out': _jnp.float32, 'ffn_norm_w': _jnp.float32, 'w_up': _jnp.float32, 'ffn_conv_w': _jnp.float32, 'ffn_conv_b': _jnp.float32, 'w_down': _jnp.float32}
MOMENT_SCALE = {'rel_bias_table': 1.377878e-01, 'attn_norm_w': 6.633062e-01, 'w_in': 9.742788e-02, 'b_gate': 1.506177e+00, 'q_norm_w': 2.942233e-01, 'k_norm_w': 2.953547e-01, 'w_attn_o': 8.040575e-02, 'conv_dw_w': 7.561736e-01, 'conv_dw_b': 1.502830e+01, 'conv_ln_w': 1.757802e+01, 'conv_ln_b': 1.351536e+01, 'w_conv_o': 2.402701e+00, 'mem_norm_w': 2.898324e-01, 'w_mem_kv': 2.120311e-01, 'xq_norm_w': 1.555201e+00, 'xk_norm_w': 1.559930e+00, 'w_cross_o': 2.317448e-01, 'w_out': 2.252204e+00, 'ffn_norm_w': 5.294340e+01, 'w_up': 8.924974e-01, 'ffn_conv_w': 7.562341e+00, 'ffn_conv_b': 6.703548e+00, 'w_down': 6.750046e-01}


def _to_microbatches(a, axis):
    t = _jnp.moveaxis(a, axis, 0)
    t = t.reshape((N_MICROBATCH, t.shape[0] // N_MICROBATCH) + t.shape[1:])
    return _jnp.moveaxis(t, 1, axis + 1)


def setup_inputs(seed: int = 0) -> dict:
    inp = _fwd_setup_inputs(seed)
    key = _jax.random.fold_in(_jax.random.key(seed), 7919)
    shape, _ = _output_shape()
    out = dict(inp)
    out["loss_target"] = _jax.random.normal(_jax.random.fold_in(key, 0), shape, _jnp.float32)
    for i, name in enumerate(TWIN_WEIGHTS):
        w = inp[name].astype(_jnp.float32)
        if MOMENT_SCALE is None:
            s = _jnp.sqrt(_jnp.mean(_jnp.square(w)) + 1e-30)
        else:
            s = MOMENT_SCALE[name]
        km, kv = _jax.random.split(_jax.random.fold_in(key, i + 1))
        out[name] = w
        out["m_" + name] = s * _jax.random.normal(km, w.shape, _jnp.float32)
        out["v_" + name] = (s * s) * _jax.random.uniform(kv, w.shape, _jnp.float32, 0.5, 1.5)
    if N_MICROBATCH > 1:
        for name, axis in PER_EXAMPLE_BATCH_AXIS.items():
            out[name] = _to_microbatches(out[name], axis)
    return {'x': out['x'], 'mem': out['mem'], 'rel_bias_table': out['rel_bias_table'], 'attn_norm_w': out['attn_norm_w'], 'w_in': out['w_in'], 'b_gate': out['b_gate'], 'q_norm_w': out['q_norm_w'], 'k_norm_w': out['k_norm_w'], 'w_attn_o': out['w_attn_o'], 'conv_dw_w': out['conv_dw_w'], 'conv_dw_b': out['conv_dw_b'], 'conv_ln_w': out['conv_ln_w'], 'conv_ln_b': out['conv_ln_b'], 'w_conv_o': out['w_conv_o'], 'mem_norm_w': out['mem_norm_w'], 'w_mem_kv': out['w_mem_kv'], 'xq_norm_w': out['xq_norm_w'], 'xk_norm_w': out['xk_norm_w'], 'w_cross_o': out['w_cross_o'], 'w_out': out['w_out'], 'ffn_norm_w': out['ffn_norm_w'], 'w_up': out['w_up'], 'ffn_conv_w': out['ffn_conv_w'], 'ffn_conv_b': out['ffn_conv_b'], 'w_down': out['w_down'], 'loss_target': out['loss_target'], 'm_rel_bias_table': out['m_rel_bias_table'], 'm_attn_norm_w': out['m_attn_norm_w'], 'm_w_in': out['m_w_in'], 'm_b_gate': out['m_b_gate'], 'm_q_norm_w': out['m_q_norm_w'], 'm_k_norm_w': out['m_k_norm_w'], 'm_w_attn_o': out['m_w_attn_o'], 'm_conv_dw_w': out['m_conv_dw_w'], 'm_conv_dw_b': out['m_conv_dw_b'], 'm_conv_ln_w': out['m_conv_ln_w'], 'm_conv_ln_b': out['m_conv_ln_b'], 'm_w_conv_o': out['m_w_conv_o'], 'm_mem_norm_w': out['m_mem_norm_w'], 'm_w_mem_kv': out['m_w_mem_kv'], 'm_xq_norm_w': out['m_xq_norm_w'], 'm_xk_norm_w': out['m_xk_norm_w'], 'm_w_cross_o': out['m_w_cross_o'], 'm_w_out': out['m_w_out'], 'm_ffn_norm_w': out['m_ffn_norm_w'], 'm_w_up': out['m_w_up'], 'm_ffn_conv_w': out['m_ffn_conv_w'], 'm_ffn_conv_b': out['m_ffn_conv_b'], 'm_w_down': out['m_w_down'], 'v_rel_bias_table': out['v_rel_bias_table'], 'v_attn_norm_w': out['v_attn_norm_w'], 'v_w_in': out['v_w_in'], 'v_b_gate': out['v_b_gate'], 'v_q_norm_w': out['v_q_norm_w'], 'v_k_norm_w': out['v_k_norm_w'], 'v_w_attn_o': out['v_w_attn_o'], 'v_conv_dw_w': out['v_conv_dw_w'], 'v_conv_dw_b': out['v_conv_dw_b'], 'v_conv_ln_w': out['v_conv_ln_w'], 'v_conv_ln_b': out['v_conv_ln_b'], 'v_w_conv_o': out['v_w_conv_o'], 'v_mem_norm_w': out['v_mem_norm_w'], 'v_w_mem_kv': out['v_w_mem_kv'], 'v_xq_norm_w': out['v_xq_norm_w'], 'v_xk_norm_w': out['v_xk_norm_w'], 'v_w_cross_o': out['v_w_cross_o'], 'v_w_out': out['v_w_out'], 'v_ffn_norm_w': out['v_ffn_norm_w'], 'v_w_up': out['v_w_up'], 'v_ffn_conv_w': out['v_ffn_conv_w'], 'v_ffn_conv_b': out['v_ffn_conv_b'], 'v_w_down': out['v_w_down']}


def _loss(weights, diff, rest, loss_target):
    with _jax.named_scope("forward"):
        args = {**rest, TWIN_DIFF_INPUT: diff, **{k: w.astype(_WEIGHT_DTYPES[k]) for k, w in weights.items()}}
        y = _forward(args)
    with _jax.named_scope("loss_head"):
        err = _jnp.square(y.astype(_jnp.float32) - loss_target)
        return 0.5 * _jnp.sum(_jnp.mean(err, axis=-1)) if err.ndim else 0.5 * err


def _adamw(w, g, m, v):
    m = ADAM_B1 * m + (1.0 - ADAM_B1) * g
    v = ADAM_B2 * v + (1.0 - ADAM_B2) * _jnp.square(g)
    m_hat = m / (1.0 - ADAM_B1 ** ADAM_STEP)
    v_hat = v / (1.0 - ADAM_B2 ** ADAM_STEP)
    delta = -ADAM_LR * (m_hat / (_jnp.sqrt(v_hat) + ADAM_EPS) + ADAM_WD * w)
    return delta, m, v


def reference(x, mem, rel_bias_table, attn_norm_w, w_in, b_gate, q_norm_w, k_norm_w, w_attn_o, conv_dw_w, conv_dw_b, conv_ln_w, conv_ln_b, w_conv_o, mem_norm_w, w_mem_kv, xq_norm_w, xk_norm_w, w_cross_o, w_out, ffn_norm_w, w_up, ffn_conv_w, ffn_conv_b, w_down, loss_target, m_rel_bias_table, m_attn_norm_w, m_w_in, m_b_gate, m_q_norm_w, m_k_norm_w, m_w_attn_o, m_conv_dw_w, m_conv_dw_b, m_conv_ln_w, m_conv_ln_b, m_w_conv_o, m_mem_norm_w, m_w_mem_kv, m_xq_norm_w, m_xk_norm_w, m_w_cross_o, m_w_out, m_ffn_norm_w, m_w_up, m_ffn_conv_w, m_ffn_conv_b, m_w_down, v_rel_bias_table, v_attn_norm_w, v_w_in, v_b_gate, v_q_norm_w, v_k_norm_w, v_w_attn_o, v_conv_dw_w, v_conv_dw_b, v_conv_ln_w, v_conv_ln_b, v_w_conv_o, v_mem_norm_w, v_w_mem_kv, v_xq_norm_w, v_xk_norm_w, v_w_cross_o, v_w_out, v_ffn_norm_w, v_w_up, v_ffn_conv_w, v_ffn_conv_b, v_w_down):
    given = dict(x=x, mem=mem, rel_bias_table=rel_bias_table, attn_norm_w=attn_norm_w, w_in=w_in, b_gate=b_gate, q_norm_w=q_norm_w, k_norm_w=k_norm_w, w_attn_o=w_attn_o, conv_dw_w=conv_dw_w, conv_dw_b=conv_dw_b, conv_ln_w=conv_ln_w, conv_ln_b=conv_ln_b, w_conv_o=w_conv_o, mem_norm_w=mem_norm_w, w_mem_kv=w_mem_kv, xq_norm_w=xq_norm_w, xk_norm_w=xk_norm_w, w_cross_o=w_cross_o, w_out=w_out, ffn_norm_w=ffn_norm_w, w_up=w_up, ffn_conv_w=ffn_conv_w, ffn_conv_b=ffn_conv_b, w_down=w_down, loss_target=loss_target, m_rel_bias_table=m_rel_bias_table, m_attn_norm_w=m_attn_norm_w, m_w_in=m_w_in, m_b_gate=m_b_gate, m_q_norm_w=m_q_norm_w, m_k_norm_w=m_k_norm_w, m_w_attn_o=m_w_attn_o, m_conv_dw_w=m_conv_dw_w, m_conv_dw_b=m_conv_dw_b, m_conv_ln_w=m_conv_ln_w, m_conv_ln_b=m_conv_ln_b, m_w_conv_o=m_w_conv_o, m_mem_norm_w=m_mem_norm_w, m_w_mem_kv=m_w_mem_kv, m_xq_norm_w=m_xq_norm_w, m_xk_norm_w=m_xk_norm_w, m_w_cross_o=m_w_cross_o, m_w_out=m_w_out, m_ffn_norm_w=m_ffn_norm_w, m_w_up=m_w_up, m_ffn_conv_w=m_ffn_conv_w, m_ffn_conv_b=m_ffn_conv_b, m_w_down=m_w_down, v_rel_bias_table=v_rel_bias_table, v_attn_norm_w=v_attn_norm_w, v_w_in=v_w_in, v_b_gate=v_b_gate, v_q_norm_w=v_q_norm_w, v_k_norm_w=v_k_norm_w, v_w_attn_o=v_w_attn_o, v_conv_dw_w=v_conv_dw_w, v_conv_dw_b=v_conv_dw_b, v_conv_ln_w=v_conv_ln_w, v_conv_ln_b=v_conv_ln_b, v_w_conv_o=v_w_conv_o, v_mem_norm_w=v_mem_norm_w, v_w_mem_kv=v_w_mem_kv, v_xq_norm_w=v_xq_norm_w, v_xk_norm_w=v_xk_norm_w, v_w_cross_o=v_w_cross_o, v_w_out=v_w_out, v_ffn_norm_w=v_ffn_norm_w, v_w_up=v_w_up, v_ffn_conv_w=v_ffn_conv_w, v_ffn_conv_b=v_ffn_conv_b, v_w_down=v_w_down)
    weights = {n: given[n] for n in TWIN_WEIGHTS}
    shared = {n: given[n] for n in SHARED_INPUTS}
    per_example = {n: given[n] for n in ['x', 'mem']}
    grad_fn = _jax.value_and_grad(_loss, argnums=(0, 1))

    def one_microbatch(ex, loss_target):
        ex = dict(ex)
        diff = ex.pop(TWIN_DIFF_INPUT)
        return grad_fn(weights, diff, {**shared, **ex}, loss_target)

    if N_MICROBATCH == 1:
        loss, (grad_w, grad_x) = one_microbatch(per_example, given["loss_target"])
    else:
        def body(carry, xs):
            loss_sum, grad_sum = carry
            l_k, (gw_k, gx_k) = one_microbatch(xs[0], xs[1])
            with _jax.named_scope("update"):
                return (loss_sum + l_k, _jax.tree.map(_jnp.add, grad_sum, gw_k)), gx_k

        init = (_jnp.zeros((), _jnp.float32), _jax.tree.map(_jnp.zeros_like, weights))
        (loss, grad_w), grad_x = _jax.lax.scan(body, init, (per_example, given["loss_target"]))
    with _jax.named_scope("update"):
        delta_w, new_m, new_v = {}, {}, {}
        for n in TWIN_WEIGHTS:
            delta_w[n], new_m[n], new_v[n] = _adamw(weights[n], grad_w[n], given["m_" + n], given["v_" + n])
    return (loss, grad_x, *[grad_w[n] for n in TWIN_WEIGHTS], *[delta_w[n] for n in TWIN_WEIGHTS],
            *[new_m[n] for n in TWIN_WEIGHTS], *[new_v[n] for n in TWIN_WEIGHTS])
```

```python
import functools
import math

import numpy as np
import jax
import jax.numpy as jnp
from jax import lax
from jax.experimental import pallas as pl
from jax.experimental.pallas import tpu as pltpu

F32, BF16 = jnp.float32, jnp.bfloat16
SDS = jax.ShapeDtypeStruct
MESH = pl.DeviceIdType.MESH

HEAD = 128
N_GROUPS, HPG = 3, 4
ATTN_GROUPS = ((128, 1), (512, 4), (2048, 16))
NQ = 128
QKV_W = N_GROUPS * HPG * HEAD
CH = 512
CONV_K, FFN_K = 31, 3
N_BUCKETS, MAX_DIST = 32, 2048
RMS_EPS, LN_EPS = 1e-6, 1e-5
O_Q, O_K, O_V, O_CV, O_CG, O_XQ, O_G = 0, QKV_W, 2 * QKV_W, 3 * QKV_W, 3 * QKV_W + CH, 3 * QKV_W + 2 * CH, 3 * QKV_W + 3 * CH
ADAM_LR, ADAM_B1, ADAM_B2, ADAM_EPS, ADAM_WD, ADAM_STEP = 0.001, 0.9, 0.999, 1e-08, 0.01, 10
NEG = -1e30
SCALE = HEAD ** -0.5
TM = 512
ATT_RB = 2048
NT_DIMS = (((1,), (1,)), ((), ()))
TN_DIMS = (((0,), (0,)), ((), ()))


def _sig(v):
    return 1.0 / (1.0 + jnp.exp(-v))


def _row_tile(rows, cap, mult=8):
    best = None
    for t in range(mult, min(rows, cap) + 1, mult):
        if rows % t == 0:
            best = t
    return best if best is not None else rows


def _full(shape):
    n = len(shape)
    return pl.BlockSpec(shape, lambda *a: (0,) * n)


def _rmsnorm(x, w, name):
    S, D = x.shape

    def body(x_ref, w_ref, o_ref):
        xv = x_ref[...]
        r = lax.rsqrt(jnp.mean(xv * xv, axis=-1, keepdims=True) + RMS_EPS)
        o_ref[...] = (xv * r * w_ref[...]).astype(BF16)

    return pl.pallas_call(
        body, name=name, grid=(S // TM,),
        in_specs=[pl.BlockSpec((TM, D), lambda i: (i, 0)), _full((1, D))],
        out_specs=pl.BlockSpec((TM, D), lambda i: (i, 0)),
        out_shape=SDS((S, D), BF16))(x, w)


def _mm_nn(a, b, out_dtype, name):
    M, K = a.shape
    G, _, n = b.shape

    def body(a_ref, b_ref, o_ref):
        o_ref[...] = jnp.dot(a_ref[...].astype(BF16), b_ref[...], preferred_element_type=F32).astype(out_dtype)

    return pl.pallas_call(
        body, name=name, grid=(G, M // TM),
        in_specs=[pl.BlockSpec((TM, K), lambda g, i: (i, 0)), pl.BlockSpec((None, K, n), lambda g, i: (g, 0, 0))],
        out_specs=pl.BlockSpec((TM, n), lambda g, i: (i, g)),
        out_shape=SDS((M, G * n), out_dtype))(a, b)


def _mm_tn(a, b, G, name):
    S, Ka = a.shape
    n = b.shape[1] // G
    tka = Ka
    while tka * n * 4 > 6 * 2 ** 20 and tka % 256 == 0:
        tka //= 2

    def body(a_ref, b_ref, o_ref):
        @pl.when(pl.program_id(2) == 0)
        def _():
            o_ref[...] = jnp.zeros_like(o_ref)
        o_ref[...] += lax.dot_general(a_ref[...].astype(BF16), b_ref[...].astype(BF16), TN_DIMS, preferred_element_type=F32)

    return pl.pallas_call(
        body, name=name, grid=(G, Ka // tka, S // TM),
        in_specs=[pl.BlockSpec((TM, tka), lambda g, i, k: (k, i)), pl.BlockSpec((TM, n), lambda g, i, k: (k, g))],
        out_specs=pl.BlockSpec((None, tka, n), lambda g, i, k: (g, i, 0)),
        out_shape=SDS((G, Ka, n), F32))(a, b)


def _norm_in_bwd(a, w, xin, nw, resid, name):
    S, K = xin.shape
    G, _, n = w.shape

    def body(a_ref, w_ref, x_ref, nw_ref, r_ref, o_ref, dnw_ref, acc):
        i, g = pl.program_id(0), pl.program_id(1)

        @pl.when((i == 0) & (g == 0))
        def _():
            dnw_ref[...] = jnp.zeros_like(dnw_ref)

        @pl.when(g == 0)
        def _():
            acc[...] = jnp.zeros_like(acc)

        acc[...] += lax.dot_general(a_ref[...], w_ref[...], NT_DIMS, preferred_element_type=F32)

        @pl.when(g == G - 1)
        def _():
            dn = acc[...]
            xv = x_ref[...]
            r = lax.rsqrt(jnp.mean(xv * xv, axis=-1, keepdims=True) + RMS_EPS)
            xhat = xv * r
            dyw = dn * nw_ref[...]
            o_ref[...] = r_ref[...] + r * (dyw - xhat * jnp.mean(dyw * xhat, axis=-1, keepdims=True))
            dnw_ref[...] += jnp.sum(dn * xhat, axis=0, keepdims=True)

    return pl.pallas_call(
        body, name=name, grid=(S // TM, G),
        in_specs=[pl.BlockSpec((TM, n), lambda i, g: (i, g)), pl.BlockSpec((None, K, n), lambda i, g: (g, 0, 0)),
                  pl.BlockSpec((TM, K), lambda i, g: (i, 0)), _full((1, K)), pl.BlockSpec((TM, K), lambda i, g: (i, 0))],
        out_specs=[pl.BlockSpec((TM, K), lambda i, g: (i, 0)), _full((1, K))],
        out_shape=[SDS((S, K), F32), SDS((1, K), F32)],
        scratch_shapes=[pltpu.VMEM((TM, K), F32)])(a, w, xin, nw, resid)


def _t5_bucket_np(dist):
    max_exact = N_BUCKETS // 2
    d = np.maximum(dist.astype(np.float32), np.float32(1.0))
    large = max_exact + (np.log(d / np.float32(max_exact)) / np.float32(math.log(MAX_DIST / max_exact))
                         * np.float32(N_BUCKETS - max_exact)).astype(np.int32)
    large = np.minimum(large, N_BUCKETS - 1)
    return np.where(dist < max_exact, dist, large).astype(np.int32)


def _bias_static():
    qi = np.arange(NQ)[:, None]
    kj = np.arange(2 * NQ)[None, :]
    step = qi + NQ - kj
    band = (step >= 0) & (step <= NQ)
    buckets = np.stack([_t5_bucket_np(np.clip(step, 0, None) * dil).reshape(1, -1) for _, dil in ATTN_GROUPS])
    return band, buckets


def _bias_fwd(table_t, buckets):
    nb = buckets.shape[-1]

    def body(t_ref, b_ref, o_ref):
        oh = (b_ref[...] == lax.broadcasted_iota(jnp.int32, (N_BUCKETS, nb), 0)).astype(F32)
        o_ref[...] = jnp.dot(t_ref[...], oh, preferred_element_type=F32, precision=lax.Precision.HIGHEST)

    return pl.pallas_call(
        body, name="bias_fwd", grid=(N_GROUPS,),
        in_specs=[pl.BlockSpec((None, 8, N_BUCKETS), lambda g: (g, 0, 0)), pl.BlockSpec((None, 1, nb), lambda g: (g, 0, 0))],
        out_specs=pl.BlockSpec((None, 8, nb), lambda g: (g, 0, 0)),
        out_shape=SDS((N_GROUPS, 8, nb), F32))(table_t, buckets)


def _bias_bwd(dsb, buckets):
    nb = buckets.shape[-1]

    def body(d_ref, b_ref, o_ref):
        oh = (b_ref[...] == lax.broadcasted_iota(jnp.int32, (N_BUCKETS, nb), 0)).astype(F32)
        o_ref[...] = lax.dot_general(d_ref[...], oh, NT_DIMS, preferred_element_type=F32, precision=lax.Precision.HIGHEST)

    return pl.pallas_call(
        body, name="bias_bwd", grid=(N_GROUPS,),
        in_specs=[pl.BlockSpec((None, 8, nb), lambda g: (g, 0, 0)), pl.BlockSpec((None, 1, nb), lambda g: (g, 0, 0))],
        out_specs=pl.BlockSpec((None, 8, N_BUCKETS), lambda g: (g, 0, 0)),
        out_shape=SDS((N_GROUPS, 8, N_BUCKETS), F32))(dsb, buckets)


def _qkv_prep(z, qw, kw):
    S = z.shape[0]
    nh = N_GROUPS * HPG

    def body(zq, zk, zv, qw_ref, kw_ref, qh, kh, vh):
        for h in range(nh):
            g = h // HPG
            sl = slice(h * HEAD, (h + 1) * HEAD)
            xq = zq[:, sl].astype(F32)
            qh[h] = xq * lax.rsqrt(jnp.mean(xq * xq, axis=-1, keepdims=True) + RMS_EPS) * qw_ref[g:g + 1, :]
            xk = zk[:, sl].astype(F32)
            kh[h] = xk * lax.rsqrt(jnp.mean(xk * xk, axis=-1, keepdims=True) + RMS_EPS) * kw_ref[g:g + 1, :]
            vh[h] = zv[:, sl].astype(F32)

    hm = pl.BlockSpec((nh, TM, HEAD), lambda i: (0, i, 0))
    return pl.pallas_call(
        body, name="qkv_prep", grid=(S // TM,),
        in_specs=[pl.BlockSpec((TM, QKV_W), lambda i: (i, 0)), pl.BlockSpec((TM, QKV_W), lambda i: (i, 1)),
                  pl.BlockSpec((TM, QKV_W), lambda i: (i, 2)), _full((N_GROUPS, HEAD)), _full((N_GROUPS, HEAD))],
        out_specs=[hm, hm, hm],
        out_shape=[SDS((nh, S, HEAD), F32)] * 3)(z, z, z, qw, kw)


def _rows(start, d):
    return pl.ds(start, NQ) if d == 1 else pl.ds(start, NQ, stride=d)


def _attn_fwd(qh, kh, vh, biasm, g, d):
    S = qh.shape[1]
    RB = ATT_RB
    nbk, nq = S // RB, RB // (NQ * d)

    def body(q_ref, k_ref, v_ref, bias_ref, o_ref, lse_ref, kbuf, vbuf):
        b = pl.program_id(1)

        @pl.when(b == 0)
        def _():
            kbuf[0:RB, :] = jnp.zeros((RB, HEAD), F32)
            vbuf[0:RB, :] = jnp.zeros((RB, HEAD), F32)

        @pl.when(b > 0)
        def _():
            kbuf[0:RB, :] = kbuf[RB:2 * RB, :]
            vbuf[0:RB, :] = vbuf[RB:2 * RB, :]

        kbuf[RB:2 * RB, :] = k_ref[...]
        vbuf[RB:2 * RB, :] = v_ref[...]
        bias = bias_ref[...]
        col = lax.broadcasted_iota(jnp.int32, (NQ, 2 * NQ), 1)

        for qb in range(nq):
            def unit(r, carry, qb=qb):
                qs = qb * NQ * d + r
                q = q_ref[_rows(qs, d), :].astype(BF16)
                kw = jnp.concatenate([kbuf[_rows(RB + qs - NQ * d, d), :], kbuf[_rows(RB + qs, d), :]], axis=0).astype(BF16)
                vw = jnp.concatenate([vbuf[_rows(RB + qs - NQ * d, d), :], vbuf[_rows(RB + qs, d), :]], axis=0).astype(BF16)
                s = lax.dot_general(q, kw, NT_DIMS, preferred_element_type=F32) * SCALE + bias
                if qb == 0:
                    s = jnp.where((col < NQ) & (b == 0), NEG, s)
                m = jnp.max(s, axis=-1, keepdims=True)
                p = jnp.exp(s - m)
                l = jnp.sum(p, axis=-1, keepdims=True)
                o = jnp.dot(p.astype(BF16), vw, preferred_element_type=F32) / l
                o_ref[_rows(qs, d), :] = o
                lse_ref[_rows(qs, d), :] = jnp.broadcast_to(m + jnp.log(l), (NQ, HEAD))
                return carry

            if d == 1:
                unit(0, 0)
            else:
                lax.fori_loop(0, d, unit, 0)

    blk = lambda f: pl.BlockSpec((None, RB, HEAD), f)
    return pl.pallas_call(
        body, name=f"attn_fwd_g{g}", grid=(HPG, nbk),
        in_specs=[blk(lambda h, b: (HPG * g + h, b, 0))] * 3 + [pl.BlockSpec((None, NQ, 2 * NQ), lambda h, b: (h, 0, 0))],
        out_specs=[blk(lambda h, b: (h, b, 0))] * 2,
        out_shape=[SDS((HPG, S, HEAD), F32)] * 2,
        scratch_shapes=[pltpu.VMEM((2 * RB, HEAD), F32)] * 2)(qh, kh, vh, biasm)


def _attn_bwd(qh, kh, vh, biasm, da, wg, dh, lse, g, d):
    S = qh.shape[1]
    RB = ATT_RB
    nbk, nq = S // RB, RB // (NQ * d)

    def body(q_ref, k_ref, v_ref, bias_ref, da_ref, wg_ref, dh_ref, lse_ref,
             dq_ref, dk_ref, dv_ref, dsb_ref, kbuf, vbuf, dkbuf, dvbuf):
        b = pl.program_id(1)
        zero = jnp.zeros((RB, HEAD), F32)

        @pl.when(b == 0)
        def _():
            kbuf[0:RB, :] = zero
            vbuf[0:RB, :] = zero
            dkbuf[0:RB, :] = zero
            dvbuf[0:RB, :] = zero
            dsb_ref[...] = jnp.zeros_like(dsb_ref)

        @pl.when(b > 0)
        def _():
            kbuf[0:RB, :] = kbuf[RB:2 * RB, :]
            vbuf[0:RB, :] = vbuf[RB:2 * RB, :]
            dkbuf[0:RB, :] = dkbuf[RB:2 * RB, :]
            dvbuf[0:RB, :] = dvbuf[RB:2 * RB, :]

        dkbuf[RB:2 * RB, :] = zero
        dvbuf[RB:2 * RB, :] = zero

        @pl.when(b < nbk)
        def _():
            kbuf[RB:2 * RB, :] = k_ref[...]
            vbuf[RB:2 * RB, :] = v_ref[...]
            bias = bias_ref[...]
            col = lax.broadcasted_iota(jnp.int32, (NQ, 2 * NQ), 1)

            for qb in range(nq):
                def unit(r, carry, qb=qb):
                    qs = qb * NQ * d + r
                    prev, cur = _rows(RB + qs - NQ * d, d), _rows(RB + qs, d)
                    q = q_ref[_rows(qs, d), :].astype(BF16)
                    kw = jnp.concatenate([kbuf[prev, :], kbuf[cur, :]], axis=0).astype(BF16)
                    vw = jnp.concatenate([vbuf[prev, :], vbuf[cur, :]], axis=0).astype(BF16)
                    s = lax.dot_general(q, kw, NT_DIMS, preferred_element_type=F32) * SCALE + bias
                    if qb == 0:
                        s = jnp.where((col < NQ) & (b == 0), NEG, s)
                    p = jnp.exp(s - lse_ref[_rows(qs, d), :][:, 0:1])
                    w = wg_ref[_rows(qs, d), :]
                    do = (da_ref[_rows(qs, d), :] * w).astype(BF16)
                    dp = lax.dot_general(do, vw, NT_DIMS, preferred_element_type=F32)
                    ds = p * (dp - w[:, 0:1] * dh_ref[_rows(qs, d), :][:, 0:1])
                    dsb_ref[...] += ds
                    dsb = ds.astype(BF16)
                    dq_ref[_rows(qs, d), :] = jnp.dot(dsb, kw, preferred_element_type=F32) * SCALE
                    dkw = lax.dot_general(dsb, q, TN_DIMS, preferred_element_type=F32) * SCALE
                    dvw = lax.dot_general(p.astype(BF16), do, TN_DIMS, preferred_element_type=F32)
                    dkbuf[prev, :] += dkw[0:NQ, :]
                    dkbuf[cur, :] += dkw[NQ:2 * NQ, :]
                    dvbuf[prev, :] += dvw[0:NQ, :]
                    dvbuf[cur, :] += dvw[NQ:2 * NQ, :]
                    return carry

                if d == 1:
                    unit(0, 0)
                else:
                    lax.fori_loop(0, d, unit, 0)

        dk_ref[...] = dkbuf[0:RB, :]
        dv_ref[...] = dvbuf[0:RB, :]

    blk = lambda f: pl.BlockSpec((None, RB, HEAD), f)
    cur_g = blk(lambda h, b: (HPG * g + h, jnp.minimum(b, nbk - 1), 0))
    cur = blk(lambda h, b: (h, jnp.minimum(b, nbk - 1), 0))
    prv = blk(lambda h, b: (h, jnp.maximum(b - 1, 0), 0))
    sq = pl.BlockSpec((None, NQ, 2 * NQ), lambda h, b: (h, 0, 0))
    return pl.pallas_call(
        body, name=f"attn_bwd_g{g}", grid=(HPG, nbk + 1),
        in_specs=[cur_g, cur_g, cur_g, sq, cur, cur, cur, cur],
        out_specs=[cur, prv, prv, sq],
        out_shape=[SDS((HPG, S, HEAD), F32)] * 3 + [SDS((HPG, NQ, 2 * NQ), F32)],
        scratch_shapes=[pltpu.VMEM((2 * RB, HEAD), F32)] * 4)(qh, kh, vh, biasm, da, wg, dh, lse)


def _merge_weights(l0, l1, l2):
    m = jnp.maximum(jnp.maximum(l0, l1), l2)
    e0, e1, e2 = jnp.exp(l0 - m), jnp.exp(l1 - m), jnp.exp(l2 - m)
    inv = 1.0 / (e0 + e1 + e2)
    return e0 * inv, e1 * inv, e2 * inv


def _merge_fwd(os_, lses):
    S = os_[0].shape[1]

    def body(o0, o1, o2, l0, l1, l2, a_ref):
        for h in range(HPG):
            w0, w1, w2 = _merge_weights(l0[h], l1[h], l2[h])
            a_ref[:, h * HEAD:(h + 1) * HEAD] = (w0 * o0[h] + w1 * o1[h] + w2 * o2[h]).astype(BF16)

    hm = pl.BlockSpec((HPG, TM, HEAD), lambda i: (0, i, 0))
    return pl.pallas_call(
        body, name="merge_fwd", grid=(S // TM,), in_specs=[hm] * 6,
        out_specs=pl.BlockSpec((TM, CH), lambda i: (i, 0)),
        out_shape=SDS((S, CH), BF16))(*os_, *lses)


def _merge_bwd(dattn, os_, lses):
    S = dattn.shape[0]

    def body(da_ref, o0, o1, o2, l0, l1, l2, wg_ref, dah_ref, dh_ref):
        for h in range(HPG):
            w = _merge_weights(l0[h], l1[h], l2[h])
            attn = w[0] * o0[h] + w[1] * o1[h] + w[2] * o2[h]
            da = da_ref[:, h * HEAD:(h + 1) * HEAD]
            for gi in range(N_GROUPS):
                wg_ref[gi, h] = w[gi]
            dah_ref[h] = da
            dh_ref[h] = jnp.broadcast_to(jnp.sum(da * attn, axis=-1, keepdims=True), (TM, HEAD))

    hm = pl.BlockSpec((HPG, TM, HEAD), lambda i: (0, i, 0))
    return pl.pallas_call(
        body, name="merge_bwd", grid=(S // TM,),
        in_specs=[pl.BlockSpec((TM, CH), lambda i: (i, 0))] + [hm] * 6,
        out_specs=[pl.BlockSpec((N_GROUPS, HPG, TM, HEAD), lambda i: (0, 0, i, 0)), hm, hm],
        out_shape=[SDS((N_GROUPS, HPG, S, HEAD), F32), SDS((HPG, S, HEAD), F32), SDS((HPG, S, HEAD), F32)])(dattn, *os_, *lses)


def _qkv_bwd(dz, z, dqs, dks, dvs, qw, kw):
    S = z.shape[0]
    nh = N_GROUPS * HPG

    def body(dz_in, zq, zk, *refs):
        del dz_in
        dq_refs, dk_refs, dv_refs = refs[0:3], refs[3:6], refs[6:9]
        qw_ref, kw_ref, dz_ref, dqw_ref, dkw_ref = refs[9:]

        @pl.when(pl.program_id(0) == 0)
        def _():
            dqw_ref[...] = jnp.zeros_like(dqw_ref)
            dkw_ref[...] = jnp.zeros_like(dkw_ref)

        def nbwd(xr, dy, wr, dwr, h, off):
            g = h // HPG
            x = xr[:, h * HEAD:(h + 1) * HEAD].astype(F32)
            r = lax.rsqrt(jnp.mean(x * x, axis=-1, keepdims=True) + RMS_EPS)
            xhat = x * r
            dyw = dy * wr[g:g + 1, :]
            dz_ref[:, off + h * HEAD:off + (h + 1) * HEAD] = (
                r * (dyw - xhat * jnp.mean(dyw * xhat, axis=-1, keepdims=True))).astype(BF16)
            dwr[g:g + 1, :] += jnp.sum(dy * xhat, axis=0, keepdims=True)

        for h in range(nh):
            g, hh = h // HPG, h % HPG
            nbwd(zq, dq_refs[g][hh], qw_ref, dqw_ref, h, O_Q)
            nbwd(zk, dk_refs[g][hh], kw_ref, dkw_ref, h, O_K)
            dz_ref[:, O_V + h * HEAD:O_V + (h + 1) * HEAD] = dv_refs[g][hh].astype(BF16)

    hm = pl.BlockSpec((HPG, TM, HEAD), lambda i: (0, i, 0))
    return pl.pallas_call(
        body, name="qkv_bwd", grid=(S // TM,),
        in_specs=[pl.BlockSpec(memory_space=pl.ANY), pl.BlockSpec((TM, QKV_W), lambda i: (i, 0)),
                  pl.BlockSpec((TM, QKV_W), lambda i: (i, 1))] + [hm] * 9 + [_full((N_GROUPS, HEAD)), _full((N_GROUPS, HEAD))],
        out_specs=[pl.BlockSpec((TM, 3 * QKV_W), lambda i: (i, 0)), _full((N_GROUPS, HEAD)), _full((N_GROUPS, HEAD))],
        out_shape=[SDS(dz.shape, BF16), SDS((N_GROUPS, HEAD), F32), SDS((N_GROUPS, HEAD), F32)],
        input_output_aliases={0: 0})(dz, z, z, *dqs, *dks, *dvs, qw, kw)


def _conv_fwd(z, cw, cb, lnw, lnb):
    S = z.shape[0]
    H = 32

    def body(zv, zg, cw_ref, cb_ref, lnw_ref, lnb_ref, u1_ref, u3_ref, xbuf):
        i = pl.program_id(0)

        @pl.when(i == 0)
        def _():
            xbuf[0:H, :] = jnp.zeros((H, CH), F32)

        @pl.when(i > 0)
        def _():
            xbuf[0:H, :] = xbuf[TM:TM + H, :]

        xbuf[H:H + TM, :] = zv[...].astype(F32) * _sig(zg[...].astype(F32))
        acc = jnp.broadcast_to(cb_ref[...], (TM, CH))
        for k in range(CONV_K):
            acc = acc + xbuf[pl.ds(H - (CONV_K - 1) + k, TM), :] * cw_ref[k:k + 1, :]
        u1_ref[...] = acc
        mu = jnp.mean(acc, axis=-1, keepdims=True)
        xc = acc - mu
        yl = xc * lax.rsqrt(jnp.mean(xc * xc, axis=-1, keepdims=True) + LN_EPS) * lnw_ref[...] + lnb_ref[...]
        u3_ref[...] = (yl * _sig(yl)).astype(BF16)

    row = pl.BlockSpec((TM, CH), lambda i: (i, 0))
    return pl.pallas_call(
        body, name="conv_fwd", grid=(S // TM,),
        in_specs=[pl.BlockSpec((TM, CH), lambda i: (i, O_CV // CH)), pl.BlockSpec((TM, CH), lambda i: (i, O_CG // CH)),
                  _full((CONV_K, CH)), _full((1, CH)), _full((1, CH)), _full((1, CH))],
        out_specs=[row, row], out_shape=[SDS((S, CH), F32), SDS((S, CH), BF16)],
        scratch_shapes=[pltpu.VMEM((TM + H, CH), F32)])(z, z, cw, cb, lnw, lnb)


def _conv_bwd_a(du3, u1, z, lnw, lnb):
    S = z.shape[0]
    H = 32

    def body(du3_ref, u1_ref, zv, zg, lnw_ref, lnb_ref, du1_ref, acc_ref, xbuf):
        i = pl.program_id(0)

        @pl.when(i == 0)
        def _():
            xbuf[0:H, :] = jnp.zeros((H, CH), F32)
            acc_ref[...] = jnp.zeros_like(acc_ref)

        @pl.when(i > 0)
        def _():
            xbuf[0:H, :] = xbuf[TM:TM + H, :]

        xbuf[H:H + TM, :] = zv[...].astype(F32) * _sig(zg[...].astype(F32))
        u1 = u1_ref[...]
        mu = jnp.mean(u1, axis=-1, keepdims=True)
        xc = u1 - mu
        r = lax.rsqrt(jnp.mean(xc * xc, axis=-1, keepdims=True) + LN_EPS)
        yhat = xc * r
        yl = yhat * lnw_ref[...] + lnb_ref[...]
        sg = _sig(yl)
        dyl = du3_ref[...] * (sg * (1.0 + yl * (1.0 - sg)))
        dyh = dyl * lnw_ref[...]
        du1 = r * (dyh - jnp.mean(dyh, axis=-1, keepdims=True) - yhat * jnp.mean(dyh * yhat, axis=-1, keepdims=True))
        du1_ref[...] = du1
        acc_ref[33:34, :] += jnp.sum(dyl * yhat, axis=0, keepdims=True)
        acc_ref[34:35, :] += jnp.sum(dyl, axis=0, keepdims=True)
        acc_ref[32:33, :] += jnp.sum(du1, axis=0, keepdims=True)
        for k in range(CONV_K):
            acc_ref[k:k + 1, :] += jnp.sum(du1 * xbuf[pl.ds(H - (CONV_K - 1) + k, TM), :], axis=0, keepdims=True)

    row = pl.BlockSpec((TM, CH), lambda i: (i, 0))
    return pl.pallas_call(
        body, name="conv_bwd_a", grid=(S // TM,),
        in_specs=[row, row, pl.BlockSpec((TM, CH), lambda i: (i, O_CV // CH)), pl.BlockSpec((TM, CH), lambda i: (i, O_CG // CH)),
                  _full((1, CH)), _full((1, CH))],
        out_specs=[row, _full((40, CH))], out_shape=[SDS((S, CH), F32), SDS((40, CH), F32)],
        scratch_shapes=[pltpu.VMEM((TM + H, CH), F32)])(du3, u1, z, z, lnw, lnb)


def _conv_bwd_b(dz, du1, z, cw):
    S = z.shape[0]
    nt = S // TM
    H = 32

    def body(dz_in, du1_ref, zv, zg, cw_ref, dz_ref, ybuf, dval, dgate):
        del dz_in
        i, p = pl.program_id(0), pl.program_id(1)

        @pl.when(p == 0)
        def _():
            @pl.when(i == 0)
            def _():
                ybuf[TM:TM + H, :] = jnp.zeros((H, CH), F32)

            @pl.when(i > 0)
            def _():
                ybuf[TM:TM + H, :] = ybuf[0:H, :]

            ybuf[0:TM, :] = du1_ref[...]
            acc = jnp.zeros((TM, CH), F32)
            for j in range(CONV_K):
                acc = acc + ybuf[pl.ds(j, TM), :] * cw_ref[CONV_K - 1 - j:CONV_K - j, :]
            val = zv[...].astype(F32)
            sg = _sig(zg[...].astype(F32))
            dval[...] = (acc * sg).astype(BF16)
            dgate[...] = (acc * val * sg * (1.0 - sg)).astype(BF16)
            dz_ref[...] = dval[...]

        @pl.when(p == 1)
        def _():
            dz_ref[...] = dgate[...]

    rev = lambda c: pl.BlockSpec((TM, CH), lambda i, p: (nt - 1 - i, c))
    return pl.pallas_call(
        body, name="conv_bwd_b", grid=(nt, 2),
        in_specs=[pl.BlockSpec(memory_space=pl.ANY), rev(0), rev(O_CV // CH), rev(O_CG // CH), _full((CONV_K, CH))],
        out_specs=pl.BlockSpec((TM, CH), lambda i, p: (nt - 1 - i, O_CV // CH + p)),
        out_shape=SDS(dz.shape, BF16),
        scratch_shapes=[pltpu.VMEM((TM + H, CH), F32), pltpu.VMEM((TM, CH), BF16), pltpu.VMEM((TM, CH), BF16)],
        input_output_aliases={0: 0})(dz, du1, z, z, cw)


def _memkv_fwd(mem, mnw, wkv, xkw):
    M, D = mem.shape

    def body(mem_ref, mnw_ref, w_ref, xkw_ref, mn_ref, kv_ref, mk_ref, mv_ref):
        x = mem_ref[...]
        mn = (x * lax.rsqrt(jnp.mean(x * x, axis=-1, keepdims=True) + RMS_EPS) * mnw_ref[...]).astype(BF16)
        mn_ref[...] = mn
        kv = jnp.dot(mn, w_ref[...], preferred_element_type=F32)
        kv_ref[...] = kv
        for h in range(HPG):
            k = kv[:, h * HEAD:(h + 1) * HEAD]
            mk_ref[:, h * HEAD:(h + 1) * HEAD] = (
                k * lax.rsqrt(jnp.mean(k * k, axis=-1, keepdims=True) + RMS_EPS) * xkw_ref[...]).astype(BF16)
        mv_ref[...] = kv[:, CH:2 * CH].astype(BF16)

    return pl.pallas_call(
        body, name="memkv_fwd",
        out_shape=[SDS((M, D), BF16), SDS((M, 2 * CH), F32), SDS((M, CH), BF16), SDS((M, CH), BF16)])(mem, mnw, wkv, xkw)


def _cross_q(zx, xqw, h):
    x = zx[:, h * HEAD:(h + 1) * HEAD].astype(F32)
    r = lax.rsqrt(jnp.mean(x * x, axis=-1, keepdims=True) + RMS_EPS)
    xhat = x * r
    return xhat, r, xhat * xqw


def _cross_fwd(z, xqw, mk, mv):
    S = z.shape[0]
    M = mk.shape[0]

    def body(zx, xqw_ref, mk_ref, mv_ref, o_ref):
        for h in range(HPG):
            sl = slice(h * HEAD, (h + 1) * HEAD)
            _, _, q = _cross_q(zx, xqw_ref[...], h)
            s = lax.dot_general(q.astype(BF16), mk_ref[:, sl], NT_DIMS, preferred_element_type=F32) * SCALE
            e = jnp.exp(s - jnp.max(s, axis=-1, keepdims=True))
            p = e / jnp.sum(e, axis=-1, keepdims=True)
            o_ref[:, sl] = jnp.dot(p.astype(BF16), mv_ref[:, sl], preferred_element_type=F32).astype(BF16)

    return pl.pallas_call(
        body, name="cross_fwd", grid=(S // TM,),
        in_specs=[pl.BlockSpec((TM, CH), lambda i: (i, O_XQ // CH)), _full((1, HEAD)), _full((M, CH)), _full((M, CH))],
        out_specs=pl.BlockSpec((TM, CH), lambda i: (i, 0)), out_shape=SDS((S, CH), BF16))(z, xqw, mk, mv)


def _cross_bwd(dz, doc, z, xqw, mk, mv):
    S = z.shape[0]
    M = mk.shape[0]

    def body(dz_in, do_ref, zx, xqw_ref, mk_ref, mv_ref, dz_ref, dmk_ref, dmv_ref, dxw_ref):
        del dz_in

        @pl.when(pl.program_id(0) == 0)
        def _():
            dmk_ref[...] = jnp.zeros_like(dmk_ref)
            dmv_ref[...] = jnp.zeros_like(dmv_ref)
            dxw_ref[...] = jnp.zeros_like(dxw_ref)

        for h in range(HPG):
            sl = slice(h * HEAD, (h + 1) * HEAD)
            xhat, r, q = _cross_q(zx, xqw_ref[...], h)
            qb = q.astype(BF16)
            s = lax.dot_general(qb, mk_ref[:, sl], NT_DIMS, preferred_element_type=F32) * SCALE
            e = jnp.exp(s - jnp.max(s, axis=-1, keepdims=True))
            p = e / jnp.sum(e, axis=-1, keepdims=True)
            do = do_ref[:, sl].astype(BF16)
            dp = lax.dot_general(do, mv_ref[:, sl], NT_DIMS, preferred_element_type=F32)
            ds = (p * (dp - jnp.sum(p * dp, axis=-1, keepdims=True)) * SCALE).astype(BF16)
            dq = jnp.dot(ds, mk_ref[:, sl], preferred_element_type=F32)
            dmk_ref[:, sl] += lax.dot_general(ds, qb, TN_DIMS, preferred_element_type=F32)
            dmv_ref[:, sl] += lax.dot_general(p.astype(BF16), do, TN_DIMS, preferred_element_type=F32)
            dyw = dq * xqw_ref[...]
            dz_ref[:, sl] = (r * (dyw - xhat * jnp.mean(dyw * xhat, axis=-1, keepdims=True))).astype(BF16)
            dxw_ref[...] += jnp.sum(dq * xhat, axis=0, keepdims=True)

    return pl.pallas_call(
        body, name="cross_bwd", grid=(S // TM,),
        in_specs=[pl.BlockSpec(memory_space=pl.ANY), pl.BlockSpec((TM, CH), lambda i: (i, 0)),
                  pl.BlockSpec((TM, CH), lambda i: (i, O_XQ // CH)), _full((1, HEAD)), _full((M, CH)), _full((M, CH))],
        out_specs=[pl.BlockSpec((TM, CH), lambda i: (i, O_XQ // CH)), _full((M, CH)), _full((M, CH)), _full((1, HEAD))],
        out_shape=[SDS(dz.shape, BF16), SDS((M, CH), F32), SDS((M, CH), F32), SDS((1, HEAD), F32)],
        input_output_aliases={0: 0})(dz, doc, z, xqw, mk, mv)


def _memkv_bwd(dmk, dmv, kv, mem, mn, mnw, wkv, xkw):
    M, D = mem.shape

    def body(dmk_ref, dmv_ref, kv_ref, mem_ref, mn_ref, mnw_ref, w_ref, xkw_ref, dw_ref, dxk_ref, dmn_ref, dkv):
        dxk = jnp.zeros((1, HEAD), F32)
        for h in range(HPG):
            sl = slice(h * HEAD, (h + 1) * HEAD)
            k = kv_ref[:, sl]
            r = lax.rsqrt(jnp.mean(k * k, axis=-1, keepdims=True) + RMS_EPS)
            khat = k * r
            dy = dmk_ref[:, sl]
            dyw = dy * xkw_ref[...]
            dkv[:, sl] = (r * (dyw - khat * jnp.mean(dyw * khat, axis=-1, keepdims=True))).astype(BF16)
            dxk = dxk + jnp.sum(dy * khat, axis=0, keepdims=True)
        dxk_ref[...] = dxk
        dkv[:, CH:2 * CH] = dmv_ref[...].astype(BF16)
        dw_ref[...] = lax.dot_general(mn_ref[...], dkv[...], TN_DIMS, preferred_element_type=F32)
        dn = lax.dot_general(dkv[...], w_ref[...], NT_DIMS, preferred_element_type=F32)
        x = mem_ref[...]
        r = lax.rsqrt(jnp.mean(x * x, axis=-1, keepdims=True) + RMS_EPS)
        dmn_ref[...] = jnp.sum(dn * x * r, axis=0, keepdims=True)

    return pl.pallas_call(
        body, name="memkv_bwd",
        out_shape=[SDS((D, 2 * CH), F32), SDS((1, HEAD), F32), SDS((1, D), F32)],
        scratch_shapes=[pltpu.VMEM((M, 2 * CH), BF16)])(dmk, dmv, kv, mem, mn, mnw, wkv, xkw)


def _branch_proj(a_ref, w_ref, y_ref):
    G, _, n = w_ref.shape
    a = a_ref[...]
    for g in range(G):
        y_ref[:, g * n:(g + 1) * n] = jnp.dot(a, w_ref[g], preferred_element_type=F32)


def _gates(zg_ref, bg_ref, k, D):
    return _sig(zg_ref[:, k * D:(k + 1) * D].astype(F32) + bg_ref[:, k * D:(k + 1) * D])


def _outproj_fwd(x, z, bg, attn, u3, oc, wao, wco, wxo, wout, fnw):
    S, D = x.shape
    tm = 256

    def body(x_ref, zg_ref, bg_ref, a_ref, u_ref, c_ref, wa, wc, wx, wo, fnw_ref, h1_ref, hn_ref, ya, yc, yx):
        _branch_proj(a_ref, wa, ya)
        _branch_proj(u_ref, wc, yc)
        _branch_proj(c_ref, wx, yx)
        merged = _gates(zg_ref, bg_ref, 0, D) * ya[...] + _gates(zg_ref, bg_ref, 1, D) * yc[...] + _gates(zg_ref, bg_ref, 2, D) * yx[...]
        h1 = x_ref[...] + jnp.dot(merged.astype(BF16), wo[...], preferred_element_type=F32)
        h1_ref[...] = h1
        hn_ref[...] = (h1 * lax.rsqrt(jnp.mean(h1 * h1, axis=-1, keepdims=True) + RMS_EPS) * fnw_ref[...]).astype(BF16)

    row = lambda w: pl.BlockSpec((tm, w), lambda i: (i, 0))
    return pl.pallas_call(
        body, name="outproj_fwd", grid=(S // tm,),
        in_specs=[row(D), pl.BlockSpec((tm, 3 * D), lambda i: (i, O_G // (3 * D))), _full((1, 3 * D)), row(CH), row(CH), row(CH),
                  _full(wao.shape), _full(wco.shape), _full(wxo.shape), _full((D, D)), _full((1, D))],
        out_specs=[row(D), row(D)], out_shape=[SDS((S, D), F32), SDS((S, D), BF16)],
        scratch_shapes=[pltpu.VMEM((tm, D), F32)] * 3)(x, z, bg, attn, u3, oc, wao, wco, wxo, wout, fnw)


def _outproj_bwd(dh1, z, bg, attn, u3, oc, wao, wco, wxo, wout, n_in):
    S, D = dh1.shape
    tm = 256

    def body(dh_ref, zg_ref, bg_ref, a_ref, u_ref, c_ref, wa, wc, wx, wo,
             dz_ref, da_ref, du_ref, dc_ref, dya_ref, dyc_ref, dyx_ref, mg_ref, dbg_ref, ya, yc, yx):
        @pl.when(pl.program_id(0) == 0)
        def _():
            dbg_ref[...] = jnp.zeros_like(dbg_ref)

        _branch_proj(a_ref, wa, ya)
        _branch_proj(u_ref, wc, yc)
        _branch_proj(c_ref, wx, yx)
        dm = lax.dot_general(dh_ref[...].astype(BF16), wo[...], NT_DIMS, preferred_element_type=F32)
        merged = jnp.zeros((tm, D), F32)
        for k, (y, dy_ref, w_ref, db_ref) in enumerate(((ya, dya_ref, wa, da_ref), (yc, dyc_ref, wc, du_ref), (yx, dyx_ref, wx, dc_ref))):
            gk = _gates(zg_ref, bg_ref, k, D)
            yk = y[...]
            merged = merged + gk * yk
            dzg = dm * yk * gk * (1.0 - gk)
            dz_ref[:, k * D:(k + 1) * D] = dzg.astype(BF16)
            dbg_ref[:, k * D:(k + 1) * D] += jnp.sum(dzg, axis=0, keepdims=True)
            dyk = (dm * gk).astype(BF16)
            dy_ref[...] = dyk
            G, _, n = w_ref.shape
            acc = jnp.zeros((tm, CH), F32)
            for g in range(G):
                acc = acc + lax.dot_general(dyk[:, g * n:(g + 1) * n], w_ref[g], NT_DIMS, preferred_element_type=F32)
            db_ref[...] = acc
        mg_ref[...] = merged.astype(BF16)

    row = lambda w: pl.BlockSpec((tm, w), lambda i: (i, 0))
    return pl.pallas_call(
        body, name="outproj_bwd", grid=(S // tm,),
        in_specs=[row(D), pl.BlockSpec((tm, 3 * D), lambda i: (i, O_G // (3 * D))), _full((1, 3 * D)), row(CH), row(CH), row(CH),
                  _full(wao.shape), _full(wco.shape), _full(wxo.shape), _full((D, D))],
        out_specs=[pl.BlockSpec((tm, 3 * D), lambda i: (i, O_G // (3 * D))), row(CH), row(CH), row(CH),
                   row(D), row(D), row(D), row(D), _full((1, 3 * D))],
        out_shape=[SDS((S, n_in), BF16)] + [SDS((S, CH), F32)] * 3 + [SDS((S, D), BF16)] * 4 + [SDS((1, 3 * D), F32)],
        scratch_shapes=[pltpu.VMEM((tm, D), F32)] * 3)(dh1, z, bg, attn, u3, oc, wao, wco, wxo, wout)


FFN_TC = 256
FFN_H = 8


def _ffn_conv(buf, w_ref, b_ref, tm):
    acc = jnp.broadcast_to(b_ref[...], (tm, FFN_TC))
    for k in range(FFN_K):
        acc = acc + buf[pl.ds(FFN_H - (FFN_K - 1) + k, tm), :] * w_ref[k:k + 1, :]
    return acc


def _ffn_act_fwd(up, cw, cb):
    S, F2 = up.shape
    nj = F2 // 2 // FFN_TC
    tm = 1024 if S % 1024 == 0 else TM

    def body(ua, ug, wa, wg, ba, bgt, o_ref, abuf, gbuf):
        i = pl.program_id(1)

        @pl.when(i == 0)
        def _():
            abuf[0:FFN_H, :] = jnp.zeros((FFN_H, FFN_TC), F32)
            gbuf[0:FFN_H, :] = jnp.zeros((FFN_H, FFN_TC), F32)

        @pl.when(i > 0)
        def _():
            abuf[0:FFN_H, :] = abuf[tm:tm + FFN_H, :]
            gbuf[0:FFN_H, :] = gbuf[tm:tm + FFN_H, :]

        abuf[FFN_H:FFN_H + tm, :] = ua[...].astype(F32)
        gbuf[FFN_H:FFN_H + tm, :] = ug[...].astype(F32)
        a = _ffn_conv(abuf, wa, ba, tm)
        gt = _ffn_conv(gbuf, wg, bgt, tm)
        o_ref[...] = (gt * _sig(gt) * a).astype(BF16)

    return pl.pallas_call(
        body, name="ffn_act_fwd", grid=(nj, S // tm),
        in_specs=[pl.BlockSpec((tm, FFN_TC), lambda j, i: (i, j)), pl.BlockSpec((tm, FFN_TC), lambda j, i: (i, nj + j)),
                  pl.BlockSpec((FFN_K, FFN_TC), lambda j, i: (0, j)), pl.BlockSpec((FFN_K, FFN_TC), lambda j, i: (0, nj + j)),
                  pl.BlockSpec((1, FFN_TC), lambda j, i: (0, j)), pl.BlockSpec((1, FFN_TC), lambda j, i: (0, nj + j))],
        out_specs=pl.BlockSpec((tm, FFN_TC), lambda j, i: (i, j)), out_shape=SDS((S, F2 // 2), BF16),
        scratch_shapes=[pltpu.VMEM((tm + FFN_H, FFN_TC), F32)] * 2)(up, up, cw, cw, cb, cb)


def _ffn_down_loss(act, wdown, h1, target):
    S, D = h1.shape
    F = act.shape[1]

    def body(a_ref, w_ref, h_ref, t_ref, dy_ref, loss_ref):
        @pl.when(pl.program_id(0) == 0)
        def _():
            loss_ref[...] = jnp.zeros_like(loss_ref)

        err = h_ref[...] + jnp.dot(a_ref[...], w_ref[...], preferred_element_type=F32) - t_ref[...]
        dy_ref[...] = err * (1.0 / D)
        loss_ref[...] += 0.5 * jnp.sum(jnp.mean(err * err, axis=-1, keepdims=True))

    row = lambda w: pl.BlockSpec((TM, w), lambda i: (i, 0))
    return pl.pallas_call(
        body, name="ffn_down_loss", grid=(S // TM,),
        in_specs=[row(F), _full((F, D)), row(D), row(D)],
        out_specs=[row(D), _full((8, 128))], out_shape=[SDS((S, D), F32), SDS((8, 128), F32)])(act, wdown, h1, target)


def _ffn_bwd_a(dy, wdown, up, cw, cb):
    S, D = dy.shape
    F2 = up.shape[1]
    F = F2 // 2
    nj = F // FFN_TC
    tm = TM

    def body(dy_ref, wd_ref, ua, ug, wa, wg, ba, bgt, da_ref, dg_ref, acca_ref, accg_ref, abuf, gbuf, hala, halg):
        i, j = pl.program_id(0), pl.program_id(1)

        @pl.when((i == 0) & (j == 0))
        def _():
            acca_ref[...] = jnp.zeros_like(acca_ref)
            accg_ref[...] = jnp.zeros_like(accg_ref)

        @pl.when(i == 0)
        def _():
            abuf[0:FFN_H, :] = jnp.zeros((FFN_H, FFN_TC), F32)
            gbuf[0:FFN_H, :] = jnp.zeros((FFN_H, FFN_TC), F32)

        @pl.when(i > 0)
        def _():
            abuf[0:FFN_H, :] = hala[j]
            gbuf[0:FFN_H, :] = halg[j]

        abuf[FFN_H:FFN_H + tm, :] = ua[...].astype(F32)
        gbuf[FFN_H:FFN_H + tm, :] = ug[...].astype(F32)
        hala[j] = abuf[tm:tm + FFN_H, :]
        halg[j] = gbuf[tm:tm + FFN_H, :]
        a = _ffn_conv(abuf, wa, ba, tm)
        gt = _ffn_conv(gbuf, wg, bgt, tm)
        dact = lax.dot_general(dy_ref[...].astype(BF16), wd_ref[...], NT_DIMS, preferred_element_type=F32)
        sg = _sig(gt)
        dac = dact * (gt * sg)
        dgc = dact * a * (sg * (1.0 + gt * (1.0 - sg)))
        da_ref[...] = dac.astype(BF16)
        dg_ref[...] = dgc.astype(BF16)
        for k in range(FFN_K):
            sh = pl.ds(FFN_H - (FFN_K - 1) + k, tm)
            acca_ref[j, k:k + 1, :] += jnp.sum(dac * abuf[sh, :], axis=0, keepdims=True)
            accg_ref[j, k:k + 1, :] += jnp.sum(dgc * gbuf[sh, :], axis=0, keepdims=True)
        acca_ref[j, FFN_K:FFN_K + 1, :] += jnp.sum(dac, axis=0, keepdims=True)
        accg_ref[j, FFN_K:FFN_K + 1, :] += jnp.sum(dgc, axis=0, keepdims=True)

    return pl.pallas_call(
        body, name="ffn_bwd_a", grid=(S // tm, nj),
        in_specs=[pl.BlockSpec((tm, D), lambda i, j: (i, 0)), pl.BlockSpec((FFN_TC, D), lambda i, j: (j, 0)),
                  pl.BlockSpec((tm, FFN_TC), lambda i, j: (i, j)), pl.BlockSpec((tm, FFN_TC), lambda i, j: (i, nj + j)),
                  pl.BlockSpec((FFN_K, FFN_TC), lambda i, j: (0, j)), pl.BlockSpec((FFN_K, FFN_TC), lambda i, j: (0, nj + j)),
                  pl.BlockSpec((1, FFN_TC), lambda i, j: (0, j)), pl.BlockSpec((1, FFN_TC), lambda i, j: (0, nj + j))],
        out_specs=[pl.BlockSpec((tm, FFN_TC), lambda i, j: (i, j))] * 2 + [_full((nj, 8, FFN_TC))] * 2,
        out_shape=[SDS((S, F), BF16)] * 2 + [SDS((nj, 8, FFN_TC), F32)] * 2,
        scratch_shapes=[pltpu.VMEM((tm + FFN_H, FFN_TC), F32)] * 2 + [pltpu.VMEM((nj, FFN_H, FFN_TC), F32)] * 2,
    )(dy, wdown, up, up, cw, cw, cb, cb)


def _ffn_bwd_b(dca, dcg, cw):
    S, F = dca.shape
    nj = F // FFN_TC
    tm = 1024 if S % 1024 == 0 else TM
    nt = S // tm

    def body(a_ref, g_ref, w_ref, o_ref, ybuf):
        j, i = pl.program_id(0), pl.program_id(1)

        @pl.when(i == 0)
        def _():
            ybuf[tm:tm + FFN_H, :] = jnp.zeros((FFN_H, FFN_TC), F32)

        @pl.when(i > 0)
        def _():
            ybuf[tm:tm + FFN_H, :] = ybuf[0:FFN_H, :]

        ybuf[0:tm, :] = jnp.where(j < nj, a_ref[...], g_ref[...]).astype(F32)
        acc = jnp.zeros((tm, FFN_TC), F32)
        for s in range(FFN_K):
            acc = acc + ybuf[pl.ds(s, tm), :] * w_ref[FFN_K - 1 - s:FFN_K - s, :]
        o_ref[...] = acc.astype(BF16)

    return pl.pallas_call(
        body, name="ffn_bwd_b", grid=(2 * nj, nt),
        in_specs=[pl.BlockSpec((tm, FFN_TC), lambda j, i: (nt - 1 - i, jnp.minimum(j, nj - 1))),
                  pl.BlockSpec((tm, FFN_TC), lambda j, i: (nt - 1 - i, jnp.maximum(j - nj, 0))),
                  pl.BlockSpec((FFN_K, FFN_TC), lambda j, i: (0, j))],
        out_specs=pl.BlockSpec((tm, FFN_TC), lambda j, i: (nt - 1 - i, j)), out_shape=SDS((S, 2 * F), BF16),
        scratch_shapes=[pltpu.VMEM((tm + FFN_H, FFN_TC), F32)])(dca, dcg, cw)


def _adamw(w, g, m, v, name):
    R, C = w.shape
    tr = _row_tile(R, max(8, (2 ** 19) // (4 * C) // 8 * 8))

    def body(w_ref, g_ref, m_ref, v_ref, d_ref, nm_ref, nv_ref):
        gv = g_ref[...]
        m2 = ADAM_B1 * m_ref[...] + (1.0 - ADAM_B1) * gv
        v2 = ADAM_B2 * v_ref[...] + (1.0 - ADAM_B2) * jnp.square(gv)
        m_hat = m2 / (1.0 - ADAM_B1 ** ADAM_STEP)
        v_hat = v2 / (1.0 - ADAM_B2 ** ADAM_STEP)
        d_ref[...] = -ADAM_LR * (m_hat / (jnp.sqrt(v_hat) + ADAM_EPS) + ADAM_WD * w_ref[...])
        nm_ref[...] = m2
        nv_ref[...] = v2

    blk = pl.BlockSpec((tr, C), lambda i: (i, 0))
    return pl.pallas_call(
        body, name=name, grid=(R // tr,), in_specs=[blk] * 4, out_specs=[blk] * 3,
        out_shape=[SDS((R, C), F32)] * 3)(w, g, m, v)


def _position():
    return lax.axis_index("x"), lax.axis_index("y"), lax.axis_index("c")


def _other_chips(x, y):
    return [(1 - x, y), (x, 1 - y), (1 - x, 1 - y)]


def _all_gather_xy(arrs, name):
    n = len(arrs)
    hbm = pl.BlockSpec(memory_space=pl.ANY)

    def body(*refs):
        ins, outs = refs[:n], refs[n:2 * n]
        send_sems, recv_sems, loc_sems = refs[2 * n:]
        x, y, c = _position()
        me = 2 * x + y
        chips = _other_chips(x, y)

        def rcopy(i, k, src, dst, to):
            return pltpu.make_async_remote_copy(src_ref=src, dst_ref=dst, send_sem=send_sems.at[i, k], recv_sem=recv_sems.at[i, k],
                                                device_id=to, device_id_type=MESH)

        started = []
        for i in range(n):
            own = pltpu.make_async_copy(ins[i], outs[i].at[me], loc_sems.at[i])
            own.start()
            started.append(own)
        sends = []
        for i in range(n):
            for j, (px, py) in enumerate(chips):
                cp = rcopy(i, j, ins[i].at[c], outs[i].at[me, c], (px, py, c))
                cp.start()
                sends.append(cp)
        for i in range(n):
            for j, (px, py) in enumerate(chips):
                got = outs[i].at[2 * px + py, c]
                rcopy(i, j, ins[i].at[c], got, (x, y, c)).wait_recv()
                fwd = rcopy(i, 3 + j, got, got, (x, y, 1 - c))
                fwd.start()
                sends.append(fwd)
        for i in range(n):
            for j, (px, py) in enumerate(chips):
                theirs = outs[i].at[2 * px + py, 1 - c]
                rcopy(i, 3 + j, theirs, theirs, (x, y, c)).wait_recv()
        for cp in sends:
            cp.wait_send()
        for own in started:
            own.wait()

    return pl.pallas_call(
        body, name=name, in_specs=[hbm] * n, out_specs=[hbm] * n,
        out_shape=[SDS((4,) + a.shape, a.dtype) for a in arrs],
        scratch_shapes=[pltpu.SemaphoreType.DMA((n, 6)), pltpu.SemaphoreType.DMA((n, 6)), pltpu.SemaphoreType.DMA((n,))])(*arrs)


def _swap_halves(g):
    hbm = pl.BlockSpec(memory_space=pl.ANY)

    def body(g_ref, r_ref, send_sem, recv_sem):
        x, y, c = _position()
        cp = pltpu.make_async_remote_copy(src_ref=g_ref.at[1 - c], dst_ref=r_ref, send_sem=send_sem, recv_sem=recv_sem,
                                          device_id=(x, y, 1 - c), device_id_type=MESH)
        cp.start()
        cp.wait()

    return pl.pallas_call(
        body, name="rs_swap_halves", in_specs=[hbm], out_specs=hbm, out_shape=SDS(g.shape[1:], g.dtype),
        scratch_shapes=[pltpu.SemaphoreType.DMA, pltpu.SemaphoreType.DMA])(g)


def _exchange_shards(s1):
    hbm = pl.BlockSpec(memory_space=pl.ANY)

    def body(s_ref, r_ref, send_sems, recv_sems):
        x, y, c = _position()
        cps = []
        for j, (px, py) in enumerate(_other_chips(x, y)):
            cp = pltpu.make_async_remote_copy(src_ref=s_ref.at[2 * px + py], dst_ref=r_ref.at[j], send_sem=send_sems.at[j],
                                              recv_sem=recv_sems.at[j], device_id=(px, py, c), device_id_type=MESH)
            cp.start()
            cps.append(cp)
        for cp in cps:
            cp.wait()

    return pl.pallas_call(
        body, name="rs_exchange_shards", in_specs=[hbm], out_specs=hbm, out_shape=SDS((3,) + s1.shape[1:], s1.dtype),
        scratch_shapes=[pltpu.SemaphoreType.DMA((3,)), pltpu.SemaphoreType.DMA((3,))])(s1)


def _join_halves(f):
    hbm = pl.BlockSpec(memory_space=pl.ANY)

    def body(f_ref, o_ref, send_sem, recv_sem, loc_sem):
        x, y, c = _position()
        own = pltpu.make_async_copy(f_ref, o_ref.at[c], loc_sem)
        own.start()
        cp = pltpu.make_async_remote_copy(src_ref=f_ref, dst_ref=o_ref.at[c], send_sem=send_sem, recv_sem=recv_sem,
                                          device_id=(x, y, 1 - c), device_id_type=MESH)
        cp.start()
        pltpu.make_async_remote_copy(src_ref=f_ref, dst_ref=o_ref.at[1 - c], send_sem=send_sem, recv_sem=recv_sem,
                                     device_id=(x, y, 1 - c), device_id_type=MESH).wait_recv()
        cp.wait_send()
        own.wait()

    return pl.pallas_call(
        body, name="rs_join_halves", in_specs=[hbm], out_specs=hbm, out_shape=SDS((2,) + f.shape, f.dtype),
        scratch_shapes=[pltpu.SemaphoreType.DMA, pltpu.SemaphoreType.DMA, pltpu.SemaphoreType.DMA])(f)


def _add_pair(g, r1, c):
    R = r1.shape[0]
    tr = _row_tile(R, 2048)

    def body(c_ref, a_ref, b_ref, o_ref):
        del c_ref
        o_ref[...] = a_ref[...] + b_ref[...]

    return pl.pallas_call(
        body, name="rs_add_pair",
        grid_spec=pltpu.PrefetchScalarGridSpec(
            num_scalar_prefetch=1, grid=(R // tr,),
            in_specs=[pl.BlockSpec((None, tr, 128), lambda i, cr: (cr[0], i, 0)), pl.BlockSpec((tr, 128), lambda i, cr: (i, 0))],
            out_specs=pl.BlockSpec((tr, 128), lambda i, cr: (i, 0))),
        out_shape=SDS((R, 128), F32))(c, g, r1)


def _add_four(s1, r2, me):
    L = s1.shape[1]
    tr = _row_tile(L, 2048)

    def body(m_ref, a_ref, b_ref, o_ref):
        del m_ref
        o_ref[...] = ((a_ref[...] + b_ref[0]) + b_ref[1]) + b_ref[2]

    return pl.pallas_call(
        body, name="rs_add_four",
        grid_spec=pltpu.PrefetchScalarGridSpec(
            num_scalar_prefetch=1, grid=(L // tr,),
            in_specs=[pl.BlockSpec((None, tr, 128), lambda i, mr: (mr[0], i, 0)), pl.BlockSpec((3, tr, 128), lambda i, mr: (0, i, 0))],
            out_specs=pl.BlockSpec((tr, 128), lambda i, mr: (i, 0))),
        out_shape=SDS((L, 128), F32))(me, s1, r2)


def _all_reduce_small(v):
    R = v.shape[0]

    def body(v_ref, o_ref, buf, send_sems, recv_sems):
        x, y, c = _position()
        me = 4 * x + 2 * y + c
        buf[me] = v_ref[...]
        cps = []
        for k in range(1, 8):
            to = (1 - x if k & 4 else x, 1 - y if k & 2 else y, 1 - c if k & 1 else c)
            cp = pltpu.make_async_remote_copy(src_ref=buf.at[me], dst_ref=buf.at[me], send_sem=send_sems.at[k - 1],
                                              recv_sem=recv_sems.at[k - 1], device_id=to, device_id_type=MESH)
            cp.start()
            cps.append(cp)
        for cp in cps:
            cp.wait_send()
        for k in range(1, 8):
            src = 4 * (1 - x if k & 4 else x) + 2 * (1 - y if k & 2 else y) + (1 - c if k & 1 else c)
            pltpu.make_async_remote_copy(src_ref=buf.at[src], dst_ref=buf.at[src], send_sem=send_sems.at[k - 1],
                                         recv_sem=recv_sems.at[k - 1], device_id=(x, y, c), device_id_type=MESH).wait_recv()
        acc = buf[0]
        for k in range(1, 8):
            acc = acc + buf[k]
        o_ref[...] = acc

    vm = pl.BlockSpec(memory_space=pltpu.VMEM)
    return pl.pallas_call(
        body, name="all_reduce_small", in_specs=[vm], out_specs=vm, out_shape=SDS((R, 128), F32),
        scratch_shapes=[pltpu.VMEM((8, R, 128), F32), pltpu.SemaphoreType.DMA((7,)), pltpu.SemaphoreType.DMA((7,))])(v)


def _reduce_scatter_xyc(gflat):
    x, y, c = _position()
    L = gflat.shape[2]
    g2 = gflat.reshape(2, 4 * L, 128)
    s1 = _add_pair(g2, _swap_halves(g2), jnp.reshape(c, (1,)).astype(jnp.int32)).reshape(4, L, 128)
    f = _add_four(s1, _exchange_shards(s1), jnp.reshape(2 * x + y, (1,)).astype(jnp.int32))
    return _join_halves(f)


def _halves(a):
    return a.reshape((2, a.shape[0] // 2) + a.shape[1:])


def _pack_rows(vecs, mult=8 * 128):
    flat = jnp.concatenate([jnp.ravel(v) for v in vecs])
    pad = (-flat.shape[0]) % mult
    return jnp.pad(flat, (0, pad)).reshape(-1, 128)


def _unpack_rows(packed, shapes):
    flat = packed.reshape(-1)
    out, off = [], 0
    for s in shapes:
        n = int(np.prod(s))
        out.append(flat[off:off + n].reshape(s))
        off += n
    return out


def _local_step(x, mem, target, sp, bw):
    S, D = x.shape
    n_in = 4 * bw["w_in"].shape[2]
    F2 = 4 * bw["w_up"].shape[2]
    band, buckets = _bias_static()
    buckets = jnp.asarray(buckets)

    xn = _rmsnorm(x, sp["attn_norm_w"], "rms_in")
    z = _mm_nn(xn, bw["w_in"], BF16, "in_proj")
    qh, kh, vh = _qkv_prep(z, sp["q_norm_w"], sp["k_norm_w"])
    tab = sp["rel_bias_table"].T.reshape(N_GROUPS, HPG, N_BUCKETS)
    bias = _bias_fwd(jnp.pad(tab, ((0, 0), (0, 8 - HPG), (0, 0))), buckets)
    biasm = jnp.where(jnp.asarray(band)[None, None], bias[:, :HPG].reshape(N_GROUPS, HPG, NQ, 2 * NQ), NEG)
    os_, lses = [], []
    for g, (_, dil) in enumerate(ATTN_GROUPS):
        o_g, lse_g = _attn_fwd(qh, kh, vh, biasm[g], g, dil)
        os_.append(o_g)
        lses.append(lse_g)
    attn = _merge_fwd(os_, lses)
    u1, u3 = _conv_fwd(z, sp["conv_dw_w"], sp["conv_dw_b"], sp["conv_ln_w"], sp["conv_ln_b"])
    mn, kv, mk, mv = _memkv_fwd(mem, sp["mem_norm_w"], bw["w_mem_kv"], sp["xk_norm_w"])
    oc = _cross_fwd(z, sp["xq_norm_w"], mk, mv)
    h1, hn = _outproj_fwd(x, z, sp["b_gate"], attn, u3, oc, bw["w_attn_o"], bw["w_conv_o"], bw["w_cross_o"], bw["w_out"],
                          sp["ffn_norm_w"])
    up = _mm_nn(hn, bw["w_up"], BF16, "ffn_up")
    act = _ffn_act_fwd(up, sp["ffn_conv_w"], sp["ffn_conv_b"])
    dy, loss_tile = _ffn_down_loss(act, bw["w_down"], h1, target)

    gs, gb = {}, {}
    gb["w_down"] = _mm_tn(act, dy, 1, "dw_down")[0]
    dca, dcg, acca, accg = _ffn_bwd_a(dy, bw["w_down"], up, sp["ffn_conv_w"], sp["ffn_conv_b"])
    tap = lambda acc: jnp.transpose(acc, (1, 0, 2)).reshape(8, F2 // 2)
    ta, tg = tap(acca), tap(accg)
    gs["ffn_conv_w"] = jnp.concatenate([ta[:FFN_K], tg[:FFN_K]], axis=1)
    gs["ffn_conv_b"] = jnp.concatenate([ta[FFN_K:FFN_K + 1], tg[FFN_K:FFN_K + 1]], axis=1)
    dup = _ffn_bwd_b(dca, dcg, sp["ffn_conv_w"])
    gb["w_up"] = _mm_tn(hn, dup, 4, "dw_up")
    dh1, gs["ffn_norm_w"] = _norm_in_bwd(dup, bw["w_up"], h1, sp["ffn_norm_w"], dy, "ffn_in_bwd")
    dz, dattn, du3, doc, dya, dyc, dyx, merged, gs["b_gate"] = _outproj_bwd(
        dh1, z, sp["b_gate"], attn, u3, oc, bw["w_attn_o"], bw["w_conv_o"], bw["w_cross_o"], bw["w_out"], n_in)
    gb["w_out"] = _mm_tn(merged, dh1, 1, "dw_out")[0]
    gb["w_attn_o"] = _mm_tn(attn, dya, 4, "dw_attn_o")
    gb["w_conv_o"] = _mm_tn(u3, dyc, 4, "dw_conv_o")
    gb["w_cross_o"] = _mm_tn(oc, dyx, 4, "dw_cross_o")
    dz, dmk, dmv, gs["xq_norm_w"] = _cross_bwd(dz, doc, z, sp["xq_norm_w"], mk, mv)
    gb["w_mem_kv"], gs["xk_norm_w"], gs["mem_norm_w"] = _memkv_bwd(
        dmk, dmv, kv, mem, mn, sp["mem_norm_w"], bw["w_mem_kv"], sp["xk_norm_w"])
    du1, cacc = _conv_bwd_a(du3, u1, z, sp["conv_ln_w"], sp["conv_ln_b"])
    gs["conv_dw_w"], gs["conv_dw_b"] = cacc[:CONV_K], cacc[32:33]
    gs["conv_ln_w"], gs["conv_ln_b"] = cacc[33:34], cacc[34:35]
    dz = _conv_bwd_b(dz, du1, z, sp["conv_dw_w"])
    wg, dah, dhb = _merge_bwd(dattn, os_, lses)
    dqs, dks, dvs, dsbs = [], [], [], []
    for g, (_, dil) in enumerate(ATTN_GROUPS):
        dq_g, dk_g, dv_g, dsb_g = _attn_bwd(qh, kh, vh, biasm[g], dah, wg[g], dhb, lses[g], g, dil)
        dqs.append(dq_g)
        dks.append(dk_g)
        dvs.append(dv_g)
        dsbs.append(dsb_g.reshape(HPG, NQ * 2 * NQ))
    dtab = _bias_bwd(jnp.pad(jnp.stack(dsbs), ((0, 0), (0, 8 - HPG), (0, 0))), buckets)
    gs["rel_bias_table"] = dtab[:, :HPG].reshape(N_GROUPS * HPG, N_BUCKETS).T
    dz, gs["q_norm_w"], gs["k_norm_w"] = _qkv_bwd(dz, z, dqs, dks, dvs, sp["q_norm_w"], sp["k_norm_w"])
    gb["w_in"] = _mm_tn(xn, dz, 4, "dw_in")
    dx, gs["attn_norm_w"] = _norm_in_bwd(dz, bw["w_in"], x, sp["attn_norm_w"], dh1, "in_bwd")
    return loss_tile, dx, gs, gb


SMALL = ("rel_bias_table", "attn_norm_w", "b_gate", "q_norm_w", "k_norm_w", "conv_dw_w", "conv_dw_b", "conv_ln_w", "conv_ln_b",
         "mem_norm_w", "xq_norm_w", "xk_norm_w", "ffn_norm_w", "ffn_conv_w", "ffn_conv_b")
SMALL_SHARDED = ("conv_dw_w", "ffn_conv_w")
BIG_COL = ("w_in", "w_attn_o", "w_conv_o", "w_cross_o", "w_up")
BIG_ROW = ("w_mem_kv", "w_out", "w_down")
BIG = BIG_COL + BIG_ROW
WEIGHTS = ("rel_bias_table", "attn_norm_w", "w_in", "b_gate", "q_norm_w", "k_norm_w", "w_attn_o", "conv_dw_w", "conv_dw_b",
           "conv_ln_w", "conv_ln_b", "w_conv_o", "mem_norm_w", "w_mem_kv", "xq_norm_w", "xk_norm_w", "w_cross_o", "w_out",
           "ffn_norm_w", "w_up", "ffn_conv_w", "ffn_conv_b", "w_down")


def _step(x, mem, target, w, m, v):
    xi, yi, _ = _position()
    shard = 2 * xi + yi

    big_local = [_halves(w[k].astype(BF16)) for k in BIG]
    small_local = []
    for k in SMALL_SHARDED:
        flat = jnp.ravel(w[k])
        small_local.append(jnp.pad(flat, (0, (-flat.shape[0]) % 2048)).reshape(2, -1, 128))
    gathered = _all_gather_xy(big_local + small_local, "all_gather_weights")
    bw = {}
    for k, ga in zip(BIG, gathered):
        ga = ga.reshape((4,) + w[k].shape)
        bw[k] = ga if k in BIG_COL else ga.reshape((4 * w[k].shape[0],) + w[k].shape[1:])
    sp = {k: w[k] for k in SMALL if k not in SMALL_SHARDED}
    for k, ga in zip(SMALL_SHARDED, gathered[len(BIG):]):
        r, cdim = w[k].shape
        parts = ga.reshape(4, -1)[:, :r * cdim].reshape(4, r, cdim)
        sp[k] = jnp.transpose(parts, (1, 0, 2)).reshape(r, 4 * cdim)

    loss_tile, dx, gs, gb = _local_step(x, mem, target, sp, bw)

    small_shapes = [(8, 128)] + [gs[k].shape for k in SMALL]
    red = _unpack_rows(_all_reduce_small(_pack_rows([loss_tile] + [gs[k] for k in SMALL])), small_shapes)
    loss = red[0][0, 0]
    g_small = dict(zip(SMALL, red[1:]))
    for k in SMALL_SHARDED:
        cdim = w[k].shape[1]
        g_small[k] = lax.dynamic_slice_in_dim(g_small[k], shard * cdim, cdim, axis=1)

    sizes = [int(np.prod(w[k].shape)) for k in BIG]
    total = sum(sizes)
    padded = -(-total // 2048) * 2048
    pieces = [gb[k].reshape(4, -1) for k in BIG]
    if padded > total:
        pieces.append(jnp.zeros((4, padded - total), F32))
    gflat = jnp.concatenate(pieces, axis=1).reshape(4, 2, padded // 256, 128)
    red_big = _reduce_scatter_xyc(jnp.transpose(gflat, (1, 0, 2, 3))).reshape(-1)
    g_big, off = {}, 0
    for k, n in zip(BIG, sizes):
        g_big[k] = red_big[off:off + n].reshape(w[k].shape)
        off += n

    grads, delta, new_m, new_v = {}, {}, {}, {}
    for k in BIG:
        grads[k] = g_big[k]
        delta[k], new_m[k], new_v[k] = _adamw(w[k], g_big[k], m[k], v[k], "adamw_" + k)
    shapes = [w[k].shape for k in SMALL]
    packed = [_pack_rows([d[k] for k in SMALL]) for d in (w, g_small, m, v)]
    outs = _adamw(*packed, "adamw_small")
    for dst, pk in zip((delta, new_m, new_v), outs):
        dst.update(zip(SMALL, _unpack_rows(pk, shapes)))
    grads.update(g_small)
    return loss, dx, grads, delta, new_m, new_v


def kernel(x, mem, rel_bias_table, attn_norm_w, w_in, b_gate, q_norm_w, k_norm_w, w_attn_o, conv_dw_w, conv_dw_b, conv_ln_w, conv_ln_b, w_conv_o, mem_norm_w, w_mem_kv, xq_norm_w, xk_norm_w, w_cross_o, w_out, ffn_norm_w, w_up, ffn_conv_w, ffn_conv_b, w_down, loss_target, m_rel_bias_table, m_attn_norm_w, m_w_in, m_b_gate, m_q_norm_w, m_k_norm_w, m_w_attn_o, m_conv_dw_w, m_conv_dw_b, m_conv_ln_w, m_conv_ln_b, m_w_conv_o, m_mem_norm_w, m_w_mem_kv, m_xq_norm_w, m_xk_norm_w, m_w_cross_o, m_w_out, m_ffn_norm_w, m_w_up, m_ffn_conv_w, m_ffn_conv_b, m_w_down, v_rel_bias_table, v_attn_norm_w, v_w_in, v_b_gate, v_q_norm_w, v_k_norm_w, v_w_attn_o, v_conv_dw_w, v_conv_dw_b, v_conv_ln_w, v_conv_ln_b, v_w_conv_o, v_mem_norm_w, v_w_mem_kv, v_xq_norm_w, v_xk_norm_w, v_w_cross_o, v_w_out, v_ffn_norm_w, v_w_up, v_ffn_conv_w, v_ffn_conv_b, v_w_down):
    args = locals()
    def block(name, k):
        a = args[name] if k == "rel_bias_table" else args[name][0]
        return a.reshape(1, -1) if a.ndim == 1 else a

    w = {k: block(k, k) for k in WEIGHTS}
    m = {k: block("m_" + k, k) for k in WEIGHTS}
    v = {k: block("v_" + k, k) for k in WEIGHTS}
    loss, dx, grads, delta, new_m, new_v = _step(x[0], mem[0], loss_target[0], w, m, v)
    out = [loss, dx[None]]
    for d in (grads, delta, new_m, new_v):
        for k in WEIGHTS:
            out.append(d[k].reshape(args[k].shape))
    return tuple(out)
```

```python
import functools
import math

import numpy as np
import jax
import jax.numpy as jnp
from jax import lax
from jax.experimental import pallas as pl
from jax.experimental.pallas import tpu as pltpu

F32, BF16 = jnp.float32, jnp.bfloat16
SDS = jax.ShapeDtypeStruct
MESH = pl.DeviceIdType.MESH

HEAD = 128
N_GROUPS, HPG = 3, 4
ATTN_GROUPS = ((128, 1), (512, 4), (2048, 16))
NQ = 128
QKV_W = N_GROUPS * HPG * HEAD
CH = 512
CONV_K, FFN_K = 31, 3
N_BUCKETS, MAX_DIST = 32, 2048
RMS_EPS, LN_EPS = 1e-6, 1e-5
O_Q, O_K, O_V, O_CV, O_CG, O_XQ, O_G = 0, QKV_W, 2 * QKV_W, 3 * QKV_W, 3 * QKV_W + CH, 3 * QKV_W + 2 * CH, 3 * QKV_W + 3 * CH
ADAM_LR, ADAM_B1, ADAM_B2, ADAM_EPS, ADAM_WD, ADAM_STEP = 0.001, 0.9, 0.999, 1e-08, 0.01, 10
NEG = -1e30
SCALE = HEAD ** -0.5
TM = 512
ATT_RB = 2048
NT_DIMS = (((1,), (1,)), ((), ()))
TN_DIMS = (((0,), (0,)), ((), ()))


def _sig(v):
    return 1.0 / (1.0 + jnp.exp(-v))


def _row_tile(rows, cap, mult=8):
    best = None
    for t in range(mult, min(rows, cap) + 1, mult):
        if rows % t == 0:
            best = t
    return best if best is not None else rows


def _full(shape):
    n = len(shape)
    return pl.BlockSpec(shape, lambda *a: (0,) * n)


def _rmsnorm(x, w, name):
    S, D = x.shape

    def body(x_ref, w_ref, o_ref):
        xv = x_ref[...]
        r = lax.rsqrt(jnp.mean(xv * xv, axis=-1, keepdims=True) + RMS_EPS)
        o_ref[...] = (xv * r * w_ref[...]).astype(BF16)

    return pl.pallas_call(
        body, name=name, grid=(S // TM,),
        in_specs=[pl.BlockSpec((TM, D), lambda i: (i, 0)), _full((1, D))],
        out_specs=pl.BlockSpec((TM, D), lambda i: (i, 0)),
        out_shape=SDS((S, D), BF16))(x, w)


def _mm_nn(a, b, out_dtype, name):
    M, K = a.shape
    G, _, n = b.shape

    def body(a_ref, b_ref, o_ref):
        o_ref[...] = jnp.dot(a_ref[...].astype(BF16), b_ref[...], preferred_element_type=F32).astype(out_dtype)

    return pl.pallas_call(
        body, name=name, grid=(G, M // TM),
        in_specs=[pl.BlockSpec((TM, K), lambda g, i: (i, 0)), pl.BlockSpec((None, K, n), lambda g, i: (g, 0, 0))],
        out_specs=pl.BlockSpec((TM, n), lambda g, i: (i, g)),
        out_shape=SDS((M, G * n), out_dtype))(a, b)


def _mm_tn(a, b, G, name):
    S, Ka = a.shape
    n = b.shape[1] // G
    tka = Ka
    while tka * n * 4 > 6 * 2 ** 20 and tka % 256 == 0:
        tka //= 2

    def body(a_ref, b_ref, o_ref):
        @pl.when(pl.program_id(2) == 0)
        def _():
            o_ref[...] = jnp.zeros_like(o_ref)
        o_ref[...] += lax.dot_general(a_ref[...].astype(BF16), b_ref[...].astype(BF16), TN_DIMS, preferred_element_type=F32)

    return pl.pallas_call(
        body, name=name, grid=(G, Ka // tka, S // TM),
        in_specs=[pl.BlockSpec((TM, tka), lambda g, i, k: (k, i)), pl.BlockSpec((TM, n), lambda g, i, k: (k, g))],
        out_specs=pl.BlockSpec((None, tka, n), lambda g, i, k: (g, i, 0)),
        out_shape=SDS((G, Ka, n), F32))(a, b)


def _norm_in_bwd(a, w, xin, nw, resid, name):
    S, K = xin.shape
    G, _, n = w.shape

    def body(a_ref, w_ref, x_ref, nw_ref, r_ref, o_ref, dnw_ref, acc):
        i, g = pl.program_id(0), pl.program_id(1)

        @pl.when((i == 0) & (g == 0))
        def _():
            dnw_ref[...] = jnp.zeros_like(dnw_ref)

        @pl.when(g == 0)
        def _():
            acc[...] = jnp.zeros_like(acc)

        acc[...] += lax.dot_general(a_ref[...], w_ref[...], NT_DIMS, preferred_element_type=F32)

        @pl.when(g == G - 1)
        def _():
            dn = acc[...]
            xv = x_ref[...]
            r = lax.rsqrt(jnp.mean(xv * xv, axis=-1, keepdims=True) + RMS_EPS)
            xhat = xv * r
            dyw = dn * nw_ref[...]
            o_ref[...] = r_ref[...] + r * (dyw - xhat * jnp.mean(dyw * xhat, axis=-1, keepdims=True))
            dnw_ref[...] += jnp.sum(dn * xhat, axis=0, keepdims=True)

    return pl.pallas_call(
        body, name=name, grid=(S // TM, G),
        in_specs=[pl.BlockSpec((TM, n), lambda i, g: (i, g)), pl.BlockSpec((None, K, n), lambda i, g: (g, 0, 0)),
                  pl.BlockSpec((TM, K), lambda i, g: (i, 0)), _full((1, K)), pl.BlockSpec((TM, K), lambda i, g: (i, 0))],
        out_specs=[pl.BlockSpec((TM, K), lambda i, g: (i, 0)), _full((1, K))],
        out_shape=[SDS((S, K), F32), SDS((1, K), F32)],
        scratch_shapes=[pltpu.VMEM((TM, K), F32)])(a, w, xin, nw, resid)


def _t5_bucket_np(dist):
    max_exact = N_BUCKETS // 2
    d = np.maximum(dist.astype(np.float32), np.float32(1.0))
    large = max_exact + (np.log(d / np.float32(max_exact)) / np.float32(math.log(MAX_DIST / max_exact))
                         * np.float32(N_BUCKETS - max_exact)).astype(np.int32)
    large = np.minimum(large, N_BUCKETS - 1)
    return np.where(dist < max_exact, dist, large).astype(np.int32)


def _bias_static():
    qi = np.arange(NQ)[:, None]
    kj = np.arange(2 * NQ)[None, :]
    step = qi + NQ - kj
    band = (step >= 0) & (step <= NQ)
    buckets = np.stack([_t5_bucket_np(np.clip(step, 0, None) * dil).reshape(1, -1) for _, dil in ATTN_GROUPS])
    return band, buckets


def _bias_fwd(table_t, buckets):
    nb = buckets.shape[-1]

    def body(t_ref, b_ref, o_ref):
        oh = (b_ref[...] == lax.broadcasted_iota(jnp.int32, (N_BUCKETS, nb), 0)).astype(F32)
        o_ref[...] = jnp.dot(t_ref[...], oh, preferred_element_type=F32, precision=lax.Precision.HIGHEST)

    return pl.pallas_call(
        body, name="bias_fwd", grid=(N_GROUPS,),
        in_specs=[pl.BlockSpec((None, 8, N_BUCKETS), lambda g: (g, 0, 0)), pl.BlockSpec((None, 1, nb), lambda g: (g, 0, 0))],
        out_specs=pl.BlockSpec((None, 8, nb), lambda g: (g, 0, 0)),
        out_shape=SDS((N_GROUPS, 8, nb), F32))(table_t, buckets)


def _bias_bwd(dsb, buckets):
    nb = buckets.shape[-1]

    def body(d_ref, b_ref, o_ref):
        oh = (b_ref[...] == lax.broadcasted_iota(jnp.int32, (N_BUCKETS, nb), 0)).astype(F32)
        o_ref[...] = lax.dot_general(d_ref[...], oh, NT_DIMS, preferred_element_type=F32, precision=lax.Precision.HIGHEST)

    return pl.pallas_call(
        body, name="bias_bwd", grid=(N_GROUPS,),
        in_specs=[pl.BlockSpec((None, 8, nb), lambda g: (g, 0, 0)), pl.BlockSpec((None, 1, nb), lambda g: (g, 0, 0))],
        out_specs=pl.BlockSpec((None, 8, N_BUCKETS), lambda g: (g, 0, 0)),
        out_shape=SDS((N_GROUPS, 8, N_BUCKETS), F32))(dsb, buckets)


def _qkv_prep(z, qw, kw):
    S = z.shape[0]
    nh = N_GROUPS * HPG

    def body(zq, zk, zv, qw_ref, kw_ref, qh, kh, vh):
        for h in range(nh):
            g = h // HPG
            sl = slice(h * HEAD, (h + 1) * HEAD)
            xq = zq[:, sl].astype(F32)
            qh[h] = xq * lax.rsqrt(jnp.mean(xq * xq, axis=-1, keepdims=True) + RMS_EPS) * qw_ref[g:g + 1, :]
            xk = zk[:, sl].astype(F32)
            kh[h] = xk * lax.rsqrt(jnp.mean(xk * xk, axis=-1, keepdims=True) + RMS_EPS) * kw_ref[g:g + 1, :]
            vh[h] = zv[:, sl].astype(F32)

    hm = pl.BlockSpec((nh, TM, HEAD), lambda i: (0, i, 0))
    return pl.pallas_call(
        body, name="qkv_prep", grid=(S // TM,),
        in_specs=[pl.BlockSpec((TM, QKV_W), lambda i: (i, 0)), pl.BlockSpec((TM, QKV_W), lambda i: (i, 1)),
                  pl.BlockSpec((TM, QKV_W), lambda i: (i, 2)), _full((N_GROUPS, HEAD)), _full((N_GROUPS, HEAD))],
        out_specs=[hm, hm, hm],
        out_shape=[SDS((nh, S, HEAD), F32)] * 3)(z, z, z, qw, kw)


def _rows(start, d):
    return pl.ds(start, NQ) if d == 1 else pl.ds(start, NQ, stride=d)


def _attn_fwd(qh, kh, vh, biasm, g, d):
    S = qh.shape[1]
    RB = ATT_RB
    nbk, nq = S // RB, RB // (NQ * d)

    def body(q_ref, k_ref, v_ref, bias_ref, o_ref, lse_ref, kbuf, vbuf):
        b = pl.program_id(1)

        @pl.when(b == 0)
        def _():
            kbuf[0:RB, :] = jnp.zeros((RB, HEAD), F32)
            vbuf[0:RB, :] = jnp.zeros((RB, HEAD), F32)

        @pl.when(b > 0)
        def _():
            kbuf[0:RB, :] = kbuf[RB:2 * RB, :]
            vbuf[0:RB, :] = vbuf[RB:2 * RB, :]

        kbuf[RB:2 * RB, :] = k_ref[...]
        vbuf[RB:2 * RB, :] = v_ref[...]
        bias = bias_ref[...]
        col = lax.broadcasted_iota(jnp.int32, (NQ, 2 * NQ), 1)

        for qb in range(nq):
            def unit(r, carry, qb=qb):
                qs = qb * NQ * d + r
                q = q_ref[_rows(qs, d), :].astype(BF16)
                kw = jnp.concatenate([kbuf[_rows(RB + qs - NQ * d, d), :], kbuf[_rows(RB + qs, d), :]], axis=0).astype(BF16)
                vw = jnp.concatenate([vbuf[_rows(RB + qs - NQ * d, d), :], vbuf[_rows(RB + qs, d), :]], axis=0).astype(BF16)
                s = lax.dot_general(q, kw, NT_DIMS, preferred_element_type=F32) * SCALE + bias
                if qb == 0:
                    s = jnp.where((col < NQ) & (b == 0), NEG, s)
                m = jnp.max(s, axis=-1, keepdims=True)
                p = jnp.exp(s - m)
                l = jnp.sum(p, axis=-1, keepdims=True)
                o = jnp.dot(p.astype(BF16), vw, preferred_element_type=F32) / l
                o_ref[_rows(qs, d), :] = o
                lse_ref[_rows(qs, d), :] = jnp.broadcast_to(m + jnp.log(l), (NQ, HEAD))
                return carry

            if d == 1:
                unit(0, 0)
            else:
                lax.fori_loop(0, d, unit, 0)

    blk = lambda f: pl.BlockSpec((None, RB, HEAD), f)
    return pl.pallas_call(
        body, name=f"attn_fwd_g{g}", grid=(HPG, nbk),
        in_specs=[blk(lambda h, b: (HPG * g + h, b, 0))] * 3 + [pl.BlockSpec((None, NQ, 2 * NQ), lambda h, b: (h, 0, 0))],
        out_specs=[blk(lambda h, b: (h, b, 0))] * 2,
        out_shape=[SDS((HPG, S, HEAD), F32)] * 2,
        scratch_shapes=[pltpu.VMEM((2 * RB, HEAD), F32)] * 2)(qh, kh, vh, biasm)


def _attn_bwd(qh, kh, vh, biasm, da, wg, dh, lse, g, d):
    S = qh.shape[1]
    RB = ATT_RB
    nbk, nq = S // RB, RB // (NQ * d)

    def body(q_ref, k_ref, v_ref, bias_ref, da_ref, wg_ref, dh_ref, lse_ref,
             dq_ref, dk_ref, dv_ref, dsb_ref, kbuf, vbuf, dkbuf, dvbuf):
        b = pl.program_id(1)
        zero = jnp.zeros((RB, HEAD), F32)

        @pl.when(b == 0)
        def _():
            kbuf[0:RB, :] = zero
            vbuf[0:RB, :] = zero
            dkbuf[0:RB, :] = zero
            dvbuf[0:RB, :] = zero
            dsb_ref[...] = jnp.zeros_like(dsb_ref)

        @pl.when(b > 0)
        def _():
            kbuf[0:RB, :] = kbuf[RB:2 * RB, :]
            vbuf[0:RB, :] = vbuf[RB:2 * RB, :]
            dkbuf[0:RB, :] = dkbuf[RB:2 * RB, :]
            dvbuf[0:RB, :] = dvbuf[RB:2 * RB, :]

        dkbuf[RB:2 * RB, :] = zero
        dvbuf[RB:2 * RB, :] = zero

        @pl.when(b < nbk)
        def _():
            kbuf[RB:2 * RB, :] = k_ref[...]
            vbuf[RB:2 * RB, :] = v_ref[...]
            bias = bias_ref[...]
            col = lax.broadcasted_iota(jnp.int32, (NQ, 2 * NQ), 1)

            for qb in range(nq):
                def unit(r, carry, qb=qb):
                    qs = qb * NQ * d + r
                    prev, cur = _rows(RB + qs - NQ * d, d), _rows(RB + qs, d)
                    q = q_ref[_rows(qs, d), :].astype(BF16)
                    kw = jnp.concatenate([kbuf[prev, :], kbuf[cur, :]], axis=0).astype(BF16)
                    vw = jnp.concatenate([vbuf[prev, :], vbuf[cur, :]], axis=0).astype(BF16)
                    s = lax.dot_general(q, kw, NT_DIMS, preferred_element_type=F32) * SCALE + bias
                    if qb == 0:
                        s = jnp.where((col < NQ) & (b == 0), NEG, s)
                    p = jnp.exp(s - lse_ref[_rows(qs, d), :][:, 0:1])
                    w = wg_ref[_rows(qs, d), :]
                    do = (da_ref[_rows(qs, d), :] * w).astype(BF16)
                    dp = lax.dot_general(do, vw, NT_DIMS, preferred_element_type=F32)
                    ds = p * (dp - w[:, 0:1] * dh_ref[_rows(qs, d), :][:, 0:1])
                    dsb_ref[...] += ds
                    dsb = ds.astype(BF16)
                    dq_ref[_rows(qs, d), :] = jnp.dot(dsb, kw, preferred_element_type=F32) * SCALE
                    dkw = lax.dot_general(dsb, q, TN_DIMS, preferred_element_type=F32) * SCALE
                    dvw = lax.dot_general(p.astype(BF16), do, TN_DIMS, preferred_element_type=F32)
                    dkbuf[prev, :] += dkw[0:NQ, :]
                    dkbuf[cur, :] += dkw[NQ:2 * NQ, :]
                    dvbuf[prev, :] += dvw[0:NQ, :]
                    dvbuf[cur, :] += dvw[NQ:2 * NQ, :]
                    return carry

                if d == 1:
                    unit(0, 0)
                else:
                    lax.fori_loop(0, d, unit, 0)

        dk_ref[...] = dkbuf[0:RB, :]
        dv_ref[...] = dvbuf[0:RB, :]

    blk = lambda f: pl.BlockSpec((None, RB, HEAD), f)
    cur_g = blk(lambda h, b: (HPG * g + h, jnp.minimum(b, nbk - 1), 0))
    cur = blk(lambda h, b: (h, jnp.minimum(b, nbk - 1), 0))
    prv = blk(lambda h, b: (h, jnp.maximum(b - 1, 0), 0))
    sq = pl.BlockSpec((None, NQ, 2 * NQ), lambda h, b: (h, 0, 0))
    return pl.pallas_call(
        body, name=f"attn_bwd_g{g}", grid=(HPG, nbk + 1),
        in_specs=[cur_g, cur_g, cur_g, sq, cur, cur, cur, cur],
        out_specs=[cur, prv, prv, sq],
        out_shape=[SDS((HPG, S, HEAD), F32)] * 3 + [SDS((HPG, NQ, 2 * NQ), F32)],
        scratch_shapes=[pltpu.VMEM((2 * RB, HEAD), F32)] * 4)(qh, kh, vh, biasm, da, wg, dh, lse)


def _merge_weights(l0, l1, l2):
    m = jnp.maximum(jnp.maximum(l0, l1), l2)
    e0, e1, e2 = jnp.exp(l0 - m), jnp.exp(l1 - m), jnp.exp(l2 - m)
    inv = 1.0 / (e0 + e1 + e2)
    return e0 * inv, e1 * inv, e2 * inv


def _merge_fwd(os_, lses):
    S = os_[0].shape[1]

    def body(o0, o1, o2, l0, l1, l2, a_ref):
        for h in range(HPG):
            w0, w1, w2 = _merge_weights(l0[h], l1[h], l2[h])
            a_ref[:, h * HEAD:(h + 1) * HEAD] = (w0 * o0[h] + w1 * o1[h] + w2 * o2[h]).astype(BF16)

    hm = pl.BlockSpec((HPG, TM, HEAD), lambda i: (0, i, 0))
    return pl.pallas_call(
        body, name="merge_fwd", grid=(S // TM,), in_specs=[hm] * 6,
        out_specs=pl.BlockSpec((TM, CH), lambda i: (i, 0)),
        out_shape=SDS((S, CH), BF16))(*os_, *lses)


def _merge_bwd(dattn, os_, lses):
    S = dattn.shape[0]

    def body(da_ref, o0, o1, o2, l0, l1, l2, wg_ref, dah_ref, dh_ref):
        for h in range(HPG):
            w = _merge_weights(l0[h], l1[h], l2[h])
            attn = w[0] * o0[h] + w[1] * o1[h] + w[2] * o2[h]
            da = da_ref[:, h * HEAD:(h + 1) * HEAD]
            for gi in range(N_GROUPS):
                wg_ref[gi, h] = w[gi]
            dah_ref[h] = da
            dh_ref[h] = jnp.broadcast_to(jnp.sum(da * attn, axis=-1, keepdims=True), (TM, HEAD))

    hm = pl.BlockSpec((HPG, TM, HEAD), lambda i: (0, i, 0))
    return pl.pallas_call(
        body, name="merge_bwd", grid=(S // TM,),
        in_specs=[pl.BlockSpec((TM, CH), lambda i: (i, 0))] + [hm] * 6,
        out_specs=[pl.BlockSpec((N_GROUPS, HPG, TM, HEAD), lambda i: (0, 0, i, 0)), hm, hm],
        out_shape=[SDS((N_GROUPS, HPG, S, HEAD), F32), SDS((HPG, S, HEAD), F32), SDS((HPG, S, HEAD), F32)])(dattn, *os_, *lses)


def _qkv_bwd(dz, z, dqs, dks, dvs, qw, kw):
    S = z.shape[0]
    nh = N_GROUPS * HPG

    def body(dz_in, zq, zk, *refs):
        del dz_in
        dq_refs, dk_refs, dv_refs = refs[0:3], refs[3:6], refs[6:9]
        qw_ref, kw_ref, dz_ref, dqw_ref, dkw_ref = refs[9:]

        @pl.when(pl.program_id(0) == 0)
        def _():
            dqw_ref[...] = jnp.zeros_like(dqw_ref)
            dkw_ref[...] = jnp.zeros_like(dkw_ref)

        def nbwd(xr, dy, wr, dwr, h, off):
            g = h // HPG
            x = xr[:, h * HEAD:(h + 1) * HEAD].astype(F32)
            r = lax.rsqrt(jnp.mean(x * x, axis=-1, keepdims=True) + RMS_EPS)
            xhat = x * r
            dyw = dy * wr[g:g + 1, :]
            dz_ref[:, off + h * HEAD:off + (h + 1) * HEAD] = (
                r * (dyw - xhat * jnp.mean(dyw * xhat, axis=-1, keepdims=True))).astype(BF16)
            dwr[g:g + 1, :] += jnp.sum(dy * xhat, axis=0, keepdims=True)

        for h in range(nh):
            g, hh = h // HPG, h % HPG
            nbwd(zq, dq_refs[g][hh], qw_ref, dqw_ref, h, O_Q)
            nbwd(zk, dk_refs[g][hh], kw_ref, dkw_ref, h, O_K)
            dz_ref[:, O_V + h * HEAD:O_V + (h + 1) * HEAD] = dv_refs[g][hh].astype(BF16)

    hm = pl.BlockSpec((HPG, TM, HEAD), lambda i: (0, i, 0))
    return pl.pallas_call(
        body, name="qkv_bwd", grid=(S // TM,),
        in_specs=[pl.BlockSpec(memory_space=pl.ANY), pl.BlockSpec((TM, QKV_W), lambda i: (i, 0)),
                  pl.BlockSpec((TM, QKV_W), lambda i: (i, 1))] + [hm] * 9 + [_full((N_GROUPS, HEAD)), _full((N_GROUPS, HEAD))],
        out_specs=[pl.BlockSpec((TM, 3 * QKV_W), lambda i: (i, 0)), _full((N_GROUPS, HEAD)), _full((N_GROUPS, HEAD))],
        out_shape=[SDS(dz.shape, BF16), SDS((N_GROUPS, HEAD), F32), SDS((N_GROUPS, HEAD), F32)],
        input_output_aliases={0: 0})(dz, z, z, *dqs, *dks, *dvs, qw, kw)


def _conv_fwd(z, cw, cb, lnw, lnb):
    S = z.shape[0]
    H = 32

    def body(zv, zg, cw_ref, cb_ref, lnw_ref, lnb_ref, u1_ref, u3_ref, xbuf):
        i = pl.program_id(0)

        @pl.when(i == 0)
        def _():
            xbuf[0:H, :] = jnp.zeros((H, CH), F32)

        @pl.when(i > 0)
        def _():
            xbuf[0:H, :] = xbuf[TM:TM + H, :]

        xbuf[H:H + TM, :] = zv[...].astype(F32) * _sig(zg[...].astype(F32))
        acc = jnp.broadcast_to(cb_ref[...], (TM, CH))
        for k in range(CONV_K):
            acc = acc + xbuf[pl.ds(H - (CONV_K - 1) + k, TM), :] * cw_ref[k:k + 1, :]
        u1_ref[...] = acc
        mu = jnp.mean(acc, axis=-1, keepdims=True)
        xc = acc - mu
        yl = xc * lax.rsqrt(jnp.mean(xc * xc, axis=-1, keepdims=True) + LN_EPS) * lnw_ref[...] + lnb_ref[...]
        u3_ref[...] = (yl * _sig(yl)).astype(BF16)

    row = pl.BlockSpec((TM, CH), lambda i: (i, 0))
    return pl.pallas_call(
        body, name="conv_fwd", grid=(S // TM,),
        in_specs=[pl.BlockSpec((TM, CH), lambda i: (i, O_CV // CH)), pl.BlockSpec((TM, CH), lambda i: (i, O_CG // CH)),
                  _full((CONV_K, CH)), _full((1, CH)), _full((1, CH)), _full((1, CH))],
        out_specs=[row, row], out_shape=[SDS((S, CH), F32), SDS((S, CH), BF16)],
        scratch_shapes=[pltpu.VMEM((TM + H, CH), F32)])(z, z, cw, cb, lnw, lnb)


def _conv_bwd_a(du3, u1, z, lnw, lnb):
    S = z.shape[0]
    H = 32

    def body(du3_ref, u1_ref, zv, zg, lnw_ref, lnb_ref, du1_ref, acc_ref, xbuf):
        i = pl.program_id(0)

        @pl.when(i == 0)
        def _():
            xbuf[0:H, :] = jnp.zeros((H, CH), F32)
            acc_ref[...] = jnp.zeros_like(acc_ref)

        @pl.when(i > 0)
        def _():
            xbuf[0:H, :] = xbuf[TM:TM + H, :]

        xbuf[H:H + TM, :] = zv[...].astype(F32) * _sig(zg[...].astype(F32))
        u1 = u1_ref[...]
        mu = jnp.mean(u1, axis=-1, keepdims=True)
        xc = u1 - mu
        r = lax.rsqrt(jnp.mean(xc * xc, axis=-1, keepdims=True) + LN_EPS)
        yhat = xc * r
        yl = yhat * lnw_ref[...] + lnb_ref[...]
        sg = _sig(yl)
        dyl = du3_ref[...] * (sg * (1.0 + yl * (1.0 - sg)))
        dyh = dyl * lnw_ref[...]
        du1 = r * (dyh - jnp.mean(dyh, axis=-1, keepdims=True) - yhat * jnp.mean(dyh * yhat, axis=-1, keepdims=True))
        du1_ref[...] = du1
        acc_ref[33:34, :] += jnp.sum(dyl * yhat, axis=0, keepdims=True)
        acc_ref[34:35, :] += jnp.sum(dyl, axis=0, keepdims=True)
        acc_ref[32:33, :] += jnp.sum(du1, axis=0, keepdims=True)
        for k in range(CONV_K):
            acc_ref[k:k + 1, :] += jnp.sum(du1 * xbuf[pl.ds(H - (CONV_K - 1) + k, TM), :], axis=0, keepdims=True)

    row = pl.BlockSpec((TM, CH), lambda i: (i, 0))
    return pl.pallas_call(
        body, name="conv_bwd_a", grid=(S // TM,),
        in_specs=[row, row, pl.BlockSpec((TM, CH), lambda i: (i, O_CV // CH)), pl.BlockSpec((TM, CH), lambda i: (i, O_CG // CH)),
                  _full((1, CH)), _full((1, CH))],
        out_specs=[row, _full((40, CH))], out_shape=[SDS((S, CH), F32), SDS((40, CH), F32)],
        scratch_shapes=[pltpu.VMEM((TM + H, CH), F32)])(du3, u1, z, z, lnw, lnb)


def _conv_bwd_b(dz, du1, z, cw):
    S = z.shape[0]
    nt = S // TM
    H = 32

    def body(dz_in, du1_ref, zv, zg, cw_ref, dz_ref, ybuf, dval, dgate):
        del dz_in
        i, p = pl.program_id(0), pl.program_id(1)

        @pl.when(p == 0)
        def _():
            @pl.when(i == 0)
            def _():
                ybuf[TM:TM + H, :] = jnp.zeros((H, CH), F32)

            @pl.when(i > 0)
            def _():
                ybuf[TM:TM + H, :] = ybuf[0:H, :]

            ybuf[0:TM, :] = du1_ref[...]
            acc = jnp.zeros((TM, CH), F32)
            for j in range(CONV_K):
                acc = acc + ybuf[pl.ds(j, TM), :] * cw_ref[CONV_K - 1 - j:CONV_K - j, :]
            val = zv[...].astype(F32)
            sg = _sig(zg[...].astype(F32))
            dval[...] = (acc * sg).astype(BF16)
            dgate[...] = (acc * val * sg * (1.0 - sg)).astype(BF16)
            dz_ref[...] = dval[...]

        @pl.when(p == 1)
        def _():
            dz_ref[...] = dgate[...]

    rev = lambda c: pl.BlockSpec((TM, CH), lambda i, p: (nt - 1 - i, c))
    return pl.pallas_call(
        body, name="conv_bwd_b", grid=(nt, 2),
        in_specs=[pl.BlockSpec(memory_space=pl.ANY), rev(0), rev(O_CV // CH), rev(O_CG // CH), _full((CONV_K, CH))],
        out_specs=pl.BlockSpec((TM, CH), lambda i, p: (nt - 1 - i, O_CV // CH + p)),
        out_shape=SDS(dz.shape, BF16),
        scratch_shapes=[pltpu.VMEM((TM + H, CH), F32), pltpu.VMEM((TM, CH), BF16), pltpu.VMEM((TM, CH), BF16)],
        input_output_aliases={0: 0})(dz, du1, z, z, cw)


def _memkv_fwd(mem, mnw, wkv, xkw):
    M, D = mem.shape

    def body(mem_ref, mnw_ref, w_ref, xkw_ref, mn_ref, kv_ref, mk_ref, mv_ref):
        x = mem_ref[...]
        mn = (x * lax.rsqrt(jnp.mean(x * x, axis=-1, keepdims=True) + RMS_EPS) * mnw_ref[...]).astype(BF16)
        mn_ref[...] = mn
        kv = jnp.dot(mn, w_ref[...], preferred_element_type=F32)
        kv_ref[...] = kv
        for h in range(HPG):
            k = kv[:, h * HEAD:(h + 1) * HEAD]
            mk_ref[:, h * HEAD:(h + 1) * HEAD] = (
                k * lax.rsqrt(jnp.mean(k * k, axis=-1, keepdims=True) + RMS_EPS) * xkw_ref[...]).astype(BF16)
        mv_ref[...] = kv[:, CH:2 * CH].astype(BF16)

    return pl.pallas_call(
        body, name="memkv_fwd",
        out_shape=[SDS((M, D), BF16), SDS((M, 2 * CH), F32), SDS((M, CH), BF16), SDS((M, CH), BF16)])(mem, mnw, wkv, xkw)


def _cross_q(zx, xqw, h):
    x = zx[:, h * HEAD:(h + 1) * HEAD].astype(F32)
    r = lax.rsqrt(jnp.mean(x * x, axis=-1, keepdims=True) + RMS_EPS)
    xhat = x * r
    return xhat, r, xhat * xqw


def _cross_fwd(z, xqw, mk, mv):
    S = z.shape[0]
    M = mk.shape[0]

    def body(zx, xqw_ref, mk_ref, mv_ref, o_ref):
        for h in range(HPG):
            sl = slice(h * HEAD, (h + 1) * HEAD)
            _, _, q = _cross_q(zx, xqw_ref[...], h)
            s = lax.dot_general(q.astype(BF16), mk_ref[:, sl], NT_DIMS, preferred_element_type=F32) * SCALE
            e = jnp.exp(s - jnp.max(s, axis=-1, keepdims=True))
            p = e / jnp.sum(e, axis=-1, keepdims=True)
            o_ref[:, sl] = jnp.dot(p.astype(BF16), mv_ref[:, sl], preferred_element_type=F32).astype(BF16)

    return pl.pallas_call(
        body, name="cross_fwd", grid=(S // TM,),
        in_specs=[pl.BlockSpec((TM, CH), lambda i: (i, O_XQ // CH)), _full((1, HEAD)), _full((M, CH)), _full((M, CH))],
        out_specs=pl.BlockSpec((TM, CH), lambda i: (i, 0)), out_shape=SDS((S, CH), BF16))(z, xqw, mk, mv)


def _cross_bwd(dz, doc, z, xqw, mk, mv):
    S = z.shape[0]
    M = mk.shape[0]

    def body(dz_in, do_ref, zx, xqw_ref, mk_ref, mv_ref, dz_ref, dmk_ref, dmv_ref, dxw_ref):
        del dz_in

        @pl.when(pl.program_id(0) == 0)
        def _():
            dmk_ref[...] = jnp.zeros_like(dmk_ref)
            dmv_ref[...] = jnp.zeros_like(dmv_ref)
            dxw_ref[...] = jnp.zeros_like(dxw_ref)

        for h in range(HPG):
            sl = slice(h * HEAD, (h + 1) * HEAD)
            xhat, r, q = _cross_q(zx, xqw_ref[...], h)
            qb = q.astype(BF16)
            s = lax.dot_general(qb, mk_ref[:, sl], NT_DIMS, preferred_element_type=F32) * SCALE
            e = jnp.exp(s - jnp.max(s, axis=-1, keepdims=True))
            p = e / jnp.sum(e, axis=-1, keepdims=True)
            do = do_ref[:, sl].astype(BF16)
            dp = lax.dot_general(do, mv_ref[:, sl], NT_DIMS, preferred_element_type=F32)
            ds = (p * (dp - jnp.sum(p * dp, axis=-1, keepdims=True)) * SCALE).astype(BF16)
            dq = jnp.dot(ds, mk_ref[:, sl], preferred_element_type=F32)
            dmk_ref[:, sl] += lax.dot_general(ds, qb, TN_DIMS, preferred_element_type=F32)
            dmv_ref[:, sl] += lax.dot_general(p.astype(BF16), do, TN_DIMS, preferred_element_type=F32)
            dyw = dq * xqw_ref[...]
            dz_ref[:, sl] = (r * (dyw - xhat * jnp.mean(dyw * xhat, axis=-1, keepdims=True))).astype(BF16)
            dxw_ref[...] += jnp.sum(dq * xhat, axis=0, keepdims=True)

    return pl.pallas_call(
        body, name="cross_bwd", grid=(S // TM,),
        in_specs=[pl.BlockSpec(memory_space=pl.ANY), pl.BlockSpec((TM, CH), lambda i: (i, 0)),
                  pl.BlockSpec((TM, CH), lambda i: (i, O_XQ // CH)), _full((1, HEAD)), _full((M, CH)), _full((M, CH))],
        out_specs=[pl.BlockSpec((TM, CH), lambda i: (i, O_XQ // CH)), _full((M, CH)), _full((M, CH)), _full((1, HEAD))],
        out_shape=[SDS(dz.shape, BF16), SDS((M, CH), F32), SDS((M, CH), F32), SDS((1, HEAD), F32)],
        input_output_aliases={0: 0})(dz, doc, z, xqw, mk, mv)


def _memkv_bwd(dmk, dmv, kv, mem, mn, mnw, wkv, xkw):
    M, D = mem.shape

    def body(dmk_ref, dmv_ref, kv_ref, mem_ref, mn_ref, mnw_ref, w_ref, xkw_ref, dw_ref, dxk_ref, dmn_ref, dkv):
        dxk = jnp.zeros((1, HEAD), F32)
        for h in range(HPG):
            sl = slice(h * HEAD, (h + 1) * HEAD)
            k = kv_ref[:, sl]
            r = lax.rsqrt(jnp.mean(k * k, axis=-1, keepdims=True) + RMS_EPS)
            khat = k * r
            dy = dmk_ref[:, sl]
            dyw = dy * xkw_ref[...]
            dkv[:, sl] = (r * (dyw - khat * jnp.mean(dyw * khat, axis=-1, keepdims=True))).astype(BF16)
            dxk = dxk + jnp.sum(dy * khat, axis=0, keepdims=True)
        dxk_ref[...] = dxk
        dkv[:, CH:2 * CH] = dmv_ref[...].astype(BF16)
        dw_ref[...] = lax.dot_general(mn_ref[...], dkv[...], TN_DIMS, preferred_element_type=F32)
        dn = lax.dot_general(dkv[...], w_ref[...], NT_DIMS, preferred_element_type=F32)
        x = mem_ref[...]
        r = lax.rsqrt(jnp.mean(x * x, axis=-1, keepdims=True) + RMS_EPS)
        dmn_ref[...] = jnp.sum(dn * x * r, axis=0, keepdims=True)

    return pl.pallas_call(
        body, name="memkv_bwd",
        out_shape=[SDS((D, 2 * CH), F32), SDS((1, HEAD), F32), SDS((1, D), F32)],
        scratch_shapes=[pltpu.VMEM((M, 2 * CH), BF16)])(dmk, dmv, kv, mem, mn, mnw, wkv, xkw)


def _branch_proj(a_ref, w_ref, y_ref):
    G, _, n = w_ref.shape
    a = a_ref[...]
    for g in range(G):
        y_ref[:, g * n:(g + 1) * n] = jnp.dot(a, w_ref[g], preferred_element_type=F32)


def _gates(zg_ref, bg_ref, k, D):
    return _sig(zg_ref[:, k * D:(k + 1) * D].astype(F32) + bg_ref[:, k * D:(k + 1) * D])


def _outproj_fwd(x, z, bg, attn, u3, oc, wao, wco, wxo, wout, fnw):
    S, D = x.shape
    tm = 256

    def body(x_ref, zg_ref, bg_ref, a_ref, u_ref, c_ref, wa, wc, wx, wo, fnw_ref, h1_ref, hn_ref, ya, yc, yx):
        _branch_proj(a_ref, wa, ya)
        _branch_proj(u_ref, wc, yc)
        _branch_proj(c_ref, wx, yx)
        merged = _gates(zg_ref, bg_ref, 0, D) * ya[...] + _gates(zg_ref, bg_ref, 1, D) * yc[...] + _gates(zg_ref, bg_ref, 2, D) * yx[...]
        h1 = x_ref[...] + jnp.dot(merged.astype(BF16), wo[...], preferred_element_type=F32)
        h1_ref[...] = h1
        hn_ref[...] = (h1 * lax.rsqrt(jnp.mean(h1 * h1, axis=-1, keepdims=True) + RMS_EPS) * fnw_ref[...]).astype(BF16)

    row = lambda w: pl.BlockSpec((tm, w), lambda i: (i, 0))
    return pl.pallas_call(
        body, name="outproj_fwd", grid=(S // tm,),
        in_specs=[row(D), pl.BlockSpec((tm, 3 * D), lambda i: (i, O_G // (3 * D))), _full((1, 3 * D)), row(CH), row(CH), row(CH),
                  _full(wao.shape), _full(wco.shape), _full(wxo.shape), _full((D, D)), _full((1, D))],
        out_specs=[row(D), row(D)], out_shape=[SDS((S, D), F32), SDS((S, D), BF16)],
        scratch_shapes=[pltpu.VMEM((tm, D), F32)] * 3)(x, z, bg, attn, u3, oc, wao, wco, wxo, wout, fnw)


def _outproj_bwd(dh1, z, bg, attn, u3, oc, wao, wco, wxo, wout, n_in):
    S, D = dh1.shape
    tm = 256

    def body(dh_ref, zg_ref, bg_ref, a_ref, u_ref, c_ref, wa, wc, wx, wo,
             dz_ref, da_ref, du_ref, dc_ref, dya_ref, dyc_ref, dyx_ref, mg_ref, dbg_ref, ya, yc, yx):
        @pl.when(pl.program_id(0) == 0)
        def _():
            dbg_ref[...] = jnp.zeros_like(dbg_ref)

        _branch_proj(a_ref, wa, ya)
        _branch_proj(u_ref, wc, yc)
        _branch_proj(c_ref, wx, yx)
        dm = lax.dot_general(dh_ref[...].astype(BF16), wo[...], NT_DIMS, preferred_element_type=F32)
        merged = jnp.zeros((tm, D), F32)
        for k, (y, dy_ref, w_ref, db_ref) in enumerate(((ya, dya_ref, wa, da_ref), (yc, dyc_ref, wc, du_ref), (yx, dyx_ref, wx, dc_ref))):
            gk = _gates(zg_ref, bg_ref, k, D)
            yk = y[...]
            merged = merged + gk * yk
            dzg = dm * yk * gk * (1.0 - gk)
            dz_ref[:, k * D:(k + 1) * D] = dzg.astype(BF16)
            dbg_ref[:, k * D:(k + 1) * D] += jnp.sum(dzg, axis=0, keepdims=True)
            dyk = (dm * gk).astype(BF16)
            dy_ref[...] = dyk
            G, _, n = w_ref.shape
            acc = jnp.zeros((tm, CH), F32)
            for g in range(G):
                acc = acc + lax.dot_general(dyk[:, g * n:(g + 1) * n], w_ref[g], NT_DIMS, preferred_element_type=F32)
            db_ref[...] = acc
        mg_ref[...] = merged.astype(BF16)

    row = lambda w: pl.BlockSpec((tm, w), lambda i: (i, 0))
    return pl.pallas_call(
        body, name="outproj_bwd", grid=(S // tm,),
        in_specs=[row(D), pl.BlockSpec((tm, 3 * D), lambda i: (i, O_G // (3 * D))), _full((1, 3 * D)), row(CH), row(CH), row(CH),
                  _full(wao.shape), _full(wco.shape), _full(wxo.shape), _full((D, D))],
        out_specs=[pl.BlockSpec((tm, 3 * D), lambda i: (i, O_G // (3 * D))), row(CH), row(CH), row(CH),
                   row(D), row(D), row(D), row(D), _full((1, 3 * D))],
        out_shape=[SDS((S, n_in), BF16)] + [SDS((S, CH), F32)] * 3 + [SDS((S, D), BF16)] * 4 + [SDS((1, 3 * D), F32)],
        scratch_shapes=[pltpu.VMEM((tm, D), F32)] * 3)(dh1, z, bg, attn, u3, oc, wao, wco, wxo, wout)


FFN_TC = 256
FFN_H = 8


def _ffn_conv(buf, w_ref, b_ref, tm):
    acc = jnp.broadcast_to(b_ref[...], (tm, FFN_TC))
    for k in range(FFN_K):
        acc = acc + buf[pl.ds(FFN_H - (FFN_K - 1) + k, tm), :] * w_ref[k:k + 1, :]
    return acc


def _ffn_act_fwd(up, cw, cb):
    S, F2 = up.shape
    nj = F2 // 2 // FFN_TC
    tm = 1024 if S % 1024 == 0 else TM

    def body(ua, ug, wa, wg, ba, bgt, o_ref, abuf, gbuf):
        i = pl.program_id(1)

        @pl.when(i == 0)
        def _():
            abuf[0:FFN_H, :] = jnp.zeros((FFN_H, FFN_TC), F32)
            gbuf[0:FFN_H, :] = jnp.zeros((FFN_H, FFN_TC), F32)

        @pl.when(i > 0)
        def _():
            abuf[0:FFN_H, :] = abuf[tm:tm + FFN_H, :]
            gbuf[0:FFN_H, :] = gbuf[tm:tm + FFN_H, :]

        abuf[FFN_H:FFN_H + tm, :] = ua[...].astype(F32)
        gbuf[FFN_H:FFN_H + tm, :] = ug[...].astype(F32)
        a = _ffn_conv(abuf, wa, ba, tm)
        gt = _ffn_conv(gbuf, wg, bgt, tm)
        o_ref[...] = (gt * _sig(gt) * a).astype(BF16)

    return pl.pallas_call(
        body, name="ffn_act_fwd", grid=(nj, S // tm),
        in_specs=[pl.BlockSpec((tm, FFN_TC), lambda j, i: (i, j)), pl.BlockSpec((tm, FFN_TC), lambda j, i: (i, nj + j)),
                  pl.BlockSpec((FFN_K, FFN_TC), lambda j, i: (0, j)), pl.BlockSpec((FFN_K, FFN_TC), lambda j, i: (0, nj + j)),
                  pl.BlockSpec((1, FFN_TC), lambda j, i: (0, j)), pl.BlockSpec((1, FFN_TC), lambda j, i: (0, nj + j))],
        out_specs=pl.BlockSpec((tm, FFN_TC), lambda j, i: (i, j)), out_shape=SDS((S, F2 // 2), BF16),
        scratch_shapes=[pltpu.VMEM((tm + FFN_H, FFN_TC), F32)] * 2)(up, up, cw, cw, cb, cb)


def _ffn_down_loss(act, wdown, h1, target):
    S, D = h1.shape
    F = act.shape[1]

    def body(a_ref, w_ref, h_ref, t_ref, dy_ref, loss_ref):
        @pl.when(pl.program_id(0) == 0)
        def _():
            loss_ref[...] = jnp.zeros_like(loss_ref)

        err = h_ref[...] + jnp.dot(a_ref[...], w_ref[...], preferred_element_type=F32) - t_ref[...]
        dy_ref[...] = err * (1.0 / D)
        loss_ref[...] += 0.5 * jnp.sum(jnp.mean(err * err, axis=-1, keepdims=True))

    row = lambda w: pl.BlockSpec((TM, w), lambda i: (i, 0))
    return pl.pallas_call(
        body, name="ffn_down_loss", grid=(S // TM,),
        in_specs=[row(F), _full((F, D)), row(D), row(D)],
        out_specs=[row(D), _full((8, 128))], out_shape=[SDS((S, D), F32), SDS((8, 128), F32)])(act, wdown, h1, target)


def _ffn_bwd_a(dy, wdown, up, cw, cb):
    S, D = dy.shape
    F2 = up.shape[1]
    F = F2 // 2
    nj = F // FFN_TC
    tm = TM

    def body(dy_ref, wd_ref, ua, ug, wa, wg, ba, bgt, da_ref, dg_ref, acca_ref, accg_ref, abuf, gbuf, hala, halg):
        i, j = pl.program_id(0), pl.program_id(1)

        @pl.when((i == 0) & (j == 0))
        def _():
            acca_ref[...] = jnp.zeros_like(acca_ref)
            accg_ref[...] = jnp.zeros_like(accg_ref)

        @pl.when(i == 0)
        def _():
            abuf[0:FFN_H, :] = jnp.zeros((FFN_H, FFN_TC), F32)
            gbuf[0:FFN_H, :] = jnp.zeros((FFN_H, FFN_TC), F32)

        @pl.when(i > 0)
        def _():
            abuf[0:FFN_H, :] = hala[j]
            gbuf[0:FFN_H, :] = halg[j]

        abuf[FFN_H:FFN_H + tm, :] = ua[...].astype(F32)
        gbuf[FFN_H:FFN_H + tm, :] = ug[...].astype(F32)
        hala[j] = abuf[tm:tm + FFN_H, :]
        halg[j] = gbuf[tm:tm + FFN_H, :]
        a = _ffn_conv(abuf, wa, ba, tm)
        gt = _ffn_conv(gbuf, wg, bgt, tm)
        dact = lax.dot_general(dy_ref[...].astype(BF16), wd_ref[...], NT_DIMS, preferred_element_type=F32)
        sg = _sig(gt)
        dac = dact * (gt * sg)
        dgc = dact * a * (sg * (1.0 + gt * (1.0 - sg)))
        da_ref[...] = dac.astype(BF16)
        dg_ref[...] = dgc.astype(BF16)
        for k in range(FFN_K):
            sh = pl.ds(FFN_H - (FFN_K - 1) + k, tm)
            acca_ref[j, k:k + 1, :] += jnp.sum(dac * abuf[sh, :], axis=0, keepdims=True)
            accg_ref[j, k:k + 1, :] += jnp.sum(dgc * gbuf[sh, :], axis=0, keepdims=True)
        acca_ref[j, FFN_K:FFN_K + 1, :] += jnp.sum(dac, axis=0, keepdims=True)
        accg_ref[j, FFN_K:FFN_K + 1, :] += jnp.sum(dgc, axis=0, keepdims=True)

    return pl.pallas_call(
        body, name="ffn_bwd_a", grid=(S // tm, nj),
        in_specs=[pl.BlockSpec((tm, D), lambda i, j: (i, 0)), pl.BlockSpec((FFN_TC, D), lambda i, j: (j, 0)),
                  pl.BlockSpec((tm, FFN_TC), lambda i, j: (i, j)), pl.BlockSpec((tm, FFN_TC), lambda i, j: (i, nj + j)),
                  pl.BlockSpec((FFN_K, FFN_TC), lambda i, j: (0, j)), pl.BlockSpec((FFN_K, FFN_TC), lambda i, j: (0, nj + j)),
                  pl.BlockSpec((1, FFN_TC), lambda i, j: (0, j)), pl.BlockSpec((1, FFN_TC), lambda i, j: (0, nj + j))],
        out_specs=[pl.BlockSpec((tm, FFN_TC), lambda i, j: (i, j))] * 2 + [_full((nj, 8, FFN_TC))] * 2,
        out_shape=[SDS((S, F), BF16)] * 2 + [SDS((nj, 8, FFN_TC), F32)] * 2,
        scratch_shapes=[pltpu.VMEM((tm + FFN_H, FFN_TC), F32)] * 2 + [pltpu.VMEM((nj, FFN_H, FFN_TC), F32)] * 2,
    )(dy, wdown, up, up, cw, cw, cb, cb)


def _ffn_bwd_b(dca, dcg, cw):
    S, F = dca.shape
    nj = F // FFN_TC
    tm = 1024 if S % 1024 == 0 else TM
    nt = S // tm

    def body(a_ref, g_ref, w_ref, o_ref, ybuf):
        j, i = pl.program_id(0), pl.program_id(1)

        @pl.when(i == 0)
        def _():
            ybuf[tm:tm + FFN_H, :] = jnp.zeros((FFN_H, FFN_TC), F32)

        @pl.when(i > 0)
        def _():
            ybuf[tm:tm + FFN_H, :] = ybuf[0:FFN_H, :]

        ybuf[0:tm, :] = jnp.where(j < nj, a_ref[...], g_ref[...]).astype(F32)
        acc = jnp.zeros((tm, FFN_TC), F32)
        for s in range(FFN_K):
            acc = acc + ybuf[pl.ds(s, tm), :] * w_ref[FFN_K - 1 - s:FFN_K - s, :]
        o_ref[...] = acc.astype(BF16)

    return pl.pallas_call(
        body, name="ffn_bwd_b", grid=(2 * nj, nt),
        in_specs=[pl.BlockSpec((tm, FFN_TC), lambda j, i: (nt - 1 - i, jnp.minimum(j, nj - 1))),
                  pl.BlockSpec((tm, FFN_TC), lambda j, i: (nt - 1 - i, jnp.maximum(j - nj, 0))),
                  pl.BlockSpec((FFN_K, FFN_TC), lambda j, i: (0, j))],
        out_specs=pl.BlockSpec((tm, FFN_TC), lambda j, i: (nt - 1 - i, j)), out_shape=SDS((S, 2 * F), BF16),
        scratch_shapes=[pltpu.VMEM((tm + FFN_H, FFN_TC), F32)])(dca, dcg, cw)


def _adamw(w, g, m, v, name):
    R, C = w.shape
    tr = _row_tile(R, max(8, (2 ** 19) // (4 * C) // 8 * 8))

    def body(w_ref, g_ref, m_ref, v_ref, d_ref, nm_ref, nv_ref):
        gv = g_ref[...]
        m2 = ADAM_B1 * m_ref[...] + (1.0 - ADAM_B1) * gv
        v2 = ADAM_B2 * v_ref[...] + (1.0 - ADAM_B2) * jnp.square(gv)
        m_hat = m2 / (1.0 - ADAM_B1 ** ADAM_STEP)
        v_hat = v2 / (1.0 - ADAM_B2 ** ADAM_STEP)
        d_ref[...] = -ADAM_LR * (m_hat / (jnp.sqrt(v_hat) + ADAM_EPS) + ADAM_WD * w_ref[...])
        nm_ref[...] = m2
        nv_ref[...] = v2

    blk = pl.BlockSpec((tr, C), lambda i: (i, 0))
    return pl.pallas_call(
        body, name=name, grid=(R // tr,), in_specs=[blk] * 4, out_specs=[blk] * 3,
        out_shape=[SDS((R, C), F32)] * 3)(w, g, m, v)


HBM_SPEC = pl.BlockSpec(memory_space=pltpu.HBM)
SEM_SPEC = pl.BlockSpec(memory_space=pltpu.SEMAPHORE)
DATAFLOW_EFFECT = pltpu.SideEffectType.DATAFLOW_SIDE_EFFECTING
LOCAL_SPLIT = 8


def _position():
    return lax.axis_index("x"), lax.axis_index("y"), lax.axis_index("c")


def _local_copies(src, dst, sems, base):
    rows = src.shape[1]
    parts = LOCAL_SPLIT // 2 if rows % (16 * LOCAL_SPLIT // 2) == 0 else 1
    step = rows // parts
    cps = []
    for h in range(2):
        for k in range(parts):
            cp = pltpu.make_async_copy(src.at[h, pl.ds(k * step, step)], dst.at[h, pl.ds(k * step, step)], sems.at[base + h * parts + k])
            cp.start()
            cps.append(cp)
    return cps


def _other_chips(x, y):
    return [(1 - x, y), (x, 1 - y), (1 - x, 1 - y)]


def _all_gather_xy(arrs, name):
    n = len(arrs)
    hbm = pl.BlockSpec(memory_space=pl.ANY)

    def body(*refs):
        ins, outs = refs[:n], refs[n:2 * n]
        send_sems, recv_sems, loc_sems = refs[2 * n:]
        x, y, c = _position()
        me = 2 * x + y
        chips = _other_chips(x, y)

        def rcopy(i, k, src, dst, to):
            return pltpu.make_async_remote_copy(src_ref=src, dst_ref=dst, send_sem=send_sems.at[i, k], recv_sem=recv_sems.at[i, k],
                                                device_id=to, device_id_type=MESH)

        started = []
        for i in range(n):
            started += _local_copies(ins[i], outs[i].at[me], loc_sems, LOCAL_SPLIT * i)
        sends = []
        for i in range(n):
            for j, (px, py) in enumerate(chips):
                cp = rcopy(i, j, ins[i].at[c], outs[i].at[me, c], (px, py, c))
                cp.start()
                sends.append(cp)
        for i in range(n):
            for j, (px, py) in enumerate(chips):
                got = outs[i].at[2 * px + py, c]
                rcopy(i, j, ins[i].at[c], got, (x, y, c)).wait_recv()
                fwd = rcopy(i, 3 + j, got, got, (x, y, 1 - c))
                fwd.start()
                sends.append(fwd)
        for i in range(n):
            for j, (px, py) in enumerate(chips):
                theirs = outs[i].at[2 * px + py, 1 - c]
                rcopy(i, 3 + j, theirs, theirs, (x, y, c)).wait_recv()
        for cp in sends:
            cp.wait_send()
        for own in started:
            own.wait()

    return pl.pallas_call(
        body, name=name, in_specs=[hbm] * n, out_specs=[hbm] * n,
        out_shape=[SDS((4,) + a.shape, a.dtype) for a in arrs],
        scratch_shapes=[pltpu.SemaphoreType.DMA((n, 6)), pltpu.SemaphoreType.DMA((n, 6)),
                        pltpu.SemaphoreType.DMA((LOCAL_SPLIT * n,))])(*arrs)


def _ag_ici_start(arrs, name):
    n = len(arrs)

    def body(*refs):
        ins, lands = refs[:n], refs[n:2 * n]
        send_sems, recv_sems = refs[2 * n:2 * n + 2]
        token = refs[-1]
        x, y, c = _position()
        for i in range(n):
            for j, (px, py) in enumerate(_other_chips(x, y)):
                pltpu.make_async_remote_copy(src_ref=ins[i].at[c], dst_ref=lands[i].at[2 * x + y, c], send_sem=send_sems.at[3 * i + j],
                                             recv_sem=recv_sems.at[3 * i + j], device_id=(px, py, c), device_id_type=MESH).start()
        token[...] = jnp.zeros_like(token)

    lands = [lax.empty((4,) + a.shape, a.dtype) for a in arrs]
    res = pl.pallas_call(
        body, name=name,
        out_shape=[pltpu.SemaphoreType.DMA((3 * n,)), pltpu.SemaphoreType.DMA((3 * n,))]
        + [pltpu.HBM(a.shape, a.dtype) for a in arrs] + [pltpu.HBM(l.shape, l.dtype) for l in lands] + [SDS((8, 128), F32)],
        in_specs=[HBM_SPEC] * (2 * n), out_specs=[SEM_SPEC, SEM_SPEC] + [HBM_SPEC] * (2 * n) + [pl.BlockSpec(memory_space=pltpu.VMEM)],
        input_output_aliases={i: 2 + i for i in range(2 * n)},
        compiler_params=pltpu.CompilerParams(has_side_effects=DATAFLOW_EFFECT),
    )(*[pltpu.with_memory_space_constraint(a, pltpu.HBM) for a in list(arrs) + lands])
    return res[0], res[1], list(res[2:2 + n]), list(res[2 + n:2 + 2 * n]), res[-1]


def _ag_ici_wait(send_sems, recv_sems, ins, lands, after, name):
    n = len(ins)

    def body(*refs):
        ins_r, lands_r = refs[:n], refs[n:2 * n]
        send_r, recv_r = refs[2 * n:2 * n + 2]
        x, y, c = _position()
        for i in range(n):
            for j, (px, py) in enumerate(_other_chips(x, y)):
                cp = pltpu.make_async_remote_copy(src_ref=ins_r[i].at[c], dst_ref=lands_r[i].at[2 * px + py, c], send_sem=send_r.at[3 * i + j],
                                                  recv_sem=recv_r.at[3 * i + j], device_id=(px, py, c), device_id_type=MESH)
                cp.wait_send()
                cp.wait_recv()

    res = pl.pallas_call(
        body, name=name,
        out_shape=[pltpu.HBM(a.shape, a.dtype) for a in list(ins) + list(lands)],
        in_specs=[HBM_SPEC] * (2 * n) + [SEM_SPEC, SEM_SPEC, pl.BlockSpec(memory_space=pl.ANY)], out_specs=[HBM_SPEC] * (2 * n),
        input_output_aliases={i: i for i in range(2 * n)},
        compiler_params=pltpu.CompilerParams(has_side_effects=DATAFLOW_EFFECT),
    )(*ins, *lands, send_sems, recv_sems, after)
    return list(res[:n]), list(res[n:])


def _ag_finish(arrs, lands, name):
    n = len(arrs)
    hbm = pl.BlockSpec(memory_space=pl.ANY)

    def body(*refs):
        ins, landed, outs = refs[:n], refs[n:2 * n], refs[2 * n:3 * n]
        send_sems, recv_sems, loc_sems = refs[3 * n:]
        x, y, c = _position()
        chips = _other_chips(x, y)
        started, sends = [], []
        for i in range(n):
            started += _local_copies(ins[i], outs[i].at[2 * x + y], loc_sems, LOCAL_SPLIT * i)
        for i in range(n):
            for j, (px, py) in enumerate(chips):
                fwd = pltpu.make_async_remote_copy(src_ref=landed[i].at[2 * px + py, c], dst_ref=outs[i].at[2 * px + py, c],
                                                   send_sem=send_sems.at[i, j], recv_sem=recv_sems.at[i, j],
                                                   device_id=(x, y, 1 - c), device_id_type=MESH)
                fwd.start()
                sends.append(fwd)
        for i in range(n):
            for j, (px, py) in enumerate(chips):
                theirs = outs[i].at[2 * px + py, 1 - c]
                pltpu.make_async_remote_copy(src_ref=theirs, dst_ref=theirs, send_sem=send_sems.at[i, j], recv_sem=recv_sems.at[i, j],
                                             device_id=(x, y, c), device_id_type=MESH).wait_recv()
        for cp in sends:
            cp.wait_send()
        for own in started:
            own.wait()

    return pl.pallas_call(
        body, name=name, in_specs=[hbm] * (2 * n), out_specs=[hbm] * n,
        out_shape=[SDS(l.shape, l.dtype) for l in lands],
        input_output_aliases={n + i: i for i in range(n)},
        scratch_shapes=[pltpu.SemaphoreType.DMA((n, 3)), pltpu.SemaphoreType.DMA((n, 3)),
                        pltpu.SemaphoreType.DMA((LOCAL_SPLIT * n,))])(*arrs, *lands)


def _swap_halves(g, name):
    hbm = pl.BlockSpec(memory_space=pl.ANY)

    def body(g_ref, r_ref, send_sem, recv_sem):
        x, y, c = _position()
        cp = pltpu.make_async_remote_copy(src_ref=g_ref.at[1 - c], dst_ref=r_ref, send_sem=send_sem, recv_sem=recv_sem,
                                          device_id=(x, y, 1 - c), device_id_type=MESH)
        cp.start()
        cp.wait()

    return pl.pallas_call(
        body, name=name, in_specs=[hbm], out_specs=hbm, out_shape=SDS(g.shape[1:], g.dtype),
        scratch_shapes=[pltpu.SemaphoreType.DMA, pltpu.SemaphoreType.DMA])(g)


def _exchange_start(s1, name):
    L = s1.shape[1]

    def body(s_ref, land_ref, send_sems, recv_sems, s_thru, land_thru, token):
        del s_thru, land_thru
        x, y, c = _position()
        for j, (px, py) in enumerate(_other_chips(x, y)):
            pltpu.make_async_remote_copy(src_ref=s_ref.at[2 * px + py], dst_ref=land_ref.at[j], send_sem=send_sems.at[j],
                                         recv_sem=recv_sems.at[j], device_id=(px, py, c), device_id_type=MESH).start()
        token[...] = jnp.zeros_like(token)

    return pl.pallas_call(
        body, name=name,
        out_shape=[pltpu.SemaphoreType.DMA((3,)), pltpu.SemaphoreType.DMA((3,)), pltpu.HBM(s1.shape, F32),
                   pltpu.HBM((3, L, 128), F32), SDS((8, 128), F32)],
        in_specs=[HBM_SPEC, HBM_SPEC], out_specs=[SEM_SPEC, SEM_SPEC, HBM_SPEC, HBM_SPEC, pl.BlockSpec(memory_space=pltpu.VMEM)],
        input_output_aliases={0: 2, 1: 3},
        compiler_params=pltpu.CompilerParams(has_side_effects=DATAFLOW_EFFECT),
    )(pltpu.with_memory_space_constraint(s1, pltpu.HBM), pltpu.with_memory_space_constraint(lax.empty((3, L, 128), F32), pltpu.HBM))


def _exchange_wait(send_sems, recv_sems, s1, land, after, name):
    def body(s_ref, land_ref, send_r, recv_r, after_ref, s_out, land_out):
        del after_ref, s_out, land_out
        x, y, c = _position()
        for j, (px, py) in enumerate(_other_chips(x, y)):
            cp = pltpu.make_async_remote_copy(src_ref=s_ref.at[2 * px + py], dst_ref=land_ref.at[j], send_sem=send_r.at[j],
                                              recv_sem=recv_r.at[j], device_id=(px, py, c), device_id_type=MESH)
            cp.wait_send()
            cp.wait_recv()

    return pl.pallas_call(
        body, name=name, out_shape=[pltpu.HBM(s1.shape, F32), pltpu.HBM(land.shape, F32)],
        in_specs=[HBM_SPEC, HBM_SPEC, SEM_SPEC, SEM_SPEC, pl.BlockSpec(memory_space=pl.ANY)], out_specs=[HBM_SPEC, HBM_SPEC],
        input_output_aliases={0: 0, 1: 1},
        compiler_params=pltpu.CompilerParams(has_side_effects=DATAFLOW_EFFECT),
    )(s1, land, send_sems, recv_sems, after)


def _join_halves(f2, name):
    hbm = pl.BlockSpec(memory_space=pl.ANY)

    def body(f_ref, o_ref, send_sem, recv_sem):
        x, y, c = _position()
        cp = pltpu.make_async_remote_copy(src_ref=f_ref.at[c], dst_ref=o_ref.at[c], send_sem=send_sem, recv_sem=recv_sem,
                                          device_id=(x, y, 1 - c), device_id_type=MESH)
        cp.start()
        pltpu.make_async_remote_copy(src_ref=f_ref.at[1 - c], dst_ref=o_ref.at[1 - c], send_sem=send_sem, recv_sem=recv_sem,
                                     device_id=(x, y, 1 - c), device_id_type=MESH).wait_recv()
        cp.wait_send()

    return pl.pallas_call(
        body, name=name, in_specs=[hbm], out_specs=hbm, out_shape=SDS(f2.shape, f2.dtype), input_output_aliases={0: 0},
        scratch_shapes=[pltpu.SemaphoreType.DMA, pltpu.SemaphoreType.DMA])(f2)


def _add_pair(g, r1, c, name):
    R = r1.shape[0]
    tr = _row_tile(R, 2048)

    def body(c_ref, a_ref, b_ref, o_ref):
        del c_ref
        o_ref[...] = a_ref[...] + b_ref[...]

    return pl.pallas_call(
        body, name=name,
        grid_spec=pltpu.PrefetchScalarGridSpec(
            num_scalar_prefetch=1, grid=(R // tr,),
            in_specs=[pl.BlockSpec((None, tr, 128), lambda i, cr: (cr[0], i, 0)), pl.BlockSpec((tr, 128), lambda i, cr: (i, 0))],
            out_specs=pl.BlockSpec((tr, 128), lambda i, cr: (i, 0))),
        out_shape=SDS((R, 128), F32))(c, g, r1)


def _add_four(s1, r2, me_c, name):
    L = s1.shape[1]
    tr = _row_tile(L, 2048)

    def body(m_ref, a_ref, b_ref, o_ref):
        del m_ref
        o_ref[...] = ((a_ref[...] + b_ref[0]) + b_ref[1]) + b_ref[2]

    return pl.pallas_call(
        body, name=name,
        grid_spec=pltpu.PrefetchScalarGridSpec(
            num_scalar_prefetch=1, grid=(L // tr,),
            in_specs=[pl.BlockSpec((None, tr, 128), lambda i, mr: (mr[0], i, 0)), pl.BlockSpec((3, tr, 128), lambda i, mr: (0, i, 0))],
            out_specs=pl.BlockSpec((None, tr, 128), lambda i, mr: (mr[1], i, 0))),
        out_shape=SDS((2, L, 128), F32))(me_c, s1, r2)


def _all_reduce_small(v):
    R = v.shape[0]

    def body(v_ref, o_ref, buf, send_sems, recv_sems):
        x, y, c = _position()
        me = 4 * x + 2 * y + c
        buf[me] = v_ref[...]
        cps = []
        for k in range(1, 8):
            to = (1 - x if k & 4 else x, 1 - y if k & 2 else y, 1 - c if k & 1 else c)
            cp = pltpu.make_async_remote_copy(src_ref=buf.at[me], dst_ref=buf.at[me], send_sem=send_sems.at[k - 1],
                                              recv_sem=recv_sems.at[k - 1], device_id=to, device_id_type=MESH)
            cp.start()
            cps.append(cp)
        for cp in cps:
            cp.wait_send()
        for k in range(1, 8):
            src = 4 * (1 - x if k & 4 else x) + 2 * (1 - y if k & 2 else y) + (1 - c if k & 1 else c)
            pltpu.make_async_remote_copy(src_ref=buf.at[src], dst_ref=buf.at[src], send_sem=send_sems.at[k - 1],
                                         recv_sem=recv_sems.at[k - 1], device_id=(x, y, c), device_id_type=MESH).wait_recv()
        acc = buf[0]
        for k in range(1, 8):
            acc = acc + buf[k]
        o_ref[...] = acc

    vm = pl.BlockSpec(memory_space=pltpu.VMEM)
    return pl.pallas_call(
        body, name="all_reduce_small", in_specs=[vm], out_specs=vm, out_shape=SDS((R, 128), F32),
        scratch_shapes=[pltpu.VMEM((8, R, 128), F32), pltpu.SemaphoreType.DMA((7,)), pltpu.SemaphoreType.DMA((7,))])(v)


def _reduce_begin(grads, tag):
    _, _, c = _position()
    sizes = [int(np.prod(g.shape[1:])) for g in grads]
    total = sum(sizes)
    padded = -(-total // 2048) * 2048
    pieces = [g.reshape(4, -1) for g in grads]
    if padded > total:
        pieces.append(jnp.zeros((4, padded - total), F32))
    L = padded // 256
    g2 = jnp.transpose(jnp.concatenate(pieces, axis=1).reshape(4, 2, L, 128), (1, 0, 2, 3)).reshape(2, 4 * L, 128)
    s1 = _add_pair(g2, _swap_halves(g2, "rs_swap_" + tag), jnp.reshape(c, (1,)).astype(jnp.int32), "rs_add_pair_" + tag)
    send_sems, recv_sems, s1, land, token = _exchange_start(s1.reshape(4, L, 128), "rs_exchange_start_" + tag)
    return (send_sems, recv_sems, s1, land, sizes), token


def _reduce_end(state, after, tag):
    x, y, c = _position()
    send_sems, recv_sems, s1, land, sizes = state
    s1, land = _exchange_wait(send_sems, recv_sems, s1, land, after, "rs_exchange_wait_" + tag)
    f2 = _add_four(s1, land, jnp.stack([2 * x + y, c]).astype(jnp.int32), "rs_add_four_" + tag)
    flat = _join_halves(f2, "rs_join_" + tag).reshape(-1)
    out, off = [], 0
    for n in sizes:
        out.append(flat[off:off + n])
        off += n
    return out


def _halves(a):
    return a.reshape((2, a.shape[0] // 2) + a.shape[1:])


def _pack_rows(vecs, mult=8 * 128):
    flat = jnp.concatenate([jnp.ravel(v) for v in vecs])
    pad = (-flat.shape[0]) % mult
    return jnp.pad(flat, (0, pad)).reshape(-1, 128)


def _unpack_rows(packed, shapes):
    flat = packed.reshape(-1)
    out, off = [], 0
    for s in shapes:
        n = int(np.prod(s))
        out.append(flat[off:off + n].reshape(s))
        off += n
    return out


def _after(a, token):
    return a + token[0, 0]


def _local_step(x, mem, target, sp, w_in, ex):
    S, D = x.shape
    n_in = 4 * w_in.shape[2]
    bw = {"w_in": w_in}
    band, buckets = _bias_static()
    buckets = jnp.asarray(buckets)

    xn = _rmsnorm(x, _after(sp["attn_norm_w"], ex.start_rest()), "rms_in")
    z = _mm_nn(xn, bw["w_in"], BF16, "in_proj")
    qh, kh, vh = _qkv_prep(z, sp["q_norm_w"], sp["k_norm_w"])
    tab = sp["rel_bias_table"].T.reshape(N_GROUPS, HPG, N_BUCKETS)
    bias = _bias_fwd(jnp.pad(tab, ((0, 0), (0, 8 - HPG), (0, 0))), buckets)
    biasm = jnp.where(jnp.asarray(band)[None, None], bias[:, :HPG].reshape(N_GROUPS, HPG, NQ, 2 * NQ), NEG)
    os_, lses = [], []
    for g, (_, dil) in enumerate(ATTN_GROUPS):
        o_g, lse_g = _attn_fwd(qh, kh, vh, biasm[g], g, dil)
        os_.append(o_g)
        lses.append(lse_g)
    attn = _merge_fwd(os_, lses)
    u1, u3 = _conv_fwd(z, sp["conv_dw_w"], sp["conv_dw_b"], sp["conv_ln_w"], sp["conv_ln_b"])
    bw.update(ex.rest_weights(after=attn))
    F2 = 4 * bw["w_up"].shape[2]
    mn, kv, mk, mv = _memkv_fwd(mem, sp["mem_norm_w"], bw["w_mem_kv"], sp["xk_norm_w"])
    oc = _cross_fwd(z, sp["xq_norm_w"], mk, mv)
    h1, hn = _outproj_fwd(x, z, sp["b_gate"], attn, u3, oc, bw["w_attn_o"], bw["w_conv_o"], bw["w_cross_o"], bw["w_out"],
                          sp["ffn_norm_w"])
    up = _mm_nn(hn, bw["w_up"], BF16, "ffn_up")
    act = _ffn_act_fwd(up, sp["ffn_conv_w"], sp["ffn_conv_b"])
    dy, loss_tile = _ffn_down_loss(act, bw["w_down"], h1, target)

    gs, gb = {}, {}
    gb["w_down"] = _mm_tn(act, dy, 1, "dw_down")[0]
    dca, dcg, acca, accg = _ffn_bwd_a(dy, bw["w_down"], up, sp["ffn_conv_w"], sp["ffn_conv_b"])
    tap = lambda acc: jnp.transpose(acc, (1, 0, 2)).reshape(8, F2 // 2)
    ta, tg = tap(acca), tap(accg)
    gs["ffn_conv_w"] = jnp.concatenate([ta[:FFN_K], tg[:FFN_K]], axis=1)
    gs["ffn_conv_b"] = jnp.concatenate([ta[FFN_K:FFN_K + 1], tg[FFN_K:FFN_K + 1]], axis=1)
    dup = _ffn_bwd_b(dca, dcg, sp["ffn_conv_w"])
    gb["w_up"] = _mm_tn(hn, dup, 4, "dw_up")
    tok = ex.reduce_begin("a", ("w_down", "w_up"), gb)
    dh1, gs["ffn_norm_w"] = _norm_in_bwd(dup, bw["w_up"], h1, _after(sp["ffn_norm_w"], tok), dy, "ffn_in_bwd")
    dz, dattn, du3, doc, dya, dyc, dyx, merged, gs["b_gate"] = _outproj_bwd(
        dh1, z, sp["b_gate"], attn, u3, oc, bw["w_attn_o"], bw["w_conv_o"], bw["w_cross_o"], bw["w_out"], n_in)
    gb["w_out"] = _mm_tn(merged, dh1, 1, "dw_out")[0]
    gb["w_attn_o"] = _mm_tn(attn, dya, 4, "dw_attn_o")
    gb["w_conv_o"] = _mm_tn(u3, dyc, 4, "dw_conv_o")
    gb["w_cross_o"] = _mm_tn(oc, dyx, 4, "dw_cross_o")
    dz, dmk, dmv, gs["xq_norm_w"] = _cross_bwd(dz, doc, z, sp["xq_norm_w"], mk, mv)
    gb["w_mem_kv"], gs["xk_norm_w"], gs["mem_norm_w"] = _memkv_bwd(
        dmk, dmv, kv, mem, mn, sp["mem_norm_w"], bw["w_mem_kv"], sp["xk_norm_w"])
    ex.reduce_end("a", after=gs["mem_norm_w"])
    tok = ex.reduce_begin("b", ("w_out", "w_attn_o", "w_conv_o", "w_cross_o", "w_mem_kv"), gb)
    du1, cacc = _conv_bwd_a(du3, u1, z, _after(sp["conv_ln_w"], tok), sp["conv_ln_b"])
    gs["conv_dw_w"], gs["conv_dw_b"] = cacc[:CONV_K], cacc[32:33]
    gs["conv_ln_w"], gs["conv_ln_b"] = cacc[33:34], cacc[34:35]
    dz = _conv_bwd_b(dz, du1, z, sp["conv_dw_w"])
    wg, dah, dhb = _merge_bwd(dattn, os_, lses)
    dqs, dks, dvs, dsbs = [], [], [], []
    for g, (_, dil) in enumerate(ATTN_GROUPS):
        dq_g, dk_g, dv_g, dsb_g = _attn_bwd(qh, kh, vh, biasm[g], dah, wg[g], dhb, lses[g], g, dil)
        dqs.append(dq_g)
        dks.append(dk_g)
        dvs.append(dv_g)
        dsbs.append(dsb_g.reshape(HPG, NQ * 2 * NQ))
    dtab = _bias_bwd(jnp.pad(jnp.stack(dsbs), ((0, 0), (0, 8 - HPG), (0, 0))), buckets)
    gs["rel_bias_table"] = dtab[:, :HPG].reshape(N_GROUPS * HPG, N_BUCKETS).T
    dz, gs["q_norm_w"], gs["k_norm_w"] = _qkv_bwd(dz, z, dqs, dks, dvs, sp["q_norm_w"], sp["k_norm_w"])
    ex.reduce_end("b", after=gs["q_norm_w"])
    gb["w_in"] = _mm_tn(xn, dz, 4, "dw_in")
    tok = ex.reduce_begin("c", ("w_in",), gb)
    dx, gs["attn_norm_w"] = _norm_in_bwd(dz, bw["w_in"], x, _after(sp["attn_norm_w"], tok), dh1, "in_bwd")
    ex.reduce_end("c", after=gs["attn_norm_w"])
    return loss_tile, dx, gs, gb


SMALL = ("rel_bias_table", "attn_norm_w", "b_gate", "q_norm_w", "k_norm_w", "conv_dw_w", "conv_dw_b", "conv_ln_w", "conv_ln_b",
         "mem_norm_w", "xq_norm_w", "xk_norm_w", "ffn_norm_w", "ffn_conv_w", "ffn_conv_b")
SMALL_SHARDED = ("conv_dw_w", "ffn_conv_w")
BIG_COL = ("w_in", "w_attn_o", "w_conv_o", "w_cross_o", "w_up")
BIG_ROW = ("w_mem_kv", "w_out", "w_down")
BIG = BIG_COL + BIG_ROW
WEIGHTS = ("rel_bias_table", "attn_norm_w", "w_in", "b_gate", "q_norm_w", "k_norm_w", "w_attn_o", "conv_dw_w", "conv_dw_b",
           "conv_ln_w", "conv_ln_b", "w_conv_o", "mem_norm_w", "w_mem_kv", "xq_norm_w", "xk_norm_w", "w_cross_o", "w_out",
           "ffn_norm_w", "w_up", "ffn_conv_w", "ffn_conv_b", "w_down")


class _Exchanges:
    REST = tuple(k for k in BIG if k != "w_in")

    def __init__(self, w):
        self.w = w
        self.pending = {}
        self.reduced = {}

    def _whole(self, k, ga):
        ga = ga.reshape((4,) + self.w[k].shape)
        return ga if k in BIG_COL else ga.reshape((4 * self.w[k].shape[0],) + self.w[k].shape[1:])

    def first_weights(self):
        local = [_halves(self.w["w_in"].astype(BF16))]
        for k in SMALL_SHARDED:
            flat = jnp.ravel(self.w[k])
            local.append(jnp.pad(flat, (0, (-flat.shape[0]) % 2048)).reshape(2, -1, 128))
        gathered = _all_gather_xy(local, "gather_first")
        small = {}
        for k, ga in zip(SMALL_SHARDED, gathered[1:]):
            r, cdim = self.w[k].shape
            parts = ga.reshape(4, -1)[:, :r * cdim].reshape(4, r, cdim)
            small[k] = jnp.transpose(parts, (1, 0, 2)).reshape(r, 4 * cdim)
        return self._whole("w_in", gathered[0]), small

    def start_rest(self):
        local = [_halves(self.w[k].astype(BF16)) for k in self.REST]
        send_sems, recv_sems, ins, lands, token = _ag_ici_start(local, "gather_rest_start")
        self.pending["rest"] = (send_sems, recv_sems, ins, lands)
        return token

    def rest_weights(self, after):
        send_sems, recv_sems, ins, lands = self.pending.pop("rest")
        ins, lands = _ag_ici_wait(send_sems, recv_sems, ins, lands, after, "gather_rest_wait")
        gathered = _ag_finish(ins, lands, "gather_rest_finish")
        return {k: self._whole(k, ga) for k, ga in zip(self.REST, gathered)}

    def reduce_begin(self, tag, names, gb):
        parts = [gb[k].reshape((4, -1)) for k in names]
        self.pending[tag], token = _reduce_begin(parts, tag)
        self.pending[tag] += (names,)
        return token

    def reduce_end(self, tag, after):
        *state, names = self.pending.pop(tag)
        for k, flat in zip(names, _reduce_end(tuple(state), after, tag)):
            self.reduced[k] = flat.reshape(self.w[k].shape)


def _step(x, mem, target, w, m, v):
    xi, yi, _ = _position()
    shard = 2 * xi + yi
    ex = _Exchanges(w)
    w_in, small_gathered = ex.first_weights()
    sp = {k: w[k] for k in SMALL if k not in SMALL_SHARDED}
    sp.update(small_gathered)

    loss_tile, dx, gs, _ = _local_step(x, mem, target, sp, w_in, ex)
    g_big = ex.reduced

    small_shapes = [(8, 128)] + [gs[k].shape for k in SMALL]
    red = _unpack_rows(_all_reduce_small(_pack_rows([loss_tile] + [gs[k] for k in SMALL])), small_shapes)
    loss = red[0][0, 0]
    g_small = dict(zip(SMALL, red[1:]))
    for k in SMALL_SHARDED:
        cdim = w[k].shape[1]
        g_small[k] = lax.dynamic_slice_in_dim(g_small[k], shard * cdim, cdim, axis=1)

    grads, delta, new_m, new_v = {}, {}, {}, {}
    for k in BIG:
        grads[k] = g_big[k]
        delta[k], new_m[k], new_v[k] = _adamw(w[k], g_big[k], m[k], v[k], "adamw_" + k)
    shapes = [w[k].shape for k in SMALL]
    packed = [_pack_rows([d[k] for k in SMALL]) for d in (w, g_small, m, v)]
    outs = _adamw(*packed, "adamw_small")
    for dst, pk in zip((delta, new_m, new_v), outs):
        dst.update(zip(SMALL, _unpack_rows(pk, shapes)))
    grads.update(g_small)
    return loss, dx, grads, delta, new_m, new_v


def kernel(x, mem, rel_bias_table, attn_norm_w, w_in, b_gate, q_norm_w, k_norm_w, w_attn_o, conv_dw_w, conv_dw_b, conv_ln_w, conv_ln_b, w_conv_o, mem_norm_w, w_mem_kv, xq_norm_w, xk_norm_w, w_cross_o, w_out, ffn_norm_w, w_up, ffn_conv_w, ffn_conv_b, w_down, loss_target, m_rel_bias_table, m_attn_norm_w, m_w_in, m_b_gate, m_q_norm_w, m_k_norm_w, m_w_attn_o, m_conv_dw_w, m_conv_dw_b, m_conv_ln_w, m_conv_ln_b, m_w_conv_o, m_mem_norm_w, m_w_mem_kv, m_xq_norm_w, m_xk_norm_w, m_w_cross_o, m_w_out, m_ffn_norm_w, m_w_up, m_ffn_conv_w, m_ffn_conv_b, m_w_down, v_rel_bias_table, v_attn_norm_w, v_w_in, v_b_gate, v_q_norm_w, v_k_norm_w, v_w_attn_o, v_conv_dw_w, v_conv_dw_b, v_conv_ln_w, v_conv_ln_b, v_w_conv_o, v_mem_norm_w, v_w_mem_kv, v_xq_norm_w, v_xk_norm_w, v_w_cross_o, v_w_out, v_ffn_norm_w, v_w_up, v_ffn_conv_w, v_ffn_conv_b, v_w_down):
    args = locals()
    def block(name, k):
        a = args[name] if k == "rel_bias_table" else args[name][0]
        return a.reshape(1, -1) if a.ndim == 1 else a

    w = {k: block(k, k) for k in WEIGHTS}
    m = {k: block("m_" + k, k) for k in WEIGHTS}
    v = {k: block("v_" + k, k) for k in WEIGHTS}
    loss, dx, grads, delta, new_m, new_v = _step(x[0], mem[0], loss_target[0], w, m, v)
    out = [loss, dx[None]]
    for d in (grads, delta, new_m, new_v):
        for k in WEIGHTS:
            out.append(d[k].reshape(args[k].shape))
    return tuple(out)
```

```python
import functools
import math

import numpy as np
import jax
import jax.numpy as jnp
from jax import lax
from jax.experimental import pallas as pl
from jax.experimental.pallas import tpu as pltpu

F32, BF16 = jnp.float32, jnp.bfloat16
SDS = jax.ShapeDtypeStruct
MESH = pl.DeviceIdType.MESH

HEAD = 128
N_GROUPS, HPG = 3, 4
ATTN_GROUPS = ((128, 1), (512, 4), (2048, 16))
NQ = 128
QKV_W = N_GROUPS * HPG * HEAD
CH = 512
CONV_K, FFN_K = 31, 3
N_BUCKETS, MAX_DIST = 32, 2048
RMS_EPS, LN_EPS = 1e-6, 1e-5
O_Q, O_K, O_V, O_CV, O_CG, O_XQ, O_G = 0, QKV_W, 2 * QKV_W, 3 * QKV_W, 3 * QKV_W + CH, 3 * QKV_W + 2 * CH, 3 * QKV_W + 3 * CH
ADAM_LR, ADAM_B1, ADAM_B2, ADAM_EPS, ADAM_WD, ADAM_STEP = 0.001, 0.9, 0.999, 1e-08, 0.01, 10
NEG = -1e30
SCALE = HEAD ** -0.5
TM = 512
ATT_RB = 2048
NT_DIMS = (((1,), (1,)), ((), ()))
TN_DIMS = (((0,), (0,)), ((), ()))


CONV_RC = 16
FFN_RC = 32


def _sig(v):
    return 0.5 * jnp.tanh(0.5 * v) + 0.5


def _fold8(v):
    acc = v[0:8]
    for r in range(8, v.shape[0], 8):
        acc = acc + v[r:r + 8]
    return acc


def _row_tile(rows, cap, mult=8):
    best = None
    for t in range(mult, min(rows, cap) + 1, mult):
        if rows % t == 0:
            best = t
    return best if best is not None else rows


def _full(shape):
    n = len(shape)
    return pl.BlockSpec(shape, lambda *a: (0,) * n)


def _rmsnorm(x, w, name):
    S, D = x.shape

    def body(x_ref, w_ref, o_ref):
        xv = x_ref[...]
        r = lax.rsqrt(jnp.mean(xv * xv, axis=-1, keepdims=True) + RMS_EPS)
        o_ref[...] = (xv * r * w_ref[...]).astype(BF16)

    return pl.pallas_call(
        body, name=name, grid=(S // TM,),
        in_specs=[pl.BlockSpec((TM, D), lambda i: (i, 0)), _full((1, D))],
        out_specs=pl.BlockSpec((TM, D), lambda i: (i, 0)),
        out_shape=SDS((S, D), BF16))(x, w)


def _mm_nn(a, b, out_dtype, name):
    M, K = a.shape
    G, _, n = b.shape

    def body(a_ref, b_ref, o_ref):
        o_ref[...] = jnp.dot(a_ref[...].astype(BF16), b_ref[...], preferred_element_type=F32).astype(out_dtype)

    return pl.pallas_call(
        body, name=name, grid=(G, M // TM),
        in_specs=[pl.BlockSpec((TM, K), lambda g, i: (i, 0)), pl.BlockSpec((None, K, n), lambda g, i: (g, 0, 0))],
        out_specs=pl.BlockSpec((TM, n), lambda g, i: (i, g)),
        out_shape=SDS((M, G * n), out_dtype))(a, b)


def _mm_tn(a, b, G, name):
    S, Ka = a.shape
    n = b.shape[1] // G
    tka = Ka
    while tka * n * 4 > 6 * 2 ** 20 and tka % 256 == 0:
        tka //= 2

    def body(a_ref, b_ref, o_ref):
        @pl.when(pl.program_id(2) == 0)
        def _():
            o_ref[...] = jnp.zeros_like(o_ref)
        o_ref[...] += lax.dot_general(a_ref[...].astype(BF16), b_ref[...].astype(BF16), TN_DIMS, preferred_element_type=F32)

    return pl.pallas_call(
        body, name=name, grid=(G, Ka // tka, S // TM),
        in_specs=[pl.BlockSpec((TM, tka), lambda g, i, k: (k, i)), pl.BlockSpec((TM, n), lambda g, i, k: (k, g))],
        out_specs=pl.BlockSpec((None, tka, n), lambda g, i, k: (g, i, 0)),
        out_shape=SDS((G, Ka, n), F32))(a, b)


def _norm_in_bwd(a, w, xin, nw, resid, name):
    S, K = xin.shape
    G, _, n = w.shape

    def body(a_ref, w_ref, x_ref, nw_ref, r_ref, o_ref, dnw_ref, acc):
        i, g = pl.program_id(0), pl.program_id(1)

        @pl.when((i == 0) & (g == 0))
        def _():
            dnw_ref[...] = jnp.zeros_like(dnw_ref)

        @pl.when(g == 0)
        def _():
            acc[...] = jnp.zeros_like(acc)

        acc[...] += lax.dot_general(a_ref[...], w_ref[...], NT_DIMS, preferred_element_type=F32)

        @pl.when(g == G - 1)
        def _():
            dn = acc[...]
            xv = x_ref[...]
            r = lax.rsqrt(jnp.mean(xv * xv, axis=-1, keepdims=True) + RMS_EPS)
            xhat = xv * r
            dyw = dn * nw_ref[...]
            o_ref[...] = r_ref[...] + r * (dyw - xhat * jnp.mean(dyw * xhat, axis=-1, keepdims=True))
            dnw_ref[...] += jnp.sum(dn * xhat, axis=0, keepdims=True)

    return pl.pallas_call(
        body, name=name, grid=(S // TM, G),
        in_specs=[pl.BlockSpec((TM, n), lambda i, g: (i, g)), pl.BlockSpec((None, K, n), lambda i, g: (g, 0, 0)),
                  pl.BlockSpec((TM, K), lambda i, g: (i, 0)), _full((1, K)), pl.BlockSpec((TM, K), lambda i, g: (i, 0))],
        out_specs=[pl.BlockSpec((TM, K), lambda i, g: (i, 0)), _full((1, K))],
        out_shape=[SDS((S, K), F32), SDS((1, K), F32)],
        scratch_shapes=[pltpu.VMEM((TM, K), F32)])(a, w, xin, nw, resid)


def _t5_bucket_np(dist):
    max_exact = N_BUCKETS // 2
    d = np.maximum(dist.astype(np.float32), np.float32(1.0))
    large = max_exact + (np.log(d / np.float32(max_exact)) / np.float32(math.log(MAX_DIST / max_exact))
                         * np.float32(N_BUCKETS - max_exact)).astype(np.int32)
    large = np.minimum(large, N_BUCKETS - 1)
    return np.where(dist < max_exact, dist, large).astype(np.int32)


def _bias_static():
    qi = np.arange(NQ)[:, None]
    kj = np.arange(2 * NQ)[None, :]
    step = qi + NQ - kj
    band = (step >= 0) & (step <= NQ)
    buckets = np.stack([_t5_bucket_np(np.clip(step, 0, None) * dil).reshape(1, -1) for _, dil in ATTN_GROUPS])
    return band, buckets


def _bias_fwd(table_t, buckets):
    nb = buckets.shape[-1]

    def body(t_ref, b_ref, o_ref):
        oh = (b_ref[...] == lax.broadcasted_iota(jnp.int32, (N_BUCKETS, nb), 0)).astype(F32)
        o_ref[...] = jnp.dot(t_ref[...], oh, preferred_element_type=F32, precision=lax.Precision.HIGHEST)

    return pl.pallas_call(
        body, name="bias_fwd", grid=(N_GROUPS,),
        in_specs=[pl.BlockSpec((None, 8, N_BUCKETS), lambda g: (g, 0, 0)), pl.BlockSpec((None, 1, nb), lambda g: (g, 0, 0))],
        out_specs=pl.BlockSpec((None, 8, nb), lambda g: (g, 0, 0)),
        out_shape=SDS((N_GROUPS, 8, nb), F32))(table_t, buckets)


def _bias_bwd(dsb, buckets):
    nb = buckets.shape[-1]

    def body(d_ref, b_ref, o_ref):
        oh = (b_ref[...] == lax.broadcasted_iota(jnp.int32, (N_BUCKETS, nb), 0)).astype(F32)
        o_ref[...] = lax.dot_general(d_ref[...], oh, NT_DIMS, preferred_element_type=F32, precision=lax.Precision.HIGHEST)

    return pl.pallas_call(
        body, name="bias_bwd", grid=(N_GROUPS,),
        in_specs=[pl.BlockSpec((None, 8, nb), lambda g: (g, 0, 0)), pl.BlockSpec((None, 1, nb), lambda g: (g, 0, 0))],
        out_specs=pl.BlockSpec((None, 8, N_BUCKETS), lambda g: (g, 0, 0)),
        out_shape=SDS((N_GROUPS, 8, N_BUCKETS), F32))(dsb, buckets)


def _qkv_prep(z, qw, kw):
    S = z.shape[0]
    nh = N_GROUPS * HPG

    def body(zq, zk, zv, qw_ref, kw_ref, qh, kh, vh):
        for h in range(nh):
            g = h // HPG
            sl = slice(h * HEAD, (h + 1) * HEAD)
            xq = zq[:, sl].astype(F32)
            qh[h] = xq * lax.rsqrt(jnp.mean(xq * xq, axis=-1, keepdims=True) + RMS_EPS) * qw_ref[g:g + 1, :]
            xk = zk[:, sl].astype(F32)
            kh[h] = xk * lax.rsqrt(jnp.mean(xk * xk, axis=-1, keepdims=True) + RMS_EPS) * kw_ref[g:g + 1, :]
            vh[h] = zv[:, sl].astype(F32)

    hm = pl.BlockSpec((nh, TM, HEAD), lambda i: (0, i, 0))
    return pl.pallas_call(
        body, name="qkv_prep", grid=(S // TM,),
        in_specs=[pl.BlockSpec((TM, QKV_W), lambda i: (i, 0)), pl.BlockSpec((TM, QKV_W), lambda i: (i, 1)),
                  pl.BlockSpec((TM, QKV_W), lambda i: (i, 2)), _full((N_GROUPS, HEAD)), _full((N_GROUPS, HEAD))],
        out_specs=[hm, hm, hm],
        out_shape=[SDS((nh, S, HEAD), F32)] * 3)(z, z, z, qw, kw)


def _rows(start, d):
    return pl.ds(start, NQ) if d == 1 else pl.ds(start, NQ, stride=d)


def _attn_fwd(qh, kh, vh, biasm, g, d):
    S = qh.shape[1]
    RB = ATT_RB
    nbk, nq = S // RB, RB // (NQ * d)

    def body(q_ref, k_ref, v_ref, bias_ref, o_ref, lse_ref, kbuf, vbuf):
        b = pl.program_id(1)

        @pl.when(b == 0)
        def _():
            kbuf[0:RB, :] = jnp.zeros((RB, HEAD), F32)
            vbuf[0:RB, :] = jnp.zeros((RB, HEAD), F32)

        @pl.when(b > 0)
        def _():
            kbuf[0:RB, :] = kbuf[RB:2 * RB, :]
            vbuf[0:RB, :] = vbuf[RB:2 * RB, :]

        kbuf[RB:2 * RB, :] = k_ref[...]
        vbuf[RB:2 * RB, :] = v_ref[...]
        bias = bias_ref[...]
        col = lax.broadcasted_iota(jnp.int32, (NQ, 2 * NQ), 1)

        for qb in range(nq):
            def unit(r, carry, qb=qb):
                qs = qb * NQ * d + r
                q = q_ref[_rows(qs, d), :].astype(BF16)
                kw = jnp.concatenate([kbuf[_rows(RB + qs - NQ * d, d), :], kbuf[_rows(RB + qs, d), :]], axis=0).astype(BF16)
                vw = jnp.concatenate([vbuf[_rows(RB + qs - NQ * d, d), :], vbuf[_rows(RB + qs, d), :]], axis=0).astype(BF16)
                s = lax.dot_general(q, kw, NT_DIMS, preferred_element_type=F32) * SCALE + bias
                if qb == 0:
                    s = jnp.where((col < NQ) & (b == 0), NEG, s)
                m = jnp.max(s, axis=-1, keepdims=True)
                p = jnp.exp(s - m)
                l = jnp.sum(p, axis=-1, keepdims=True)
                o = jnp.dot(p.astype(BF16), vw, preferred_element_type=F32) / l
                o_ref[_rows(qs, d), :] = o
                lse_ref[_rows(qs, d), :] = jnp.broadcast_to(m + jnp.log(l), (NQ, HEAD))
                return carry

            if d == 1:
                unit(0, 0)
            else:
                lax.fori_loop(0, d, unit, 0)

    blk = lambda f: pl.BlockSpec((None, RB, HEAD), f)
    return pl.pallas_call(
        body, name=f"attn_fwd_g{g}", grid=(HPG, nbk),
        in_specs=[blk(lambda h, b: (HPG * g + h, b, 0))] * 3 + [pl.BlockSpec((None, NQ, 2 * NQ), lambda h, b: (h, 0, 0))],
        out_specs=[blk(lambda h, b: (h, b, 0))] * 2,
        out_shape=[SDS((HPG, S, HEAD), F32)] * 2,
        scratch_shapes=[pltpu.VMEM((2 * RB, HEAD), F32)] * 2)(qh, kh, vh, biasm)


def _attn_bwd(qh, kh, vh, biasm, da, wg, dh, lse, g, d):
    S = qh.shape[1]
    RB = ATT_RB
    nbk, nq = S // RB, RB // (NQ * d)

    def body(q_ref, k_ref, v_ref, bias_ref, da_ref, wg_ref, dh_ref, lse_ref,
             dq_ref, dk_ref, dv_ref, dsb_ref, kbuf, vbuf, dkbuf, dvbuf):
        b = pl.program_id(1)
        zero = jnp.zeros((RB, HEAD), F32)

        @pl.when(b == 0)
        def _():
            kbuf[0:RB, :] = zero
            vbuf[0:RB, :] = zero
            dkbuf[0:RB, :] = zero
            dvbuf[0:RB, :] = zero
            dsb_ref[...] = jnp.zeros_like(dsb_ref)

        @pl.when(b > 0)
        def _():
            kbuf[0:RB, :] = kbuf[RB:2 * RB, :]
            vbuf[0:RB, :] = vbuf[RB:2 * RB, :]
            dkbuf[0:RB, :] = dkbuf[RB:2 * RB, :]
            dvbuf[0:RB, :] = dvbuf[RB:2 * RB, :]

        dkbuf[RB:2 * RB, :] = zero
        dvbuf[RB:2 * RB, :] = zero

        @pl.when(b < nbk)
        def _():
            kbuf[RB:2 * RB, :] = k_ref[...]
            vbuf[RB:2 * RB, :] = v_ref[...]
            bias = bias_ref[...]
            col = lax.broadcasted_iota(jnp.int32, (NQ, 2 * NQ), 1)

            for qb in range(nq):
                def unit(r, carry, qb=qb):
                    qs = qb * NQ * d + r
                    prev, cur = _rows(RB + qs - NQ * d, d), _rows(RB + qs, d)
                    q = q_ref[_rows(qs, d), :].astype(BF16)
                    kw = jnp.concatenate([kbuf[prev, :], kbuf[cur, :]], axis=0).astype(BF16)
                    vw = jnp.concatenate([vbuf[prev, :], vbuf[cur, :]], axis=0).astype(BF16)
                    s = lax.dot_general(q, kw, NT_DIMS, preferred_element_type=F32) * SCALE + bias
                    if qb == 0:
                        s = jnp.where((col < NQ) & (b == 0), NEG, s)
                    p = jnp.exp(s - lse_ref[_rows(qs, d), :][:, 0:1])
                    w = wg_ref[_rows(qs, d), :]
                    do = (da_ref[_rows(qs, d), :] * w).astype(BF16)
                    dp = lax.dot_general(do, vw, NT_DIMS, preferred_element_type=F32)
                    ds = p * (dp - w[:, 0:1] * dh_ref[_rows(qs, d), :][:, 0:1])
                    dsb_ref[...] += ds
                    dsb = ds.astype(BF16)
                    dq_ref[_rows(qs, d), :] = jnp.dot(dsb, kw, preferred_element_type=F32) * SCALE
                    dkw = lax.dot_general(dsb, q, TN_DIMS, preferred_element_type=F32) * SCALE
                    dvw = lax.dot_general(p.astype(BF16), do, TN_DIMS, preferred_element_type=F32)
                    dkbuf[prev, :] += dkw[0:NQ, :]
                    dkbuf[cur, :] += dkw[NQ:2 * NQ, :]
                    dvbuf[prev, :] += dvw[0:NQ, :]
                    dvbuf[cur, :] += dvw[NQ:2 * NQ, :]
                    return carry

                if d == 1:
                    unit(0, 0)
                else:
                    lax.fori_loop(0, d, unit, 0)

        dk_ref[...] = dkbuf[0:RB, :]
        dv_ref[...] = dvbuf[0:RB, :]

    blk = lambda f: pl.BlockSpec((None, RB, HEAD), f)
    cur_g = blk(lambda h, b: (HPG * g + h, jnp.minimum(b, nbk - 1), 0))
    cur = blk(lambda h, b: (h, jnp.minimum(b, nbk - 1), 0))
    prv = blk(lambda h, b: (h, jnp.maximum(b - 1, 0), 0))
    sq = pl.BlockSpec((None, NQ, 2 * NQ), lambda h, b: (h, 0, 0))
    return pl.pallas_call(
        body, name=f"attn_bwd_g{g}", grid=(HPG, nbk + 1),
        in_specs=[cur_g, cur_g, cur_g, sq, cur, cur, cur, cur],
        out_specs=[cur, prv, prv, sq],
        out_shape=[SDS((HPG, S, HEAD), F32)] * 3 + [SDS((HPG, NQ, 2 * NQ), F32)],
        scratch_shapes=[pltpu.VMEM((2 * RB, HEAD), F32)] * 4)(qh, kh, vh, biasm, da, wg, dh, lse)


def _merge_weights(l0, l1, l2):
    m = jnp.maximum(jnp.maximum(l0, l1), l2)
    e0, e1, e2 = jnp.exp(l0 - m), jnp.exp(l1 - m), jnp.exp(l2 - m)
    inv = 1.0 / (e0 + e1 + e2)
    return e0 * inv, e1 * inv, e2 * inv


def _merge_fwd(os_, lses):
    S = os_[0].shape[1]

    def body(o0, o1, o2, l0, l1, l2, a_ref):
        for h in range(HPG):
            w0, w1, w2 = _merge_weights(l0[h], l1[h], l2[h])
            a_ref[:, h * HEAD:(h + 1) * HEAD] = (w0 * o0[h] + w1 * o1[h] + w2 * o2[h]).astype(BF16)

    hm = pl.BlockSpec((HPG, TM, HEAD), lambda i: (0, i, 0))
    return pl.pallas_call(
        body, name="merge_fwd", grid=(S // TM,), in_specs=[hm] * 6,
        out_specs=pl.BlockSpec((TM, CH), lambda i: (i, 0)),
        out_shape=SDS((S, CH), BF16))(*os_, *lses)


def _merge_bwd(dattn, os_, lses):
    S = dattn.shape[0]

    def body(da_ref, o0, o1, o2, l0, l1, l2, wg_ref, dah_ref, dh_ref):
        for h in range(HPG):
            w = _merge_weights(l0[h], l1[h], l2[h])
            attn = w[0] * o0[h] + w[1] * o1[h] + w[2] * o2[h]
            da = da_ref[:, h * HEAD:(h + 1) * HEAD]
            for gi in range(N_GROUPS):
                wg_ref[gi, h] = w[gi]
            dah_ref[h] = da
            dh_ref[h] = jnp.broadcast_to(jnp.sum(da * attn, axis=-1, keepdims=True), (TM, HEAD))

    hm = pl.BlockSpec((HPG, TM, HEAD), lambda i: (0, i, 0))
    return pl.pallas_call(
        body, name="merge_bwd", grid=(S // TM,),
        in_specs=[pl.BlockSpec((TM, CH), lambda i: (i, 0))] + [hm] * 6,
        out_specs=[pl.BlockSpec((N_GROUPS, HPG, TM, HEAD), lambda i: (0, 0, i, 0)), hm, hm],
        out_shape=[SDS((N_GROUPS, HPG, S, HEAD), F32), SDS((HPG, S, HEAD), F32), SDS((HPG, S, HEAD), F32)])(dattn, *os_, *lses)


def _qkv_bwd(dz, z, dqs, dks, dvs, qw, kw):
    S = z.shape[0]
    nh = N_GROUPS * HPG

    def body(dz_in, zq, zk, *refs):
        del dz_in
        dq_refs, dk_refs, dv_refs = refs[0:3], refs[3:6], refs[6:9]
        qw_ref, kw_ref, dz_ref, dqw_ref, dkw_ref = refs[9:]

        @pl.when(pl.program_id(0) == 0)
        def _():
            dqw_ref[...] = jnp.zeros_like(dqw_ref)
            dkw_ref[...] = jnp.zeros_like(dkw_ref)

        def nbwd(xr, dy, wr, dwr, h, off):
            g = h // HPG
            x = xr[:, h * HEAD:(h + 1) * HEAD].astype(F32)
            r = lax.rsqrt(jnp.mean(x * x, axis=-1, keepdims=True) + RMS_EPS)
            xhat = x * r
            dyw = dy * wr[g:g + 1, :]
            dz_ref[:, off + h * HEAD:off + (h + 1) * HEAD] = (
                r * (dyw - xhat * jnp.mean(dyw * xhat, axis=-1, keepdims=True))).astype(BF16)
            dwr[g:g + 1, :] += jnp.sum(dy * xhat, axis=0, keepdims=True)

        for h in range(nh):
            g, hh = h // HPG, h % HPG
            nbwd(zq, dq_refs[g][hh], qw_ref, dqw_ref, h, O_Q)
            nbwd(zk, dk_refs[g][hh], kw_ref, dkw_ref, h, O_K)
            dz_ref[:, O_V + h * HEAD:O_V + (h + 1) * HEAD] = dv_refs[g][hh].astype(BF16)

    hm = pl.BlockSpec((HPG, TM, HEAD), lambda i: (0, i, 0))
    return pl.pallas_call(
        body, name="qkv_bwd", grid=(S // TM,),
        in_specs=[pl.BlockSpec(memory_space=pl.ANY), pl.BlockSpec((TM, QKV_W), lambda i: (i, 0)),
                  pl.BlockSpec((TM, QKV_W), lambda i: (i, 1))] + [hm] * 9 + [_full((N_GROUPS, HEAD)), _full((N_GROUPS, HEAD))],
        out_specs=[pl.BlockSpec((TM, 3 * QKV_W), lambda i: (i, 0)), _full((N_GROUPS, HEAD)), _full((N_GROUPS, HEAD))],
        out_shape=[SDS(dz.shape, BF16), SDS((N_GROUPS, HEAD), F32), SDS((N_GROUPS, HEAD), F32)],
        input_output_aliases={0: 0})(dz, z, z, *dqs, *dks, *dvs, qw, kw)


def _conv_fwd(z, cw, cb, lnw, lnb):
    S = z.shape[0]
    H = 32

    def body(zv, zg, cw_ref, cb_ref, lnw_ref, lnb_ref, u1_ref, u3_ref, xbuf):
        i = pl.program_id(0)

        @pl.when(i == 0)
        def _():
            xbuf[0:H, :] = jnp.zeros((H, CH), F32)

        @pl.when(i > 0)
        def _():
            xbuf[0:H, :] = xbuf[TM:TM + H, :]

        xbuf[H:H + TM, :] = zv[...].astype(F32) * _sig(zg[...].astype(F32))
        for r0 in range(0, TM, CONV_RC):
            rows = pl.ds(r0, CONV_RC)
            acc = jnp.broadcast_to(cb_ref[...], (CONV_RC, CH))
            for k in range(CONV_K):
                acc = acc + xbuf[pl.ds(H - (CONV_K - 1) + k + r0, CONV_RC), :] * cw_ref[k:k + 1, :]
            u1_ref[rows, :] = acc
            mu = jnp.mean(acc, axis=-1, keepdims=True)
            xc = acc - mu
            yl = xc * lax.rsqrt(jnp.mean(xc * xc, axis=-1, keepdims=True) + LN_EPS) * lnw_ref[...] + lnb_ref[...]
            u3_ref[rows, :] = (yl * _sig(yl)).astype(BF16)

    row = pl.BlockSpec((TM, CH), lambda i: (i, 0))
    return pl.pallas_call(
        body, name="conv_fwd", grid=(S // TM,),
        in_specs=[pl.BlockSpec((TM, CH), lambda i: (i, O_CV // CH)), pl.BlockSpec((TM, CH), lambda i: (i, O_CG // CH)),
                  _full((CONV_K, CH)), _full((1, CH)), _full((1, CH)), _full((1, CH))],
        out_specs=[row, row], out_shape=[SDS((S, CH), F32), SDS((S, CH), BF16)],
        scratch_shapes=[pltpu.VMEM((TM + H, CH), F32)])(z, z, cw, cb, lnw, lnb)


def _conv_bwd_a(du3, u1, z, lnw, lnb):
    S = z.shape[0]
    H = 32

    nt = S // TM

    def body(du3_ref, u1_ref, zv, zg, lnw_ref, lnb_ref, du1_ref, acc_ref, xbuf, tacc):
        i = pl.program_id(0)

        @pl.when(i == 0)
        def _():
            xbuf[0:H, :] = jnp.zeros((H, CH), F32)
            tacc[...] = jnp.zeros_like(tacc)

        @pl.when(i > 0)
        def _():
            xbuf[0:H, :] = xbuf[TM:TM + H, :]

        xbuf[H:H + TM, :] = zv[...].astype(F32) * _sig(zg[...].astype(F32))
        for r0 in range(0, TM, CONV_RC):
            rows = pl.ds(r0, CONV_RC)
            u1 = u1_ref[rows, :]
            mu = jnp.mean(u1, axis=-1, keepdims=True)
            xc = u1 - mu
            r = lax.rsqrt(jnp.mean(xc * xc, axis=-1, keepdims=True) + LN_EPS)
            yhat = xc * r
            yl = yhat * lnw_ref[...] + lnb_ref[...]
            sg = _sig(yl)
            dyl = du3_ref[rows, :] * (sg * (1.0 + yl * (1.0 - sg)))
            dyh = dyl * lnw_ref[...]
            du1 = r * (dyh - jnp.mean(dyh, axis=-1, keepdims=True) - yhat * jnp.mean(dyh * yhat, axis=-1, keepdims=True))
            du1_ref[rows, :] = du1
            tacc[33] += _fold8(dyl * yhat)
            tacc[34] += _fold8(dyl)
            tacc[32] += _fold8(du1)
            for k in range(CONV_K):
                tacc[k] += _fold8(du1 * xbuf[pl.ds(H - (CONV_K - 1) + k + r0, CONV_RC), :])

        @pl.when(i == nt - 1)
        def _():
            for k in range(40):
                acc_ref[k:k + 1, :] = jnp.sum(tacc[k], axis=0, keepdims=True)

    row = pl.BlockSpec((TM, CH), lambda i: (i, 0))
    return pl.pallas_call(
        body, name="conv_bwd_a", grid=(nt,),
        in_specs=[row, row, pl.BlockSpec((TM, CH), lambda i: (i, O_CV // CH)), pl.BlockSpec((TM, CH), lambda i: (i, O_CG // CH)),
                  _full((1, CH)), _full((1, CH))],
        out_specs=[row, _full((40, CH))], out_shape=[SDS((S, CH), F32), SDS((40, CH), F32)],
        scratch_shapes=[pltpu.VMEM((TM + H, CH), F32), pltpu.VMEM((40, 8, CH), F32)])(du3, u1, z, z, lnw, lnb)


def _conv_bwd_b(dz, du1, z, cw):
    S = z.shape[0]
    nt = S // TM
    H = 32

    def body(dz_in, du1_ref, zv, zg, cw_ref, dz_ref, ybuf, dgate):
        del dz_in
        i, p = pl.program_id(0), pl.program_id(1)

        @pl.when(p == 0)
        def _():
            @pl.when(i == 0)
            def _():
                ybuf[TM:TM + H, :] = jnp.zeros((H, CH), F32)

            @pl.when(i > 0)
            def _():
                ybuf[TM:TM + H, :] = ybuf[0:H, :]

            ybuf[0:TM, :] = du1_ref[...]
            for r0 in range(0, TM, CONV_RC):
                rows = pl.ds(r0, CONV_RC)
                acc = ybuf[rows, :] * cw_ref[CONV_K - 1:CONV_K, :]
                for j in range(1, CONV_K):
                    acc = acc + ybuf[pl.ds(r0 + j, CONV_RC), :] * cw_ref[CONV_K - 1 - j:CONV_K - j, :]
                val = zv[rows, :].astype(F32)
                sg = _sig(zg[rows, :].astype(F32))
                dz_ref[rows, :] = (acc * sg).astype(BF16)
                dgate[rows, :] = (acc * val * sg * (1.0 - sg)).astype(BF16)

        @pl.when(p == 1)
        def _():
            dz_ref[...] = dgate[...]

    rev = lambda c: pl.BlockSpec((TM, CH), lambda i, p: (nt - 1 - i, c))
    return pl.pallas_call(
        body, name="conv_bwd_b", grid=(nt, 2),
        in_specs=[pl.BlockSpec(memory_space=pl.ANY), rev(0), rev(O_CV // CH), rev(O_CG // CH), _full((CONV_K, CH))],
        out_specs=pl.BlockSpec((TM, CH), lambda i, p: (nt - 1 - i, O_CV // CH + p)),
        out_shape=SDS(dz.shape, BF16),
        scratch_shapes=[pltpu.VMEM((TM + H, CH), F32), pltpu.VMEM((TM, CH), BF16)],
        input_output_aliases={0: 0})(dz, du1, z, z, cw)


def _memkv_fwd(mem, mnw, wkv, xkw):
    M, D = mem.shape

    def body(mem_ref, mnw_ref, w_ref, xkw_ref, mn_ref, kv_ref, mk_ref, mv_ref):
        x = mem_ref[...]
        mn = (x * lax.rsqrt(jnp.mean(x * x, axis=-1, keepdims=True) + RMS_EPS) * mnw_ref[...]).astype(BF16)
        mn_ref[...] = mn
        kv = jnp.dot(mn, w_ref[...], preferred_element_type=F32)
        kv_ref[...] = kv
        for h in range(HPG):
            k = kv[:, h * HEAD:(h + 1) * HEAD]
            mk_ref[:, h * HEAD:(h + 1) * HEAD] = (
                k * lax.rsqrt(jnp.mean(k * k, axis=-1, keepdims=True) + RMS_EPS) * xkw_ref[...]).astype(BF16)
        mv_ref[...] = kv[:, CH:2 * CH].astype(BF16)

    return pl.pallas_call(
        body, name="memkv_fwd",
        out_shape=[SDS((M, D), BF16), SDS((M, 2 * CH), F32), SDS((M, CH), BF16), SDS((M, CH), BF16)])(mem, mnw, wkv, xkw)


def _cross_q(zx, xqw, h):
    x = zx[:, h * HEAD:(h + 1) * HEAD].astype(F32)
    r = lax.rsqrt(jnp.mean(x * x, axis=-1, keepdims=True) + RMS_EPS)
    xhat = x * r
    return xhat, r, xhat * xqw


def _cross_fwd(z, xqw, mk, mv):
    S = z.shape[0]
    M = mk.shape[0]

    def body(zx, xqw_ref, mk_ref, mv_ref, o_ref):
        for h in range(HPG):
            sl = slice(h * HEAD, (h + 1) * HEAD)
            _, _, q = _cross_q(zx, xqw_ref[...], h)
            s = lax.dot_general(q.astype(BF16), mk_ref[:, sl], NT_DIMS, preferred_element_type=F32) * SCALE
            e = jnp.exp(s - jnp.max(s, axis=-1, keepdims=True))
            p = e / jnp.sum(e, axis=-1, keepdims=True)
            o_ref[:, sl] = jnp.dot(p.astype(BF16), mv_ref[:, sl], preferred_element_type=F32).astype(BF16)

    return pl.pallas_call(
        body, name="cross_fwd", grid=(S // TM,),
        in_specs=[pl.BlockSpec((TM, CH), lambda i: (i, O_XQ // CH)), _full((1, HEAD)), _full((M, CH)), _full((M, CH))],
        out_specs=pl.BlockSpec((TM, CH), lambda i: (i, 0)), out_shape=SDS((S, CH), BF16))(z, xqw, mk, mv)


def _cross_bwd(dz, doc, z, xqw, mk, mv):
    S = z.shape[0]
    M = mk.shape[0]

    def body(dz_in, do_ref, zx, xqw_ref, mk_ref, mv_ref, dz_ref, dmk_ref, dmv_ref, dxw_ref):
        del dz_in

        @pl.when(pl.program_id(0) == 0)
        def _():
            dmk_ref[...] = jnp.zeros_like(dmk_ref)
            dmv_ref[...] = jnp.zeros_like(dmv_ref)
            dxw_ref[...] = jnp.zeros_like(dxw_ref)

        for h in range(HPG):
            sl = slice(h * HEAD, (h + 1) * HEAD)
            xhat, r, q = _cross_q(zx, xqw_ref[...], h)
            qb = q.astype(BF16)
            s = lax.dot_general(qb, mk_ref[:, sl], NT_DIMS, preferred_element_type=F32) * SCALE
            e = jnp.exp(s - jnp.max(s, axis=-1, keepdims=True))
            p = e / jnp.sum(e, axis=-1, keepdims=True)
            do = do_ref[:, sl].astype(BF16)
            dp = lax.dot_general(do, mv_ref[:, sl], NT_DIMS, preferred_element_type=F32)
            ds = (p * (dp - jnp.sum(p * dp, axis=-1, keepdims=True)) * SCALE).astype(BF16)
            dq = jnp.dot(ds, mk_ref[:, sl], preferred_element_type=F32)
            dmk_ref[:, sl] += lax.dot_general(ds, qb, TN_DIMS, preferred_element_type=F32)
            dmv_ref[:, sl] += lax.dot_general(p.astype(BF16), do, TN_DIMS, preferred_element_type=F32)
            dyw = dq * xqw_ref[...]
            dz_ref[:, sl] = (r * (dyw - xhat * jnp.mean(dyw * xhat, axis=-1, keepdims=True))).astype(BF16)
            dxw_ref[...] += jnp.sum(dq * xhat, axis=0, keepdims=True)

    return pl.pallas_call(
        body, name="cross_bwd", grid=(S // TM,),
        in_specs=[pl.BlockSpec(memory_space=pl.ANY), pl.BlockSpec((TM, CH), lambda i: (i, 0)),
                  pl.BlockSpec((TM, CH), lambda i: (i, O_XQ // CH)), _full((1, HEAD)), _full((M, CH)), _full((M, CH))],
        out_specs=[pl.BlockSpec((TM, CH), lambda i: (i, O_XQ // CH)), _full((M, CH)), _full((M, CH)), _full((1, HEAD))],
        out_shape=[SDS(dz.shape, BF16), SDS((M, CH), F32), SDS((M, CH), F32), SDS((1, HEAD), F32)],
        input_output_aliases={0: 0})(dz, doc, z, xqw, mk, mv)


def _memkv_bwd(dmk, dmv, kv, mem, mn, mnw, wkv, xkw):
    M, D = mem.shape

    def body(dmk_ref, dmv_ref, kv_ref, mem_ref, mn_ref, mnw_ref, w_ref, xkw_ref, dw_ref, dxk_ref, dmn_ref, dkv):
        dxk = jnp.zeros((1, HEAD), F32)
        for h in range(HPG):
            sl = slice(h * HEAD, (h + 1) * HEAD)
            k = kv_ref[:, sl]
            r = lax.rsqrt(jnp.mean(k * k, axis=-1, keepdims=True) + RMS_EPS)
            khat = k * r
            dy = dmk_ref[:, sl]
            dyw = dy * xkw_ref[...]
            dkv[:, sl] = (r * (dyw - khat * jnp.mean(dyw * khat, axis=-1, keepdims=True))).astype(BF16)
            dxk = dxk + jnp.sum(dy * khat, axis=0, keepdims=True)
        dxk_ref[...] = dxk
        dkv[:, CH:2 * CH] = dmv_ref[...].astype(BF16)
        dw_ref[...] = lax.dot_general(mn_ref[...], dkv[...], TN_DIMS, preferred_element_type=F32)
        dn = lax.dot_general(dkv[...], w_ref[...], NT_DIMS, preferred_element_type=F32)
        x = mem_ref[...]
        r = lax.rsqrt(jnp.mean(x * x, axis=-1, keepdims=True) + RMS_EPS)
        dmn_ref[...] = jnp.sum(dn * x * r, axis=0, keepdims=True)

    return pl.pallas_call(
        body, name="memkv_bwd",
        out_shape=[SDS((D, 2 * CH), F32), SDS((1, HEAD), F32), SDS((1, D), F32)],
        scratch_shapes=[pltpu.VMEM((M, 2 * CH), BF16)])(dmk, dmv, kv, mem, mn, mnw, wkv, xkw)


def _branch_proj(a_ref, w_ref, y_ref):
    G, _, n = w_ref.shape
    a = a_ref[...]
    for g in range(G):
        y_ref[:, g * n:(g + 1) * n] = jnp.dot(a, w_ref[g], preferred_element_type=F32)


def _gates(zg_ref, bg_ref, k, D):
    return _sig(zg_ref[:, k * D:(k + 1) * D].astype(F32) + bg_ref[:, k * D:(k + 1) * D])


def _outproj_fwd(x, z, bg, attn, u3, oc, wao, wco, wxo, wout, fnw):
    S, D = x.shape
    tm = 256

    def body(x_ref, zg_ref, bg_ref, a_ref, u_ref, c_ref, wa, wc, wx, wo, fnw_ref, h1_ref, hn_ref, ya, yc, yx):
        _branch_proj(a_ref, wa, ya)
        _branch_proj(u_ref, wc, yc)
        _branch_proj(c_ref, wx, yx)
        merged = _gates(zg_ref, bg_ref, 0, D) * ya[...] + _gates(zg_ref, bg_ref, 1, D) * yc[...] + _gates(zg_ref, bg_ref, 2, D) * yx[...]
        h1 = x_ref[...] + jnp.dot(merged.astype(BF16), wo[...], preferred_element_type=F32)
        h1_ref[...] = h1
        hn_ref[...] = (h1 * lax.rsqrt(jnp.mean(h1 * h1, axis=-1, keepdims=True) + RMS_EPS) * fnw_ref[...]).astype(BF16)

    row = lambda w: pl.BlockSpec((tm, w), lambda i: (i, 0))
    return pl.pallas_call(
        body, name="outproj_fwd", grid=(S // tm,),
        in_specs=[row(D), pl.BlockSpec((tm, 3 * D), lambda i: (i, O_G // (3 * D))), _full((1, 3 * D)), row(CH), row(CH), row(CH),
                  _full(wao.shape), _full(wco.shape), _full(wxo.shape), _full((D, D)), _full((1, D))],
        out_specs=[row(D), row(D)], out_shape=[SDS((S, D), F32), SDS((S, D), BF16)],
        scratch_shapes=[pltpu.VMEM((tm, D), F32)] * 3)(x, z, bg, attn, u3, oc, wao, wco, wxo, wout, fnw)


def _outproj_bwd(dh1, z, bg, attn, u3, oc, wao, wco, wxo, wout, n_in):
    S, D = dh1.shape
    tm = 256

    def body(dh_ref, zg_ref, bg_ref, a_ref, u_ref, c_ref, wa, wc, wx, wo,
             dz_ref, da_ref, du_ref, dc_ref, dya_ref, dyc_ref, dyx_ref, mg_ref, dbg_ref, ya, yc, yx):
        @pl.when(pl.program_id(0) == 0)
        def _():
            dbg_ref[...] = jnp.zeros_like(dbg_ref)

        _branch_proj(a_ref, wa, ya)
        _branch_proj(u_ref, wc, yc)
        _branch_proj(c_ref, wx, yx)
        dm = lax.dot_general(dh_ref[...].astype(BF16), wo[...], NT_DIMS, preferred_element_type=F32)
        merged = jnp.zeros((tm, D), F32)
        for k, (y, dy_ref, w_ref, db_ref) in enumerate(((ya, dya_ref, wa, da_ref), (yc, dyc_ref, wc, du_ref), (yx, dyx_ref, wx, dc_ref))):
            gk = _gates(zg_ref, bg_ref, k, D)
            yk = y[...]
            merged = merged + gk * yk
            dzg = dm * yk * gk * (1.0 - gk)
            dz_ref[:, k * D:(k + 1) * D] = dzg.astype(BF16)
            dbg_ref[:, k * D:(k + 1) * D] += jnp.sum(dzg, axis=0, keepdims=True)
            dyk = (dm * gk).astype(BF16)
            dy_ref[...] = dyk
            G, _, n = w_ref.shape
            acc = jnp.zeros((tm, CH), F32)
            for g in range(G):
                acc = acc + lax.dot_general(dyk[:, g * n:(g + 1) * n], w_ref[g], NT_DIMS, preferred_element_type=F32)
            db_ref[...] = acc
        mg_ref[...] = merged.astype(BF16)

    row = lambda w: pl.BlockSpec((tm, w), lambda i: (i, 0))
    return pl.pallas_call(
        body, name="outproj_bwd", grid=(S // tm,),
        in_specs=[row(D), pl.BlockSpec((tm, 3 * D), lambda i: (i, O_G // (3 * D))), _full((1, 3 * D)), row(CH), row(CH), row(CH),
                  _full(wao.shape), _full(wco.shape), _full(wxo.shape), _full((D, D))],
        out_specs=[pl.BlockSpec((tm, 3 * D), lambda i: (i, O_G // (3 * D))), row(CH), row(CH), row(CH),
                   row(D), row(D), row(D), row(D), _full((1, 3 * D))],
        out_shape=[SDS((S, n_in), BF16)] + [SDS((S, CH), F32)] * 3 + [SDS((S, D), BF16)] * 4 + [SDS((1, 3 * D), F32)],
        scratch_shapes=[pltpu.VMEM((tm, D), F32)] * 3)(dh1, z, bg, attn, u3, oc, wao, wco, wxo, wout)


FFN_TC = 256
FFN_H = 8


def _ffn_conv(buf, w_ref, b_ref, r0):
    acc = jnp.broadcast_to(b_ref[...], (FFN_RC, FFN_TC))
    for k in range(FFN_K):
        acc = acc + buf[pl.ds(FFN_H - (FFN_K - 1) + k + r0, FFN_RC), :] * w_ref[k:k + 1, :]
    return acc


def _ffn_act_fwd(up, cw, cb):
    S, F2 = up.shape
    nj = F2 // 2 // FFN_TC
    tm = 1024 if S % 1024 == 0 else TM

    def body(ua, ug, wa, wg, ba, bgt, o_ref, abuf, gbuf):
        i = pl.program_id(1)

        @pl.when(i == 0)
        def _():
            abuf[0:FFN_H, :] = jnp.zeros((FFN_H, FFN_TC), F32)
            gbuf[0:FFN_H, :] = jnp.zeros((FFN_H, FFN_TC), F32)

        @pl.when(i > 0)
        def _():
            abuf[0:FFN_H, :] = abuf[tm:tm + FFN_H, :]
            gbuf[0:FFN_H, :] = gbuf[tm:tm + FFN_H, :]

        abuf[FFN_H:FFN_H + tm, :] = ua[...].astype(F32)
        gbuf[FFN_H:FFN_H + tm, :] = ug[...].astype(F32)
        for r0 in range(0, tm, FFN_RC):
            a = _ffn_conv(abuf, wa, ba, r0)
            gt = _ffn_conv(gbuf, wg, bgt, r0)
            o_ref[pl.ds(r0, FFN_RC), :] = (gt * _sig(gt) * a).astype(BF16)

    return pl.pallas_call(
        body, name="ffn_act_fwd", grid=(nj, S // tm),
        in_specs=[pl.BlockSpec((tm, FFN_TC), lambda j, i: (i, j)), pl.BlockSpec((tm, FFN_TC), lambda j, i: (i, nj + j)),
                  pl.BlockSpec((FFN_K, FFN_TC), lambda j, i: (0, j)), pl.BlockSpec((FFN_K, FFN_TC), lambda j, i: (0, nj + j)),
                  pl.BlockSpec((1, FFN_TC), lambda j, i: (0, j)), pl.BlockSpec((1, FFN_TC), lambda j, i: (0, nj + j))],
        out_specs=pl.BlockSpec((tm, FFN_TC), lambda j, i: (i, j)), out_shape=SDS((S, F2 // 2), BF16),
        scratch_shapes=[pltpu.VMEM((tm + FFN_H, FFN_TC), F32)] * 2)(up, up, cw, cw, cb, cb)


def _ffn_down_loss(act, wdown, h1, target):
    S, D = h1.shape
    F = act.shape[1]

    def body(a_ref, w_ref, h_ref, t_ref, dy_ref, loss_ref):
        @pl.when(pl.program_id(0) == 0)
        def _():
            loss_ref[...] = jnp.zeros_like(loss_ref)

        err = h_ref[...] + jnp.dot(a_ref[...], w_ref[...], preferred_element_type=F32) - t_ref[...]
        dy_ref[...] = err * (1.0 / D)
        loss_ref[...] += 0.5 * jnp.sum(jnp.mean(err * err, axis=-1, keepdims=True))

    row = lambda w: pl.BlockSpec((TM, w), lambda i: (i, 0))
    return pl.pallas_call(
        body, name="ffn_down_loss", grid=(S // TM,),
        in_specs=[row(F), _full((F, D)), row(D), row(D)],
        out_specs=[row(D), _full((8, 128))], out_shape=[SDS((S, D), F32), SDS((8, 128), F32)])(act, wdown, h1, target)


def _ffn_bwd_a(dy, wdown, up, cw, cb):
    S, D = dy.shape
    F2 = up.shape[1]
    F = F2 // 2
    nj = F // FFN_TC
    tm = TM

    def body(dy_ref, wd_ref, ua, ug, wa, wg, ba, bgt, da_ref, dg_ref, acca_ref, accg_ref, abuf, gbuf, hala, halg, dact_s):
        i, j = pl.program_id(0), pl.program_id(1)

        @pl.when((i == 0) & (j == 0))
        def _():
            acca_ref[...] = jnp.zeros_like(acca_ref)
            accg_ref[...] = jnp.zeros_like(accg_ref)

        @pl.when(i == 0)
        def _():
            abuf[0:FFN_H, :] = jnp.zeros((FFN_H, FFN_TC), F32)
            gbuf[0:FFN_H, :] = jnp.zeros((FFN_H, FFN_TC), F32)

        @pl.when(i > 0)
        def _():
            abuf[0:FFN_H, :] = hala[j]
            gbuf[0:FFN_H, :] = halg[j]

        abuf[FFN_H:FFN_H + tm, :] = ua[...].astype(F32)
        gbuf[FFN_H:FFN_H + tm, :] = ug[...].astype(F32)
        hala[j] = abuf[tm:tm + FFN_H, :]
        halg[j] = gbuf[tm:tm + FFN_H, :]
        dact_s[...] = lax.dot_general(dy_ref[...].astype(BF16), wd_ref[...], NT_DIMS, preferred_element_type=F32)
        zero8 = jnp.zeros((8, FFN_TC), F32)
        pa, pg = [zero8] * (FFN_K + 1), [zero8] * (FFN_K + 1)
        for r0 in range(0, tm, FFN_RC):
            rows = pl.ds(r0, FFN_RC)
            a = _ffn_conv(abuf, wa, ba, r0)
            gt = _ffn_conv(gbuf, wg, bgt, r0)
            dact = dact_s[rows, :]
            sg = _sig(gt)
            dac = dact * (gt * sg)
            dgc = dact * a * (sg * (1.0 + gt * (1.0 - sg)))
            da_ref[rows, :] = dac.astype(BF16)
            dg_ref[rows, :] = dgc.astype(BF16)
            for k in range(FFN_K):
                sh = pl.ds(FFN_H - (FFN_K - 1) + k + r0, FFN_RC)
                pa[k] = pa[k] + _fold8(dac * abuf[sh, :])
                pg[k] = pg[k] + _fold8(dgc * gbuf[sh, :])
            pa[FFN_K] = pa[FFN_K] + _fold8(dac)
            pg[FFN_K] = pg[FFN_K] + _fold8(dgc)
        for k in range(FFN_K + 1):
            acca_ref[j, k:k + 1, :] += jnp.sum(pa[k], axis=0, keepdims=True)
            accg_ref[j, k:k + 1, :] += jnp.sum(pg[k], axis=0, keepdims=True)

    return pl.pallas_call(
        body, name="ffn_bwd_a", grid=(S // tm, nj),
        in_specs=[pl.BlockSpec((tm, D), lambda i, j: (i, 0)), pl.BlockSpec((FFN_TC, D), lambda i, j: (j, 0)),
                  pl.BlockSpec((tm, FFN_TC), lambda i, j: (i, j)), pl.BlockSpec((tm, FFN_TC), lambda i, j: (i, nj + j)),
                  pl.BlockSpec((FFN_K, FFN_TC), lambda i, j: (0, j)), pl.BlockSpec((FFN_K, FFN_TC), lambda i, j: (0, nj + j)),
                  pl.BlockSpec((1, FFN_TC), lambda i, j: (0, j)), pl.BlockSpec((1, FFN_TC), lambda i, j: (0, nj + j))],
        out_specs=[pl.BlockSpec((tm, FFN_TC), lambda i, j: (i, j))] * 2 + [_full((nj, 8, FFN_TC))] * 2,
        out_shape=[SDS((S, F), BF16)] * 2 + [SDS((nj, 8, FFN_TC), F32)] * 2,
        scratch_shapes=[pltpu.VMEM((tm + FFN_H, FFN_TC), F32)] * 2 + [pltpu.VMEM((nj, FFN_H, FFN_TC), F32)] * 2
        + [pltpu.VMEM((tm, FFN_TC), F32)],
    )(dy, wdown, up, up, cw, cw, cb, cb)


def _ffn_bwd_b(dca, dcg, cw):
    S, F = dca.shape
    nj = F // FFN_TC
    tm = 1024 if S % 1024 == 0 else TM
    nt = S // tm

    def body(a_ref, g_ref, w_ref, o_ref, ybuf):
        j, i = pl.program_id(0), pl.program_id(1)

        @pl.when(i == 0)
        def _():
            ybuf[tm:tm + FFN_H, :] = jnp.zeros((FFN_H, FFN_TC), F32)

        @pl.when(i > 0)
        def _():
            ybuf[tm:tm + FFN_H, :] = ybuf[0:FFN_H, :]

        ybuf[0:tm, :] = jnp.where(j < nj, a_ref[...], g_ref[...]).astype(F32)
        for r0 in range(0, tm, FFN_RC):
            acc = ybuf[pl.ds(r0, FFN_RC), :] * w_ref[FFN_K - 1:FFN_K, :]
            for s in range(1, FFN_K):
                acc = acc + ybuf[pl.ds(r0 + s, FFN_RC), :] * w_ref[FFN_K - 1 - s:FFN_K - s, :]
            o_ref[pl.ds(r0, FFN_RC), :] = acc.astype(BF16)

    return pl.pallas_call(
        body, name="ffn_bwd_b", grid=(2 * nj, nt),
        in_specs=[pl.BlockSpec((tm, FFN_TC), lambda j, i: (nt - 1 - i, jnp.minimum(j, nj - 1))),
                  pl.BlockSpec((tm, FFN_TC), lambda j, i: (nt - 1 - i, jnp.maximum(j - nj, 0))),
                  pl.BlockSpec((FFN_K, FFN_TC), lambda j, i: (0, j))],
        out_specs=pl.BlockSpec((tm, FFN_TC), lambda j, i: (nt - 1 - i, j)), out_shape=SDS((S, 2 * F), BF16),
        scratch_shapes=[pltpu.VMEM((tm + FFN_H, FFN_TC), F32)])(dca, dcg, cw)


def _adamw(w, g, m, v, name):
    R, C = w.shape
    tr = _row_tile(R, max(8, (2 ** 19) // (4 * C) // 8 * 8))

    def body(w_ref, g_ref, m_ref, v_ref, d_ref, nm_ref, nv_ref):
        gv = g_ref[...]
        m2 = ADAM_B1 * m_ref[...] + (1.0 - ADAM_B1) * gv
        v2 = ADAM_B2 * v_ref[...] + (1.0 - ADAM_B2) * jnp.square(gv)
        m_hat = m2 / (1.0 - ADAM_B1 ** ADAM_STEP)
        v_hat = v2 / (1.0 - ADAM_B2 ** ADAM_STEP)
        d_ref[...] = -ADAM_LR * (m_hat / (jnp.sqrt(v_hat) + ADAM_EPS) + ADAM_WD * w_ref[...])
        nm_ref[...] = m2
        nv_ref[...] = v2

    blk = pl.BlockSpec((tr, C), lambda i: (i, 0))
    return pl.pallas_call(
        body, name=name, grid=(R // tr,), in_specs=[blk] * 4, out_specs=[blk] * 3,
        out_shape=[SDS((R, C), F32)] * 3)(w, g, m, v)


HBM_SPEC = pl.BlockSpec(memory_space=pltpu.HBM)
SEM_SPEC = pl.BlockSpec(memory_space=pltpu.SEMAPHORE)
DATAFLOW_EFFECT = pltpu.SideEffectType.DATAFLOW_SIDE_EFFECTING


def _position():
    return lax.axis_index("x"), lax.axis_index("y"), lax.axis_index("c")


def _other_chips(x, y):
    return [(1 - x, y), (x, 1 - y), (1 - x, 1 - y)]


def _all_gather_xy(arrs, name):
    n = len(arrs)
    hbm = pl.BlockSpec(memory_space=pl.ANY)

    def body(*refs):
        ins, outs = refs[:n], refs[n:2 * n]
        send_sems, recv_sems = refs[2 * n:]
        x, y, c = _position()
        me = 2 * x + y
        chips = _other_chips(x, y)

        def rcopy(i, k, src, dst, to):
            return pltpu.make_async_remote_copy(src_ref=src, dst_ref=dst, send_sem=send_sems.at[i, k], recv_sem=recv_sems.at[i, k],
                                                device_id=to, device_id_type=MESH)

        sends = []
        for i in range(n):
            own = rcopy(i, 6, ins[i], outs[i].at[me], (x, y, 1 - c))
            own.start()
            sends.append(own)
        for i in range(n):
            for j, (px, py) in enumerate(chips):
                cp = rcopy(i, j, ins[i].at[c], outs[i].at[me, c], (px, py, c))
                cp.start()
                sends.append(cp)
        for i in range(n):
            for j, (px, py) in enumerate(chips):
                got = outs[i].at[2 * px + py, c]
                rcopy(i, j, ins[i].at[c], got, (x, y, c)).wait_recv()
                fwd = rcopy(i, 3 + j, got, got, (x, y, 1 - c))
                fwd.start()
                sends.append(fwd)
        for i in range(n):
            for j, (px, py) in enumerate(chips):
                theirs = outs[i].at[2 * px + py, 1 - c]
                rcopy(i, 3 + j, theirs, theirs, (x, y, c)).wait_recv()
        for i in range(n):
            rcopy(i, 6, ins[i], outs[i].at[me], (x, y, c)).wait_recv()
        for cp in sends:
            cp.wait_send()

    return pl.pallas_call(
        body, name=name, in_specs=[hbm] * n, out_specs=[hbm] * n,
        out_shape=[SDS((4,) + a.shape, a.dtype) for a in arrs],
        scratch_shapes=[pltpu.SemaphoreType.DMA((n, 7)), pltpu.SemaphoreType.DMA((n, 7))])(*arrs)


def _ag_ici_start(arrs, name):
    n = len(arrs)

    def body(*refs):
        ins, lands = refs[:n], refs[n:2 * n]
        send_sems, recv_sems = refs[2 * n:2 * n + 2]
        token = refs[-1]
        x, y, c = _position()
        for i in range(n):
            for j, (px, py) in enumerate(_other_chips(x, y)):
                pltpu.make_async_remote_copy(src_ref=ins[i].at[c], dst_ref=lands[i].at[2 * x + y, c], send_sem=send_sems.at[3 * i + j],
                                             recv_sem=recv_sems.at[3 * i + j], device_id=(px, py, c), device_id_type=MESH).start()
        token[...] = jnp.zeros_like(token)

    lands = [lax.empty((4,) + a.shape, a.dtype) for a in arrs]
    res = pl.pallas_call(
        body, name=name,
        out_shape=[pltpu.SemaphoreType.DMA((3 * n,)), pltpu.SemaphoreType.DMA((3 * n,))]
        + [pltpu.HBM(a.shape, a.dtype) for a in arrs] + [pltpu.HBM(l.shape, l.dtype) for l in lands] + [SDS((8, 128), F32)],
        in_specs=[HBM_SPEC] * (2 * n), out_specs=[SEM_SPEC, SEM_SPEC] + [HBM_SPEC] * (2 * n) + [pl.BlockSpec(memory_space=pltpu.VMEM)],
        input_output_aliases={i: 2 + i for i in range(2 * n)},
        compiler_params=pltpu.CompilerParams(has_side_effects=DATAFLOW_EFFECT),
    )(*[pltpu.with_memory_space_constraint(a, pltpu.HBM) for a in list(arrs) + lands])
    return res[0], res[1], list(res[2:2 + n]), list(res[2 + n:2 + 2 * n]), res[-1]


def _ag_ici_wait(send_sems, recv_sems, ins, lands, after, name):
    n = len(ins)

    def body(*refs):
        ins_r, lands_r = refs[:n], refs[n:2 * n]
        send_r, recv_r = refs[2 * n:2 * n + 2]
        x, y, c = _position()
        for i in range(n):
            for j, (px, py) in enumerate(_other_chips(x, y)):
                cp = pltpu.make_async_remote_copy(src_ref=ins_r[i].at[c], dst_ref=lands_r[i].at[2 * px + py, c], send_sem=send_r.at[3 * i + j],
                                                  recv_sem=recv_r.at[3 * i + j], device_id=(px, py, c), device_id_type=MESH)
                cp.wait_send()
                cp.wait_recv()

    res = pl.pallas_call(
        body, name=name,
        out_shape=[pltpu.HBM(a.shape, a.dtype) for a in list(ins) + list(lands)],
        in_specs=[HBM_SPEC] * (2 * n) + [SEM_SPEC, SEM_SPEC, pl.BlockSpec(memory_space=pl.ANY)], out_specs=[HBM_SPEC] * (2 * n),
        input_output_aliases={i: i for i in range(2 * n)},
        compiler_params=pltpu.CompilerParams(has_side_effects=DATAFLOW_EFFECT),
    )(*ins, *lands, send_sems, recv_sems, after)
    return list(res[:n]), list(res[n:])


def _ag_finish(arrs, lands, name):
    n = len(arrs)
    hbm = pl.BlockSpec(memory_space=pl.ANY)

    def body(*refs):
        ins, landed, outs = refs[:n], refs[n:2 * n], refs[2 * n:3 * n]
        send_sems, recv_sems = refs[3 * n:]
        x, y, c = _position()
        chips = _other_chips(x, y)
        sends = []
        for i in range(n):
            own = pltpu.make_async_remote_copy(src_ref=ins[i], dst_ref=outs[i].at[2 * x + y], send_sem=send_sems.at[i, 3],
                                               recv_sem=recv_sems.at[i, 3], device_id=(x, y, 1 - c), device_id_type=MESH)
            own.start()
            sends.append(own)
        for i in range(n):
            for j, (px, py) in enumerate(chips):
                fwd = pltpu.make_async_remote_copy(src_ref=landed[i].at[2 * px + py, c], dst_ref=outs[i].at[2 * px + py, c],
                                                   send_sem=send_sems.at[i, j], recv_sem=recv_sems.at[i, j],
                                                   device_id=(x, y, 1 - c), device_id_type=MESH)
                fwd.start()
                sends.append(fwd)
        for i in range(n):
            for j, (px, py) in enumerate(chips):
                theirs = outs[i].at[2 * px + py, 1 - c]
                pltpu.make_async_remote_copy(src_ref=theirs, dst_ref=theirs, send_sem=send_sems.at[i, j], recv_sem=recv_sems.at[i, j],
                                             device_id=(x, y, c), device_id_type=MESH).wait_recv()
        for i in range(n):
            pltpu.make_async_remote_copy(src_ref=ins[i], dst_ref=outs[i].at[2 * x + y], send_sem=send_sems.at[i, 3],
                                         recv_sem=recv_sems.at[i, 3], device_id=(x, y, c), device_id_type=MESH).wait_recv()
        for cp in sends:
            cp.wait_send()

    return pl.pallas_call(
        body, name=name, in_specs=[hbm] * (2 * n), out_specs=[hbm] * n,
        out_shape=[SDS(l.shape, l.dtype) for l in lands],
        input_output_aliases={n + i: i for i in range(n)},
        scratch_shapes=[pltpu.SemaphoreType.DMA((n, 4)), pltpu.SemaphoreType.DMA((n, 4))])(*arrs, *lands)


def _swap_halves(g, name):
    hbm = pl.BlockSpec(memory_space=pl.ANY)

    def body(g_ref, r_ref, send_sem, recv_sem):
        x, y, c = _position()
        cp = pltpu.make_async_remote_copy(src_ref=g_ref.at[1 - c], dst_ref=r_ref, send_sem=send_sem, recv_sem=recv_sem,
                                          device_id=(x, y, 1 - c), device_id_type=MESH)
        cp.start()
        cp.wait()

    return pl.pallas_call(
        body, name=name, in_specs=[hbm], out_specs=hbm, out_shape=SDS(g.shape[1:], g.dtype),
        scratch_shapes=[pltpu.SemaphoreType.DMA, pltpu.SemaphoreType.DMA])(g)


def _exchange_start(s1, name):
    L = s1.shape[1]

    def body(s_ref, land_ref, send_sems, recv_sems, s_thru, land_thru, token):
        del s_thru, land_thru
        x, y, c = _position()
        for j, (px, py) in enumerate(_other_chips(x, y)):
            pltpu.make_async_remote_copy(src_ref=s_ref.at[2 * px + py], dst_ref=land_ref.at[j], send_sem=send_sems.at[j],
                                         recv_sem=recv_sems.at[j], device_id=(px, py, c), device_id_type=MESH).start()
        token[...] = jnp.zeros_like(token)

    return pl.pallas_call(
        body, name=name,
        out_shape=[pltpu.SemaphoreType.DMA((3,)), pltpu.SemaphoreType.DMA((3,)), pltpu.HBM(s1.shape, F32),
                   pltpu.HBM((3, L, 128), F32), SDS((8, 128), F32)],
        in_specs=[HBM_SPEC, HBM_SPEC], out_specs=[SEM_SPEC, SEM_SPEC, HBM_SPEC, HBM_SPEC, pl.BlockSpec(memory_space=pltpu.VMEM)],
        input_output_aliases={0: 2, 1: 3},
        compiler_params=pltpu.CompilerParams(has_side_effects=DATAFLOW_EFFECT),
    )(pltpu.with_memory_space_constraint(s1, pltpu.HBM), pltpu.with_memory_space_constraint(lax.empty((3, L, 128), F32), pltpu.HBM))


def _exchange_wait(send_sems, recv_sems, s1, land, after, name):
    def body(s_ref, land_ref, send_r, recv_r, after_ref, s_out, land_out):
        del after_ref, s_out, land_out
        x, y, c = _position()
        for j, (px, py) in enumerate(_other_chips(x, y)):
            cp = pltpu.make_async_remote_copy(src_ref=s_ref.at[2 * px + py], dst_ref=land_ref.at[j], send_sem=send_r.at[j],
                                              recv_sem=recv_r.at[j], device_id=(px, py, c), device_id_type=MESH)
            cp.wait_send()
            cp.wait_recv()

    return pl.pallas_call(
        body, name=name, out_shape=[pltpu.HBM(s1.shape, F32), pltpu.HBM(land.shape, F32)],
        in_specs=[HBM_SPEC, HBM_SPEC, SEM_SPEC, SEM_SPEC, pl.BlockSpec(memory_space=pl.ANY)], out_specs=[HBM_SPEC, HBM_SPEC],
        input_output_aliases={0: 0, 1: 1},
        compiler_params=pltpu.CompilerParams(has_side_effects=DATAFLOW_EFFECT),
    )(s1, land, send_sems, recv_sems, after)


def _join_halves(f2, name):
    hbm = pl.BlockSpec(memory_space=pl.ANY)

    def body(f_ref, o_ref, send_sem, recv_sem):
        x, y, c = _position()
        cp = pltpu.make_async_remote_copy(src_ref=f_ref.at[c], dst_ref=o_ref.at[c], send_sem=send_sem, recv_sem=recv_sem,
                                          device_id=(x, y, 1 - c), device_id_type=MESH)
        cp.start()
        pltpu.make_async_remote_copy(src_ref=f_ref.at[1 - c], dst_ref=o_ref.at[1 - c], send_sem=send_sem, recv_sem=recv_sem,
                                     device_id=(x, y, 1 - c), device_id_type=MESH).wait_recv()
        cp.wait_send()

    return pl.pallas_call(
        body, name=name, in_specs=[hbm], out_specs=hbm, out_shape=SDS(f2.shape, f2.dtype), input_output_aliases={0: 0},
        scratch_shapes=[pltpu.SemaphoreType.DMA, pltpu.SemaphoreType.DMA])(f2)


def _add_pair(g, r1, c, name):
    R = r1.shape[0]
    tr = _row_tile(R, 2048)

    def body(c_ref, a_ref, b_ref, o_ref):
        del c_ref
        o_ref[...] = a_ref[...] + b_ref[...]

    return pl.pallas_call(
        body, name=name,
        grid_spec=pltpu.PrefetchScalarGridSpec(
            num_scalar_prefetch=1, grid=(R // tr,),
            in_specs=[pl.BlockSpec((None, tr, 128), lambda i, cr: (cr[0], i, 0)), pl.BlockSpec((tr, 128), lambda i, cr: (i, 0))],
            out_specs=pl.BlockSpec((tr, 128), lambda i, cr: (i, 0))),
        out_shape=SDS((R, 128), F32))(c, g, r1)


def _add_four(s1, r2, me_c, name):
    L = s1.shape[1]
    tr = _row_tile(L, 2048)

    def body(m_ref, a_ref, b_ref, o_ref):
        del m_ref
        o_ref[...] = ((a_ref[...] + b_ref[0]) + b_ref[1]) + b_ref[2]

    return pl.pallas_call(
        body, name=name,
        grid_spec=pltpu.PrefetchScalarGridSpec(
            num_scalar_prefetch=1, grid=(L // tr,),
            in_specs=[pl.BlockSpec((None, tr, 128), lambda i, mr: (mr[0], i, 0)), pl.BlockSpec((3, tr, 128), lambda i, mr: (0, i, 0))],
            out_specs=pl.BlockSpec((None, tr, 128), lambda i, mr: (mr[1], i, 0))),
        out_shape=SDS((2, L, 128), F32))(me_c, s1, r2)


def _all_reduce_small(v):
    R = v.shape[0]

    def body(v_ref, o_ref, buf, send_sems, recv_sems):
        x, y, c = _position()
        me = 4 * x + 2 * y + c
        buf[me] = v_ref[...]
        cps = []
        for k in range(1, 8):
            to = (1 - x if k & 4 else x, 1 - y if k & 2 else y, 1 - c if k & 1 else c)
            cp = pltpu.make_async_remote_copy(src_ref=buf.at[me], dst_ref=buf.at[me], send_sem=send_sems.at[k - 1],
                                              recv_sem=recv_sems.at[k - 1], device_id=to, device_id_type=MESH)
            cp.start()
            cps.append(cp)
        for cp in cps:
            cp.wait_send()
        for k in range(1, 8):
            src = 4 * (1 - x if k & 4 else x) + 2 * (1 - y if k & 2 else y) + (1 - c if k & 1 else c)
            pltpu.make_async_remote_copy(src_ref=buf.at[src], dst_ref=buf.at[src], send_sem=send_sems.at[k - 1],
                                         recv_sem=recv_sems.at[k - 1], device_id=(x, y, c), device_id_type=MESH).wait_recv()
        acc = buf[0]
        for k in range(1, 8):
            acc = acc + buf[k]
        o_ref[...] = acc

    vm = pl.BlockSpec(memory_space=pltpu.VMEM)
    return pl.pallas_call(
        body, name="all_reduce_small", in_specs=[vm], out_specs=vm, out_shape=SDS((R, 128), F32),
        scratch_shapes=[pltpu.VMEM((8, R, 128), F32), pltpu.SemaphoreType.DMA((7,)), pltpu.SemaphoreType.DMA((7,))])(v)


def _reduce_begin(grads, tag):
    _, _, c = _position()
    sizes = [int(np.prod(g.shape[1:])) for g in grads]
    total = sum(sizes)
    padded = -(-total // 2048) * 2048
    pieces = [g.reshape(4, -1) for g in grads]
    if padded > total:
        pieces.append(jnp.zeros((4, padded - total), F32))
    L = padded // 256
    g2 = jnp.transpose(jnp.concatenate(pieces, axis=1).reshape(4, 2, L, 128), (1, 0, 2, 3)).reshape(2, 4 * L, 128)
    s1 = _add_pair(g2, _swap_halves(g2, "rs_swap_" + tag), jnp.reshape(c, (1,)).astype(jnp.int32), "rs_add_pair_" + tag)
    send_sems, recv_sems, s1, land, token = _exchange_start(s1.reshape(4, L, 128), "rs_exchange_start_" + tag)
    return (send_sems, recv_sems, s1, land, sizes), token


def _reduce_end(state, after, tag):
    x, y, c = _position()
    send_sems, recv_sems, s1, land, sizes = state
    s1, land = _exchange_wait(send_sems, recv_sems, s1, land, after, "rs_exchange_wait_" + tag)
    f2 = _add_four(s1, land, jnp.stack([2 * x + y, c]).astype(jnp.int32), "rs_add_four_" + tag)
    flat = _join_halves(f2, "rs_join_" + tag).reshape(-1)
    out, off = [], 0
    for n in sizes:
        out.append(flat[off:off + n])
        off += n
    return out


def _halves(a):
    return a.reshape((2, a.shape[0] // 2) + a.shape[1:])


def _pack_rows(vecs, mult=8 * 128):
    flat = jnp.concatenate([jnp.ravel(v) for v in vecs])
    pad = (-flat.shape[0]) % mult
    return jnp.pad(flat, (0, pad)).reshape(-1, 128)


def _unpack_rows(packed, shapes):
    flat = packed.reshape(-1)
    out, off = [], 0
    for s in shapes:
        n = int(np.prod(s))
        out.append(flat[off:off + n].reshape(s))
        off += n
    return out


def _after(a, token):
    return a + token[0, 0]


def _local_step(x, mem, target, sp, w_in, ex):
    S, D = x.shape
    n_in = 4 * w_in.shape[2]
    bw = {"w_in": w_in}
    band, buckets = _bias_static()
    buckets = jnp.asarray(buckets)

    xn = _rmsnorm(x, _after(sp["attn_norm_w"], ex.start_rest()), "rms_in")
    z = _mm_nn(xn, bw["w_in"], BF16, "in_proj")
    qh, kh, vh = _qkv_prep(z, sp["q_norm_w"], sp["k_norm_w"])
    tab = sp["rel_bias_table"].T.reshape(N_GROUPS, HPG, N_BUCKETS)
    bias = _bias_fwd(jnp.pad(tab, ((0, 0), (0, 8 - HPG), (0, 0))), buckets)
    biasm = jnp.where(jnp.asarray(band)[None, None], bias[:, :HPG].reshape(N_GROUPS, HPG, NQ, 2 * NQ), NEG)
    os_, lses = [], []
    for g, (_, dil) in enumerate(ATTN_GROUPS):
        o_g, lse_g = _attn_fwd(qh, kh, vh, biasm[g], g, dil)
        os_.append(o_g)
        lses.append(lse_g)
    attn = _merge_fwd(os_, lses)
    u1, u3 = _conv_fwd(z, sp["conv_dw_w"], sp["conv_dw_b"], sp["conv_ln_w"], sp["conv_ln_b"])
    bw.update(ex.rest_weights(after=attn))
    F2 = 4 * bw["w_up"].shape[2]
    mn, kv, mk, mv = _memkv_fwd(mem, sp["mem_norm_w"], bw["w_mem_kv"], sp["xk_norm_w"])
    oc = _cross_fwd(z, sp["xq_norm_w"], mk, mv)
    h1, hn = _outproj_fwd(x, z, sp["b_gate"], attn, u3, oc, bw["w_attn_o"], bw["w_conv_o"], bw["w_cross_o"], bw["w_out"],
                          sp["ffn_norm_w"])
    up = _mm_nn(hn, bw["w_up"], BF16, "ffn_up")
    act = _ffn_act_fwd(up, sp["ffn_conv_w"], sp["ffn_conv_b"])
    dy, loss_tile = _ffn_down_loss(act, bw["w_down"], h1, target)

    gs, gb = {}, {}
    gb["w_down"] = _mm_tn(act, dy, 1, "dw_down")[0]
    dca, dcg, acca, accg = _ffn_bwd_a(dy, bw["w_down"], up, sp["ffn_conv_w"], sp["ffn_conv_b"])
    tap = lambda acc: jnp.transpose(acc, (1, 0, 2)).reshape(8, F2 // 2)
    ta, tg = tap(acca), tap(accg)
    gs["ffn_conv_w"] = jnp.concatenate([ta[:FFN_K], tg[:FFN_K]], axis=1)
    gs["ffn_conv_b"] = jnp.concatenate([ta[FFN_K:FFN_K + 1], tg[FFN_K:FFN_K + 1]], axis=1)
    dup = _ffn_bwd_b(dca, dcg, sp["ffn_conv_w"])
    gb["w_up"] = _mm_tn(hn, dup, 4, "dw_up")
    tok = ex.reduce_begin("a", ("w_down", "w_up"), gb)
    dh1, gs["ffn_norm_w"] = _norm_in_bwd(dup, bw["w_up"], h1, _after(sp["ffn_norm_w"], tok), dy, "ffn_in_bwd")
    dz, dattn, du3, doc, dya, dyc, dyx, merged, gs["b_gate"] = _outproj_bwd(
        dh1, z, sp["b_gate"], attn, u3, oc, bw["w_attn_o"], bw["w_conv_o"], bw["w_cross_o"], bw["w_out"], n_in)
    gb["w_out"] = _mm_tn(merged, dh1, 1, "dw_out")[0]
    gb["w_attn_o"] = _mm_tn(attn, dya, 4, "dw_attn_o")
    gb["w_conv_o"] = _mm_tn(u3, dyc, 4, "dw_conv_o")
    gb["w_cross_o"] = _mm_tn(oc, dyx, 4, "dw_cross_o")
    dz, dmk, dmv, gs["xq_norm_w"] = _cross_bwd(dz, doc, z, sp["xq_norm_w"], mk, mv)
    gb["w_mem_kv"], gs["xk_norm_w"], gs["mem_norm_w"] = _memkv_bwd(
        dmk, dmv, kv, mem, mn, sp["mem_norm_w"], bw["w_mem_kv"], sp["xk_norm_w"])
    ex.reduce_end("a", after=gs["mem_norm_w"])
    tok = ex.reduce_begin("b", ("w_out", "w_attn_o", "w_conv_o", "w_cross_o", "w_mem_kv"), gb)
    du1, cacc = _conv_bwd_a(du3, u1, z, _after(sp["conv_ln_w"], tok), sp["conv_ln_b"])
    gs["conv_dw_w"], gs["conv_dw_b"] = cacc[:CONV_K], cacc[32:33]
    gs["conv_ln_w"], gs["conv_ln_b"] = cacc[33:34], cacc[34:35]
    dz = _conv_bwd_b(dz, du1, z, sp["conv_dw_w"])
    wg, dah, dhb = _merge_bwd(dattn, os_, lses)
    dqs, dks, dvs, dsbs = [], [], [], []
    for g, (_, dil) in enumerate(ATTN_GROUPS):
        dq_g, dk_g, dv_g, dsb_g = _attn_bwd(qh, kh, vh, biasm[g], dah, wg[g], dhb, lses[g], g, dil)
        dqs.append(dq_g)
        dks.append(dk_g)
        dvs.append(dv_g)
        dsbs.append(dsb_g.reshape(HPG, NQ * 2 * NQ))
    dtab = _bias_bwd(jnp.pad(jnp.stack(dsbs), ((0, 0), (0, 8 - HPG), (0, 0))), buckets)
    gs["rel_bias_table"] = dtab[:, :HPG].reshape(N_GROUPS * HPG, N_BUCKETS).T
    dz, gs["q_norm_w"], gs["k_norm_w"] = _qkv_bwd(dz, z, dqs, dks, dvs, sp["q_norm_w"], sp["k_norm_w"])
    ex.reduce_end("b", after=gs["q_norm_w"])
    gb["w_in"] = _mm_tn(xn, dz, 4, "dw_in")
    tok = ex.reduce_begin("c", ("w_in",), gb)
    dx, gs["attn_norm_w"] = _norm_in_bwd(dz, bw["w_in"], x, _after(sp["attn_norm_w"], tok), dh1, "in_bwd")
    ex.reduce_end("c", after=gs["attn_norm_w"])
    return loss_tile, dx, gs, gb


SMALL = ("rel_bias_table", "attn_norm_w", "b_gate", "q_norm_w", "k_norm_w", "conv_dw_w", "conv_dw_b", "conv_ln_w", "conv_ln_b",
         "mem_norm_w", "xq_norm_w", "xk_norm_w", "ffn_norm_w", "ffn_conv_w", "ffn_conv_b")
SMALL_SHARDED = ("conv_dw_w", "ffn_conv_w")
BIG_COL = ("w_in", "w_attn_o", "w_conv_o", "w_cross_o", "w_up")
BIG_ROW = ("w_mem_kv", "w_out", "w_down")
BIG = BIG_COL + BIG_ROW
WEIGHTS = ("rel_bias_table", "attn_norm_w", "w_in", "b_gate", "q_norm_w", "k_norm_w", "w_attn_o", "conv_dw_w", "conv_dw_b",
           "conv_ln_w", "conv_ln_b", "w_conv_o", "mem_norm_w", "w_mem_kv", "xq_norm_w", "xk_norm_w", "w_cross_o", "w_out",
           "ffn_norm_w", "w_up", "ffn_conv_w", "ffn_conv_b", "w_down")


class _Exchanges:
    REST = tuple(k for k in BIG if k != "w_in")

    def __init__(self, w):
        self.w = w
        self.pending = {}
        self.reduced = {}

    def _whole(self, k, ga):
        ga = ga.reshape((4,) + self.w[k].shape)
        return ga if k in BIG_COL else ga.reshape((4 * self.w[k].shape[0],) + self.w[k].shape[1:])

    def first_weights(self):
        local = [_halves(self.w["w_in"].astype(BF16))]
        for k in SMALL_SHARDED:
            flat = jnp.ravel(self.w[k])
            local.append(jnp.pad(flat, (0, (-flat.shape[0]) % 2048)).reshape(2, -1, 128))
        gathered = _all_gather_xy(local, "gather_first")
        small = {}
        for k, ga in zip(SMALL_SHARDED, gathered[1:]):
            r, cdim = self.w[k].shape
            parts = ga.reshape(4, -1)[:, :r * cdim].reshape(4, r, cdim)
            small[k] = jnp.transpose(parts, (1, 0, 2)).reshape(r, 4 * cdim)
        return self._whole("w_in", gathered[0]), small

    def start_rest(self):
        local = [_halves(self.w[k].astype(BF16)) for k in self.REST]
        send_sems, recv_sems, ins, lands, token = _ag_ici_start(local, "gather_rest_start")
        self.pending["rest"] = (send_sems, recv_sems, ins, lands)
        return token

    def rest_weights(self, after):
        send_sems, recv_sems, ins, lands = self.pending.pop("rest")
        ins, lands = _ag_ici_wait(send_sems, recv_sems, ins, lands, after, "gather_rest_wait")
        gathered = _ag_finish(ins, lands, "gather_rest_finish")
        return {k: self._whole(k, ga) for k, ga in zip(self.REST, gathered)}

    def reduce_begin(self, tag, names, gb):
        parts = [gb[k].reshape((4, -1)) for k in names]
        self.pending[tag], token = _reduce_begin(parts, tag)
        self.pending[tag] += (names,)
        return token

    def reduce_end(self, tag, after):
        *state, names = self.pending.pop(tag)
        for k, flat in zip(names, _reduce_end(tuple(state), after, tag)):
            self.reduced[k] = flat.reshape(self.w[k].shape)


def _step(x, mem, target, w, m, v):
    xi, yi, _ = _position()
    shard = 2 * xi + yi
    ex = _Exchanges(w)
    w_in, small_gathered = ex.first_weights()
    sp = {k: w[k] for k in SMALL if k not in SMALL_SHARDED}
    sp.update(small_gathered)

    loss_tile, dx, gs, _ = _local_step(x, mem, target, sp, w_in, ex)
    g_big = ex.reduced

    small_shapes = [(8, 128)] + [gs[k].shape for k in SMALL]
    red = _unpack_rows(_all_reduce_small(_pack_rows([loss_tile] + [gs[k] for k in SMALL])), small_shapes)
    loss = red[0][0, 0]
    g_small = dict(zip(SMALL, red[1:]))
    for k in SMALL_SHARDED:
        cdim = w[k].shape[1]
        g_small[k] = lax.dynamic_slice_in_dim(g_small[k], shard * cdim, cdim, axis=1)

    grads, delta, new_m, new_v = {}, {}, {}, {}
    for k in BIG:
        grads[k] = g_big[k]
        delta[k], new_m[k], new_v[k] = _adamw(w[k], g_big[k], m[k], v[k], "adamw_" + k)
    shapes = [w[k].shape for k in SMALL]
    packed = [_pack_rows([d[k] for k in SMALL]) for d in (w, g_small, m, v)]
    outs = _adamw(*packed, "adamw_small")
    for dst, pk in zip((delta, new_m, new_v), outs):
        dst.update(zip(SMALL, _unpack_rows(pk, shapes)))
    grads.update(g_small)
    return loss, dx, grads, delta, new_m, new_v


def kernel(x, mem, rel_bias_table, attn_norm_w, w_in, b_gate, q_norm_w, k_norm_w, w_attn_o, conv_dw_w, conv_dw_b, conv_ln_w, conv_ln_b, w_conv_o, mem_norm_w, w_mem_kv, xq_norm_w, xk_norm_w, w_cross_o, w_out, ffn_norm_w, w_up, ffn_conv_w, ffn_conv_b, w_down, loss_target, m_rel_bias_table, m_attn_norm_w, m_w_in, m_b_gate, m_q_norm_w, m_k_norm_w, m_w_attn_o, m_conv_dw_w, m_conv_dw_b, m_conv_ln_w, m_conv_ln_b, m_w_conv_o, m_mem_norm_w, m_w_mem_kv, m_xq_norm_w, m_xk_norm_w, m_w_cross_o, m_w_out, m_ffn_norm_w, m_w_up, m_ffn_conv_w, m_ffn_conv_b, m_w_down, v_rel_bias_table, v_attn_norm_w, v_w_in, v_b_gate, v_q_norm_w, v_k_norm_w, v_w_attn_o, v_conv_dw_w, v_conv_dw_b, v_conv_ln_w, v_conv_ln_b, v_w_conv_o, v_mem_norm_w, v_w_mem_kv, v_xq_norm_w, v_xk_norm_w, v_w_cross_o, v_w_out, v_ffn_norm_w, v_w_up, v_ffn_conv_w, v_ffn_conv_b, v_w_down):
    args = locals()
    def block(name, k):
        a = args[name] if k == "rel_bias_table" else args[name][0]
        return a.reshape(1, -1) if a.ndim == 1 else a

    w = {k: block(k, k) for k in WEIGHTS}
    m = {k: block("m_" + k, k) for k in WEIGHTS}
    v = {k: block("v_" + k, k) for k in WEIGHTS}
    loss, dx, grads, delta, new_m, new_v = _step(x[0], mem[0], loss_target[0], w, m, v)
    out = [loss, dx[None]]
    for d in (grads, delta, new_m, new_v):
        for k in WEIGHTS:
            out.append(d[k].reshape(args[k].shape))
    return tuple(out)
```

```python
import functools
import math

import numpy as np
import jax
import jax.numpy as jnp
from jax import lax
from jax.experimental import pallas as pl
from jax.experimental.pallas import tpu as pltpu

F32, BF16 = jnp.float32, jnp.bfloat16
SDS = jax.ShapeDtypeStruct
MESH = pl.DeviceIdType.MESH

HEAD = 128
N_GROUPS, HPG = 3, 4
ATTN_GROUPS = ((128, 1), (512, 4), (2048, 16))
NQ = 128
QKV_W = N_GROUPS * HPG * HEAD
CH = 512
CONV_K, FFN_K = 31, 3
N_BUCKETS, MAX_DIST = 32, 2048
RMS_EPS, LN_EPS = 1e-6, 1e-5
O_Q, O_K, O_V, O_CV, O_CG, O_XQ, O_G = 0, QKV_W, 2 * QKV_W, 3 * QKV_W, 3 * QKV_W + CH, 3 * QKV_W + 2 * CH, 3 * QKV_W + 3 * CH
ADAM_LR, ADAM_B1, ADAM_B2, ADAM_EPS, ADAM_WD, ADAM_STEP = 0.001, 0.9, 0.999, 1e-08, 0.01, 10
NEG = -1e30
SCALE = HEAD ** -0.5
TM = 512
ATT_RB = 2048
NT_DIMS = (((1,), (1,)), ((), ()))
TN_DIMS = (((0,), (0,)), ((), ()))


CONV_RC = 16
FFN_RC = 32


def _sig(v):
    return 0.5 * jnp.tanh(0.5 * v) + 0.5


def _fold8(v):
    acc = v[0:8]
    for r in range(8, v.shape[0], 8):
        acc = acc + v[r:r + 8]
    return acc


def _row_tile(rows, cap, mult=8):
    best = None
    for t in range(mult, min(rows, cap) + 1, mult):
        if rows % t == 0:
            best = t
    return best if best is not None else rows


def _full(shape):
    n = len(shape)
    return pl.BlockSpec(shape, lambda *a: (0,) * n)


def _rmsnorm(x, w, name):
    S, D = x.shape

    def body(x_ref, w_ref, o_ref):
        xv = x_ref[...]
        r = lax.rsqrt(jnp.mean(xv * xv, axis=-1, keepdims=True) + RMS_EPS)
        o_ref[...] = (xv * r * w_ref[...]).astype(BF16)

    return pl.pallas_call(
        body, name=name, grid=(S // TM,),
        in_specs=[pl.BlockSpec((TM, D), lambda i: (i, 0)), _full((1, D))],
        out_specs=pl.BlockSpec((TM, D), lambda i: (i, 0)),
        out_shape=SDS((S, D), BF16))(x, w)


def _mm_nn(a, b, out_dtype, name):
    M, K = a.shape
    G, _, n = b.shape

    def body(a_ref, b_ref, o_ref):
        o_ref[...] = jnp.dot(a_ref[...].astype(BF16), b_ref[...], preferred_element_type=F32).astype(out_dtype)

    return pl.pallas_call(
        body, name=name, grid=(G, M // TM),
        in_specs=[pl.BlockSpec((TM, K), lambda g, i: (i, 0)), pl.BlockSpec((None, K, n), lambda g, i: (g, 0, 0))],
        out_specs=pl.BlockSpec((TM, n), lambda g, i: (i, g)),
        out_shape=SDS((M, G * n), out_dtype))(a, b)


def _mm_tn(a, b, G, name):
    S, Ka = a.shape
    n = b.shape[1] // G
    if G * n <= 1024:
        tk = 2048 if S % 2048 == 0 else TM

        def wide_body(a_ref, b_ref, o_ref, acc):
            k = pl.program_id(0)

            @pl.when(k == 0)
            def _():
                acc[...] = jnp.zeros_like(acc)

            acc[...] += lax.dot_general(a_ref[...].astype(BF16), b_ref[...].astype(BF16), TN_DIMS, preferred_element_type=F32)

            @pl.when(k == S // tk - 1)
            def _():
                for g in range(G):
                    o_ref[g] = acc[:, g * n:(g + 1) * n]

        return pl.pallas_call(
            wide_body, name=name, grid=(S // tk,),
            in_specs=[pl.BlockSpec((tk, Ka), lambda k: (k, 0)), pl.BlockSpec((tk, G * n), lambda k: (k, 0))],
            out_specs=_full((G, Ka, n)), out_shape=SDS((G, Ka, n), F32),
            scratch_shapes=[pltpu.VMEM((Ka, G * n), F32)])(a, b)

    tka = Ka
    while tka * n * 4 > 10 * 2 ** 20 and tka % 256 == 0:
        tka //= 2

    def body(a_ref, b_ref, o_ref):
        @pl.when(pl.program_id(2) == 0)
        def _():
            o_ref[...] = jnp.zeros_like(o_ref)
        o_ref[...] += lax.dot_general(a_ref[...].astype(BF16), b_ref[...].astype(BF16), TN_DIMS, preferred_element_type=F32)

    return pl.pallas_call(
        body, name=name, grid=(G, Ka // tka, S // TM),
        in_specs=[pl.BlockSpec((TM, tka), lambda g, i, k: (k, i)), pl.BlockSpec((TM, n), lambda g, i, k: (k, g))],
        out_specs=pl.BlockSpec((None, tka, n), lambda g, i, k: (g, i, 0)),
        out_shape=SDS((G, Ka, n), F32))(a, b)


NORM_RC = 16


def _norm_in_bwd(a, w, xin, nw, resid, name):
    S, K = xin.shape
    G, _, n = w.shape
    tm = 1024 if S % 1024 == 0 else TM
    split = 2 if n > 1408 and (n // 2) % 128 == 0 else 1
    n2, steps = n // split, G * split
    w_index = (lambda i, g: (jnp.right_shift(g, 1), 0, jnp.bitwise_and(g, 1))) if split == 2 else (lambda i, g: (g, 0, 0))

    def body(a_ref, w_ref, x_ref, nw_ref, r_ref, o_ref, dnw_ref, acc, part):
        i, g = pl.program_id(0), pl.program_id(1)

        @pl.when((i == 0) & (g == 0))
        def _():
            part[...] = jnp.zeros_like(part)

        @pl.when(g == 0)
        def _():
            acc[...] = jnp.zeros_like(acc)

        acc[...] += lax.dot_general(a_ref[...], w_ref[...], NT_DIMS, preferred_element_type=F32)

        @pl.when(g == steps - 1)
        def _():
            for r0 in range(0, tm, NORM_RC):
                rows = pl.ds(r0, NORM_RC)
                dn = acc[rows, :]
                xv = x_ref[rows, :]
                r = lax.rsqrt(jnp.mean(xv * xv, axis=-1, keepdims=True) + RMS_EPS)
                xhat = xv * r
                dyw = dn * nw_ref[...]
                o_ref[rows, :] = r_ref[rows, :] + r * (dyw - xhat * jnp.mean(dyw * xhat, axis=-1, keepdims=True))
                part[...] += _fold8(dn * xhat)

        @pl.when((i == S // tm - 1) & (g == steps - 1))
        def _():
            dnw_ref[...] = jnp.sum(part[...], axis=0, keepdims=True)

    return pl.pallas_call(
        body, name=name, grid=(S // tm, steps),
        in_specs=[pl.BlockSpec((tm, n2), lambda i, g: (i, g)), pl.BlockSpec((None, K, n2), w_index),
                  pl.BlockSpec((tm, K), lambda i, g: (i, 0)), _full((1, K)), pl.BlockSpec((tm, K), lambda i, g: (i, 0))],
        out_specs=[pl.BlockSpec((tm, K), lambda i, g: (i, 0)), _full((1, K))],
        out_shape=[SDS((S, K), F32), SDS((1, K), F32)],
        scratch_shapes=[pltpu.VMEM((tm, K), F32), pltpu.VMEM((8, K), F32)])(a, w, xin, nw, resid)


def _t5_bucket_np(dist):
    max_exact = N_BUCKETS // 2
    d = np.maximum(dist.astype(np.float32), np.float32(1.0))
    large = max_exact + (np.log(d / np.float32(max_exact)) / np.float32(math.log(MAX_DIST / max_exact))
                         * np.float32(N_BUCKETS - max_exact)).astype(np.int32)
    large = np.minimum(large, N_BUCKETS - 1)
    return np.where(dist < max_exact, dist, large).astype(np.int32)


def _bias_static():
    qi = np.arange(NQ)[:, None]
    kj = np.arange(2 * NQ)[None, :]
    step = qi + NQ - kj
    band = (step >= 0) & (step <= NQ)
    buckets = np.stack([_t5_bucket_np(np.clip(step, 0, None) * dil).reshape(1, -1) for _, dil in ATTN_GROUPS])
    return band, buckets


def _bias_fwd(table_t, buckets):
    nb = buckets.shape[-1]

    def body(t_ref, b_ref, o_ref):
        oh = (b_ref[...] == lax.broadcasted_iota(jnp.int32, (N_BUCKETS, nb), 0)).astype(F32)
        o_ref[...] = jnp.dot(t_ref[...], oh, preferred_element_type=F32, precision=lax.Precision.HIGHEST)

    return pl.pallas_call(
        body, name="bias_fwd", grid=(N_GROUPS,),
        in_specs=[pl.BlockSpec((None, 8, N_BUCKETS), lambda g: (g, 0, 0)), pl.BlockSpec((None, 1, nb), lambda g: (g, 0, 0))],
        out_specs=pl.BlockSpec((None, 8, nb), lambda g: (g, 0, 0)),
        out_shape=SDS((N_GROUPS, 8, nb), F32))(table_t, buckets)


def _bias_bwd(dsb, buckets):
    nb = buckets.shape[-1]

    def body(d_ref, b_ref, o_ref):
        oh = (b_ref[...] == lax.broadcasted_iota(jnp.int32, (N_BUCKETS, nb), 0)).astype(F32)
        o_ref[...] = lax.dot_general(d_ref[...], oh, NT_DIMS, preferred_element_type=F32, precision=lax.Precision.HIGHEST)

    return pl.pallas_call(
        body, name="bias_bwd", grid=(N_GROUPS,),
        in_specs=[pl.BlockSpec((None, 8, nb), lambda g: (g, 0, 0)), pl.BlockSpec((None, 1, nb), lambda g: (g, 0, 0))],
        out_specs=pl.BlockSpec((None, 8, N_BUCKETS), lambda g: (g, 0, 0)),
        out_shape=SDS((N_GROUPS, 8, N_BUCKETS), F32))(dsb, buckets)


def _qkv_prep(z, qw, kw):
    S = z.shape[0]
    nh = N_GROUPS * HPG

    def body(zq, zk, zv, qw_ref, kw_ref, qh, kh, vh):
        for h in range(nh):
            g = h // HPG
            sl = slice(h * HEAD, (h + 1) * HEAD)
            xq = zq[:, sl].astype(F32)
            qh[h] = xq * lax.rsqrt(jnp.mean(xq * xq, axis=-1, keepdims=True) + RMS_EPS) * qw_ref[g:g + 1, :]
            xk = zk[:, sl].astype(F32)
            kh[h] = xk * lax.rsqrt(jnp.mean(xk * xk, axis=-1, keepdims=True) + RMS_EPS) * kw_ref[g:g + 1, :]
            vh[h] = zv[:, sl].astype(F32)

    hm = pl.BlockSpec((nh, TM, HEAD), lambda i: (0, i, 0))
    return pl.pallas_call(
        body, name="qkv_prep", grid=(S // TM,),
        in_specs=[pl.BlockSpec((TM, QKV_W), lambda i: (i, 0)), pl.BlockSpec((TM, QKV_W), lambda i: (i, 1)),
                  pl.BlockSpec((TM, QKV_W), lambda i: (i, 2)), _full((N_GROUPS, HEAD)), _full((N_GROUPS, HEAD))],
        out_specs=[hm, hm, hm],
        out_shape=[SDS((nh, S, HEAD), F32)] * 3)(z, z, z, qw, kw)


def _rows(start, d):
    return pl.ds(start, NQ) if d == 1 else pl.ds(start, NQ, stride=d)


def _attn_fwd(qh, kh, vh, biasm, g, d):
    S = qh.shape[1]
    RB = ATT_RB
    nbk, nq = S // RB, RB // (NQ * d)

    def body(q_ref, k_ref, v_ref, bias_ref, o_ref, lse_ref, kbuf, vbuf):
        b = pl.program_id(1)

        @pl.when(b == 0)
        def _():
            kbuf[0:RB, :] = jnp.zeros((RB, HEAD), F32)
            vbuf[0:RB, :] = jnp.zeros((RB, HEAD), F32)

        @pl.when(b > 0)
        def _():
            kbuf[0:RB, :] = kbuf[RB:2 * RB, :]
            vbuf[0:RB, :] = vbuf[RB:2 * RB, :]

        kbuf[RB:2 * RB, :] = k_ref[...]
        vbuf[RB:2 * RB, :] = v_ref[...]
        bias = bias_ref[...]
        col = lax.broadcasted_iota(jnp.int32, (NQ, 2 * NQ), 1)

        for qb in range(nq):
            def unit(r, carry, qb=qb):
                qs = qb * NQ * d + r
                q = q_ref[_rows(qs, d), :].astype(BF16)
                kw = jnp.concatenate([kbuf[_rows(RB + qs - NQ * d, d), :], kbuf[_rows(RB + qs, d), :]], axis=0).astype(BF16)
                vw = jnp.concatenate([vbuf[_rows(RB + qs - NQ * d, d), :], vbuf[_rows(RB + qs, d), :]], axis=0).astype(BF16)
                s = lax.dot_general(q, kw, NT_DIMS, preferred_element_type=F32) * SCALE + bias
                if qb == 0:
                    s = jnp.where((col < NQ) & (b == 0), NEG, s)
                m = jnp.max(s, axis=-1, keepdims=True)
                p = jnp.exp(s - m)
                l = jnp.sum(p, axis=-1, keepdims=True)
                o = jnp.dot(p.astype(BF16), vw, preferred_element_type=F32) / l
                o_ref[_rows(qs, d), :] = o
                lse_ref[_rows(qs, d), :] = jnp.broadcast_to(m + jnp.log(l), (NQ, HEAD))
                return carry

            if d == 1:
                unit(0, 0)
            else:
                lax.fori_loop(0, d, unit, 0)

    blk = lambda f: pl.BlockSpec((None, RB, HEAD), f)
    return pl.pallas_call(
        body, name=f"attn_fwd_g{g}", grid=(HPG, nbk),
        in_specs=[blk(lambda h, b: (HPG * g + h, b, 0))] * 3 + [pl.BlockSpec((None, NQ, 2 * NQ), lambda h, b: (h, 0, 0))],
        out_specs=[blk(lambda h, b: (h, b, 0))] * 2,
        out_shape=[SDS((HPG, S, HEAD), F32)] * 2,
        scratch_shapes=[pltpu.VMEM((2 * RB, HEAD), F32)] * 2)(qh, kh, vh, biasm)


def _attn_bwd(qh, kh, vh, biasm, da, wg, dh, lse, g, d):
    S = qh.shape[1]
    RB = ATT_RB
    nbk, nq = S // RB, RB // (NQ * d)

    def body(q_ref, k_ref, v_ref, bias_ref, da_ref, wg_ref, dh_ref, lse_ref,
             dq_ref, dk_ref, dv_ref, dsb_ref, kbuf, vbuf, dkbuf, dvbuf):
        b = pl.program_id(1)
        zero = jnp.zeros((RB, HEAD), F32)

        @pl.when(b == 0)
        def _():
            kbuf[0:RB, :] = zero
            vbuf[0:RB, :] = zero
            dkbuf[0:RB, :] = zero
            dvbuf[0:RB, :] = zero
            dsb_ref[...] = jnp.zeros_like(dsb_ref)

        @pl.when(b > 0)
        def _():
            kbuf[0:RB, :] = kbuf[RB:2 * RB, :]
            vbuf[0:RB, :] = vbuf[RB:2 * RB, :]
            dkbuf[0:RB, :] = dkbuf[RB:2 * RB, :]
            dvbuf[0:RB, :] = dvbuf[RB:2 * RB, :]

        dkbuf[RB:2 * RB, :] = zero
        dvbuf[RB:2 * RB, :] = zero

        @pl.when(b < nbk)
        def _():
            kbuf[RB:2 * RB, :] = k_ref[...]
            vbuf[RB:2 * RB, :] = v_ref[...]
            bias = bias_ref[...]
            col = lax.broadcasted_iota(jnp.int32, (NQ, 2 * NQ), 1)

            for qb in range(nq):
                def unit(r, carry, qb=qb):
                    qs = qb * NQ * d + r
                    prev, cur = _rows(RB + qs - NQ * d, d), _rows(RB + qs, d)
                    q = q_ref[_rows(qs, d), :].astype(BF16)
                    kw = jnp.concatenate([kbuf[prev, :], kbuf[cur, :]], axis=0).astype(BF16)
                    vw = jnp.concatenate([vbuf[prev, :], vbuf[cur, :]], axis=0).astype(BF16)
                    s = lax.dot_general(q, kw, NT_DIMS, preferred_element_type=F32) * SCALE + bias
                    if qb == 0:
                        s = jnp.where((col < NQ) & (b == 0), NEG, s)
                    p = jnp.exp(s - lse_ref[_rows(qs, d), :][:, 0:1])
                    w = wg_ref[_rows(qs, d), :]
                    do = (da_ref[_rows(qs, d), :] * w).astype(BF16)
                    dp = lax.dot_general(do, vw, NT_DIMS, preferred_element_type=F32)
                    ds = p * (dp - w[:, 0:1] * dh_ref[_rows(qs, d), :][:, 0:1])
                    dsb_ref[...] += ds
                    dsb = ds.astype(BF16)
                    dq_ref[_rows(qs, d), :] = jnp.dot(dsb, kw, preferred_element_type=F32) * SCALE
                    dkw = lax.dot_general(dsb, q, TN_DIMS, preferred_element_type=F32) * SCALE
                    dvw = lax.dot_general(p.astype(BF16), do, TN_DIMS, preferred_element_type=F32)
                    dkbuf[prev, :] += dkw[0:NQ, :]
                    dkbuf[cur, :] += dkw[NQ:2 * NQ, :]
                    dvbuf[prev, :] += dvw[0:NQ, :]
                    dvbuf[cur, :] += dvw[NQ:2 * NQ, :]
                    return carry

                if d == 1:
                    unit(0, 0)
                else:
                    lax.fori_loop(0, d, unit, 0)

        dk_ref[...] = dkbuf[0:RB, :]
        dv_ref[...] = dvbuf[0:RB, :]

    blk = lambda f: pl.BlockSpec((None, RB, HEAD), f)
    cur_g = blk(lambda h, b: (HPG * g + h, jnp.minimum(b, nbk - 1), 0))
    cur = blk(lambda h, b: (h, jnp.minimum(b, nbk - 1), 0))
    prv = blk(lambda h, b: (h, jnp.maximum(b - 1, 0), 0))
    sq = pl.BlockSpec((None, NQ, 2 * NQ), lambda h, b: (h, 0, 0))
    return pl.pallas_call(
        body, name=f"attn_bwd_g{g}", grid=(HPG, nbk + 1),
        in_specs=[cur_g, cur_g, cur_g, sq, cur, cur, cur, cur],
        out_specs=[cur, prv, prv, sq],
        out_shape=[SDS((HPG, S, HEAD), F32)] * 3 + [SDS((HPG, NQ, 2 * NQ), F32)],
        scratch_shapes=[pltpu.VMEM((2 * RB, HEAD), F32)] * 4)(qh, kh, vh, biasm, da, wg, dh, lse)


def _merge_weights(l0, l1, l2):
    m = jnp.maximum(jnp.maximum(l0, l1), l2)
    e0, e1, e2 = jnp.exp(l0 - m), jnp.exp(l1 - m), jnp.exp(l2 - m)
    inv = 1.0 / (e0 + e1 + e2)
    return e0 * inv, e1 * inv, e2 * inv


def _merge_fwd(os_, lses):
    S = os_[0].shape[1]

    def body(o0, o1, o2, l0, l1, l2, a_ref):
        for h in range(HPG):
            w0, w1, w2 = _merge_weights(l0[h], l1[h], l2[h])
            a_ref[:, h * HEAD:(h + 1) * HEAD] = (w0 * o0[h] + w1 * o1[h] + w2 * o2[h]).astype(BF16)

    hm = pl.BlockSpec((HPG, TM, HEAD), lambda i: (0, i, 0))
    return pl.pallas_call(
        body, name="merge_fwd", grid=(S // TM,), in_specs=[hm] * 6,
        out_specs=pl.BlockSpec((TM, CH), lambda i: (i, 0)),
        out_shape=SDS((S, CH), BF16))(*os_, *lses)


def _merge_bwd(dattn, os_, lses):
    S = dattn.shape[0]

    def body(da_ref, o0, o1, o2, l0, l1, l2, w0_ref, w1_ref, w2_ref, dah_ref, dh_ref):
        for h in range(HPG):
            w = _merge_weights(l0[h], l1[h], l2[h])
            attn = w[0] * o0[h] + w[1] * o1[h] + w[2] * o2[h]
            da = da_ref[:, h * HEAD:(h + 1) * HEAD]
            for w_ref, wv in zip((w0_ref, w1_ref, w2_ref), w):
                w_ref[h] = wv
            dah_ref[h] = da
            dh_ref[h] = jnp.broadcast_to(jnp.sum(da * attn, axis=-1, keepdims=True), (TM, HEAD))

    hm = pl.BlockSpec((HPG, TM, HEAD), lambda i: (0, i, 0))
    res = pl.pallas_call(
        body, name="merge_bwd", grid=(S // TM,),
        in_specs=[pl.BlockSpec((TM, CH), lambda i: (i, 0))] + [hm] * 6,
        out_specs=[hm] * 5, out_shape=[SDS((HPG, S, HEAD), F32)] * 5)(dattn, *os_, *lses)
    return res[0:3], res[3], res[4]


def _qkv_bwd(dz, z, dqs, dks, dvs, qw, kw):
    S = z.shape[0]
    nh = N_GROUPS * HPG

    def body(dz_in, zq, zk, *refs):
        del dz_in
        dq_refs, dk_refs, dv_refs = refs[0:3], refs[3:6], refs[6:9]
        qw_ref, kw_ref, dz_ref, dqw_ref, dkw_ref = refs[9:]

        @pl.when(pl.program_id(0) == 0)
        def _():
            dqw_ref[...] = jnp.zeros_like(dqw_ref)
            dkw_ref[...] = jnp.zeros_like(dkw_ref)

        def nbwd(xr, dy, wr, dwr, h, off):
            g = h // HPG
            x = xr[:, h * HEAD:(h + 1) * HEAD].astype(F32)
            r = lax.rsqrt(jnp.mean(x * x, axis=-1, keepdims=True) + RMS_EPS)
            xhat = x * r
            dyw = dy * wr[g:g + 1, :]
            dz_ref[:, off + h * HEAD:off + (h + 1) * HEAD] = (
                r * (dyw - xhat * jnp.mean(dyw * xhat, axis=-1, keepdims=True))).astype(BF16)
            dwr[g:g + 1, :] += jnp.sum(dy * xhat, axis=0, keepdims=True)

        for h in range(nh):
            g, hh = h // HPG, h % HPG
            nbwd(zq, dq_refs[g][hh], qw_ref, dqw_ref, h, O_Q)
            nbwd(zk, dk_refs[g][hh], kw_ref, dkw_ref, h, O_K)
            dz_ref[:, O_V + h * HEAD:O_V + (h + 1) * HEAD] = dv_refs[g][hh].astype(BF16)

    hm = pl.BlockSpec((HPG, TM, HEAD), lambda i: (0, i, 0))
    return pl.pallas_call(
        body, name="qkv_bwd", grid=(S // TM,),
        in_specs=[pl.BlockSpec(memory_space=pl.ANY), pl.BlockSpec((TM, QKV_W), lambda i: (i, 0)),
                  pl.BlockSpec((TM, QKV_W), lambda i: (i, 1))] + [hm] * 9 + [_full((N_GROUPS, HEAD)), _full((N_GROUPS, HEAD))],
        out_specs=[pl.BlockSpec((TM, 3 * QKV_W), lambda i: (i, 0)), _full((N_GROUPS, HEAD)), _full((N_GROUPS, HEAD))],
        out_shape=[SDS(dz.shape, BF16), SDS((N_GROUPS, HEAD), F32), SDS((N_GROUPS, HEAD), F32)],
        input_output_aliases={0: 0})(dz, z, z, *dqs, *dks, *dvs, qw, kw)


def _conv_fwd(z, cw, cb, lnw, lnb):
    S = z.shape[0]
    H = 32

    def body(zv, zg, cw_ref, cb_ref, lnw_ref, lnb_ref, u1_ref, u3_ref, xbuf):
        i = pl.program_id(0)

        @pl.when(i == 0)
        def _():
            xbuf[0:H, :] = jnp.zeros((H, CH), F32)

        @pl.when(i > 0)
        def _():
            xbuf[0:H, :] = xbuf[TM:TM + H, :]

        xbuf[H:H + TM, :] = zv[...].astype(F32) * _sig(zg[...].astype(F32))
        for r0 in range(0, TM, CONV_RC):
            rows = pl.ds(r0, CONV_RC)
            acc = jnp.broadcast_to(cb_ref[...], (CONV_RC, CH))
            for k in range(CONV_K):
                acc = acc + xbuf[pl.ds(H - (CONV_K - 1) + k + r0, CONV_RC), :] * cw_ref[k:k + 1, :]
            u1_ref[rows, :] = acc
            mu = jnp.mean(acc, axis=-1, keepdims=True)
            xc = acc - mu
            yl = xc * lax.rsqrt(jnp.mean(xc * xc, axis=-1, keepdims=True) + LN_EPS) * lnw_ref[...] + lnb_ref[...]
            u3_ref[rows, :] = (yl * _sig(yl)).astype(BF16)

    row = pl.BlockSpec((TM, CH), lambda i: (i, 0))
    return pl.pallas_call(
        body, name="conv_fwd", grid=(S // TM,),
        in_specs=[pl.BlockSpec((TM, CH), lambda i: (i, O_CV // CH)), pl.BlockSpec((TM, CH), lambda i: (i, O_CG // CH)),
                  _full((CONV_K, CH)), _full((1, CH)), _full((1, CH)), _full((1, CH))],
        out_specs=[row, row], out_shape=[SDS((S, CH), F32), SDS((S, CH), BF16)],
        scratch_shapes=[pltpu.VMEM((TM + H, CH), F32)])(z, z, cw, cb, lnw, lnb)


def _conv_bwd_a(du3, u1, z, lnw, lnb):
    S = z.shape[0]
    H = 32

    nt = S // TM

    def body(du3_ref, u1_ref, zv, zg, lnw_ref, lnb_ref, du1_ref, acc_ref, xbuf, tacc):
        i = pl.program_id(0)

        @pl.when(i == 0)
        def _():
            xbuf[0:H, :] = jnp.zeros((H, CH), F32)
            tacc[...] = jnp.zeros_like(tacc)

        @pl.when(i > 0)
        def _():
            xbuf[0:H, :] = xbuf[TM:TM + H, :]

        xbuf[H:H + TM, :] = zv[...].astype(F32) * _sig(zg[...].astype(F32))
        for r0 in range(0, TM, CONV_RC):
            rows = pl.ds(r0, CONV_RC)
            u1 = u1_ref[rows, :]
            mu = jnp.mean(u1, axis=-1, keepdims=True)
            xc = u1 - mu
            r = lax.rsqrt(jnp.mean(xc * xc, axis=-1, keepdims=True) + LN_EPS)
            yhat = xc * r
            yl = yhat * lnw_ref[...] + lnb_ref[...]
            sg = _sig(yl)
            dyl = du3_ref[rows, :] * (sg * (1.0 + yl * (1.0 - sg)))
            dyh = dyl * lnw_ref[...]
            du1 = r * (dyh - jnp.mean(dyh, axis=-1, keepdims=True) - yhat * jnp.mean(dyh * yhat, axis=-1, keepdims=True))
            du1_ref[rows, :] = du1
            tacc[33] += _fold8(dyl * yhat)
            tacc[34] += _fold8(dyl)
            tacc[32] += _fold8(du1)
            for k in range(CONV_K):
                tacc[k] += _fold8(du1 * xbuf[pl.ds(H - (CONV_K - 1) + k + r0, CONV_RC), :])

        @pl.when(i == nt - 1)
        def _():
            for k in range(40):
                acc_ref[k:k + 1, :] = jnp.sum(tacc[k], axis=0, keepdims=True)

    row = pl.BlockSpec((TM, CH), lambda i: (i, 0))
    return pl.pallas_call(
        body, name="conv_bwd_a", grid=(nt,),
        in_specs=[row, row, pl.BlockSpec((TM, CH), lambda i: (i, O_CV // CH)), pl.BlockSpec((TM, CH), lambda i: (i, O_CG // CH)),
                  _full((1, CH)), _full((1, CH))],
        out_specs=[row, _full((40, CH))], out_shape=[SDS((S, CH), F32), SDS((40, CH), F32)],
        scratch_shapes=[pltpu.VMEM((TM + H, CH), F32), pltpu.VMEM((40, 8, CH), F32)])(du3, u1, z, z, lnw, lnb)


def _conv_bwd_b(dz, du1, z, cw):
    S = z.shape[0]
    nt = S // TM
    H = 32

    def body(dz_in, du1_ref, zv, zg, cw_ref, dz_ref, ybuf, dgate):
        del dz_in
        i, p = pl.program_id(0), pl.program_id(1)

        @pl.when(p == 0)
        def _():
            @pl.when(i == 0)
            def _():
                ybuf[TM:TM + H, :] = jnp.zeros((H, CH), F32)

            @pl.when(i > 0)
            def _():
                ybuf[TM:TM + H, :] = ybuf[0:H, :]

            ybuf[0:TM, :] = du1_ref[...]
            for r0 in range(0, TM, CONV_RC):
                rows = pl.ds(r0, CONV_RC)
                acc = ybuf[rows, :] * cw_ref[CONV_K - 1:CONV_K, :]
                for j in range(1, CONV_K):
                    acc = acc + ybuf[pl.ds(r0 + j, CONV_RC), :] * cw_ref[CONV_K - 1 - j:CONV_K - j, :]
                val = zv[rows, :].astype(F32)
                sg = _sig(zg[rows, :].astype(F32))
                dz_ref[rows, :] = (acc * sg).astype(BF16)
                dgate[rows, :] = (acc * val * sg * (1.0 - sg)).astype(BF16)

        @pl.when(p == 1)
        def _():
            dz_ref[...] = dgate[...]

    rev = lambda c: pl.BlockSpec((TM, CH), lambda i, p: (nt - 1 - i, c))
    return pl.pallas_call(
        body, name="conv_bwd_b", grid=(nt, 2),
        in_specs=[pl.BlockSpec(memory_space=pl.ANY), rev(0), rev(O_CV // CH), rev(O_CG // CH), _full((CONV_K, CH))],
        out_specs=pl.BlockSpec((TM, CH), lambda i, p: (nt - 1 - i, O_CV // CH + p)),
        out_shape=SDS(dz.shape, BF16),
        scratch_shapes=[pltpu.VMEM((TM + H, CH), F32), pltpu.VMEM((TM, CH), BF16)],
        input_output_aliases={0: 0})(dz, du1, z, z, cw)


def _memkv_fwd(mem, mnw, wkv, xkw):
    M, D = mem.shape

    def body(mem_ref, mnw_ref, w_ref, xkw_ref, mn_ref, kv_ref, mk_ref, mv_ref):
        x = mem_ref[...]
        mn = (x * lax.rsqrt(jnp.mean(x * x, axis=-1, keepdims=True) + RMS_EPS) * mnw_ref[...]).astype(BF16)
        mn_ref[...] = mn
        kv = jnp.dot(mn, w_ref[...], preferred_element_type=F32)
        kv_ref[...] = kv
        for h in range(HPG):
            k = kv[:, h * HEAD:(h + 1) * HEAD]
            mk_ref[:, h * HEAD:(h + 1) * HEAD] = (
                k * lax.rsqrt(jnp.mean(k * k, axis=-1, keepdims=True) + RMS_EPS) * xkw_ref[...]).astype(BF16)
        mv_ref[...] = kv[:, CH:2 * CH].astype(BF16)

    return pl.pallas_call(
        body, name="memkv_fwd",
        out_shape=[SDS((M, D), BF16), SDS((M, 2 * CH), F32), SDS((M, CH), BF16), SDS((M, CH), BF16)])(mem, mnw, wkv, xkw)


def _cross_q(zx, xqw, h):
    x = zx[:, h * HEAD:(h + 1) * HEAD].astype(F32)
    r = lax.rsqrt(jnp.mean(x * x, axis=-1, keepdims=True) + RMS_EPS)
    xhat = x * r
    return xhat, r, xhat * xqw


def _cross_fwd(z, xqw, mk, mv):
    S = z.shape[0]
    M = mk.shape[0]

    def body(zx, xqw_ref, mk_ref, mv_ref, o_ref):
        for h in range(HPG):
            sl = slice(h * HEAD, (h + 1) * HEAD)
            _, _, q = _cross_q(zx, xqw_ref[...], h)
            s = lax.dot_general(q.astype(BF16), mk_ref[:, sl], NT_DIMS, preferred_element_type=F32) * SCALE
            e = jnp.exp(s - jnp.max(s, axis=-1, keepdims=True))
            p = e / jnp.sum(e, axis=-1, keepdims=True)
            o_ref[:, sl] = jnp.dot(p.astype(BF16), mv_ref[:, sl], preferred_element_type=F32).astype(BF16)

    return pl.pallas_call(
        body, name="cross_fwd", grid=(S // TM,),
        in_specs=[pl.BlockSpec((TM, CH), lambda i: (i, O_XQ // CH)), _full((1, HEAD)), _full((M, CH)), _full((M, CH))],
        out_specs=pl.BlockSpec((TM, CH), lambda i: (i, 0)), out_shape=SDS((S, CH), BF16))(z, xqw, mk, mv)


def _cross_bwd(dz, doc, z, xqw, mk, mv):
    S = z.shape[0]
    M = mk.shape[0]

    def body(dz_in, do_ref, zx, xqw_ref, mk_ref, mv_ref, dz_ref, dmk_ref, dmv_ref, dxw_ref):
        del dz_in

        @pl.when(pl.program_id(0) == 0)
        def _():
            dmk_ref[...] = jnp.zeros_like(dmk_ref)
            dmv_ref[...] = jnp.zeros_like(dmv_ref)
            dxw_ref[...] = jnp.zeros_like(dxw_ref)

        for h in range(HPG):
            sl = slice(h * HEAD, (h + 1) * HEAD)
            xhat, r, q = _cross_q(zx, xqw_ref[...], h)
            qb = q.astype(BF16)
            s = lax.dot_general(qb, mk_ref[:, sl], NT_DIMS, preferred_element_type=F32) * SCALE
            e = jnp.exp(s - jnp.max(s, axis=-1, keepdims=True))
            p = e / jnp.sum(e, axis=-1, keepdims=True)
            do = do_ref[:, sl].astype(BF16)
            dp = lax.dot_general(do, mv_ref[:, sl], NT_DIMS, preferred_element_type=F32)
            ds = (p * (dp - jnp.sum(p * dp, axis=-1, keepdims=True)) * SCALE).astype(BF16)
            dq = jnp.dot(ds, mk_ref[:, sl], preferred_element_type=F32)
            dmk_ref[:, sl] += lax.dot_general(ds, qb, TN_DIMS, preferred_element_type=F32)
            dmv_ref[:, sl] += lax.dot_general(p.astype(BF16), do, TN_DIMS, preferred_element_type=F32)
            dyw = dq * xqw_ref[...]
            dz_ref[:, sl] = (r * (dyw - xhat * jnp.mean(dyw * xhat, axis=-1, keepdims=True))).astype(BF16)
            dxw_ref[...] += jnp.sum(dq * xhat, axis=0, keepdims=True)

    return pl.pallas_call(
        body, name="cross_bwd", grid=(S // TM,),
        in_specs=[pl.BlockSpec(memory_space=pl.ANY), pl.BlockSpec((TM, CH), lambda i: (i, 0)),
                  pl.BlockSpec((TM, CH), lambda i: (i, O_XQ // CH)), _full((1, HEAD)), _full((M, CH)), _full((M, CH))],
        out_specs=[pl.BlockSpec((TM, CH), lambda i: (i, O_XQ // CH)), _full((M, CH)), _full((M, CH)), _full((1, HEAD))],
        out_shape=[SDS(dz.shape, BF16), SDS((M, CH), F32), SDS((M, CH), F32), SDS((1, HEAD), F32)],
        input_output_aliases={0: 0})(dz, doc, z, xqw, mk, mv)


def _memkv_bwd(dmk, dmv, kv, mem, mn, mnw, wkv, xkw):
    M, D = mem.shape

    def body(dmk_ref, dmv_ref, kv_ref, mem_ref, mn_ref, mnw_ref, w_ref, xkw_ref, dw_ref, dxk_ref, dmn_ref, dkv):
        dxk = jnp.zeros((1, HEAD), F32)
        for h in range(HPG):
            sl = slice(h * HEAD, (h + 1) * HEAD)
            k = kv_ref[:, sl]
            r = lax.rsqrt(jnp.mean(k * k, axis=-1, keepdims=True) + RMS_EPS)
            khat = k * r
            dy = dmk_ref[:, sl]
            dyw = dy * xkw_ref[...]
            dkv[:, sl] = (r * (dyw - khat * jnp.mean(dyw * khat, axis=-1, keepdims=True))).astype(BF16)
            dxk = dxk + jnp.sum(dy * khat, axis=0, keepdims=True)
        dxk_ref[...] = dxk
        dkv[:, CH:2 * CH] = dmv_ref[...].astype(BF16)
        dw_ref[...] = lax.dot_general(mn_ref[...], dkv[...], TN_DIMS, preferred_element_type=F32)
        dn = lax.dot_general(dkv[...], w_ref[...], NT_DIMS, preferred_element_type=F32)
        x = mem_ref[...]
        r = lax.rsqrt(jnp.mean(x * x, axis=-1, keepdims=True) + RMS_EPS)
        dmn_ref[...] = jnp.sum(dn * x * r, axis=0, keepdims=True)

    return pl.pallas_call(
        body, name="memkv_bwd",
        out_shape=[SDS((D, 2 * CH), F32), SDS((1, HEAD), F32), SDS((1, D), F32)],
        scratch_shapes=[pltpu.VMEM((M, 2 * CH), BF16)])(dmk, dmv, kv, mem, mn, mnw, wkv, xkw)


def _branch_proj(a_ref, w_ref, y_ref):
    G, _, n = w_ref.shape
    a = a_ref[...]
    for g in range(G):
        y_ref[:, g * n:(g + 1) * n] = jnp.dot(a, w_ref[g], preferred_element_type=F32)


def _gates(zg_ref, bg_ref, k, D):
    return _sig(zg_ref[:, k * D:(k + 1) * D].astype(F32) + bg_ref[:, k * D:(k + 1) * D])


def _outproj_fwd(x, z, bg, attn, u3, oc, wao, wco, wxo, wout, fnw):
    S, D = x.shape
    tm = 256

    def body(x_ref, zg_ref, bg_ref, a_ref, u_ref, c_ref, wa, wc, wx, wo, fnw_ref, h1_ref, hn_ref, ya, yc, yx):
        _branch_proj(a_ref, wa, ya)
        _branch_proj(u_ref, wc, yc)
        _branch_proj(c_ref, wx, yx)
        merged = _gates(zg_ref, bg_ref, 0, D) * ya[...] + _gates(zg_ref, bg_ref, 1, D) * yc[...] + _gates(zg_ref, bg_ref, 2, D) * yx[...]
        h1 = x_ref[...] + jnp.dot(merged.astype(BF16), wo[...], preferred_element_type=F32)
        h1_ref[...] = h1
        hn_ref[...] = (h1 * lax.rsqrt(jnp.mean(h1 * h1, axis=-1, keepdims=True) + RMS_EPS) * fnw_ref[...]).astype(BF16)

    row = lambda w: pl.BlockSpec((tm, w), lambda i: (i, 0))
    return pl.pallas_call(
        body, name="outproj_fwd", grid=(S // tm,),
        in_specs=[row(D), pl.BlockSpec((tm, 3 * D), lambda i: (i, O_G // (3 * D))), _full((1, 3 * D)), row(CH), row(CH), row(CH),
                  _full(wao.shape), _full(wco.shape), _full(wxo.shape), _full((D, D)), _full((1, D))],
        out_specs=[row(D), row(D)], out_shape=[SDS((S, D), F32), SDS((S, D), BF16)],
        scratch_shapes=[pltpu.VMEM((tm, D), F32)] * 3)(x, z, bg, attn, u3, oc, wao, wco, wxo, wout, fnw)


def _outproj_bwd(dh1, z, bg, attn, u3, oc, wao, wco, wxo, wout, n_in):
    S, D = dh1.shape
    tm = 256

    def body(dh_ref, zg_ref, bg_ref, a_ref, u_ref, c_ref, wa, wc, wx, wo,
             dz_ref, da_ref, du_ref, dc_ref, dya_ref, dyc_ref, dyx_ref, mg_ref, dbg_ref, ya, yc, yx):
        @pl.when(pl.program_id(0) == 0)
        def _():
            dbg_ref[...] = jnp.zeros_like(dbg_ref)

        _branch_proj(a_ref, wa, ya)
        _branch_proj(u_ref, wc, yc)
        _branch_proj(c_ref, wx, yx)
        dm = lax.dot_general(dh_ref[...].astype(BF16), wo[...], NT_DIMS, preferred_element_type=F32)
        merged = jnp.zeros((tm, D), F32)
        for k, (y, dy_ref, w_ref, db_ref) in enumerate(((ya, dya_ref, wa, da_ref), (yc, dyc_ref, wc, du_ref), (yx, dyx_ref, wx, dc_ref))):
            gk = _gates(zg_ref, bg_ref, k, D)
            yk = y[...]
            merged = merged + gk * yk
            dzg = dm * yk * gk * (1.0 - gk)
            dz_ref[:, k * D:(k + 1) * D] = dzg.astype(BF16)
            dbg_ref[:, k * D:(k + 1) * D] += jnp.sum(dzg, axis=0, keepdims=True)
            dyk = (dm * gk).astype(BF16)
            dy_ref[...] = dyk
            G, _, n = w_ref.shape
            acc = jnp.zeros((tm, CH), F32)
            for g in range(G):
                acc = acc + lax.dot_general(dyk[:, g * n:(g + 1) * n], w_ref[g], NT_DIMS, preferred_element_type=F32)
            db_ref[...] = acc
        mg_ref[...] = merged.astype(BF16)

    row = lambda w: pl.BlockSpec((tm, w), lambda i: (i, 0))
    return pl.pallas_call(
        body, name="outproj_bwd", grid=(S // tm,),
        in_specs=[row(D), pl.BlockSpec((tm, 3 * D), lambda i: (i, O_G // (3 * D))), _full((1, 3 * D)), row(CH), row(CH), row(CH),
                  _full(wao.shape), _full(wco.shape), _full(wxo.shape), _full((D, D))],
        out_specs=[pl.BlockSpec((tm, 3 * D), lambda i: (i, O_G // (3 * D))), row(CH), row(CH), row(CH),
                   row(D), row(D), row(D), row(D), _full((1, 3 * D))],
        out_shape=[SDS((S, n_in), BF16)] + [SDS((S, CH), F32)] * 3 + [SDS((S, D), BF16)] * 4 + [SDS((1, 3 * D), F32)],
        scratch_shapes=[pltpu.VMEM((tm, D), F32)] * 3)(dh1, z, bg, attn, u3, oc, wao, wco, wxo, wout)


FFN_TC = 256
FFN_H = 8


def _ffn_taps(buf, r0):
    xx = buf[pl.ds(r0, FFN_RC + FFN_H), :]
    return xx[FFN_H:], pltpu.roll(xx, 1, 0)[FFN_H:], pltpu.roll(xx, 2, 0)[FFN_H:]


def _ffn_conv(taps, w_ref, b_ref):
    x0, x1, x2 = taps
    return b_ref[...] + x0 * w_ref[2:3, :] + x1 * w_ref[1:2, :] + x2 * w_ref[0:1, :]


def _ffn_act_fwd(up, cw, cb):
    S, F2 = up.shape
    nj = F2 // 2 // FFN_TC
    tm = 1024 if S % 1024 == 0 else TM

    def body(ua, ug, wa, wg, ba, bgt, o_ref, abuf, gbuf):
        i = pl.program_id(1)

        @pl.when(i == 0)
        def _():
            abuf[0:FFN_H, :] = jnp.zeros((FFN_H, FFN_TC), F32)
            gbuf[0:FFN_H, :] = jnp.zeros((FFN_H, FFN_TC), F32)

        @pl.when(i > 0)
        def _():
            abuf[0:FFN_H, :] = abuf[tm:tm + FFN_H, :]
            gbuf[0:FFN_H, :] = gbuf[tm:tm + FFN_H, :]

        abuf[FFN_H:FFN_H + tm, :] = ua[...].astype(F32)
        gbuf[FFN_H:FFN_H + tm, :] = ug[...].astype(F32)
        for r0 in range(0, tm, FFN_RC):
            a = _ffn_conv(_ffn_taps(abuf, r0), wa, ba)
            gt = _ffn_conv(_ffn_taps(gbuf, r0), wg, bgt)
            o_ref[pl.ds(r0, FFN_RC), :] = (gt * _sig(gt) * a).astype(BF16)

    return pl.pallas_call(
        body, name="ffn_act_fwd", grid=(nj, S // tm),
        in_specs=[pl.BlockSpec((tm, FFN_TC), lambda j, i: (i, j)), pl.BlockSpec((tm, FFN_TC), lambda j, i: (i, nj + j)),
                  pl.BlockSpec((FFN_K, FFN_TC), lambda j, i: (0, j)), pl.BlockSpec((FFN_K, FFN_TC), lambda j, i: (0, nj + j)),
                  pl.BlockSpec((1, FFN_TC), lambda j, i: (0, j)), pl.BlockSpec((1, FFN_TC), lambda j, i: (0, nj + j))],
        out_specs=pl.BlockSpec((tm, FFN_TC), lambda j, i: (i, j)), out_shape=SDS((S, F2 // 2), BF16),
        scratch_shapes=[pltpu.VMEM((tm + FFN_H, FFN_TC), F32)] * 2)(up, up, cw, cw, cb, cb)


def _ffn_down_loss(act, wdown, h1, target):
    S, D = h1.shape
    F = act.shape[1]

    def body(a_ref, w_ref, h_ref, t_ref, dy_ref, loss_ref):
        @pl.when(pl.program_id(0) == 0)
        def _():
            loss_ref[...] = jnp.zeros_like(loss_ref)

        err = h_ref[...] + jnp.dot(a_ref[...], w_ref[...], preferred_element_type=F32) - t_ref[...]
        dy_ref[...] = err * (1.0 / D)
        loss_ref[...] += 0.5 * jnp.sum(jnp.mean(err * err, axis=-1, keepdims=True))

    row = lambda w: pl.BlockSpec((TM, w), lambda i: (i, 0))
    return pl.pallas_call(
        body, name="ffn_down_loss", grid=(S // TM,),
        in_specs=[row(F), _full((F, D)), row(D), row(D)],
        out_specs=[row(D), _full((8, 128))], out_shape=[SDS((S, D), F32), SDS((8, 128), F32)])(act, wdown, h1, target)


def _ffn_bwd_a(dy, wdown, up, cw, cb):
    S, D = dy.shape
    F2 = up.shape[1]
    F = F2 // 2
    nj = F // FFN_TC
    tm = TM

    def body(dy_ref, wd_ref, ua, ug, wa, wg, ba, bgt, da_ref, dg_ref, acca_ref, accg_ref, dwd_ref,
             abuf, gbuf, hala, halg, dact_s, act_s):
        i, j = pl.program_id(0), pl.program_id(1)

        @pl.when((i == 0) & (j == 0))
        def _():
            acca_ref[...] = jnp.zeros_like(acca_ref)
            accg_ref[...] = jnp.zeros_like(accg_ref)
            dwd_ref[...] = jnp.zeros_like(dwd_ref)

        @pl.when(i == 0)
        def _():
            abuf[0:FFN_H, :] = jnp.zeros((FFN_H, FFN_TC), F32)
            gbuf[0:FFN_H, :] = jnp.zeros((FFN_H, FFN_TC), F32)

        @pl.when(i > 0)
        def _():
            abuf[0:FFN_H, :] = hala[j]
            gbuf[0:FFN_H, :] = halg[j]

        abuf[FFN_H:FFN_H + tm, :] = ua[...].astype(F32)
        gbuf[FFN_H:FFN_H + tm, :] = ug[...].astype(F32)
        hala[j] = abuf[tm:tm + FFN_H, :]
        halg[j] = gbuf[tm:tm + FFN_H, :]
        dyb = dy_ref[...].astype(BF16)
        dact_s[...] = lax.dot_general(dyb, wd_ref[...], NT_DIMS, preferred_element_type=F32)
        zero8 = jnp.zeros((8, FFN_TC), F32)
        pa, pg = [zero8] * (FFN_K + 1), [zero8] * (FFN_K + 1)
        for r0 in range(0, tm, FFN_RC):
            rows = pl.ds(r0, FFN_RC)
            ta, tg = _ffn_taps(abuf, r0), _ffn_taps(gbuf, r0)
            a = _ffn_conv(ta, wa, ba)
            gt = _ffn_conv(tg, wg, bgt)
            dact = dact_s[rows, :]
            sg = _sig(gt)
            silu = gt * sg
            act_s[rows, :] = (silu * a).astype(BF16)
            dac = dact * silu
            dgc = dact * a * (sg * (1.0 + gt * (1.0 - sg)))
            da_ref[rows, :] = dac.astype(BF16)
            dg_ref[rows, :] = dgc.astype(BF16)
            for k in range(FFN_K):
                pa[k] = pa[k] + _fold8(dac * ta[FFN_K - 1 - k])
                pg[k] = pg[k] + _fold8(dgc * tg[FFN_K - 1 - k])
            pa[FFN_K] = pa[FFN_K] + _fold8(dac)
            pg[FFN_K] = pg[FFN_K] + _fold8(dgc)
        for k in range(FFN_K + 1):
            acca_ref[j, k:k + 1, :] += jnp.sum(pa[k], axis=0, keepdims=True)
            accg_ref[j, k:k + 1, :] += jnp.sum(pg[k], axis=0, keepdims=True)
        dwd_ref[pl.ds(pl.multiple_of(j * FFN_TC, FFN_TC), FFN_TC), :] += lax.dot_general(
            act_s[...], dyb, TN_DIMS, preferred_element_type=F32)

    return pl.pallas_call(
        body, name="ffn_bwd_a", grid=(S // tm, nj),
        in_specs=[pl.BlockSpec((tm, D), lambda i, j: (i, 0)), pl.BlockSpec((FFN_TC, D), lambda i, j: (j, 0)),
                  pl.BlockSpec((tm, FFN_TC), lambda i, j: (i, j)), pl.BlockSpec((tm, FFN_TC), lambda i, j: (i, nj + j)),
                  pl.BlockSpec((FFN_K, FFN_TC), lambda i, j: (0, j)), pl.BlockSpec((FFN_K, FFN_TC), lambda i, j: (0, nj + j)),
                  pl.BlockSpec((1, FFN_TC), lambda i, j: (0, j)), pl.BlockSpec((1, FFN_TC), lambda i, j: (0, nj + j))],
        out_specs=[pl.BlockSpec((tm, FFN_TC), lambda i, j: (i, j))] * 2 + [_full((nj, 8, FFN_TC))] * 2 + [_full((F, D))],
        out_shape=[SDS((S, F), BF16)] * 2 + [SDS((nj, 8, FFN_TC), F32)] * 2 + [SDS((F, D), F32)],
        scratch_shapes=[pltpu.VMEM((tm + FFN_H, FFN_TC), F32)] * 2 + [pltpu.VMEM((nj, FFN_H, FFN_TC), F32)] * 2
        + [pltpu.VMEM((tm, FFN_TC), F32), pltpu.VMEM((tm, FFN_TC), BF16)],
    )(dy, wdown, up, up, cw, cw, cb, cb)


def _ffn_bwd_b(dca, dcg, cw):
    S, F = dca.shape
    nj = F // FFN_TC
    tm = 1024 if S % 1024 == 0 else TM
    nt = S // tm
    span = FFN_RC + FFN_H

    def body(a_ref, g_ref, w_ref, o_ref, ybuf):
        j, i = pl.program_id(0), pl.program_id(1)

        @pl.when(i == 0)
        def _():
            ybuf[tm:tm + FFN_H, :] = jnp.zeros((FFN_H, FFN_TC), F32)

        @pl.when(i > 0)
        def _():
            ybuf[tm:tm + FFN_H, :] = ybuf[0:FFN_H, :]

        ybuf[0:tm, :] = jnp.where(j < nj, a_ref[...], g_ref[...]).astype(F32)
        for r0 in range(0, tm, FFN_RC):
            yy = ybuf[pl.ds(r0, span), :]
            acc = yy[:FFN_RC] * w_ref[2:3, :] + pltpu.roll(yy, span - 1, 0)[:FFN_RC] * w_ref[1:2, :] \
                + pltpu.roll(yy, span - 2, 0)[:FFN_RC] * w_ref[0:1, :]
            o_ref[pl.ds(r0, FFN_RC), :] = acc.astype(BF16)

    return pl.pallas_call(
        body, name="ffn_bwd_b", grid=(2 * nj, nt),
        in_specs=[pl.BlockSpec((tm, FFN_TC), lambda j, i: (nt - 1 - i, jnp.minimum(j, nj - 1))),
                  pl.BlockSpec((tm, FFN_TC), lambda j, i: (nt - 1 - i, jnp.maximum(j - nj, 0))),
                  pl.BlockSpec((FFN_K, FFN_TC), lambda j, i: (0, j))],
        out_specs=pl.BlockSpec((tm, FFN_TC), lambda j, i: (nt - 1 - i, j)), out_shape=SDS((S, 2 * F), BF16),
        scratch_shapes=[pltpu.VMEM((tm + FFN_H, FFN_TC), F32)])(dca, dcg, cw)


def _adamw(w, g, m, v, name):
    R, C = w.shape
    tr = _row_tile(R, max(8, (2 ** 19) // (4 * C) // 8 * 8))

    def body(w_ref, g_ref, m_ref, v_ref, d_ref, nm_ref, nv_ref):
        gv = g_ref[...]
        m2 = ADAM_B1 * m_ref[...] + (1.0 - ADAM_B1) * gv
        v2 = ADAM_B2 * v_ref[...] + (1.0 - ADAM_B2) * jnp.square(gv)
        m_hat = m2 / (1.0 - ADAM_B1 ** ADAM_STEP)
        v_hat = v2 / (1.0 - ADAM_B2 ** ADAM_STEP)
        d_ref[...] = -ADAM_LR * (m_hat / (jnp.sqrt(v_hat) + ADAM_EPS) + ADAM_WD * w_ref[...])
        nm_ref[...] = m2
        nv_ref[...] = v2

    blk = pl.BlockSpec((tr, C), lambda i: (i, 0))
    return pl.pallas_call(
        body, name=name, grid=(R // tr,), in_specs=[blk] * 4, out_specs=[blk] * 3,
        out_shape=[SDS((R, C), F32)] * 3)(w, g, m, v)


HBM_SPEC = pl.BlockSpec(memory_space=pltpu.HBM)
SEM_SPEC = pl.BlockSpec(memory_space=pltpu.SEMAPHORE)
DATAFLOW_EFFECT = pltpu.SideEffectType.DATAFLOW_SIDE_EFFECTING


def _position():
    return lax.axis_index("x"), lax.axis_index("y"), lax.axis_index("c")


def _other_chips(x, y):
    return [(1 - x, y), (x, 1 - y), (1 - x, 1 - y)]


def _all_gather_xy(arrs, name):
    n = len(arrs)
    hbm = pl.BlockSpec(memory_space=pl.ANY)

    def body(*refs):
        ins, outs = refs[:n], refs[n:2 * n]
        send_sems, recv_sems = refs[2 * n:]
        x, y, c = _position()
        me = 2 * x + y
        chips = _other_chips(x, y)

        def rcopy(i, k, src, dst, to):
            return pltpu.make_async_remote_copy(src_ref=src, dst_ref=dst, send_sem=send_sems.at[i, k], recv_sem=recv_sems.at[i, k],
                                                device_id=to, device_id_type=MESH)

        sends = []
        for i in range(n):
            own = rcopy(i, 6, ins[i], outs[i].at[me], (x, y, 1 - c))
            own.start()
            sends.append(own)
        for i in range(n):
            for j, (px, py) in enumerate(chips):
                cp = rcopy(i, j, ins[i].at[c], outs[i].at[me, c], (px, py, c))
                cp.start()
                sends.append(cp)
        for i in range(n):
            for j, (px, py) in enumerate(chips):
                got = outs[i].at[2 * px + py, c]
                rcopy(i, j, ins[i].at[c], got, (x, y, c)).wait_recv()
                fwd = rcopy(i, 3 + j, got, got, (x, y, 1 - c))
                fwd.start()
                sends.append(fwd)
        for i in range(n):
            for j, (px, py) in enumerate(chips):
                theirs = outs[i].at[2 * px + py, 1 - c]
                rcopy(i, 3 + j, theirs, theirs, (x, y, c)).wait_recv()
        for i in range(n):
            rcopy(i, 6, ins[i], outs[i].at[me], (x, y, c)).wait_recv()
        for cp in sends:
            cp.wait_send()

    return pl.pallas_call(
        body, name=name, in_specs=[hbm] * n, out_specs=[hbm] * n,
        out_shape=[SDS((4,) + a.shape, a.dtype) for a in arrs],
        scratch_shapes=[pltpu.SemaphoreType.DMA((n, 7)), pltpu.SemaphoreType.DMA((n, 7))])(*arrs)


def _ag_ici_start(arrs, name):
    n = len(arrs)

    def body(*refs):
        ins, lands = refs[:n], refs[n:2 * n]
        send_sems, recv_sems = refs[2 * n:2 * n + 2]
        token = refs[-1]
        x, y, c = _position()
        for i in range(n):
            for j, (px, py) in enumerate(_other_chips(x, y)):
                pltpu.make_async_remote_copy(src_ref=ins[i].at[c], dst_ref=lands[i].at[2 * x + y, c], send_sem=send_sems.at[3 * i + j],
                                             recv_sem=recv_sems.at[3 * i + j], device_id=(px, py, c), device_id_type=MESH).start()
        token[...] = jnp.zeros_like(token)

    lands = [lax.empty((4,) + a.shape, a.dtype) for a in arrs]
    res = pl.pallas_call(
        body, name=name,
        out_shape=[pltpu.SemaphoreType.DMA((3 * n,)), pltpu.SemaphoreType.DMA((3 * n,))]
        + [pltpu.HBM(a.shape, a.dtype) for a in arrs] + [pltpu.HBM(l.shape, l.dtype) for l in lands] + [SDS((8, 128), F32)],
        in_specs=[HBM_SPEC] * (2 * n), out_specs=[SEM_SPEC, SEM_SPEC] + [HBM_SPEC] * (2 * n) + [pl.BlockSpec(memory_space=pltpu.VMEM)],
        input_output_aliases={i: 2 + i for i in range(2 * n)},
        compiler_params=pltpu.CompilerParams(has_side_effects=DATAFLOW_EFFECT),
    )(*[pltpu.with_memory_space_constraint(a, pltpu.HBM) for a in list(arrs) + lands])
    return res[0], res[1], list(res[2:2 + n]), list(res[2 + n:2 + 2 * n]), res[-1]


def _ag_ici_wait(send_sems, recv_sems, ins, lands, after, name):
    n = len(ins)

    def body(*refs):
        ins_r, lands_r = refs[:n], refs[n:2 * n]
        send_r, recv_r = refs[2 * n:2 * n + 2]
        x, y, c = _position()
        for i in range(n):
            for j, (px, py) in enumerate(_other_chips(x, y)):
                cp = pltpu.make_async_remote_copy(src_ref=ins_r[i].at[c], dst_ref=lands_r[i].at[2 * px + py, c], send_sem=send_r.at[3 * i + j],
                                                  recv_sem=recv_r.at[3 * i + j], device_id=(px, py, c), device_id_type=MESH)
                cp.wait_send()
                cp.wait_recv()

    res = pl.pallas_call(
        body, name=name,
        out_shape=[pltpu.HBM(a.shape, a.dtype) for a in list(ins) + list(lands)],
        in_specs=[HBM_SPEC] * (2 * n) + [SEM_SPEC, SEM_SPEC, pl.BlockSpec(memory_space=pl.ANY)], out_specs=[HBM_SPEC] * (2 * n),
        input_output_aliases={i: i for i in range(2 * n)},
        compiler_params=pltpu.CompilerParams(has_side_effects=DATAFLOW_EFFECT),
    )(*ins, *lands, send_sems, recv_sems, after)
    return list(res[:n]), list(res[n:])


def _ag_finish(arrs, lands, name):
    n = len(arrs)
    hbm = pl.BlockSpec(memory_space=pl.ANY)

    def body(*refs):
        ins, landed, outs = refs[:n], refs[n:2 * n], refs[2 * n:3 * n]
        send_sems, recv_sems = refs[3 * n:]
        x, y, c = _position()
        chips = _other_chips(x, y)
        sends = []
        for i in range(n):
            own = pltpu.make_async_remote_copy(src_ref=ins[i], dst_ref=outs[i].at[2 * x + y], send_sem=send_sems.at[i, 3],
                                               recv_sem=recv_sems.at[i, 3], device_id=(x, y, 1 - c), device_id_type=MESH)
            own.start()
            sends.append(own)
        for i in range(n):
            for j, (px, py) in enumerate(chips):
                fwd = pltpu.make_async_remote_copy(src_ref=landed[i].at[2 * px + py, c], dst_ref=outs[i].at[2 * px + py, c],
                                                   send_sem=send_sems.at[i, j], recv_sem=recv_sems.at[i, j],
                                                   device_id=(x, y, 1 - c), device_id_type=MESH)
                fwd.start()
                sends.append(fwd)
        for i in range(n):
            for j, (px, py) in enumerate(chips):
                theirs = outs[i].at[2 * px + py, 1 - c]
                pltpu.make_async_remote_copy(src_ref=theirs, dst_ref=theirs, send_sem=send_sems.at[i, j], recv_sem=recv_sems.at[i, j],
                                             device_id=(x, y, c), device_id_type=MESH).wait_recv()
        for i in range(n):
            pltpu.make_async_remote_copy(src_ref=ins[i], dst_ref=outs[i].at[2 * x + y], send_sem=send_sems.at[i, 3],
                                         recv_sem=recv_sems.at[i, 3], device_id=(x, y, c), device_id_type=MESH).wait_recv()
        for cp in sends:
            cp.wait_send()

    return pl.pallas_call(
        body, name=name, in_specs=[hbm] * (2 * n), out_specs=[hbm] * n,
        out_shape=[SDS(l.shape, l.dtype) for l in lands],
        input_output_aliases={n + i: i for i in range(n)},
        scratch_shapes=[pltpu.SemaphoreType.DMA((n, 4)), pltpu.SemaphoreType.DMA((n, 4))])(*arrs, *lands)


def _swap_halves(g, name):
    hbm = pl.BlockSpec(memory_space=pl.ANY)

    def body(g_ref, r_ref, send_sem, recv_sem):
        x, y, c = _position()
        cp = pltpu.make_async_remote_copy(src_ref=g_ref.at[1 - c], dst_ref=r_ref, send_sem=send_sem, recv_sem=recv_sem,
                                          device_id=(x, y, 1 - c), device_id_type=MESH)
        cp.start()
        cp.wait()

    return pl.pallas_call(
        body, name=name, in_specs=[hbm], out_specs=hbm, out_shape=SDS(g.shape[1:], g.dtype),
        scratch_shapes=[pltpu.SemaphoreType.DMA, pltpu.SemaphoreType.DMA])(g)


def _exchange_start(s1, name):
    L = s1.shape[1]

    def body(s_ref, land_ref, send_sems, recv_sems, s_thru, land_thru, token):
        del s_thru, land_thru
        x, y, c = _position()
        for j, (px, py) in enumerate(_other_chips(x, y)):
            pltpu.make_async_remote_copy(src_ref=s_ref.at[2 * px + py], dst_ref=land_ref.at[j], send_sem=send_sems.at[j],
                                         recv_sem=recv_sems.at[j], device_id=(px, py, c), device_id_type=MESH).start()
        token[...] = jnp.zeros_like(token)

    return pl.pallas_call(
        body, name=name,
        out_shape=[pltpu.SemaphoreType.DMA((3,)), pltpu.SemaphoreType.DMA((3,)), pltpu.HBM(s1.shape, F32),
                   pltpu.HBM((3, L, 128), F32), SDS((8, 128), F32)],
        in_specs=[HBM_SPEC, HBM_SPEC], out_specs=[SEM_SPEC, SEM_SPEC, HBM_SPEC, HBM_SPEC, pl.BlockSpec(memory_space=pltpu.VMEM)],
        input_output_aliases={0: 2, 1: 3},
        compiler_params=pltpu.CompilerParams(has_side_effects=DATAFLOW_EFFECT),
    )(pltpu.with_memory_space_constraint(s1, pltpu.HBM), pltpu.with_memory_space_constraint(lax.empty((3, L, 128), F32), pltpu.HBM))


def _exchange_wait(send_sems, recv_sems, s1, land, after, name):
    def body(s_ref, land_ref, send_r, recv_r, after_ref, s_out, land_out):
        del after_ref, s_out, land_out
        x, y, c = _position()
        for j, (px, py) in enumerate(_other_chips(x, y)):
            cp = pltpu.make_async_remote_copy(src_ref=s_ref.at[2 * px + py], dst_ref=land_ref.at[j], send_sem=send_r.at[j],
                                              recv_sem=recv_r.at[j], device_id=(px, py, c), device_id_type=MESH)
            cp.wait_send()
            cp.wait_recv()

    return pl.pallas_call(
        body, name=name, out_shape=[pltpu.HBM(s1.shape, F32), pltpu.HBM(land.shape, F32)],
        in_specs=[HBM_SPEC, HBM_SPEC, SEM_SPEC, SEM_SPEC, pl.BlockSpec(memory_space=pl.ANY)], out_specs=[HBM_SPEC, HBM_SPEC],
        input_output_aliases={0: 0, 1: 1},
        compiler_params=pltpu.CompilerParams(has_side_effects=DATAFLOW_EFFECT),
    )(s1, land, send_sems, recv_sems, after)


def _join_halves(f2, name):
    hbm = pl.BlockSpec(memory_space=pl.ANY)

    def body(f_ref, o_ref, send_sem, recv_sem):
        x, y, c = _position()
        cp = pltpu.make_async_remote_copy(src_ref=f_ref.at[c], dst_ref=o_ref.at[c], send_sem=send_sem, recv_sem=recv_sem,
                                          device_id=(x, y, 1 - c), device_id_type=MESH)
        cp.start()
        pltpu.make_async_remote_copy(src_ref=f_ref.at[1 - c], dst_ref=o_ref.at[1 - c], send_sem=send_sem, recv_sem=recv_sem,
                                     device_id=(x, y, 1 - c), device_id_type=MESH).wait_recv()
        cp.wait_send()

    return pl.pallas_call(
        body, name=name, in_specs=[hbm], out_specs=hbm, out_shape=SDS(f2.shape, f2.dtype), input_output_aliases={0: 0},
        scratch_shapes=[pltpu.SemaphoreType.DMA, pltpu.SemaphoreType.DMA])(f2)


def _add_pair(g, r1, c, name):
    R = r1.shape[0]
    tr = _row_tile(R, 2048)

    def body(c_ref, a_ref, b_ref, o_ref):
        del c_ref
        o_ref[...] = a_ref[...] + b_ref[...]

    return pl.pallas_call(
        body, name=name,
        grid_spec=pltpu.PrefetchScalarGridSpec(
            num_scalar_prefetch=1, grid=(R // tr,),
            in_specs=[pl.BlockSpec((None, tr, 128), lambda i, cr: (cr[0], i, 0)), pl.BlockSpec((tr, 128), lambda i, cr: (i, 0))],
            out_specs=pl.BlockSpec((tr, 128), lambda i, cr: (i, 0))),
        out_shape=SDS((R, 128), F32))(c, g, r1)


def _add_four(s1, r2, me_c, name):
    L = s1.shape[1]
    tr = _row_tile(L, 2048)

    def body(m_ref, a_ref, b_ref, o_ref):
        del m_ref
        o_ref[...] = ((a_ref[...] + b_ref[0]) + b_ref[1]) + b_ref[2]

    return pl.pallas_call(
        body, name=name,
        grid_spec=pltpu.PrefetchScalarGridSpec(
            num_scalar_prefetch=1, grid=(L // tr,),
            in_specs=[pl.BlockSpec((None, tr, 128), lambda i, mr: (mr[0], i, 0)), pl.BlockSpec((3, tr, 128), lambda i, mr: (0, i, 0))],
            out_specs=pl.BlockSpec((None, tr, 128), lambda i, mr: (mr[1], i, 0))),
        out_shape=SDS((2, L, 128), F32))(me_c, s1, r2)


def _all_reduce_small(v):
    R = v.shape[0]

    def body(v_ref, o_ref, buf, send_sems, recv_sems):
        x, y, c = _position()
        me = 4 * x + 2 * y + c
        buf[me] = v_ref[...]
        cps = []
        for k in range(1, 8):
            to = (1 - x if k & 4 else x, 1 - y if k & 2 else y, 1 - c if k & 1 else c)
            cp = pltpu.make_async_remote_copy(src_ref=buf.at[me], dst_ref=buf.at[me], send_sem=send_sems.at[k - 1],
                                              recv_sem=recv_sems.at[k - 1], device_id=to, device_id_type=MESH)
            cp.start()
            cps.append(cp)
        for cp in cps:
            cp.wait_send()
        for k in range(1, 8):
            src = 4 * (1 - x if k & 4 else x) + 2 * (1 - y if k & 2 else y) + (1 - c if k & 1 else c)
            pltpu.make_async_remote_copy(src_ref=buf.at[src], dst_ref=buf.at[src], send_sem=send_sems.at[k - 1],
                                         recv_sem=recv_sems.at[k - 1], device_id=(x, y, c), device_id_type=MESH).wait_recv()
        acc = buf[0]
        for k in range(1, 8):
            acc = acc + buf[k]
        o_ref[...] = acc

    vm = pl.BlockSpec(memory_space=pltpu.VMEM)
    return pl.pallas_call(
        body, name="all_reduce_small", in_specs=[vm], out_specs=vm, out_shape=SDS((R, 128), F32),
        scratch_shapes=[pltpu.VMEM((8, R, 128), F32), pltpu.SemaphoreType.DMA((7,)), pltpu.SemaphoreType.DMA((7,))])(v)


def _reduce_begin(grads, tag):
    _, _, c = _position()
    sizes = [int(np.prod(g.shape[1:])) for g in grads]
    total = sum(sizes)
    padded = -(-total // 2048) * 2048
    pieces = [g.reshape(4, -1) for g in grads]
    if padded > total:
        pieces.append(jnp.zeros((4, padded - total), F32))
    L = padded // 256
    g2 = jnp.transpose(jnp.concatenate(pieces, axis=1).reshape(4, 2, L, 128), (1, 0, 2, 3)).reshape(2, 4 * L, 128)
    s1 = _add_pair(g2, _swap_halves(g2, "rs_swap_" + tag), jnp.reshape(c, (1,)).astype(jnp.int32), "rs_add_pair_" + tag)
    send_sems, recv_sems, s1, land, token = _exchange_start(s1.reshape(4, L, 128), "rs_exchange_start_" + tag)
    return (send_sems, recv_sems, s1, land, sizes), token


def _reduce_end(state, after, tag):
    x, y, c = _position()
    send_sems, recv_sems, s1, land, sizes = state
    s1, land = _exchange_wait(send_sems, recv_sems, s1, land, after, "rs_exchange_wait_" + tag)
    f2 = _add_four(s1, land, jnp.stack([2 * x + y, c]).astype(jnp.int32), "rs_add_four_" + tag)
    flat = _join_halves(f2, "rs_join_" + tag).reshape(-1)
    out, off = [], 0
    for n in sizes:
        out.append(flat[off:off + n])
        off += n
    return out


def _halves(a):
    return a.reshape((2, a.shape[0] // 2) + a.shape[1:])


def _pack_rows(vecs, mult=8 * 128):
    flat = jnp.concatenate([jnp.ravel(v) for v in vecs])
    pad = (-flat.shape[0]) % mult
    return jnp.pad(flat, (0, pad)).reshape(-1, 128)


def _unpack_rows(packed, shapes):
    flat = packed.reshape(-1)
    out, off = [], 0
    for s in shapes:
        n = int(np.prod(s))
        out.append(flat[off:off + n].reshape(s))
        off += n
    return out


def _after(a, token):
    return a + token[0, 0]


def _local_step(x, mem, target, sp, w_in, ex):
    S, D = x.shape
    n_in = 4 * w_in.shape[2]
    bw = {"w_in": w_in}
    band, buckets = _bias_static()
    buckets = jnp.asarray(buckets)

    xn = _rmsnorm(x, _after(sp["attn_norm_w"], ex.start_rest()), "rms_in")
    z = _mm_nn(xn, bw["w_in"], BF16, "in_proj")
    qh, kh, vh = _qkv_prep(z, sp["q_norm_w"], sp["k_norm_w"])
    tab = sp["rel_bias_table"].T.reshape(N_GROUPS, HPG, N_BUCKETS)
    bias = _bias_fwd(jnp.pad(tab, ((0, 0), (0, 8 - HPG), (0, 0))), buckets)
    biasm = jnp.where(jnp.asarray(band)[None, None], bias[:, :HPG].reshape(N_GROUPS, HPG, NQ, 2 * NQ), NEG)
    os_, lses = [], []
    for g, (_, dil) in enumerate(ATTN_GROUPS):
        o_g, lse_g = _attn_fwd(qh, kh, vh, biasm[g], g, dil)
        os_.append(o_g)
        lses.append(lse_g)
    attn = _merge_fwd(os_, lses)
    u1, u3 = _conv_fwd(z, sp["conv_dw_w"], sp["conv_dw_b"], sp["conv_ln_w"], sp["conv_ln_b"])
    bw.update(ex.rest_weights(after=attn))
    F2 = 4 * bw["w_up"].shape[2]
    mn, kv, mk, mv = _memkv_fwd(mem, sp["mem_norm_w"], bw["w_mem_kv"], sp["xk_norm_w"])
    oc = _cross_fwd(z, sp["xq_norm_w"], mk, mv)
    h1, hn = _outproj_fwd(x, z, sp["b_gate"], attn, u3, oc, bw["w_attn_o"], bw["w_conv_o"], bw["w_cross_o"], bw["w_out"],
                          sp["ffn_norm_w"])
    up = _mm_nn(hn, bw["w_up"], BF16, "ffn_up")
    act = _ffn_act_fwd(up, sp["ffn_conv_w"], sp["ffn_conv_b"])
    dy, loss_tile = _ffn_down_loss(act, bw["w_down"], h1, target)

    gs, gb = {}, {}
    dca, dcg, acca, accg, gb["w_down"] = _ffn_bwd_a(dy, bw["w_down"], up, sp["ffn_conv_w"], sp["ffn_conv_b"])
    tap = lambda acc: jnp.transpose(acc, (1, 0, 2)).reshape(8, F2 // 2)
    ta, tg = tap(acca), tap(accg)
    gs["ffn_conv_w"] = jnp.concatenate([ta[:FFN_K], tg[:FFN_K]], axis=1)
    gs["ffn_conv_b"] = jnp.concatenate([ta[FFN_K:FFN_K + 1], tg[FFN_K:FFN_K + 1]], axis=1)
    dup = _ffn_bwd_b(dca, dcg, sp["ffn_conv_w"])
    gb["w_up"] = _mm_tn(hn, dup, 4, "dw_up")
    tok = ex.reduce_begin("a", ("w_down", "w_up"), gb)
    dh1, gs["ffn_norm_w"] = _norm_in_bwd(dup, bw["w_up"], h1, _after(sp["ffn_norm_w"], tok), dy, "ffn_in_bwd")
    dz, dattn, du3, doc, dya, dyc, dyx, merged, gs["b_gate"] = _outproj_bwd(
        dh1, z, sp["b_gate"], attn, u3, oc, bw["w_attn_o"], bw["w_conv_o"], bw["w_cross_o"], bw["w_out"], n_in)
    gb["w_out"] = _mm_tn(merged, dh1, 1, "dw_out")[0]
    gb["w_attn_o"] = _mm_tn(attn, dya, 4, "dw_attn_o")
    gb["w_conv_o"] = _mm_tn(u3, dyc, 4, "dw_conv_o")
    gb["w_cross_o"] = _mm_tn(oc, dyx, 4, "dw_cross_o")
    dz, dmk, dmv, gs["xq_norm_w"] = _cross_bwd(dz, doc, z, sp["xq_norm_w"], mk, mv)
    gb["w_mem_kv"], gs["xk_norm_w"], gs["mem_norm_w"] = _memkv_bwd(
        dmk, dmv, kv, mem, mn, sp["mem_norm_w"], bw["w_mem_kv"], sp["xk_norm_w"])
    ex.reduce_end("a", after=gs["mem_norm_w"])
    tok = ex.reduce_begin("b", ("w_out", "w_attn_o", "w_conv_o", "w_cross_o", "w_mem_kv"), gb)
    du1, cacc = _conv_bwd_a(du3, u1, z, _after(sp["conv_ln_w"], tok), sp["conv_ln_b"])
    gs["conv_dw_w"], gs["conv_dw_b"] = cacc[:CONV_K], cacc[32:33]
    gs["conv_ln_w"], gs["conv_ln_b"] = cacc[33:34], cacc[34:35]
    dz = _conv_bwd_b(dz, du1, z, sp["conv_dw_w"])
    wg, dah, dhb = _merge_bwd(dattn, os_, lses)
    dqs, dks, dvs, dsbs = [], [], [], []
    for g, (_, dil) in enumerate(ATTN_GROUPS):
        dq_g, dk_g, dv_g, dsb_g = _attn_bwd(qh, kh, vh, biasm[g], dah, wg[g], dhb, lses[g], g, dil)
        dqs.append(dq_g)
        dks.append(dk_g)
        dvs.append(dv_g)
        dsbs.append(dsb_g.reshape(HPG, NQ * 2 * NQ))
    dtab = _bias_bwd(jnp.pad(jnp.stack(dsbs), ((0, 0), (0, 8 - HPG), (0, 0))), buckets)
    gs["rel_bias_table"] = dtab[:, :HPG].reshape(N_GROUPS * HPG, N_BUCKETS).T
    dz, gs["q_norm_w"], gs["k_norm_w"] = _qkv_bwd(dz, z, dqs, dks, dvs, sp["q_norm_w"], sp["k_norm_w"])
    ex.reduce_end("b", after=gs["q_norm_w"])
    gb["w_in"] = _mm_tn(xn, dz, 4, "dw_in")
    tok = ex.reduce_begin("c", ("w_in",), gb)
    dx, gs["attn_norm_w"] = _norm_in_bwd(dz, bw["w_in"], x, _after(sp["attn_norm_w"], tok), dh1, "in_bwd")
    ex.reduce_end("c", after=gs["attn_norm_w"])
    return loss_tile, dx, gs, gb


SMALL = ("rel_bias_table", "attn_norm_w", "b_gate", "q_norm_w", "k_norm_w", "conv_dw_w", "conv_dw_b", "conv_ln_w", "conv_ln_b",
         "mem_norm_w", "xq_norm_w", "xk_norm_w", "ffn_norm_w", "ffn_conv_w", "ffn_conv_b")
SMALL_SHARDED = ("conv_dw_w", "ffn_conv_w")
BIG_COL = ("w_in", "w_attn_o", "w_conv_o", "w_cross_o", "w_up")
BIG_ROW = ("w_mem_kv", "w_out", "w_down")
BIG = BIG_COL + BIG_ROW
WEIGHTS = ("rel_bias_table", "attn_norm_w", "w_in", "b_gate", "q_norm_w", "k_norm_w", "w_attn_o", "conv_dw_w", "conv_dw_b",
           "conv_ln_w", "conv_ln_b", "w_conv_o", "mem_norm_w", "w_mem_kv", "xq_norm_w", "xk_norm_w", "w_cross_o", "w_out",
           "ffn_norm_w", "w_up", "ffn_conv_w", "ffn_conv_b", "w_down")


class _Exchanges:
    REST = tuple(k for k in BIG if k != "w_in")

    def __init__(self, w):
        self.w = w
        self.pending = {}
        self.reduced = {}

    def _whole(self, k, ga):
        ga = ga.reshape((4,) + self.w[k].shape)
        return ga if k in BIG_COL else ga.reshape((4 * self.w[k].shape[0],) + self.w[k].shape[1:])

    def first_weights(self):
        local = [_halves(self.w["w_in"].astype(BF16))]
        for k in SMALL_SHARDED:
            flat = jnp.ravel(self.w[k])
            local.append(jnp.pad(flat, (0, (-flat.shape[0]) % 2048)).reshape(2, -1, 128))
        gathered = _all_gather_xy(local, "gather_first")
        self.first = gathered[0]
        small = {}
        for k, ga in zip(SMALL_SHARDED, gathered[1:]):
            r, cdim = self.w[k].shape
            parts = ga.reshape(4, -1)[:, :r * cdim].reshape(4, r, cdim)
            small[k] = jnp.transpose(parts, (1, 0, 2)).reshape(r, 4 * cdim)
        return self._whole("w_in", gathered[0]), small

    def start_rest(self):
        local = [_halves(self.w[k].astype(BF16)) for k in self.REST]
        local, _ = lax.optimization_barrier((local, self.first))
        send_sems, recv_sems, ins, lands, token = _ag_ici_start(local, "gather_rest_start")
        self.pending["rest"] = (send_sems, recv_sems, ins, lands)
        return token

    def rest_weights(self, after):
        send_sems, recv_sems, ins, lands = self.pending.pop("rest")
        ins, lands = _ag_ici_wait(send_sems, recv_sems, ins, lands, after, "gather_rest_wait")
        gathered = _ag_finish(ins, lands, "gather_rest_finish")
        return {k: self._whole(k, ga) for k, ga in zip(self.REST, gathered)}

    def reduce_begin(self, tag, names, gb):
        parts = [gb[k].reshape((4, -1)) for k in names]
        self.pending[tag], token = _reduce_begin(parts, tag)
        self.pending[tag] += (names,)
        return token

    def reduce_end(self, tag, after):
        *state, names = self.pending.pop(tag)
        for k, flat in zip(names, _reduce_end(tuple(state), after, tag)):
            self.reduced[k] = flat.reshape(self.w[k].shape)


def _step(x, mem, target, w, m, v):
    xi, yi, _ = _position()
    shard = 2 * xi + yi
    ex = _Exchanges(w)
    w_in, small_gathered = ex.first_weights()
    sp = {k: w[k] for k in SMALL if k not in SMALL_SHARDED}
    sp.update(small_gathered)

    loss_tile, dx, gs, _ = _local_step(x, mem, target, sp, w_in, ex)
    g_big = ex.reduced

    small_shapes = [(8, 128)] + [gs[k].shape for k in SMALL]
    red = _unpack_rows(_all_reduce_small(_pack_rows([loss_tile] + [gs[k] for k in SMALL])), small_shapes)
    loss = red[0][0, 0]
    g_small = dict(zip(SMALL, red[1:]))
    for k in SMALL_SHARDED:
        cdim = w[k].shape[1]
        g_small[k] = lax.dynamic_slice_in_dim(g_small[k], shard * cdim, cdim, axis=1)

    grads, delta, new_m, new_v = {}, {}, {}, {}
    for k in BIG:
        grads[k] = g_big[k]
        delta[k], new_m[k], new_v[k] = _adamw(w[k], g_big[k], m[k], v[k], "adamw_" + k)
    shapes = [w[k].shape for k in SMALL]
    packed = [_pack_rows([d[k] for k in SMALL]) for d in (w, g_small, m, v)]
    outs = _adamw(*packed, "adamw_small")
    for dst, pk in zip((delta, new_m, new_v), outs):
        dst.update(zip(SMALL, _unpack_rows(pk, shapes)))
    grads.update(g_small)
    return loss, dx, grads, delta, new_m, new_v


def kernel(x, mem, rel_bias_table, attn_norm_w, w_in, b_gate, q_norm_w, k_norm_w, w_attn_o, conv_dw_w, conv_dw_b, conv_ln_w, conv_ln_b, w_conv_o, mem_norm_w, w_mem_kv, xq_norm_w, xk_norm_w, w_cross_o, w_out, ffn_norm_w, w_up, ffn_conv_w, ffn_conv_b, w_down, loss_target, m_rel_bias_table, m_attn_norm_w, m_w_in, m_b_gate, m_q_norm_w, m_k_norm_w, m_w_attn_o, m_conv_dw_w, m_conv_dw_b, m_conv_ln_w, m_conv_ln_b, m_w_conv_o, m_mem_norm_w, m_w_mem_kv, m_xq_norm_w, m_xk_norm_w, m_w_cross_o, m_w_out, m_ffn_norm_w, m_w_up, m_ffn_conv_w, m_ffn_conv_b, m_w_down, v_rel_bias_table, v_attn_norm_w, v_w_in, v_b_gate, v_q_norm_w, v_k_norm_w, v_w_attn_o, v_conv_dw_w, v_conv_dw_b, v_conv_ln_w, v_conv_ln_b, v_w_conv_o, v_mem_norm_w, v_w_mem_kv, v_xq_norm_w, v_xk_norm_w, v_w_cross_o, v_w_out, v_ffn_norm_w, v_w_up, v_ffn_conv_w, v_ffn_conv_b, v_w_down):
    args = locals()
    def block(name, k):
        a = args[name] if k == "rel_bias_table" else args[name][0]
        return a.reshape(1, -1) if a.ndim == 1 else a

    w = {k: block(k, k) for k in WEIGHTS}
    m = {k: block("m_" + k, k) for k in WEIGHTS}
    v = {k: block("v_" + k, k) for k in WEIGHTS}
    loss, dx, grads, delta, new_m, new_v = _step(x[0], mem[0], loss_target[0], w, m, v)
    out = [loss, dx[None]]
    for d in (grads, delta, new_m, new_v):
        for k in WEIGHTS:
            out.append(d[k].reshape(args[k].shape))
    return tuple(out)
```

```python
import functools
import math

import numpy as np
import jax
import jax.numpy as jnp
from jax import lax
from jax.experimental import pallas as pl
from jax.experimental.pallas import tpu as pltpu

F32, BF16 = jnp.float32, jnp.bfloat16
SDS = jax.ShapeDtypeStruct
MESH = pl.DeviceIdType.MESH

HEAD = 128
N_GROUPS, HPG = 3, 4
ATTN_GROUPS = ((128, 1), (512, 4), (2048, 16))
NQ = 128
QKV_W = N_GROUPS * HPG * HEAD
CH = 512
CONV_K, FFN_K = 31, 3
N_BUCKETS, MAX_DIST = 32, 2048
RMS_EPS, LN_EPS = 1e-6, 1e-5
O_Q, O_K, O_V, O_CV, O_CG, O_XQ, O_G = 0, QKV_W, 2 * QKV_W, 3 * QKV_W, 3 * QKV_W + CH, 3 * QKV_W + 2 * CH, 3 * QKV_W + 3 * CH
ADAM_LR, ADAM_B1, ADAM_B2, ADAM_EPS, ADAM_WD, ADAM_STEP = 0.001, 0.9, 0.999, 1e-08, 0.01, 10
NEG = -1e30
SCALE = HEAD ** -0.5
TM = 512
ATT_RB = 2048
NT_DIMS = (((1,), (1,)), ((), ()))
TN_DIMS = (((0,), (0,)), ((), ()))


CONV_RC = 16
FFN_RC = 32


def _sig(v):
    return 0.5 * jnp.tanh(0.5 * v) + 0.5


def _fold8(v):
    acc = v[0:8]
    for r in range(8, v.shape[0], 8):
        acc = acc + v[r:r + 8]
    return acc


def _row_tile(rows, cap, mult=8):
    best = None
    for t in range(mult, min(rows, cap) + 1, mult):
        if rows % t == 0:
            best = t
    return best if best is not None else rows


def _full(shape):
    n = len(shape)
    return pl.BlockSpec(shape, lambda *a: (0,) * n)


def _rmsnorm(x, w, name):
    S, D = x.shape

    def body(x_ref, w_ref, o_ref):
        xv = x_ref[...]
        r = lax.rsqrt(jnp.mean(xv * xv, axis=-1, keepdims=True) + RMS_EPS)
        o_ref[...] = (xv * r * w_ref[...]).astype(BF16)

    return pl.pallas_call(
        body, name=name, grid=(S // TM,),
        in_specs=[pl.BlockSpec((TM, D), lambda i: (i, 0)), _full((1, D))],
        out_specs=pl.BlockSpec((TM, D), lambda i: (i, 0)),
        out_shape=SDS((S, D), BF16))(x, w)


def _mm_nn(a, b, out_dtype, name):
    M, K = a.shape
    G, _, n = b.shape

    def body(a_ref, b_ref, o_ref):
        o_ref[...] = jnp.dot(a_ref[...].astype(BF16), b_ref[...], preferred_element_type=F32).astype(out_dtype)

    return pl.pallas_call(
        body, name=name, grid=(G, M // TM),
        in_specs=[pl.BlockSpec((TM, K), lambda g, i: (i, 0)), pl.BlockSpec((None, K, n), lambda g, i: (g, 0, 0))],
        out_specs=pl.BlockSpec((TM, n), lambda g, i: (i, g)),
        out_shape=SDS((M, G * n), out_dtype))(a, b)


def _mm_tn(a, b, G, name):
    S, Ka = a.shape
    n = b.shape[1] // G
    if G * n <= 1024:
        tk = 2048 if S % 2048 == 0 else TM

        def wide_body(a_ref, b_ref, o_ref, acc):
            k = pl.program_id(0)

            @pl.when(k == 0)
            def _():
                acc[...] = jnp.zeros_like(acc)

            acc[...] += lax.dot_general(a_ref[...].astype(BF16), b_ref[...].astype(BF16), TN_DIMS, preferred_element_type=F32)

            @pl.when(k == S // tk - 1)
            def _():
                for g in range(G):
                    o_ref[g] = acc[:, g * n:(g + 1) * n]

        return pl.pallas_call(
            wide_body, name=name, grid=(S // tk,),
            in_specs=[pl.BlockSpec((tk, Ka), lambda k: (k, 0)), pl.BlockSpec((tk, G * n), lambda k: (k, 0))],
            out_specs=_full((G, Ka, n)), out_shape=SDS((G, Ka, n), F32),
            scratch_shapes=[pltpu.VMEM((Ka, G * n), F32)])(a, b)

    tka = Ka
    while tka * n * 4 > 10 * 2 ** 20 and tka % 256 == 0:
        tka //= 2

    def body(a_ref, b_ref, o_ref):
        @pl.when(pl.program_id(2) == 0)
        def _():
            o_ref[...] = jnp.zeros_like(o_ref)
        o_ref[...] += lax.dot_general(a_ref[...].astype(BF16), b_ref[...].astype(BF16), TN_DIMS, preferred_element_type=F32)

    return pl.pallas_call(
        body, name=name, grid=(G, Ka // tka, S // TM),
        in_specs=[pl.BlockSpec((TM, tka), lambda g, i, k: (k, i)), pl.BlockSpec((TM, n), lambda g, i, k: (k, g))],
        out_specs=pl.BlockSpec((None, tka, n), lambda g, i, k: (g, i, 0)),
        out_shape=SDS((G, Ka, n), F32))(a, b)


NORM_RC = 16


def _norm_in_bwd(a, w, xin, nw, resid, name):
    S, K = xin.shape
    G, _, n = w.shape
    tm = 1024 if S % 1024 == 0 else TM
    split = 2 if n > 1408 and (n // 2) % 128 == 0 else 1
    n2, steps = n // split, G * split
    w_index = (lambda i, g: (jnp.right_shift(g, 1), 0, jnp.bitwise_and(g, 1))) if split == 2 else (lambda i, g: (g, 0, 0))

    def body(a_ref, w_ref, x_ref, nw_ref, r_ref, o_ref, dnw_ref, acc, part):
        i, g = pl.program_id(0), pl.program_id(1)

        @pl.when((i == 0) & (g == 0))
        def _():
            part[...] = jnp.zeros_like(part)

        @pl.when(g == 0)
        def _():
            acc[...] = jnp.zeros_like(acc)

        acc[...] += lax.dot_general(a_ref[...], w_ref[...], NT_DIMS, preferred_element_type=F32)

        @pl.when(g == steps - 1)
        def _():
            for r0 in range(0, tm, NORM_RC):
                rows = pl.ds(r0, NORM_RC)
                dn = acc[rows, :]
                xv = x_ref[rows, :]
                r = lax.rsqrt(jnp.mean(xv * xv, axis=-1, keepdims=True) + RMS_EPS)
                xhat = xv * r
                dyw = dn * nw_ref[...]
                o_ref[rows, :] = r_ref[rows, :] + r * (dyw - xhat * jnp.mean(dyw * xhat, axis=-1, keepdims=True))
                part[...] += _fold8(dn * xhat)

        @pl.when((i == S // tm - 1) & (g == steps - 1))
        def _():
            dnw_ref[...] = jnp.sum(part[...], axis=0, keepdims=True)

    return pl.pallas_call(
        body, name=name, grid=(S // tm, steps),
        in_specs=[pl.BlockSpec((tm, n2), lambda i, g: (i, g)), pl.BlockSpec((None, K, n2), w_index),
                  pl.BlockSpec((tm, K), lambda i, g: (i, 0)), _full((1, K)), pl.BlockSpec((tm, K), lambda i, g: (i, 0))],
        out_specs=[pl.BlockSpec((tm, K), lambda i, g: (i, 0)), _full((1, K))],
        out_shape=[SDS((S, K), F32), SDS((1, K), F32)],
        scratch_shapes=[pltpu.VMEM((tm, K), F32), pltpu.VMEM((8, K), F32)])(a, w, xin, nw, resid)


def _t5_bucket_np(dist):
    max_exact = N_BUCKETS // 2
    d = np.maximum(dist.astype(np.float32), np.float32(1.0))
    large = max_exact + (np.log(d / np.float32(max_exact)) / np.float32(math.log(MAX_DIST / max_exact))
                         * np.float32(N_BUCKETS - max_exact)).astype(np.int32)
    large = np.minimum(large, N_BUCKETS - 1)
    return np.where(dist < max_exact, dist, large).astype(np.int32)


def _bias_static():
    qi = np.arange(NQ)[:, None]
    kj = np.arange(2 * NQ)[None, :]
    step = qi + NQ - kj
    band = (step >= 0) & (step <= NQ)
    buckets = np.stack([_t5_bucket_np(np.clip(step, 0, None) * dil).reshape(1, -1) for _, dil in ATTN_GROUPS])
    return band, buckets


def _bias_fwd(table_t, buckets):
    nb = buckets.shape[-1]

    def body(t_ref, b_ref, o_ref):
        oh = (b_ref[...] == lax.broadcasted_iota(jnp.int32, (N_BUCKETS, nb), 0)).astype(F32)
        o_ref[...] = jnp.dot(t_ref[...], oh, preferred_element_type=F32, precision=lax.Precision.HIGHEST)

    return pl.pallas_call(
        body, name="bias_fwd", grid=(N_GROUPS,),
        in_specs=[pl.BlockSpec((None, 8, N_BUCKETS), lambda g: (g, 0, 0)), pl.BlockSpec((None, 1, nb), lambda g: (g, 0, 0))],
        out_specs=pl.BlockSpec((None, 8, nb), lambda g: (g, 0, 0)),
        out_shape=SDS((N_GROUPS, 8, nb), F32))(table_t, buckets)


def _bias_bwd(dsb, buckets):
    nb = buckets.shape[-1]

    def body(d_ref, b_ref, o_ref):
        oh = (b_ref[...] == lax.broadcasted_iota(jnp.int32, (N_BUCKETS, nb), 0)).astype(F32)
        o_ref[...] = lax.dot_general(d_ref[...], oh, NT_DIMS, preferred_element_type=F32, precision=lax.Precision.HIGHEST)

    return pl.pallas_call(
        body, name="bias_bwd", grid=(N_GROUPS,),
        in_specs=[pl.BlockSpec((None, 8, nb), lambda g: (g, 0, 0)), pl.BlockSpec((None, 1, nb), lambda g: (g, 0, 0))],
        out_specs=pl.BlockSpec((None, 8, N_BUCKETS), lambda g: (g, 0, 0)),
        out_shape=SDS((N_GROUPS, 8, N_BUCKETS), F32))(dsb, buckets)


def _qkv_prep(z, qw, kw):
    S = z.shape[0]
    nh = N_GROUPS * HPG

    def body(zq, zk, zv, qw_ref, kw_ref, qh, kh, vh):
        for h in range(nh):
            g = h // HPG
            sl = slice(h * HEAD, (h + 1) * HEAD)
            xq = zq[:, sl].astype(F32)
            qh[h] = xq * lax.rsqrt(jnp.mean(xq * xq, axis=-1, keepdims=True) + RMS_EPS) * qw_ref[g:g + 1, :]
            xk = zk[:, sl].astype(F32)
            kh[h] = xk * lax.rsqrt(jnp.mean(xk * xk, axis=-1, keepdims=True) + RMS_EPS) * kw_ref[g:g + 1, :]
            vh[h] = zv[:, sl].astype(F32)

    hm = pl.BlockSpec((nh, TM, HEAD), lambda i: (0, i, 0))
    return pl.pallas_call(
        body, name="qkv_prep", grid=(S // TM,),
        in_specs=[pl.BlockSpec((TM, QKV_W), lambda i: (i, 0)), pl.BlockSpec((TM, QKV_W), lambda i: (i, 1)),
                  pl.BlockSpec((TM, QKV_W), lambda i: (i, 2)), _full((N_GROUPS, HEAD)), _full((N_GROUPS, HEAD))],
        out_specs=[hm, hm, hm],
        out_shape=[SDS((nh, S, HEAD), F32)] * 3)(z, z, z, qw, kw)


def _rows(start, d):
    return pl.ds(start, NQ) if d == 1 else pl.ds(start, NQ, stride=d)


def _attn_fwd(qh, kh, vh, biasm, g, d):
    S = qh.shape[1]
    RB = ATT_RB
    nbk, nq = S // RB, RB // (NQ * d)

    def body(q_ref, k_ref, v_ref, bias_ref, o_ref, lse_ref, kbuf, vbuf):
        b = pl.program_id(1)

        @pl.when(b == 0)
        def _():
            kbuf[0:RB, :] = jnp.zeros((RB, HEAD), F32)
            vbuf[0:RB, :] = jnp.zeros((RB, HEAD), F32)

        @pl.when(b > 0)
        def _():
            kbuf[0:RB, :] = kbuf[RB:2 * RB, :]
            vbuf[0:RB, :] = vbuf[RB:2 * RB, :]

        kbuf[RB:2 * RB, :] = k_ref[...]
        vbuf[RB:2 * RB, :] = v_ref[...]
        bias = bias_ref[...]
        col = lax.broadcasted_iota(jnp.int32, (NQ, 2 * NQ), 1)

        for qb in range(nq):
            def unit(r, carry, qb=qb):
                qs = qb * NQ * d + r
                q = q_ref[_rows(qs, d), :].astype(BF16)
                kw = jnp.concatenate([kbuf[_rows(RB + qs - NQ * d, d), :], kbuf[_rows(RB + qs, d), :]], axis=0).astype(BF16)
                vw = jnp.concatenate([vbuf[_rows(RB + qs - NQ * d, d), :], vbuf[_rows(RB + qs, d), :]], axis=0).astype(BF16)
                s = lax.dot_general(q, kw, NT_DIMS, preferred_element_type=F32) * SCALE + bias
                if qb == 0:
                    s = jnp.where((col < NQ) & (b == 0), NEG, s)
                m = jnp.max(s, axis=-1, keepdims=True)
                p = jnp.exp(s - m)
                l = jnp.sum(p, axis=-1, keepdims=True)
                o = jnp.dot(p.astype(BF16), vw, preferred_element_type=F32) / l
                o_ref[_rows(qs, d), :] = o
                lse_ref[_rows(qs, d), :] = jnp.broadcast_to(m + jnp.log(l), (NQ, HEAD))
                return carry

            for r in range(d):
                unit(r, 0)

    blk = lambda f: pl.BlockSpec((None, RB, HEAD), f)
    return pl.pallas_call(
        body, name=f"attn_fwd_g{g}", grid=(HPG, nbk),
        in_specs=[blk(lambda h, b: (HPG * g + h, b, 0))] * 3 + [pl.BlockSpec((None, NQ, 2 * NQ), lambda h, b: (h, 0, 0))],
        out_specs=[blk(lambda h, b: (h, b, 0))] * 2,
        out_shape=[SDS((HPG, S, HEAD), F32)] * 2,
        scratch_shapes=[pltpu.VMEM((2 * RB, HEAD), F32)] * 2)(qh, kh, vh, biasm)


def _attn_bwd(qh, kh, vh, biasm, da, wg, dh, lse, g, d):
    S = qh.shape[1]
    RB = ATT_RB
    nbk, nq = S // RB, RB // (NQ * d)

    def body(q_ref, k_ref, v_ref, bias_ref, da_ref, wg_ref, dh_ref, lse_ref,
             dq_ref, dk_ref, dv_ref, dsb_ref, kbuf, vbuf, dkbuf, dvbuf):
        b = pl.program_id(1)
        zero = jnp.zeros((RB, HEAD), F32)

        @pl.when(b == 0)
        def _():
            kbuf[0:RB, :] = zero
            vbuf[0:RB, :] = zero
            dkbuf[0:RB, :] = zero
            dvbuf[0:RB, :] = zero
            dsb_ref[...] = jnp.zeros_like(dsb_ref)

        @pl.when(b > 0)
        def _():
            kbuf[0:RB, :] = kbuf[RB:2 * RB, :]
            vbuf[0:RB, :] = vbuf[RB:2 * RB, :]
            dkbuf[0:RB, :] = dkbuf[RB:2 * RB, :]
            dvbuf[0:RB, :] = dvbuf[RB:2 * RB, :]

        dkbuf[RB:2 * RB, :] = zero
        dvbuf[RB:2 * RB, :] = zero

        @pl.when(b < nbk)
        def _():
            kbuf[RB:2 * RB, :] = k_ref[...]
            vbuf[RB:2 * RB, :] = v_ref[...]
            bias = bias_ref[...]
            col = lax.broadcasted_iota(jnp.int32, (NQ, 2 * NQ), 1)

            for qb in range(nq):
                def unit(r, carry, qb=qb):
                    qs = qb * NQ * d + r
                    prev, cur = _rows(RB + qs - NQ * d, d), _rows(RB + qs, d)
                    q = q_ref[_rows(qs, d), :].astype(BF16)
                    kw = jnp.concatenate([kbuf[prev, :], kbuf[cur, :]], axis=0).astype(BF16)
                    vw = jnp.concatenate([vbuf[prev, :], vbuf[cur, :]], axis=0).astype(BF16)
                    s = lax.dot_general(q, kw, NT_DIMS, preferred_element_type=F32) * SCALE + bias
                    if qb == 0:
                        s = jnp.where((col < NQ) & (b == 0), NEG, s)
                    p = jnp.exp(s - lse_ref[_rows(qs, d), :][:, 0:1])
                    w = wg_ref[_rows(qs, d), :]
                    do = (da_ref[_rows(qs, d), :] * w).astype(BF16)
                    dp = lax.dot_general(do, vw, NT_DIMS, preferred_element_type=F32)
                    ds = p * (dp - w[:, 0:1] * dh_ref[_rows(qs, d), :][:, 0:1])
                    dsb_ref[...] += ds
                    dsb = ds.astype(BF16)
                    dq_ref[_rows(qs, d), :] = jnp.dot(dsb, kw, preferred_element_type=F32) * SCALE
                    dkw = lax.dot_general(dsb, q, TN_DIMS, preferred_element_type=F32) * SCALE
                    dvw = lax.dot_general(p.astype(BF16), do, TN_DIMS, preferred_element_type=F32)
                    dkbuf[prev, :] += dkw[0:NQ, :]
                    dkbuf[cur, :] += dkw[NQ:2 * NQ, :]
                    dvbuf[prev, :] += dvw[0:NQ, :]
                    dvbuf[cur, :] += dvw[NQ:2 * NQ, :]
                    return carry

                for r in range(d):
                    unit(r, 0)

        dk_ref[...] = dkbuf[0:RB, :]
        dv_ref[...] = dvbuf[0:RB, :]

    blk = lambda f: pl.BlockSpec((None, RB, HEAD), f)
    cur_g = blk(lambda h, b: (HPG * g + h, jnp.minimum(b, nbk - 1), 0))
    cur = blk(lambda h, b: (h, jnp.minimum(b, nbk - 1), 0))
    prv = blk(lambda h, b: (h, jnp.maximum(b - 1, 0), 0))
    sq = pl.BlockSpec((None, NQ, 2 * NQ), lambda h, b: (h, 0, 0))
    return pl.pallas_call(
        body, name=f"attn_bwd_g{g}", grid=(HPG, nbk + 1),
        in_specs=[cur_g, cur_g, cur_g, sq, cur, cur, cur, cur],
        out_specs=[cur, prv, prv, sq],
        out_shape=[SDS((HPG, S, HEAD), F32)] * 3 + [SDS((HPG, NQ, 2 * NQ), F32)],
        scratch_shapes=[pltpu.VMEM((2 * RB, HEAD), F32)] * 4)(qh, kh, vh, biasm, da, wg, dh, lse)


def _merge_weights(l0, l1, l2):
    m = jnp.maximum(jnp.maximum(l0, l1), l2)
    e0, e1, e2 = jnp.exp(l0 - m), jnp.exp(l1 - m), jnp.exp(l2 - m)
    inv = 1.0 / (e0 + e1 + e2)
    return e0 * inv, e1 * inv, e2 * inv


def _merge_fwd(os_, lses):
    S = os_[0].shape[1]

    def body(o0, o1, o2, l0, l1, l2, a_ref):
        for h in range(HPG):
            w0, w1, w2 = _merge_weights(l0[h], l1[h], l2[h])
            a_ref[:, h * HEAD:(h + 1) * HEAD] = (w0 * o0[h] + w1 * o1[h] + w2 * o2[h]).astype(BF16)

    hm = pl.BlockSpec((HPG, TM, HEAD), lambda i: (0, i, 0))
    return pl.pallas_call(
        body, name="merge_fwd", grid=(S // TM,), in_specs=[hm] * 6,
        out_specs=pl.BlockSpec((TM, CH), lambda i: (i, 0)),
        out_shape=SDS((S, CH), BF16))(*os_, *lses)


def _merge_bwd(dattn, os_, lses):
    S = dattn.shape[0]

    def body(da_ref, o0, o1, o2, l0, l1, l2, w0_ref, w1_ref, w2_ref, dah_ref, dh_ref):
        for h in range(HPG):
            w = _merge_weights(l0[h], l1[h], l2[h])
            attn = w[0] * o0[h] + w[1] * o1[h] + w[2] * o2[h]
            da = da_ref[:, h * HEAD:(h + 1) * HEAD]
            for w_ref, wv in zip((w0_ref, w1_ref, w2_ref), w):
                w_ref[h] = wv
            dah_ref[h] = da
            dh_ref[h] = jnp.broadcast_to(jnp.sum(da * attn, axis=-1, keepdims=True), (TM, HEAD))

    hm = pl.BlockSpec((HPG, TM, HEAD), lambda i: (0, i, 0))
    res = pl.pallas_call(
        body, name="merge_bwd", grid=(S // TM,),
        in_specs=[pl.BlockSpec((TM, CH), lambda i: (i, 0))] + [hm] * 6,
        out_specs=[hm] * 5, out_shape=[SDS((HPG, S, HEAD), F32)] * 5)(dattn, *os_, *lses)
    return res[0:3], res[3], res[4]


def _qkv_bwd(dz, z, dqs, dks, dvs, qw, kw):
    S = z.shape[0]
    nh = N_GROUPS * HPG

    def body(dz_in, zq, zk, *refs):
        del dz_in
        dq_refs, dk_refs, dv_refs = refs[0:3], refs[3:6], refs[6:9]
        qw_ref, kw_ref, dz_ref, dqw_ref, dkw_ref = refs[9:]

        @pl.when(pl.program_id(0) == 0)
        def _():
            dqw_ref[...] = jnp.zeros_like(dqw_ref)
            dkw_ref[...] = jnp.zeros_like(dkw_ref)

        def nbwd(xr, dy, wr, dwr, h, off):
            g = h // HPG
            x = xr[:, h * HEAD:(h + 1) * HEAD].astype(F32)
            r = lax.rsqrt(jnp.mean(x * x, axis=-1, keepdims=True) + RMS_EPS)
            xhat = x * r
            dyw = dy * wr[g:g + 1, :]
            dz_ref[:, off + h * HEAD:off + (h + 1) * HEAD] = (
                r * (dyw - xhat * jnp.mean(dyw * xhat, axis=-1, keepdims=True))).astype(BF16)
            dwr[g:g + 1, :] += jnp.sum(dy * xhat, axis=0, keepdims=True)

        for h in range(nh):
            g, hh = h // HPG, h % HPG
            nbwd(zq, dq_refs[g][hh], qw_ref, dqw_ref, h, O_Q)
            nbwd(zk, dk_refs[g][hh], kw_ref, dkw_ref, h, O_K)
            dz_ref[:, O_V + h * HEAD:O_V + (h + 1) * HEAD] = dv_refs[g][hh].astype(BF16)

    hm = pl.BlockSpec((HPG, TM, HEAD), lambda i: (0, i, 0))
    return pl.pallas_call(
        body, name="qkv_bwd", grid=(S // TM,),
        in_specs=[pl.BlockSpec(memory_space=pl.ANY), pl.BlockSpec((TM, QKV_W), lambda i: (i, 0)),
                  pl.BlockSpec((TM, QKV_W), lambda i: (i, 1))] + [hm] * 9 + [_full((N_GROUPS, HEAD)), _full((N_GROUPS, HEAD))],
        out_specs=[pl.BlockSpec((TM, 3 * QKV_W), lambda i: (i, 0)), _full((N_GROUPS, HEAD)), _full((N_GROUPS, HEAD))],
        out_shape=[SDS(dz.shape, BF16), SDS((N_GROUPS, HEAD), F32), SDS((N_GROUPS, HEAD), F32)],
        input_output_aliases={0: 0})(dz, z, z, *dqs, *dks, *dvs, qw, kw)


def _conv_fwd(z, cw, cb, lnw, lnb):
    S = z.shape[0]
    H = 32

    def body(zv, zg, cw_ref, cb_ref, lnw_ref, lnb_ref, u1_ref, u3_ref, xbuf):
        i = pl.program_id(0)

        @pl.when(i == 0)
        def _():
            xbuf[0:H, :] = jnp.zeros((H, CH), F32)

        @pl.when(i > 0)
        def _():
            xbuf[0:H, :] = xbuf[TM:TM + H, :]

        xbuf[H:H + TM, :] = zv[...].astype(F32) * _sig(zg[...].astype(F32))
        for r0 in range(0, TM, CONV_RC):
            rows = pl.ds(r0, CONV_RC)
            acc = jnp.broadcast_to(cb_ref[...], (CONV_RC, CH))
            for k in range(CONV_K):
                acc = acc + xbuf[pl.ds(H - (CONV_K - 1) + k + r0, CONV_RC), :] * cw_ref[k:k + 1, :]
            u1_ref[rows, :] = acc
            mu = jnp.mean(acc, axis=-1, keepdims=True)
            xc = acc - mu
            yl = xc * lax.rsqrt(jnp.mean(xc * xc, axis=-1, keepdims=True) + LN_EPS) * lnw_ref[...] + lnb_ref[...]
            u3_ref[rows, :] = (yl * _sig(yl)).astype(BF16)

    row = pl.BlockSpec((TM, CH), lambda i: (i, 0))
    return pl.pallas_call(
        body, name="conv_fwd", grid=(S // TM,),
        in_specs=[pl.BlockSpec((TM, CH), lambda i: (i, O_CV // CH)), pl.BlockSpec((TM, CH), lambda i: (i, O_CG // CH)),
                  _full((CONV_K, CH)), _full((1, CH)), _full((1, CH)), _full((1, CH))],
        out_specs=[row, row], out_shape=[SDS((S, CH), F32), SDS((S, CH), BF16)],
        scratch_shapes=[pltpu.VMEM((TM + H, CH), F32)])(z, z, cw, cb, lnw, lnb)


def _conv_bwd_a(du3, u1, z, lnw, lnb):
    S = z.shape[0]
    H = 32

    nt = S // TM

    def body(du3_ref, u1_ref, zv, zg, lnw_ref, lnb_ref, du1_ref, acc_ref, xbuf, tacc):
        i = pl.program_id(0)

        @pl.when(i == 0)
        def _():
            xbuf[0:H, :] = jnp.zeros((H, CH), F32)
            tacc[...] = jnp.zeros_like(tacc)

        @pl.when(i > 0)
        def _():
            xbuf[0:H, :] = xbuf[TM:TM + H, :]

        xbuf[H:H + TM, :] = zv[...].astype(F32) * _sig(zg[...].astype(F32))
        for r0 in range(0, TM, CONV_RC):
            rows = pl.ds(r0, CONV_RC)
            u1 = u1_ref[rows, :]
            mu = jnp.mean(u1, axis=-1, keepdims=True)
            xc = u1 - mu
            r = lax.rsqrt(jnp.mean(xc * xc, axis=-1, keepdims=True) + LN_EPS)
            yhat = xc * r
            yl = yhat * lnw_ref[...] + lnb_ref[...]
            sg = _sig(yl)
            dyl = du3_ref[rows, :] * (sg * (1.0 + yl * (1.0 - sg)))
            dyh = dyl * lnw_ref[...]
            du1 = r * (dyh - jnp.mean(dyh, axis=-1, keepdims=True) - yhat * jnp.mean(dyh * yhat, axis=-1, keepdims=True))
            du1_ref[rows, :] = du1
            tacc[33] += _fold8(dyl * yhat)
            tacc[34] += _fold8(dyl)
            tacc[32] += _fold8(du1)
            for k in range(CONV_K):
                tacc[k] += _fold8(du1 * xbuf[pl.ds(H - (CONV_K - 1) + k + r0, CONV_RC), :])

        @pl.when(i == nt - 1)
        def _():
            for k in range(40):
                acc_ref[k:k + 1, :] = jnp.sum(tacc[k], axis=0, keepdims=True)

    row = pl.BlockSpec((TM, CH), lambda i: (i, 0))
    return pl.pallas_call(
        body, name="conv_bwd_a", grid=(nt,),
        in_specs=[row, row, pl.BlockSpec((TM, CH), lambda i: (i, O_CV // CH)), pl.BlockSpec((TM, CH), lambda i: (i, O_CG // CH)),
                  _full((1, CH)), _full((1, CH))],
        out_specs=[row, _full((40, CH))], out_shape=[SDS((S, CH), F32), SDS((40, CH), F32)],
        scratch_shapes=[pltpu.VMEM((TM + H, CH), F32), pltpu.VMEM((40, 8, CH), F32)])(du3, u1, z, z, lnw, lnb)


def _conv_bwd_b(dz, du1, z, cw):
    S = z.shape[0]
    nt = S // TM
    H = 32

    def body(dz_in, du1_ref, zv, zg, cw_ref, dz_ref, ybuf, dgate):
        del dz_in
        i, p = pl.program_id(0), pl.program_id(1)

        @pl.when(p == 0)
        def _():
            @pl.when(i == 0)
            def _():
                ybuf[TM:TM + H, :] = jnp.zeros((H, CH), F32)

            @pl.when(i > 0)
            def _():
                ybuf[TM:TM + H, :] = ybuf[0:H, :]

            ybuf[0:TM, :] = du1_ref[...]
            for r0 in range(0, TM, CONV_RC):
                rows = pl.ds(r0, CONV_RC)
                acc = ybuf[rows, :] * cw_ref[CONV_K - 1:CONV_K, :]
                for j in range(1, CONV_K):
                    acc = acc + ybuf[pl.ds(r0 + j, CONV_RC), :] * cw_ref[CONV_K - 1 - j:CONV_K - j, :]
                val = zv[rows, :].astype(F32)
                sg = _sig(zg[rows, :].astype(F32))
                dz_ref[rows, :] = (acc * sg).astype(BF16)
                dgate[rows, :] = (acc * val * sg * (1.0 - sg)).astype(BF16)

        @pl.when(p == 1)
        def _():
            dz_ref[...] = dgate[...]

    rev = lambda c: pl.BlockSpec((TM, CH), lambda i, p: (nt - 1 - i, c))
    return pl.pallas_call(
        body, name="conv_bwd_b", grid=(nt, 2),
        in_specs=[pl.BlockSpec(memory_space=pl.ANY), rev(0), rev(O_CV // CH), rev(O_CG // CH), _full((CONV_K, CH))],
        out_specs=pl.BlockSpec((TM, CH), lambda i, p: (nt - 1 - i, O_CV // CH + p)),
        out_shape=SDS(dz.shape, BF16),
        scratch_shapes=[pltpu.VMEM((TM + H, CH), F32), pltpu.VMEM((TM, CH), BF16)],
        input_output_aliases={0: 0})(dz, du1, z, z, cw)


def _memkv_fwd(mem, mnw, wkv, xkw):
    M, D = mem.shape

    def body(mem_ref, mnw_ref, w_ref, xkw_ref, mn_ref, kv_ref, mk_ref, mv_ref):
        x = mem_ref[...]
        mn = (x * lax.rsqrt(jnp.mean(x * x, axis=-1, keepdims=True) + RMS_EPS) * mnw_ref[...]).astype(BF16)
        mn_ref[...] = mn
        kv = jnp.dot(mn, w_ref[...], preferred_element_type=F32)
        kv_ref[...] = kv
        for h in range(HPG):
            k = kv[:, h * HEAD:(h + 1) * HEAD]
            mk_ref[:, h * HEAD:(h + 1) * HEAD] = (
                k * lax.rsqrt(jnp.mean(k * k, axis=-1, keepdims=True) + RMS_EPS) * xkw_ref[...]).astype(BF16)
        mv_ref[...] = kv[:, CH:2 * CH].astype(BF16)

    return pl.pallas_call(
        body, name="memkv_fwd",
        out_shape=[SDS((M, D), BF16), SDS((M, 2 * CH), F32), SDS((M, CH), BF16), SDS((M, CH), BF16)])(mem, mnw, wkv, xkw)


def _cross_q(zx, xqw, h):
    x = zx[:, h * HEAD:(h + 1) * HEAD].astype(F32)
    r = lax.rsqrt(jnp.mean(x * x, axis=-1, keepdims=True) + RMS_EPS)
    xhat = x * r
    return xhat, r, xhat * xqw


def _cross_fwd(z, xqw, mk, mv):
    S = z.shape[0]
    M = mk.shape[0]

    def body(zx, xqw_ref, mk_ref, mv_ref, o_ref):
        for h in range(HPG):
            sl = slice(h * HEAD, (h + 1) * HEAD)
            _, _, q = _cross_q(zx, xqw_ref[...], h)
            s = lax.dot_general(q.astype(BF16), mk_ref[:, sl], NT_DIMS, preferred_element_type=F32) * SCALE
            e = jnp.exp(s - jnp.max(s, axis=-1, keepdims=True))
            p = e / jnp.sum(e, axis=-1, keepdims=True)
            o_ref[:, sl] = jnp.dot(p.astype(BF16), mv_ref[:, sl], preferred_element_type=F32).astype(BF16)

    return pl.pallas_call(
        body, name="cross_fwd", grid=(S // TM,),
        in_specs=[pl.BlockSpec((TM, CH), lambda i: (i, O_XQ // CH)), _full((1, HEAD)), _full((M, CH)), _full((M, CH))],
        out_specs=pl.BlockSpec((TM, CH), lambda i: (i, 0)), out_shape=SDS((S, CH), BF16))(z, xqw, mk, mv)


def _cross_bwd(dz, doc, z, xqw, mk, mv):
    S = z.shape[0]
    M = mk.shape[0]

    def body(dz_in, do_ref, zx, xqw_ref, mk_ref, mv_ref, dz_ref, dmk_ref, dmv_ref, dxw_ref):
        del dz_in

        @pl.when(pl.program_id(0) == 0)
        def _():
            dmk_ref[...] = jnp.zeros_like(dmk_ref)
            dmv_ref[...] = jnp.zeros_like(dmv_ref)
            dxw_ref[...] = jnp.zeros_like(dxw_ref)

        for h in range(HPG):
            sl = slice(h * HEAD, (h + 1) * HEAD)
            xhat, r, q = _cross_q(zx, xqw_ref[...], h)
            qb = q.astype(BF16)
            s = lax.dot_general(qb, mk_ref[:, sl], NT_DIMS, preferred_element_type=F32) * SCALE
            e = jnp.exp(s - jnp.max(s, axis=-1, keepdims=True))
            p = e / jnp.sum(e, axis=-1, keepdims=True)
            do = do_ref[:, sl].astype(BF16)
            dp = lax.dot_general(do, mv_ref[:, sl], NT_DIMS, preferred_element_type=F32)
            ds = (p * (dp - jnp.sum(p * dp, axis=-1, keepdims=True)) * SCALE).astype(BF16)
            dq = jnp.dot(ds, mk_ref[:, sl], preferred_element_type=F32)
            dmk_ref[:, sl] += lax.dot_general(ds, qb, TN_DIMS, preferred_element_type=F32)
            dmv_ref[:, sl] += lax.dot_general(p.astype(BF16), do, TN_DIMS, preferred_element_type=F32)
            dyw = dq * xqw_ref[...]
            dz_ref[:, sl] = (r * (dyw - xhat * jnp.mean(dyw * xhat, axis=-1, keepdims=True))).astype(BF16)
            dxw_ref[...] += jnp.sum(dq * xhat, axis=0, keepdims=True)

    return pl.pallas_call(
        body, name="cross_bwd", grid=(S // TM,),
        in_specs=[pl.BlockSpec(memory_space=pl.ANY), pl.BlockSpec((TM, CH), lambda i: (i, 0)),
                  pl.BlockSpec((TM, CH), lambda i: (i, O_XQ // CH)), _full((1, HEAD)), _full((M, CH)), _full((M, CH))],
        out_specs=[pl.BlockSpec((TM, CH), lambda i: (i, O_XQ // CH)), _full((M, CH)), _full((M, CH)), _full((1, HEAD))],
        out_shape=[SDS(dz.shape, BF16), SDS((M, CH), F32), SDS((M, CH), F32), SDS((1, HEAD), F32)],
        input_output_aliases={0: 0})(dz, doc, z, xqw, mk, mv)


def _memkv_bwd(dmk, dmv, kv, mem, mn, mnw, wkv, xkw):
    M, D = mem.shape

    def body(dmk_ref, dmv_ref, kv_ref, mem_ref, mn_ref, mnw_ref, w_ref, xkw_ref, dw_ref, dxk_ref, dmn_ref, dkv):
        dxk = jnp.zeros((1, HEAD), F32)
        for h in range(HPG):
            sl = slice(h * HEAD, (h + 1) * HEAD)
            k = kv_ref[:, sl]
            r = lax.rsqrt(jnp.mean(k * k, axis=-1, keepdims=True) + RMS_EPS)
            khat = k * r
            dy = dmk_ref[:, sl]
            dyw = dy * xkw_ref[...]
            dkv[:, sl] = (r * (dyw - khat * jnp.mean(dyw * khat, axis=-1, keepdims=True))).astype(BF16)
            dxk = dxk + jnp.sum(dy * khat, axis=0, keepdims=True)
        dxk_ref[...] = dxk
        dkv[:, CH:2 * CH] = dmv_ref[...].astype(BF16)
        dw_ref[...] = lax.dot_general(mn_ref[...], dkv[...], TN_DIMS, preferred_element_type=F32)
        dn = lax.dot_general(dkv[...], w_ref[...], NT_DIMS, preferred_element_type=F32)
        x = mem_ref[...]
        r = lax.rsqrt(jnp.mean(x * x, axis=-1, keepdims=True) + RMS_EPS)
        dmn_ref[...] = jnp.sum(dn * x * r, axis=0, keepdims=True)

    return pl.pallas_call(
        body, name="memkv_bwd",
        out_shape=[SDS((D, 2 * CH), F32), SDS((1, HEAD), F32), SDS((1, D), F32)],
        scratch_shapes=[pltpu.VMEM((M, 2 * CH), BF16)])(dmk, dmv, kv, mem, mn, mnw, wkv, xkw)


def _branch_proj(a_ref, w_ref, y_ref):
    G, _, n = w_ref.shape
    a = a_ref[...]
    for g in range(G):
        y_ref[:, g * n:(g + 1) * n] = jnp.dot(a, w_ref[g], preferred_element_type=F32)


def _gates(zg_ref, bg_ref, k, D):
    return _sig(zg_ref[:, k * D:(k + 1) * D].astype(F32) + bg_ref[:, k * D:(k + 1) * D])


def _outproj_fwd(x, z, bg, attn, u3, oc, wao, wco, wxo, wout, fnw):
    S, D = x.shape
    tm = 256

    def body(x_ref, zg_ref, bg_ref, a_ref, u_ref, c_ref, wa, wc, wx, wo, fnw_ref, h1_ref, hn_ref, ya, yc, yx):
        _branch_proj(a_ref, wa, ya)
        _branch_proj(u_ref, wc, yc)
        _branch_proj(c_ref, wx, yx)
        merged = _gates(zg_ref, bg_ref, 0, D) * ya[...] + _gates(zg_ref, bg_ref, 1, D) * yc[...] + _gates(zg_ref, bg_ref, 2, D) * yx[...]
        h1 = x_ref[...] + jnp.dot(merged.astype(BF16), wo[...], preferred_element_type=F32)
        h1_ref[...] = h1
        hn_ref[...] = (h1 * lax.rsqrt(jnp.mean(h1 * h1, axis=-1, keepdims=True) + RMS_EPS) * fnw_ref[...]).astype(BF16)

    row = lambda w: pl.BlockSpec((tm, w), lambda i: (i, 0))
    return pl.pallas_call(
        body, name="outproj_fwd", grid=(S // tm,),
        in_specs=[row(D), pl.BlockSpec((tm, 3 * D), lambda i: (i, O_G // (3 * D))), _full((1, 3 * D)), row(CH), row(CH), row(CH),
                  _full(wao.shape), _full(wco.shape), _full(wxo.shape), _full((D, D)), _full((1, D))],
        out_specs=[row(D), row(D)], out_shape=[SDS((S, D), F32), SDS((S, D), BF16)],
        scratch_shapes=[pltpu.VMEM((tm, D), F32)] * 3)(x, z, bg, attn, u3, oc, wao, wco, wxo, wout, fnw)


def _outproj_bwd(dh1, z, bg, attn, u3, oc, wao, wco, wxo, wout, n_in):
    S, D = dh1.shape
    tm = 256

    def body(dh_ref, zg_ref, bg_ref, a_ref, u_ref, c_ref, wa, wc, wx, wo,
             dz_ref, da_ref, du_ref, dc_ref, dya_ref, dyc_ref, dyx_ref, mg_ref, dbg_ref, ya, yc, yx):
        @pl.when(pl.program_id(0) == 0)
        def _():
            dbg_ref[...] = jnp.zeros_like(dbg_ref)

        _branch_proj(a_ref, wa, ya)
        _branch_proj(u_ref, wc, yc)
        _branch_proj(c_ref, wx, yx)
        dm = lax.dot_general(dh_ref[...].astype(BF16), wo[...], NT_DIMS, preferred_element_type=F32)
        merged = jnp.zeros((tm, D), F32)
        for k, (y, dy_ref, w_ref, db_ref) in enumerate(((ya, dya_ref, wa, da_ref), (yc, dyc_ref, wc, du_ref), (yx, dyx_ref, wx, dc_ref))):
            gk = _gates(zg_ref, bg_ref, k, D)
            yk = y[...]
            merged = merged + gk * yk
            dzg = dm * yk * gk * (1.0 - gk)
            dz_ref[:, k * D:(k + 1) * D] = dzg.astype(BF16)
            dbg_ref[:, k * D:(k + 1) * D] += jnp.sum(dzg, axis=0, keepdims=True)
            dyk = (dm * gk).astype(BF16)
            dy_ref[...] = dyk
            G, _, n = w_ref.shape
            acc = jnp.zeros((tm, CH), F32)
            for g in range(G):
                acc = acc + lax.dot_general(dyk[:, g * n:(g + 1) * n], w_ref[g], NT_DIMS, preferred_element_type=F32)
            db_ref[...] = acc
        mg_ref[...] = merged.astype(BF16)

    row = lambda w: pl.BlockSpec((tm, w), lambda i: (i, 0))
    return pl.pallas_call(
        body, name="outproj_bwd", grid=(S // tm,),
        in_specs=[row(D), pl.BlockSpec((tm, 3 * D), lambda i: (i, O_G // (3 * D))), _full((1, 3 * D)), row(CH), row(CH), row(CH),
                  _full(wao.shape), _full(wco.shape), _full(wxo.shape), _full((D, D))],
        out_specs=[pl.BlockSpec((tm, 3 * D), lambda i: (i, O_G // (3 * D))), row(CH), row(CH), row(CH),
                   row(D), row(D), row(D), row(D), _full((1, 3 * D))],
        out_shape=[SDS((S, n_in), BF16)] + [SDS((S, CH), F32)] * 3 + [SDS((S, D), BF16)] * 4 + [SDS((1, 3 * D), F32)],
        scratch_shapes=[pltpu.VMEM((tm, D), F32)] * 3)(dh1, z, bg, attn, u3, oc, wao, wco, wxo, wout)


FFN_TC = 256
FFN_H = 8


def _ffn_taps(buf, r0):
    xx = buf[pl.ds(r0, FFN_RC + FFN_H), :]
    return xx[FFN_H:], pltpu.roll(xx, 1, 0)[FFN_H:], pltpu.roll(xx, 2, 0)[FFN_H:]


def _ffn_conv(taps, w_ref, b_ref):
    x0, x1, x2 = taps
    return b_ref[...] + x0 * w_ref[2:3, :] + x1 * w_ref[1:2, :] + x2 * w_ref[0:1, :]


def _ffn_act_fwd(up, cw, cb):
    S, F2 = up.shape
    nj = F2 // 2 // FFN_TC
    tm = 1024 if S % 1024 == 0 else TM

    def body(ua, ug, wa, wg, ba, bgt, o_ref, abuf, gbuf):
        i = pl.program_id(1)

        @pl.when(i == 0)
        def _():
            abuf[0:FFN_H, :] = jnp.zeros((FFN_H, FFN_TC), F32)
            gbuf[0:FFN_H, :] = jnp.zeros((FFN_H, FFN_TC), F32)

        @pl.when(i > 0)
        def _():
            abuf[0:FFN_H, :] = abuf[tm:tm + FFN_H, :]
            gbuf[0:FFN_H, :] = gbuf[tm:tm + FFN_H, :]

        abuf[FFN_H:FFN_H + tm, :] = ua[...].astype(F32)
        gbuf[FFN_H:FFN_H + tm, :] = ug[...].astype(F32)
        for r0 in range(0, tm, FFN_RC):
            a = _ffn_conv(_ffn_taps(abuf, r0), wa, ba)
            gt = _ffn_conv(_ffn_taps(gbuf, r0), wg, bgt)
            o_ref[pl.ds(r0, FFN_RC), :] = (gt * _sig(gt) * a).astype(BF16)

    return pl.pallas_call(
        body, name="ffn_act_fwd", grid=(nj, S // tm),
        in_specs=[pl.BlockSpec((tm, FFN_TC), lambda j, i: (i, j)), pl.BlockSpec((tm, FFN_TC), lambda j, i: (i, nj + j)),
                  pl.BlockSpec((FFN_K, FFN_TC), lambda j, i: (0, j)), pl.BlockSpec((FFN_K, FFN_TC), lambda j, i: (0, nj + j)),
                  pl.BlockSpec((1, FFN_TC), lambda j, i: (0, j)), pl.BlockSpec((1, FFN_TC), lambda j, i: (0, nj + j))],
        out_specs=pl.BlockSpec((tm, FFN_TC), lambda j, i: (i, j)), out_shape=SDS((S, F2 // 2), BF16),
        scratch_shapes=[pltpu.VMEM((tm + FFN_H, FFN_TC), F32)] * 2)(up, up, cw, cw, cb, cb)


def _ffn_down_loss(act, wdown, h1, target):
    S, D = h1.shape
    F = act.shape[1]

    def body(a_ref, w_ref, h_ref, t_ref, dy_ref, loss_ref):
        @pl.when(pl.program_id(0) == 0)
        def _():
            loss_ref[...] = jnp.zeros_like(loss_ref)

        err = h_ref[...] + jnp.dot(a_ref[...], w_ref[...], preferred_element_type=F32) - t_ref[...]
        dy_ref[...] = err * (1.0 / D)
        loss_ref[...] += 0.5 * jnp.sum(jnp.mean(err * err, axis=-1, keepdims=True))

    row = lambda w: pl.BlockSpec((TM, w), lambda i: (i, 0))
    return pl.pallas_call(
        body, name="ffn_down_loss", grid=(S // TM,),
        in_specs=[row(F), _full((F, D)), row(D), row(D)],
        out_specs=[row(D), _full((8, 128))], out_shape=[SDS((S, D), F32), SDS((8, 128), F32)])(act, wdown, h1, target)


def _ffn_bwd_a(dy, wdown, up, cw, cb):
    S, D = dy.shape
    F2 = up.shape[1]
    F = F2 // 2
    nj = F // FFN_TC
    tm = TM

    def body(dy_ref, wd_ref, ua, ug, wa, wg, ba, bgt, da_ref, dg_ref, acca_ref, accg_ref, dwd_ref,
             abuf, gbuf, hala, halg, dact_s, act_s):
        i, j = pl.program_id(0), pl.program_id(1)

        @pl.when((i == 0) & (j == 0))
        def _():
            acca_ref[...] = jnp.zeros_like(acca_ref)
            accg_ref[...] = jnp.zeros_like(accg_ref)
            dwd_ref[...] = jnp.zeros_like(dwd_ref)

        @pl.when(i == 0)
        def _():
            abuf[0:FFN_H, :] = jnp.zeros((FFN_H, FFN_TC), F32)
            gbuf[0:FFN_H, :] = jnp.zeros((FFN_H, FFN_TC), F32)

        @pl.when(i > 0)
        def _():
            abuf[0:FFN_H, :] = hala[j]
            gbuf[0:FFN_H, :] = halg[j]

        abuf[FFN_H:FFN_H + tm, :] = ua[...].astype(F32)
        gbuf[FFN_H:FFN_H + tm, :] = ug[...].astype(F32)
        hala[j] = abuf[tm:tm + FFN_H, :]
        halg[j] = gbuf[tm:tm + FFN_H, :]
        dyb = dy_ref[...].astype(BF16)
        dact_s[...] = lax.dot_general(dyb, wd_ref[...], NT_DIMS, preferred_element_type=F32)
        zero8 = jnp.zeros((8, FFN_TC), F32)
        pa, pg = [zero8] * (FFN_K + 1), [zero8] * (FFN_K + 1)
        for r0 in range(0, tm, FFN_RC):
            rows = pl.ds(r0, FFN_RC)
            ta, tg = _ffn_taps(abuf, r0), _ffn_taps(gbuf, r0)
            a = _ffn_conv(ta, wa, ba)
            gt = _ffn_conv(tg, wg, bgt)
            dact = dact_s[rows, :]
            sg = _sig(gt)
            silu = gt * sg
            act_s[rows, :] = (silu * a).astype(BF16)
            dac = dact * silu
            dgc = dact * a * (sg * (1.0 + gt * (1.0 - sg)))
            da_ref[rows, :] = dac.astype(BF16)
            dg_ref[rows, :] = dgc.astype(BF16)
            for k in range(FFN_K):
                pa[k] = pa[k] + _fold8(dac * ta[FFN_K - 1 - k])
                pg[k] = pg[k] + _fold8(dgc * tg[FFN_K - 1 - k])
            pa[FFN_K] = pa[FFN_K] + _fold8(dac)
            pg[FFN_K] = pg[FFN_K] + _fold8(dgc)
        for k in range(FFN_K + 1):
            acca_ref[j, k:k + 1, :] += jnp.sum(pa[k], axis=0, keepdims=True)
            accg_ref[j, k:k + 1, :] += jnp.sum(pg[k], axis=0, keepdims=True)
        dwd_ref[pl.ds(pl.multiple_of(j * FFN_TC, FFN_TC), FFN_TC), :] += lax.dot_general(
            act_s[...], dyb, TN_DIMS, preferred_element_type=F32)

    return pl.pallas_call(
        body, name="ffn_bwd_a", grid=(S // tm, nj),
        in_specs=[pl.BlockSpec((tm, D), lambda i, j: (i, 0)), pl.BlockSpec((FFN_TC, D), lambda i, j: (j, 0)),
                  pl.BlockSpec((tm, FFN_TC), lambda i, j: (i, j)), pl.BlockSpec((tm, FFN_TC), lambda i, j: (i, nj + j)),
                  pl.BlockSpec((FFN_K, FFN_TC), lambda i, j: (0, j)), pl.BlockSpec((FFN_K, FFN_TC), lambda i, j: (0, nj + j)),
                  pl.BlockSpec((1, FFN_TC), lambda i, j: (0, j)), pl.BlockSpec((1, FFN_TC), lambda i, j: (0, nj + j))],
        out_specs=[pl.BlockSpec((tm, FFN_TC), lambda i, j: (i, j))] * 2 + [_full((nj, 8, FFN_TC))] * 2 + [_full((F, D))],
        out_shape=[SDS((S, F), BF16)] * 2 + [SDS((nj, 8, FFN_TC), F32)] * 2 + [SDS((F, D), F32)],
        scratch_shapes=[pltpu.VMEM((tm + FFN_H, FFN_TC), F32)] * 2 + [pltpu.VMEM((nj, FFN_H, FFN_TC), F32)] * 2
        + [pltpu.VMEM((tm, FFN_TC), F32), pltpu.VMEM((tm, FFN_TC), BF16)],
    )(dy, wdown, up, up, cw, cw, cb, cb)


def _ffn_bwd_b(dca, dcg, cw):
    S, F = dca.shape
    nj = F // FFN_TC
    tm = 1024 if S % 1024 == 0 else TM
    nt = S // tm
    span = FFN_RC + FFN_H

    def body(a_ref, g_ref, w_ref, o_ref, ybuf):
        j, i = pl.program_id(0), pl.program_id(1)

        @pl.when(i == 0)
        def _():
            ybuf[tm:tm + FFN_H, :] = jnp.zeros((FFN_H, FFN_TC), F32)

        @pl.when(i > 0)
        def _():
            ybuf[tm:tm + FFN_H, :] = ybuf[0:FFN_H, :]

        ybuf[0:tm, :] = jnp.where(j < nj, a_ref[...], g_ref[...]).astype(F32)
        for r0 in range(0, tm, FFN_RC):
            yy = ybuf[pl.ds(r0, span), :]
            acc = yy[:FFN_RC] * w_ref[2:3, :] + pltpu.roll(yy, span - 1, 0)[:FFN_RC] * w_ref[1:2, :] \
                + pltpu.roll(yy, span - 2, 0)[:FFN_RC] * w_ref[0:1, :]
            o_ref[pl.ds(r0, FFN_RC), :] = acc.astype(BF16)

    return pl.pallas_call(
        body, name="ffn_bwd_b", grid=(2 * nj, nt),
        in_specs=[pl.BlockSpec((tm, FFN_TC), lambda j, i: (nt - 1 - i, jnp.minimum(j, nj - 1))),
                  pl.BlockSpec((tm, FFN_TC), lambda j, i: (nt - 1 - i, jnp.maximum(j - nj, 0))),
                  pl.BlockSpec((FFN_K, FFN_TC), lambda j, i: (0, j))],
        out_specs=pl.BlockSpec((tm, FFN_TC), lambda j, i: (nt - 1 - i, j)), out_shape=SDS((S, 2 * F), BF16),
        scratch_shapes=[pltpu.VMEM((tm + FFN_H, FFN_TC), F32)])(dca, dcg, cw)


def _adamw_update(w_ref, g_ref, m_ref, v_ref, d_ref, nm_ref, nv_ref):
    gv = g_ref[...]
    m2 = ADAM_B1 * m_ref[...] + (1.0 - ADAM_B1) * gv
    v2 = ADAM_B2 * v_ref[...] + (1.0 - ADAM_B2) * jnp.square(gv)
    m_hat = m2 / (1.0 - ADAM_B1 ** ADAM_STEP)
    v_hat = v2 / (1.0 - ADAM_B2 ** ADAM_STEP)
    d_ref[...] = -ADAM_LR * (m_hat / (jnp.sqrt(v_hat) + ADAM_EPS) + ADAM_WD * w_ref[...])
    nm_ref[...] = m2
    nv_ref[...] = v2


def _adamw_small(ws, gs, ms, vs):
    n = len(ws)

    def body(*refs):
        for i in range(n):
            _adamw_update(*[refs[k * n + i] for k in range(7)])

    shapes = [SDS(w.shape, F32) for w in ws]
    res = pl.pallas_call(body, name="adamw_small", out_shape=shapes * 3)(*ws, *gs, *ms, *vs)
    return res[:n], res[n:2 * n], res[2 * n:]


def _adamw(w, g, m, v, name):
    R, C = w.shape
    tr = _row_tile(R, max(8, (2 ** 19) // (4 * C) // 8 * 8))

    def body(w_ref, g_ref, m_ref, v_ref, d_ref, nm_ref, nv_ref):
        _adamw_update(w_ref, g_ref, m_ref, v_ref, d_ref, nm_ref, nv_ref)

    blk = pl.BlockSpec((tr, C), lambda i: (i, 0))
    return pl.pallas_call(
        body, name=name, grid=(R // tr,), in_specs=[blk] * 4, out_specs=[blk] * 3,
        out_shape=[SDS((R, C), F32)] * 3)(w, g, m, v)


HBM_SPEC = pl.BlockSpec(memory_space=pltpu.HBM)
SEM_SPEC = pl.BlockSpec(memory_space=pltpu.SEMAPHORE)
DATAFLOW_EFFECT = pltpu.SideEffectType.DATAFLOW_SIDE_EFFECTING


def _position():
    return lax.axis_index("x"), lax.axis_index("y"), lax.axis_index("c")


def _other_chips(x, y):
    return [(1 - x, y), (x, 1 - y), (1 - x, 1 - y)]


def _all_gather_xy(arrs, name):
    n = len(arrs)
    hbm = pl.BlockSpec(memory_space=pl.ANY)

    def body(*refs):
        ins, outs = refs[:n], refs[n:2 * n]
        send_sems, recv_sems = refs[2 * n:]
        x, y, c = _position()
        me = 2 * x + y
        chips = _other_chips(x, y)

        def rcopy(i, k, src, dst, to):
            return pltpu.make_async_remote_copy(src_ref=src, dst_ref=dst, send_sem=send_sems.at[i, k], recv_sem=recv_sems.at[i, k],
                                                device_id=to, device_id_type=MESH)

        sends = []
        for i in range(n):
            own = rcopy(i, 6, ins[i], outs[i].at[me], (x, y, 1 - c))
            own.start()
            sends.append(own)
        for i in range(n):
            for j, (px, py) in enumerate(chips):
                cp = rcopy(i, j, ins[i].at[c], outs[i].at[me, c], (px, py, c))
                cp.start()
                sends.append(cp)
        for i in range(n):
            for j, (px, py) in enumerate(chips):
                got = outs[i].at[2 * px + py, c]
                rcopy(i, j, ins[i].at[c], got, (x, y, c)).wait_recv()
                fwd = rcopy(i, 3 + j, got, got, (x, y, 1 - c))
                fwd.start()
                sends.append(fwd)
        for i in range(n):
            for j, (px, py) in enumerate(chips):
                theirs = outs[i].at[2 * px + py, 1 - c]
                rcopy(i, 3 + j, theirs, theirs, (x, y, c)).wait_recv()
        for i in range(n):
            rcopy(i, 6, ins[i], outs[i].at[me], (x, y, c)).wait_recv()
        for cp in sends:
            cp.wait_send()

    return pl.pallas_call(
        body, name=name, in_specs=[hbm] * n, out_specs=[hbm] * n,
        out_shape=[SDS((4,) + a.shape, a.dtype) for a in arrs],
        scratch_shapes=[pltpu.SemaphoreType.DMA((n, 7)), pltpu.SemaphoreType.DMA((n, 7))])(*arrs)


def _ag_ici_start(arrs, name):
    n = len(arrs)

    def body(*refs):
        ins, lands = refs[:n], refs[n:2 * n]
        send_sems, recv_sems = refs[2 * n:2 * n + 2]
        token = refs[-1]
        x, y, c = _position()
        for i in range(n):
            for j, (px, py) in enumerate(_other_chips(x, y)):
                pltpu.make_async_remote_copy(src_ref=ins[i].at[c], dst_ref=lands[i].at[2 * x + y, c], send_sem=send_sems.at[3 * i + j],
                                             recv_sem=recv_sems.at[3 * i + j], device_id=(px, py, c), device_id_type=MESH).start()
        token[...] = jnp.zeros_like(token)

    lands = [lax.empty((4,) + a.shape, a.dtype) for a in arrs]
    res = pl.pallas_call(
        body, name=name,
        out_shape=[pltpu.SemaphoreType.DMA((3 * n,)), pltpu.SemaphoreType.DMA((3 * n,))]
        + [pltpu.HBM(a.shape, a.dtype) for a in arrs] + [pltpu.HBM(l.shape, l.dtype) for l in lands] + [SDS((8, 128), F32)],
        in_specs=[HBM_SPEC] * (2 * n), out_specs=[SEM_SPEC, SEM_SPEC] + [HBM_SPEC] * (2 * n) + [pl.BlockSpec(memory_space=pltpu.VMEM)],
        input_output_aliases={i: 2 + i for i in range(2 * n)},
        compiler_params=pltpu.CompilerParams(has_side_effects=DATAFLOW_EFFECT),
    )(*[pltpu.with_memory_space_constraint(a, pltpu.HBM) for a in list(arrs) + lands])
    return res[0], res[1], list(res[2:2 + n]), list(res[2 + n:2 + 2 * n]), res[-1]


def _ag_ici_wait(send_sems, recv_sems, ins, lands, after, name):
    n = len(ins)

    def body(*refs):
        ins_r, lands_r = refs[:n], refs[n:2 * n]
        send_r, recv_r = refs[2 * n:2 * n + 2]
        x, y, c = _position()
        for i in range(n):
            for j, (px, py) in enumerate(_other_chips(x, y)):
                cp = pltpu.make_async_remote_copy(src_ref=ins_r[i].at[c], dst_ref=lands_r[i].at[2 * px + py, c], send_sem=send_r.at[3 * i + j],
                                                  recv_sem=recv_r.at[3 * i + j], device_id=(px, py, c), device_id_type=MESH)
                cp.wait_send()
                cp.wait_recv()

    res = pl.pallas_call(
        body, name=name,
        out_shape=[pltpu.HBM(a.shape, a.dtype) for a in list(ins) + list(lands)],
        in_specs=[HBM_SPEC] * (2 * n) + [SEM_SPEC, SEM_SPEC, pl.BlockSpec(memory_space=pl.ANY)], out_specs=[HBM_SPEC] * (2 * n),
        input_output_aliases={i: i for i in range(2 * n)},
        compiler_params=pltpu.CompilerParams(has_side_effects=DATAFLOW_EFFECT),
    )(*ins, *lands, send_sems, recv_sems, after)
    return list(res[:n]), list(res[n:])


def _ag_finish(arrs, lands, name):
    n = len(arrs)
    hbm = pl.BlockSpec(memory_space=pl.ANY)

    def body(*refs):
        ins, landed, outs = refs[:n], refs[n:2 * n], refs[2 * n:3 * n]
        send_sems, recv_sems = refs[3 * n:]
        x, y, c = _position()
        chips = _other_chips(x, y)
        sends = []
        for i in range(n):
            own = pltpu.make_async_remote_copy(src_ref=ins[i], dst_ref=outs[i].at[2 * x + y], send_sem=send_sems.at[i, 3],
                                               recv_sem=recv_sems.at[i, 3], device_id=(x, y, 1 - c), device_id_type=MESH)
            own.start()
            sends.append(own)
        for i in range(n):
            for j, (px, py) in enumerate(chips):
                fwd = pltpu.make_async_remote_copy(src_ref=landed[i].at[2 * px + py, c], dst_ref=outs[i].at[2 * px + py, c],
                                                   send_sem=send_sems.at[i, j], recv_sem=recv_sems.at[i, j],
                                                   device_id=(x, y, 1 - c), device_id_type=MESH)
                fwd.start()
                sends.append(fwd)
        for i in range(n):
            for j, (px, py) in enumerate(chips):
                theirs = outs[i].at[2 * px + py, 1 - c]
                pltpu.make_async_remote_copy(src_ref=theirs, dst_ref=theirs, send_sem=send_sems.at[i, j], recv_sem=recv_sems.at[i, j],
                                             device_id=(x, y, c), device_id_type=MESH).wait_recv()
        for i in range(n):
            pltpu.make_async_remote_copy(src_ref=ins[i], dst_ref=outs[i].at[2 * x + y], send_sem=send_sems.at[i, 3],
                                         recv_sem=recv_sems.at[i, 3], device_id=(x, y, c), device_id_type=MESH).wait_recv()
        for cp in sends:
            cp.wait_send()

    return pl.pallas_call(
        body, name=name, in_specs=[hbm] * (2 * n), out_specs=[hbm] * n,
        out_shape=[SDS(l.shape, l.dtype) for l in lands],
        input_output_aliases={n + i: i for i in range(n)},
        scratch_shapes=[pltpu.SemaphoreType.DMA((n, 4)), pltpu.SemaphoreType.DMA((n, 4))])(*arrs, *lands)


def _swap_halves(gs, name):
    n = len(gs)
    hbm = pl.BlockSpec(memory_space=pl.ANY)

    def body(*refs):
        ins, outs = refs[:n], refs[n:2 * n]
        send_sems, recv_sems = refs[2 * n:]
        x, y, c = _position()
        cps = []
        for i in range(n):
            for p in range(4):
                cp = pltpu.make_async_remote_copy(src_ref=ins[i].at[p, 1 - c], dst_ref=outs[i].at[p], send_sem=send_sems.at[4 * i + p],
                                                  recv_sem=recv_sems.at[4 * i + p], device_id=(x, y, 1 - c), device_id_type=MESH)
                cp.start()
                cps.append(cp)
        for cp in cps:
            cp.wait()

    return pl.pallas_call(
        body, name=name, in_specs=[hbm] * n, out_specs=[hbm] * n,
        out_shape=[SDS((4,) + g.shape[2:], g.dtype) for g in gs],
        scratch_shapes=[pltpu.SemaphoreType.DMA((4 * n,)), pltpu.SemaphoreType.DMA((4 * n,))])(*gs)


def _exchange_start(s1s, name):
    n = len(s1s)

    def body(*refs):
        srcs, lands = refs[:n], refs[n:2 * n]
        send_sems, recv_sems = refs[2 * n:2 * n + 2]
        token = refs[-1]
        x, y, c = _position()
        for i in range(n):
            for j, (px, py) in enumerate(_other_chips(x, y)):
                pltpu.make_async_remote_copy(src_ref=srcs[i].at[2 * px + py], dst_ref=lands[i].at[j], send_sem=send_sems.at[3 * i + j],
                                             recv_sem=recv_sems.at[3 * i + j], device_id=(px, py, c), device_id_type=MESH).start()
        token[...] = jnp.zeros_like(token)

    lands = [lax.empty((3,) + s.shape[1:], F32) for s in s1s]
    res = pl.pallas_call(
        body, name=name,
        out_shape=[pltpu.SemaphoreType.DMA((3 * n,)), pltpu.SemaphoreType.DMA((3 * n,))]
        + [pltpu.HBM(a.shape, F32) for a in list(s1s) + lands] + [SDS((8, 128), F32)],
        in_specs=[HBM_SPEC] * (2 * n), out_specs=[SEM_SPEC, SEM_SPEC] + [HBM_SPEC] * (2 * n) + [pl.BlockSpec(memory_space=pltpu.VMEM)],
        input_output_aliases={i: 2 + i for i in range(2 * n)},
        compiler_params=pltpu.CompilerParams(has_side_effects=DATAFLOW_EFFECT),
    )(*[pltpu.with_memory_space_constraint(a, pltpu.HBM) for a in list(s1s) + lands])
    return res[0], res[1], list(res[2:2 + n]), list(res[2 + n:2 + 2 * n]), res[-1]


def _exchange_wait(send_sems, recv_sems, s1s, lands, after, name):
    n = len(s1s)

    def body(*refs):
        srcs, lands_r = refs[:n], refs[n:2 * n]
        send_r, recv_r = refs[2 * n:2 * n + 2]
        x, y, c = _position()
        for i in range(n):
            for j, (px, py) in enumerate(_other_chips(x, y)):
                cp = pltpu.make_async_remote_copy(src_ref=srcs[i].at[2 * px + py], dst_ref=lands_r[i].at[j], send_sem=send_r.at[3 * i + j],
                                                  recv_sem=recv_r.at[3 * i + j], device_id=(px, py, c), device_id_type=MESH)
                cp.wait_send()
                cp.wait_recv()

    res = pl.pallas_call(
        body, name=name, out_shape=[pltpu.HBM(a.shape, F32) for a in list(s1s) + list(lands)],
        in_specs=[HBM_SPEC] * (2 * n) + [SEM_SPEC, SEM_SPEC, pl.BlockSpec(memory_space=pl.ANY)], out_specs=[HBM_SPEC] * (2 * n),
        input_output_aliases={i: i for i in range(2 * n)},
        compiler_params=pltpu.CompilerParams(has_side_effects=DATAFLOW_EFFECT),
    )(*s1s, *lands, send_sems, recv_sems, after)
    return list(res[:n]), list(res[n:])


def _join_halves(f2s, name):
    n = len(f2s)
    hbm = pl.BlockSpec(memory_space=pl.ANY)

    def body(*refs):
        ins, outs = refs[:n], refs[n:2 * n]
        send_sems, recv_sems = refs[2 * n:]
        x, y, c = _position()
        cps = []
        for i in range(n):
            cp = pltpu.make_async_remote_copy(src_ref=ins[i].at[c], dst_ref=outs[i].at[c], send_sem=send_sems.at[i],
                                              recv_sem=recv_sems.at[i], device_id=(x, y, 1 - c), device_id_type=MESH)
            cp.start()
            cps.append(cp)
        for i in range(n):
            pltpu.make_async_remote_copy(src_ref=ins[i].at[1 - c], dst_ref=outs[i].at[1 - c], send_sem=send_sems.at[i],
                                         recv_sem=recv_sems.at[i], device_id=(x, y, 1 - c), device_id_type=MESH).wait_recv()
        for cp in cps:
            cp.wait_send()

    return pl.pallas_call(
        body, name=name, in_specs=[hbm] * n, out_specs=[hbm] * n, out_shape=[SDS(f.shape, f.dtype) for f in f2s],
        input_output_aliases={i: i for i in range(n)},
        scratch_shapes=[pltpu.SemaphoreType.DMA((n,)), pltpu.SemaphoreType.DMA((n,))])(*f2s)


def _sum_tile(rows, cols):
    return _row_tile(rows, max(8, (2 ** 18 // cols) // 8 * 8))


def _add_pair(g, r1, c, name):
    _, R, C = r1.shape
    tr = _sum_tile(R, C)

    def body(c_ref, a_ref, b_ref, o_ref):
        del c_ref
        o_ref[...] = a_ref[...] + b_ref[...]

    blk = pl.BlockSpec((None, tr, C), lambda p, i, cr: (p, i, 0))
    return pl.pallas_call(
        body, name=name,
        grid_spec=pltpu.PrefetchScalarGridSpec(
            num_scalar_prefetch=1, grid=(4, R // tr),
            in_specs=[pl.BlockSpec((None, None, tr, C), lambda p, i, cr: (p, cr[0], i, 0)), blk], out_specs=blk),
        out_shape=SDS((4, R, C), F32))(c, g, r1)


def _add_four(s1, r2, me_c, name):
    _, R, C = s1.shape
    tr = _sum_tile(R, C)

    def body(m_ref, a_ref, b_ref, o_ref):
        del m_ref
        o_ref[...] = ((a_ref[...] + b_ref[0]) + b_ref[1]) + b_ref[2]

    return pl.pallas_call(
        body, name=name,
        grid_spec=pltpu.PrefetchScalarGridSpec(
            num_scalar_prefetch=1, grid=(R // tr,),
            in_specs=[pl.BlockSpec((None, tr, C), lambda i, mr: (mr[0], i, 0)), pl.BlockSpec((3, tr, C), lambda i, mr: (0, i, 0))],
            out_specs=pl.BlockSpec((None, tr, C), lambda i, mr: (mr[1], i, 0))),
        out_shape=SDS((2, R, C), F32))(me_c, s1, r2)


def _all_reduce_small(vs):
    n = len(vs)

    def body(*refs):
        ins, outs, bufs = refs[:n], refs[n:2 * n], refs[2 * n:3 * n]
        send_sems, recv_sems = refs[3 * n:]
        x, y, c = _position()
        me = 4 * x + 2 * y + c
        for i in range(n):
            bufs[i][me] = ins[i][...]
        cps = []
        for k in range(1, 8):
            to = (1 - x if k & 4 else x, 1 - y if k & 2 else y, 1 - c if k & 1 else c)
            for i in range(n):
                cp = pltpu.make_async_remote_copy(src_ref=bufs[i].at[me], dst_ref=bufs[i].at[me], send_sem=send_sems.at[7 * i + k - 1],
                                                  recv_sem=recv_sems.at[7 * i + k - 1], device_id=to, device_id_type=MESH)
                cp.start()
                cps.append(cp)
        for cp in cps:
            cp.wait_send()
        for k in range(1, 8):
            src = 4 * (1 - x if k & 4 else x) + 2 * (1 - y if k & 2 else y) + (1 - c if k & 1 else c)
            for i in range(n):
                pltpu.make_async_remote_copy(src_ref=bufs[i].at[src], dst_ref=bufs[i].at[src], send_sem=send_sems.at[7 * i + k - 1],
                                             recv_sem=recv_sems.at[7 * i + k - 1], device_id=(x, y, c), device_id_type=MESH).wait_recv()
        for i in range(n):
            acc = bufs[i][0]
            for k in range(1, 8):
                acc = acc + bufs[i][k]
            outs[i][...] = acc

    vm = pl.BlockSpec(memory_space=pltpu.VMEM)
    return pl.pallas_call(
        body, name="all_reduce_small", in_specs=[vm] * n, out_specs=[vm] * n, out_shape=[SDS(v.shape, F32) for v in vs],
        scratch_shapes=[pltpu.VMEM((8,) + v.shape, F32) for v in vs]
        + [pltpu.SemaphoreType.DMA((7 * n,)), pltpu.SemaphoreType.DMA((7 * n,))])(*vs)


def _reduce_begin(grads, tag):
    _, _, c = _position()
    cs = jnp.reshape(c, (1,)).astype(jnp.int32)
    g4 = [g.reshape(4, 2, g.shape[1] // 2, g.shape[2]) for g in grads]
    r1 = _swap_halves(g4, "rs_swap_" + tag)
    s1 = [_add_pair(g, r, cs, f"rs_add_pair_{tag}{i}") for i, (g, r) in enumerate(zip(g4, r1))]
    send_sems, recv_sems, s1, lands, token = _exchange_start(s1, "rs_exchange_start_" + tag)
    return (send_sems, recv_sems, s1, lands), token


def _reduce_end(state, after, tag):
    x, y, c = _position()
    send_sems, recv_sems, s1, lands = state
    s1, lands = _exchange_wait(send_sems, recv_sems, s1, lands, after, "rs_exchange_wait_" + tag)
    me_c = jnp.stack([2 * x + y, c]).astype(jnp.int32)
    f2 = [_add_four(s, l, me_c, f"rs_add_four_{tag}{i}") for i, (s, l) in enumerate(zip(s1, lands))]
    return [f.reshape(2 * f.shape[1], f.shape[2]) for f in _join_halves(f2, "rs_join_" + tag)]


def _halves(a):
    return a.reshape((2, a.shape[0] // 2) + a.shape[1:])


def _after(a, token):
    return a + token[0, 0]


def _local_step(x, mem, target, sp, w_in, ex):
    S, D = x.shape
    n_in = 4 * w_in.shape[2]
    bw = {"w_in": w_in}
    band, buckets = _bias_static()
    buckets = jnp.asarray(buckets)

    xn = _rmsnorm(x, _after(sp["attn_norm_w"], ex.start_rest()), "rms_in")
    z = _mm_nn(xn, bw["w_in"], BF16, "in_proj")
    qh, kh, vh = _qkv_prep(z, sp["q_norm_w"], sp["k_norm_w"])
    tab = sp["rel_bias_table"].T.reshape(N_GROUPS, HPG, N_BUCKETS)
    bias = _bias_fwd(jnp.pad(tab, ((0, 0), (0, 8 - HPG), (0, 0))), buckets)
    biasm = jnp.where(jnp.asarray(band)[None, None], bias[:, :HPG].reshape(N_GROUPS, HPG, NQ, 2 * NQ), NEG)
    os_, lses = [], []
    for g, (_, dil) in enumerate(ATTN_GROUPS):
        o_g, lse_g = _attn_fwd(qh, kh, vh, biasm[g], g, dil)
        os_.append(o_g)
        lses.append(lse_g)
    attn = _merge_fwd(os_, lses)
    u1, u3 = _conv_fwd(z, sp["conv_dw_w"], sp["conv_dw_b"], sp["conv_ln_w"], sp["conv_ln_b"])
    bw.update(ex.rest_weights(after=attn))
    F2 = 4 * bw["w_up"].shape[2]
    mn, kv, mk, mv = _memkv_fwd(mem, sp["mem_norm_w"], bw["w_mem_kv"], sp["xk_norm_w"])
    oc = _cross_fwd(z, sp["xq_norm_w"], mk, mv)
    h1, hn = _outproj_fwd(x, z, sp["b_gate"], attn, u3, oc, bw["w_attn_o"], bw["w_conv_o"], bw["w_cross_o"], bw["w_out"],
                          sp["ffn_norm_w"])
    up = _mm_nn(hn, bw["w_up"], BF16, "ffn_up")
    act = _ffn_act_fwd(up, sp["ffn_conv_w"], sp["ffn_conv_b"])
    dy, loss_tile = _ffn_down_loss(act, bw["w_down"], h1, target)

    gs, gb = {}, {}
    dca, dcg, acca, accg, gb["w_down"] = _ffn_bwd_a(dy, bw["w_down"], up, sp["ffn_conv_w"], sp["ffn_conv_b"])
    tap = lambda acc: jnp.transpose(acc, (1, 0, 2)).reshape(8, F2 // 2)
    ta, tg = tap(acca), tap(accg)
    gs["ffn_conv_w"] = jnp.concatenate([ta[:FFN_K], tg[:FFN_K]], axis=1)
    gs["ffn_conv_b"] = jnp.concatenate([ta[FFN_K:FFN_K + 1], tg[FFN_K:FFN_K + 1]], axis=1)
    dup = _ffn_bwd_b(dca, dcg, sp["ffn_conv_w"])
    gb["w_up"] = _mm_tn(hn, dup, 4, "dw_up")
    tok = ex.reduce_begin("a", ("w_down", "w_up"), gb)
    dh1, gs["ffn_norm_w"] = _norm_in_bwd(dup, bw["w_up"], h1, _after(sp["ffn_norm_w"], tok), dy, "ffn_in_bwd")
    dz, dattn, du3, doc, dya, dyc, dyx, merged, gs["b_gate"] = _outproj_bwd(
        dh1, z, sp["b_gate"], attn, u3, oc, bw["w_attn_o"], bw["w_conv_o"], bw["w_cross_o"], bw["w_out"], n_in)
    gb["w_out"] = _mm_tn(merged, dh1, 1, "dw_out")[0]
    gb["w_attn_o"] = _mm_tn(attn, dya, 4, "dw_attn_o")
    gb["w_conv_o"] = _mm_tn(u3, dyc, 4, "dw_conv_o")
    gb["w_cross_o"] = _mm_tn(oc, dyx, 4, "dw_cross_o")
    dz, dmk, dmv, gs["xq_norm_w"] = _cross_bwd(dz, doc, z, sp["xq_norm_w"], mk, mv)
    gb["w_mem_kv"], gs["xk_norm_w"], gs["mem_norm_w"] = _memkv_bwd(
        dmk, dmv, kv, mem, mn, sp["mem_norm_w"], bw["w_mem_kv"], sp["xk_norm_w"])
    ex.reduce_end("a", after=gs["mem_norm_w"])
    tok = ex.reduce_begin("b", ("w_out", "w_attn_o", "w_conv_o", "w_cross_o", "w_mem_kv"), gb)
    du1, cacc = _conv_bwd_a(du3, u1, z, _after(sp["conv_ln_w"], tok), sp["conv_ln_b"])
    gs["conv_dw_w"], gs["conv_dw_b"] = cacc[:CONV_K], cacc[32:33]
    gs["conv_ln_w"], gs["conv_ln_b"] = cacc[33:34], cacc[34:35]
    dz = _conv_bwd_b(dz, du1, z, sp["conv_dw_w"])
    wg, dah, dhb = _merge_bwd(dattn, os_, lses)
    dqs, dks, dvs, dsbs = [], [], [], []
    for g, (_, dil) in enumerate(ATTN_GROUPS):
        dq_g, dk_g, dv_g, dsb_g = _attn_bwd(qh, kh, vh, biasm[g], dah, wg[g], dhb, lses[g], g, dil)
        dqs.append(dq_g)
        dks.append(dk_g)
        dvs.append(dv_g)
        dsbs.append(dsb_g.reshape(HPG, NQ * 2 * NQ))
    dtab = _bias_bwd(jnp.pad(jnp.stack(dsbs), ((0, 0), (0, 8 - HPG), (0, 0))), buckets)
    gs["rel_bias_table"] = dtab[:, :HPG].reshape(N_GROUPS * HPG, N_BUCKETS).T
    dz, gs["q_norm_w"], gs["k_norm_w"] = _qkv_bwd(dz, z, dqs, dks, dvs, sp["q_norm_w"], sp["k_norm_w"])
    ex.reduce_end("b", after=gs["q_norm_w"])
    gb["w_in"] = _mm_tn(xn, dz, 4, "dw_in")
    tok = ex.reduce_begin("c", ("w_in",), gb)
    dx, gs["attn_norm_w"] = _norm_in_bwd(dz, bw["w_in"], x, _after(sp["attn_norm_w"], tok), dh1, "in_bwd")
    ex.reduce_end("c", after=gs["attn_norm_w"])
    return loss_tile, dx, gs, gb


SMALL = ("rel_bias_table", "attn_norm_w", "b_gate", "q_norm_w", "k_norm_w", "conv_dw_w", "conv_dw_b", "conv_ln_w", "conv_ln_b",
         "mem_norm_w", "xq_norm_w", "xk_norm_w", "ffn_norm_w", "ffn_conv_w", "ffn_conv_b")
SMALL_SHARDED = ("conv_dw_w", "ffn_conv_w")
BIG_COL = ("w_in", "w_attn_o", "w_conv_o", "w_cross_o", "w_up")
BIG_ROW = ("w_mem_kv", "w_out", "w_down")
BIG = BIG_COL + BIG_ROW
WEIGHTS = ("rel_bias_table", "attn_norm_w", "w_in", "b_gate", "q_norm_w", "k_norm_w", "w_attn_o", "conv_dw_w", "conv_dw_b",
           "conv_ln_w", "conv_ln_b", "w_conv_o", "mem_norm_w", "w_mem_kv", "xq_norm_w", "xk_norm_w", "w_cross_o", "w_out",
           "ffn_norm_w", "w_up", "ffn_conv_w", "ffn_conv_b", "w_down")


class _Exchanges:
    REST = tuple(k for k in BIG if k != "w_in")

    def __init__(self, w):
        self.w = w
        self.pending = {}
        self.reduced = {}

    def _whole(self, k, ga):
        ga = ga.reshape((4,) + self.w[k].shape)
        return ga if k in BIG_COL else ga.reshape((4 * self.w[k].shape[0],) + self.w[k].shape[1:])

    def first_weights(self):
        local = [_halves(self.w["w_in"].astype(BF16))]
        for k in SMALL_SHARDED:
            flat = jnp.ravel(self.w[k])
            local.append(jnp.pad(flat, (0, (-flat.shape[0]) % 2048)).reshape(2, -1, 128))
        gathered = _all_gather_xy(local, "gather_first")
        self.first = gathered[0]
        small = {}
        for k, ga in zip(SMALL_SHARDED, gathered[1:]):
            r, cdim = self.w[k].shape
            parts = ga.reshape(4, -1)[:, :r * cdim].reshape(4, r, cdim)
            small[k] = jnp.transpose(parts, (1, 0, 2)).reshape(r, 4 * cdim)
        return self._whole("w_in", gathered[0]), small

    def start_rest(self):
        local = [_halves(self.w[k].astype(BF16)) for k in self.REST]
        local, _ = lax.optimization_barrier((local, self.first))
        send_sems, recv_sems, ins, lands, token = _ag_ici_start(local, "gather_rest_start")
        self.pending["rest"] = (send_sems, recv_sems, ins, lands)
        return token

    def rest_weights(self, after):
        send_sems, recv_sems, ins, lands = self.pending.pop("rest")
        ins, lands = _ag_ici_wait(send_sems, recv_sems, ins, lands, after, "gather_rest_wait")
        gathered = _ag_finish(ins, lands, "gather_rest_finish")
        return {k: self._whole(k, ga) for k, ga in zip(self.REST, gathered)}

    def reduce_begin(self, tag, names, gb):
        parts = [gb[k].reshape((4,) + self.w[k].shape) for k in names]
        state, token = _reduce_begin(parts, tag)
        self.pending[tag] = (state, names)
        return token

    def reduce_end(self, tag, after):
        state, names = self.pending.pop(tag)
        self.reduced.update(zip(names, _reduce_end(state, after, tag)))


def _step(x, mem, target, w, m, v):
    xi, yi, _ = _position()
    shard = 2 * xi + yi
    ex = _Exchanges(w)
    w_in, small_gathered = ex.first_weights()
    sp = {k: w[k] for k in SMALL if k not in SMALL_SHARDED}
    sp.update(small_gathered)

    loss_tile, dx, gs, _ = _local_step(x, mem, target, sp, w_in, ex)
    g_big = ex.reduced

    red = _all_reduce_small([loss_tile] + [gs[k] for k in SMALL])
    loss = red[0][0, 0]
    g_small = dict(zip(SMALL, red[1:]))
    for k in SMALL_SHARDED:
        cdim = w[k].shape[1]
        g_small[k] = lax.dynamic_slice_in_dim(g_small[k], shard * cdim, cdim, axis=1)

    grads, delta, new_m, new_v = {}, {}, {}, {}
    for k in BIG:
        grads[k] = g_big[k]
        delta[k], new_m[k], new_v[k] = _adamw(w[k], g_big[k], m[k], v[k], "adamw_" + k)
    outs = _adamw_small([w[k] for k in SMALL], [g_small[k] for k in SMALL], [m[k] for k in SMALL], [v[k] for k in SMALL])
    for dst, vals in zip((delta, new_m, new_v), outs):
        dst.update(zip(SMALL, vals))
    grads.update(g_small)
    return loss, dx, grads, delta, new_m, new_v


def kernel(x, mem, rel_bias_table, attn_norm_w, w_in, b_gate, q_norm_w, k_norm_w, w_attn_o, conv_dw_w, conv_dw_b, conv_ln_w, conv_ln_b, w_conv_o, mem_norm_w, w_mem_kv, xq_norm_w, xk_norm_w, w_cross_o, w_out, ffn_norm_w, w_up, ffn_conv_w, ffn_conv_b, w_down, loss_target, m_rel_bias_table, m_attn_norm_w, m_w_in, m_b_gate, m_q_norm_w, m_k_norm_w, m_w_attn_o, m_conv_dw_w, m_conv_dw_b, m_conv_ln_w, m_conv_ln_b, m_w_conv_o, m_mem_norm_w, m_w_mem_kv, m_xq_norm_w, m_xk_norm_w, m_w_cross_o, m_w_out, m_ffn_norm_w, m_w_up, m_ffn_conv_w, m_ffn_conv_b, m_w_down, v_rel_bias_table, v_attn_norm_w, v_w_in, v_b_gate, v_q_norm_w, v_k_norm_w, v_w_attn_o, v_conv_dw_w, v_conv_dw_b, v_conv_ln_w, v_conv_ln_b, v_w_conv_o, v_mem_norm_w, v_w_mem_kv, v_xq_norm_w, v_xk_norm_w, v_w_cross_o, v_w_out, v_ffn_norm_w, v_w_up, v_ffn_conv_w, v_ffn_conv_b, v_w_down):
    args = locals()
    def block(name, k):
        a = args[name] if k == "rel_bias_table" else args[name][0]
        return a.reshape(1, -1) if a.ndim == 1 else a

    w = {k: block(k, k) for k in WEIGHTS}
    m = {k: block("m_" + k, k) for k in WEIGHTS}
    v = {k: block("v_" + k, k) for k in WEIGHTS}
    loss, dx, grads, delta, new_m, new_v = _step(x[0], mem[0], loss_target[0], w, m, v)
    out = [loss, dx[None]]
    for d in (grads, delta, new_m, new_v):
        for k in WEIGHTS:
            out.append(d[k].reshape(args[k].shape))
    return tuple(out)
```

```python
import functools
import math

import numpy as np
import jax
import jax.numpy as jnp
from jax import lax
from jax.experimental import pallas as pl
from jax.experimental.pallas import tpu as pltpu

F32, BF16 = jnp.float32, jnp.bfloat16
SDS = jax.ShapeDtypeStruct
MESH = pl.DeviceIdType.MESH

HEAD = 128
N_GROUPS, HPG = 3, 4
ATTN_GROUPS = ((128, 1), (512, 4), (2048, 16))
NQ = 128
QKV_W = N_GROUPS * HPG * HEAD
CH = 512
CONV_K, FFN_K = 31, 3
N_BUCKETS, MAX_DIST = 32, 2048
RMS_EPS, LN_EPS = 1e-6, 1e-5
O_Q, O_K, O_V, O_CV, O_CG, O_XQ, O_G = 0, QKV_W, 2 * QKV_W, 3 * QKV_W, 3 * QKV_W + CH, 3 * QKV_W + 2 * CH, 3 * QKV_W + 3 * CH
ADAM_LR, ADAM_B1, ADAM_B2, ADAM_EPS, ADAM_WD, ADAM_STEP = 0.001, 0.9, 0.999, 1e-08, 0.01, 10
NEG = -1e30
SCALE = HEAD ** -0.5
TM = 512
ATT_RB = 2048
NT_DIMS = (((1,), (1,)), ((), ()))
TN_DIMS = (((0,), (0,)), ((), ()))


CONV_RC = 16
FFN_RC = 32


def _sig(v):
    return 0.5 * jnp.tanh(0.5 * v) + 0.5


def _fold8(v):
    acc = v[0:8]
    for r in range(8, v.shape[0], 8):
        acc = acc + v[r:r + 8]
    return acc


def _row_tile(rows, cap, mult=8):
    best = None
    for t in range(mult, min(rows, cap) + 1, mult):
        if rows % t == 0:
            best = t
    return best if best is not None else rows


def _full(shape):
    n = len(shape)
    return pl.BlockSpec(shape, lambda *a: (0,) * n)


def _rmsnorm(x, w, name):
    S, D = x.shape

    def body(x_ref, w_ref, o_ref):
        xv = x_ref[...]
        r = lax.rsqrt(jnp.mean(xv * xv, axis=-1, keepdims=True) + RMS_EPS)
        o_ref[...] = (xv * r * w_ref[...]).astype(BF16)

    return pl.pallas_call(
        body, name=name, grid=(S // TM,),
        in_specs=[pl.BlockSpec((TM, D), lambda i: (i, 0)), _full((1, D))],
        out_specs=pl.BlockSpec((TM, D), lambda i: (i, 0)),
        out_shape=SDS((S, D), BF16))(x, w)


def _mm_nn(a, b, out_dtype, name):
    M, K = a.shape
    G, _, n = b.shape

    def body(a_ref, b_ref, o_ref):
        o_ref[...] = jnp.dot(a_ref[...].astype(BF16), b_ref[...], preferred_element_type=F32).astype(out_dtype)

    return pl.pallas_call(
        body, name=name, grid=(G, M // TM),
        in_specs=[pl.BlockSpec((TM, K), lambda g, i: (i, 0)), pl.BlockSpec((None, K, n), lambda g, i: (g, 0, 0))],
        out_specs=pl.BlockSpec((TM, n), lambda g, i: (i, g)),
        out_shape=SDS((M, G * n), out_dtype))(a, b)


def _mm_tn(a, b, G, name):
    S, Ka = a.shape
    n = b.shape[1] // G
    if G * n <= 1024:
        tk = 2048 if S % 2048 == 0 else TM

        def wide_body(a_ref, b_ref, o_ref, acc):
            k = pl.program_id(0)

            @pl.when(k == 0)
            def _():
                acc[...] = jnp.zeros_like(acc)

            acc[...] += lax.dot_general(a_ref[...].astype(BF16), b_ref[...].astype(BF16), TN_DIMS, preferred_element_type=F32)

            @pl.when(k == S // tk - 1)
            def _():
                for g in range(G):
                    o_ref[g] = acc[:, g * n:(g + 1) * n]

        return pl.pallas_call(
            wide_body, name=name, grid=(S // tk,),
            in_specs=[pl.BlockSpec((tk, Ka), lambda k: (k, 0)), pl.BlockSpec((tk, G * n), lambda k: (k, 0))],
            out_specs=_full((G, Ka, n)), out_shape=SDS((G, Ka, n), F32),
            scratch_shapes=[pltpu.VMEM((Ka, G * n), F32)])(a, b)

    tka = Ka
    while tka * n * 4 > 10 * 2 ** 20 and tka % 256 == 0:
        tka //= 2

    def body(a_ref, b_ref, o_ref):
        @pl.when(pl.program_id(2) == 0)
        def _():
            o_ref[...] = jnp.zeros_like(o_ref)
        o_ref[...] += lax.dot_general(a_ref[...].astype(BF16), b_ref[...].astype(BF16), TN_DIMS, preferred_element_type=F32)

    return pl.pallas_call(
        body, name=name, grid=(G, Ka // tka, S // TM),
        in_specs=[pl.BlockSpec((TM, tka), lambda g, i, k: (k, i)), pl.BlockSpec((TM, n), lambda g, i, k: (k, g))],
        out_specs=pl.BlockSpec((None, tka, n), lambda g, i, k: (g, i, 0)),
        out_shape=SDS((G, Ka, n), F32))(a, b)


NORM_RC = 16


def _norm_in_bwd(a, w, xin, nw, resid, name):
    S, K = xin.shape
    G, _, n = w.shape
    tm = TM
    n2, steps = n, G

    def body(a_ref, w_ref, x_ref, nw_ref, r_ref, o_ref, dnw_ref, acc, part):
        i, g = pl.program_id(0), pl.program_id(1)

        @pl.when((i == 0) & (g == 0))
        def _():
            part[...] = jnp.zeros_like(part)

        @pl.when(g == 0)
        def _():
            acc[...] = jnp.zeros_like(acc)

        acc[...] += lax.dot_general(a_ref[...], w_ref[g], NT_DIMS, preferred_element_type=F32)

        @pl.when(g == steps - 1)
        def _():
            for r0 in range(0, tm, NORM_RC):
                rows = pl.ds(r0, NORM_RC)
                dn = acc[rows, :]
                xv = x_ref[rows, :]
                r = lax.rsqrt(jnp.mean(xv * xv, axis=-1, keepdims=True) + RMS_EPS)
                xhat = xv * r
                dyw = dn * nw_ref[...]
                o_ref[rows, :] = r_ref[rows, :] + r * (dyw - xhat * jnp.mean(dyw * xhat, axis=-1, keepdims=True))
                part[...] += _fold8(dn * xhat)

        @pl.when((i == S // tm - 1) & (g == steps - 1))
        def _():
            dnw_ref[...] = jnp.sum(part[...], axis=0, keepdims=True)

    return pl.pallas_call(
        body, name=name, grid=(S // tm, steps),
        in_specs=[pl.BlockSpec((tm, n2), lambda i, g: (i, g)),
                  pl.BlockSpec((G, K, n), lambda i, g: (0, 0, 0), pipeline_mode=pl.Buffered(1)),
                  pl.BlockSpec((tm, K), lambda i, g: (i, 0)), _full((1, K)), pl.BlockSpec((tm, K), lambda i, g: (i, 0))],
        out_specs=[pl.BlockSpec((tm, K), lambda i, g: (i, 0)), _full((1, K))],
        out_shape=[SDS((S, K), F32), SDS((1, K), F32)],
        scratch_shapes=[pltpu.VMEM((tm, K), F32), pltpu.VMEM((8, K), F32)])(a, w, xin, nw, resid)


def _t5_bucket_np(dist):
    max_exact = N_BUCKETS // 2
    d = np.maximum(dist.astype(np.float32), np.float32(1.0))
    large = max_exact + (np.log(d / np.float32(max_exact)) / np.float32(math.log(MAX_DIST / max_exact))
                         * np.float32(N_BUCKETS - max_exact)).astype(np.int32)
    large = np.minimum(large, N_BUCKETS - 1)
    return np.where(dist < max_exact, dist, large).astype(np.int32)


def _bias_static():
    qi = np.arange(NQ)[:, None]
    kj = np.arange(2 * NQ)[None, :]
    step = qi + NQ - kj
    band = (step >= 0) & (step <= NQ)
    buckets = np.stack([_t5_bucket_np(np.clip(step, 0, None) * dil).reshape(1, -1) for _, dil in ATTN_GROUPS])
    return band, buckets


def _bias_fwd(table_t, buckets):
    nb = buckets.shape[-1]

    def body(t_ref, b_ref, o_ref):
        oh = (b_ref[...] == lax.broadcasted_iota(jnp.int32, (N_BUCKETS, nb), 0)).astype(F32)
        o_ref[...] = jnp.dot(t_ref[...], oh, preferred_element_type=F32, precision=lax.Precision.HIGHEST)

    return pl.pallas_call(
        body, name="bias_fwd", grid=(N_GROUPS,),
        in_specs=[pl.BlockSpec((None, 8, N_BUCKETS), lambda g: (g, 0, 0)), pl.BlockSpec((None, 1, nb), lambda g: (g, 0, 0))],
        out_specs=pl.BlockSpec((None, 8, nb), lambda g: (g, 0, 0)),
        out_shape=SDS((N_GROUPS, 8, nb), F32))(table_t, buckets)


def _bias_bwd(dsb, buckets):
    nb = buckets.shape[-1]

    def body(d_ref, b_ref, o_ref):
        oh = (b_ref[...] == lax.broadcasted_iota(jnp.int32, (N_BUCKETS, nb), 0)).astype(F32)
        o_ref[...] = lax.dot_general(d_ref[...], oh, NT_DIMS, preferred_element_type=F32, precision=lax.Precision.HIGHEST)

    return pl.pallas_call(
        body, name="bias_bwd", grid=(N_GROUPS,),
        in_specs=[pl.BlockSpec((None, 8, nb), lambda g: (g, 0, 0)), pl.BlockSpec((None, 1, nb), lambda g: (g, 0, 0))],
        out_specs=pl.BlockSpec((None, 8, N_BUCKETS), lambda g: (g, 0, 0)),
        out_shape=SDS((N_GROUPS, 8, N_BUCKETS), F32))(dsb, buckets)


def _qkv_prep(z, qw, kw):
    S = z.shape[0]
    nh = N_GROUPS * HPG

    def body(zq, zk, zv, qw_ref, kw_ref, qh, kh, vh):
        for h in range(nh):
            g = h // HPG
            sl = slice(h * HEAD, (h + 1) * HEAD)
            xq = zq[:, sl].astype(F32)
            qh[h] = xq * lax.rsqrt(jnp.mean(xq * xq, axis=-1, keepdims=True) + RMS_EPS) * qw_ref[g:g + 1, :]
            xk = zk[:, sl].astype(F32)
            kh[h] = xk * lax.rsqrt(jnp.mean(xk * xk, axis=-1, keepdims=True) + RMS_EPS) * kw_ref[g:g + 1, :]
            vh[h] = zv[:, sl].astype(F32)

    hm = pl.BlockSpec((nh, TM, HEAD), lambda i: (0, i, 0))
    return pl.pallas_call(
        body, name="qkv_prep", grid=(S // TM,),
        in_specs=[pl.BlockSpec((TM, QKV_W), lambda i: (i, 0)), pl.BlockSpec((TM, QKV_W), lambda i: (i, 1)),
                  pl.BlockSpec((TM, QKV_W), lambda i: (i, 2)), _full((N_GROUPS, HEAD)), _full((N_GROUPS, HEAD))],
        out_specs=[hm, hm, hm],
        out_shape=[SDS((nh, S, HEAD), F32)] * 3)(z, z, z, qw, kw)


def _rows(start, d):
    return pl.ds(start, NQ) if d == 1 else pl.ds(start, NQ, stride=d)


def _attn_fwd(qh, kh, vh, biasm, g, d):
    S = qh.shape[1]
    RB = ATT_RB
    nbk, nq = S // RB, RB // (NQ * d)

    def body(q_ref, k_ref, v_ref, bias_ref, o_ref, lse_ref, kbuf, vbuf):
        b = pl.program_id(1)

        @pl.when(b == 0)
        def _():
            kbuf[0:RB, :] = jnp.zeros((RB, HEAD), F32)
            vbuf[0:RB, :] = jnp.zeros((RB, HEAD), F32)

        @pl.when(b > 0)
        def _():
            kbuf[0:RB, :] = kbuf[RB:2 * RB, :]
            vbuf[0:RB, :] = vbuf[RB:2 * RB, :]

        kbuf[RB:2 * RB, :] = k_ref[...]
        vbuf[RB:2 * RB, :] = v_ref[...]
        bias = bias_ref[...]
        col = lax.broadcasted_iota(jnp.int32, (NQ, 2 * NQ), 1)

        for qb in range(nq):
            def unit(r, carry, qb=qb):
                qs = qb * NQ * d + r
                q = q_ref[_rows(qs, d), :].astype(BF16)
                kw = jnp.concatenate([kbuf[_rows(RB + qs - NQ * d, d), :], kbuf[_rows(RB + qs, d), :]], axis=0).astype(BF16)
                vw = jnp.concatenate([vbuf[_rows(RB + qs - NQ * d, d), :], vbuf[_rows(RB + qs, d), :]], axis=0).astype(BF16)
                s = lax.dot_general(q, kw, NT_DIMS, preferred_element_type=F32) * SCALE + bias
                if qb == 0:
                    s = jnp.where((col < NQ) & (b == 0), NEG, s)
                m = jnp.max(s, axis=-1, keepdims=True)
                p = jnp.exp(s - m)
                l = jnp.sum(p, axis=-1, keepdims=True)
                o = jnp.dot(p.astype(BF16), vw, preferred_element_type=F32) / l
                o_ref[_rows(qs, d), :] = o
                lse_ref[_rows(qs, d), :] = jnp.broadcast_to(m + jnp.log(l), (NQ, HEAD))
                return carry

            for r in range(d):
                unit(r, 0)

    blk = lambda f: pl.BlockSpec((None, RB, HEAD), f)
    return pl.pallas_call(
        body, name=f"attn_fwd_g{g}", grid=(HPG, nbk),
        in_specs=[blk(lambda h, b: (HPG * g + h, b, 0))] * 3 + [pl.BlockSpec((None, NQ, 2 * NQ), lambda h, b: (h, 0, 0))],
        out_specs=[blk(lambda h, b: (h, b, 0))] * 2,
        out_shape=[SDS((HPG, S, HEAD), F32)] * 2,
        scratch_shapes=[pltpu.VMEM((2 * RB, HEAD), F32)] * 2)(qh, kh, vh, biasm)


def _attn_bwd(qh, kh, vh, biasm, da, wg, dh, lse, g, d):
    S = qh.shape[1]
    RB = ATT_RB
    nbk, nq = S // RB, RB // (NQ * d)

    def body(q_ref, k_ref, v_ref, bias_ref, da_ref, wg_ref, dh_ref, lse_ref,
             dq_ref, dk_ref, dv_ref, dsb_ref, kbuf, vbuf, dkbuf, dvbuf):
        b = pl.program_id(1)
        zero = jnp.zeros((RB, HEAD), F32)

        @pl.when(b == 0)
        def _():
            kbuf[0:RB, :] = zero
            vbuf[0:RB, :] = zero
            dkbuf[0:RB, :] = zero
            dvbuf[0:RB, :] = zero
            dsb_ref[...] = jnp.zeros_like(dsb_ref)

        @pl.when(b > 0)
        def _():
            kbuf[0:RB, :] = kbuf[RB:2 * RB, :]
            vbuf[0:RB, :] = vbuf[RB:2 * RB, :]
            dkbuf[0:RB, :] = dkbuf[RB:2 * RB, :]
            dvbuf[0:RB, :] = dvbuf[RB:2 * RB, :]

        dkbuf[RB:2 * RB, :] = zero
        dvbuf[RB:2 * RB, :] = zero

        @pl.when(b < nbk)
        def _():
            kbuf[RB:2 * RB, :] = k_ref[...]
            vbuf[RB:2 * RB, :] = v_ref[...]
            bias = bias_ref[...]
            col = lax.broadcasted_iota(jnp.int32, (NQ, 2 * NQ), 1)

            for qb in range(nq):
                def unit(r, carry, qb=qb):
                    qs = qb * NQ * d + r
                    prev, cur = _rows(RB + qs - NQ * d, d), _rows(RB + qs, d)
                    q = q_ref[_rows(qs, d), :].astype(BF16)
                    kw = jnp.concatenate([kbuf[prev, :], kbuf[cur, :]], axis=0).astype(BF16)
                    vw = jnp.concatenate([vbuf[prev, :], vbuf[cur, :]], axis=0).astype(BF16)
                    s = lax.dot_general(q, kw, NT_DIMS, preferred_element_type=F32) * SCALE + bias
                    if qb == 0:
                        s = jnp.where((col < NQ) & (b == 0), NEG, s)
                    p = jnp.exp(s - lse_ref[_rows(qs, d), :][:, 0:1])
                    w = wg_ref[_rows(qs, d), :]
                    do = (da_ref[_rows(qs, d), :] * w).astype(BF16)
                    dp = lax.dot_general(do, vw, NT_DIMS, preferred_element_type=F32)
                    ds = p * (dp - w[:, 0:1] * dh_ref[_rows(qs, d), :][:, 0:1])
                    dsb_ref[...] += ds
                    dsb = ds.astype(BF16)
                    dq_ref[_rows(qs, d), :] = jnp.dot(dsb, kw, preferred_element_type=F32) * SCALE
                    dkw = lax.dot_general(dsb, q, TN_DIMS, preferred_element_type=F32) * SCALE
                    dvw = lax.dot_general(p.astype(BF16), do, TN_DIMS, preferred_element_type=F32)
                    dkbuf[prev, :] += dkw[0:NQ, :]
                    dkbuf[cur, :] += dkw[NQ:2 * NQ, :]
                    dvbuf[prev, :] += dvw[0:NQ, :]
                    dvbuf[cur, :] += dvw[NQ:2 * NQ, :]
                    return carry

                for r in range(d):
                    unit(r, 0)

        dk_ref[...] = dkbuf[0:RB, :]
        dv_ref[...] = dvbuf[0:RB, :]

    blk = lambda f: pl.BlockSpec((None, RB, HEAD), f)
    cur_g = blk(lambda h, b: (HPG * g + h, jnp.minimum(b, nbk - 1), 0))
    cur = blk(lambda h, b: (h, jnp.minimum(b, nbk - 1), 0))
    prv = blk(lambda h, b: (h, jnp.maximum(b - 1, 0), 0))
    sq = pl.BlockSpec((None, NQ, 2 * NQ), lambda h, b: (h, 0, 0))
    return pl.pallas_call(
        body, name=f"attn_bwd_g{g}", grid=(HPG, nbk + 1),
        in_specs=[cur_g, cur_g, cur_g, sq, cur, cur, cur, cur],
        out_specs=[cur, prv, prv, sq],
        out_shape=[SDS((HPG, S, HEAD), F32)] * 3 + [SDS((HPG, NQ, 2 * NQ), F32)],
        scratch_shapes=[pltpu.VMEM((2 * RB, HEAD), F32)] * 4)(qh, kh, vh, biasm, da, wg, dh, lse)


def _merge_weights(l0, l1, l2):
    m = jnp.maximum(jnp.maximum(l0, l1), l2)
    e0, e1, e2 = jnp.exp(l0 - m), jnp.exp(l1 - m), jnp.exp(l2 - m)
    inv = 1.0 / (e0 + e1 + e2)
    return e0 * inv, e1 * inv, e2 * inv


def _merge_fwd(os_, lses):
    S = os_[0].shape[1]

    def body(o0, o1, o2, l0, l1, l2, a_ref):
        for h in range(HPG):
            w0, w1, w2 = _merge_weights(l0[h], l1[h], l2[h])
            a_ref[:, h * HEAD:(h + 1) * HEAD] = (w0 * o0[h] + w1 * o1[h] + w2 * o2[h]).astype(BF16)

    hm = pl.BlockSpec((HPG, TM, HEAD), lambda i: (0, i, 0))
    return pl.pallas_call(
        body, name="merge_fwd", grid=(S // TM,), in_specs=[hm] * 6,
        out_specs=pl.BlockSpec((TM, CH), lambda i: (i, 0)),
        out_shape=SDS((S, CH), BF16))(*os_, *lses)


def _merge_bwd(dattn, os_, lses):
    S = dattn.shape[0]

    def body(da_ref, o0, o1, o2, l0, l1, l2, w0_ref, w1_ref, w2_ref, dah_ref, dh_ref):
        for h in range(HPG):
            w = _merge_weights(l0[h], l1[h], l2[h])
            attn = w[0] * o0[h] + w[1] * o1[h] + w[2] * o2[h]
            da = da_ref[:, h * HEAD:(h + 1) * HEAD]
            for w_ref, wv in zip((w0_ref, w1_ref, w2_ref), w):
                w_ref[h] = wv
            dah_ref[h] = da
            dh_ref[h] = jnp.broadcast_to(jnp.sum(da * attn, axis=-1, keepdims=True), (TM, HEAD))

    hm = pl.BlockSpec((HPG, TM, HEAD), lambda i: (0, i, 0))
    res = pl.pallas_call(
        body, name="merge_bwd", grid=(S // TM,),
        in_specs=[pl.BlockSpec((TM, CH), lambda i: (i, 0))] + [hm] * 6,
        out_specs=[hm] * 5, out_shape=[SDS((HPG, S, HEAD), F32)] * 5)(dattn, *os_, *lses)
    return res[0:3], res[3], res[4]


def _qkv_bwd(dz, z, dqs, dks, dvs, qw, kw):
    S = z.shape[0]
    nh = N_GROUPS * HPG

    def body(dz_in, zq, zk, *refs):
        del dz_in
        dq_refs, dk_refs, dv_refs = refs[0:3], refs[3:6], refs[6:9]
        qw_ref, kw_ref, dz_ref, dqw_ref, dkw_ref = refs[9:]

        @pl.when(pl.program_id(0) == 0)
        def _():
            dqw_ref[...] = jnp.zeros_like(dqw_ref)
            dkw_ref[...] = jnp.zeros_like(dkw_ref)

        def nbwd(xr, dy, wr, dwr, h, off):
            g = h // HPG
            x = xr[:, h * HEAD:(h + 1) * HEAD].astype(F32)
            r = lax.rsqrt(jnp.mean(x * x, axis=-1, keepdims=True) + RMS_EPS)
            xhat = x * r
            dyw = dy * wr[g:g + 1, :]
            dz_ref[:, off + h * HEAD:off + (h + 1) * HEAD] = (
                r * (dyw - xhat * jnp.mean(dyw * xhat, axis=-1, keepdims=True))).astype(BF16)
            dwr[g:g + 1, :] += jnp.sum(dy * xhat, axis=0, keepdims=True)

        for h in range(nh):
            g, hh = h // HPG, h % HPG
            nbwd(zq, dq_refs[g][hh], qw_ref, dqw_ref, h, O_Q)
            nbwd(zk, dk_refs[g][hh], kw_ref, dkw_ref, h, O_K)
            dz_ref[:, O_V + h * HEAD:O_V + (h + 1) * HEAD] = dv_refs[g][hh].astype(BF16)

    hm = pl.BlockSpec((HPG, TM, HEAD), lambda i: (0, i, 0))
    return pl.pallas_call(
        body, name="qkv_bwd", grid=(S // TM,),
        in_specs=[pl.BlockSpec(memory_space=pl.ANY), pl.BlockSpec((TM, QKV_W), lambda i: (i, 0)),
                  pl.BlockSpec((TM, QKV_W), lambda i: (i, 1))] + [hm] * 9 + [_full((N_GROUPS, HEAD)), _full((N_GROUPS, HEAD))],
        out_specs=[pl.BlockSpec((TM, 3 * QKV_W), lambda i: (i, 0)), _full((N_GROUPS, HEAD)), _full((N_GROUPS, HEAD))],
        out_shape=[SDS(dz.shape, BF16), SDS((N_GROUPS, HEAD), F32), SDS((N_GROUPS, HEAD), F32)],
        input_output_aliases={0: 0})(dz, z, z, *dqs, *dks, *dvs, qw, kw)


def _conv_fwd(z, cw, cb, lnw, lnb):
    S = z.shape[0]
    H = 32

    def body(zv, zg, cw_ref, cb_ref, lnw_ref, lnb_ref, u1_ref, u3_ref, xbuf):
        i = pl.program_id(0)

        @pl.when(i == 0)
        def _():
            xbuf[0:H, :] = jnp.zeros((H, CH), F32)

        @pl.when(i > 0)
        def _():
            xbuf[0:H, :] = xbuf[TM:TM + H, :]

        xbuf[H:H + TM, :] = zv[...].astype(F32) * _sig(zg[...].astype(F32))
        for r0 in range(0, TM, CONV_RC):
            rows = pl.ds(r0, CONV_RC)
            acc = jnp.broadcast_to(cb_ref[...], (CONV_RC, CH))
            for k in range(CONV_K):
                acc = acc + xbuf[pl.ds(H - (CONV_K - 1) + k + r0, CONV_RC), :] * cw_ref[k:k + 1, :]
            u1_ref[rows, :] = acc
            mu = jnp.mean(acc, axis=-1, keepdims=True)
            xc = acc - mu
            yl = xc * lax.rsqrt(jnp.mean(xc * xc, axis=-1, keepdims=True) + LN_EPS) * lnw_ref[...] + lnb_ref[...]
            u3_ref[rows, :] = (yl * _sig(yl)).astype(BF16)

    row = pl.BlockSpec((TM, CH), lambda i: (i, 0))
    return pl.pallas_call(
        body, name="conv_fwd", grid=(S // TM,),
        in_specs=[pl.BlockSpec((TM, CH), lambda i: (i, O_CV // CH)), pl.BlockSpec((TM, CH), lambda i: (i, O_CG // CH)),
                  _full((CONV_K, CH)), _full((1, CH)), _full((1, CH)), _full((1, CH))],
        out_specs=[row, row], out_shape=[SDS((S, CH), F32), SDS((S, CH), BF16)],
        scratch_shapes=[pltpu.VMEM((TM + H, CH), F32)])(z, z, cw, cb, lnw, lnb)


def _conv_bwd_a(du3, u1, z, lnw, lnb):
    S = z.shape[0]
    H = 32

    nt = S // TM

    def body(du3_ref, u1_ref, zv, zg, lnw_ref, lnb_ref, du1_ref, acc_ref, xbuf, tacc):
        i = pl.program_id(0)

        @pl.when(i == 0)
        def _():
            xbuf[0:H, :] = jnp.zeros((H, CH), F32)
            tacc[...] = jnp.zeros_like(tacc)

        @pl.when(i > 0)
        def _():
            xbuf[0:H, :] = xbuf[TM:TM + H, :]

        xbuf[H:H + TM, :] = zv[...].astype(F32) * _sig(zg[...].astype(F32))
        for r0 in range(0, TM, CONV_RC):
            rows = pl.ds(r0, CONV_RC)
            u1 = u1_ref[rows, :]
            mu = jnp.mean(u1, axis=-1, keepdims=True)
            xc = u1 - mu
            r = lax.rsqrt(jnp.mean(xc * xc, axis=-1, keepdims=True) + LN_EPS)
            yhat = xc * r
            yl = yhat * lnw_ref[...] + lnb_ref[...]
            sg = _sig(yl)
            dyl = du3_ref[rows, :] * (sg * (1.0 + yl * (1.0 - sg)))
            dyh = dyl * lnw_ref[...]
            du1 = r * (dyh - jnp.mean(dyh, axis=-1, keepdims=True) - yhat * jnp.mean(dyh * yhat, axis=-1, keepdims=True))
            du1_ref[rows, :] = du1
            tacc[33] += _fold8(dyl * yhat)
            tacc[34] += _fold8(dyl)
            tacc[32] += _fold8(du1)
            for k in range(CONV_K):
                tacc[k] += _fold8(du1 * xbuf[pl.ds(H - (CONV_K - 1) + k + r0, CONV_RC), :])

        @pl.when(i == nt - 1)
        def _():
            for k in range(40):
                acc_ref[k:k + 1, :] = jnp.sum(tacc[k], axis=0, keepdims=True)

    row = pl.BlockSpec((TM, CH), lambda i: (i, 0))
    return pl.pallas_call(
        body, name="conv_bwd_a", grid=(nt,),
        in_specs=[row, row, pl.BlockSpec((TM, CH), lambda i: (i, O_CV // CH)), pl.BlockSpec((TM, CH), lambda i: (i, O_CG // CH)),
                  _full((1, CH)), _full((1, CH))],
        out_specs=[row, _full((40, CH))], out_shape=[SDS((S, CH), F32), SDS((40, CH), F32)],
        scratch_shapes=[pltpu.VMEM((TM + H, CH), F32), pltpu.VMEM((40, 8, CH), F32)])(du3, u1, z, z, lnw, lnb)


def _conv_bwd_b(dz, du1, z, cw):
    S = z.shape[0]
    nt = S // TM
    H = 32

    def body(dz_in, du1_ref, zv, zg, cw_ref, dz_ref, ybuf, dgate):
        del dz_in
        i, p = pl.program_id(0), pl.program_id(1)

        @pl.when(p == 0)
        def _():
            @pl.when(i == 0)
            def _():
                ybuf[TM:TM + H, :] = jnp.zeros((H, CH), F32)

            @pl.when(i > 0)
            def _():
                ybuf[TM:TM + H, :] = ybuf[0:H, :]

            ybuf[0:TM, :] = du1_ref[...]
            for r0 in range(0, TM, CONV_RC):
                rows = pl.ds(r0, CONV_RC)
                acc = ybuf[rows, :] * cw_ref[CONV_K - 1:CONV_K, :]
                for j in range(1, CONV_K):
                    acc = acc + ybuf[pl.ds(r0 + j, CONV_RC), :] * cw_ref[CONV_K - 1 - j:CONV_K - j, :]
                val = zv[rows, :].astype(F32)
                sg = _sig(zg[rows, :].astype(F32))
                dz_ref[rows, :] = (acc * sg).astype(BF16)
                dgate[rows, :] = (acc * val * sg * (1.0 - sg)).astype(BF16)

        @pl.when(p == 1)
        def _():
            dz_ref[...] = dgate[...]

    rev = lambda c: pl.BlockSpec((TM, CH), lambda i, p: (nt - 1 - i, c))
    return pl.pallas_call(
        body, name="conv_bwd_b", grid=(nt, 2),
        in_specs=[pl.BlockSpec(memory_space=pl.ANY), rev(0), rev(O_CV // CH), rev(O_CG // CH), _full((CONV_K, CH))],
        out_specs=pl.BlockSpec((TM, CH), lambda i, p: (nt - 1 - i, O_CV // CH + p)),
        out_shape=SDS(dz.shape, BF16),
        scratch_shapes=[pltpu.VMEM((TM + H, CH), F32), pltpu.VMEM((TM, CH), BF16)],
        input_output_aliases={0: 0})(dz, du1, z, z, cw)


def _memkv_fwd(mem, mnw, wkv, xkw):
    M, D = mem.shape

    def body(mem_ref, mnw_ref, w_ref, xkw_ref, mn_ref, kv_ref, mk_ref, mv_ref):
        x = mem_ref[...]
        mn = (x * lax.rsqrt(jnp.mean(x * x, axis=-1, keepdims=True) + RMS_EPS) * mnw_ref[...]).astype(BF16)
        mn_ref[...] = mn
        kv = jnp.dot(mn, w_ref[...], preferred_element_type=F32)
        kv_ref[...] = kv
        for h in range(HPG):
            k = kv[:, h * HEAD:(h + 1) * HEAD]
            mk_ref[:, h * HEAD:(h + 1) * HEAD] = (
                k * lax.rsqrt(jnp.mean(k * k, axis=-1, keepdims=True) + RMS_EPS) * xkw_ref[...]).astype(BF16)
        mv_ref[...] = kv[:, CH:2 * CH].astype(BF16)

    return pl.pallas_call(
        body, name="memkv_fwd",
        out_shape=[SDS((M, D), BF16), SDS((M, 2 * CH), F32), SDS((M, CH), BF16), SDS((M, CH), BF16)])(mem, mnw, wkv, xkw)


def _cross_q(zx, xqw, h):
    x = zx[:, h * HEAD:(h + 1) * HEAD].astype(F32)
    r = lax.rsqrt(jnp.mean(x * x, axis=-1, keepdims=True) + RMS_EPS)
    xhat = x * r
    return xhat, r, xhat * xqw


def _cross_fwd(z, xqw, mk, mv):
    S = z.shape[0]
    M = mk.shape[0]

    def body(zx, xqw_ref, mk_ref, mv_ref, o_ref):
        for h in range(HPG):
            sl = slice(h * HEAD, (h + 1) * HEAD)
            _, _, q = _cross_q(zx, xqw_ref[...], h)
            s = lax.dot_general(q.astype(BF16), mk_ref[:, sl], NT_DIMS, preferred_element_type=F32) * SCALE
            e = jnp.exp(s - jnp.max(s, axis=-1, keepdims=True))
            p = e / jnp.sum(e, axis=-1, keepdims=True)
            o_ref[:, sl] = jnp.dot(p.astype(BF16), mv_ref[:, sl], preferred_element_type=F32).astype(BF16)

    return pl.pallas_call(
        body, name="cross_fwd", grid=(S // TM,),
        in_specs=[pl.BlockSpec((TM, CH), lambda i: (i, O_XQ // CH)), _full((1, HEAD)), _full((M, CH)), _full((M, CH))],
        out_specs=pl.BlockSpec((TM, CH), lambda i: (i, 0)), out_shape=SDS((S, CH), BF16))(z, xqw, mk, mv)


def _cross_bwd(dz, doc, z, xqw, mk, mv):
    S = z.shape[0]
    M = mk.shape[0]

    def body(dz_in, do_ref, zx, xqw_ref, mk_ref, mv_ref, dz_ref, dmk_ref, dmv_ref, dxw_ref):
        del dz_in

        @pl.when(pl.program_id(0) == 0)
        def _():
            dmk_ref[...] = jnp.zeros_like(dmk_ref)
            dmv_ref[...] = jnp.zeros_like(dmv_ref)
            dxw_ref[...] = jnp.zeros_like(dxw_ref)

        for h in range(HPG):
            sl = slice(h * HEAD, (h + 1) * HEAD)
            xhat, r, q = _cross_q(zx, xqw_ref[...], h)
            qb = q.astype(BF16)
            s = lax.dot_general(qb, mk_ref[:, sl], NT_DIMS, preferred_element_type=F32) * SCALE
            e = jnp.exp(s - jnp.max(s, axis=-1, keepdims=True))
            p = e / jnp.sum(e, axis=-1, keepdims=True)
            do = do_ref[:, sl].astype(BF16)
            dp = lax.dot_general(do, mv_ref[:, sl], NT_DIMS, preferred_element_type=F32)
            ds = (p * (dp - jnp.sum(p * dp, axis=-1, keepdims=True)) * SCALE).astype(BF16)
            dq = jnp.dot(ds, mk_ref[:, sl], preferred_element_type=F32)
            dmk_ref[:, sl] += lax.dot_general(ds, qb, TN_DIMS, preferred_element_type=F32)
            dmv_ref[:, sl] += lax.dot_general(p.astype(BF16), do, TN_DIMS, preferred_element_type=F32)
            dyw = dq * xqw_ref[...]
            dz_ref[:, sl] = (r * (dyw - xhat * jnp.mean(dyw * xhat, axis=-1, keepdims=True))).astype(BF16)
            dxw_ref[...] += jnp.sum(dq * xhat, axis=0, keepdims=True)

    return pl.pallas_call(
        body, name="cross_bwd", grid=(S // TM,),
        in_specs=[pl.BlockSpec(memory_space=pl.ANY), pl.BlockSpec((TM, CH), lambda i: (i, 0)),
                  pl.BlockSpec((TM, CH), lambda i: (i, O_XQ // CH)), _full((1, HEAD)), _full((M, CH)), _full((M, CH))],
        out_specs=[pl.BlockSpec((TM, CH), lambda i: (i, O_XQ // CH)), _full((M, CH)), _full((M, CH)), _full((1, HEAD))],
        out_shape=[SDS(dz.shape, BF16), SDS((M, CH), F32), SDS((M, CH), F32), SDS((1, HEAD), F32)],
        input_output_aliases={0: 0})(dz, doc, z, xqw, mk, mv)


def _memkv_bwd(dmk, dmv, kv, mem, mn, mnw, wkv, xkw):
    M, D = mem.shape

    def body(dmk_ref, dmv_ref, kv_ref, mem_ref, mn_ref, mnw_ref, w_ref, xkw_ref, dw_ref, dxk_ref, dmn_ref, dkv):
        dxk = jnp.zeros((1, HEAD), F32)
        for h in range(HPG):
            sl = slice(h * HEAD, (h + 1) * HEAD)
            k = kv_ref[:, sl]
            r = lax.rsqrt(jnp.mean(k * k, axis=-1, keepdims=True) + RMS_EPS)
            khat = k * r
            dy = dmk_ref[:, sl]
            dyw = dy * xkw_ref[...]
            dkv[:, sl] = (r * (dyw - khat * jnp.mean(dyw * khat, axis=-1, keepdims=True))).astype(BF16)
            dxk = dxk + jnp.sum(dy * khat, axis=0, keepdims=True)
        dxk_ref[...] = dxk
        dkv[:, CH:2 * CH] = dmv_ref[...].astype(BF16)
        dw_ref[...] = lax.dot_general(mn_ref[...], dkv[...], TN_DIMS, preferred_element_type=F32)
        dn = lax.dot_general(dkv[...], w_ref[...], NT_DIMS, preferred_element_type=F32)
        x = mem_ref[...]
        r = lax.rsqrt(jnp.mean(x * x, axis=-1, keepdims=True) + RMS_EPS)
        dmn_ref[...] = jnp.sum(dn * x * r, axis=0, keepdims=True)

    return pl.pallas_call(
        body, name="memkv_bwd",
        out_shape=[SDS((D, 2 * CH), F32), SDS((1, HEAD), F32), SDS((1, D), F32)],
        scratch_shapes=[pltpu.VMEM((M, 2 * CH), BF16)])(dmk, dmv, kv, mem, mn, mnw, wkv, xkw)


def _branch_proj(a_ref, w_ref, y_ref):
    G, _, n = w_ref.shape
    a = a_ref[...]
    for g in range(G):
        y_ref[:, g * n:(g + 1) * n] = jnp.dot(a, w_ref[g], preferred_element_type=F32)


def _gates(zg_ref, bg_ref, k, D):
    return _sig(zg_ref[:, k * D:(k + 1) * D].astype(F32) + bg_ref[:, k * D:(k + 1) * D])


def _outproj_fwd(x, z, bg, attn, u3, oc, wao, wco, wxo, wout, fnw):
    S, D = x.shape
    tm = 256

    def body(x_ref, zg_ref, bg_ref, a_ref, u_ref, c_ref, wa, wc, wx, wo, fnw_ref, h1_ref, hn_ref, ya, yc, yx):
        _branch_proj(a_ref, wa, ya)
        _branch_proj(u_ref, wc, yc)
        _branch_proj(c_ref, wx, yx)
        merged = _gates(zg_ref, bg_ref, 0, D) * ya[...] + _gates(zg_ref, bg_ref, 1, D) * yc[...] + _gates(zg_ref, bg_ref, 2, D) * yx[...]
        h1 = x_ref[...] + jnp.dot(merged.astype(BF16), wo[...], preferred_element_type=F32)
        h1_ref[...] = h1
        hn_ref[...] = (h1 * lax.rsqrt(jnp.mean(h1 * h1, axis=-1, keepdims=True) + RMS_EPS) * fnw_ref[...]).astype(BF16)

    row = lambda w: pl.BlockSpec((tm, w), lambda i: (i, 0))
    return pl.pallas_call(
        body, name="outproj_fwd", grid=(S // tm,),
        in_specs=[row(D), pl.BlockSpec((tm, 3 * D), lambda i: (i, O_G // (3 * D))), _full((1, 3 * D)), row(CH), row(CH), row(CH),
                  _full(wao.shape), _full(wco.shape), _full(wxo.shape), _full((D, D)), _full((1, D))],
        out_specs=[row(D), row(D)], out_shape=[SDS((S, D), F32), SDS((S, D), BF16)],
        scratch_shapes=[pltpu.VMEM((tm, D), F32)] * 3)(x, z, bg, attn, u3, oc, wao, wco, wxo, wout, fnw)


def _outproj_bwd(dh1, z, bg, attn, u3, oc, wao, wco, wxo, wout, n_in):
    S, D = dh1.shape
    tm = 256

    def body(dh_ref, zg_ref, bg_ref, a_ref, u_ref, c_ref, wa, wc, wx, wo,
             dz_ref, da_ref, du_ref, dc_ref, dya_ref, dyc_ref, dyx_ref, mg_ref, dbg_ref, ya, yc, yx):
        @pl.when(pl.program_id(0) == 0)
        def _():
            dbg_ref[...] = jnp.zeros_like(dbg_ref)

        _branch_proj(a_ref, wa, ya)
        _branch_proj(u_ref, wc, yc)
        _branch_proj(c_ref, wx, yx)
        dm = lax.dot_general(dh_ref[...].astype(BF16), wo[...], NT_DIMS, preferred_element_type=F32)
        merged = jnp.zeros((tm, D), F32)
        for k, (y, dy_ref, w_ref, db_ref) in enumerate(((ya, dya_ref, wa, da_ref), (yc, dyc_ref, wc, du_ref), (yx, dyx_ref, wx, dc_ref))):
            gk = _gates(zg_ref, bg_ref, k, D)
            yk = y[...]
            merged = merged + gk * yk
            dzg = dm * yk * gk * (1.0 - gk)
            dz_ref[:, k * D:(k + 1) * D] = dzg.astype(BF16)
            dbg_ref[:, k * D:(k + 1) * D] += jnp.sum(dzg, axis=0, keepdims=True)
            dyk = (dm * gk).astype(BF16)
            dy_ref[...] = dyk
            G, _, n = w_ref.shape
            acc = jnp.zeros((tm, CH), F32)
            for g in range(G):
                acc = acc + lax.dot_general(dyk[:, g * n:(g + 1) * n], w_ref[g], NT_DIMS, preferred_element_type=F32)
            db_ref[...] = acc
        mg_ref[...] = merged.astype(BF16)

    row = lambda w: pl.BlockSpec((tm, w), lambda i: (i, 0))
    return pl.pallas_call(
        body, name="outproj_bwd", grid=(S // tm,),
        in_specs=[row(D), pl.BlockSpec((tm, 3 * D), lambda i: (i, O_G // (3 * D))), _full((1, 3 * D)), row(CH), row(CH), row(CH),
                  _full(wao.shape), _full(wco.shape), _full(wxo.shape), _full((D, D))],
        out_specs=[pl.BlockSpec((tm, 3 * D), lambda i: (i, O_G // (3 * D))), row(CH), row(CH), row(CH),
                   row(D), row(D), row(D), row(D), _full((1, 3 * D))],
        out_shape=[SDS((S, n_in), BF16)] + [SDS((S, CH), F32)] * 3 + [SDS((S, D), BF16)] * 4 + [SDS((1, 3 * D), F32)],
        scratch_shapes=[pltpu.VMEM((tm, D), F32)] * 3)(dh1, z, bg, attn, u3, oc, wao, wco, wxo, wout)


FFN_TC = 256
FFN_H = 8


def _ffn_taps(buf, r0):
    xx = buf[pl.ds(r0, FFN_RC + FFN_H), :]
    return xx[FFN_H:], pltpu.roll(xx, 1, 0)[FFN_H:], pltpu.roll(xx, 2, 0)[FFN_H:]


def _ffn_conv(taps, w_ref, b_ref):
    x0, x1, x2 = taps
    return b_ref[...] + x0 * w_ref[2:3, :] + x1 * w_ref[1:2, :] + x2 * w_ref[0:1, :]


def _ffn_act_fwd(up, cw, cb):
    S, F2 = up.shape
    nj = F2 // 2 // FFN_TC
    tm = 2048 if S % 2048 == 0 else TM

    def body(ua, ug, wa, wg, ba, bgt, o_ref, abuf, gbuf):
        i = pl.program_id(1)

        @pl.when(i == 0)
        def _():
            abuf[0:FFN_H, :] = jnp.zeros((FFN_H, FFN_TC), F32)
            gbuf[0:FFN_H, :] = jnp.zeros((FFN_H, FFN_TC), F32)

        @pl.when(i > 0)
        def _():
            abuf[0:FFN_H, :] = abuf[tm:tm + FFN_H, :]
            gbuf[0:FFN_H, :] = gbuf[tm:tm + FFN_H, :]

        abuf[FFN_H:FFN_H + tm, :] = ua[...].astype(F32)
        gbuf[FFN_H:FFN_H + tm, :] = ug[...].astype(F32)
        for r0 in range(0, tm, FFN_RC):
            a = _ffn_conv(_ffn_taps(abuf, r0), wa, ba)
            gt = _ffn_conv(_ffn_taps(gbuf, r0), wg, bgt)
            o_ref[pl.ds(r0, FFN_RC), :] = (gt * _sig(gt) * a).astype(BF16)

    return pl.pallas_call(
        body, name="ffn_act_fwd", grid=(nj, S // tm),
        in_specs=[pl.BlockSpec((tm, FFN_TC), lambda j, i: (i, j)), pl.BlockSpec((tm, FFN_TC), lambda j, i: (i, nj + j)),
                  pl.BlockSpec((FFN_K, FFN_TC), lambda j, i: (0, j)), pl.BlockSpec((FFN_K, FFN_TC), lambda j, i: (0, nj + j)),
                  pl.BlockSpec((1, FFN_TC), lambda j, i: (0, j)), pl.BlockSpec((1, FFN_TC), lambda j, i: (0, nj + j))],
        out_specs=pl.BlockSpec((tm, FFN_TC), lambda j, i: (i, j)), out_shape=SDS((S, F2 // 2), BF16),
        scratch_shapes=[pltpu.VMEM((tm + FFN_H, FFN_TC), F32)] * 2)(up, up, cw, cw, cb, cb)


def _ffn_down_loss(act, wdown, h1, target):
    S, D = h1.shape
    F = act.shape[1]

    def body(a_ref, w_ref, h_ref, t_ref, dy_ref, loss_ref):
        @pl.when(pl.program_id(0) == 0)
        def _():
            loss_ref[...] = jnp.zeros_like(loss_ref)

        err = h_ref[...] + jnp.dot(a_ref[...], w_ref[...], preferred_element_type=F32) - t_ref[...]
        dy_ref[...] = err * (1.0 / D)
        loss_ref[...] += 0.5 * jnp.sum(jnp.mean(err * err, axis=-1, keepdims=True))

    row = lambda w: pl.BlockSpec((TM, w), lambda i: (i, 0))
    return pl.pallas_call(
        body, name="ffn_down_loss", grid=(S // TM,),
        in_specs=[row(F), _full((F, D)), row(D), row(D)],
        out_specs=[row(D), _full((8, 128))], out_shape=[SDS((S, D), F32), SDS((8, 128), F32)])(act, wdown, h1, target)


def _ffn_bwd_a(dy, wdown, up, cw, cb):
    S, D = dy.shape
    F2 = up.shape[1]
    F = F2 // 2
    nj = F // FFN_TC
    tm = 1024 if S % 1024 == 0 else TM

    def body(dy_ref, wd_ref, ua, ug, wa, wg, ba, bgt, da_ref, dg_ref, acca_ref, accg_ref, dwd_ref,
             abuf, gbuf, hala, halg, dact_s, act_s):
        i, j = pl.program_id(0), pl.program_id(1)

        @pl.when((i == 0) & (j == 0))
        def _():
            acca_ref[...] = jnp.zeros_like(acca_ref)
            accg_ref[...] = jnp.zeros_like(accg_ref)
            dwd_ref[...] = jnp.zeros_like(dwd_ref)

        @pl.when(i == 0)
        def _():
            abuf[0:FFN_H, :] = jnp.zeros((FFN_H, FFN_TC), F32)
            gbuf[0:FFN_H, :] = jnp.zeros((FFN_H, FFN_TC), F32)

        @pl.when(i > 0)
        def _():
            abuf[0:FFN_H, :] = hala[j]
            gbuf[0:FFN_H, :] = halg[j]

        abuf[FFN_H:FFN_H + tm, :] = ua[...].astype(F32)
        gbuf[FFN_H:FFN_H + tm, :] = ug[...].astype(F32)
        hala[j] = abuf[tm:tm + FFN_H, :]
        halg[j] = gbuf[tm:tm + FFN_H, :]
        dyb = dy_ref[...].astype(BF16)
        dact_s[...] = lax.dot_general(dyb, wd_ref[...], NT_DIMS, preferred_element_type=F32)
        zero8 = jnp.zeros((8, FFN_TC), F32)
        pa, pg = [zero8] * (FFN_K + 1), [zero8] * (FFN_K + 1)
        for r0 in range(0, tm, FFN_RC):
            rows = pl.ds(r0, FFN_RC)
            ta, tg = _ffn_taps(abuf, r0), _ffn_taps(gbuf, r0)
            a = _ffn_conv(ta, wa, ba)
            gt = _ffn_conv(tg, wg, bgt)
            dact = dact_s[rows, :]
            sg = _sig(gt)
            silu = gt * sg
            act_s[rows, :] = (silu * a).astype(BF16)
            dac = dact * silu
            dgc = dact * a * (sg * (1.0 + gt * (1.0 - sg)))
            da_ref[rows, :] = dac.astype(BF16)
            dg_ref[rows, :] = dgc.astype(BF16)
            for k in range(FFN_K):
                pa[k] = pa[k] + _fold8(dac * ta[FFN_K - 1 - k])
                pg[k] = pg[k] + _fold8(dgc * tg[FFN_K - 1 - k])
            pa[FFN_K] = pa[FFN_K] + _fold8(dac)
            pg[FFN_K] = pg[FFN_K] + _fold8(dgc)
        for k in range(FFN_K + 1):
            acca_ref[j, k:k + 1, :] += jnp.sum(pa[k], axis=0, keepdims=True)
            accg_ref[j, k:k + 1, :] += jnp.sum(pg[k], axis=0, keepdims=True)
        dwd_ref[pl.ds(pl.multiple_of(j * FFN_TC, FFN_TC), FFN_TC), :] += lax.dot_general(
            act_s[...], dyb, TN_DIMS, preferred_element_type=F32)

    return pl.pallas_call(
        body, name="ffn_bwd_a", grid=(S // tm, nj),
        in_specs=[pl.BlockSpec((tm, D), lambda i, j: (i, 0)), pl.BlockSpec((FFN_TC, D), lambda i, j: (j, 0)),
                  pl.BlockSpec((tm, FFN_TC), lambda i, j: (i, j)), pl.BlockSpec((tm, FFN_TC), lambda i, j: (i, nj + j)),
                  pl.BlockSpec((FFN_K, FFN_TC), lambda i, j: (0, j)), pl.BlockSpec((FFN_K, FFN_TC), lambda i, j: (0, nj + j)),
                  pl.BlockSpec((1, FFN_TC), lambda i, j: (0, j)), pl.BlockSpec((1, FFN_TC), lambda i, j: (0, nj + j))],
        out_specs=[pl.BlockSpec((tm, FFN_TC), lambda i, j: (i, j))] * 2 + [_full((nj, 8, FFN_TC))] * 2 + [_full((F, D))],
        out_shape=[SDS((S, F), BF16)] * 2 + [SDS((nj, 8, FFN_TC), F32)] * 2 + [SDS((F, D), F32)],
        scratch_shapes=[pltpu.VMEM((tm + FFN_H, FFN_TC), F32)] * 2 + [pltpu.VMEM((nj, FFN_H, FFN_TC), F32)] * 2
        + [pltpu.VMEM((tm, FFN_TC), F32), pltpu.VMEM((tm, FFN_TC), BF16)],
    )(dy, wdown, up, up, cw, cw, cb, cb)


def _ffn_bwd_b(dca, dcg, cw):
    S, F = dca.shape
    nj = F // FFN_TC
    tm = 2048 if S % 2048 == 0 else TM
    nt = S // tm
    span = FFN_RC + FFN_H

    def body(a_ref, g_ref, w_ref, o_ref, ybuf):
        j, i = pl.program_id(0), pl.program_id(1)

        @pl.when(i == 0)
        def _():
            ybuf[tm:tm + FFN_H, :] = jnp.zeros((FFN_H, FFN_TC), F32)

        @pl.when(i > 0)
        def _():
            ybuf[tm:tm + FFN_H, :] = ybuf[0:FFN_H, :]

        ybuf[0:tm, :] = jnp.where(j < nj, a_ref[...], g_ref[...]).astype(F32)
        for r0 in range(0, tm, FFN_RC):
            yy = ybuf[pl.ds(r0, span), :]
            acc = yy[:FFN_RC] * w_ref[2:3, :] + pltpu.roll(yy, span - 1, 0)[:FFN_RC] * w_ref[1:2, :] \
                + pltpu.roll(yy, span - 2, 0)[:FFN_RC] * w_ref[0:1, :]
            o_ref[pl.ds(r0, FFN_RC), :] = acc.astype(BF16)

    return pl.pallas_call(
        body, name="ffn_bwd_b", grid=(2 * nj, nt),
        in_specs=[pl.BlockSpec((tm, FFN_TC), lambda j, i: (nt - 1 - i, jnp.minimum(j, nj - 1))),
                  pl.BlockSpec((tm, FFN_TC), lambda j, i: (nt - 1 - i, jnp.maximum(j - nj, 0))),
                  pl.BlockSpec((FFN_K, FFN_TC), lambda j, i: (0, j))],
        out_specs=pl.BlockSpec((tm, FFN_TC), lambda j, i: (nt - 1 - i, j)), out_shape=SDS((S, 2 * F), BF16),
        scratch_shapes=[pltpu.VMEM((tm + FFN_H, FFN_TC), F32)])(dca, dcg, cw)


def _adamw_update(w_ref, g_ref, m_ref, v_ref, d_ref, nm_ref, nv_ref):
    gv = g_ref[...]
    m2 = ADAM_B1 * m_ref[...] + (1.0 - ADAM_B1) * gv
    v2 = ADAM_B2 * v_ref[...] + (1.0 - ADAM_B2) * jnp.square(gv)
    m_hat = m2 / (1.0 - ADAM_B1 ** ADAM_STEP)
    v_hat = v2 / (1.0 - ADAM_B2 ** ADAM_STEP)
    d_ref[...] = -ADAM_LR * (m_hat / (jnp.sqrt(v_hat) + ADAM_EPS) + ADAM_WD * w_ref[...])
    nm_ref[...] = m2
    nv_ref[...] = v2


def _adamw_small(ws, gs, ms, vs):
    n = len(ws)

    def body(*refs):
        for i in range(n):
            _adamw_update(*[refs[k * n + i] for k in range(7)])

    shapes = [SDS(w.shape, F32) for w in ws]
    res = pl.pallas_call(body, name="adamw_small", out_shape=shapes * 3)(*ws, *gs, *ms, *vs)
    return res[:n], res[n:2 * n], res[2 * n:]


def _adamw(w, g, m, v, name):
    R, C = w.shape
    tr = _row_tile(R, max(8, (2 ** 19) // (4 * C) // 8 * 8))

    def body(w_ref, g_ref, m_ref, v_ref, d_ref, nm_ref, nv_ref):
        _adamw_update(w_ref, g_ref, m_ref, v_ref, d_ref, nm_ref, nv_ref)

    blk = pl.BlockSpec((tr, C), lambda i: (i, 0))
    return pl.pallas_call(
        body, name=name, grid=(R // tr,), in_specs=[blk] * 4, out_specs=[blk] * 3,
        out_shape=[SDS((R, C), F32)] * 3)(w, g, m, v)


HBM_SPEC = pl.BlockSpec(memory_space=pltpu.HBM)
SEM_SPEC = pl.BlockSpec(memory_space=pltpu.SEMAPHORE)
DATAFLOW_EFFECT = pltpu.SideEffectType.DATAFLOW_SIDE_EFFECTING


def _position():
    return lax.axis_index("x"), lax.axis_index("y"), lax.axis_index("c")


def _other_chips(x, y):
    return [(1 - x, y), (x, 1 - y), (1 - x, 1 - y)]


def _all_gather_xy(arrs, name):
    n = len(arrs)
    hbm = pl.BlockSpec(memory_space=pl.ANY)

    def body(*refs):
        ins, outs = refs[:n], refs[n:2 * n]
        send_sems, recv_sems = refs[2 * n:]
        x, y, c = _position()
        me = 2 * x + y
        chips = _other_chips(x, y)

        def rcopy(i, k, src, dst, to):
            return pltpu.make_async_remote_copy(src_ref=src, dst_ref=dst, send_sem=send_sems.at[i, k], recv_sem=recv_sems.at[i, k],
                                                device_id=to, device_id_type=MESH)

        sends = []
        for i in range(n):
            own = rcopy(i, 6, ins[i], outs[i].at[me], (x, y, 1 - c))
            own.start()
            sends.append(own)
        for i in range(n):
            for j, (px, py) in enumerate(chips):
                cp = rcopy(i, j, ins[i].at[c], outs[i].at[me, c], (px, py, c))
                cp.start()
                sends.append(cp)
        for i in range(n):
            for j, (px, py) in enumerate(chips):
                got = outs[i].at[2 * px + py, c]
                rcopy(i, j, ins[i].at[c], got, (x, y, c)).wait_recv()
                fwd = rcopy(i, 3 + j, got, got, (x, y, 1 - c))
                fwd.start()
                sends.append(fwd)
        for i in range(n):
            for j, (px, py) in enumerate(chips):
                theirs = outs[i].at[2 * px + py, 1 - c]
                rcopy(i, 3 + j, theirs, theirs, (x, y, c)).wait_recv()
        for i in range(n):
            rcopy(i, 6, ins[i], outs[i].at[me], (x, y, c)).wait_recv()
        for cp in sends:
            cp.wait_send()

    return pl.pallas_call(
        body, name=name, in_specs=[hbm] * n, out_specs=[hbm] * n,
        out_shape=[SDS((4,) + a.shape, a.dtype) for a in arrs],
        scratch_shapes=[pltpu.SemaphoreType.DMA((n, 7)), pltpu.SemaphoreType.DMA((n, 7))])(*arrs)


def _ag_ici_start(arrs, name):
    n = len(arrs)

    def body(*refs):
        ins, lands = refs[:n], refs[n:2 * n]
        send_sems, recv_sems = refs[2 * n:2 * n + 2]
        token = refs[-1]
        x, y, c = _position()
        for i in range(n):
            for j, (px, py) in enumerate(_other_chips(x, y)):
                pltpu.make_async_remote_copy(src_ref=ins[i].at[c], dst_ref=lands[i].at[2 * x + y, c], send_sem=send_sems.at[3 * i + j],
                                             recv_sem=recv_sems.at[3 * i + j], device_id=(px, py, c), device_id_type=MESH).start()
        token[...] = jnp.zeros_like(token)

    lands = [lax.empty((4,) + a.shape, a.dtype) for a in arrs]
    res = pl.pallas_call(
        body, name=name,
        out_shape=[pltpu.SemaphoreType.DMA((3 * n,)), pltpu.SemaphoreType.DMA((3 * n,))]
        + [pltpu.HBM(a.shape, a.dtype) for a in arrs] + [pltpu.HBM(l.shape, l.dtype) for l in lands] + [SDS((8, 128), F32)],
        in_specs=[HBM_SPEC] * (2 * n), out_specs=[SEM_SPEC, SEM_SPEC] + [HBM_SPEC] * (2 * n) + [pl.BlockSpec(memory_space=pltpu.VMEM)],
        input_output_aliases={i: 2 + i for i in range(2 * n)},
        compiler_params=pltpu.CompilerParams(has_side_effects=DATAFLOW_EFFECT),
    )(*[pltpu.with_memory_space_constraint(a, pltpu.HBM) for a in list(arrs) + lands])
    return res[0], res[1], list(res[2:2 + n]), list(res[2 + n:2 + 2 * n]), res[-1]


def _ag_ici_wait(send_sems, recv_sems, ins, lands, after, name):
    n = len(ins)

    def body(*refs):
        ins_r, lands_r = refs[:n], refs[n:2 * n]
        send_r, recv_r = refs[2 * n:2 * n + 2]
        x, y, c = _position()
        for i in range(n):
            for j, (px, py) in enumerate(_other_chips(x, y)):
                cp = pltpu.make_async_remote_copy(src_ref=ins_r[i].at[c], dst_ref=lands_r[i].at[2 * px + py, c], send_sem=send_r.at[3 * i + j],
                                                  recv_sem=recv_r.at[3 * i + j], device_id=(px, py, c), device_id_type=MESH)
                cp.wait_send()
                cp.wait_recv()

    res = pl.pallas_call(
        body, name=name,
        out_shape=[pltpu.HBM(a.shape, a.dtype) for a in list(ins) + list(lands)],
        in_specs=[HBM_SPEC] * (2 * n) + [SEM_SPEC, SEM_SPEC, pl.BlockSpec(memory_space=pl.ANY)], out_specs=[HBM_SPEC] * (2 * n),
        input_output_aliases={i: i for i in range(2 * n)},
        compiler_params=pltpu.CompilerParams(has_side_effects=DATAFLOW_EFFECT),
    )(*ins, *lands, send_sems, recv_sems, after)
    return list(res[:n]), list(res[n:])


def _ag_finish(arrs, lands, name):
    n = len(arrs)
    hbm = pl.BlockSpec(memory_space=pl.ANY)

    def body(*refs):
        ins, landed, outs = refs[:n], refs[n:2 * n], refs[2 * n:3 * n]
        send_sems, recv_sems = refs[3 * n:]
        x, y, c = _position()
        chips = _other_chips(x, y)
        sends = []
        for i in range(n):
            own = pltpu.make_async_remote_copy(src_ref=ins[i], dst_ref=outs[i].at[2 * x + y], send_sem=send_sems.at[i, 3],
                                               recv_sem=recv_sems.at[i, 3], device_id=(x, y, 1 - c), device_id_type=MESH)
            own.start()
            sends.append(own)
        for i in range(n):
            for j, (px, py) in enumerate(chips):
                fwd = pltpu.make_async_remote_copy(src_ref=landed[i].at[2 * px + py, c], dst_ref=outs[i].at[2 * px + py, c],
                                                   send_sem=send_sems.at[i, j], recv_sem=recv_sems.at[i, j],
                                                   device_id=(x, y, 1 - c), device_id_type=MESH)
                fwd.start()
                sends.append(fwd)
        for i in range(n):
            for j, (px, py) in enumerate(chips):
                theirs = outs[i].at[2 * px + py, 1 - c]
                pltpu.make_async_remote_copy(src_ref=theirs, dst_ref=theirs, send_sem=send_sems.at[i, j], recv_sem=recv_sems.at[i, j],
                                             device_id=(x, y, c), device_id_type=MESH).wait_recv()
        for i in range(n):
            pltpu.make_async_remote_copy(src_ref=ins[i], dst_ref=outs[i].at[2 * x + y], send_sem=send_sems.at[i, 3],
                                         recv_sem=recv_sems.at[i, 3], device_id=(x, y, c), device_id_type=MESH).wait_recv()
        for cp in sends:
            cp.wait_send()

    return pl.pallas_call(
        body, name=name, in_specs=[hbm] * (2 * n), out_specs=[hbm] * n,
        out_shape=[SDS(l.shape, l.dtype) for l in lands],
        input_output_aliases={n + i: i for i in range(n)},
        scratch_shapes=[pltpu.SemaphoreType.DMA((n, 4)), pltpu.SemaphoreType.DMA((n, 4))])(*arrs, *lands)


def _swap_halves(gs, name):
    n = len(gs)
    hbm = pl.BlockSpec(memory_space=pl.ANY)

    def body(*refs):
        ins, outs = refs[:n], refs[n:2 * n]
        send_sems, recv_sems = refs[2 * n:]
        x, y, c = _position()
        cps = []
        for i in range(n):
            for p in range(4):
                cp = pltpu.make_async_remote_copy(src_ref=ins[i].at[p, 1 - c], dst_ref=outs[i].at[p], send_sem=send_sems.at[4 * i + p],
                                                  recv_sem=recv_sems.at[4 * i + p], device_id=(x, y, 1 - c), device_id_type=MESH)
                cp.start()
                cps.append(cp)
        for cp in cps:
            cp.wait()

    return pl.pallas_call(
        body, name=name, in_specs=[hbm] * n, out_specs=[hbm] * n,
        out_shape=[SDS((4,) + g.shape[2:], g.dtype) for g in gs],
        scratch_shapes=[pltpu.SemaphoreType.DMA((4 * n,)), pltpu.SemaphoreType.DMA((4 * n,))])(*gs)


def _exchange_start(s1s, name):
    n = len(s1s)

    def body(*refs):
        srcs, lands = refs[:n], refs[n:2 * n]
        send_sems, recv_sems = refs[2 * n:2 * n + 2]
        token = refs[-1]
        x, y, c = _position()
        for i in range(n):
            for j, (px, py) in enumerate(_other_chips(x, y)):
                pltpu.make_async_remote_copy(src_ref=srcs[i].at[2 * px + py], dst_ref=lands[i].at[j], send_sem=send_sems.at[3 * i + j],
                                             recv_sem=recv_sems.at[3 * i + j], device_id=(px, py, c), device_id_type=MESH).start()
        token[...] = jnp.zeros_like(token)

    lands = [lax.empty((3,) + s.shape[1:], F32) for s in s1s]
    res = pl.pallas_call(
        body, name=name,
        out_shape=[pltpu.SemaphoreType.DMA((3 * n,)), pltpu.SemaphoreType.DMA((3 * n,))]
        + [pltpu.HBM(a.shape, F32) for a in list(s1s) + lands] + [SDS((8, 128), F32)],
        in_specs=[HBM_SPEC] * (2 * n), out_specs=[SEM_SPEC, SEM_SPEC] + [HBM_SPEC] * (2 * n) + [pl.BlockSpec(memory_space=pltpu.VMEM)],
        input_output_aliases={i: 2 + i for i in range(2 * n)},
        compiler_params=pltpu.CompilerParams(has_side_effects=DATAFLOW_EFFECT),
    )(*[pltpu.with_memory_space_constraint(a, pltpu.HBM) for a in list(s1s) + lands])
    return res[0], res[1], list(res[2:2 + n]), list(res[2 + n:2 + 2 * n]), res[-1]


def _exchange_wait(send_sems, recv_sems, s1s, lands, after, name):
    n = len(s1s)

    def body(*refs):
        srcs, lands_r = refs[:n], refs[n:2 * n]
        send_r, recv_r = refs[2 * n:2 * n + 2]
        x, y, c = _position()
        for i in range(n):
            for j, (px, py) in enumerate(_other_chips(x, y)):
                cp = pltpu.make_async_remote_copy(src_ref=srcs[i].at[2 * px + py], dst_ref=lands_r[i].at[j], send_sem=send_r.at[3 * i + j],
                                                  recv_sem=recv_r.at[3 * i + j], device_id=(px, py, c), device_id_type=MESH)
                cp.wait_send()
                cp.wait_recv()

    res = pl.pallas_call(
        body, name=name, out_shape=[pltpu.HBM(a.shape, F32) for a in list(s1s) + list(lands)],
        in_specs=[HBM_SPEC] * (2 * n) + [SEM_SPEC, SEM_SPEC, pl.BlockSpec(memory_space=pl.ANY)], out_specs=[HBM_SPEC] * (2 * n),
        input_output_aliases={i: i for i in range(2 * n)},
        compiler_params=pltpu.CompilerParams(has_side_effects=DATAFLOW_EFFECT),
    )(*s1s, *lands, send_sems, recv_sems, after)
    return list(res[:n]), list(res[n:])


def _join_halves(f2s, name):
    n = len(f2s)
    hbm = pl.BlockSpec(memory_space=pl.ANY)

    def body(*refs):
        ins, outs = refs[:n], refs[n:2 * n]
        send_sems, recv_sems = refs[2 * n:]
        x, y, c = _position()
        cps = []
        for i in range(n):
            cp = pltpu.make_async_remote_copy(src_ref=ins[i].at[c], dst_ref=outs[i].at[c], send_sem=send_sems.at[i],
                                              recv_sem=recv_sems.at[i], device_id=(x, y, 1 - c), device_id_type=MESH)
            cp.start()
            cps.append(cp)
        for i in range(n):
            pltpu.make_async_remote_copy(src_ref=ins[i].at[1 - c], dst_ref=outs[i].at[1 - c], send_sem=send_sems.at[i],
                                         recv_sem=recv_sems.at[i], device_id=(x, y, 1 - c), device_id_type=MESH).wait_recv()
        for cp in cps:
            cp.wait_send()

    return pl.pallas_call(
        body, name=name, in_specs=[hbm] * n, out_specs=[hbm] * n, out_shape=[SDS(f.shape, f.dtype) for f in f2s],
        input_output_aliases={i: i for i in range(n)},
        scratch_shapes=[pltpu.SemaphoreType.DMA((n,)), pltpu.SemaphoreType.DMA((n,))])(*f2s)


def _sum_tile(rows, cols):
    return _row_tile(rows, max(8, (2 ** 18 // cols) // 8 * 8))


def _add_pair(g, r1, c, name):
    _, R, C = r1.shape
    tr = _sum_tile(R, C)

    def body(c_ref, a_ref, b_ref, o_ref):
        del c_ref
        o_ref[...] = a_ref[...] + b_ref[...]

    blk = pl.BlockSpec((None, tr, C), lambda p, i, cr: (p, i, 0))
    return pl.pallas_call(
        body, name=name,
        grid_spec=pltpu.PrefetchScalarGridSpec(
            num_scalar_prefetch=1, grid=(4, R // tr),
            in_specs=[pl.BlockSpec((None, None, tr, C), lambda p, i, cr: (p, cr[0], i, 0)), blk], out_specs=blk),
        out_shape=SDS((4, R, C), F32))(c, g, r1)


def _add_four(s1, r2, me_c, name):
    _, R, C = s1.shape
    tr = _sum_tile(R, C)

    def body(m_ref, a_ref, b_ref, o_ref):
        del m_ref
        o_ref[...] = ((a_ref[...] + b_ref[0]) + b_ref[1]) + b_ref[2]

    return pl.pallas_call(
        body, name=name,
        grid_spec=pltpu.PrefetchScalarGridSpec(
            num_scalar_prefetch=1, grid=(R // tr,),
            in_specs=[pl.BlockSpec((None, tr, C), lambda i, mr: (mr[0], i, 0)), pl.BlockSpec((3, tr, C), lambda i, mr: (0, i, 0))],
            out_specs=pl.BlockSpec((None, tr, C), lambda i, mr: (mr[1], i, 0))),
        out_shape=SDS((2, R, C), F32))(me_c, s1, r2)


def _all_reduce_small(vs):
    n = len(vs)

    def body(*refs):
        ins, outs, bufs = refs[:n], refs[n:2 * n], refs[2 * n:3 * n]
        send_sems, recv_sems = refs[3 * n:]
        x, y, c = _position()
        me = 4 * x + 2 * y + c
        for i in range(n):
            bufs[i][me] = ins[i][...]
        cps = []
        for k in range(1, 8):
            to = (1 - x if k & 4 else x, 1 - y if k & 2 else y, 1 - c if k & 1 else c)
            for i in range(n):
                cp = pltpu.make_async_remote_copy(src_ref=bufs[i].at[me], dst_ref=bufs[i].at[me], send_sem=send_sems.at[7 * i + k - 1],
                                                  recv_sem=recv_sems.at[7 * i + k - 1], device_id=to, device_id_type=MESH)
                cp.start()
                cps.append(cp)
        for cp in cps:
            cp.wait_send()
        for k in range(1, 8):
            src = 4 * (1 - x if k & 4 else x) + 2 * (1 - y if k & 2 else y) + (1 - c if k & 1 else c)
            for i in range(n):
                pltpu.make_async_remote_copy(src_ref=bufs[i].at[src], dst_ref=bufs[i].at[src], send_sem=send_sems.at[7 * i + k - 1],
                                             recv_sem=recv_sems.at[7 * i + k - 1], device_id=(x, y, c), device_id_type=MESH).wait_recv()
        for i in range(n):
            acc = bufs[i][0]
            for k in range(1, 8):
                acc = acc + bufs[i][k]
            outs[i][...] = acc

    vm = pl.BlockSpec(memory_space=pltpu.VMEM)
    return pl.pallas_call(
        body, name="all_reduce_small", in_specs=[vm] * n, out_specs=[vm] * n, out_shape=[SDS(v.shape, F32) for v in vs],
        scratch_shapes=[pltpu.VMEM((8,) + v.shape, F32) for v in vs]
        + [pltpu.SemaphoreType.DMA((7 * n,)), pltpu.SemaphoreType.DMA((7 * n,))])(*vs)


def _reduce_begin(grads, tag):
    _, _, c = _position()
    cs = jnp.reshape(c, (1,)).astype(jnp.int32)
    g4 = [g.reshape(4, 2, g.shape[1] // 2, g.shape[2]) for g in grads]
    r1 = _swap_halves(g4, "rs_swap_" + tag)
    s1 = [_add_pair(g, r, cs, f"rs_add_pair_{tag}{i}") for i, (g, r) in enumerate(zip(g4, r1))]
    send_sems, recv_sems, s1, lands, token = _exchange_start(s1, "rs_exchange_start_" + tag)
    return (send_sems, recv_sems, s1, lands), token


def _reduce_end(state, after, tag):
    x, y, c = _position()
    send_sems, recv_sems, s1, lands = state
    s1, lands = _exchange_wait(send_sems, recv_sems, s1, lands, after, "rs_exchange_wait_" + tag)
    me_c = jnp.stack([2 * x + y, c]).astype(jnp.int32)
    f2 = [_add_four(s, l, me_c, f"rs_add_four_{tag}{i}") for i, (s, l) in enumerate(zip(s1, lands))]
    return [f.reshape(2 * f.shape[1], f.shape[2]) for f in _join_halves(f2, "rs_join_" + tag)]


def _halves(a):
    return a.reshape((2, a.shape[0] // 2) + a.shape[1:])


def _after(a, token):
    return a + token[0, 0]


def _local_step(x, mem, target, sp, w_in, ex):
    S, D = x.shape
    n_in = 4 * w_in.shape[2]
    bw = {"w_in": w_in}
    band, buckets = _bias_static()
    buckets = jnp.asarray(buckets)

    xn = _rmsnorm(x, _after(sp["attn_norm_w"], ex.start_rest()), "rms_in")
    z = _mm_nn(xn, bw["w_in"], BF16, "in_proj")
    qh, kh, vh = _qkv_prep(z, sp["q_norm_w"], sp["k_norm_w"])
    tab = sp["rel_bias_table"].T.reshape(N_GROUPS, HPG, N_BUCKETS)
    bias = _bias_fwd(jnp.pad(tab, ((0, 0), (0, 8 - HPG), (0, 0))), buckets)
    biasm = jnp.where(jnp.asarray(band)[None, None], bias[:, :HPG].reshape(N_GROUPS, HPG, NQ, 2 * NQ), NEG)
    os_, lses = [], []
    for g, (_, dil) in enumerate(ATTN_GROUPS):
        o_g, lse_g = _attn_fwd(qh, kh, vh, biasm[g], g, dil)
        os_.append(o_g)
        lses.append(lse_g)
    attn = _merge_fwd(os_, lses)
    u1, u3 = _conv_fwd(z, sp["conv_dw_w"], sp["conv_dw_b"], sp["conv_ln_w"], sp["conv_ln_b"])
    bw.update(ex.rest_weights(after=attn))
    F2 = 4 * bw["w_up"].shape[2]
    mn, kv, mk, mv = _memkv_fwd(mem, sp["mem_norm_w"], bw["w_mem_kv"], sp["xk_norm_w"])
    oc = _cross_fwd(z, sp["xq_norm_w"], mk, mv)
    h1, hn = _outproj_fwd(x, z, sp["b_gate"], attn, u3, oc, bw["w_attn_o"], bw["w_conv_o"], bw["w_cross_o"], bw["w_out"],
                          sp["ffn_norm_w"])
    up = _mm_nn(hn, bw["w_up"], BF16, "ffn_up")
    act = _ffn_act_fwd(up, sp["ffn_conv_w"], sp["ffn_conv_b"])
    dy, loss_tile = _ffn_down_loss(act, bw["w_down"], h1, target)

    gs, gb = {}, {}
    dca, dcg, acca, accg, gb["w_down"] = _ffn_bwd_a(dy, bw["w_down"], up, sp["ffn_conv_w"], sp["ffn_conv_b"])
    tap = lambda acc: jnp.transpose(acc, (1, 0, 2)).reshape(8, F2 // 2)
    ta, tg = tap(acca), tap(accg)
    gs["ffn_conv_w"] = jnp.concatenate([ta[:FFN_K], tg[:FFN_K]], axis=1)
    gs["ffn_conv_b"] = jnp.concatenate([ta[FFN_K:FFN_K + 1], tg[FFN_K:FFN_K + 1]], axis=1)
    dup = _ffn_bwd_b(dca, dcg, sp["ffn_conv_w"])
    gb["w_up"] = _mm_tn(hn, dup, 4, "dw_up")
    tok = ex.reduce_begin("a", ("w_down", "w_up"), gb)
    dh1, gs["ffn_norm_w"] = _norm_in_bwd(dup, bw["w_up"], h1, _after(sp["ffn_norm_w"], tok), dy, "ffn_in_bwd")
    dz, dattn, du3, doc, dya, dyc, dyx, merged, gs["b_gate"] = _outproj_bwd(
        dh1, z, sp["b_gate"], attn, u3, oc, bw["w_attn_o"], bw["w_conv_o"], bw["w_cross_o"], bw["w_out"], n_in)
    gb["w_out"] = _mm_tn(merged, dh1, 1, "dw_out")[0]
    gb["w_attn_o"] = _mm_tn(attn, dya, 4, "dw_attn_o")
    gb["w_conv_o"] = _mm_tn(u3, dyc, 4, "dw_conv_o")
    gb["w_cross_o"] = _mm_tn(oc, dyx, 4, "dw_cross_o")
    dz, dmk, dmv, gs["xq_norm_w"] = _cross_bwd(dz, doc, z, sp["xq_norm_w"], mk, mv)
    gb["w_mem_kv"], gs["xk_norm_w"], gs["mem_norm_w"] = _memkv_bwd(
        dmk, dmv, kv, mem, mn, sp["mem_norm_w"], bw["w_mem_kv"], sp["xk_norm_w"])
    ex.reduce_end("a", after=gs["mem_norm_w"])
    tok = ex.reduce_begin("b", ("w_out", "w_attn_o", "w_conv_o", "w_cross_o", "w_mem_kv"), gb)
    du1, cacc = _conv_bwd_a(du3, u1, z, _after(sp["conv_ln_w"], tok), sp["conv_ln_b"])
    gs["conv_dw_w"], gs["conv_dw_b"] = cacc[:CONV_K], cacc[32:33]
    gs["conv_ln_w"], gs["conv_ln_b"] = cacc[33:34], cacc[34:35]
    dz = _conv_bwd_b(dz, du1, z, sp["conv_dw_w"])
    wg, dah, dhb = _merge_bwd(dattn, os_, lses)
    dqs, dks, dvs, dsbs = [], [], [], []
    for g, (_, dil) in enumerate(ATTN_GROUPS):
        dq_g, dk_g, dv_g, dsb_g = _attn_bwd(qh, kh, vh, biasm[g], dah, wg[g], dhb, lses[g], g, dil)
        dqs.append(dq_g)
        dks.append(dk_g)
        dvs.append(dv_g)
        dsbs.append(dsb_g.reshape(HPG, NQ * 2 * NQ))
    dtab = _bias_bwd(jnp.pad(jnp.stack(dsbs), ((0, 0), (0, 8 - HPG), (0, 0))), buckets)
    gs["rel_bias_table"] = dtab[:, :HPG].reshape(N_GROUPS * HPG, N_BUCKETS).T
    dz, gs["q_norm_w"], gs["k_norm_w"] = _qkv_bwd(dz, z, dqs, dks, dvs, sp["q_norm_w"], sp["k_norm_w"])
    ex.reduce_end("b", after=gs["q_norm_w"])
    gb["w_in"] = _mm_tn(xn, dz, 4, "dw_in")
    tok = ex.reduce_begin("c", ("w_in",), gb)
    dx, gs["attn_norm_w"] = _norm_in_bwd(dz, bw["w_in"], x, _after(sp["attn_norm_w"], tok), dh1, "in_bwd")
    ex.reduce_end("c", after=gs["attn_norm_w"])
    return loss_tile, dx, gs, gb


SMALL = ("rel_bias_table", "attn_norm_w", "b_gate", "q_norm_w", "k_norm_w", "conv_dw_w", "conv_dw_b", "conv_ln_w", "conv_ln_b",
         "mem_norm_w", "xq_norm_w", "xk_norm_w", "ffn_norm_w", "ffn_conv_w", "ffn_conv_b")
SMALL_SHARDED = ("conv_dw_w", "ffn_conv_w")
BIG_COL = ("w_in", "w_attn_o", "w_conv_o", "w_cross_o", "w_up")
BIG_ROW = ("w_mem_kv", "w_out", "w_down")
BIG = BIG_COL + BIG_ROW
WEIGHTS = ("rel_bias_table", "attn_norm_w", "w_in", "b_gate", "q_norm_w", "k_norm_w", "w_attn_o", "conv_dw_w", "conv_dw_b",
           "conv_ln_w", "conv_ln_b", "w_conv_o", "mem_norm_w", "w_mem_kv", "xq_norm_w", "xk_norm_w", "w_cross_o", "w_out",
           "ffn_norm_w", "w_up", "ffn_conv_w", "ffn_conv_b", "w_down")


class _Exchanges:
    REST = tuple(k for k in BIG if k != "w_in")

    def __init__(self, w):
        self.w = w
        self.pending = {}
        self.reduced = {}

    def _whole(self, k, ga):
        ga = ga.reshape((4,) + self.w[k].shape)
        return ga if k in BIG_COL else ga.reshape((4 * self.w[k].shape[0],) + self.w[k].shape[1:])

    def first_weights(self):
        local = [_halves(self.w["w_in"].astype(BF16))]
        for k in SMALL_SHARDED:
            flat = jnp.ravel(self.w[k])
            local.append(jnp.pad(flat, (0, (-flat.shape[0]) % 2048)).reshape(2, -1, 128))
        gathered = _all_gather_xy(local, "gather_first")
        self.first = gathered[0]
        small = {}
        for k, ga in zip(SMALL_SHARDED, gathered[1:]):
            r, cdim = self.w[k].shape
            parts = ga.reshape(4, -1)[:, :r * cdim].reshape(4, r, cdim)
            small[k] = jnp.transpose(parts, (1, 0, 2)).reshape(r, 4 * cdim)
        return self._whole("w_in", gathered[0]), small

    def start_rest(self):
        local = [_halves(self.w[k].astype(BF16)) for k in self.REST]
        local, _ = lax.optimization_barrier((local, self.first))
        send_sems, recv_sems, ins, lands, token = _ag_ici_start(local, "gather_rest_start")
        self.pending["rest"] = (send_sems, recv_sems, ins, lands)
        return token

    def rest_weights(self, after):
        send_sems, recv_sems, ins, lands = self.pending.pop("rest")
        ins, lands = _ag_ici_wait(send_sems, recv_sems, ins, lands, after, "gather_rest_wait")
        gathered = _ag_finish(ins, lands, "gather_rest_finish")
        return {k: self._whole(k, ga) for k, ga in zip(self.REST, gathered)}

    def reduce_begin(self, tag, names, gb):
        parts = [gb[k].reshape((4,) + self.w[k].shape) for k in names]
        state, token = _reduce_begin(parts, tag)
        self.pending[tag] = (state, names)
        return token

    def reduce_end(self, tag, after):
        state, names = self.pending.pop(tag)
        self.reduced.update(zip(names, _reduce_end(state, after, tag)))


def _step(x, mem, target, w, m, v):
    xi, yi, _ = _position()
    shard = 2 * xi + yi
    ex = _Exchanges(w)
    w_in, small_gathered = ex.first_weights()
    sp = {k: w[k] for k in SMALL if k not in SMALL_SHARDED}
    sp.update(small_gathered)

    loss_tile, dx, gs, _ = _local_step(x, mem, target, sp, w_in, ex)
    g_big = ex.reduced

    red = _all_reduce_small([loss_tile] + [gs[k] for k in SMALL])
    loss = red[0][0, 0]
    g_small = dict(zip(SMALL, red[1:]))
    for k in SMALL_SHARDED:
        cdim = w[k].shape[1]
        g_small[k] = lax.dynamic_slice_in_dim(g_small[k], shard * cdim, cdim, axis=1)

    grads, delta, new_m, new_v = {}, {}, {}, {}
    for k in BIG:
        grads[k] = g_big[k]
        delta[k], new_m[k], new_v[k] = _adamw(w[k], g_big[k], m[k], v[k], "adamw_" + k)
    outs = _adamw_small([w[k] for k in SMALL], [g_small[k] for k in SMALL], [m[k] for k in SMALL], [v[k] for k in SMALL])
    for dst, vals in zip((delta, new_m, new_v), outs):
        dst.update(zip(SMALL, vals))
    grads.update(g_small)
    return loss, dx, grads, delta, new_m, new_v


def kernel(x, mem, rel_bias_table, attn_norm_w, w_in, b_gate, q_norm_w, k_norm_w, w_attn_o, conv_dw_w, conv_dw_b, conv_ln_w, conv_ln_b, w_conv_o, mem_norm_w, w_mem_kv, xq_norm_w, xk_norm_w, w_cross_o, w_out, ffn_norm_w, w_up, ffn_conv_w, ffn_conv_b, w_down, loss_target, m_rel_bias_table, m_attn_norm_w, m_w_in, m_b_gate, m_q_norm_w, m_k_norm_w, m_w_attn_o, m_conv_dw_w, m_conv_dw_b, m_conv_ln_w, m_conv_ln_b, m_w_conv_o, m_mem_norm_w, m_w_mem_kv, m_xq_norm_w, m_xk_norm_w, m_w_cross_o, m_w_out, m_ffn_norm_w, m_w_up, m_ffn_conv_w, m_ffn_conv_b, m_w_down, v_rel_bias_table, v_attn_norm_w, v_w_in, v_b_gate, v_q_norm_w, v_k_norm_w, v_w_attn_o, v_conv_dw_w, v_conv_dw_b, v_conv_ln_w, v_conv_ln_b, v_w_conv_o, v_mem_norm_w, v_w_mem_kv, v_xq_norm_w, v_xk_norm_w, v_w_cross_o, v_w_out, v_ffn_norm_w, v_w_up, v_ffn_conv_w, v_ffn_conv_b, v_w_down):
    args = locals()
    def block(name, k):
        a = args[name] if k == "rel_bias_table" else args[name][0]
        return a.reshape(1, -1) if a.ndim == 1 else a

    w = {k: block(k, k) for k in WEIGHTS}
    m = {k: block("m_" + k, k) for k in WEIGHTS}
    v = {k: block("v_" + k, k) for k in WEIGHTS}
    loss, dx, grads, delta, new_m, new_v = _step(x[0], mem[0], loss_target[0], w, m, v)
    out = [loss, dx[None]]
    for d in (grads, delta, new_m, new_v):
        for k in WEIGHTS:
            out.append(d[k].reshape(args[k].shape))
    return tuple(out)
```

```python
import functools
import math

import numpy as np
import jax
import jax.numpy as jnp
from jax import lax
from jax.experimental import pallas as pl
from jax.experimental.pallas import tpu as pltpu

F32, BF16 = jnp.float32, jnp.bfloat16
SDS = jax.ShapeDtypeStruct
MESH = pl.DeviceIdType.MESH

HEAD = 128
N_GROUPS, HPG = 3, 4
ATTN_GROUPS = ((128, 1), (512, 4), (2048, 16))
NQ = 128
QKV_W = N_GROUPS * HPG * HEAD
CH = 512
CONV_K, FFN_K = 31, 3
N_BUCKETS, MAX_DIST = 32, 2048
RMS_EPS, LN_EPS = 1e-6, 1e-5
O_Q, O_K, O_V, O_CV, O_CG, O_XQ, O_G = 0, QKV_W, 2 * QKV_W, 3 * QKV_W, 3 * QKV_W + CH, 3 * QKV_W + 2 * CH, 3 * QKV_W + 3 * CH
ADAM_LR, ADAM_B1, ADAM_B2, ADAM_EPS, ADAM_WD, ADAM_STEP = 0.001, 0.9, 0.999, 1e-08, 0.01, 10
NEG = -1e30
SCALE = HEAD ** -0.5
TM = 512
ATT_RB = 2048
NT_DIMS = (((1,), (1,)), ((), ()))
TN_DIMS = (((0,), (0,)), ((), ()))


CONV_RC = 16
FFN_RC = 32


def _sig(v):
    return 0.5 * jnp.tanh(0.5 * v) + 0.5


def _fold8(v):
    acc = v[0:8]
    for r in range(8, v.shape[0], 8):
        acc = acc + v[r:r + 8]
    return acc


def _row_tile(rows, cap, mult=8):
    best = None
    for t in range(mult, min(rows, cap) + 1, mult):
        if rows % t == 0:
            best = t
    return best if best is not None else rows


def _full(shape):
    n = len(shape)
    return pl.BlockSpec(shape, lambda *a: (0,) * n)


def _rmsnorm(x, w, name):
    S, D = x.shape

    def body(x_ref, w_ref, o_ref):
        xv = x_ref[...]
        r = lax.rsqrt(jnp.mean(xv * xv, axis=-1, keepdims=True) + RMS_EPS)
        o_ref[...] = (xv * r * w_ref[...]).astype(BF16)

    return pl.pallas_call(
        body, name=name, grid=(S // TM,),
        in_specs=[pl.BlockSpec((TM, D), lambda i: (i, 0)), _full((1, D))],
        out_specs=pl.BlockSpec((TM, D), lambda i: (i, 0)),
        out_shape=SDS((S, D), BF16))(x, w)


def _mm_nn(a, b, out_dtype, name):
    M, K = a.shape
    G, _, n = b.shape

    def body(a_ref, b_ref, o_ref):
        o_ref[...] = jnp.dot(a_ref[...].astype(BF16), b_ref[...], preferred_element_type=F32).astype(out_dtype)

    return pl.pallas_call(
        body, name=name, grid=(G, M // TM),
        in_specs=[pl.BlockSpec((TM, K), lambda g, i: (i, 0)), pl.BlockSpec((None, K, n), lambda g, i: (g, 0, 0))],
        out_specs=pl.BlockSpec((TM, n), lambda g, i: (i, g)),
        out_shape=SDS((M, G * n), out_dtype))(a, b)


def _mm_tn(a, b, G, name):
    S, Ka = a.shape
    n = b.shape[1] // G
    tka = Ka
    while tka * n * 4 > 10 * 2 ** 20 and tka % 256 == 0:
        tka //= 2

    def body(a_ref, b_ref, o_ref):
        @pl.when(pl.program_id(2) == 0)
        def _():
            o_ref[...] = jnp.zeros_like(o_ref)
        o_ref[...] += lax.dot_general(a_ref[...].astype(BF16), b_ref[...].astype(BF16), TN_DIMS, preferred_element_type=F32)

    return pl.pallas_call(
        body, name=name, grid=(G, Ka // tka, S // TM),
        in_specs=[pl.BlockSpec((TM, tka), lambda g, i, k: (k, i)), pl.BlockSpec((TM, n), lambda g, i, k: (k, g))],
        out_specs=pl.BlockSpec((None, tka, n), lambda g, i, k: (g, i, 0)),
        out_shape=SDS((G, Ka, n), F32))(a, b)


NORM_RC = 16


def _norm_in_bwd(a, w, xin, nw, resid, name):
    S, K = xin.shape
    G, _, n = w.shape
    tm = 1024 if S % 1024 == 0 and G * K * n * 2 <= 12 * 2 ** 20 else TM
    n2, steps = n, G

    def body(a_ref, w_ref, x_ref, nw_ref, r_ref, o_ref, dnw_ref, acc, part):
        i, g = pl.program_id(0), pl.program_id(1)

        @pl.when((i == 0) & (g == 0))
        def _():
            part[...] = jnp.zeros_like(part)

        @pl.when(g == 0)
        def _():
            acc[...] = jnp.zeros_like(acc)

        acc[...] += lax.dot_general(a_ref[...], w_ref[g], NT_DIMS, preferred_element_type=F32)

        @pl.when(g == steps - 1)
        def _():
            for r0 in range(0, tm, NORM_RC):
                rows = pl.ds(r0, NORM_RC)
                dn = acc[rows, :]
                xv = x_ref[rows, :]
                r = lax.rsqrt(jnp.mean(xv * xv, axis=-1, keepdims=True) + RMS_EPS)
                xhat = xv * r
                dyw = dn * nw_ref[...]
                o_ref[rows, :] = r_ref[rows, :] + r * (dyw - xhat * jnp.mean(dyw * xhat, axis=-1, keepdims=True))
                part[...] += _fold8(dn * xhat)

        @pl.when((i == S // tm - 1) & (g == steps - 1))
        def _():
            dnw_ref[...] = jnp.sum(part[...], axis=0, keepdims=True)

    return pl.pallas_call(
        body, name=name, grid=(S // tm, steps),
        in_specs=[pl.BlockSpec((tm, n2), lambda i, g: (i, g)),
                  pl.BlockSpec((G, K, n), lambda i, g: (0, 0, 0), pipeline_mode=pl.Buffered(1)),
                  pl.BlockSpec((tm, K), lambda i, g: (i, 0)), _full((1, K)), pl.BlockSpec((tm, K), lambda i, g: (i, 0))],
        out_specs=[pl.BlockSpec((tm, K), lambda i, g: (i, 0)), _full((1, K))],
        out_shape=[SDS((S, K), F32), SDS((1, K), F32)],
        scratch_shapes=[pltpu.VMEM((tm, K), F32), pltpu.VMEM((8, K), F32)])(a, w, xin, nw, resid)


def _t5_bucket_np(dist):
    max_exact = N_BUCKETS // 2
    d = np.maximum(dist.astype(np.float32), np.float32(1.0))
    large = max_exact + (np.log(d / np.float32(max_exact)) / np.float32(math.log(MAX_DIST / max_exact))
                         * np.float32(N_BUCKETS - max_exact)).astype(np.int32)
    large = np.minimum(large, N_BUCKETS - 1)
    return np.where(dist < max_exact, dist, large).astype(np.int32)


def _bias_static():
    qi = np.arange(NQ)[:, None]
    kj = np.arange(2 * NQ)[None, :]
    step = qi + NQ - kj
    band = (step >= 0) & (step <= NQ)
    buckets = np.stack([_t5_bucket_np(np.clip(step, 0, None) * dil).reshape(1, -1) for _, dil in ATTN_GROUPS])
    return band, buckets


def _bias_fwd(table_t, buckets):
    nb = buckets.shape[-1]

    def body(t_ref, b_ref, o_ref):
        oh = (b_ref[...] == lax.broadcasted_iota(jnp.int32, (N_BUCKETS, nb), 0)).astype(F32)
        o_ref[...] = jnp.dot(t_ref[...], oh, preferred_element_type=F32, precision=lax.Precision.HIGHEST)

    return pl.pallas_call(
        body, name="bias_fwd", grid=(N_GROUPS,),
        in_specs=[pl.BlockSpec((None, 8, N_BUCKETS), lambda g: (g, 0, 0)), pl.BlockSpec((None, 1, nb), lambda g: (g, 0, 0))],
        out_specs=pl.BlockSpec((None, 8, nb), lambda g: (g, 0, 0)),
        out_shape=SDS((N_GROUPS, 8, nb), F32))(table_t, buckets)


def _bias_bwd(dsb, buckets):
    nb = buckets.shape[-1]

    def body(d_ref, b_ref, o_ref):
        oh = (b_ref[...] == lax.broadcasted_iota(jnp.int32, (N_BUCKETS, nb), 0)).astype(F32)
        o_ref[...] = lax.dot_general(d_ref[...], oh, NT_DIMS, preferred_element_type=F32, precision=lax.Precision.HIGHEST)

    return pl.pallas_call(
        body, name="bias_bwd", grid=(N_GROUPS,),
        in_specs=[pl.BlockSpec((None, 8, nb), lambda g: (g, 0, 0)), pl.BlockSpec((None, 1, nb), lambda g: (g, 0, 0))],
        out_specs=pl.BlockSpec((None, 8, N_BUCKETS), lambda g: (g, 0, 0)),
        out_shape=SDS((N_GROUPS, 8, N_BUCKETS), F32))(dsb, buckets)


def _qkv_prep(z, qw, kw):
    S = z.shape[0]
    nh = N_GROUPS * HPG

    def body(zq, zk, zv, qw_ref, kw_ref, qh, kh, vh):
        for h in range(nh):
            g = h // HPG
            sl = slice(h * HEAD, (h + 1) * HEAD)
            xq = zq[:, sl].astype(F32)
            qh[h] = xq * lax.rsqrt(jnp.mean(xq * xq, axis=-1, keepdims=True) + RMS_EPS) * qw_ref[g:g + 1, :]
            xk = zk[:, sl].astype(F32)
            kh[h] = xk * lax.rsqrt(jnp.mean(xk * xk, axis=-1, keepdims=True) + RMS_EPS) * kw_ref[g:g + 1, :]
            vh[h] = zv[:, sl].astype(F32)

    hm = pl.BlockSpec((nh, TM, HEAD), lambda i: (0, i, 0))
    return pl.pallas_call(
        body, name="qkv_prep", grid=(S // TM,),
        in_specs=[pl.BlockSpec((TM, QKV_W), lambda i: (i, 0)), pl.BlockSpec((TM, QKV_W), lambda i: (i, 1)),
                  pl.BlockSpec((TM, QKV_W), lambda i: (i, 2)), _full((N_GROUPS, HEAD)), _full((N_GROUPS, HEAD))],
        out_specs=[hm, hm, hm],
        out_shape=[SDS((nh, S, HEAD), F32)] * 3)(z, z, z, qw, kw)


def _rows(start, d):
    return pl.ds(start, NQ) if d == 1 else pl.ds(start, NQ, stride=d)


def _attn_fwd(qh, kh, vh, biasm, g, d):
    S = qh.shape[1]
    RB = ATT_RB
    nbk, nq = S // RB, RB // (NQ * d)

    def body(q_ref, k_ref, v_ref, bias_ref, o_ref, lse_ref, kbuf, vbuf):
        b = pl.program_id(1)

        @pl.when(b == 0)
        def _():
            kbuf[0:RB, :] = jnp.zeros((RB, HEAD), F32)
            vbuf[0:RB, :] = jnp.zeros((RB, HEAD), F32)

        @pl.when(b > 0)
        def _():
            kbuf[0:RB, :] = kbuf[RB:2 * RB, :]
            vbuf[0:RB, :] = vbuf[RB:2 * RB, :]

        kbuf[RB:2 * RB, :] = k_ref[...]
        vbuf[RB:2 * RB, :] = v_ref[...]
        bias = bias_ref[...]
        col = lax.broadcasted_iota(jnp.int32, (NQ, 2 * NQ), 1)

        for qb in range(nq):
            def unit(r, carry, qb=qb):
                qs = qb * NQ * d + r
                q = q_ref[_rows(qs, d), :].astype(BF16)
                kw = jnp.concatenate([kbuf[_rows(RB + qs - NQ * d, d), :], kbuf[_rows(RB + qs, d), :]], axis=0).astype(BF16)
                vw = jnp.concatenate([vbuf[_rows(RB + qs - NQ * d, d), :], vbuf[_rows(RB + qs, d), :]], axis=0).astype(BF16)
                s = lax.dot_general(q, kw, NT_DIMS, preferred_element_type=F32) * SCALE + bias
                if qb == 0:
                    s = jnp.where((col < NQ) & (b == 0), NEG, s)
                m = jnp.max(s, axis=-1, keepdims=True)
                p = jnp.exp(s - m)
                l = jnp.sum(p, axis=-1, keepdims=True)
                o = jnp.dot(p.astype(BF16), vw, preferred_element_type=F32) / l
                o_ref[_rows(qs, d), :] = o
                lse_ref[_rows(qs, d), :] = jnp.broadcast_to(m + jnp.log(l), (NQ, HEAD))
                return carry

            for r in range(d):
                unit(r, 0)

    blk = lambda f: pl.BlockSpec((None, RB, HEAD), f)
    return pl.pallas_call(
        body, name=f"attn_fwd_g{g}", grid=(HPG, nbk),
        in_specs=[blk(lambda h, b: (HPG * g + h, b, 0))] * 3 + [pl.BlockSpec((None, NQ, 2 * NQ), lambda h, b: (h, 0, 0))],
        out_specs=[blk(lambda h, b: (h, b, 0))] * 2,
        out_shape=[SDS((HPG, S, HEAD), F32)] * 2,
        scratch_shapes=[pltpu.VMEM((2 * RB, HEAD), F32)] * 2)(qh, kh, vh, biasm)


def _attn_bwd(qh, kh, vh, biasm, da, wg, dh, lse, g, d):
    S = qh.shape[1]
    RB = ATT_RB
    nbk, nq = S // RB, RB // (NQ * d)

    def body(q_ref, k_ref, v_ref, bias_ref, da_ref, wg_ref, dh_ref, lse_ref,
             dq_ref, dk_ref, dv_ref, dsb_ref, kbuf, vbuf, dkbuf, dvbuf):
        b = pl.program_id(1)
        zero = jnp.zeros((RB, HEAD), F32)

        @pl.when(b == 0)
        def _():
            kbuf[0:RB, :] = zero
            vbuf[0:RB, :] = zero
            dkbuf[0:RB, :] = zero
            dvbuf[0:RB, :] = zero
            dsb_ref[...] = jnp.zeros_like(dsb_ref)

        @pl.when(b > 0)
        def _():
            kbuf[0:RB, :] = kbuf[RB:2 * RB, :]
            vbuf[0:RB, :] = vbuf[RB:2 * RB, :]
            dkbuf[0:RB, :] = dkbuf[RB:2 * RB, :]
            dvbuf[0:RB, :] = dvbuf[RB:2 * RB, :]

        dkbuf[RB:2 * RB, :] = zero
        dvbuf[RB:2 * RB, :] = zero

        @pl.when(b < nbk)
        def _():
            kbuf[RB:2 * RB, :] = k_ref[...]
            vbuf[RB:2 * RB, :] = v_ref[...]
            bias = bias_ref[...]
            col = lax.broadcasted_iota(jnp.int32, (NQ, 2 * NQ), 1)

            for qb in range(nq):
                def unit(r, carry, qb=qb):
                    qs = qb * NQ * d + r
                    prev, cur = _rows(RB + qs - NQ * d, d), _rows(RB + qs, d)
                    q = q_ref[_rows(qs, d), :].astype(BF16)
                    kw = jnp.concatenate([kbuf[prev, :], kbuf[cur, :]], axis=0).astype(BF16)
                    vw = jnp.concatenate([vbuf[prev, :], vbuf[cur, :]], axis=0).astype(BF16)
                    s = lax.dot_general(q, kw, NT_DIMS, preferred_element_type=F32) * SCALE + bias
                    if qb == 0:
                        s = jnp.where((col < NQ) & (b == 0), NEG, s)
                    p = jnp.exp(s - lse_ref[_rows(qs, d), :][:, 0:1])
                    w = wg_ref[_rows(qs, d), :]
                    do = (da_ref[_rows(qs, d), :] * w).astype(BF16)
                    dp = lax.dot_general(do, vw, NT_DIMS, preferred_element_type=F32)
                    ds = p * (dp - w[:, 0:1] * dh_ref[_rows(qs, d), :][:, 0:1])
                    dsb_ref[...] += ds
                    dsb = ds.astype(BF16)
                    dq_ref[_rows(qs, d), :] = jnp.dot(dsb, kw, preferred_element_type=F32) * SCALE
                    dkw = lax.dot_general(dsb, q, TN_DIMS, preferred_element_type=F32) * SCALE
                    dvw = lax.dot_general(p.astype(BF16), do, TN_DIMS, preferred_element_type=F32)
                    dkbuf[prev, :] += dkw[0:NQ, :]
                    dkbuf[cur, :] += dkw[NQ:2 * NQ, :]
                    dvbuf[prev, :] += dvw[0:NQ, :]
                    dvbuf[cur, :] += dvw[NQ:2 * NQ, :]
                    return carry

                for r in range(d):
                    unit(r, 0)

        dk_ref[...] = dkbuf[0:RB, :]
        dv_ref[...] = dvbuf[0:RB, :]

    blk = lambda f: pl.BlockSpec((None, RB, HEAD), f)
    cur_g = blk(lambda h, b: (HPG * g + h, jnp.minimum(b, nbk - 1), 0))
    cur = blk(lambda h, b: (h, jnp.minimum(b, nbk - 1), 0))
    prv = blk(lambda h, b: (h, jnp.maximum(b - 1, 0), 0))
    sq = pl.BlockSpec((None, NQ, 2 * NQ), lambda h, b: (h, 0, 0))
    return pl.pallas_call(
        body, name=f"attn_bwd_g{g}", grid=(HPG, nbk + 1),
        in_specs=[cur_g, cur_g, cur_g, sq, cur, cur, cur, cur],
        out_specs=[cur, prv, prv, sq],
        out_shape=[SDS((HPG, S, HEAD), F32)] * 3 + [SDS((HPG, NQ, 2 * NQ), F32)],
        scratch_shapes=[pltpu.VMEM((2 * RB, HEAD), F32)] * 4)(qh, kh, vh, biasm, da, wg, dh, lse)


def _merge_weights(l0, l1, l2):
    m = jnp.maximum(jnp.maximum(l0, l1), l2)
    e0, e1, e2 = jnp.exp(l0 - m), jnp.exp(l1 - m), jnp.exp(l2 - m)
    inv = 1.0 / (e0 + e1 + e2)
    return e0 * inv, e1 * inv, e2 * inv


def _merge_fwd(os_, lses):
    S = os_[0].shape[1]

    def body(o0, o1, o2, l0, l1, l2, a_ref):
        for h in range(HPG):
            w0, w1, w2 = _merge_weights(l0[h], l1[h], l2[h])
            a_ref[:, h * HEAD:(h + 1) * HEAD] = (w0 * o0[h] + w1 * o1[h] + w2 * o2[h]).astype(BF16)

    hm = pl.BlockSpec((HPG, TM, HEAD), lambda i: (0, i, 0))
    return pl.pallas_call(
        body, name="merge_fwd", grid=(S // TM,), in_specs=[hm] * 6,
        out_specs=pl.BlockSpec((TM, CH), lambda i: (i, 0)),
        out_shape=SDS((S, CH), BF16))(*os_, *lses)


def _merge_bwd(dattn, os_, lses):
    S = dattn.shape[0]

    def body(da_ref, o0, o1, o2, l0, l1, l2, w0_ref, w1_ref, w2_ref, dah_ref, dh_ref):
        for h in range(HPG):
            w = _merge_weights(l0[h], l1[h], l2[h])
            attn = w[0] * o0[h] + w[1] * o1[h] + w[2] * o2[h]
            da = da_ref[:, h * HEAD:(h + 1) * HEAD]
            for w_ref, wv in zip((w0_ref, w1_ref, w2_ref), w):
                w_ref[h] = wv
            dah_ref[h] = da
            dh_ref[h] = jnp.broadcast_to(jnp.sum(da * attn, axis=-1, keepdims=True), (TM, HEAD))

    hm = pl.BlockSpec((HPG, TM, HEAD), lambda i: (0, i, 0))
    res = pl.pallas_call(
        body, name="merge_bwd", grid=(S // TM,),
        in_specs=[pl.BlockSpec((TM, CH), lambda i: (i, 0))] + [hm] * 6,
        out_specs=[hm] * 5, out_shape=[SDS((HPG, S, HEAD), F32)] * 5)(dattn, *os_, *lses)
    return res[0:3], res[3], res[4]


def _qkv_bwd(dz, z, dqs, dks, dvs, qw, kw):
    S = z.shape[0]
    nh = N_GROUPS * HPG

    def body(dz_in, zq, zk, *refs):
        del dz_in
        dq_refs, dk_refs, dv_refs = refs[0:3], refs[3:6], refs[6:9]
        qw_ref, kw_ref, dz_ref, dqw_ref, dkw_ref = refs[9:]

        @pl.when(pl.program_id(0) == 0)
        def _():
            dqw_ref[...] = jnp.zeros_like(dqw_ref)
            dkw_ref[...] = jnp.zeros_like(dkw_ref)

        def nbwd(xr, dy, wr, dwr, h, off):
            g = h // HPG
            x = xr[:, h * HEAD:(h + 1) * HEAD].astype(F32)
            r = lax.rsqrt(jnp.mean(x * x, axis=-1, keepdims=True) + RMS_EPS)
            xhat = x * r
            dyw = dy * wr[g:g + 1, :]
            dz_ref[:, off + h * HEAD:off + (h + 1) * HEAD] = (
                r * (dyw - xhat * jnp.mean(dyw * xhat, axis=-1, keepdims=True))).astype(BF16)
            dwr[g:g + 1, :] += jnp.sum(dy * xhat, axis=0, keepdims=True)

        for h in range(nh):
            g, hh = h // HPG, h % HPG
            nbwd(zq, dq_refs[g][hh], qw_ref, dqw_ref, h, O_Q)
            nbwd(zk, dk_refs[g][hh], kw_ref, dkw_ref, h, O_K)
            dz_ref[:, O_V + h * HEAD:O_V + (h + 1) * HEAD] = dv_refs[g][hh].astype(BF16)

    hm = pl.BlockSpec((HPG, TM, HEAD), lambda i: (0, i, 0))
    return pl.pallas_call(
        body, name="qkv_bwd", grid=(S // TM,),
        in_specs=[pl.BlockSpec(memory_space=pl.ANY), pl.BlockSpec((TM, QKV_W), lambda i: (i, 0)),
                  pl.BlockSpec((TM, QKV_W), lambda i: (i, 1))] + [hm] * 9 + [_full((N_GROUPS, HEAD)), _full((N_GROUPS, HEAD))],
        out_specs=[pl.BlockSpec((TM, 3 * QKV_W), lambda i: (i, 0)), _full((N_GROUPS, HEAD)), _full((N_GROUPS, HEAD))],
        out_shape=[SDS(dz.shape, BF16), SDS((N_GROUPS, HEAD), F32), SDS((N_GROUPS, HEAD), F32)],
        input_output_aliases={0: 0})(dz, z, z, *dqs, *dks, *dvs, qw, kw)


def _conv_fwd(z, cw, cb, lnw, lnb):
    S = z.shape[0]
    H = 32

    def body(zv, zg, cw_ref, cb_ref, lnw_ref, lnb_ref, u1_ref, u3_ref, xbuf):
        i = pl.program_id(0)

        @pl.when(i == 0)
        def _():
            xbuf[0:H, :] = jnp.zeros((H, CH), F32)

        @pl.when(i > 0)
        def _():
            xbuf[0:H, :] = xbuf[TM:TM + H, :]

        xbuf[H:H + TM, :] = zv[...].astype(F32) * _sig(zg[...].astype(F32))
        for r0 in range(0, TM, CONV_RC):
            rows = pl.ds(r0, CONV_RC)
            acc = jnp.broadcast_to(cb_ref[...], (CONV_RC, CH))
            for k in range(CONV_K):
                acc = acc + xbuf[pl.ds(H - (CONV_K - 1) + k + r0, CONV_RC), :] * cw_ref[k:k + 1, :]
            u1_ref[rows, :] = acc
            mu = jnp.mean(acc, axis=-1, keepdims=True)
            xc = acc - mu
            yl = xc * lax.rsqrt(jnp.mean(xc * xc, axis=-1, keepdims=True) + LN_EPS) * lnw_ref[...] + lnb_ref[...]
            u3_ref[rows, :] = (yl * _sig(yl)).astype(BF16)

    row = pl.BlockSpec((TM, CH), lambda i: (i, 0))
    return pl.pallas_call(
        body, name="conv_fwd", grid=(S // TM,),
        in_specs=[pl.BlockSpec((TM, CH), lambda i: (i, O_CV // CH)), pl.BlockSpec((TM, CH), lambda i: (i, O_CG // CH)),
                  _full((CONV_K, CH)), _full((1, CH)), _full((1, CH)), _full((1, CH))],
        out_specs=[row, row], out_shape=[SDS((S, CH), F32), SDS((S, CH), BF16)],
        scratch_shapes=[pltpu.VMEM((TM + H, CH), F32)])(z, z, cw, cb, lnw, lnb)


def _conv_bwd_a(du3, u1, z, lnw, lnb):
    S = z.shape[0]
    H = 32

    nt = S // TM

    def body(du3_ref, u1_ref, zv, zg, lnw_ref, lnb_ref, du1_ref, acc_ref, xbuf, tacc):
        i = pl.program_id(0)

        @pl.when(i == 0)
        def _():
            xbuf[0:H, :] = jnp.zeros((H, CH), F32)
            tacc[...] = jnp.zeros_like(tacc)

        @pl.when(i > 0)
        def _():
            xbuf[0:H, :] = xbuf[TM:TM + H, :]

        xbuf[H:H + TM, :] = zv[...].astype(F32) * _sig(zg[...].astype(F32))
        for r0 in range(0, TM, CONV_RC):
            rows = pl.ds(r0, CONV_RC)
            u1 = u1_ref[rows, :]
            mu = jnp.mean(u1, axis=-1, keepdims=True)
            xc = u1 - mu
            r = lax.rsqrt(jnp.mean(xc * xc, axis=-1, keepdims=True) + LN_EPS)
            yhat = xc * r
            yl = yhat * lnw_ref[...] + lnb_ref[...]
            sg = _sig(yl)
            dyl = du3_ref[rows, :] * (sg * (1.0 + yl * (1.0 - sg)))
            dyh = dyl * lnw_ref[...]
            du1 = r * (dyh - jnp.mean(dyh, axis=-1, keepdims=True) - yhat * jnp.mean(dyh * yhat, axis=-1, keepdims=True))
            du1_ref[rows, :] = du1
            tacc[33] += _fold8(dyl * yhat)
            tacc[34] += _fold8(dyl)
            tacc[32] += _fold8(du1)
            for k in range(CONV_K):
                tacc[k] += _fold8(du1 * xbuf[pl.ds(H - (CONV_K - 1) + k + r0, CONV_RC), :])

        @pl.when(i == nt - 1)
        def _():
            for k in range(40):
                acc_ref[k:k + 1, :] = jnp.sum(tacc[k], axis=0, keepdims=True)

    row = pl.BlockSpec((TM, CH), lambda i: (i, 0))
    return pl.pallas_call(
        body, name="conv_bwd_a", grid=(nt,),
        in_specs=[row, row, pl.BlockSpec((TM, CH), lambda i: (i, O_CV // CH)), pl.BlockSpec((TM, CH), lambda i: (i, O_CG // CH)),
                  _full((1, CH)), _full((1, CH))],
        out_specs=[row, _full((40, CH))], out_shape=[SDS((S, CH), F32), SDS((40, CH), F32)],
        scratch_shapes=[pltpu.VMEM((TM + H, CH), F32), pltpu.VMEM((40, 8, CH), F32)])(du3, u1, z, z, lnw, lnb)


def _conv_bwd_b(dz, du1, z, cw):
    S = z.shape[0]
    nt = S // TM
    H = 32

    def body(dz_in, du1_ref, zv, zg, cw_ref, dz_ref, ybuf, dgate):
        del dz_in
        i, p = pl.program_id(0), pl.program_id(1)

        @pl.when(p == 0)
        def _():
            @pl.when(i == 0)
            def _():
                ybuf[TM:TM + H, :] = jnp.zeros((H, CH), F32)

            @pl.when(i > 0)
            def _():
                ybuf[TM:TM + H, :] = ybuf[0:H, :]

            ybuf[0:TM, :] = du1_ref[...]
            for r0 in range(0, TM, CONV_RC):
                rows = pl.ds(r0, CONV_RC)
                acc = ybuf[rows, :] * cw_ref[CONV_K - 1:CONV_K, :]
                for j in range(1, CONV_K):
                    acc = acc + ybuf[pl.ds(r0 + j, CONV_RC), :] * cw_ref[CONV_K - 1 - j:CONV_K - j, :]
                val = zv[rows, :].astype(F32)
                sg = _sig(zg[rows, :].astype(F32))
                dz_ref[rows, :] = (acc * sg).astype(BF16)
                dgate[rows, :] = (acc * val * sg * (1.0 - sg)).astype(BF16)

        @pl.when(p == 1)
        def _():
            dz_ref[...] = dgate[...]

    rev = lambda c: pl.BlockSpec((TM, CH), lambda i, p: (nt - 1 - i, c))
    return pl.pallas_call(
        body, name="conv_bwd_b", grid=(nt, 2),
        in_specs=[pl.BlockSpec(memory_space=pl.ANY), rev(0), rev(O_CV // CH), rev(O_CG // CH), _full((CONV_K, CH))],
        out_specs=pl.BlockSpec((TM, CH), lambda i, p: (nt - 1 - i, O_CV // CH + p)),
        out_shape=SDS(dz.shape, BF16),
        scratch_shapes=[pltpu.VMEM((TM + H, CH), F32), pltpu.VMEM((TM, CH), BF16)],
        input_output_aliases={0: 0})(dz, du1, z, z, cw)


def _memkv_fwd(mem, mnw, wkv, xkw):
    M, D = mem.shape

    def body(mem_ref, mnw_ref, w_ref, xkw_ref, mn_ref, kv_ref, mk_ref, mv_ref):
        x = mem_ref[...]
        mn = (x * lax.rsqrt(jnp.mean(x * x, axis=-1, keepdims=True) + RMS_EPS) * mnw_ref[...]).astype(BF16)
        mn_ref[...] = mn
        kv = jnp.dot(mn, w_ref[...], preferred_element_type=F32)
        kv_ref[...] = kv
        for h in range(HPG):
            k = kv[:, h * HEAD:(h + 1) * HEAD]
            mk_ref[:, h * HEAD:(h + 1) * HEAD] = (
                k * lax.rsqrt(jnp.mean(k * k, axis=-1, keepdims=True) + RMS_EPS) * xkw_ref[...]).astype(BF16)
        mv_ref[...] = kv[:, CH:2 * CH].astype(BF16)

    return pl.pallas_call(
        body, name="memkv_fwd",
        out_shape=[SDS((M, D), BF16), SDS((M, 2 * CH), F32), SDS((M, CH), BF16), SDS((M, CH), BF16)])(mem, mnw, wkv, xkw)


def _cross_q(zx, xqw, h):
    x = zx[:, h * HEAD:(h + 1) * HEAD].astype(F32)
    r = lax.rsqrt(jnp.mean(x * x, axis=-1, keepdims=True) + RMS_EPS)
    xhat = x * r
    return xhat, r, xhat * xqw


def _cross_fwd(z, xqw, mk, mv):
    S = z.shape[0]
    M = mk.shape[0]

    def body(zx, xqw_ref, mk_ref, mv_ref, o_ref):
        for h in range(HPG):
            sl = slice(h * HEAD, (h + 1) * HEAD)
            _, _, q = _cross_q(zx, xqw_ref[...], h)
            s = lax.dot_general(q.astype(BF16), mk_ref[:, sl], NT_DIMS, preferred_element_type=F32) * SCALE
            e = jnp.exp(s - jnp.max(s, axis=-1, keepdims=True))
            p = e / jnp.sum(e, axis=-1, keepdims=True)
            o_ref[:, sl] = jnp.dot(p.astype(BF16), mv_ref[:, sl], preferred_element_type=F32).astype(BF16)

    return pl.pallas_call(
        body, name="cross_fwd", grid=(S // TM,),
        in_specs=[pl.BlockSpec((TM, CH), lambda i: (i, O_XQ // CH)), _full((1, HEAD)), _full((M, CH)), _full((M, CH))],
        out_specs=pl.BlockSpec((TM, CH), lambda i: (i, 0)), out_shape=SDS((S, CH), BF16))(z, xqw, mk, mv)


def _cross_bwd(dz, doc, z, xqw, mk, mv):
    S = z.shape[0]
    M = mk.shape[0]

    def body(dz_in, do_ref, zx, xqw_ref, mk_ref, mv_ref, dz_ref, dmk_ref, dmv_ref, dxw_ref):
        del dz_in

        @pl.when(pl.program_id(0) == 0)
        def _():
            dmk_ref[...] = jnp.zeros_like(dmk_ref)
            dmv_ref[...] = jnp.zeros_like(dmv_ref)
            dxw_ref[...] = jnp.zeros_like(dxw_ref)

        for h in range(HPG):
            sl = slice(h * HEAD, (h + 1) * HEAD)
            xhat, r, q = _cross_q(zx, xqw_ref[...], h)
            qb = q.astype(BF16)
            s = lax.dot_general(qb, mk_ref[:, sl], NT_DIMS, preferred_element_type=F32) * SCALE
            e = jnp.exp(s - jnp.max(s, axis=-1, keepdims=True))
            p = e / jnp.sum(e, axis=-1, keepdims=True)
            do = do_ref[:, sl].astype(BF16)
            dp = lax.dot_general(do, mv_ref[:, sl], NT_DIMS, preferred_element_type=F32)
            ds = (p * (dp - jnp.sum(p * dp, axis=-1, keepdims=True)) * SCALE).astype(BF16)
            dq = jnp.dot(ds, mk_ref[:, sl], preferred_element_type=F32)
            dmk_ref[:, sl] += lax.dot_general(ds, qb, TN_DIMS, preferred_element_type=F32)
            dmv_ref[:, sl] += lax.dot_general(p.astype(BF16), do, TN_DIMS, preferred_element_type=F32)
            dyw = dq * xqw_ref[...]
            dz_ref[:, sl] = (r * (dyw - xhat * jnp.mean(dyw * xhat, axis=-1, keepdims=True))).astype(BF16)
            dxw_ref[...] += jnp.sum(dq * xhat, axis=0, keepdims=True)

    return pl.pallas_call(
        body, name="cross_bwd", grid=(S // TM,),
        in_specs=[pl.BlockSpec(memory_space=pl.ANY), pl.BlockSpec((TM, CH), lambda i: (i, 0)),
                  pl.BlockSpec((TM, CH), lambda i: (i, O_XQ // CH)), _full((1, HEAD)), _full((M, CH)), _full((M, CH))],
        out_specs=[pl.BlockSpec((TM, CH), lambda i: (i, O_XQ // CH)), _full((M, CH)), _full((M, CH)), _full((1, HEAD))],
        out_shape=[SDS(dz.shape, BF16), SDS((M, CH), F32), SDS((M, CH), F32), SDS((1, HEAD), F32)],
        input_output_aliases={0: 0})(dz, doc, z, xqw, mk, mv)


def _memkv_bwd(dmk, dmv, kv, mem, mn, mnw, wkv, xkw):
    M, D = mem.shape

    def body(dmk_ref, dmv_ref, kv_ref, mem_ref, mn_ref, mnw_ref, w_ref, xkw_ref, dw_ref, dxk_ref, dmn_ref, dkv):
        dxk = jnp.zeros((1, HEAD), F32)
        for h in range(HPG):
            sl = slice(h * HEAD, (h + 1) * HEAD)
            k = kv_ref[:, sl]
            r = lax.rsqrt(jnp.mean(k * k, axis=-1, keepdims=True) + RMS_EPS)
            khat = k * r
            dy = dmk_ref[:, sl]
            dyw = dy * xkw_ref[...]
            dkv[:, sl] = (r * (dyw - khat * jnp.mean(dyw * khat, axis=-1, keepdims=True))).astype(BF16)
            dxk = dxk + jnp.sum(dy * khat, axis=0, keepdims=True)
        dxk_ref[...] = dxk
        dkv[:, CH:2 * CH] = dmv_ref[...].astype(BF16)
        dw_ref[...] = lax.dot_general(mn_ref[...], dkv[...], TN_DIMS, preferred_element_type=F32)
        dn = lax.dot_general(dkv[...], w_ref[...], NT_DIMS, preferred_element_type=F32)
        x = mem_ref[...]
        r = lax.rsqrt(jnp.mean(x * x, axis=-1, keepdims=True) + RMS_EPS)
        dmn_ref[...] = jnp.sum(dn * x * r, axis=0, keepdims=True)

    return pl.pallas_call(
        body, name="memkv_bwd",
        out_shape=[SDS((D, 2 * CH), F32), SDS((1, HEAD), F32), SDS((1, D), F32)],
        scratch_shapes=[pltpu.VMEM((M, 2 * CH), BF16)])(dmk, dmv, kv, mem, mn, mnw, wkv, xkw)


def _branch_proj(a_ref, w_ref, y_ref):
    G, _, n = w_ref.shape
    a = a_ref[...]
    for g in range(G):
        y_ref[:, g * n:(g + 1) * n] = jnp.dot(a, w_ref[g], preferred_element_type=F32)


OUT_RC = 16


def _gates(zg_ref, bg_ref, rows, k, D):
    return _sig(zg_ref[rows, k * D:(k + 1) * D].astype(F32) + bg_ref[:, k * D:(k + 1) * D])


def _outproj_fwd(x, z, bg, attn, u3, oc, wao, wco, wxo, wout, fnw):
    S, D = x.shape
    tm = TM

    def body(x_ref, zg_ref, bg_ref, a_ref, u_ref, c_ref, wa, wc, wx, wo, fnw_ref, h1_ref, hn_ref, ya, yc, yx, mg):
        _branch_proj(a_ref, wa, ya)
        _branch_proj(u_ref, wc, yc)
        _branch_proj(c_ref, wx, yx)
        for r0 in range(0, tm, OUT_RC):
            rows = pl.ds(r0, OUT_RC)
            mg[rows, :] = (_gates(zg_ref, bg_ref, rows, 0, D) * ya[rows, :] + _gates(zg_ref, bg_ref, rows, 1, D) * yc[rows, :]
                           + _gates(zg_ref, bg_ref, rows, 2, D) * yx[rows, :]).astype(BF16)
        ya[...] = jnp.dot(mg[...], wo[...], preferred_element_type=F32)
        for r0 in range(0, tm, OUT_RC):
            rows = pl.ds(r0, OUT_RC)
            h1 = x_ref[rows, :] + ya[rows, :]
            h1_ref[rows, :] = h1
            hn_ref[rows, :] = (h1 * lax.rsqrt(jnp.mean(h1 * h1, axis=-1, keepdims=True) + RMS_EPS) * fnw_ref[...]).astype(BF16)

    row = lambda w: pl.BlockSpec((tm, w), lambda i: (i, 0))
    return pl.pallas_call(
        body, name="outproj_fwd", grid=(S // tm,),
        in_specs=[row(D), pl.BlockSpec((tm, 3 * D), lambda i: (i, O_G // (3 * D))), _full((1, 3 * D)), row(CH), row(CH), row(CH),
                  _full(wao.shape), _full(wco.shape), _full(wxo.shape), _full((D, D)), _full((1, D))],
        out_specs=[row(D), row(D)], out_shape=[SDS((S, D), F32), SDS((S, D), BF16)],
        scratch_shapes=[pltpu.VMEM((tm, D), F32)] * 3 + [pltpu.VMEM((tm, D), BF16)])(x, z, bg, attn, u3, oc, wao, wco, wxo, wout, fnw)


def _outproj_bwd(dh1, z, bg, attn, u3, oc, wao, wco, wxo, wout, n_in):
    S, D = dh1.shape
    tm = 256
    nt = S // tm
    G, _, n = wao.shape

    def body(dh_ref, zg_ref, bg_ref, a_ref, u_ref, c_ref, wa, wc, wx, wo,
             dz_ref, da_ref, du_ref, dc_ref, dbg_ref, dwo_ref, dwa_ref, dwc_ref, dwx_ref,
             ya, yc, yx, dm, dy, mg, bacc, wacc):
        i = pl.program_id(0)

        @pl.when(i == 0)
        def _():
            bacc[...] = jnp.zeros_like(bacc)
            wacc[...] = jnp.zeros_like(wacc)
            dwo_ref[...] = jnp.zeros_like(dwo_ref)

        _branch_proj(a_ref, wa, ya)
        _branch_proj(u_ref, wc, yc)
        _branch_proj(c_ref, wx, yx)
        dhb = dh_ref[...].astype(BF16)
        dm[...] = lax.dot_general(dhb, wo[...], NT_DIMS, preferred_element_type=F32)
        for r0 in range(0, tm, OUT_RC):
            rows = pl.ds(r0, OUT_RC)
            dmv = dm[rows, :]
            merged = jnp.zeros((OUT_RC, D), F32)
            for k, y in enumerate((ya, yc, yx)):
                gk = _gates(zg_ref, bg_ref, rows, k, D)
                yk = y[rows, :]
                merged = merged + gk * yk
                dzg = dmv * yk * gk * (1.0 - gk)
                dz_ref[rows, k * D:(k + 1) * D] = dzg.astype(BF16)
                bacc[:, k * D:(k + 1) * D] += _fold8(dzg)
                dy[k, rows, :] = (dmv * gk).astype(BF16)
            mg[rows, :] = merged.astype(BF16)
        dwo_ref[...] += lax.dot_general(mg[...], dhb, TN_DIMS, preferred_element_type=F32)
        for k, (b_ref, w_ref, db_ref) in enumerate(((a_ref, wa, da_ref), (u_ref, wc, du_ref), (c_ref, wx, dc_ref))):
            dyk = dy[k]
            acc = jnp.zeros((tm, CH), F32)
            for g in range(G):
                acc = acc + lax.dot_general(dyk[:, g * n:(g + 1) * n], w_ref[g], NT_DIMS, preferred_element_type=F32)
            db_ref[...] = acc
            wacc[k] += lax.dot_general(b_ref[...], dyk, TN_DIMS, preferred_element_type=F32)

        @pl.when(i == nt - 1)
        def _():
            dbg_ref[...] = jnp.sum(bacc[...], axis=0, keepdims=True)
            for k, dw_ref in enumerate((dwa_ref, dwc_ref, dwx_ref)):
                for g in range(G):
                    dw_ref[g] = wacc[k, :, g * n:(g + 1) * n]

    row = lambda w: pl.BlockSpec((tm, w), lambda i: (i, 0))
    return pl.pallas_call(
        body, name="outproj_bwd", grid=(nt,),
        in_specs=[row(D), pl.BlockSpec((tm, 3 * D), lambda i: (i, O_G // (3 * D))), _full((1, 3 * D)), row(CH), row(CH), row(CH),
                  _full(wao.shape), _full(wco.shape), _full(wxo.shape), _full((D, D))],
        out_specs=[pl.BlockSpec((tm, 3 * D), lambda i: (i, O_G // (3 * D))), row(CH), row(CH), row(CH), _full((1, 3 * D)),
                   _full((D, D))] + [_full(wao.shape)] * 3,
        out_shape=[SDS((S, n_in), BF16)] + [SDS((S, CH), F32)] * 3 + [SDS((1, 3 * D), F32), SDS((D, D), F32)]
        + [SDS(wao.shape, F32)] * 3,
        scratch_shapes=[pltpu.VMEM((tm, D), F32)] * 4 + [pltpu.VMEM((3, tm, D), BF16), pltpu.VMEM((tm, D), BF16),
                                                        pltpu.VMEM((8, 3 * D), F32), pltpu.VMEM((3, CH, D), F32)],
    )(dh1, z, bg, attn, u3, oc, wao, wco, wxo, wout)


FFN_TC = 256
FFN_H = 8


def _ffn_taps(buf, r0):
    xx = buf[pl.ds(r0, FFN_RC + FFN_H), :]
    return xx[FFN_H:], pltpu.roll(xx, 1, 0)[FFN_H:], pltpu.roll(xx, 2, 0)[FFN_H:]


def _ffn_conv(taps, w_ref, b_ref):
    x0, x1, x2 = taps
    return b_ref[...] + x0 * w_ref[2:3, :] + x1 * w_ref[1:2, :] + x2 * w_ref[0:1, :]


def _ffn_act_fwd(up, cw, cb):
    S, F2 = up.shape
    nj = F2 // 2 // FFN_TC
    tm = 2048 if S % 2048 == 0 else TM

    def body(ua, ug, wa, wg, ba, bgt, o_ref, abuf, gbuf):
        i = pl.program_id(1)

        @pl.when(i == 0)
        def _():
            abuf[0:FFN_H, :] = jnp.zeros((FFN_H, FFN_TC), F32)
            gbuf[0:FFN_H, :] = jnp.zeros((FFN_H, FFN_TC), F32)

        @pl.when(i > 0)
        def _():
            abuf[0:FFN_H, :] = abuf[tm:tm + FFN_H, :]
            gbuf[0:FFN_H, :] = gbuf[tm:tm + FFN_H, :]

        abuf[FFN_H:FFN_H + tm, :] = ua[...].astype(F32)
        gbuf[FFN_H:FFN_H + tm, :] = ug[...].astype(F32)
        for r0 in range(0, tm, FFN_RC):
            a = _ffn_conv(_ffn_taps(abuf, r0), wa, ba)
            gt = _ffn_conv(_ffn_taps(gbuf, r0), wg, bgt)
            o_ref[pl.ds(r0, FFN_RC), :] = (gt * _sig(gt) * a).astype(BF16)

    return pl.pallas_call(
        body, name="ffn_act_fwd", grid=(nj, S // tm),
        in_specs=[pl.BlockSpec((tm, FFN_TC), lambda j, i: (i, j)), pl.BlockSpec((tm, FFN_TC), lambda j, i: (i, nj + j)),
                  pl.BlockSpec((FFN_K, FFN_TC), lambda j, i: (0, j)), pl.BlockSpec((FFN_K, FFN_TC), lambda j, i: (0, nj + j)),
                  pl.BlockSpec((1, FFN_TC), lambda j, i: (0, j)), pl.BlockSpec((1, FFN_TC), lambda j, i: (0, nj + j))],
        out_specs=pl.BlockSpec((tm, FFN_TC), lambda j, i: (i, j)), out_shape=SDS((S, F2 // 2), BF16),
        scratch_shapes=[pltpu.VMEM((tm + FFN_H, FFN_TC), F32)] * 2)(up, up, cw, cw, cb, cb)


def _ffn_down_loss(act, wdown, h1, target):
    S, D = h1.shape
    F = act.shape[1]

    def body(a_ref, w_ref, h_ref, t_ref, dy_ref, loss_ref):
        @pl.when(pl.program_id(0) == 0)
        def _():
            loss_ref[...] = jnp.zeros_like(loss_ref)

        err = h_ref[...] + jnp.dot(a_ref[...], w_ref[...], preferred_element_type=F32) - t_ref[...]
        dy_ref[...] = err * (1.0 / D)
        loss_ref[...] += 0.5 * jnp.sum(jnp.mean(err * err, axis=-1, keepdims=True))

    row = lambda w: pl.BlockSpec((TM, w), lambda i: (i, 0))
    return pl.pallas_call(
        body, name="ffn_down_loss", grid=(S // TM,),
        in_specs=[row(F), _full((F, D)), row(D), row(D)],
        out_specs=[row(D), _full((8, 128))], out_shape=[SDS((S, D), F32), SDS((8, 128), F32)])(act, wdown, h1, target)


def _ffn_bwd_a(dy, wdown, up, cw, cb):
    S, D = dy.shape
    F2 = up.shape[1]
    F = F2 // 2
    nj = F // FFN_TC
    tm = 1024 if S % 1024 == 0 else TM

    def body(dy_ref, wd_ref, ua, ug, wa, wg, ba, bgt, da_ref, dg_ref, acca_ref, accg_ref, dwd_ref,
             abuf, gbuf, hala, halg, dact_s, act_s):
        i, j = pl.program_id(0), pl.program_id(1)

        @pl.when((i == 0) & (j == 0))
        def _():
            acca_ref[...] = jnp.zeros_like(acca_ref)
            accg_ref[...] = jnp.zeros_like(accg_ref)
            dwd_ref[...] = jnp.zeros_like(dwd_ref)

        @pl.when(i == 0)
        def _():
            abuf[0:FFN_H, :] = jnp.zeros((FFN_H, FFN_TC), F32)
            gbuf[0:FFN_H, :] = jnp.zeros((FFN_H, FFN_TC), F32)

        @pl.when(i > 0)
        def _():
            abuf[0:FFN_H, :] = hala[j]
            gbuf[0:FFN_H, :] = halg[j]

        abuf[FFN_H:FFN_H + tm, :] = ua[...].astype(F32)
        gbuf[FFN_H:FFN_H + tm, :] = ug[...].astype(F32)
        hala[j] = abuf[tm:tm + FFN_H, :]
        halg[j] = gbuf[tm:tm + FFN_H, :]
        dyb = dy_ref[...].astype(BF16)
        dact_s[...] = lax.dot_general(dyb, wd_ref[...], NT_DIMS, preferred_element_type=F32)
        zero8 = jnp.zeros((8, FFN_TC), F32)
        pa, pg = [zero8] * (FFN_K + 1), [zero8] * (FFN_K + 1)
        for r0 in range(0, tm, FFN_RC):
            rows = pl.ds(r0, FFN_RC)
            ta, tg = _ffn_taps(abuf, r0), _ffn_taps(gbuf, r0)
            a = _ffn_conv(ta, wa, ba)
            gt = _ffn_conv(tg, wg, bgt)
            dact = dact_s[rows, :]
            sg = _sig(gt)
            silu = gt * sg
            act_s[rows, :] = (silu * a).astype(BF16)
            dac = dact * silu
            dgc = dact * a * (sg * (1.0 + gt * (1.0 - sg)))
            da_ref[rows, :] = dac.astype(BF16)
            dg_ref[rows, :] = dgc.astype(BF16)
            for k in range(FFN_K):
                pa[k] = pa[k] + _fold8(dac * ta[FFN_K - 1 - k])
                pg[k] = pg[k] + _fold8(dgc * tg[FFN_K - 1 - k])
            pa[FFN_K] = pa[FFN_K] + _fold8(dac)
            pg[FFN_K] = pg[FFN_K] + _fold8(dgc)
        for k in range(FFN_K + 1):
            acca_ref[j, k:k + 1, :] += jnp.sum(pa[k], axis=0, keepdims=True)
            accg_ref[j, k:k + 1, :] += jnp.sum(pg[k], axis=0, keepdims=True)
        dwd_ref[pl.ds(pl.multiple_of(j * FFN_TC, FFN_TC), FFN_TC), :] += lax.dot_general(
            act_s[...], dyb, TN_DIMS, preferred_element_type=F32)

    return pl.pallas_call(
        body, name="ffn_bwd_a", grid=(S // tm, nj),
        in_specs=[pl.BlockSpec((tm, D), lambda i, j: (i, 0)), pl.BlockSpec((FFN_TC, D), lambda i, j: (j, 0)),
                  pl.BlockSpec((tm, FFN_TC), lambda i, j: (i, j)), pl.BlockSpec((tm, FFN_TC), lambda i, j: (i, nj + j)),
                  pl.BlockSpec((FFN_K, FFN_TC), lambda i, j: (0, j)), pl.BlockSpec((FFN_K, FFN_TC), lambda i, j: (0, nj + j)),
                  pl.BlockSpec((1, FFN_TC), lambda i, j: (0, j)), pl.BlockSpec((1, FFN_TC), lambda i, j: (0, nj + j))],
        out_specs=[pl.BlockSpec((tm, FFN_TC), lambda i, j: (i, j))] * 2 + [_full((nj, 8, FFN_TC))] * 2 + [_full((F, D))],
        out_shape=[SDS((S, F), BF16)] * 2 + [SDS((nj, 8, FFN_TC), F32)] * 2 + [SDS((F, D), F32)],
        scratch_shapes=[pltpu.VMEM((tm + FFN_H, FFN_TC), F32)] * 2 + [pltpu.VMEM((nj, FFN_H, FFN_TC), F32)] * 2
        + [pltpu.VMEM((tm, FFN_TC), F32), pltpu.VMEM((tm, FFN_TC), BF16)],
    )(dy, wdown, up, up, cw, cw, cb, cb)


def _ffn_bwd_b(dca, dcg, cw):
    S, F = dca.shape
    nj = F // FFN_TC
    tm = 2048 if S % 2048 == 0 else TM
    nt = S // tm
    span = FFN_RC + FFN_H

    def body(a_ref, g_ref, w_ref, o_ref, ybuf):
        j, i = pl.program_id(0), pl.program_id(1)

        @pl.when(i == 0)
        def _():
            ybuf[tm:tm + FFN_H, :] = jnp.zeros((FFN_H, FFN_TC), F32)

        @pl.when(i > 0)
        def _():
            ybuf[tm:tm + FFN_H, :] = ybuf[0:FFN_H, :]

        ybuf[0:tm, :] = jnp.where(j < nj, a_ref[...], g_ref[...]).astype(F32)
        for r0 in range(0, tm, FFN_RC):
            yy = ybuf[pl.ds(r0, span), :]
            acc = yy[:FFN_RC] * w_ref[2:3, :] + pltpu.roll(yy, span - 1, 0)[:FFN_RC] * w_ref[1:2, :] \
                + pltpu.roll(yy, span - 2, 0)[:FFN_RC] * w_ref[0:1, :]
            o_ref[pl.ds(r0, FFN_RC), :] = acc.astype(BF16)

    return pl.pallas_call(
        body, name="ffn_bwd_b", grid=(2 * nj, nt),
        in_specs=[pl.BlockSpec((tm, FFN_TC), lambda j, i: (nt - 1 - i, jnp.minimum(j, nj - 1))),
                  pl.BlockSpec((tm, FFN_TC), lambda j, i: (nt - 1 - i, jnp.maximum(j - nj, 0))),
                  pl.BlockSpec((FFN_K, FFN_TC), lambda j, i: (0, j))],
        out_specs=pl.BlockSpec((tm, FFN_TC), lambda j, i: (nt - 1 - i, j)), out_shape=SDS((S, 2 * F), BF16),
        scratch_shapes=[pltpu.VMEM((tm + FFN_H, FFN_TC), F32)])(dca, dcg, cw)


def _adamw_update(w_ref, g_ref, m_ref, v_ref, d_ref, nm_ref, nv_ref):
    gv = g_ref[...]
    m2 = ADAM_B1 * m_ref[...] + (1.0 - ADAM_B1) * gv
    v2 = ADAM_B2 * v_ref[...] + (1.0 - ADAM_B2) * jnp.square(gv)
    m_hat = m2 / (1.0 - ADAM_B1 ** ADAM_STEP)
    v_hat = v2 / (1.0 - ADAM_B2 ** ADAM_STEP)
    d_ref[...] = -ADAM_LR * (m_hat / (jnp.sqrt(v_hat) + ADAM_EPS) + ADAM_WD * w_ref[...])
    nm_ref[...] = m2
    nv_ref[...] = v2


def _adamw_small(ws, gs, ms, vs):
    n = len(ws)

    def body(*refs):
        for i in range(n):
            _adamw_update(*[refs[k * n + i] for k in range(7)])

    shapes = [SDS(w.shape, F32) for w in ws]
    res = pl.pallas_call(body, name="adamw_small", out_shape=shapes * 3)(*ws, *gs, *ms, *vs)
    return res[:n], res[n:2 * n], res[2 * n:]


def _adamw(w, g, m, v, name):
    R, C = w.shape
    tr = _row_tile(R, max(8, (2 ** 19) // (4 * C) // 8 * 8))

    def body(w_ref, g_ref, m_ref, v_ref, d_ref, nm_ref, nv_ref):
        _adamw_update(w_ref, g_ref, m_ref, v_ref, d_ref, nm_ref, nv_ref)

    blk = pl.BlockSpec((tr, C), lambda i: (i, 0))
    return pl.pallas_call(
        body, name=name, grid=(R // tr,), in_specs=[blk] * 4, out_specs=[blk] * 3,
        out_shape=[SDS((R, C), F32)] * 3)(w, g, m, v)


HBM_SPEC = pl.BlockSpec(memory_space=pltpu.HBM)
SEM_SPEC = pl.BlockSpec(memory_space=pltpu.SEMAPHORE)
DATAFLOW_EFFECT = pltpu.SideEffectType.DATAFLOW_SIDE_EFFECTING


def _position():
    return lax.axis_index("x"), lax.axis_index("y"), lax.axis_index("c")


def _other_chips(x, y):
    return [(1 - x, y), (x, 1 - y), (1 - x, 1 - y)]


def _all_gather_xy(arrs, name):
    n = len(arrs)
    hbm = pl.BlockSpec(memory_space=pl.ANY)

    def body(*refs):
        ins, outs = refs[:n], refs[n:2 * n]
        send_sems, recv_sems = refs[2 * n:]
        x, y, c = _position()
        me = 2 * x + y
        chips = _other_chips(x, y)

        def rcopy(i, k, src, dst, to):
            return pltpu.make_async_remote_copy(src_ref=src, dst_ref=dst, send_sem=send_sems.at[i, k], recv_sem=recv_sems.at[i, k],
                                                device_id=to, device_id_type=MESH)

        sends = []
        for i in range(n):
            own = rcopy(i, 6, ins[i], outs[i].at[me], (x, y, 1 - c))
            own.start()
            sends.append(own)
        for i in range(n):
            for j, (px, py) in enumerate(chips):
                cp = rcopy(i, j, ins[i].at[c], outs[i].at[me, c], (px, py, c))
                cp.start()
                sends.append(cp)
        for i in range(n):
            for j, (px, py) in enumerate(chips):
                got = outs[i].at[2 * px + py, c]
                rcopy(i, j, ins[i].at[c], got, (x, y, c)).wait_recv()
                fwd = rcopy(i, 3 + j, got, got, (x, y, 1 - c))
                fwd.start()
                sends.append(fwd)
        for i in range(n):
            for j, (px, py) in enumerate(chips):
                theirs = outs[i].at[2 * px + py, 1 - c]
                rcopy(i, 3 + j, theirs, theirs, (x, y, c)).wait_recv()
        for i in range(n):
            rcopy(i, 6, ins[i], outs[i].at[me], (x, y, c)).wait_recv()
        for cp in sends:
            cp.wait_send()

    return pl.pallas_call(
        body, name=name, in_specs=[hbm] * n, out_specs=[hbm] * n,
        out_shape=[SDS((4,) + a.shape, a.dtype) for a in arrs],
        scratch_shapes=[pltpu.SemaphoreType.DMA((n, 7)), pltpu.SemaphoreType.DMA((n, 7))])(*arrs)


def _ag_ici_start(arrs, name):
    n = len(arrs)

    def body(*refs):
        ins, lands = refs[:n], refs[n:2 * n]
        send_sems, recv_sems = refs[2 * n:2 * n + 2]
        token = refs[-1]
        x, y, c = _position()
        for i in range(n):
            for j, (px, py) in enumerate(_other_chips(x, y)):
                pltpu.make_async_remote_copy(src_ref=ins[i].at[c], dst_ref=lands[i].at[2 * x + y, c], send_sem=send_sems.at[3 * i + j],
                                             recv_sem=recv_sems.at[3 * i + j], device_id=(px, py, c), device_id_type=MESH).start()
        token[...] = jnp.zeros_like(token)

    lands = [lax.empty((4,) + a.shape, a.dtype) for a in arrs]
    res = pl.pallas_call(
        body, name=name,
        out_shape=[pltpu.SemaphoreType.DMA((3 * n,)), pltpu.SemaphoreType.DMA((3 * n,))]
        + [pltpu.HBM(a.shape, a.dtype) for a in arrs] + [pltpu.HBM(l.shape, l.dtype) for l in lands] + [SDS((8, 128), F32)],
        in_specs=[HBM_SPEC] * (2 * n), out_specs=[SEM_SPEC, SEM_SPEC] + [HBM_SPEC] * (2 * n) + [pl.BlockSpec(memory_space=pltpu.VMEM)],
        input_output_aliases={i: 2 + i for i in range(2 * n)},
        compiler_params=pltpu.CompilerParams(has_side_effects=DATAFLOW_EFFECT),
    )(*[pltpu.with_memory_space_constraint(a, pltpu.HBM) for a in list(arrs) + lands])
    return res[0], res[1], list(res[2:2 + n]), list(res[2 + n:2 + 2 * n]), res[-1]


def _ag_ici_wait(send_sems, recv_sems, ins, lands, after, name):
    n = len(ins)

    def body(*refs):
        ins_r, lands_r = refs[:n], refs[n:2 * n]
        send_r, recv_r = refs[2 * n:2 * n + 2]
        x, y, c = _position()
        for i in range(n):
            for j, (px, py) in enumerate(_other_chips(x, y)):
                cp = pltpu.make_async_remote_copy(src_ref=ins_r[i].at[c], dst_ref=lands_r[i].at[2 * px + py, c], send_sem=send_r.at[3 * i + j],
                                                  recv_sem=recv_r.at[3 * i + j], device_id=(px, py, c), device_id_type=MESH)
                cp.wait_send()
                cp.wait_recv()

    res = pl.pallas_call(
        body, name=name,
        out_shape=[pltpu.HBM(a.shape, a.dtype) for a in list(ins) + list(lands)],
        in_specs=[HBM_SPEC] * (2 * n) + [SEM_SPEC, SEM_SPEC, pl.BlockSpec(memory_space=pl.ANY)], out_specs=[HBM_SPEC] * (2 * n),
        input_output_aliases={i: i for i in range(2 * n)},
        compiler_params=pltpu.CompilerParams(has_side_effects=DATAFLOW_EFFECT),
    )(*ins, *lands, send_sems, recv_sems, after)
    return list(res[:n]), list(res[n:])


def _ag_finish(arrs, lands, name):
    n = len(arrs)
    hbm = pl.BlockSpec(memory_space=pl.ANY)

    def body(*refs):
        ins, landed, outs = refs[:n], refs[n:2 * n], refs[2 * n:3 * n]
        send_sems, recv_sems = refs[3 * n:]
        x, y, c = _position()
        chips = _other_chips(x, y)
        sends = []
        for i in range(n):
            own = pltpu.make_async_remote_copy(src_ref=ins[i], dst_ref=outs[i].at[2 * x + y], send_sem=send_sems.at[i, 3],
                                               recv_sem=recv_sems.at[i, 3], device_id=(x, y, 1 - c), device_id_type=MESH)
            own.start()
            sends.append(own)
        for i in range(n):
            for j, (px, py) in enumerate(chips):
                fwd = pltpu.make_async_remote_copy(src_ref=landed[i].at[2 * px + py, c], dst_ref=outs[i].at[2 * px + py, c],
                                                   send_sem=send_sems.at[i, j], recv_sem=recv_sems.at[i, j],
                                                   device_id=(x, y, 1 - c), device_id_type=MESH)
                fwd.start()
                sends.append(fwd)
        for i in range(n):
            for j, (px, py) in enumerate(chips):
                theirs = outs[i].at[2 * px + py, 1 - c]
                pltpu.make_async_remote_copy(src_ref=theirs, dst_ref=theirs, send_sem=send_sems.at[i, j], recv_sem=recv_sems.at[i, j],
                                             device_id=(x, y, c), device_id_type=MESH).wait_recv()
        for i in range(n):
            pltpu.make_async_remote_copy(src_ref=ins[i], dst_ref=outs[i].at[2 * x + y], send_sem=send_sems.at[i, 3],
                                         recv_sem=recv_sems.at[i, 3], device_id=(x, y, c), device_id_type=MESH).wait_recv()
        for cp in sends:
            cp.wait_send()

    return pl.pallas_call(
        body, name=name, in_specs=[hbm] * (2 * n), out_specs=[hbm] * n,
        out_shape=[SDS(l.shape, l.dtype) for l in lands],
        input_output_aliases={n + i: i for i in range(n)},
        scratch_shapes=[pltpu.SemaphoreType.DMA((n, 4)), pltpu.SemaphoreType.DMA((n, 4))])(*arrs, *lands)


def _swap_halves(gs, name):
    n = len(gs)
    hbm = pl.BlockSpec(memory_space=pl.ANY)

    def body(*refs):
        ins, outs = refs[:n], refs[n:2 * n]
        send_sems, recv_sems = refs[2 * n:]
        x, y, c = _position()
        cps = []
        for i in range(n):
            for p in range(4):
                cp = pltpu.make_async_remote_copy(src_ref=ins[i].at[p, 1 - c], dst_ref=outs[i].at[p], send_sem=send_sems.at[4 * i + p],
                                                  recv_sem=recv_sems.at[4 * i + p], device_id=(x, y, 1 - c), device_id_type=MESH)
                cp.start()
                cps.append(cp)
        for cp in cps:
            cp.wait()

    return pl.pallas_call(
        body, name=name, in_specs=[hbm] * n, out_specs=[hbm] * n,
        out_shape=[SDS((4,) + g.shape[2:], g.dtype) for g in gs],
        scratch_shapes=[pltpu.SemaphoreType.DMA((4 * n,)), pltpu.SemaphoreType.DMA((4 * n,))])(*gs)


def _exchange_start(s1s, name):
    n = len(s1s)

    def body(*refs):
        srcs, lands = refs[:n], refs[n:2 * n]
        send_sems, recv_sems = refs[2 * n:2 * n + 2]
        token = refs[-1]
        x, y, c = _position()
        for i in range(n):
            for j, (px, py) in enumerate(_other_chips(x, y)):
                pltpu.make_async_remote_copy(src_ref=srcs[i].at[2 * px + py], dst_ref=lands[i].at[j], send_sem=send_sems.at[3 * i + j],
                                             recv_sem=recv_sems.at[3 * i + j], device_id=(px, py, c), device_id_type=MESH).start()
        token[...] = jnp.zeros_like(token)

    lands = [lax.empty((3,) + s.shape[1:], F32) for s in s1s]
    res = pl.pallas_call(
        body, name=name,
        out_shape=[pltpu.SemaphoreType.DMA((3 * n,)), pltpu.SemaphoreType.DMA((3 * n,))]
        + [pltpu.HBM(a.shape, F32) for a in list(s1s) + lands] + [SDS((8, 128), F32)],
        in_specs=[HBM_SPEC] * (2 * n), out_specs=[SEM_SPEC, SEM_SPEC] + [HBM_SPEC] * (2 * n) + [pl.BlockSpec(memory_space=pltpu.VMEM)],
        input_output_aliases={i: 2 + i for i in range(2 * n)},
        compiler_params=pltpu.CompilerParams(has_side_effects=DATAFLOW_EFFECT),
    )(*[pltpu.with_memory_space_constraint(a, pltpu.HBM) for a in list(s1s) + lands])
    return res[0], res[1], list(res[2:2 + n]), list(res[2 + n:2 + 2 * n]), res[-1]


def _exchange_wait(send_sems, recv_sems, s1s, lands, after, name):
    n = len(s1s)

    def body(*refs):
        srcs, lands_r = refs[:n], refs[n:2 * n]
        send_r, recv_r = refs[2 * n:2 * n + 2]
        x, y, c = _position()
        for i in range(n):
            for j, (px, py) in enumerate(_other_chips(x, y)):
                cp = pltpu.make_async_remote_copy(src_ref=srcs[i].at[2 * px + py], dst_ref=lands_r[i].at[j], send_sem=send_r.at[3 * i + j],
                                                  recv_sem=recv_r.at[3 * i + j], device_id=(px, py, c), device_id_type=MESH)
                cp.wait_send()
                cp.wait_recv()

    res = pl.pallas_call(
        body, name=name, out_shape=[pltpu.HBM(a.shape, F32) for a in list(s1s) + list(lands)],
        in_specs=[HBM_SPEC] * (2 * n) + [SEM_SPEC, SEM_SPEC, pl.BlockSpec(memory_space=pl.ANY)], out_specs=[HBM_SPEC] * (2 * n),
        input_output_aliases={i: i for i in range(2 * n)},
        compiler_params=pltpu.CompilerParams(has_side_effects=DATAFLOW_EFFECT),
    )(*s1s, *lands, send_sems, recv_sems, after)
    return list(res[:n]), list(res[n:])


def _join_halves(f2s, name):
    n = len(f2s)
    hbm = pl.BlockSpec(memory_space=pl.ANY)

    def body(*refs):
        ins, outs = refs[:n], refs[n:2 * n]
        send_sems, recv_sems = refs[2 * n:]
        x, y, c = _position()
        cps = []
        for i in range(n):
            cp = pltpu.make_async_remote_copy(src_ref=ins[i].at[c], dst_ref=outs[i].at[c], send_sem=send_sems.at[i],
                                              recv_sem=recv_sems.at[i], device_id=(x, y, 1 - c), device_id_type=MESH)
            cp.start()
            cps.append(cp)
        for i in range(n):
            pltpu.make_async_remote_copy(src_ref=ins[i].at[1 - c], dst_ref=outs[i].at[1 - c], send_sem=send_sems.at[i],
                                         recv_sem=recv_sems.at[i], device_id=(x, y, 1 - c), device_id_type=MESH).wait_recv()
        for cp in cps:
            cp.wait_send()

    return pl.pallas_call(
        body, name=name, in_specs=[hbm] * n, out_specs=[hbm] * n, out_shape=[SDS(f.shape, f.dtype) for f in f2s],
        input_output_aliases={i: i for i in range(n)},
        scratch_shapes=[pltpu.SemaphoreType.DMA((n,)), pltpu.SemaphoreType.DMA((n,))])(*f2s)


def _sum_tile(rows, cols):
    return _row_tile(rows, max(8, (2 ** 18 // cols) // 8 * 8))


def _add_pair(g, r1, c, name):
    _, R, C = r1.shape
    tr = _sum_tile(R, C)

    def body(c_ref, a_ref, b_ref, o_ref):
        del c_ref
        o_ref[...] = a_ref[...] + b_ref[...]

    blk = pl.BlockSpec((None, tr, C), lambda p, i, cr: (p, i, 0))
    return pl.pallas_call(
        body, name=name,
        grid_spec=pltpu.PrefetchScalarGridSpec(
            num_scalar_prefetch=1, grid=(4, R // tr),
            in_specs=[pl.BlockSpec((None, None, tr, C), lambda p, i, cr: (p, cr[0], i, 0)), blk], out_specs=blk),
        out_shape=SDS((4, R, C), F32))(c, g, r1)


def _add_four(s1, r2, me_c, name):
    _, R, C = s1.shape
    tr = _sum_tile(R, C)

    def body(m_ref, a_ref, b_ref, o_ref):
        del m_ref
        o_ref[...] = ((a_ref[...] + b_ref[0]) + b_ref[1]) + b_ref[2]

    return pl.pallas_call(
        body, name=name,
        grid_spec=pltpu.PrefetchScalarGridSpec(
            num_scalar_prefetch=1, grid=(R // tr,),
            in_specs=[pl.BlockSpec((None, tr, C), lambda i, mr: (mr[0], i, 0)), pl.BlockSpec((3, tr, C), lambda i, mr: (0, i, 0))],
            out_specs=pl.BlockSpec((None, tr, C), lambda i, mr: (mr[1], i, 0))),
        out_shape=SDS((2, R, C), F32))(me_c, s1, r2)


def _all_reduce_small(vs):
    n = len(vs)

    def body(*refs):
        ins, outs, bufs = refs[:n], refs[n:2 * n], refs[2 * n:3 * n]
        send_sems, recv_sems = refs[3 * n:]
        x, y, c = _position()
        me = 4 * x + 2 * y + c
        for i in range(n):
            bufs[i][me] = ins[i][...]
        cps = []
        for k in range(1, 8):
            to = (1 - x if k & 4 else x, 1 - y if k & 2 else y, 1 - c if k & 1 else c)
            for i in range(n):
                cp = pltpu.make_async_remote_copy(src_ref=bufs[i].at[me], dst_ref=bufs[i].at[me], send_sem=send_sems.at[7 * i + k - 1],
                                                  recv_sem=recv_sems.at[7 * i + k - 1], device_id=to, device_id_type=MESH)
                cp.start()
                cps.append(cp)
        for cp in cps:
            cp.wait_send()
        for k in range(1, 8):
            src = 4 * (1 - x if k & 4 else x) + 2 * (1 - y if k & 2 else y) + (1 - c if k & 1 else c)
            for i in range(n):
                pltpu.make_async_remote_copy(src_ref=bufs[i].at[src], dst_ref=bufs[i].at[src], send_sem=send_sems.at[7 * i + k - 1],
                                             recv_sem=recv_sems.at[7 * i + k - 1], device_id=(x, y, c), device_id_type=MESH).wait_recv()
        for i in range(n):
            acc = bufs[i][0]
            for k in range(1, 8):
                acc = acc + bufs[i][k]
            outs[i][...] = acc

    vm = pl.BlockSpec(memory_space=pltpu.VMEM)
    return pl.pallas_call(
        body, name="all_reduce_small", in_specs=[vm] * n, out_specs=[vm] * n, out_shape=[SDS(v.shape, F32) for v in vs],
        scratch_shapes=[pltpu.VMEM((8,) + v.shape, F32) for v in vs]
        + [pltpu.SemaphoreType.DMA((7 * n,)), pltpu.SemaphoreType.DMA((7 * n,))])(*vs)


def _reduce_begin(grads, tag):
    _, _, c = _position()
    cs = jnp.reshape(c, (1,)).astype(jnp.int32)
    g4 = [g.reshape(4, 2, g.shape[1] // 2, g.shape[2]) for g in grads]
    r1 = _swap_halves(g4, "rs_swap_" + tag)
    s1 = [_add_pair(g, r, cs, f"rs_add_pair_{tag}{i}") for i, (g, r) in enumerate(zip(g4, r1))]
    send_sems, recv_sems, s1, lands, token = _exchange_start(s1, "rs_exchange_start_" + tag)
    return (send_sems, recv_sems, s1, lands), token


def _reduce_end(state, after, tag):
    x, y, c = _position()
    send_sems, recv_sems, s1, lands = state
    s1, lands = _exchange_wait(send_sems, recv_sems, s1, lands, after, "rs_exchange_wait_" + tag)
    me_c = jnp.stack([2 * x + y, c]).astype(jnp.int32)
    f2 = [_add_four(s, l, me_c, f"rs_add_four_{tag}{i}") for i, (s, l) in enumerate(zip(s1, lands))]
    return [f.reshape(2 * f.shape[1], f.shape[2]) for f in _join_halves(f2, "rs_join_" + tag)]


def _halves(a):
    return a.reshape((2, a.shape[0] // 2) + a.shape[1:])


def _after(a, token):
    return a + token[0, 0]


def _in_proj_parts(xn, w, parts, z, name):
    M, K = xn.shape
    G, _, n = w.shape
    P = parts.shape[0]

    def body(p_ref, a_ref, b_ref, *rest):
        del p_ref
        rest[-1][...] = jnp.dot(a_ref[...], b_ref[...], preferred_element_type=F32).astype(BF16)

    in_specs = [pl.BlockSpec((TM, K), lambda g, i, pr: (i, 0)),
                pl.BlockSpec((None, K, n), (lambda g, i, pr: (pr[g], 0, 0)) if G > 1 else (lambda g, i, pr: (0, 0, 0)))]
    args = [parts, xn, w]
    if z is not None:
        in_specs.append(pl.BlockSpec(memory_space=pl.ANY))
        args.append(z)
    return pl.pallas_call(
        body, name=name,
        grid_spec=pltpu.PrefetchScalarGridSpec(
            num_scalar_prefetch=1, grid=(P, M // TM), in_specs=in_specs,
            out_specs=pl.BlockSpec((TM, n), lambda g, i, pr: (i, pr[g]))),
        out_shape=SDS((M, 4 * n), BF16), input_output_aliases={3: 0} if z is not None else {})(*args)


def _local_step(x, mem, target, sp, ex):
    S, D = x.shape
    band, buckets = _bias_static()
    buckets = jnp.asarray(buckets)

    tok = ex.start_first()
    xn = _rmsnorm(x, _after(sp["attn_norm_w"], tok), "rms_in")
    own, me, others = ex.own_w_in()
    z = _in_proj_parts(xn, own, me, None, "in_proj_own")
    w_in, gathered_small = ex.first_weights(after=z)
    sp = {**sp, **gathered_small}
    n_in = 4 * w_in.shape[2]
    bw = {"w_in": w_in}
    z = _in_proj_parts(xn, w_in, others + ex.start_rest()[0, 0].astype(jnp.int32), z, "in_proj_rest")
    qh, kh, vh = _qkv_prep(z, sp["q_norm_w"], sp["k_norm_w"])
    tab = sp["rel_bias_table"].T.reshape(N_GROUPS, HPG, N_BUCKETS)
    bias = _bias_fwd(jnp.pad(tab, ((0, 0), (0, 8 - HPG), (0, 0))), buckets)
    biasm = jnp.where(jnp.asarray(band)[None, None], bias[:, :HPG].reshape(N_GROUPS, HPG, NQ, 2 * NQ), NEG)
    os_, lses = [], []
    for g, (_, dil) in enumerate(ATTN_GROUPS):
        o_g, lse_g = _attn_fwd(qh, kh, vh, biasm[g], g, dil)
        os_.append(o_g)
        lses.append(lse_g)
    attn = _merge_fwd(os_, lses)
    u1, u3 = _conv_fwd(z, sp["conv_dw_w"], sp["conv_dw_b"], sp["conv_ln_w"], sp["conv_ln_b"])
    bw.update(ex.rest_weights(after=attn))
    F2 = 4 * bw["w_up"].shape[2]
    mn, kv, mk, mv = _memkv_fwd(mem, sp["mem_norm_w"], bw["w_mem_kv"], sp["xk_norm_w"])
    oc = _cross_fwd(z, sp["xq_norm_w"], mk, mv)
    h1, hn = _outproj_fwd(x, z, sp["b_gate"], attn, u3, oc, bw["w_attn_o"], bw["w_conv_o"], bw["w_cross_o"], bw["w_out"],
                          sp["ffn_norm_w"])
    up = _mm_nn(hn, bw["w_up"], BF16, "ffn_up")
    act = _ffn_act_fwd(up, sp["ffn_conv_w"], sp["ffn_conv_b"])
    dy, loss_tile = _ffn_down_loss(act, bw["w_down"], h1, target)

    gs, gb = {}, {}
    dca, dcg, acca, accg, gb["w_down"] = _ffn_bwd_a(dy, bw["w_down"], up, sp["ffn_conv_w"], sp["ffn_conv_b"])
    tap = lambda acc: jnp.transpose(acc, (1, 0, 2)).reshape(8, F2 // 2)
    ta, tg = tap(acca), tap(accg)
    gs["ffn_conv_w"] = jnp.concatenate([ta[:FFN_K], tg[:FFN_K]], axis=1)
    gs["ffn_conv_b"] = jnp.concatenate([ta[FFN_K:FFN_K + 1], tg[FFN_K:FFN_K + 1]], axis=1)
    dup = _ffn_bwd_b(dca, dcg, sp["ffn_conv_w"])
    gb["w_up"] = _mm_tn(hn, dup, 4, "dw_up")
    tok = ex.reduce_begin("a", ("w_down", "w_up"), gb)
    dh1, gs["ffn_norm_w"] = _norm_in_bwd(dup, bw["w_up"], h1, _after(sp["ffn_norm_w"], tok), dy, "ffn_in_bwd")
    dz, dattn, du3, doc, gs["b_gate"], gb["w_out"], gb["w_attn_o"], gb["w_conv_o"], gb["w_cross_o"] = _outproj_bwd(
        dh1, z, sp["b_gate"], attn, u3, oc, bw["w_attn_o"], bw["w_conv_o"], bw["w_cross_o"], bw["w_out"], n_in)
    dz, dmk, dmv, gs["xq_norm_w"] = _cross_bwd(dz, doc, z, sp["xq_norm_w"], mk, mv)
    gb["w_mem_kv"], gs["xk_norm_w"], gs["mem_norm_w"] = _memkv_bwd(
        dmk, dmv, kv, mem, mn, sp["mem_norm_w"], bw["w_mem_kv"], sp["xk_norm_w"])
    ex.reduce_end("a", after=gs["mem_norm_w"])
    tok = ex.reduce_begin("b", ("w_out", "w_attn_o", "w_conv_o", "w_cross_o", "w_mem_kv"), gb)
    du1, cacc = _conv_bwd_a(du3, u1, z, _after(sp["conv_ln_w"], tok), sp["conv_ln_b"])
    gs["conv_dw_w"], gs["conv_dw_b"] = cacc[:CONV_K], cacc[32:33]
    gs["conv_ln_w"], gs["conv_ln_b"] = cacc[33:34], cacc[34:35]
    dz = _conv_bwd_b(dz, du1, z, sp["conv_dw_w"])
    wg, dah, dhb = _merge_bwd(dattn, os_, lses)
    dqs, dks, dvs, dsbs = [], [], [], []
    for g, (_, dil) in enumerate(ATTN_GROUPS):
        dq_g, dk_g, dv_g, dsb_g = _attn_bwd(qh, kh, vh, biasm[g], dah, wg[g], dhb, lses[g], g, dil)
        dqs.append(dq_g)
        dks.append(dk_g)
        dvs.append(dv_g)
        dsbs.append(dsb_g.reshape(HPG, NQ * 2 * NQ))
    dtab = _bias_bwd(jnp.pad(jnp.stack(dsbs), ((0, 0), (0, 8 - HPG), (0, 0))), buckets)
    gs["rel_bias_table"] = dtab[:, :HPG].reshape(N_GROUPS * HPG, N_BUCKETS).T
    dz, gs["q_norm_w"], gs["k_norm_w"] = _qkv_bwd(dz, z, dqs, dks, dvs, sp["q_norm_w"], sp["k_norm_w"])
    ex.reduce_end("b", after=gs["q_norm_w"])
    gb["w_in"] = _mm_tn(xn, dz, 4, "dw_in")
    tok = ex.reduce_begin("c", ("w_in",), gb)
    dx, gs["attn_norm_w"] = _norm_in_bwd(dz, bw["w_in"], x, _after(sp["attn_norm_w"], tok), dh1, "in_bwd")
    ex.reduce_end("c", after=gs["attn_norm_w"])
    return loss_tile, dx, gs, gb


SMALL = ("rel_bias_table", "attn_norm_w", "b_gate", "q_norm_w", "k_norm_w", "conv_dw_w", "conv_dw_b", "conv_ln_w", "conv_ln_b",
         "mem_norm_w", "xq_norm_w", "xk_norm_w", "ffn_norm_w", "ffn_conv_w", "ffn_conv_b")
SMALL_SHARDED = ("conv_dw_w", "ffn_conv_w")
BIG_COL = ("w_in", "w_attn_o", "w_conv_o", "w_cross_o", "w_up")
BIG_ROW = ("w_mem_kv", "w_out", "w_down")
BIG = BIG_COL + BIG_ROW
WEIGHTS = ("rel_bias_table", "attn_norm_w", "w_in", "b_gate", "q_norm_w", "k_norm_w", "w_attn_o", "conv_dw_w", "conv_dw_b",
           "conv_ln_w", "conv_ln_b", "w_conv_o", "mem_norm_w", "w_mem_kv", "xq_norm_w", "xk_norm_w", "w_cross_o", "w_out",
           "ffn_norm_w", "w_up", "ffn_conv_w", "ffn_conv_b", "w_down")


class _Exchanges:
    REST = tuple(k for k in BIG if k != "w_in")

    def __init__(self, w):
        self.w = w
        self.pending = {}
        self.reduced = {}

    def _whole(self, k, ga):
        ga = ga.reshape((4,) + self.w[k].shape)
        return ga if k in BIG_COL else ga.reshape((4 * self.w[k].shape[0],) + self.w[k].shape[1:])

    def start_first(self):
        self.w_in_local = self.w["w_in"].astype(BF16)
        local = [_halves(self.w_in_local)]
        for k in SMALL_SHARDED:
            flat = jnp.ravel(self.w[k])
            local.append(jnp.pad(flat, (0, (-flat.shape[0]) % 2048)).reshape(2, -1, 128))
        send_sems, recv_sems, ins, lands, token = _ag_ici_start(local, "gather_first_start")
        self.pending["first"] = (send_sems, recv_sems, ins, lands)
        return token

    def own_w_in(self):
        x, y, _ = _position()
        me = 2 * x + y
        others = jnp.stack([me ^ 1, me ^ 2, me ^ 3]).astype(jnp.int32)
        return self.w_in_local[None], jnp.reshape(me, (1,)).astype(jnp.int32), others

    def first_weights(self, after):
        send_sems, recv_sems, ins, lands = self.pending.pop("first")
        ins, lands = _ag_ici_wait(send_sems, recv_sems, ins, lands, after, "gather_first_wait")
        gathered = _ag_finish(ins, lands, "gather_first_finish")
        self.first = gathered[0]
        small = {}
        for k, ga in zip(SMALL_SHARDED, gathered[1:]):
            r, cdim = self.w[k].shape
            parts = ga.reshape(4, -1)[:, :r * cdim].reshape(4, r, cdim)
            small[k] = jnp.transpose(parts, (1, 0, 2)).reshape(r, 4 * cdim)
        return self._whole("w_in", gathered[0]), small

    def start_rest(self):
        local = [_halves(self.w[k].astype(BF16)) for k in self.REST]
        local, _ = lax.optimization_barrier((local, self.first))
        send_sems, recv_sems, ins, lands, token = _ag_ici_start(local, "gather_rest_start")
        self.pending["rest"] = (send_sems, recv_sems, ins, lands)
        return token

    def rest_weights(self, after):
        send_sems, recv_sems, ins, lands = self.pending.pop("rest")
        ins, lands = _ag_ici_wait(send_sems, recv_sems, ins, lands, after, "gather_rest_wait")
        gathered = _ag_finish(ins, lands, "gather_rest_finish")
        return {k: self._whole(k, ga) for k, ga in zip(self.REST, gathered)}

    def reduce_begin(self, tag, names, gb):
        parts = [gb[k].reshape((4,) + self.w[k].shape) for k in names]
        state, token = _reduce_begin(parts, tag)
        self.pending[tag] = (state, names)
        return token

    def reduce_end(self, tag, after):
        state, names = self.pending.pop(tag)
        self.reduced.update(zip(names, _reduce_end(state, after, tag)))


def _step(x, mem, target, w, m, v):
    xi, yi, _ = _position()
    shard = 2 * xi + yi
    ex = _Exchanges(w)
    sp = {k: w[k] for k in SMALL if k not in SMALL_SHARDED}
    loss_tile, dx, gs, _ = _local_step(x, mem, target, sp, ex)
    g_big = ex.reduced

    red = _all_reduce_small([loss_tile] + [gs[k] for k in SMALL])
    loss = red[0][0, 0]
    g_small = dict(zip(SMALL, red[1:]))
    for k in SMALL_SHARDED:
        cdim = w[k].shape[1]
        g_small[k] = lax.dynamic_slice_in_dim(g_small[k], shard * cdim, cdim, axis=1)

    grads, delta, new_m, new_v = {}, {}, {}, {}
    for k in BIG:
        grads[k] = g_big[k]
        delta[k], new_m[k], new_v[k] = _adamw(w[k], g_big[k], m[k], v[k], "adamw_" + k)
    outs = _adamw_small([w[k] for k in SMALL], [g_small[k] for k in SMALL], [m[k] for k in SMALL], [v[k] for k in SMALL])
    for dst, vals in zip((delta, new_m, new_v), outs):
        dst.update(zip(SMALL, vals))
    grads.update(g_small)
    return loss, dx, grads, delta, new_m, new_v


def kernel(x, mem, rel_bias_table, attn_norm_w, w_in, b_gate, q_norm_w, k_norm_w, w_attn_o, conv_dw_w, conv_dw_b, conv_ln_w, conv_ln_b, w_conv_o, mem_norm_w, w_mem_kv, xq_norm_w, xk_norm_w, w_cross_o, w_out, ffn_norm_w, w_up, ffn_conv_w, ffn_conv_b, w_down, loss_target, m_rel_bias_table, m_attn_norm_w, m_w_in, m_b_gate, m_q_norm_w, m_k_norm_w, m_w_attn_o, m_conv_dw_w, m_conv_dw_b, m_conv_ln_w, m_conv_ln_b, m_w_conv_o, m_mem_norm_w, m_w_mem_kv, m_xq_norm_w, m_xk_norm_w, m_w_cross_o, m_w_out, m_ffn_norm_w, m_w_up, m_ffn_conv_w, m_ffn_conv_b, m_w_down, v_rel_bias_table, v_attn_norm_w, v_w_in, v_b_gate, v_q_norm_w, v_k_norm_w, v_w_attn_o, v_conv_dw_w, v_conv_dw_b, v_conv_ln_w, v_conv_ln_b, v_w_conv_o, v_mem_norm_w, v_w_mem_kv, v_xq_norm_w, v_xk_norm_w, v_w_cross_o, v_w_out, v_ffn_norm_w, v_w_up, v_ffn_conv_w, v_ffn_conv_b, v_w_down):
    args = locals()
    def block(name, k):
        a = args[name] if k == "rel_bias_table" else args[name][0]
        return a.reshape(1, -1) if a.ndim == 1 else a

    w = {k: block(k, k) for k in WEIGHTS}
    m = {k: block("m_" + k, k) for k in WEIGHTS}
    v = {k: block("v_" + k, k) for k in WEIGHTS}
    loss, dx, grads, delta, new_m, new_v = _step(x[0], mem[0], loss_target[0], w, m, v)
    out = [loss, dx[None]]
    for d in (grads, delta, new_m, new_v):
        for k in WEIGHTS:
            out.append(d[k].reshape(args[k].shape))
    return tuple(out)
```

```python
import functools
import math

import numpy as np
import jax
import jax.numpy as jnp
from jax import lax
from jax.experimental import pallas as pl
from jax.experimental.pallas import tpu as pltpu

F32, BF16 = jnp.float32, jnp.bfloat16
SDS = jax.ShapeDtypeStruct
MESH = pl.DeviceIdType.MESH

HEAD = 128
N_GROUPS, HPG = 3, 4
ATTN_GROUPS = ((128, 1), (512, 4), (2048, 16))
NQ = 128
QKV_W = N_GROUPS * HPG * HEAD
CH = 512
CONV_K, FFN_K = 31, 3
N_BUCKETS, MAX_DIST = 32, 2048
RMS_EPS, LN_EPS = 1e-6, 1e-5
O_Q, O_K, O_V, O_CV, O_CG, O_XQ, O_G = 0, QKV_W, 2 * QKV_W, 3 * QKV_W, 3 * QKV_W + CH, 3 * QKV_W + 2 * CH, 3 * QKV_W + 3 * CH
ADAM_LR, ADAM_B1, ADAM_B2, ADAM_EPS, ADAM_WD, ADAM_STEP = 0.001, 0.9, 0.999, 1e-08, 0.01, 10
NEG = -1e30
SCALE = HEAD ** -0.5
TM = 512
MM_TM = 1024
ATT_RB = 2048
NT_DIMS = (((1,), (1,)), ((), ()))
TN_DIMS = (((0,), (0,)), ((), ()))


CONV_RC = 32
CONV_LB = 256
CONV_LANES = tuple(slice(l, l + CONV_LB) for l in range(0, CH, CONV_LB))
CONV_HALO = 32
FFN_RC = 32


def _conv_taps(buf, base, lanes, q_lo, q_hi, visit):
    span = CONV_RC + CONV_HALO
    xx = buf[pl.ds(base, span), lanes]
    for s in range(8):
        xs = xx if s == 0 else pltpu.roll(xx, span - s, 0)
        for q in range(s, q_hi + 1, 8):
            if q >= q_lo:
                visit(q, xs[q - s:q - s + CONV_RC])


def _sig(v):
    return 0.5 * jnp.tanh(0.5 * v) + 0.5


def _fold8(v):
    acc = v[0:8]
    for r in range(8, v.shape[0], 8):
        acc = acc + v[r:r + 8]
    return acc


def _row_tile(rows, cap, mult=8):
    best = None
    for t in range(mult, min(rows, cap) + 1, mult):
        if rows % t == 0:
            best = t
    return best if best is not None else rows


def _full(shape):
    n = len(shape)
    return pl.BlockSpec(shape, lambda *a: (0,) * n)


def _rmsnorm(x, w, name):
    S, D = x.shape

    def body(x_ref, w_ref, o_ref):
        xv = x_ref[...]
        r = lax.rsqrt(jnp.mean(xv * xv, axis=-1, keepdims=True) + RMS_EPS)
        o_ref[...] = (xv * r * w_ref[...]).astype(BF16)

    return pl.pallas_call(
        body, name=name, grid=(S // TM,),
        in_specs=[pl.BlockSpec((TM, D), lambda i: (i, 0)), _full((1, D))],
        out_specs=pl.BlockSpec((TM, D), lambda i: (i, 0)),
        out_shape=SDS((S, D), BF16))(x, w)


def _mm_nn(a, b, out_dtype, name):
    M, K = a.shape
    G, _, n = b.shape

    def body(a_ref, b_ref, o_ref):
        o_ref[...] = jnp.dot(a_ref[...].astype(BF16), b_ref[...], preferred_element_type=F32).astype(out_dtype)

    return pl.pallas_call(
        body, name=name, grid=(G, M // MM_TM),
        in_specs=[pl.BlockSpec((MM_TM, K), lambda g, i: (i, 0)), pl.BlockSpec((None, K, n), lambda g, i: (g, 0, 0))],
        out_specs=pl.BlockSpec((MM_TM, n), lambda g, i: (i, g)),
        out_shape=SDS((M, G * n), out_dtype))(a, b)


def _mm_tn(a, b, G, name):
    S, Ka = a.shape
    n = b.shape[1] // G
    tka = Ka
    while tka * n * 4 > 10 * 2 ** 20 and tka % 256 == 0:
        tka //= 2

    def body(a_ref, b_ref, o_ref):
        @pl.when(pl.program_id(2) == 0)
        def _():
            o_ref[...] = jnp.zeros_like(o_ref)
        o_ref[...] += lax.dot_general(a_ref[...].astype(BF16), b_ref[...].astype(BF16), TN_DIMS, preferred_element_type=F32)

    return pl.pallas_call(
        body, name=name, grid=(G, Ka // tka, S // MM_TM),
        in_specs=[pl.BlockSpec((MM_TM, tka), lambda g, i, k: (k, i)), pl.BlockSpec((MM_TM, n), lambda g, i, k: (k, g))],
        out_specs=pl.BlockSpec((None, tka, n), lambda g, i, k: (g, i, 0)),
        out_shape=SDS((G, Ka, n), F32))(a, b)


NORM_RC = 16


def _norm_in_bwd(a, w, xin, nw, resid, name):
    S, K = xin.shape
    G, _, n = w.shape
    tm = 1024 if S % 1024 == 0 and G * K * n * 2 <= 12 * 2 ** 20 else TM
    n2, steps = n, G

    def body(a_ref, w_ref, x_ref, nw_ref, r_ref, o_ref, dnw_ref, acc, part):
        i, g = pl.program_id(0), pl.program_id(1)

        @pl.when((i == 0) & (g == 0))
        def _():
            part[...] = jnp.zeros_like(part)

        @pl.when(g == 0)
        def _():
            acc[...] = jnp.zeros_like(acc)

        acc[...] += lax.dot_general(a_ref[...], w_ref[g], NT_DIMS, preferred_element_type=F32)

        @pl.when(g == steps - 1)
        def _():
            for r0 in range(0, tm, NORM_RC):
                rows = pl.ds(r0, NORM_RC)
                dn = acc[rows, :]
                xv = x_ref[rows, :]
                r = lax.rsqrt(jnp.mean(xv * xv, axis=-1, keepdims=True) + RMS_EPS)
                xhat = xv * r
                dyw = dn * nw_ref[...]
                o_ref[rows, :] = r_ref[rows, :] + r * (dyw - xhat * jnp.mean(dyw * xhat, axis=-1, keepdims=True))
                part[...] += _fold8(dn * xhat)

        @pl.when((i == S // tm - 1) & (g == steps - 1))
        def _():
            dnw_ref[...] = jnp.sum(part[...], axis=0, keepdims=True)

    return pl.pallas_call(
        body, name=name, grid=(S // tm, steps),
        in_specs=[pl.BlockSpec((tm, n2), lambda i, g: (i, g)),
                  pl.BlockSpec((G, K, n), lambda i, g: (0, 0, 0), pipeline_mode=pl.Buffered(1)),
                  pl.BlockSpec((tm, K), lambda i, g: (i, 0)), _full((1, K)), pl.BlockSpec((tm, K), lambda i, g: (i, 0))],
        out_specs=[pl.BlockSpec((tm, K), lambda i, g: (i, 0)), _full((1, K))],
        out_shape=[SDS((S, K), F32), SDS((1, K), F32)],
        scratch_shapes=[pltpu.VMEM((tm, K), F32), pltpu.VMEM((8, K), F32)])(a, w, xin, nw, resid)


def _t5_bucket_np(dist):
    max_exact = N_BUCKETS // 2
    d = np.maximum(dist.astype(np.float32), np.float32(1.0))
    large = max_exact + (np.log(d / np.float32(max_exact)) / np.float32(math.log(MAX_DIST / max_exact))
                         * np.float32(N_BUCKETS - max_exact)).astype(np.int32)
    large = np.minimum(large, N_BUCKETS - 1)
    return np.where(dist < max_exact, dist, large).astype(np.int32)


def _bias_static():
    qi = np.arange(NQ)[:, None]
    kj = np.arange(2 * NQ)[None, :]
    step = qi + NQ - kj
    band = (step >= 0) & (step <= NQ)
    buckets = np.stack([_t5_bucket_np(np.clip(step, 0, None) * dil).reshape(1, -1) for _, dil in ATTN_GROUPS])
    return band, buckets


def _bias_fwd(table_t, buckets):
    nb = buckets.shape[-1]

    def body(t_ref, b_ref, o_ref):
        oh = (b_ref[...] == lax.broadcasted_iota(jnp.int32, (N_BUCKETS, nb), 0)).astype(F32)
        o_ref[...] = jnp.dot(t_ref[...], oh, preferred_element_type=F32, precision=lax.Precision.HIGHEST)

    return pl.pallas_call(
        body, name="bias_fwd", grid=(N_GROUPS,),
        in_specs=[pl.BlockSpec((None, 8, N_BUCKETS), lambda g: (g, 0, 0)), pl.BlockSpec((None, 1, nb), lambda g: (g, 0, 0))],
        out_specs=pl.BlockSpec((None, 8, nb), lambda g: (g, 0, 0)),
        out_shape=SDS((N_GROUPS, 8, nb), F32))(table_t, buckets)


def _bias_bwd(dsb, buckets):
    nb = buckets.shape[-1]

    def body(d_ref, b_ref, o_ref):
        oh = (b_ref[...] == lax.broadcasted_iota(jnp.int32, (N_BUCKETS, nb), 0)).astype(F32)
        o_ref[...] = lax.dot_general(d_ref[...], oh, NT_DIMS, preferred_element_type=F32, precision=lax.Precision.HIGHEST)

    return pl.pallas_call(
        body, name="bias_bwd", grid=(N_GROUPS,),
        in_specs=[pl.BlockSpec((None, 8, nb), lambda g: (g, 0, 0)), pl.BlockSpec((None, 1, nb), lambda g: (g, 0, 0))],
        out_specs=pl.BlockSpec((None, 8, N_BUCKETS), lambda g: (g, 0, 0)),
        out_shape=SDS((N_GROUPS, 8, N_BUCKETS), F32))(dsb, buckets)


def _qkv_prep(z, qw, kw):
    S = z.shape[0]
    nh = N_GROUPS * HPG

    def body(zq, zk, zv, qw_ref, kw_ref, qh, kh, vh):
        for h in range(nh):
            g = h // HPG
            sl = slice(h * HEAD, (h + 1) * HEAD)
            xq = zq[:, sl].astype(F32)
            qh[h] = xq * lax.rsqrt(jnp.mean(xq * xq, axis=-1, keepdims=True) + RMS_EPS) * qw_ref[g:g + 1, :]
            xk = zk[:, sl].astype(F32)
            kh[h] = xk * lax.rsqrt(jnp.mean(xk * xk, axis=-1, keepdims=True) + RMS_EPS) * kw_ref[g:g + 1, :]
            vh[h] = zv[:, sl].astype(F32)

    hm = pl.BlockSpec((nh, TM, HEAD), lambda i: (0, i, 0))
    return pl.pallas_call(
        body, name="qkv_prep", grid=(S // TM,),
        in_specs=[pl.BlockSpec((TM, QKV_W), lambda i: (i, 0)), pl.BlockSpec((TM, QKV_W), lambda i: (i, 1)),
                  pl.BlockSpec((TM, QKV_W), lambda i: (i, 2)), _full((N_GROUPS, HEAD)), _full((N_GROUPS, HEAD))],
        out_specs=[hm, hm, hm],
        out_shape=[SDS((nh, S, HEAD), F32)] * 3)(z, z, z, qw, kw)


def _rows(start, d):
    return pl.ds(start, NQ) if d == 1 else pl.ds(start, NQ, stride=d)


def _attn_fwd(qh, kh, vh, biasm, g, d):
    S = qh.shape[1]
    RB = ATT_RB
    nbk, nq = S // RB, RB // (NQ * d)

    def body(q_ref, k_ref, v_ref, bias_ref, o_ref, lse_ref, kbuf, vbuf):
        b = pl.program_id(1)

        @pl.when(b == 0)
        def _():
            kbuf[0:RB, :] = jnp.zeros((RB, HEAD), F32)
            vbuf[0:RB, :] = jnp.zeros((RB, HEAD), F32)

        @pl.when(b > 0)
        def _():
            kbuf[0:RB, :] = kbuf[RB:2 * RB, :]
            vbuf[0:RB, :] = vbuf[RB:2 * RB, :]

        kbuf[RB:2 * RB, :] = k_ref[...]
        vbuf[RB:2 * RB, :] = v_ref[...]
        bias = bias_ref[...]
        col = lax.broadcasted_iota(jnp.int32, (NQ, 2 * NQ), 1)

        for qb in range(nq):
            def unit(r, carry, qb=qb):
                qs = qb * NQ * d + r
                q = q_ref[_rows(qs, d), :].astype(BF16)
                kw = jnp.concatenate([kbuf[_rows(RB + qs - NQ * d, d), :], kbuf[_rows(RB + qs, d), :]], axis=0).astype(BF16)
                vw = jnp.concatenate([vbuf[_rows(RB + qs - NQ * d, d), :], vbuf[_rows(RB + qs, d), :]], axis=0).astype(BF16)
                s = lax.dot_general(q, kw, NT_DIMS, preferred_element_type=F32) * SCALE + bias
                if qb == 0:
                    s = jnp.where((col < NQ) & (b == 0), NEG, s)
                m = jnp.max(s, axis=-1, keepdims=True)
                p = jnp.exp(s - m)
                l = jnp.sum(p, axis=-1, keepdims=True)
                o = jnp.dot(p.astype(BF16), vw, preferred_element_type=F32) / l
                o_ref[_rows(qs, d), :] = o
                lse_ref[_rows(qs, d), :] = jnp.broadcast_to(m + jnp.log(l), (NQ, HEAD))
                return carry

            for r in range(d):
                unit(r, 0)

    blk = lambda f: pl.BlockSpec((None, RB, HEAD), f)
    return pl.pallas_call(
        body, name=f"attn_fwd_g{g}", grid=(HPG, nbk),
        in_specs=[blk(lambda h, b: (HPG * g + h, b, 0))] * 3 + [pl.BlockSpec((None, NQ, 2 * NQ), lambda h, b: (h, 0, 0))],
        out_specs=[blk(lambda h, b: (h, b, 0))] * 2,
        out_shape=[SDS((HPG, S, HEAD), F32)] * 2,
        scratch_shapes=[pltpu.VMEM((2 * RB, HEAD), F32)] * 2)(qh, kh, vh, biasm)


def _attn_bwd(qh, kh, vh, biasm, da, wg, dh, lse, g, d):
    S = qh.shape[1]
    RB = ATT_RB
    nbk, nq = S // RB, RB // (NQ * d)

    def body(q_ref, k_ref, v_ref, bias_ref, da_ref, wg_ref, dh_ref, lse_ref,
             dq_ref, dk_ref, dv_ref, dsb_ref, kbuf, vbuf, dkbuf, dvbuf):
        b = pl.program_id(1)
        zero = jnp.zeros((RB, HEAD), F32)

        @pl.when(b == 0)
        def _():
            kbuf[0:RB, :] = zero
            vbuf[0:RB, :] = zero
            dkbuf[0:RB, :] = zero
            dvbuf[0:RB, :] = zero
            dsb_ref[...] = jnp.zeros_like(dsb_ref)

        @pl.when(b > 0)
        def _():
            kbuf[0:RB, :] = kbuf[RB:2 * RB, :]
            vbuf[0:RB, :] = vbuf[RB:2 * RB, :]
            dkbuf[0:RB, :] = dkbuf[RB:2 * RB, :]
            dvbuf[0:RB, :] = dvbuf[RB:2 * RB, :]

        dkbuf[RB:2 * RB, :] = zero
        dvbuf[RB:2 * RB, :] = zero

        @pl.when(b < nbk)
        def _():
            kbuf[RB:2 * RB, :] = k_ref[...]
            vbuf[RB:2 * RB, :] = v_ref[...]
            bias = bias_ref[...]
            col = lax.broadcasted_iota(jnp.int32, (NQ, 2 * NQ), 1)

            for qb in range(nq):
                def unit(r, carry, qb=qb):
                    qs = qb * NQ * d + r
                    prev, cur = _rows(RB + qs - NQ * d, d), _rows(RB + qs, d)
                    q = q_ref[_rows(qs, d), :].astype(BF16)
                    kw = jnp.concatenate([kbuf[prev, :], kbuf[cur, :]], axis=0).astype(BF16)
                    vw = jnp.concatenate([vbuf[prev, :], vbuf[cur, :]], axis=0).astype(BF16)
                    s = lax.dot_general(q, kw, NT_DIMS, preferred_element_type=F32) * SCALE + bias
                    if qb == 0:
                        s = jnp.where((col < NQ) & (b == 0), NEG, s)
                    p = jnp.exp(s - lse_ref[_rows(qs, d), :][:, 0:1])
                    w = wg_ref[_rows(qs, d), :]
                    do = (da_ref[_rows(qs, d), :] * w).astype(BF16)
                    dp = lax.dot_general(do, vw, NT_DIMS, preferred_element_type=F32)
                    ds = p * (dp - w[:, 0:1] * dh_ref[_rows(qs, d), :][:, 0:1])
                    dsb_ref[...] += ds
                    dsb = ds.astype(BF16)
                    dq_ref[_rows(qs, d), :] = jnp.dot(dsb, kw, preferred_element_type=F32) * SCALE
                    dkw = lax.dot_general(dsb, q, TN_DIMS, preferred_element_type=F32) * SCALE
                    dvw = lax.dot_general(p.astype(BF16), do, TN_DIMS, preferred_element_type=F32)
                    dkbuf[prev, :] += dkw[0:NQ, :]
                    dkbuf[cur, :] += dkw[NQ:2 * NQ, :]
                    dvbuf[prev, :] += dvw[0:NQ, :]
                    dvbuf[cur, :] += dvw[NQ:2 * NQ, :]
                    return carry

                for r in range(d):
                    unit(r, 0)

        dk_ref[...] = dkbuf[0:RB, :]
        dv_ref[...] = dvbuf[0:RB, :]

    blk = lambda f: pl.BlockSpec((None, RB, HEAD), f)
    cur_g = blk(lambda h, b: (HPG * g + h, jnp.minimum(b, nbk - 1), 0))
    cur = blk(lambda h, b: (h, jnp.minimum(b, nbk - 1), 0))
    prv = blk(lambda h, b: (h, jnp.maximum(b - 1, 0), 0))
    sq = pl.BlockSpec((None, NQ, 2 * NQ), lambda h, b: (h, 0, 0))
    return pl.pallas_call(
        body, name=f"attn_bwd_g{g}", grid=(HPG, nbk + 1),
        in_specs=[cur_g, cur_g, cur_g, sq, cur, cur, cur, cur],
        out_specs=[cur, prv, prv, sq],
        out_shape=[SDS((HPG, S, HEAD), F32)] * 3 + [SDS((HPG, NQ, 2 * NQ), F32)],
        scratch_shapes=[pltpu.VMEM((2 * RB, HEAD), F32)] * 4)(qh, kh, vh, biasm, da, wg, dh, lse)


def _merge_weights(l0, l1, l2):
    m = jnp.maximum(jnp.maximum(l0, l1), l2)
    e0, e1, e2 = jnp.exp(l0 - m), jnp.exp(l1 - m), jnp.exp(l2 - m)
    inv = 1.0 / (e0 + e1 + e2)
    return e0 * inv, e1 * inv, e2 * inv


def _merge_fwd(os_, lses):
    S = os_[0].shape[1]

    def body(o0, o1, o2, l0, l1, l2, a_ref):
        for h in range(HPG):
            w0, w1, w2 = _merge_weights(l0[h], l1[h], l2[h])
            a_ref[:, h * HEAD:(h + 1) * HEAD] = (w0 * o0[h] + w1 * o1[h] + w2 * o2[h]).astype(BF16)

    hm = pl.BlockSpec((HPG, TM, HEAD), lambda i: (0, i, 0))
    return pl.pallas_call(
        body, name="merge_fwd", grid=(S // TM,), in_specs=[hm] * 6,
        out_specs=pl.BlockSpec((TM, CH), lambda i: (i, 0)),
        out_shape=SDS((S, CH), BF16))(*os_, *lses)


def _merge_bwd(dattn, os_, lses):
    S = dattn.shape[0]

    def body(da_ref, o0, o1, o2, l0, l1, l2, w0_ref, w1_ref, w2_ref, dah_ref, dh_ref):
        for h in range(HPG):
            w = _merge_weights(l0[h], l1[h], l2[h])
            attn = w[0] * o0[h] + w[1] * o1[h] + w[2] * o2[h]
            da = da_ref[:, h * HEAD:(h + 1) * HEAD]
            for w_ref, wv in zip((w0_ref, w1_ref, w2_ref), w):
                w_ref[h] = wv
            dah_ref[h] = da
            dh_ref[h] = jnp.broadcast_to(jnp.sum(da * attn, axis=-1, keepdims=True), (TM, HEAD))

    hm = pl.BlockSpec((HPG, TM, HEAD), lambda i: (0, i, 0))
    res = pl.pallas_call(
        body, name="merge_bwd", grid=(S // TM,),
        in_specs=[pl.BlockSpec((TM, CH), lambda i: (i, 0))] + [hm] * 6,
        out_specs=[hm] * 5, out_shape=[SDS((HPG, S, HEAD), F32)] * 5)(dattn, *os_, *lses)
    return res[0:3], res[3], res[4]


def _qkv_bwd(dz, z, dqs, dks, dvs, qw, kw):
    S = z.shape[0]
    nh = N_GROUPS * HPG

    def body(dz_in, zq, zk, *refs):
        del dz_in
        dq_refs, dk_refs, dv_refs = refs[0:3], refs[3:6], refs[6:9]
        qw_ref, kw_ref, dz_ref, dqw_ref, dkw_ref = refs[9:]

        @pl.when(pl.program_id(0) == 0)
        def _():
            dqw_ref[...] = jnp.zeros_like(dqw_ref)
            dkw_ref[...] = jnp.zeros_like(dkw_ref)

        def nbwd(xr, dy, wr, dwr, h, off):
            g = h // HPG
            x = xr[:, h * HEAD:(h + 1) * HEAD].astype(F32)
            r = lax.rsqrt(jnp.mean(x * x, axis=-1, keepdims=True) + RMS_EPS)
            xhat = x * r
            dyw = dy * wr[g:g + 1, :]
            dz_ref[:, off + h * HEAD:off + (h + 1) * HEAD] = (
                r * (dyw - xhat * jnp.mean(dyw * xhat, axis=-1, keepdims=True))).astype(BF16)
            dwr[g:g + 1, :] += jnp.sum(dy * xhat, axis=0, keepdims=True)

        for h in range(nh):
            g, hh = h // HPG, h % HPG
            nbwd(zq, dq_refs[g][hh], qw_ref, dqw_ref, h, O_Q)
            nbwd(zk, dk_refs[g][hh], kw_ref, dkw_ref, h, O_K)
            dz_ref[:, O_V + h * HEAD:O_V + (h + 1) * HEAD] = dv_refs[g][hh].astype(BF16)

    hm = pl.BlockSpec((HPG, TM, HEAD), lambda i: (0, i, 0))
    return pl.pallas_call(
        body, name="qkv_bwd", grid=(S // TM,),
        in_specs=[pl.BlockSpec(memory_space=pl.ANY), pl.BlockSpec((TM, QKV_W), lambda i: (i, 0)),
                  pl.BlockSpec((TM, QKV_W), lambda i: (i, 1))] + [hm] * 9 + [_full((N_GROUPS, HEAD)), _full((N_GROUPS, HEAD))],
        out_specs=[pl.BlockSpec((TM, 3 * QKV_W), lambda i: (i, 0)), _full((N_GROUPS, HEAD)), _full((N_GROUPS, HEAD))],
        out_shape=[SDS(dz.shape, BF16), SDS((N_GROUPS, HEAD), F32), SDS((N_GROUPS, HEAD), F32)],
        input_output_aliases={0: 0})(dz, z, z, *dqs, *dks, *dvs, qw, kw)


def _conv_fwd(z, cw, cb, lnw, lnb):
    S = z.shape[0]
    H = 32

    def body(zv, zg, cw_ref, cb_ref, lnw_ref, lnb_ref, u1_ref, u3_ref, xbuf):
        i = pl.program_id(0)

        @pl.when(i == 0)
        def _():
            xbuf[0:H, :] = jnp.zeros((H, CH), F32)

        @pl.when(i > 0)
        def _():
            xbuf[0:H, :] = xbuf[TM:TM + H, :]

        xbuf[H:H + TM, :] = zv[...].astype(F32) * _sig(zg[...].astype(F32))
        for r0 in range(0, TM, CONV_RC):
            rows = pl.ds(r0, CONV_RC)
            parts = []
            for lanes in CONV_LANES:
                part = [jnp.broadcast_to(cb_ref[:, lanes], (CONV_RC, CONV_LB))]

                def tap(q, view, part=part, lanes=lanes):
                    part[0] = part[0] + view * cw_ref[q - 2:q - 1, lanes]

                _conv_taps(xbuf, r0, lanes, 2, CONV_K + 1, tap)
                parts.append(part[0])
            acc = jnp.concatenate(parts, axis=1)
            u1_ref[rows, :] = acc
            mu = jnp.mean(acc, axis=-1, keepdims=True)
            xc = acc - mu
            yl = xc * lax.rsqrt(jnp.mean(xc * xc, axis=-1, keepdims=True) + LN_EPS) * lnw_ref[...] + lnb_ref[...]
            u3_ref[rows, :] = (yl * _sig(yl)).astype(BF16)

    row = pl.BlockSpec((TM, CH), lambda i: (i, 0))
    return pl.pallas_call(
        body, name="conv_fwd", grid=(S // TM,),
        in_specs=[pl.BlockSpec((TM, CH), lambda i: (i, O_CV // CH)), pl.BlockSpec((TM, CH), lambda i: (i, O_CG // CH)),
                  _full((CONV_K, CH)), _full((1, CH)), _full((1, CH)), _full((1, CH))],
        out_specs=[row, row], out_shape=[SDS((S, CH), F32), SDS((S, CH), BF16)],
        scratch_shapes=[pltpu.VMEM((TM + H, CH), F32)])(z, z, cw, cb, lnw, lnb)


def _conv_bwd_a(du3, u1, z, lnw, lnb):
    S = z.shape[0]
    H = 32

    nt = S // TM

    def body(du3_ref, u1_ref, zv, zg, lnw_ref, lnb_ref, du1_ref, acc_ref, xbuf, tacc):
        i = pl.program_id(0)

        @pl.when(i == 0)
        def _():
            xbuf[0:H, :] = jnp.zeros((H, CH), F32)
            tacc[...] = jnp.zeros_like(tacc)

        @pl.when(i > 0)
        def _():
            xbuf[0:H, :] = xbuf[TM:TM + H, :]

        xbuf[H:H + TM, :] = zv[...].astype(F32) * _sig(zg[...].astype(F32))
        for r0 in range(0, TM, CONV_RC):
            rows = pl.ds(r0, CONV_RC)
            u1 = u1_ref[rows, :]
            mu = jnp.mean(u1, axis=-1, keepdims=True)
            xc = u1 - mu
            r = lax.rsqrt(jnp.mean(xc * xc, axis=-1, keepdims=True) + LN_EPS)
            yhat = xc * r
            yl = yhat * lnw_ref[...] + lnb_ref[...]
            sg = _sig(yl)
            dyl = du3_ref[rows, :] * (sg * (1.0 + yl * (1.0 - sg)))
            dyh = dyl * lnw_ref[...]
            du1 = r * (dyh - jnp.mean(dyh, axis=-1, keepdims=True) - yhat * jnp.mean(dyh * yhat, axis=-1, keepdims=True))
            du1_ref[rows, :] = du1
            tacc[33] += _fold8(dyl * yhat)
            tacc[34] += _fold8(dyl)
            tacc[32] += _fold8(du1)
            for lanes in CONV_LANES:
                d = du1[:, lanes]

                def tap(q, view, d=d, lanes=lanes):
                    tacc[q - 2, :, lanes] += _fold8(d * view)

                _conv_taps(xbuf, r0, lanes, 2, CONV_K + 1, tap)

        @pl.when(i == nt - 1)
        def _():
            for k in range(40):
                acc_ref[k:k + 1, :] = jnp.sum(tacc[k], axis=0, keepdims=True)

    row = pl.BlockSpec((TM, CH), lambda i: (i, 0))
    return pl.pallas_call(
        body, name="conv_bwd_a", grid=(nt,),
        in_specs=[row, row, pl.BlockSpec((TM, CH), lambda i: (i, O_CV // CH)), pl.BlockSpec((TM, CH), lambda i: (i, O_CG // CH)),
                  _full((1, CH)), _full((1, CH))],
        out_specs=[row, _full((40, CH))], out_shape=[SDS((S, CH), F32), SDS((40, CH), F32)],
        scratch_shapes=[pltpu.VMEM((TM + H, CH), F32), pltpu.VMEM((40, 8, CH), F32)])(du3, u1, z, z, lnw, lnb)


def _conv_bwd_b(dz, du1, z, cw):
    S = z.shape[0]
    nt = S // TM
    H = 32

    def body(dz_in, du1_ref, zv, zg, cw_ref, dz_ref, ybuf, dgate):
        del dz_in
        i, p = pl.program_id(0), pl.program_id(1)

        @pl.when(p == 0)
        def _():
            @pl.when(i == 0)
            def _():
                ybuf[TM:TM + H, :] = jnp.zeros((H, CH), F32)

            @pl.when(i > 0)
            def _():
                ybuf[TM:TM + H, :] = ybuf[0:H, :]

            ybuf[0:TM, :] = du1_ref[...]
            for r0 in range(0, TM, CONV_RC):
                rows = pl.ds(r0, CONV_RC)
                parts = []
                for lanes in CONV_LANES:
                    part = [jnp.zeros((CONV_RC, CONV_LB), F32)]

                    def tap(q, view, part=part, lanes=lanes):
                        part[0] = part[0] + view * cw_ref[CONV_K - 1 - q:CONV_K - q, lanes]

                    _conv_taps(ybuf, r0, lanes, 0, CONV_K - 1, tap)
                    parts.append(part[0])
                acc = jnp.concatenate(parts, axis=1)
                val = zv[rows, :].astype(F32)
                sg = _sig(zg[rows, :].astype(F32))
                dz_ref[rows, :] = (acc * sg).astype(BF16)
                dgate[rows, :] = (acc * val * sg * (1.0 - sg)).astype(BF16)

        @pl.when(p == 1)
        def _():
            dz_ref[...] = dgate[...]

    rev = lambda c: pl.BlockSpec((TM, CH), lambda i, p: (nt - 1 - i, c))
    return pl.pallas_call(
        body, name="conv_bwd_b", grid=(nt, 2),
        in_specs=[pl.BlockSpec(memory_space=pl.ANY), rev(0), rev(O_CV // CH), rev(O_CG // CH), _full((CONV_K, CH))],
        out_specs=pl.BlockSpec((TM, CH), lambda i, p: (nt - 1 - i, O_CV // CH + p)),
        out_shape=SDS(dz.shape, BF16),
        scratch_shapes=[pltpu.VMEM((TM + H, CH), F32), pltpu.VMEM((TM, CH), BF16)],
        input_output_aliases={0: 0})(dz, du1, z, z, cw)


def _memkv_fwd(mem, mnw, wkv, xkw):
    M, D = mem.shape

    def body(mem_ref, mnw_ref, w_ref, xkw_ref, mn_ref, kv_ref, mk_ref, mv_ref):
        x = mem_ref[...]
        mn = (x * lax.rsqrt(jnp.mean(x * x, axis=-1, keepdims=True) + RMS_EPS) * mnw_ref[...]).astype(BF16)
        mn_ref[...] = mn
        kv = jnp.dot(mn, w_ref[...], preferred_element_type=F32)
        kv_ref[...] = kv
        for h in range(HPG):
            k = kv[:, h * HEAD:(h + 1) * HEAD]
            mk_ref[:, h * HEAD:(h + 1) * HEAD] = (
                k * lax.rsqrt(jnp.mean(k * k, axis=-1, keepdims=True) + RMS_EPS) * xkw_ref[...]).astype(BF16)
        mv_ref[...] = kv[:, CH:2 * CH].astype(BF16)

    return pl.pallas_call(
        body, name="memkv_fwd",
        out_shape=[SDS((M, D), BF16), SDS((M, 2 * CH), F32), SDS((M, CH), BF16), SDS((M, CH), BF16)])(mem, mnw, wkv, xkw)


def _cross_q(zx, xqw, h):
    x = zx[:, h * HEAD:(h + 1) * HEAD].astype(F32)
    r = lax.rsqrt(jnp.mean(x * x, axis=-1, keepdims=True) + RMS_EPS)
    xhat = x * r
    return xhat, r, xhat * xqw


def _cross_fwd(z, xqw, mk, mv):
    S = z.shape[0]
    M = mk.shape[0]

    def body(zx, xqw_ref, mk_ref, mv_ref, o_ref):
        for h in range(HPG):
            sl = slice(h * HEAD, (h + 1) * HEAD)
            _, _, q = _cross_q(zx, xqw_ref[...], h)
            s = lax.dot_general(q.astype(BF16), mk_ref[:, sl], NT_DIMS, preferred_element_type=F32) * SCALE
            e = jnp.exp(s - jnp.max(s, axis=-1, keepdims=True))
            p = e / jnp.sum(e, axis=-1, keepdims=True)
            o_ref[:, sl] = jnp.dot(p.astype(BF16), mv_ref[:, sl], preferred_element_type=F32).astype(BF16)

    return pl.pallas_call(
        body, name="cross_fwd", grid=(S // TM,),
        in_specs=[pl.BlockSpec((TM, CH), lambda i: (i, O_XQ // CH)), _full((1, HEAD)), _full((M, CH)), _full((M, CH))],
        out_specs=pl.BlockSpec((TM, CH), lambda i: (i, 0)), out_shape=SDS((S, CH), BF16))(z, xqw, mk, mv)


def _cross_bwd(dz, doc, z, xqw, mk, mv):
    S = z.shape[0]
    M = mk.shape[0]

    def body(dz_in, do_ref, zx, xqw_ref, mk_ref, mv_ref, dz_ref, dmk_ref, dmv_ref, dxw_ref):
        del dz_in

        @pl.when(pl.program_id(0) == 0)
        def _():
            dmk_ref[...] = jnp.zeros_like(dmk_ref)
            dmv_ref[...] = jnp.zeros_like(dmv_ref)
            dxw_ref[...] = jnp.zeros_like(dxw_ref)

        for h in range(HPG):
            sl = slice(h * HEAD, (h + 1) * HEAD)
            xhat, r, q = _cross_q(zx, xqw_ref[...], h)
            qb = q.astype(BF16)
            s = lax.dot_general(qb, mk_ref[:, sl], NT_DIMS, preferred_element_type=F32) * SCALE
            e = jnp.exp(s - jnp.max(s, axis=-1, keepdims=True))
            p = e / jnp.sum(e, axis=-1, keepdims=True)
            do = do_ref[:, sl].astype(BF16)
            dp = lax.dot_general(do, mv_ref[:, sl], NT_DIMS, preferred_element_type=F32)
            ds = (p * (dp - jnp.sum(p * dp, axis=-1, keepdims=True)) * SCALE).astype(BF16)
            dq = jnp.dot(ds, mk_ref[:, sl], preferred_element_type=F32)
            dmk_ref[:, sl] += lax.dot_general(ds, qb, TN_DIMS, preferred_element_type=F32)
            dmv_ref[:, sl] += lax.dot_general(p.astype(BF16), do, TN_DIMS, preferred_element_type=F32)
            dyw = dq * xqw_ref[...]
            dz_ref[:, sl] = (r * (dyw - xhat * jnp.mean(dyw * xhat, axis=-1, keepdims=True))).astype(BF16)
            dxw_ref[...] += jnp.sum(dq * xhat, axis=0, keepdims=True)

    return pl.pallas_call(
        body, name="cross_bwd", grid=(S // TM,),
        in_specs=[pl.BlockSpec(memory_space=pl.ANY), pl.BlockSpec((TM, CH), lambda i: (i, 0)),
                  pl.BlockSpec((TM, CH), lambda i: (i, O_XQ // CH)), _full((1, HEAD)), _full((M, CH)), _full((M, CH))],
        out_specs=[pl.BlockSpec((TM, CH), lambda i: (i, O_XQ // CH)), _full((M, CH)), _full((M, CH)), _full((1, HEAD))],
        out_shape=[SDS(dz.shape, BF16), SDS((M, CH), F32), SDS((M, CH), F32), SDS((1, HEAD), F32)],
        input_output_aliases={0: 0})(dz, doc, z, xqw, mk, mv)


def _memkv_bwd(dmk, dmv, kv, mem, mn, mnw, wkv, xkw):
    M, D = mem.shape

    def body(dmk_ref, dmv_ref, kv_ref, mem_ref, mn_ref, mnw_ref, w_ref, xkw_ref, dw_ref, dxk_ref, dmn_ref, dkv):
        dxk = jnp.zeros((1, HEAD), F32)
        for h in range(HPG):
            sl = slice(h * HEAD, (h + 1) * HEAD)
            k = kv_ref[:, sl]
            r = lax.rsqrt(jnp.mean(k * k, axis=-1, keepdims=True) + RMS_EPS)
            khat = k * r
            dy = dmk_ref[:, sl]
            dyw = dy * xkw_ref[...]
            dkv[:, sl] = (r * (dyw - khat * jnp.mean(dyw * khat, axis=-1, keepdims=True))).astype(BF16)
            dxk = dxk + jnp.sum(dy * khat, axis=0, keepdims=True)
        dxk_ref[...] = dxk
        dkv[:, CH:2 * CH] = dmv_ref[...].astype(BF16)
        dw_ref[...] = lax.dot_general(mn_ref[...], dkv[...], TN_DIMS, preferred_element_type=F32)
        dn = lax.dot_general(dkv[...], w_ref[...], NT_DIMS, preferred_element_type=F32)
        x = mem_ref[...]
        r = lax.rsqrt(jnp.mean(x * x, axis=-1, keepdims=True) + RMS_EPS)
        dmn_ref[...] = jnp.sum(dn * x * r, axis=0, keepdims=True)

    return pl.pallas_call(
        body, name="memkv_bwd",
        out_shape=[SDS((D, 2 * CH), F32), SDS((1, HEAD), F32), SDS((1, D), F32)],
        scratch_shapes=[pltpu.VMEM((M, 2 * CH), BF16)])(dmk, dmv, kv, mem, mn, mnw, wkv, xkw)


def _branch_proj(a_ref, w_ref, y_ref):
    G, _, n = w_ref.shape
    a = a_ref[...]
    for g in range(G):
        y_ref[:, g * n:(g + 1) * n] = jnp.dot(a, w_ref[g], preferred_element_type=F32)


OUT_RC = 16


def _gates(zg_ref, bg_ref, rows, k, D):
    return _sig(zg_ref[rows, k * D:(k + 1) * D].astype(F32) + bg_ref[:, k * D:(k + 1) * D])


def _outproj_fwd(x, z, bg, attn, u3, oc, wao, wco, wxo, wout, fnw):
    S, D = x.shape
    tm = TM

    def body(x_ref, zg_ref, bg_ref, a_ref, u_ref, c_ref, wa, wc, wx, wo, fnw_ref, h1_ref, hn_ref, ya, yc, yx, mg):
        _branch_proj(a_ref, wa, ya)
        _branch_proj(u_ref, wc, yc)
        _branch_proj(c_ref, wx, yx)
        for r0 in range(0, tm, OUT_RC):
            rows = pl.ds(r0, OUT_RC)
            mg[rows, :] = (_gates(zg_ref, bg_ref, rows, 0, D) * ya[rows, :] + _gates(zg_ref, bg_ref, rows, 1, D) * yc[rows, :]
                           + _gates(zg_ref, bg_ref, rows, 2, D) * yx[rows, :]).astype(BF16)
        ya[...] = jnp.dot(mg[...], wo[...], preferred_element_type=F32)
        for r0 in range(0, tm, OUT_RC):
            rows = pl.ds(r0, OUT_RC)
            h1 = x_ref[rows, :] + ya[rows, :]
            h1_ref[rows, :] = h1
            hn_ref[rows, :] = (h1 * lax.rsqrt(jnp.mean(h1 * h1, axis=-1, keepdims=True) + RMS_EPS) * fnw_ref[...]).astype(BF16)

    row = lambda w: pl.BlockSpec((tm, w), lambda i: (i, 0))
    return pl.pallas_call(
        body, name="outproj_fwd", grid=(S // tm,),
        in_specs=[row(D), pl.BlockSpec((tm, 3 * D), lambda i: (i, O_G // (3 * D))), _full((1, 3 * D)), row(CH), row(CH), row(CH),
                  _full(wao.shape), _full(wco.shape), _full(wxo.shape), _full((D, D)), _full((1, D))],
        out_specs=[row(D), row(D)], out_shape=[SDS((S, D), F32), SDS((S, D), BF16)],
        scratch_shapes=[pltpu.VMEM((tm, D), F32)] * 3 + [pltpu.VMEM((tm, D), BF16)])(x, z, bg, attn, u3, oc, wao, wco, wxo, wout, fnw)


def _outproj_bwd(dh1, z, bg, attn, u3, oc, wao, wco, wxo, wout, n_in):
    S, D = dh1.shape
    tm = 256
    nt = S // tm
    G, _, n = wao.shape

    def body(dh_ref, zg_ref, bg_ref, a_ref, u_ref, c_ref, wa, wc, wx, wo,
             dz_ref, da_ref, du_ref, dc_ref, dbg_ref, dwo_ref, dwa_ref, dwc_ref, dwx_ref,
             ya, yc, yx, dm, dy, mg, bacc, wacc):
        i = pl.program_id(0)

        @pl.when(i == 0)
        def _():
            bacc[...] = jnp.zeros_like(bacc)
            wacc[...] = jnp.zeros_like(wacc)
            dwo_ref[...] = jnp.zeros_like(dwo_ref)

        _branch_proj(a_ref, wa, ya)
        _branch_proj(u_ref, wc, yc)
        _branch_proj(c_ref, wx, yx)
        dhb = dh_ref[...].astype(BF16)
        dm[...] = lax.dot_general(dhb, wo[...], NT_DIMS, preferred_element_type=F32)
        for r0 in range(0, tm, OUT_RC):
            rows = pl.ds(r0, OUT_RC)
            dmv = dm[rows, :]
            merged = jnp.zeros((OUT_RC, D), F32)
            for k, y in enumerate((ya, yc, yx)):
                gk = _gates(zg_ref, bg_ref, rows, k, D)
                yk = y[rows, :]
                merged = merged + gk * yk
                dzg = dmv * yk * gk * (1.0 - gk)
                dz_ref[rows, k * D:(k + 1) * D] = dzg.astype(BF16)
                bacc[:, k * D:(k + 1) * D] += _fold8(dzg)
                dy[k, rows, :] = (dmv * gk).astype(BF16)
            mg[rows, :] = merged.astype(BF16)
        dwo_ref[...] += lax.dot_general(mg[...], dhb, TN_DIMS, preferred_element_type=F32)
        for k, (b_ref, w_ref, db_ref) in enumerate(((a_ref, wa, da_ref), (u_ref, wc, du_ref), (c_ref, wx, dc_ref))):
            dyk = dy[k]
            acc = jnp.zeros((tm, CH), F32)
            for g in range(G):
                acc = acc + lax.dot_general(dyk[:, g * n:(g + 1) * n], w_ref[g], NT_DIMS, preferred_element_type=F32)
            db_ref[...] = acc
            wacc[k] += lax.dot_general(b_ref[...], dyk, TN_DIMS, preferred_element_type=F32)

        @pl.when(i == nt - 1)
        def _():
            dbg_ref[...] = jnp.sum(bacc[...], axis=0, keepdims=True)
            for k, dw_ref in enumerate((dwa_ref, dwc_ref, dwx_ref)):
                for g in range(G):
                    dw_ref[g] = wacc[k, :, g * n:(g + 1) * n]

    row = lambda w: pl.BlockSpec((tm, w), lambda i: (i, 0))
    return pl.pallas_call(
        body, name="outproj_bwd", grid=(nt,),
        in_specs=[row(D), pl.BlockSpec((tm, 3 * D), lambda i: (i, O_G // (3 * D))), _full((1, 3 * D)), row(CH), row(CH), row(CH),
                  _full(wao.shape), _full(wco.shape), _full(wxo.shape), _full((D, D))],
        out_specs=[pl.BlockSpec((tm, 3 * D), lambda i: (i, O_G // (3 * D))), row(CH), row(CH), row(CH), _full((1, 3 * D)),
                   _full((D, D))] + [_full(wao.shape)] * 3,
        out_shape=[SDS((S, n_in), BF16)] + [SDS((S, CH), F32)] * 3 + [SDS((1, 3 * D), F32), SDS((D, D), F32)]
        + [SDS(wao.shape, F32)] * 3,
        scratch_shapes=[pltpu.VMEM((tm, D), F32)] * 4 + [pltpu.VMEM((3, tm, D), BF16), pltpu.VMEM((tm, D), BF16),
                                                        pltpu.VMEM((8, 3 * D), F32), pltpu.VMEM((3, CH, D), F32)],
    )(dh1, z, bg, attn, u3, oc, wao, wco, wxo, wout)


FFN_TC = 256
FFN_H = 8


def _ffn_taps(buf, r0):
    xx = buf[pl.ds(r0, FFN_RC + FFN_H), :]
    return xx[FFN_H:], pltpu.roll(xx, 1, 0)[FFN_H:], pltpu.roll(xx, 2, 0)[FFN_H:]


def _ffn_conv(taps, w_ref, b_ref):
    x0, x1, x2 = taps
    return b_ref[...] + x0 * w_ref[2:3, :] + x1 * w_ref[1:2, :] + x2 * w_ref[0:1, :]


def _ffn_act_fwd(up, cw, cb):
    S, F2 = up.shape
    nj = F2 // 2 // FFN_TC
    tm = 2048 if S % 2048 == 0 else TM

    def body(ua, ug, wa, wg, ba, bgt, o_ref, abuf, gbuf):
        i = pl.program_id(1)

        @pl.when(i == 0)
        def _():
            abuf[0:FFN_H, :] = jnp.zeros((FFN_H, FFN_TC), F32)
            gbuf[0:FFN_H, :] = jnp.zeros((FFN_H, FFN_TC), F32)

        @pl.when(i > 0)
        def _():
            abuf[0:FFN_H, :] = abuf[tm:tm + FFN_H, :]
            gbuf[0:FFN_H, :] = gbuf[tm:tm + FFN_H, :]

        abuf[FFN_H:FFN_H + tm, :] = ua[...].astype(F32)
        gbuf[FFN_H:FFN_H + tm, :] = ug[...].astype(F32)
        for r0 in range(0, tm, FFN_RC):
            a = _ffn_conv(_ffn_taps(abuf, r0), wa, ba)
            gt = _ffn_conv(_ffn_taps(gbuf, r0), wg, bgt)
            o_ref[pl.ds(r0, FFN_RC), :] = (gt * _sig(gt) * a).astype(BF16)

    return pl.pallas_call(
        body, name="ffn_act_fwd", grid=(nj, S // tm),
        in_specs=[pl.BlockSpec((tm, FFN_TC), lambda j, i: (i, j)), pl.BlockSpec((tm, FFN_TC), lambda j, i: (i, nj + j)),
                  pl.BlockSpec((FFN_K, FFN_TC), lambda j, i: (0, j)), pl.BlockSpec((FFN_K, FFN_TC), lambda j, i: (0, nj + j)),
                  pl.BlockSpec((1, FFN_TC), lambda j, i: (0, j)), pl.BlockSpec((1, FFN_TC), lambda j, i: (0, nj + j))],
        out_specs=pl.BlockSpec((tm, FFN_TC), lambda j, i: (i, j)), out_shape=SDS((S, F2 // 2), BF16),
        scratch_shapes=[pltpu.VMEM((tm + FFN_H, FFN_TC), F32)] * 2)(up, up, cw, cw, cb, cb)


def _ffn_down_loss(act, wdown, h1, target):
    S, D = h1.shape
    F = act.shape[1]

    def body(a_ref, w_ref, h_ref, t_ref, dy_ref, loss_ref):
        @pl.when(pl.program_id(0) == 0)
        def _():
            loss_ref[...] = jnp.zeros_like(loss_ref)

        err = h_ref[...] + jnp.dot(a_ref[...], w_ref[...], preferred_element_type=F32) - t_ref[...]
        dy_ref[...] = err * (1.0 / D)
        loss_ref[...] += 0.5 * jnp.sum(jnp.mean(err * err, axis=-1, keepdims=True))

    row = lambda w: pl.BlockSpec((TM, w), lambda i: (i, 0))
    return pl.pallas_call(
        body, name="ffn_down_loss", grid=(S // TM,),
        in_specs=[row(F), _full((F, D)), row(D), row(D)],
        out_specs=[row(D), _full((8, 128))], out_shape=[SDS((S, D), F32), SDS((8, 128), F32)])(act, wdown, h1, target)


def _ffn_bwd_a(dy, wdown, up, cw, cb):
    S, D = dy.shape
    F2 = up.shape[1]
    F = F2 // 2
    nj = F // FFN_TC
    tm = 1024 if S % 1024 == 0 else TM

    def body(dy_ref, wd_ref, ua, ug, wa, wg, ba, bgt, da_ref, dg_ref, acca_ref, accg_ref, dwd_ref,
             abuf, gbuf, hala, halg, dact_s, act_s):
        i, j = pl.program_id(0), pl.program_id(1)

        @pl.when((i == 0) & (j == 0))
        def _():
            acca_ref[...] = jnp.zeros_like(acca_ref)
            accg_ref[...] = jnp.zeros_like(accg_ref)
            dwd_ref[...] = jnp.zeros_like(dwd_ref)

        @pl.when(i == 0)
        def _():
            abuf[0:FFN_H, :] = jnp.zeros((FFN_H, FFN_TC), F32)
            gbuf[0:FFN_H, :] = jnp.zeros((FFN_H, FFN_TC), F32)

        @pl.when(i > 0)
        def _():
            abuf[0:FFN_H, :] = hala[j]
            gbuf[0:FFN_H, :] = halg[j]

        abuf[FFN_H:FFN_H + tm, :] = ua[...].astype(F32)
        gbuf[FFN_H:FFN_H + tm, :] = ug[...].astype(F32)
        hala[j] = abuf[tm:tm + FFN_H, :]
        halg[j] = gbuf[tm:tm + FFN_H, :]
        dyb = dy_ref[...].astype(BF16)
        dact_s[...] = lax.dot_general(dyb, wd_ref[...], NT_DIMS, preferred_element_type=F32)
        zero8 = jnp.zeros((8, FFN_TC), F32)
        pa, pg = [zero8] * (FFN_K + 1), [zero8] * (FFN_K + 1)
        for r0 in range(0, tm, FFN_RC):
            rows = pl.ds(r0, FFN_RC)
            ta, tg = _ffn_taps(abuf, r0), _ffn_taps(gbuf, r0)
            a = _ffn_conv(ta, wa, ba)
            gt = _ffn_conv(tg, wg, bgt)
            dact = dact_s[rows, :]
            sg = _sig(gt)
            silu = gt * sg
            act_s[rows, :] = (silu * a).astype(BF16)
            dac = dact * silu
            dgc = dact * a * (sg * (1.0 + gt * (1.0 - sg)))
            da_ref[rows, :] = dac.astype(BF16)
            dg_ref[rows, :] = dgc.astype(BF16)
            for k in range(FFN_K):
                pa[k] = pa[k] + _fold8(dac * ta[FFN_K - 1 - k])
                pg[k] = pg[k] + _fold8(dgc * tg[FFN_K - 1 - k])
            pa[FFN_K] = pa[FFN_K] + _fold8(dac)
            pg[FFN_K] = pg[FFN_K] + _fold8(dgc)
        for k in range(FFN_K + 1):
            acca_ref[j, k:k + 1, :] += jnp.sum(pa[k], axis=0, keepdims=True)
            accg_ref[j, k:k + 1, :] += jnp.sum(pg[k], axis=0, keepdims=True)
        dwd_ref[pl.ds(pl.multiple_of(j * FFN_TC, FFN_TC), FFN_TC), :] += lax.dot_general(
            act_s[...], dyb, TN_DIMS, preferred_element_type=F32)

    return pl.pallas_call(
        body, name="ffn_bwd_a", grid=(S // tm, nj),
        in_specs=[pl.BlockSpec((tm, D), lambda i, j: (i, 0)), pl.BlockSpec((FFN_TC, D), lambda i, j: (j, 0)),
                  pl.BlockSpec((tm, FFN_TC), lambda i, j: (i, j)), pl.BlockSpec((tm, FFN_TC), lambda i, j: (i, nj + j)),
                  pl.BlockSpec((FFN_K, FFN_TC), lambda i, j: (0, j)), pl.BlockSpec((FFN_K, FFN_TC), lambda i, j: (0, nj + j)),
                  pl.BlockSpec((1, FFN_TC), lambda i, j: (0, j)), pl.BlockSpec((1, FFN_TC), lambda i, j: (0, nj + j))],
        out_specs=[pl.BlockSpec((tm, FFN_TC), lambda i, j: (i, j))] * 2 + [_full((nj, 8, FFN_TC))] * 2 + [_full((F, D))],
        out_shape=[SDS((S, F), BF16)] * 2 + [SDS((nj, 8, FFN_TC), F32)] * 2 + [SDS((F, D), F32)],
        scratch_shapes=[pltpu.VMEM((tm + FFN_H, FFN_TC), F32)] * 2 + [pltpu.VMEM((nj, FFN_H, FFN_TC), F32)] * 2
        + [pltpu.VMEM((tm, FFN_TC), F32), pltpu.VMEM((tm, FFN_TC), BF16)],
    )(dy, wdown, up, up, cw, cw, cb, cb)


def _ffn_bwd_b(dca, dcg, cw):
    S, F = dca.shape
    nj = F // FFN_TC
    tm = 2048 if S % 2048 == 0 else TM
    nt = S // tm
    span = FFN_RC + FFN_H

    def body(a_ref, g_ref, w_ref, o_ref, ybuf):
        j, i = pl.program_id(0), pl.program_id(1)

        @pl.when(i == 0)
        def _():
            ybuf[tm:tm + FFN_H, :] = jnp.zeros((FFN_H, FFN_TC), F32)

        @pl.when(i > 0)
        def _():
            ybuf[tm:tm + FFN_H, :] = ybuf[0:FFN_H, :]

        ybuf[0:tm, :] = jnp.where(j < nj, a_ref[...], g_ref[...]).astype(F32)
        for r0 in range(0, tm, FFN_RC):
            yy = ybuf[pl.ds(r0, span), :]
            acc = yy[:FFN_RC] * w_ref[2:3, :] + pltpu.roll(yy, span - 1, 0)[:FFN_RC] * w_ref[1:2, :] \
                + pltpu.roll(yy, span - 2, 0)[:FFN_RC] * w_ref[0:1, :]
            o_ref[pl.ds(r0, FFN_RC), :] = acc.astype(BF16)

    return pl.pallas_call(
        body, name="ffn_bwd_b", grid=(2 * nj, nt),
        in_specs=[pl.BlockSpec((tm, FFN_TC), lambda j, i: (nt - 1 - i, jnp.minimum(j, nj - 1))),
                  pl.BlockSpec((tm, FFN_TC), lambda j, i: (nt - 1 - i, jnp.maximum(j - nj, 0))),
                  pl.BlockSpec((FFN_K, FFN_TC), lambda j, i: (0, j))],
        out_specs=pl.BlockSpec((tm, FFN_TC), lambda j, i: (nt - 1 - i, j)), out_shape=SDS((S, 2 * F), BF16),
        scratch_shapes=[pltpu.VMEM((tm + FFN_H, FFN_TC), F32)])(dca, dcg, cw)


def _adamw_update(w_ref, g_ref, m_ref, v_ref, d_ref, nm_ref, nv_ref):
    gv = g_ref[...]
    m2 = ADAM_B1 * m_ref[...] + (1.0 - ADAM_B1) * gv
    v2 = ADAM_B2 * v_ref[...] + (1.0 - ADAM_B2) * jnp.square(gv)
    m_hat = m2 / (1.0 - ADAM_B1 ** ADAM_STEP)
    v_hat = v2 / (1.0 - ADAM_B2 ** ADAM_STEP)
    d_ref[...] = -ADAM_LR * (m_hat / (jnp.sqrt(v_hat) + ADAM_EPS) + ADAM_WD * w_ref[...])
    nm_ref[...] = m2
    nv_ref[...] = v2


def _adamw_small(ws, gs, ms, vs):
    n = len(ws)

    def body(*refs):
        for i in range(n):
            _adamw_update(*[refs[k * n + i] for k in range(7)])

    shapes = [SDS(w.shape, F32) for w in ws]
    res = pl.pallas_call(body, name="adamw_small", out_shape=shapes * 3)(*ws, *gs, *ms, *vs)
    return res[:n], res[n:2 * n], res[2 * n:]


def _adamw(w, g, m, v, name):
    R, C = w.shape
    tr = _row_tile(R, max(8, (2 ** 20) // (4 * C) // 8 * 8))

    def body(w_ref, g_ref, m_ref, v_ref, d_ref, nm_ref, nv_ref):
        _adamw_update(w_ref, g_ref, m_ref, v_ref, d_ref, nm_ref, nv_ref)

    blk = pl.BlockSpec((tr, C), lambda i: (i, 0))
    return pl.pallas_call(
        body, name=name, grid=(R // tr,), in_specs=[blk] * 4, out_specs=[blk] * 3,
        out_shape=[SDS((R, C), F32)] * 3)(w, g, m, v)


HBM_SPEC = pl.BlockSpec(memory_space=pltpu.HBM)
SEM_SPEC = pl.BlockSpec(memory_space=pltpu.SEMAPHORE)
DATAFLOW_EFFECT = pltpu.SideEffectType.DATAFLOW_SIDE_EFFECTING


def _position():
    return lax.axis_index("x"), lax.axis_index("y"), lax.axis_index("c")


def _other_chips(x, y):
    return [(1 - x, y), (x, 1 - y), (1 - x, 1 - y)]


def _all_gather_xy(arrs, name):
    n = len(arrs)
    hbm = pl.BlockSpec(memory_space=pl.ANY)

    def body(*refs):
        ins, outs = refs[:n], refs[n:2 * n]
        send_sems, recv_sems = refs[2 * n:]
        x, y, c = _position()
        me = 2 * x + y
        chips = _other_chips(x, y)

        def rcopy(i, k, src, dst, to):
            return pltpu.make_async_remote_copy(src_ref=src, dst_ref=dst, send_sem=send_sems.at[i, k], recv_sem=recv_sems.at[i, k],
                                                device_id=to, device_id_type=MESH)

        sends = []
        for i in range(n):
            own = rcopy(i, 6, ins[i], outs[i].at[me], (x, y, 1 - c))
            own.start()
            sends.append(own)
        for i in range(n):
            for j, (px, py) in enumerate(chips):
                cp = rcopy(i, j, ins[i].at[c], outs[i].at[me, c], (px, py, c))
                cp.start()
                sends.append(cp)
        for i in range(n):
            for j, (px, py) in enumerate(chips):
                got = outs[i].at[2 * px + py, c]
                rcopy(i, j, ins[i].at[c], got, (x, y, c)).wait_recv()
                fwd = rcopy(i, 3 + j, got, got, (x, y, 1 - c))
                fwd.start()
                sends.append(fwd)
        for i in range(n):
            for j, (px, py) in enumerate(chips):
                theirs = outs[i].at[2 * px + py, 1 - c]
                rcopy(i, 3 + j, theirs, theirs, (x, y, c)).wait_recv()
        for i in range(n):
            rcopy(i, 6, ins[i], outs[i].at[me], (x, y, c)).wait_recv()
        for cp in sends:
            cp.wait_send()

    return pl.pallas_call(
        body, name=name, in_specs=[hbm] * n, out_specs=[hbm] * n,
        out_shape=[SDS((4,) + a.shape, a.dtype) for a in arrs],
        scratch_shapes=[pltpu.SemaphoreType.DMA((n, 7)), pltpu.SemaphoreType.DMA((n, 7))])(*arrs)


def _ag_ici_start(arrs, name):
    n = len(arrs)

    def body(*refs):
        ins, lands = refs[:n], refs[n:2 * n]
        send_sems, recv_sems = refs[2 * n:2 * n + 2]
        token = refs[-1]
        x, y, c = _position()
        for i in range(n):
            for j, (px, py) in enumerate(_other_chips(x, y)):
                pltpu.make_async_remote_copy(src_ref=ins[i].at[c], dst_ref=lands[i].at[2 * x + y, c], send_sem=send_sems.at[3 * i + j],
                                             recv_sem=recv_sems.at[3 * i + j], device_id=(px, py, c), device_id_type=MESH).start()
        token[...] = jnp.zeros_like(token)

    lands = [lax.empty((4,) + a.shape, a.dtype) for a in arrs]
    res = pl.pallas_call(
        body, name=name,
        out_shape=[pltpu.SemaphoreType.DMA((3 * n,)), pltpu.SemaphoreType.DMA((3 * n,))]
        + [pltpu.HBM(a.shape, a.dtype) for a in arrs] + [pltpu.HBM(l.shape, l.dtype) for l in lands] + [SDS((8, 128), F32)],
        in_specs=[HBM_SPEC] * (2 * n), out_specs=[SEM_SPEC, SEM_SPEC] + [HBM_SPEC] * (2 * n) + [pl.BlockSpec(memory_space=pltpu.VMEM)],
        input_output_aliases={i: 2 + i for i in range(2 * n)},
        compiler_params=pltpu.CompilerParams(has_side_effects=DATAFLOW_EFFECT),
    )(*[pltpu.with_memory_space_constraint(a, pltpu.HBM) for a in list(arrs) + lands])
    return res[0], res[1], list(res[2:2 + n]), list(res[2 + n:2 + 2 * n]), res[-1]


def _ag_ici_wait(send_sems, recv_sems, ins, lands, after, name):
    n = len(ins)

    def body(*refs):
        ins_r, lands_r = refs[:n], refs[n:2 * n]
        send_r, recv_r = refs[2 * n:2 * n + 2]
        x, y, c = _position()
        for i in range(n):
            for j, (px, py) in enumerate(_other_chips(x, y)):
                cp = pltpu.make_async_remote_copy(src_ref=ins_r[i].at[c], dst_ref=lands_r[i].at[2 * px + py, c], send_sem=send_r.at[3 * i + j],
                                                  recv_sem=recv_r.at[3 * i + j], device_id=(px, py, c), device_id_type=MESH)
                cp.wait_send()
                cp.wait_recv()

    res = pl.pallas_call(
        body, name=name,
        out_shape=[pltpu.HBM(a.shape, a.dtype) for a in list(ins) + list(lands)],
        in_specs=[HBM_SPEC] * (2 * n) + [SEM_SPEC, SEM_SPEC, pl.BlockSpec(memory_space=pl.ANY)], out_specs=[HBM_SPEC] * (2 * n),
        input_output_aliases={i: i for i in range(2 * n)},
        compiler_params=pltpu.CompilerParams(has_side_effects=DATAFLOW_EFFECT),
    )(*ins, *lands, send_sems, recv_sems, after)
    return list(res[:n]), list(res[n:])


def _ag_finish(arrs, lands, name):
    n = len(arrs)
    hbm = pl.BlockSpec(memory_space=pl.ANY)

    def body(*refs):
        ins, landed, outs = refs[:n], refs[n:2 * n], refs[2 * n:3 * n]
        send_sems, recv_sems = refs[3 * n:]
        x, y, c = _position()
        chips = _other_chips(x, y)
        sends = []
        for i in range(n):
            own = pltpu.make_async_remote_copy(src_ref=ins[i], dst_ref=outs[i].at[2 * x + y], send_sem=send_sems.at[i, 3],
                                               recv_sem=recv_sems.at[i, 3], device_id=(x, y, 1 - c), device_id_type=MESH)
            own.start()
            sends.append(own)
        for i in range(n):
            for j, (px, py) in enumerate(chips):
                fwd = pltpu.make_async_remote_copy(src_ref=landed[i].at[2 * px + py, c], dst_ref=outs[i].at[2 * px + py, c],
                                                   send_sem=send_sems.at[i, j], recv_sem=recv_sems.at[i, j],
                                                   device_id=(x, y, 1 - c), device_id_type=MESH)
                fwd.start()
                sends.append(fwd)
        for i in range(n):
            for j, (px, py) in enumerate(chips):
                theirs = outs[i].at[2 * px + py, 1 - c]
                pltpu.make_async_remote_copy(src_ref=theirs, dst_ref=theirs, send_sem=send_sems.at[i, j], recv_sem=recv_sems.at[i, j],
                                             device_id=(x, y, c), device_id_type=MESH).wait_recv()
        for i in range(n):
            pltpu.make_async_remote_copy(src_ref=ins[i], dst_ref=outs[i].at[2 * x + y], send_sem=send_sems.at[i, 3],
                                         recv_sem=recv_sems.at[i, 3], device_id=(x, y, c), device_id_type=MESH).wait_recv()
        for cp in sends:
            cp.wait_send()

    return pl.pallas_call(
        body, name=name, in_specs=[hbm] * (2 * n), out_specs=[hbm] * n,
        out_shape=[SDS(l.shape, l.dtype) for l in lands],
        input_output_aliases={n + i: i for i in range(n)},
        scratch_shapes=[pltpu.SemaphoreType.DMA((n, 4)), pltpu.SemaphoreType.DMA((n, 4))])(*arrs, *lands)


def _swap_halves(gs, name):
    n = len(gs)
    hbm = pl.BlockSpec(memory_space=pl.ANY)

    def body(*refs):
        ins, outs = refs[:n], refs[n:2 * n]
        send_sems, recv_sems = refs[2 * n:]
        x, y, c = _position()
        cps = []
        for i in range(n):
            for p in range(4):
                cp = pltpu.make_async_remote_copy(src_ref=ins[i].at[p, 1 - c], dst_ref=outs[i].at[p], send_sem=send_sems.at[4 * i + p],
                                                  recv_sem=recv_sems.at[4 * i + p], device_id=(x, y, 1 - c), device_id_type=MESH)
                cp.start()
                cps.append(cp)
        for cp in cps:
            cp.wait()

    return pl.pallas_call(
        body, name=name, in_specs=[hbm] * n, out_specs=[hbm] * n,
        out_shape=[SDS((4,) + g.shape[2:], g.dtype) for g in gs],
        scratch_shapes=[pltpu.SemaphoreType.DMA((4 * n,)), pltpu.SemaphoreType.DMA((4 * n,))])(*gs)


def _exchange_start(s1s, name):
    n = len(s1s)

    def body(*refs):
        srcs, lands = refs[:n], refs[n:2 * n]
        send_sems, recv_sems = refs[2 * n:2 * n + 2]
        token = refs[-1]
        x, y, c = _position()
        for i in range(n):
            for j, (px, py) in enumerate(_other_chips(x, y)):
                pltpu.make_async_remote_copy(src_ref=srcs[i].at[2 * px + py], dst_ref=lands[i].at[j], send_sem=send_sems.at[3 * i + j],
                                             recv_sem=recv_sems.at[3 * i + j], device_id=(px, py, c), device_id_type=MESH).start()
        token[...] = jnp.zeros_like(token)

    lands = [lax.empty((3,) + s.shape[1:], F32) for s in s1s]
    res = pl.pallas_call(
        body, name=name,
        out_shape=[pltpu.SemaphoreType.DMA((3 * n,)), pltpu.SemaphoreType.DMA((3 * n,))]
        + [pltpu.HBM(a.shape, F32) for a in list(s1s) + lands] + [SDS((8, 128), F32)],
        in_specs=[HBM_SPEC] * (2 * n), out_specs=[SEM_SPEC, SEM_SPEC] + [HBM_SPEC] * (2 * n) + [pl.BlockSpec(memory_space=pltpu.VMEM)],
        input_output_aliases={i: 2 + i for i in range(2 * n)},
        compiler_params=pltpu.CompilerParams(has_side_effects=DATAFLOW_EFFECT),
    )(*[pltpu.with_memory_space_constraint(a, pltpu.HBM) for a in list(s1s) + lands])
    return res[0], res[1], list(res[2:2 + n]), list(res[2 + n:2 + 2 * n]), res[-1]


def _exchange_wait(send_sems, recv_sems, s1s, lands, after, name):
    n = len(s1s)

    def body(*refs):
        srcs, lands_r = refs[:n], refs[n:2 * n]
        send_r, recv_r = refs[2 * n:2 * n + 2]
        x, y, c = _position()
        for i in range(n):
            for j, (px, py) in enumerate(_other_chips(x, y)):
                cp = pltpu.make_async_remote_copy(src_ref=srcs[i].at[2 * px + py], dst_ref=lands_r[i].at[j], send_sem=send_r.at[3 * i + j],
                                                  recv_sem=recv_r.at[3 * i + j], device_id=(px, py, c), device_id_type=MESH)
                cp.wait_send()
                cp.wait_recv()

    res = pl.pallas_call(
        body, name=name, out_shape=[pltpu.HBM(a.shape, F32) for a in list(s1s) + list(lands)],
        in_specs=[HBM_SPEC] * (2 * n) + [SEM_SPEC, SEM_SPEC, pl.BlockSpec(memory_space=pl.ANY)], out_specs=[HBM_SPEC] * (2 * n),
        input_output_aliases={i: i for i in range(2 * n)},
        compiler_params=pltpu.CompilerParams(has_side_effects=DATAFLOW_EFFECT),
    )(*s1s, *lands, send_sems, recv_sems, after)
    return list(res[:n]), list(res[n:])


def _join_halves(f2s, name):
    n = len(f2s)
    hbm = pl.BlockSpec(memory_space=pl.ANY)

    def body(*refs):
        ins, outs = refs[:n], refs[n:2 * n]
        send_sems, recv_sems = refs[2 * n:]
        x, y, c = _position()
        cps = []
        for i in range(n):
            cp = pltpu.make_async_remote_copy(src_ref=ins[i].at[c], dst_ref=outs[i].at[c], send_sem=send_sems.at[i],
                                              recv_sem=recv_sems.at[i], device_id=(x, y, 1 - c), device_id_type=MESH)
            cp.start()
            cps.append(cp)
        for i in range(n):
            pltpu.make_async_remote_copy(src_ref=ins[i].at[1 - c], dst_ref=outs[i].at[1 - c], send_sem=send_sems.at[i],
                                         recv_sem=recv_sems.at[i], device_id=(x, y, 1 - c), device_id_type=MESH).wait_recv()
        for cp in cps:
            cp.wait_send()

    return pl.pallas_call(
        body, name=name, in_specs=[hbm] * n, out_specs=[hbm] * n, out_shape=[SDS(f.shape, f.dtype) for f in f2s],
        input_output_aliases={i: i for i in range(n)},
        scratch_shapes=[pltpu.SemaphoreType.DMA((n,)), pltpu.SemaphoreType.DMA((n,))])(*f2s)


def _sum_tile(rows, cols):
    return _row_tile(rows, max(8, (2 ** 19 // cols) // 8 * 8))


def _add_pair(g, r1, c, name):
    _, R, C = r1.shape
    tr = _sum_tile(R, C)

    def body(c_ref, a_ref, b_ref, o_ref):
        del c_ref
        o_ref[...] = a_ref[...] + b_ref[...]

    blk = pl.BlockSpec((None, tr, C), lambda p, i, cr: (p, i, 0))
    return pl.pallas_call(
        body, name=name,
        grid_spec=pltpu.PrefetchScalarGridSpec(
            num_scalar_prefetch=1, grid=(4, R // tr),
            in_specs=[pl.BlockSpec((None, None, tr, C), lambda p, i, cr: (p, cr[0], i, 0)), blk], out_specs=blk),
        out_shape=SDS((4, R, C), F32))(c, g, r1)


def _add_four(s1, r2, me_c, name):
    _, R, C = s1.shape
    tr = _sum_tile(R, C)

    def body(m_ref, a_ref, b_ref, o_ref):
        del m_ref
        o_ref[...] = ((a_ref[...] + b_ref[0]) + b_ref[1]) + b_ref[2]

    return pl.pallas_call(
        body, name=name,
        grid_spec=pltpu.PrefetchScalarGridSpec(
            num_scalar_prefetch=1, grid=(R // tr,),
            in_specs=[pl.BlockSpec((None, tr, C), lambda i, mr: (mr[0], i, 0)), pl.BlockSpec((3, tr, C), lambda i, mr: (0, i, 0))],
            out_specs=pl.BlockSpec((None, tr, C), lambda i, mr: (mr[1], i, 0))),
        out_shape=SDS((2, R, C), F32))(me_c, s1, r2)


def _all_reduce_small(vs):
    n = len(vs)

    def body(*refs):
        ins, outs, bufs = refs[:n], refs[n:2 * n], refs[2 * n:3 * n]
        send_sems, recv_sems = refs[3 * n:]
        x, y, c = _position()
        me = 4 * x + 2 * y + c
        for i in range(n):
            bufs[i][me] = ins[i][...]
        cps = []
        for k in range(1, 8):
            to = (1 - x if k & 4 else x, 1 - y if k & 2 else y, 1 - c if k & 1 else c)
            for i in range(n):
                cp = pltpu.make_async_remote_copy(src_ref=bufs[i].at[me], dst_ref=bufs[i].at[me], send_sem=send_sems.at[7 * i + k - 1],
                                                  recv_sem=recv_sems.at[7 * i + k - 1], device_id=to, device_id_type=MESH)
                cp.start()
                cps.append(cp)
        for cp in cps:
            cp.wait_send()
        for k in range(1, 8):
            src = 4 * (1 - x if k & 4 else x) + 2 * (1 - y if k & 2 else y) + (1 - c if k & 1 else c)
            for i in range(n):
                pltpu.make_async_remote_copy(src_ref=bufs[i].at[src], dst_ref=bufs[i].at[src], send_sem=send_sems.at[7 * i + k - 1],
                                             recv_sem=recv_sems.at[7 * i + k - 1], device_id=(x, y, c), device_id_type=MESH).wait_recv()
        for i in range(n):
            acc = bufs[i][0]
            for k in range(1, 8):
                acc = acc + bufs[i][k]
            outs[i][...] = acc

    vm = pl.BlockSpec(memory_space=pltpu.VMEM)
    return pl.pallas_call(
        body, name="all_reduce_small", in_specs=[vm] * n, out_specs=[vm] * n, out_shape=[SDS(v.shape, F32) for v in vs],
        scratch_shapes=[pltpu.VMEM((8,) + v.shape, F32) for v in vs]
        + [pltpu.SemaphoreType.DMA((7 * n,)), pltpu.SemaphoreType.DMA((7 * n,))])(*vs)


def _reduce_begin(grads, tag):
    _, _, c = _position()
    cs = jnp.reshape(c, (1,)).astype(jnp.int32)
    g4 = [g.reshape(4, 2, g.shape[1] // 2, g.shape[2]) for g in grads]
    r1 = _swap_halves(g4, "rs_swap_" + tag)
    s1 = [_add_pair(g, r, cs, f"rs_add_pair_{tag}{i}") for i, (g, r) in enumerate(zip(g4, r1))]
    send_sems, recv_sems, s1, lands, token = _exchange_start(s1, "rs_exchange_start_" + tag)
    return (send_sems, recv_sems, s1, lands), token


def _reduce_end(state, after, tag):
    x, y, c = _position()
    send_sems, recv_sems, s1, lands = state
    s1, lands = _exchange_wait(send_sems, recv_sems, s1, lands, after, "rs_exchange_wait_" + tag)
    me_c = jnp.stack([2 * x + y, c]).astype(jnp.int32)
    f2 = [_add_four(s, l, me_c, f"rs_add_four_{tag}{i}") for i, (s, l) in enumerate(zip(s1, lands))]
    return [f.reshape(2 * f.shape[1], f.shape[2]) for f in _join_halves(f2, "rs_join_" + tag)]


def _halves(a):
    return a.reshape((2, a.shape[0] // 2) + a.shape[1:])


def _after(a, token):
    return a + token[0, 0]


def _in_proj_parts(xn, w, parts, z, name):
    M, K = xn.shape
    G, _, n = w.shape
    P = parts.shape[0]

    def body(p_ref, a_ref, b_ref, *rest):
        del p_ref
        rest[-1][...] = jnp.dot(a_ref[...], b_ref[...], preferred_element_type=F32).astype(BF16)

    in_specs = [pl.BlockSpec((MM_TM, K), lambda g, i, pr: (i, 0)),
                pl.BlockSpec((None, K, n), (lambda g, i, pr: (pr[g], 0, 0)) if G > 1 else (lambda g, i, pr: (0, 0, 0)))]
    args = [parts, xn, w]
    if z is not None:
        in_specs.append(pl.BlockSpec(memory_space=pl.ANY))
        args.append(z)
    return pl.pallas_call(
        body, name=name,
        grid_spec=pltpu.PrefetchScalarGridSpec(
            num_scalar_prefetch=1, grid=(P, M // MM_TM), in_specs=in_specs,
            out_specs=pl.BlockSpec((MM_TM, n), lambda g, i, pr: (i, pr[g]))),
        out_shape=SDS((M, 4 * n), BF16), input_output_aliases={3: 0} if z is not None else {})(*args)


def _local_step(x, mem, target, sp, ex):
    S, D = x.shape
    band, buckets = _bias_static()
    buckets = jnp.asarray(buckets)

    tok = ex.start_first()
    xn = _rmsnorm(x, _after(sp["attn_norm_w"], tok), "rms_in")
    own, me, others = ex.own_w_in()
    z = _in_proj_parts(xn, own, me, None, "in_proj_own")
    w_in, gathered_small = ex.first_weights(after=z)
    sp = {**sp, **gathered_small}
    n_in = 4 * w_in.shape[2]
    bw = {"w_in": w_in}
    z = _in_proj_parts(xn, w_in, others + ex.start_rest()[0, 0].astype(jnp.int32), z, "in_proj_rest")
    qh, kh, vh = _qkv_prep(z, sp["q_norm_w"], sp["k_norm_w"])
    tab = sp["rel_bias_table"].T.reshape(N_GROUPS, HPG, N_BUCKETS)
    bias = _bias_fwd(jnp.pad(tab, ((0, 0), (0, 8 - HPG), (0, 0))), buckets)
    biasm = jnp.where(jnp.asarray(band)[None, None], bias[:, :HPG].reshape(N_GROUPS, HPG, NQ, 2 * NQ), NEG)
    os_, lses = [], []
    for g, (_, dil) in enumerate(ATTN_GROUPS):
        o_g, lse_g = _attn_fwd(qh, kh, vh, biasm[g], g, dil)
        os_.append(o_g)
        lses.append(lse_g)
    attn = _merge_fwd(os_, lses)
    u1, u3 = _conv_fwd(z, sp["conv_dw_w"], sp["conv_dw_b"], sp["conv_ln_w"], sp["conv_ln_b"])
    bw.update(ex.rest_weights(after=attn))
    F2 = 4 * bw["w_up"].shape[2]
    mn, kv, mk, mv = _memkv_fwd(mem, sp["mem_norm_w"], bw["w_mem_kv"], sp["xk_norm_w"])
    oc = _cross_fwd(z, sp["xq_norm_w"], mk, mv)
    h1, hn = _outproj_fwd(x, z, sp["b_gate"], attn, u3, oc, bw["w_attn_o"], bw["w_conv_o"], bw["w_cross_o"], bw["w_out"],
                          sp["ffn_norm_w"])
    up = _mm_nn(hn, bw["w_up"], BF16, "ffn_up")
    act = _ffn_act_fwd(up, sp["ffn_conv_w"], sp["ffn_conv_b"])
    dy, loss_tile = _ffn_down_loss(act, bw["w_down"], h1, target)

    gs, gb = {}, {}
    dca, dcg, acca, accg, gb["w_down"] = _ffn_bwd_a(dy, bw["w_down"], up, sp["ffn_conv_w"], sp["ffn_conv_b"])
    tap = lambda acc: jnp.transpose(acc, (1, 0, 2)).reshape(8, F2 // 2)
    ta, tg = tap(acca), tap(accg)
    gs["ffn_conv_w"] = jnp.concatenate([ta[:FFN_K], tg[:FFN_K]], axis=1)
    gs["ffn_conv_b"] = jnp.concatenate([ta[FFN_K:FFN_K + 1], tg[FFN_K:FFN_K + 1]], axis=1)
    dup = _ffn_bwd_b(dca, dcg, sp["ffn_conv_w"])
    gb["w_up"] = _mm_tn(hn, dup, 4, "dw_up")
    tok = ex.reduce_begin("a", ("w_down", "w_up"), gb)
    dh1, gs["ffn_norm_w"] = _norm_in_bwd(dup, bw["w_up"], h1, _after(sp["ffn_norm_w"], tok), dy, "ffn_in_bwd")
    dz, dattn, du3, doc, gs["b_gate"], gb["w_out"], gb["w_attn_o"], gb["w_conv_o"], gb["w_cross_o"] = _outproj_bwd(
        dh1, z, sp["b_gate"], attn, u3, oc, bw["w_attn_o"], bw["w_conv_o"], bw["w_cross_o"], bw["w_out"], n_in)
    dz, dmk, dmv, gs["xq_norm_w"] = _cross_bwd(dz, doc, z, sp["xq_norm_w"], mk, mv)
    gb["w_mem_kv"], gs["xk_norm_w"], gs["mem_norm_w"] = _memkv_bwd(
        dmk, dmv, kv, mem, mn, sp["mem_norm_w"], bw["w_mem_kv"], sp["xk_norm_w"])
    ex.reduce_end("a", after=gs["mem_norm_w"])
    tok = ex.reduce_begin("b", ("w_out", "w_attn_o", "w_conv_o", "w_cross_o", "w_mem_kv"), gb)
    du1, cacc = _conv_bwd_a(du3, u1, z, _after(sp["conv_ln_w"], tok), sp["conv_ln_b"])
    gs["conv_dw_w"], gs["conv_dw_b"] = cacc[:CONV_K], cacc[32:33]
    gs["conv_ln_w"], gs["conv_ln_b"] = cacc[33:34], cacc[34:35]
    dz = _conv_bwd_b(dz, du1, z, sp["conv_dw_w"])
    wg, dah, dhb = _merge_bwd(dattn, os_, lses)
    dqs, dks, dvs, dsbs = [], [], [], []
    for g, (_, dil) in enumerate(ATTN_GROUPS):
        dq_g, dk_g, dv_g, dsb_g = _attn_bwd(qh, kh, vh, biasm[g], dah, wg[g], dhb, lses[g], g, dil)
        dqs.append(dq_g)
        dks.append(dk_g)
        dvs.append(dv_g)
        dsbs.append(dsb_g.reshape(HPG, NQ * 2 * NQ))
    dtab = _bias_bwd(jnp.pad(jnp.stack(dsbs), ((0, 0), (0, 8 - HPG), (0, 0))), buckets)
    gs["rel_bias_table"] = dtab[:, :HPG].reshape(N_GROUPS * HPG, N_BUCKETS).T
    dz, gs["q_norm_w"], gs["k_norm_w"] = _qkv_bwd(dz, z, dqs, dks, dvs, sp["q_norm_w"], sp["k_norm_w"])
    ex.reduce_end("b", after=gs["q_norm_w"])
    gb["w_in"] = _mm_tn(xn, dz, 4, "dw_in")
    tok = ex.reduce_begin("c", ("w_in",), gb)
    dx, gs["attn_norm_w"] = _norm_in_bwd(dz, bw["w_in"], x, _after(sp["attn_norm_w"], tok), dh1, "in_bwd")
    ex.reduce_end("c", after=gs["attn_norm_w"])
    return loss_tile, dx, gs, gb


SMALL = ("rel_bias_table", "attn_norm_w", "b_gate", "q_norm_w", "k_norm_w", "conv_dw_w", "conv_dw_b", "conv_ln_w", "conv_ln_b",
         "mem_norm_w", "xq_norm_w", "xk_norm_w", "ffn_norm_w", "ffn_conv_w", "ffn_conv_b")
SMALL_SHARDED = ("conv_dw_w", "ffn_conv_w")
BIG_COL = ("w_in", "w_attn_o", "w_conv_o", "w_cross_o", "w_up")
BIG_ROW = ("w_mem_kv", "w_out", "w_down")
BIG = BIG_COL + BIG_ROW
WEIGHTS = ("rel_bias_table", "attn_norm_w", "w_in", "b_gate", "q_norm_w", "k_norm_w", "w_attn_o", "conv_dw_w", "conv_dw_b",
           "conv_ln_w", "conv_ln_b", "w_conv_o", "mem_norm_w", "w_mem_kv", "xq_norm_w", "xk_norm_w", "w_cross_o", "w_out",
           "ffn_norm_w", "w_up", "ffn_conv_w", "ffn_conv_b", "w_down")


class _Exchanges:
    REST = tuple(k for k in BIG if k != "w_in")

    def __init__(self, w):
        self.w = w
        self.pending = {}
        self.reduced = {}

    def _whole(self, k, ga):
        ga = ga.reshape((4,) + self.w[k].shape)
        return ga if k in BIG_COL else ga.reshape((4 * self.w[k].shape[0],) + self.w[k].shape[1:])

    def start_first(self):
        self.w_in_local = self.w["w_in"].astype(BF16)
        local = [_halves(self.w_in_local)]
        for k in SMALL_SHARDED:
            flat = jnp.ravel(self.w[k])
            local.append(jnp.pad(flat, (0, (-flat.shape[0]) % 2048)).reshape(2, -1, 128))
        send_sems, recv_sems, ins, lands, token = _ag_ici_start(local, "gather_first_start")
        self.pending["first"] = (send_sems, recv_sems, ins, lands)
        return token

    def own_w_in(self):
        x, y, _ = _position()
        me = 2 * x + y
        others = jnp.stack([me ^ 1, me ^ 2, me ^ 3]).astype(jnp.int32)
        return self.w_in_local[None], jnp.reshape(me, (1,)).astype(jnp.int32), others

    def first_weights(self, after):
        send_sems, recv_sems, ins, lands = self.pending.pop("first")
        ins, lands = _ag_ici_wait(send_sems, recv_sems, ins, lands, after, "gather_first_wait")
        gathered = _ag_finish(ins, lands, "gather_first_finish")
        self.first = gathered[0]
        small = {}
        for k, ga in zip(SMALL_SHARDED, gathered[1:]):
            r, cdim = self.w[k].shape
            parts = ga.reshape(4, -1)[:, :r * cdim].reshape(4, r, cdim)
            small[k] = jnp.transpose(parts, (1, 0, 2)).reshape(r, 4 * cdim)
        return self._whole("w_in", gathered[0]), small

    def start_rest(self):
        local = [_halves(self.w[k].astype(BF16)) for k in self.REST]
        local, _ = lax.optimization_barrier((local, self.first))
        send_sems, recv_sems, ins, lands, token = _ag_ici_start(local, "gather_rest_start")
        self.pending["rest"] = (send_sems, recv_sems, ins, lands)
        return token

    def rest_weights(self, after):
        send_sems, recv_sems, ins, lands = self.pending.pop("rest")
        ins, lands = _ag_ici_wait(send_sems, recv_sems, ins, lands, after, "gather_rest_wait")
        gathered = _ag_finish(ins, lands, "gather_rest_finish")
        return {k: self._whole(k, ga) for k, ga in zip(self.REST, gathered)}

    def reduce_begin(self, tag, names, gb):
        parts = [gb[k].reshape((4,) + self.w[k].shape) for k in names]
        state, token = _reduce_begin(parts, tag)
        self.pending[tag] = (state, names)
        return token

    def reduce_end(self, tag, after):
        state, names = self.pending.pop(tag)
        self.reduced.update(zip(names, _reduce_end(state, after, tag)))


def _step(x, mem, target, w, m, v):
    xi, yi, _ = _position()
    shard = 2 * xi + yi
    ex = _Exchanges(w)
    sp = {k: w[k] for k in SMALL if k not in SMALL_SHARDED}
    loss_tile, dx, gs, _ = _local_step(x, mem, target, sp, ex)
    g_big = ex.reduced

    red = _all_reduce_small([loss_tile] + [gs[k] for k in SMALL])
    loss = red[0][0, 0]
    g_small = dict(zip(SMALL, red[1:]))
    for k in SMALL_SHARDED:
        cdim = w[k].shape[1]
        g_small[k] = lax.dynamic_slice_in_dim(g_small[k], shard * cdim, cdim, axis=1)

    grads, delta, new_m, new_v = {}, {}, {}, {}
    for k in BIG:
        grads[k] = g_big[k]
        delta[k], new_m[k], new_v[k] = _adamw(w[k], g_big[k], m[k], v[k], "adamw_" + k)
    outs = _adamw_small([w[k] for k in SMALL], [g_small[k] for k in SMALL], [m[k] for k in SMALL], [v[k] for k in SMALL])
    for dst, vals in zip((delta, new_m, new_v), outs):
        dst.update(zip(SMALL, vals))
    grads.update(g_small)
    return loss, dx, grads, delta, new_m, new_v


def kernel(x, mem, rel_bias_table, attn_norm_w, w_in, b_gate, q_norm_w, k_norm_w, w_attn_o, conv_dw_w, conv_dw_b, conv_ln_w, conv_ln_b, w_conv_o, mem_norm_w, w_mem_kv, xq_norm_w, xk_norm_w, w_cross_o, w_out, ffn_norm_w, w_up, ffn_conv_w, ffn_conv_b, w_down, loss_target, m_rel_bias_table, m_attn_norm_w, m_w_in, m_b_gate, m_q_norm_w, m_k_norm_w, m_w_attn_o, m_conv_dw_w, m_conv_dw_b, m_conv_ln_w, m_conv_ln_b, m_w_conv_o, m_mem_norm_w, m_w_mem_kv, m_xq_norm_w, m_xk_norm_w, m_w_cross_o, m_w_out, m_ffn_norm_w, m_w_up, m_ffn_conv_w, m_ffn_conv_b, m_w_down, v_rel_bias_table, v_attn_norm_w, v_w_in, v_b_gate, v_q_norm_w, v_k_norm_w, v_w_attn_o, v_conv_dw_w, v_conv_dw_b, v_conv_ln_w, v_conv_ln_b, v_w_conv_o, v_mem_norm_w, v_w_mem_kv, v_xq_norm_w, v_xk_norm_w, v_w_cross_o, v_w_out, v_ffn_norm_w, v_w_up, v_ffn_conv_w, v_ffn_conv_b, v_w_down):
    args = locals()
    def block(name, k):
        a = args[name] if k == "rel_bias_table" else args[name][0]
        return a.reshape(1, -1) if a.ndim == 1 else a

    w = {k: block(k, k) for k in WEIGHTS}
    m = {k: block("m_" + k, k) for k in WEIGHTS}
    v = {k: block("v_" + k, k) for k in WEIGHTS}
    loss, dx, grads, delta, new_m, new_v = _step(x[0], mem[0], loss_target[0], w, m, v)
    out = [loss, dx[None]]
    for d in (grads, delta, new_m, new_v):
        for k in WEIGHTS:
            out.append(d[k].reshape(args[k].shape))
    return tuple(out)
```

```python
import functools
import math

import numpy as np
import jax
import jax.numpy as jnp
from jax import lax
from jax.experimental import pallas as pl
from jax.experimental.pallas import tpu as pltpu

F32, BF16 = jnp.float32, jnp.bfloat16
SDS = jax.ShapeDtypeStruct
MESH = pl.DeviceIdType.MESH

HEAD = 128
N_GROUPS, HPG = 3, 4
ATTN_GROUPS = ((128, 1), (512, 4), (2048, 16))
NQ = 128
QKV_W = N_GROUPS * HPG * HEAD
CH = 512
CONV_K, FFN_K = 31, 3
N_BUCKETS, MAX_DIST = 32, 2048
RMS_EPS, LN_EPS = 1e-6, 1e-5
O_Q, O_K, O_V, O_CV, O_CG, O_XQ, O_G = 0, QKV_W, 2 * QKV_W, 3 * QKV_W, 3 * QKV_W + CH, 3 * QKV_W + 2 * CH, 3 * QKV_W + 3 * CH
ADAM_LR, ADAM_B1, ADAM_B2, ADAM_EPS, ADAM_WD, ADAM_STEP = 0.001, 0.9, 0.999, 1e-08, 0.01, 10
NEG = -1e30
SCALE = HEAD ** -0.5
TM = 512
MM_TM = 1024
ATT_RB = 2048
NT_DIMS = (((1,), (1,)), ((), ()))
TN_DIMS = (((0,), (0,)), ((), ()))


CONV_RC = 32
CONV_LB = 256
CONV_LANES = tuple(slice(l, l + CONV_LB) for l in range(0, CH, CONV_LB))
CONV_HALO = 32
FFN_RC = 32


def _conv_taps(buf, base, lanes, q_lo, q_hi, visit):
    span = CONV_RC + CONV_HALO
    xx = buf[pl.ds(base, span), lanes]
    for s in range(8):
        xs = xx if s == 0 else pltpu.roll(xx, span - s, 0)
        for q in range(s, q_hi + 1, 8):
            if q >= q_lo:
                visit(q, xs[q - s:q - s + CONV_RC])


def _sig(v):
    return 0.5 * jnp.tanh(0.5 * v) + 0.5


def _fold8(v):
    acc = v[0:8]
    for r in range(8, v.shape[0], 8):
        acc = acc + v[r:r + 8]
    return acc


def _row_tile(rows, cap, mult=8):
    best = None
    for t in range(mult, min(rows, cap) + 1, mult):
        if rows % t == 0:
            best = t
    return best if best is not None else rows


def _full(shape):
    n = len(shape)
    return pl.BlockSpec(shape, lambda *a: (0,) * n)


def _rmsnorm(x, w, name):
    S, D = x.shape

    def body(x_ref, w_ref, o_ref):
        xv = x_ref[...]
        r = lax.rsqrt(jnp.mean(xv * xv, axis=-1, keepdims=True) + RMS_EPS)
        o_ref[...] = (xv * r * w_ref[...]).astype(BF16)

    return pl.pallas_call(
        body, name=name, grid=(S // TM,),
        in_specs=[pl.BlockSpec((TM, D), lambda i: (i, 0)), _full((1, D))],
        out_specs=pl.BlockSpec((TM, D), lambda i: (i, 0)),
        out_shape=SDS((S, D), BF16))(x, w)


def _mm_nn(a, b, out_dtype, name):
    M, K = a.shape
    G, _, n = b.shape

    def body(a_ref, b_ref, o_ref):
        o_ref[...] = jnp.dot(a_ref[...].astype(BF16), b_ref[...], preferred_element_type=F32).astype(out_dtype)

    return pl.pallas_call(
        body, name=name, grid=(G, M // MM_TM),
        in_specs=[pl.BlockSpec((MM_TM, K), lambda g, i: (i, 0)), pl.BlockSpec((None, K, n), lambda g, i: (g, 0, 0))],
        out_specs=pl.BlockSpec((MM_TM, n), lambda g, i: (i, g)),
        out_shape=SDS((M, G * n), out_dtype))(a, b)


def _mm_tn(a, b, G, name):
    S, Ka = a.shape
    n = b.shape[1] // G
    tka = Ka
    while tka * n * 4 > 10 * 2 ** 20 and tka % 256 == 0:
        tka //= 2

    def body(a_ref, b_ref, o_ref):
        @pl.when(pl.program_id(2) == 0)
        def _():
            o_ref[...] = jnp.zeros_like(o_ref)
        o_ref[...] += lax.dot_general(a_ref[...].astype(BF16), b_ref[...].astype(BF16), TN_DIMS, preferred_element_type=F32)

    return pl.pallas_call(
        body, name=name, grid=(G, Ka // tka, S // MM_TM),
        in_specs=[pl.BlockSpec((MM_TM, tka), lambda g, i, k: (k, i)), pl.BlockSpec((MM_TM, n), lambda g, i, k: (k, g))],
        out_specs=pl.BlockSpec((None, tka, n), lambda g, i, k: (g, i, 0)),
        out_shape=SDS((G, Ka, n), F32))(a, b)


NORM_RC = 16


def _norm_in_bwd(a, w, xin, nw, resid, name):
    S, K = xin.shape
    G, _, n = w.shape
    tm = 1024 if S % 1024 == 0 and G * K * n * 2 <= 12 * 2 ** 20 else TM
    n2, steps = n, G

    def body(a_ref, w_ref, x_ref, nw_ref, r_ref, o_ref, dnw_ref, acc, part):
        i, g = pl.program_id(0), pl.program_id(1)

        @pl.when((i == 0) & (g == 0))
        def _():
            part[...] = jnp.zeros_like(part)

        @pl.when(g == 0)
        def _():
            acc[...] = jnp.zeros_like(acc)

        acc[...] += lax.dot_general(a_ref[...], w_ref[g], NT_DIMS, preferred_element_type=F32)

        @pl.when(g == steps - 1)
        def _():
            for r0 in range(0, tm, NORM_RC):
                rows = pl.ds(r0, NORM_RC)
                dn = acc[rows, :]
                xv = x_ref[rows, :]
                r = lax.rsqrt(jnp.mean(xv * xv, axis=-1, keepdims=True) + RMS_EPS)
                xhat = xv * r
                dyw = dn * nw_ref[...]
                o_ref[rows, :] = r_ref[rows, :] + r * (dyw - xhat * jnp.mean(dyw * xhat, axis=-1, keepdims=True))
                part[...] += _fold8(dn * xhat)

        @pl.when((i == S // tm - 1) & (g == steps - 1))
        def _():
            dnw_ref[...] = jnp.sum(part[...], axis=0, keepdims=True)

    return pl.pallas_call(
        body, name=name, grid=(S // tm, steps),
        in_specs=[pl.BlockSpec((tm, n2), lambda i, g: (i, g)),
                  pl.BlockSpec((G, K, n), lambda i, g: (0, 0, 0), pipeline_mode=pl.Buffered(1)),
                  pl.BlockSpec((tm, K), lambda i, g: (i, 0)), _full((1, K)), pl.BlockSpec((tm, K), lambda i, g: (i, 0))],
        out_specs=[pl.BlockSpec((tm, K), lambda i, g: (i, 0)), _full((1, K))],
        out_shape=[SDS((S, K), F32), SDS((1, K), F32)],
        scratch_shapes=[pltpu.VMEM((tm, K), F32), pltpu.VMEM((8, K), F32)])(a, w, xin, nw, resid)


def _t5_bucket_np(dist):
    max_exact = N_BUCKETS // 2
    d = np.maximum(dist.astype(np.float32), np.float32(1.0))
    large = max_exact + (np.log(d / np.float32(max_exact)) / np.float32(math.log(MAX_DIST / max_exact))
                         * np.float32(N_BUCKETS - max_exact)).astype(np.int32)
    large = np.minimum(large, N_BUCKETS - 1)
    return np.where(dist < max_exact, dist, large).astype(np.int32)


def _bias_static():
    qi = np.arange(NQ)[:, None]
    kj = np.arange(2 * NQ)[None, :]
    step = qi + NQ - kj
    band = (step >= 0) & (step <= NQ)
    buckets = np.stack([_t5_bucket_np(np.clip(step, 0, None) * dil).reshape(1, -1) for _, dil in ATTN_GROUPS])
    return band, buckets


def _bias_fwd(table_t, buckets):
    nb = buckets.shape[-1]

    def body(t_ref, b_ref, o_ref):
        oh = (b_ref[...] == lax.broadcasted_iota(jnp.int32, (N_BUCKETS, nb), 0)).astype(F32)
        o_ref[...] = jnp.dot(t_ref[...], oh, preferred_element_type=F32, precision=lax.Precision.HIGHEST)

    return pl.pallas_call(
        body, name="bias_fwd", grid=(N_GROUPS,),
        in_specs=[pl.BlockSpec((None, 8, N_BUCKETS), lambda g: (g, 0, 0)), pl.BlockSpec((None, 1, nb), lambda g: (g, 0, 0))],
        out_specs=pl.BlockSpec((None, 8, nb), lambda g: (g, 0, 0)),
        out_shape=SDS((N_GROUPS, 8, nb), F32))(table_t, buckets)


def _bias_bwd(dsb, buckets):
    nb = buckets.shape[-1]

    def body(d_ref, b_ref, o_ref):
        oh = (b_ref[...] == lax.broadcasted_iota(jnp.int32, (N_BUCKETS, nb), 0)).astype(F32)
        o_ref[...] = lax.dot_general(d_ref[...], oh, NT_DIMS, preferred_element_type=F32, precision=lax.Precision.HIGHEST)

    return pl.pallas_call(
        body, name="bias_bwd", grid=(N_GROUPS,),
        in_specs=[pl.BlockSpec((None, 8, nb), lambda g: (g, 0, 0)), pl.BlockSpec((None, 1, nb), lambda g: (g, 0, 0))],
        out_specs=pl.BlockSpec((None, 8, N_BUCKETS), lambda g: (g, 0, 0)),
        out_shape=SDS((N_GROUPS, 8, N_BUCKETS), F32))(dsb, buckets)


def _qkv_prep(z, qw, kw):
    S = z.shape[0]
    nh = N_GROUPS * HPG

    def body(zq, zk, zv, qw_ref, kw_ref, qh, kh, vh):
        for h in range(nh):
            g = h // HPG
            sl = slice(h * HEAD, (h + 1) * HEAD)
            xq = zq[:, sl].astype(F32)
            qh[h] = xq * lax.rsqrt(jnp.mean(xq * xq, axis=-1, keepdims=True) + RMS_EPS) * qw_ref[g:g + 1, :]
            xk = zk[:, sl].astype(F32)
            kh[h] = xk * lax.rsqrt(jnp.mean(xk * xk, axis=-1, keepdims=True) + RMS_EPS) * kw_ref[g:g + 1, :]
            vh[h] = zv[:, sl].astype(F32)

    hm = pl.BlockSpec((nh, TM, HEAD), lambda i: (0, i, 0))
    return pl.pallas_call(
        body, name="qkv_prep", grid=(S // TM,),
        in_specs=[pl.BlockSpec((TM, QKV_W), lambda i: (i, 0)), pl.BlockSpec((TM, QKV_W), lambda i: (i, 1)),
                  pl.BlockSpec((TM, QKV_W), lambda i: (i, 2)), _full((N_GROUPS, HEAD)), _full((N_GROUPS, HEAD))],
        out_specs=[hm, hm, hm],
        out_shape=[SDS((nh, S, HEAD), F32)] * 3)(z, z, z, qw, kw)


def _rows(start, d):
    return pl.ds(start, NQ) if d == 1 else pl.ds(start, NQ, stride=d)


def _attn_fwd(qh, kh, vh, biasm, g, d):
    S = qh.shape[1]
    RB = ATT_RB
    nbk, nq = S // RB, RB // (NQ * d)

    def body(q_ref, k_ref, v_ref, bias_ref, o_ref, lse_ref, kbuf, vbuf):
        b = pl.program_id(1)

        @pl.when(b == 0)
        def _():
            kbuf[0:RB, :] = jnp.zeros((RB, HEAD), F32)
            vbuf[0:RB, :] = jnp.zeros((RB, HEAD), F32)

        @pl.when(b > 0)
        def _():
            kbuf[0:RB, :] = kbuf[RB:2 * RB, :]
            vbuf[0:RB, :] = vbuf[RB:2 * RB, :]

        kbuf[RB:2 * RB, :] = k_ref[...]
        vbuf[RB:2 * RB, :] = v_ref[...]
        bias = bias_ref[...]
        col = lax.broadcasted_iota(jnp.int32, (NQ, 2 * NQ), 1)

        for qb in range(nq):
            def unit(r, carry, qb=qb):
                qs = qb * NQ * d + r
                q = q_ref[_rows(qs, d), :].astype(BF16)
                kw = jnp.concatenate([kbuf[_rows(RB + qs - NQ * d, d), :], kbuf[_rows(RB + qs, d), :]], axis=0).astype(BF16)
                vw = jnp.concatenate([vbuf[_rows(RB + qs - NQ * d, d), :], vbuf[_rows(RB + qs, d), :]], axis=0).astype(BF16)
                s = lax.dot_general(q, kw, NT_DIMS, preferred_element_type=F32) * SCALE + bias
                if qb == 0:
                    s = jnp.where((col < NQ) & (b == 0), NEG, s)
                m = jnp.max(s, axis=-1, keepdims=True)
                p = jnp.exp(s - m)
                l = jnp.sum(p, axis=-1, keepdims=True)
                o = jnp.dot(p.astype(BF16), vw, preferred_element_type=F32) / l
                o_ref[_rows(qs, d), :] = o
                lse_ref[_rows(qs, d), :] = jnp.broadcast_to(m + jnp.log(l), (NQ, HEAD))
                return carry

            for r in range(d):
                unit(r, 0)

    blk = lambda f: pl.BlockSpec((None, RB, HEAD), f)
    return pl.pallas_call(
        body, name=f"attn_fwd_g{g}", grid=(HPG, nbk),
        in_specs=[blk(lambda h, b: (HPG * g + h, b, 0))] * 3 + [pl.BlockSpec((None, NQ, 2 * NQ), lambda h, b: (h, 0, 0))],
        out_specs=[blk(lambda h, b: (h, b, 0))] * 2,
        out_shape=[SDS((HPG, S, HEAD), F32)] * 2,
        scratch_shapes=[pltpu.VMEM((2 * RB, HEAD), F32)] * 2)(qh, kh, vh, biasm)


def _attn_bwd(qh, kh, vh, biasm, da, wg, dh, lse, g, d):
    S = qh.shape[1]
    RB = ATT_RB
    nbk, nq = S // RB, RB // (NQ * d)

    def body(q_ref, k_ref, v_ref, bias_ref, da_ref, wg_ref, dh_ref, lse_ref,
             dq_ref, dk_ref, dv_ref, dsb_ref, kbuf, vbuf, dkbuf, dvbuf):
        b = pl.program_id(1)
        zero = jnp.zeros((RB, HEAD), F32)

        @pl.when(b == 0)
        def _():
            kbuf[0:RB, :] = zero
            vbuf[0:RB, :] = zero
            dkbuf[0:RB, :] = zero
            dvbuf[0:RB, :] = zero
            dsb_ref[...] = jnp.zeros_like(dsb_ref)

        @pl.when(b > 0)
        def _():
            kbuf[0:RB, :] = kbuf[RB:2 * RB, :]
            vbuf[0:RB, :] = vbuf[RB:2 * RB, :]
            dkbuf[0:RB, :] = dkbuf[RB:2 * RB, :]
            dvbuf[0:RB, :] = dvbuf[RB:2 * RB, :]

        dkbuf[RB:2 * RB, :] = zero
        dvbuf[RB:2 * RB, :] = zero

        @pl.when(b < nbk)
        def _():
            kbuf[RB:2 * RB, :] = k_ref[...]
            vbuf[RB:2 * RB, :] = v_ref[...]
            bias = bias_ref[...]
            col = lax.broadcasted_iota(jnp.int32, (NQ, 2 * NQ), 1)

            for qb in range(nq):
                def unit(r, carry, qb=qb):
                    qs = qb * NQ * d + r
                    prev, cur = _rows(RB + qs - NQ * d, d), _rows(RB + qs, d)
                    q = q_ref[_rows(qs, d), :].astype(BF16)
                    kw = jnp.concatenate([kbuf[prev, :], kbuf[cur, :]], axis=0).astype(BF16)
                    vw = jnp.concatenate([vbuf[prev, :], vbuf[cur, :]], axis=0).astype(BF16)
                    s = lax.dot_general(q, kw, NT_DIMS, preferred_element_type=F32) * SCALE + bias
                    if qb == 0:
                        s = jnp.where((col < NQ) & (b == 0), NEG, s)
                    p = jnp.exp(s - lse_ref[_rows(qs, d), :][:, 0:1])
                    w = wg_ref[_rows(qs, d), :]
                    do = (da_ref[_rows(qs, d), :] * w).astype(BF16)
                    dp = lax.dot_general(do, vw, NT_DIMS, preferred_element_type=F32)
                    ds = p * (dp - w[:, 0:1] * dh_ref[_rows(qs, d), :][:, 0:1])
                    dsb_ref[...] += ds
                    dsb = ds.astype(BF16)
                    dq_ref[_rows(qs, d), :] = jnp.dot(dsb, kw, preferred_element_type=F32) * SCALE
                    dkw = lax.dot_general(dsb, q, TN_DIMS, preferred_element_type=F32) * SCALE
                    dvw = lax.dot_general(p.astype(BF16), do, TN_DIMS, preferred_element_type=F32)
                    dkbuf[prev, :] += dkw[0:NQ, :]
                    dkbuf[cur, :] += dkw[NQ:2 * NQ, :]
                    dvbuf[prev, :] += dvw[0:NQ, :]
                    dvbuf[cur, :] += dvw[NQ:2 * NQ, :]
                    return carry

                for r in range(d):
                    unit(r, 0)

        dk_ref[...] = dkbuf[0:RB, :]
        dv_ref[...] = dvbuf[0:RB, :]

    blk = lambda f: pl.BlockSpec((None, RB, HEAD), f)
    cur_g = blk(lambda h, b: (HPG * g + h, jnp.minimum(b, nbk - 1), 0))
    cur = blk(lambda h, b: (h, jnp.minimum(b, nbk - 1), 0))
    prv = blk(lambda h, b: (h, jnp.maximum(b - 1, 0), 0))
    sq = pl.BlockSpec((None, NQ, 2 * NQ), lambda h, b: (h, 0, 0))
    return pl.pallas_call(
        body, name=f"attn_bwd_g{g}", grid=(HPG, nbk + 1),
        in_specs=[cur_g, cur_g, cur_g, sq, cur, cur, cur, cur],
        out_specs=[cur, prv, prv, sq],
        out_shape=[SDS((HPG, S, HEAD), F32)] * 3 + [SDS((HPG, NQ, 2 * NQ), F32)],
        scratch_shapes=[pltpu.VMEM((2 * RB, HEAD), F32)] * 4)(qh, kh, vh, biasm, da, wg, dh, lse)


def _merge_weights(l0, l1, l2):
    m = jnp.maximum(jnp.maximum(l0, l1), l2)
    e0, e1, e2 = jnp.exp(l0 - m), jnp.exp(l1 - m), jnp.exp(l2 - m)
    inv = 1.0 / (e0 + e1 + e2)
    return e0 * inv, e1 * inv, e2 * inv


def _merge_fwd(os_, lses):
    S = os_[0].shape[1]

    def body(o0, o1, o2, l0, l1, l2, a_ref):
        for h in range(HPG):
            w0, w1, w2 = _merge_weights(l0[h], l1[h], l2[h])
            a_ref[:, h * HEAD:(h + 1) * HEAD] = (w0 * o0[h] + w1 * o1[h] + w2 * o2[h]).astype(BF16)

    hm = pl.BlockSpec((HPG, TM, HEAD), lambda i: (0, i, 0))
    return pl.pallas_call(
        body, name="merge_fwd", grid=(S // TM,), in_specs=[hm] * 6,
        out_specs=pl.BlockSpec((TM, CH), lambda i: (i, 0)),
        out_shape=SDS((S, CH), BF16))(*os_, *lses)


def _merge_bwd(dattn, os_, lses):
    S = dattn.shape[0]

    def body(da_ref, o0, o1, o2, l0, l1, l2, w0_ref, w1_ref, w2_ref, dah_ref, dh_ref):
        for h in range(HPG):
            w = _merge_weights(l0[h], l1[h], l2[h])
            attn = w[0] * o0[h] + w[1] * o1[h] + w[2] * o2[h]
            da = da_ref[:, h * HEAD:(h + 1) * HEAD]
            for w_ref, wv in zip((w0_ref, w1_ref, w2_ref), w):
                w_ref[h] = wv
            dah_ref[h] = da
            dh_ref[h] = jnp.broadcast_to(jnp.sum(da * attn, axis=-1, keepdims=True), (TM, HEAD))

    hm = pl.BlockSpec((HPG, TM, HEAD), lambda i: (0, i, 0))
    res = pl.pallas_call(
        body, name="merge_bwd", grid=(S // TM,),
        in_specs=[pl.BlockSpec((TM, CH), lambda i: (i, 0))] + [hm] * 6,
        out_specs=[hm] * 5, out_shape=[SDS((HPG, S, HEAD), F32)] * 5)(dattn, *os_, *lses)
    return res[0:3], res[3], res[4]


def _qkv_bwd(dz, z, dqs, dks, dvs, qw, kw):
    S = z.shape[0]
    nh = N_GROUPS * HPG

    def body(dz_in, zq, zk, *refs):
        del dz_in
        dq_refs, dk_refs, dv_refs = refs[0:3], refs[3:6], refs[6:9]
        qw_ref, kw_ref, dz_ref, dqw_ref, dkw_ref = refs[9:]

        @pl.when(pl.program_id(0) == 0)
        def _():
            dqw_ref[...] = jnp.zeros_like(dqw_ref)
            dkw_ref[...] = jnp.zeros_like(dkw_ref)

        def nbwd(xr, dy, wr, dwr, h, off):
            g = h // HPG
            x = xr[:, h * HEAD:(h + 1) * HEAD].astype(F32)
            r = lax.rsqrt(jnp.mean(x * x, axis=-1, keepdims=True) + RMS_EPS)
            xhat = x * r
            dyw = dy * wr[g:g + 1, :]
            dz_ref[:, off + h * HEAD:off + (h + 1) * HEAD] = (
                r * (dyw - xhat * jnp.mean(dyw * xhat, axis=-1, keepdims=True))).astype(BF16)
            dwr[g:g + 1, :] += jnp.sum(dy * xhat, axis=0, keepdims=True)

        for h in range(nh):
            g, hh = h // HPG, h % HPG
            nbwd(zq, dq_refs[g][hh], qw_ref, dqw_ref, h, O_Q)
            nbwd(zk, dk_refs[g][hh], kw_ref, dkw_ref, h, O_K)
            dz_ref[:, O_V + h * HEAD:O_V + (h + 1) * HEAD] = dv_refs[g][hh].astype(BF16)

    hm = pl.BlockSpec((HPG, TM, HEAD), lambda i: (0, i, 0))
    return pl.pallas_call(
        body, name="qkv_bwd", grid=(S // TM,),
        in_specs=[pl.BlockSpec(memory_space=pl.ANY), pl.BlockSpec((TM, QKV_W), lambda i: (i, 0)),
                  pl.BlockSpec((TM, QKV_W), lambda i: (i, 1))] + [hm] * 9 + [_full((N_GROUPS, HEAD)), _full((N_GROUPS, HEAD))],
        out_specs=[pl.BlockSpec((TM, 3 * QKV_W), lambda i: (i, 0)), _full((N_GROUPS, HEAD)), _full((N_GROUPS, HEAD))],
        out_shape=[SDS(dz.shape, BF16), SDS((N_GROUPS, HEAD), F32), SDS((N_GROUPS, HEAD), F32)],
        input_output_aliases={0: 0})(dz, z, z, *dqs, *dks, *dvs, qw, kw)


def _conv_fwd(z, cw, cb, lnw, lnb):
    S = z.shape[0]
    H = 32

    def body(zv, zg, cw_ref, cb_ref, lnw_ref, lnb_ref, u1_ref, u3_ref, xbuf):
        i = pl.program_id(0)

        @pl.when(i == 0)
        def _():
            xbuf[0:H, :] = jnp.zeros((H, CH), F32)

        @pl.when(i > 0)
        def _():
            xbuf[0:H, :] = xbuf[TM:TM + H, :]

        xbuf[H:H + TM, :] = zv[...].astype(F32) * _sig(zg[...].astype(F32))
        for r0 in range(0, TM, CONV_RC):
            rows = pl.ds(r0, CONV_RC)
            parts = []
            for lanes in CONV_LANES:
                part = [jnp.broadcast_to(cb_ref[:, lanes], (CONV_RC, CONV_LB))]

                def tap(q, view, part=part, lanes=lanes):
                    part[0] = part[0] + view * cw_ref[q - 2:q - 1, lanes]

                _conv_taps(xbuf, r0, lanes, 2, CONV_K + 1, tap)
                parts.append(part[0])
            acc = jnp.concatenate(parts, axis=1)
            u1_ref[rows, :] = acc
            mu = jnp.mean(acc, axis=-1, keepdims=True)
            xc = acc - mu
            yl = xc * lax.rsqrt(jnp.mean(xc * xc, axis=-1, keepdims=True) + LN_EPS) * lnw_ref[...] + lnb_ref[...]
            u3_ref[rows, :] = (yl * _sig(yl)).astype(BF16)

    row = pl.BlockSpec((TM, CH), lambda i: (i, 0))
    return pl.pallas_call(
        body, name="conv_fwd", grid=(S // TM,),
        in_specs=[pl.BlockSpec((TM, CH), lambda i: (i, O_CV // CH)), pl.BlockSpec((TM, CH), lambda i: (i, O_CG // CH)),
                  _full((CONV_K, CH)), _full((1, CH)), _full((1, CH)), _full((1, CH))],
        out_specs=[row, row], out_shape=[SDS((S, CH), F32), SDS((S, CH), BF16)],
        scratch_shapes=[pltpu.VMEM((TM + H, CH), F32)])(z, z, cw, cb, lnw, lnb)


def _conv_bwd_a(du3, u1, z, lnw, lnb):
    S = z.shape[0]
    H = 32

    nt = S // TM

    def body(du3_ref, u1_ref, zv, zg, lnw_ref, lnb_ref, du1_ref, acc_ref, xbuf, tacc):
        i = pl.program_id(0)

        @pl.when(i == 0)
        def _():
            xbuf[0:H, :] = jnp.zeros((H, CH), F32)
            tacc[...] = jnp.zeros_like(tacc)

        @pl.when(i > 0)
        def _():
            xbuf[0:H, :] = xbuf[TM:TM + H, :]

        xbuf[H:H + TM, :] = zv[...].astype(F32) * _sig(zg[...].astype(F32))
        for r0 in range(0, TM, CONV_RC):
            rows = pl.ds(r0, CONV_RC)
            u1 = u1_ref[rows, :]
            mu = jnp.mean(u1, axis=-1, keepdims=True)
            xc = u1 - mu
            r = lax.rsqrt(jnp.mean(xc * xc, axis=-1, keepdims=True) + LN_EPS)
            yhat = xc * r
            yl = yhat * lnw_ref[...] + lnb_ref[...]
            sg = _sig(yl)
            dyl = du3_ref[rows, :] * (sg * (1.0 + yl * (1.0 - sg)))
            dyh = dyl * lnw_ref[...]
            du1 = r * (dyh - jnp.mean(dyh, axis=-1, keepdims=True) - yhat * jnp.mean(dyh * yhat, axis=-1, keepdims=True))
            du1_ref[rows, :] = du1
            tacc[33] += _fold8(dyl * yhat)
            tacc[34] += _fold8(dyl)
            tacc[32] += _fold8(du1)
            for lanes in CONV_LANES:
                d = du1[:, lanes]

                def tap(q, view, d=d, lanes=lanes):
                    tacc[q - 2, :, lanes] += _fold8(d * view)

                _conv_taps(xbuf, r0, lanes, 2, CONV_K + 1, tap)

        @pl.when(i == nt - 1)
        def _():
            for k in range(40):
                acc_ref[k:k + 1, :] = jnp.sum(tacc[k], axis=0, keepdims=True)

    row = pl.BlockSpec((TM, CH), lambda i: (i, 0))
    return pl.pallas_call(
        body, name="conv_bwd_a", grid=(nt,),
        in_specs=[row, row, pl.BlockSpec((TM, CH), lambda i: (i, O_CV // CH)), pl.BlockSpec((TM, CH), lambda i: (i, O_CG // CH)),
                  _full((1, CH)), _full((1, CH))],
        out_specs=[row, _full((40, CH))], out_shape=[SDS((S, CH), F32), SDS((40, CH), F32)],
        scratch_shapes=[pltpu.VMEM((TM + H, CH), F32), pltpu.VMEM((40, 8, CH), F32)])(du3, u1, z, z, lnw, lnb)


def _conv_bwd_b(dz, du1, z, cw):
    S = z.shape[0]
    nt = S // TM
    H = 32

    def body(dz_in, du1_ref, zv, zg, cw_ref, dz_ref, ybuf, dgate):
        del dz_in
        i, p = pl.program_id(0), pl.program_id(1)

        @pl.when(p == 0)
        def _():
            @pl.when(i == 0)
            def _():
                ybuf[TM:TM + H, :] = jnp.zeros((H, CH), F32)

            @pl.when(i > 0)
            def _():
                ybuf[TM:TM + H, :] = ybuf[0:H, :]

            ybuf[0:TM, :] = du1_ref[...]
            for r0 in range(0, TM, CONV_RC):
                rows = pl.ds(r0, CONV_RC)
                parts = []
                for lanes in CONV_LANES:
                    part = [jnp.zeros((CONV_RC, CONV_LB), F32)]

                    def tap(q, view, part=part, lanes=lanes):
                        part[0] = part[0] + view * cw_ref[CONV_K - 1 - q:CONV_K - q, lanes]

                    _conv_taps(ybuf, r0, lanes, 0, CONV_K - 1, tap)
                    parts.append(part[0])
                acc = jnp.concatenate(parts, axis=1)
                val = zv[rows, :].astype(F32)
                sg = _sig(zg[rows, :].astype(F32))
                dz_ref[rows, :] = (acc * sg).astype(BF16)
                dgate[rows, :] = (acc * val * sg * (1.0 - sg)).astype(BF16)

        @pl.when(p == 1)
        def _():
            dz_ref[...] = dgate[...]

    rev = lambda c: pl.BlockSpec((TM, CH), lambda i, p: (nt - 1 - i, c))
    return pl.pallas_call(
        body, name="conv_bwd_b", grid=(nt, 2),
        in_specs=[pl.BlockSpec(memory_space=pl.ANY), rev(0), rev(O_CV // CH), rev(O_CG // CH), _full((CONV_K, CH))],
        out_specs=pl.BlockSpec((TM, CH), lambda i, p: (nt - 1 - i, O_CV // CH + p)),
        out_shape=SDS(dz.shape, BF16),
        scratch_shapes=[pltpu.VMEM((TM + H, CH), F32), pltpu.VMEM((TM, CH), BF16)],
        input_output_aliases={0: 0})(dz, du1, z, z, cw)


def _memkv_fwd(mem, mnw, wkv, xkw):
    M, D = mem.shape

    def body(mem_ref, mnw_ref, w_ref, xkw_ref, mn_ref, kv_ref, mk_ref, mv_ref):
        x = mem_ref[...]
        mn = (x * lax.rsqrt(jnp.mean(x * x, axis=-1, keepdims=True) + RMS_EPS) * mnw_ref[...]).astype(BF16)
        mn_ref[...] = mn
        kv = jnp.dot(mn, w_ref[...], preferred_element_type=F32)
        kv_ref[...] = kv
        for h in range(HPG):
            k = kv[:, h * HEAD:(h + 1) * HEAD]
            mk_ref[:, h * HEAD:(h + 1) * HEAD] = (
                k * lax.rsqrt(jnp.mean(k * k, axis=-1, keepdims=True) + RMS_EPS) * xkw_ref[...]).astype(BF16)
        mv_ref[...] = kv[:, CH:2 * CH].astype(BF16)

    return pl.pallas_call(
        body, name="memkv_fwd",
        out_shape=[SDS((M, D), BF16), SDS((M, 2 * CH), F32), SDS((M, CH), BF16), SDS((M, CH), BF16)])(mem, mnw, wkv, xkw)


def _cross_q(zx, xqw, h):
    x = zx[:, h * HEAD:(h + 1) * HEAD].astype(F32)
    r = lax.rsqrt(jnp.mean(x * x, axis=-1, keepdims=True) + RMS_EPS)
    xhat = x * r
    return xhat, r, xhat * xqw


def _cross_fwd(z, xqw, mk, mv):
    S = z.shape[0]
    M = mk.shape[0]

    def body(zx, xqw_ref, mk_ref, mv_ref, o_ref):
        for h in range(HPG):
            sl = slice(h * HEAD, (h + 1) * HEAD)
            _, _, q = _cross_q(zx, xqw_ref[...], h)
            s = lax.dot_general(q.astype(BF16), mk_ref[:, sl], NT_DIMS, preferred_element_type=F32) * SCALE
            e = jnp.exp(s - jnp.max(s, axis=-1, keepdims=True))
            p = e / jnp.sum(e, axis=-1, keepdims=True)
            o_ref[:, sl] = jnp.dot(p.astype(BF16), mv_ref[:, sl], preferred_element_type=F32).astype(BF16)

    return pl.pallas_call(
        body, name="cross_fwd", grid=(S // TM,),
        in_specs=[pl.BlockSpec((TM, CH), lambda i: (i, O_XQ // CH)), _full((1, HEAD)), _full((M, CH)), _full((M, CH))],
        out_specs=pl.BlockSpec((TM, CH), lambda i: (i, 0)), out_shape=SDS((S, CH), BF16))(z, xqw, mk, mv)


def _cross_bwd(dz, doc, z, xqw, mk, mv):
    S = z.shape[0]
    M = mk.shape[0]

    def body(dz_in, do_ref, zx, xqw_ref, mk_ref, mv_ref, dz_ref, dmk_ref, dmv_ref, dxw_ref):
        del dz_in

        @pl.when(pl.program_id(0) == 0)
        def _():
            dmk_ref[...] = jnp.zeros_like(dmk_ref)
            dmv_ref[...] = jnp.zeros_like(dmv_ref)
            dxw_ref[...] = jnp.zeros_like(dxw_ref)

        for h in range(HPG):
            sl = slice(h * HEAD, (h + 1) * HEAD)
            xhat, r, q = _cross_q(zx, xqw_ref[...], h)
            qb = q.astype(BF16)
            s = lax.dot_general(qb, mk_ref[:, sl], NT_DIMS, preferred_element_type=F32) * SCALE
            e = jnp.exp(s - jnp.max(s, axis=-1, keepdims=True))
            p = e / jnp.sum(e, axis=-1, keepdims=True)
            do = do_ref[:, sl].astype(BF16)
            dp = lax.dot_general(do, mv_ref[:, sl], NT_DIMS, preferred_element_type=F32)
            ds = (p * (dp - jnp.sum(p * dp, axis=-1, keepdims=True)) * SCALE).astype(BF16)
            dq = jnp.dot(ds, mk_ref[:, sl], preferred_element_type=F32)
            dmk_ref[:, sl] += lax.dot_general(ds, qb, TN_DIMS, preferred_element_type=F32)
            dmv_ref[:, sl] += lax.dot_general(p.astype(BF16), do, TN_DIMS, preferred_element_type=F32)
            dyw = dq * xqw_ref[...]
            dz_ref[:, sl] = (r * (dyw - xhat * jnp.mean(dyw * xhat, axis=-1, keepdims=True))).astype(BF16)
            dxw_ref[...] += jnp.sum(dq * xhat, axis=0, keepdims=True)

    return pl.pallas_call(
        body, name="cross_bwd", grid=(S // TM,),
        in_specs=[pl.BlockSpec(memory_space=pl.ANY), pl.BlockSpec((TM, CH), lambda i: (i, 0)),
                  pl.BlockSpec((TM, CH), lambda i: (i, O_XQ // CH)), _full((1, HEAD)), _full((M, CH)), _full((M, CH))],
        out_specs=[pl.BlockSpec((TM, CH), lambda i: (i, O_XQ // CH)), _full((M, CH)), _full((M, CH)), _full((1, HEAD))],
        out_shape=[SDS(dz.shape, BF16), SDS((M, CH), F32), SDS((M, CH), F32), SDS((1, HEAD), F32)],
        input_output_aliases={0: 0})(dz, doc, z, xqw, mk, mv)


def _memkv_bwd(dmk, dmv, kv, mem, mn, mnw, wkv, xkw):
    M, D = mem.shape

    def body(dmk_ref, dmv_ref, kv_ref, mem_ref, mn_ref, mnw_ref, w_ref, xkw_ref, dw_ref, dxk_ref, dmn_ref, dkv):
        dxk = jnp.zeros((1, HEAD), F32)
        for h in range(HPG):
            sl = slice(h * HEAD, (h + 1) * HEAD)
            k = kv_ref[:, sl]
            r = lax.rsqrt(jnp.mean(k * k, axis=-1, keepdims=True) + RMS_EPS)
            khat = k * r
            dy = dmk_ref[:, sl]
            dyw = dy * xkw_ref[...]
            dkv[:, sl] = (r * (dyw - khat * jnp.mean(dyw * khat, axis=-1, keepdims=True))).astype(BF16)
            dxk = dxk + jnp.sum(dy * khat, axis=0, keepdims=True)
        dxk_ref[...] = dxk
        dkv[:, CH:2 * CH] = dmv_ref[...].astype(BF16)
        dw_ref[...] = lax.dot_general(mn_ref[...], dkv[...], TN_DIMS, preferred_element_type=F32)
        dn = lax.dot_general(dkv[...], w_ref[...], NT_DIMS, preferred_element_type=F32)
        x = mem_ref[...]
        r = lax.rsqrt(jnp.mean(x * x, axis=-1, keepdims=True) + RMS_EPS)
        dmn_ref[...] = jnp.sum(dn * x * r, axis=0, keepdims=True)

    return pl.pallas_call(
        body, name="memkv_bwd",
        out_shape=[SDS((D, 2 * CH), F32), SDS((1, HEAD), F32), SDS((1, D), F32)],
        scratch_shapes=[pltpu.VMEM((M, 2 * CH), BF16)])(dmk, dmv, kv, mem, mn, mnw, wkv, xkw)


def _branch_proj(a_ref, w_ref, y_ref):
    G, _, n = w_ref.shape
    a = a_ref[...]
    for g in range(G):
        y_ref[:, g * n:(g + 1) * n] = jnp.dot(a, w_ref[g], preferred_element_type=F32)


OUT_RC = 16


def _gates(zg_ref, bg_ref, rows, k, D):
    return _sig(zg_ref[rows, k * D:(k + 1) * D].astype(F32) + bg_ref[:, k * D:(k + 1) * D])


def _outproj_fwd(x, z, bg, attn, u3, oc, wao, wco, wxo, wout, fnw):
    S, D = x.shape
    tm = TM

    def body(x_ref, zg_ref, bg_ref, a_ref, u_ref, c_ref, wa, wc, wx, wo, fnw_ref, h1_ref, hn_ref, ya, yc, yx, mg):
        _branch_proj(a_ref, wa, ya)
        _branch_proj(u_ref, wc, yc)
        _branch_proj(c_ref, wx, yx)
        for r0 in range(0, tm, OUT_RC):
            rows = pl.ds(r0, OUT_RC)
            mg[rows, :] = (_gates(zg_ref, bg_ref, rows, 0, D) * ya[rows, :] + _gates(zg_ref, bg_ref, rows, 1, D) * yc[rows, :]
                           + _gates(zg_ref, bg_ref, rows, 2, D) * yx[rows, :]).astype(BF16)
        ya[...] = jnp.dot(mg[...], wo[...], preferred_element_type=F32)
        for r0 in range(0, tm, OUT_RC):
            rows = pl.ds(r0, OUT_RC)
            h1 = x_ref[rows, :] + ya[rows, :]
            h1_ref[rows, :] = h1
            hn_ref[rows, :] = (h1 * lax.rsqrt(jnp.mean(h1 * h1, axis=-1, keepdims=True) + RMS_EPS) * fnw_ref[...]).astype(BF16)

    row = lambda w: pl.BlockSpec((tm, w), lambda i: (i, 0))
    return pl.pallas_call(
        body, name="outproj_fwd", grid=(S // tm,),
        in_specs=[row(D), pl.BlockSpec((tm, 3 * D), lambda i: (i, O_G // (3 * D))), _full((1, 3 * D)), row(CH), row(CH), row(CH),
                  _full(wao.shape), _full(wco.shape), _full(wxo.shape), _full((D, D)), _full((1, D))],
        out_specs=[row(D), row(D)], out_shape=[SDS((S, D), F32), SDS((S, D), BF16)],
        scratch_shapes=[pltpu.VMEM((tm, D), F32)] * 3 + [pltpu.VMEM((tm, D), BF16)])(x, z, bg, attn, u3, oc, wao, wco, wxo, wout, fnw)


def _outproj_bwd(dh1, z, bg, attn, u3, oc, wao, wco, wxo, wout, n_in):
    S, D = dh1.shape
    tm = 256
    nt = S // tm
    G, _, n = wao.shape

    def body(dh_ref, zg_ref, bg_ref, a_ref, u_ref, c_ref, wa, wc, wx, wo,
             dz_ref, da_ref, du_ref, dc_ref, dbg_ref, dwo_ref, dwa_ref, dwc_ref, dwx_ref,
             ya, yc, yx, dm, dy, mg, bacc, wacc):
        i = pl.program_id(0)

        @pl.when(i == 0)
        def _():
            bacc[...] = jnp.zeros_like(bacc)
            wacc[...] = jnp.zeros_like(wacc)
            dwo_ref[...] = jnp.zeros_like(dwo_ref)

        _branch_proj(a_ref, wa, ya)
        _branch_proj(u_ref, wc, yc)
        _branch_proj(c_ref, wx, yx)
        dhb = dh_ref[...].astype(BF16)
        dm[...] = lax.dot_general(dhb, wo[...], NT_DIMS, preferred_element_type=F32)
        for r0 in range(0, tm, OUT_RC):
            rows = pl.ds(r0, OUT_RC)
            dmv = dm[rows, :]
            merged = jnp.zeros((OUT_RC, D), F32)
            for k, y in enumerate((ya, yc, yx)):
                gk = _gates(zg_ref, bg_ref, rows, k, D)
                yk = y[rows, :]
                merged = merged + gk * yk
                dzg = dmv * yk * gk * (1.0 - gk)
                dz_ref[rows, k * D:(k + 1) * D] = dzg.astype(BF16)
                bacc[:, k * D:(k + 1) * D] += _fold8(dzg)
                dy[k, rows, :] = (dmv * gk).astype(BF16)
            mg[rows, :] = merged.astype(BF16)
        dwo_ref[...] += lax.dot_general(mg[...], dhb, TN_DIMS, preferred_element_type=F32)
        for k, (b_ref, w_ref, db_ref) in enumerate(((a_ref, wa, da_ref), (u_ref, wc, du_ref), (c_ref, wx, dc_ref))):
            dyk = dy[k]
            acc = jnp.zeros((tm, CH), F32)
            for g in range(G):
                acc = acc + lax.dot_general(dyk[:, g * n:(g + 1) * n], w_ref[g], NT_DIMS, preferred_element_type=F32)
            db_ref[...] = acc
            wacc[k] += lax.dot_general(b_ref[...], dyk, TN_DIMS, preferred_element_type=F32)

        @pl.when(i == nt - 1)
        def _():
            dbg_ref[...] = jnp.sum(bacc[...], axis=0, keepdims=True)
            for k, dw_ref in enumerate((dwa_ref, dwc_ref, dwx_ref)):
                for g in range(G):
                    dw_ref[g] = wacc[k, :, g * n:(g + 1) * n]

    row = lambda w: pl.BlockSpec((tm, w), lambda i: (i, 0))
    return pl.pallas_call(
        body, name="outproj_bwd", grid=(nt,),
        in_specs=[row(D), pl.BlockSpec((tm, 3 * D), lambda i: (i, O_G // (3 * D))), _full((1, 3 * D)), row(CH), row(CH), row(CH),
                  _full(wao.shape), _full(wco.shape), _full(wxo.shape), _full((D, D))],
        out_specs=[pl.BlockSpec((tm, 3 * D), lambda i: (i, O_G // (3 * D))), row(CH), row(CH), row(CH), _full((1, 3 * D)),
                   _full((D, D))] + [_full(wao.shape)] * 3,
        out_shape=[SDS((S, n_in), BF16)] + [SDS((S, CH), F32)] * 3 + [SDS((1, 3 * D), F32), SDS((D, D), F32)]
        + [SDS(wao.shape, F32)] * 3,
        scratch_shapes=[pltpu.VMEM((tm, D), F32)] * 4 + [pltpu.VMEM((3, tm, D), BF16), pltpu.VMEM((tm, D), BF16),
                                                        pltpu.VMEM((8, 3 * D), F32), pltpu.VMEM((3, CH, D), F32)],
    )(dh1, z, bg, attn, u3, oc, wao, wco, wxo, wout)


FFN_TC = 256
FFN_H = 8


def _ffn_taps(buf, r0):
    xx = buf[pl.ds(r0, FFN_RC + FFN_H), :]
    return xx[FFN_H:], pltpu.roll(xx, 1, 0)[FFN_H:], pltpu.roll(xx, 2, 0)[FFN_H:]


def _ffn_conv(taps, w_ref, b_ref):
    x0, x1, x2 = taps
    return b_ref[...] + x0 * w_ref[2:3, :] + x1 * w_ref[1:2, :] + x2 * w_ref[0:1, :]


def _ffn_fwd(up, cw, cb, wdown, h1, target):
    S, D = h1.shape
    F2 = up.shape[1]
    F = F2 // 2
    nj = F // FFN_TC
    tm = TM

    def body(up_ref, halo_ref, cw_ref, cb_ref, w_ref, h_ref, t_ref, dy_ref, loss_ref, ac_ref, gc_ref, abuf, gbuf, act_s):
        i = pl.program_id(0)

        @pl.when(i == 0)
        def _():
            loss_ref[...] = jnp.zeros_like(loss_ref)

        for j in range(nj):
            ca, cg = slice(j * FFN_TC, (j + 1) * FFN_TC), slice(F + j * FFN_TC, F + (j + 1) * FFN_TC)
            first = i == 0
            abuf[0:FFN_H, :] = jnp.where(first, 0.0, halo_ref[:, ca].astype(F32))
            gbuf[0:FFN_H, :] = jnp.where(first, 0.0, halo_ref[:, cg].astype(F32))
            abuf[FFN_H:FFN_H + tm, :] = up_ref[:, ca].astype(F32)
            gbuf[FFN_H:FFN_H + tm, :] = up_ref[:, cg].astype(F32)
            for r0 in range(0, tm, FFN_RC):
                rows = pl.ds(r0, FFN_RC)
                a = _ffn_conv(_ffn_taps(abuf, r0), cw_ref[:, ca], cb_ref[:, ca])
                gt = _ffn_conv(_ffn_taps(gbuf, r0), cw_ref[:, cg], cb_ref[:, cg])
                ac_ref[rows, ca] = a.astype(BF16)
                gc_ref[rows, ca] = gt.astype(BF16)
                act_s[rows, ca] = (gt * _sig(gt) * a).astype(BF16)
        err = h_ref[...] + jnp.dot(act_s[...], w_ref[...], preferred_element_type=F32) - t_ref[...]
        dy_ref[...] = err * (1.0 / D)
        loss_ref[...] += 0.5 * jnp.sum(jnp.mean(err * err, axis=-1, keepdims=True))

    row = lambda w: pl.BlockSpec((tm, w), lambda i: (i, 0))
    halo = pl.BlockSpec((FFN_H, F2), lambda i: (jnp.maximum(i * (tm // FFN_H) - 1, 0), 0))
    return pl.pallas_call(
        body, name="ffn_fwd", grid=(S // tm,),
        in_specs=[row(F2), halo, _full((FFN_K, F2)), _full((1, F2)), _full((F, D)), row(D), row(D)],
        out_specs=[row(D), _full((8, 128)), row(F), row(F)],
        out_shape=[SDS((S, D), F32), SDS((8, 128), F32), SDS((S, F), BF16), SDS((S, F), BF16)],
        scratch_shapes=[pltpu.VMEM((tm + FFN_H, FFN_TC), F32)] * 2 + [pltpu.VMEM((tm, F), BF16)])(up, up, cw, cb, wdown, h1, target)


def _ffn_bwd_a(dy, wdown, ac, gc):
    S, D = dy.shape
    F = ac.shape[1]
    nj = F // FFN_TC
    tm = 1024 if S % 1024 == 0 else TM

    def body(dy_ref, wd_ref, a_ref, g_ref, da_ref, dg_ref, acca_ref, accg_ref, dwd_ref, dact_s, act_s):
        i, j = pl.program_id(0), pl.program_id(1)

        @pl.when((i == 0) & (j == 0))
        def _():
            acca_ref[...] = jnp.zeros_like(acca_ref)
            accg_ref[...] = jnp.zeros_like(accg_ref)
            dwd_ref[...] = jnp.zeros_like(dwd_ref)

        dyb = dy_ref[...].astype(BF16)
        dact_s[...] = lax.dot_general(dyb, wd_ref[...], NT_DIMS, preferred_element_type=F32)
        pa = pg = jnp.zeros((8, FFN_TC), F32)
        for r0 in range(0, tm, FFN_RC):
            rows = pl.ds(r0, FFN_RC)
            a = a_ref[rows, :].astype(F32)
            gt = g_ref[rows, :].astype(F32)
            dact = dact_s[rows, :]
            sg = _sig(gt)
            silu = gt * sg
            act_s[rows, :] = (silu * a).astype(BF16)
            dac = dact * silu
            dgc = dact * a * (sg * (1.0 + gt * (1.0 - sg)))
            da_ref[rows, :] = dac.astype(BF16)
            dg_ref[rows, :] = dgc.astype(BF16)
            pa = pa + _fold8(dac)
            pg = pg + _fold8(dgc)
        acca_ref[j] += pa
        accg_ref[j] += pg
        dwd_ref[pl.ds(pl.multiple_of(j * FFN_TC, FFN_TC), FFN_TC), :] += lax.dot_general(
            act_s[...], dyb, TN_DIMS, preferred_element_type=F32)

    col = pl.BlockSpec((tm, FFN_TC), lambda i, j: (i, j))
    return pl.pallas_call(
        body, name="ffn_bwd_a", grid=(S // tm, nj),
        in_specs=[pl.BlockSpec((tm, D), lambda i, j: (i, 0)), pl.BlockSpec((FFN_TC, D), lambda i, j: (j, 0)), col, col],
        out_specs=[col, col] + [_full((nj, 8, FFN_TC))] * 2 + [_full((F, D))],
        out_shape=[SDS((S, F), BF16)] * 2 + [SDS((nj, 8, FFN_TC), F32)] * 2 + [SDS((F, D), F32)],
        scratch_shapes=[pltpu.VMEM((tm, FFN_TC), F32), pltpu.VMEM((tm, FFN_TC), BF16)])(dy, wdown, ac, gc)


def _ffn_bwd_b(dca, dcg, up, cw):
    S, F = dca.shape
    nj = F // FFN_TC
    tm = 2048 if S % 2048 == 0 else TM
    nt = S // tm
    span = FFN_RC + FFN_H

    def body(a_ref, g_ref, u_ref, w_ref, o_ref, tacc_ref, ybuf):
        j, i = pl.program_id(0), pl.program_id(1)

        @pl.when(i == 0)
        def _():
            ybuf[tm:tm + FFN_H, :] = jnp.zeros((FFN_H, FFN_TC), F32)
            tacc_ref[...] = jnp.zeros_like(tacc_ref)

        @pl.when(i > 0)
        def _():
            ybuf[tm:tm + FFN_H, :] = ybuf[0:FFN_H, :]

        ybuf[0:tm, :] = jnp.where(j < nj, a_ref[...], g_ref[...]).astype(F32)
        p = [jnp.zeros((8, FFN_TC), F32)] * FFN_K
        for r0 in range(0, tm, FFN_RC):
            rows = pl.ds(r0, FFN_RC)
            yy = ybuf[pl.ds(r0, span), :]
            ys = (yy[:FFN_RC], pltpu.roll(yy, span - 1, 0)[:FFN_RC], pltpu.roll(yy, span - 2, 0)[:FFN_RC])
            o_ref[rows, :] = (ys[0] * w_ref[2:3, :] + ys[1] * w_ref[1:2, :] + ys[2] * w_ref[0:1, :]).astype(BF16)
            u = u_ref[rows, :].astype(F32)
            for k in range(FFN_K):
                p[k] = p[k] + _fold8(ys[FFN_K - 1 - k] * u)
        for k in range(FFN_K):
            tacc_ref[k] += p[k]

    rev = lambda f: pl.BlockSpec((tm, FFN_TC), lambda j, i: (nt - 1 - i, f(j)))
    return pl.pallas_call(
        body, name="ffn_bwd_b", grid=(2 * nj, nt),
        in_specs=[rev(lambda j: jnp.minimum(j, nj - 1)), rev(lambda j: jnp.maximum(j - nj, 0)), rev(lambda j: j),
                  pl.BlockSpec((FFN_K, FFN_TC), lambda j, i: (0, j))],
        out_specs=[rev(lambda j: j), pl.BlockSpec((None, FFN_K, 8, FFN_TC), lambda j, i: (j, 0, 0, 0))],
        out_shape=[SDS((S, 2 * F), BF16), SDS((2 * nj, FFN_K, 8, FFN_TC), F32)],
        scratch_shapes=[pltpu.VMEM((tm + FFN_H, FFN_TC), F32)])(dca, dcg, up, cw)


def _adamw_update(w_ref, g_ref, m_ref, v_ref, d_ref, nm_ref, nv_ref):
    gv = g_ref[...]
    m2 = ADAM_B1 * m_ref[...] + (1.0 - ADAM_B1) * gv
    v2 = ADAM_B2 * v_ref[...] + (1.0 - ADAM_B2) * jnp.square(gv)
    m_hat = m2 / (1.0 - ADAM_B1 ** ADAM_STEP)
    v_hat = v2 / (1.0 - ADAM_B2 ** ADAM_STEP)
    d_ref[...] = -ADAM_LR * (m_hat / (jnp.sqrt(v_hat) + ADAM_EPS) + ADAM_WD * w_ref[...])
    nm_ref[...] = m2
    nv_ref[...] = v2


def _adamw_small(ws, gs, ms, vs):
    n = len(ws)

    def body(*refs):
        for i in range(n):
            _adamw_update(*[refs[k * n + i] for k in range(7)])

    shapes = [SDS(w.shape, F32) for w in ws]
    res = pl.pallas_call(body, name="adamw_small", out_shape=shapes * 3)(*ws, *gs, *ms, *vs)
    return res[:n], res[n:2 * n], res[2 * n:]


def _adamw(w, g, m, v, name):
    R, C = w.shape
    tr = _row_tile(R, max(8, (2 ** 20) // (4 * C) // 8 * 8))

    def body(w_ref, g_ref, m_ref, v_ref, d_ref, nm_ref, nv_ref):
        _adamw_update(w_ref, g_ref, m_ref, v_ref, d_ref, nm_ref, nv_ref)

    blk = pl.BlockSpec((tr, C), lambda i: (i, 0))
    return pl.pallas_call(
        body, name=name, grid=(R // tr,), in_specs=[blk] * 4, out_specs=[blk] * 3,
        out_shape=[SDS((R, C), F32)] * 3)(w, g, m, v)


HBM_SPEC = pl.BlockSpec(memory_space=pltpu.HBM)
SEM_SPEC = pl.BlockSpec(memory_space=pltpu.SEMAPHORE)
DATAFLOW_EFFECT = pltpu.SideEffectType.DATAFLOW_SIDE_EFFECTING


def _position():
    return lax.axis_index("x"), lax.axis_index("y"), lax.axis_index("c")


def _other_chips(x, y):
    return [(1 - x, y), (x, 1 - y), (1 - x, 1 - y)]


def _all_gather_xy(arrs, name):
    n = len(arrs)
    hbm = pl.BlockSpec(memory_space=pl.ANY)

    def body(*refs):
        ins, outs = refs[:n], refs[n:2 * n]
        send_sems, recv_sems = refs[2 * n:]
        x, y, c = _position()
        me = 2 * x + y
        chips = _other_chips(x, y)

        def rcopy(i, k, src, dst, to):
            return pltpu.make_async_remote_copy(src_ref=src, dst_ref=dst, send_sem=send_sems.at[i, k], recv_sem=recv_sems.at[i, k],
                                                device_id=to, device_id_type=MESH)

        sends = []
        for i in range(n):
            own = rcopy(i, 6, ins[i], outs[i].at[me], (x, y, 1 - c))
            own.start()
            sends.append(own)
        for i in range(n):
            for j, (px, py) in enumerate(chips):
                cp = rcopy(i, j, ins[i].at[c], outs[i].at[me, c], (px, py, c))
                cp.start()
                sends.append(cp)
        for i in range(n):
            for j, (px, py) in enumerate(chips):
                got = outs[i].at[2 * px + py, c]
                rcopy(i, j, ins[i].at[c], got, (x, y, c)).wait_recv()
                fwd = rcopy(i, 3 + j, got, got, (x, y, 1 - c))
                fwd.start()
                sends.append(fwd)
        for i in range(n):
            for j, (px, py) in enumerate(chips):
                theirs = outs[i].at[2 * px + py, 1 - c]
                rcopy(i, 3 + j, theirs, theirs, (x, y, c)).wait_recv()
        for i in range(n):
            rcopy(i, 6, ins[i], outs[i].at[me], (x, y, c)).wait_recv()
        for cp in sends:
            cp.wait_send()

    return pl.pallas_call(
        body, name=name, in_specs=[hbm] * n, out_specs=[hbm] * n,
        out_shape=[SDS((4,) + a.shape, a.dtype) for a in arrs],
        scratch_shapes=[pltpu.SemaphoreType.DMA((n, 7)), pltpu.SemaphoreType.DMA((n, 7))])(*arrs)


def _ag_ici_start(arrs, name):
    n = len(arrs)

    def body(*refs):
        ins, lands = refs[:n], refs[n:2 * n]
        send_sems, recv_sems = refs[2 * n:2 * n + 2]
        token = refs[-1]
        x, y, c = _position()
        for i in range(n):
            for j, (px, py) in enumerate(_other_chips(x, y)):
                pltpu.make_async_remote_copy(src_ref=ins[i].at[c], dst_ref=lands[i].at[2 * x + y, c], send_sem=send_sems.at[3 * i + j],
                                             recv_sem=recv_sems.at[3 * i + j], device_id=(px, py, c), device_id_type=MESH).start()
        token[...] = jnp.zeros_like(token)

    lands = [lax.empty((4,) + a.shape, a.dtype) for a in arrs]
    res = pl.pallas_call(
        body, name=name,
        out_shape=[pltpu.SemaphoreType.DMA((3 * n,)), pltpu.SemaphoreType.DMA((3 * n,))]
        + [pltpu.HBM(a.shape, a.dtype) for a in arrs] + [pltpu.HBM(l.shape, l.dtype) for l in lands] + [SDS((8, 128), F32)],
        in_specs=[HBM_SPEC] * (2 * n), out_specs=[SEM_SPEC, SEM_SPEC] + [HBM_SPEC] * (2 * n) + [pl.BlockSpec(memory_space=pltpu.VMEM)],
        input_output_aliases={i: 2 + i for i in range(2 * n)},
        compiler_params=pltpu.CompilerParams(has_side_effects=DATAFLOW_EFFECT),
    )(*[pltpu.with_memory_space_constraint(a, pltpu.HBM) for a in list(arrs) + lands])
    return res[0], res[1], list(res[2:2 + n]), list(res[2 + n:2 + 2 * n]), res[-1]


def _ag_ici_wait(send_sems, recv_sems, ins, lands, after, name):
    n = len(ins)

    def body(*refs):
        ins_r, lands_r = refs[:n], refs[n:2 * n]
        send_r, recv_r = refs[2 * n:2 * n + 2]
        x, y, c = _position()
        for i in range(n):
            for j, (px, py) in enumerate(_other_chips(x, y)):
                cp = pltpu.make_async_remote_copy(src_ref=ins_r[i].at[c], dst_ref=lands_r[i].at[2 * px + py, c], send_sem=send_r.at[3 * i + j],
                                                  recv_sem=recv_r.at[3 * i + j], device_id=(px, py, c), device_id_type=MESH)
                cp.wait_send()
                cp.wait_recv()

    res = pl.pallas_call(
        body, name=name,
        out_shape=[pltpu.HBM(a.shape, a.dtype) for a in list(ins) + list(lands)],
        in_specs=[HBM_SPEC] * (2 * n) + [SEM_SPEC, SEM_SPEC, pl.BlockSpec(memory_space=pl.ANY)], out_specs=[HBM_SPEC] * (2 * n),
        input_output_aliases={i: i for i in range(2 * n)},
        compiler_params=pltpu.CompilerParams(has_side_effects=DATAFLOW_EFFECT),
    )(*ins, *lands, send_sems, recv_sems, after)
    return list(res[:n]), list(res[n:])


def _ag_finish(arrs, lands, name):
    n = len(arrs)
    hbm = pl.BlockSpec(memory_space=pl.ANY)

    def body(*refs):
        ins, landed, outs = refs[:n], refs[n:2 * n], refs[2 * n:3 * n]
        send_sems, recv_sems = refs[3 * n:]
        x, y, c = _position()
        chips = _other_chips(x, y)
        sends = []
        for i in range(n):
            own = pltpu.make_async_remote_copy(src_ref=ins[i], dst_ref=outs[i].at[2 * x + y], send_sem=send_sems.at[i, 3],
                                               recv_sem=recv_sems.at[i, 3], device_id=(x, y, 1 - c), device_id_type=MESH)
            own.start()
            sends.append(own)
        for i in range(n):
            for j, (px, py) in enumerate(chips):
                fwd = pltpu.make_async_remote_copy(src_ref=landed[i].at[2 * px + py, c], dst_ref=outs[i].at[2 * px + py, c],
                                                   send_sem=send_sems.at[i, j], recv_sem=recv_sems.at[i, j],
                                                   device_id=(x, y, 1 - c), device_id_type=MESH)
                fwd.start()
                sends.append(fwd)
        for i in range(n):
            for j, (px, py) in enumerate(chips):
                theirs = outs[i].at[2 * px + py, 1 - c]
                pltpu.make_async_remote_copy(src_ref=theirs, dst_ref=theirs, send_sem=send_sems.at[i, j], recv_sem=recv_sems.at[i, j],
                                             device_id=(x, y, c), device_id_type=MESH).wait_recv()
        for i in range(n):
            pltpu.make_async_remote_copy(src_ref=ins[i], dst_ref=outs[i].at[2 * x + y], send_sem=send_sems.at[i, 3],
                                         recv_sem=recv_sems.at[i, 3], device_id=(x, y, c), device_id_type=MESH).wait_recv()
        for cp in sends:
            cp.wait_send()

    return pl.pallas_call(
        body, name=name, in_specs=[hbm] * (2 * n), out_specs=[hbm] * n,
        out_shape=[SDS(l.shape, l.dtype) for l in lands],
        input_output_aliases={n + i: i for i in range(n)},
        scratch_shapes=[pltpu.SemaphoreType.DMA((n, 4)), pltpu.SemaphoreType.DMA((n, 4))])(*arrs, *lands)


def _swap_halves(gs, name):
    n = len(gs)
    hbm = pl.BlockSpec(memory_space=pl.ANY)

    def body(*refs):
        ins, outs = refs[:n], refs[n:2 * n]
        send_sems, recv_sems = refs[2 * n:]
        x, y, c = _position()
        cps = []
        for i in range(n):
            for p in range(4):
                cp = pltpu.make_async_remote_copy(src_ref=ins[i].at[p, 1 - c], dst_ref=outs[i].at[p], send_sem=send_sems.at[4 * i + p],
                                                  recv_sem=recv_sems.at[4 * i + p], device_id=(x, y, 1 - c), device_id_type=MESH)
                cp.start()
                cps.append(cp)
        for cp in cps:
            cp.wait()

    return pl.pallas_call(
        body, name=name, in_specs=[hbm] * n, out_specs=[hbm] * n,
        out_shape=[SDS((4,) + g.shape[2:], g.dtype) for g in gs],
        scratch_shapes=[pltpu.SemaphoreType.DMA((4 * n,)), pltpu.SemaphoreType.DMA((4 * n,))])(*gs)


def _exchange_start(s1s, name):
    n = len(s1s)

    def body(*refs):
        srcs, lands = refs[:n], refs[n:2 * n]
        send_sems, recv_sems = refs[2 * n:2 * n + 2]
        token = refs[-1]
        x, y, c = _position()
        for i in range(n):
            for j, (px, py) in enumerate(_other_chips(x, y)):
                pltpu.make_async_remote_copy(src_ref=srcs[i].at[2 * px + py], dst_ref=lands[i].at[j], send_sem=send_sems.at[3 * i + j],
                                             recv_sem=recv_sems.at[3 * i + j], device_id=(px, py, c), device_id_type=MESH).start()
        token[...] = jnp.zeros_like(token)

    lands = [lax.empty((3,) + s.shape[1:], F32) for s in s1s]
    res = pl.pallas_call(
        body, name=name,
        out_shape=[pltpu.SemaphoreType.DMA((3 * n,)), pltpu.SemaphoreType.DMA((3 * n,))]
        + [pltpu.HBM(a.shape, F32) for a in list(s1s) + lands] + [SDS((8, 128), F32)],
        in_specs=[HBM_SPEC] * (2 * n), out_specs=[SEM_SPEC, SEM_SPEC] + [HBM_SPEC] * (2 * n) + [pl.BlockSpec(memory_space=pltpu.VMEM)],
        input_output_aliases={i: 2 + i for i in range(2 * n)},
        compiler_params=pltpu.CompilerParams(has_side_effects=DATAFLOW_EFFECT),
    )(*[pltpu.with_memory_space_constraint(a, pltpu.HBM) for a in list(s1s) + lands])
    return res[0], res[1], list(res[2:2 + n]), list(res[2 + n:2 + 2 * n]), res[-1]


def _exchange_wait(send_sems, recv_sems, s1s, lands, after, name):
    n = len(s1s)

    def body(*refs):
        srcs, lands_r = refs[:n], refs[n:2 * n]
        send_r, recv_r = refs[2 * n:2 * n + 2]
        x, y, c = _position()
        for i in range(n):
            for j, (px, py) in enumerate(_other_chips(x, y)):
                cp = pltpu.make_async_remote_copy(src_ref=srcs[i].at[2 * px + py], dst_ref=lands_r[i].at[j], send_sem=send_r.at[3 * i + j],
                                                  recv_sem=recv_r.at[3 * i + j], device_id=(px, py, c), device_id_type=MESH)
                cp.wait_send()
                cp.wait_recv()

    res = pl.pallas_call(
        body, name=name, out_shape=[pltpu.HBM(a.shape, F32) for a in list(s1s) + list(lands)],
        in_specs=[HBM_SPEC] * (2 * n) + [SEM_SPEC, SEM_SPEC, pl.BlockSpec(memory_space=pl.ANY)], out_specs=[HBM_SPEC] * (2 * n),
        input_output_aliases={i: i for i in range(2 * n)},
        compiler_params=pltpu.CompilerParams(has_side_effects=DATAFLOW_EFFECT),
    )(*s1s, *lands, send_sems, recv_sems, after)
    return list(res[:n]), list(res[n:])


def _join_halves(f2s, name):
    n = len(f2s)
    hbm = pl.BlockSpec(memory_space=pl.ANY)

    def body(*refs):
        ins, outs = refs[:n], refs[n:2 * n]
        send_sems, recv_sems = refs[2 * n:]
        x, y, c = _position()
        cps = []
        for i in range(n):
            cp = pltpu.make_async_remote_copy(src_ref=ins[i].at[c], dst_ref=outs[i].at[c], send_sem=send_sems.at[i],
                                              recv_sem=recv_sems.at[i], device_id=(x, y, 1 - c), device_id_type=MESH)
            cp.start()
            cps.append(cp)
        for i in range(n):
            pltpu.make_async_remote_copy(src_ref=ins[i].at[1 - c], dst_ref=outs[i].at[1 - c], send_sem=send_sems.at[i],
                                         recv_sem=recv_sems.at[i], device_id=(x, y, 1 - c), device_id_type=MESH).wait_recv()
        for cp in cps:
            cp.wait_send()

    return pl.pallas_call(
        body, name=name, in_specs=[hbm] * n, out_specs=[hbm] * n, out_shape=[SDS(f.shape, f.dtype) for f in f2s],
        input_output_aliases={i: i for i in range(n)},
        scratch_shapes=[pltpu.SemaphoreType.DMA((n,)), pltpu.SemaphoreType.DMA((n,))])(*f2s)


def _sum_tile(rows, cols):
    return _row_tile(rows, max(8, (2 ** 19 // cols) // 8 * 8))


def _add_pair(g, r1, c, name):
    _, R, C = r1.shape
    tr = _sum_tile(R, C)

    def body(c_ref, a_ref, b_ref, o_ref):
        del c_ref
        o_ref[...] = a_ref[...] + b_ref[...]

    blk = pl.BlockSpec((None, tr, C), lambda p, i, cr: (p, i, 0))
    return pl.pallas_call(
        body, name=name,
        grid_spec=pltpu.PrefetchScalarGridSpec(
            num_scalar_prefetch=1, grid=(4, R // tr),
            in_specs=[pl.BlockSpec((None, None, tr, C), lambda p, i, cr: (p, cr[0], i, 0)), blk], out_specs=blk),
        out_shape=SDS((4, R, C), F32))(c, g, r1)


def _add_four(s1, r2, me_c, name):
    _, R, C = s1.shape
    tr = _sum_tile(R, C)

    def body(m_ref, a_ref, b_ref, o_ref):
        del m_ref
        o_ref[...] = ((a_ref[...] + b_ref[0]) + b_ref[1]) + b_ref[2]

    return pl.pallas_call(
        body, name=name,
        grid_spec=pltpu.PrefetchScalarGridSpec(
            num_scalar_prefetch=1, grid=(R // tr,),
            in_specs=[pl.BlockSpec((None, tr, C), lambda i, mr: (mr[0], i, 0)), pl.BlockSpec((3, tr, C), lambda i, mr: (0, i, 0))],
            out_specs=pl.BlockSpec((None, tr, C), lambda i, mr: (mr[1], i, 0))),
        out_shape=SDS((2, R, C), F32))(me_c, s1, r2)


def _all_reduce_small(vs):
    n = len(vs)

    def body(*refs):
        ins, outs, bufs = refs[:n], refs[n:2 * n], refs[2 * n:3 * n]
        send_sems, recv_sems = refs[3 * n:]
        x, y, c = _position()
        me = 4 * x + 2 * y + c
        for i in range(n):
            bufs[i][me] = ins[i][...]
        cps = []
        for k in range(1, 8):
            to = (1 - x if k & 4 else x, 1 - y if k & 2 else y, 1 - c if k & 1 else c)
            for i in range(n):
                cp = pltpu.make_async_remote_copy(src_ref=bufs[i].at[me], dst_ref=bufs[i].at[me], send_sem=send_sems.at[7 * i + k - 1],
                                                  recv_sem=recv_sems.at[7 * i + k - 1], device_id=to, device_id_type=MESH)
                cp.start()
                cps.append(cp)
        for cp in cps:
            cp.wait_send()
        for k in range(1, 8):
            src = 4 * (1 - x if k & 4 else x) + 2 * (1 - y if k & 2 else y) + (1 - c if k & 1 else c)
            for i in range(n):
                pltpu.make_async_remote_copy(src_ref=bufs[i].at[src], dst_ref=bufs[i].at[src], send_sem=send_sems.at[7 * i + k - 1],
                                             recv_sem=recv_sems.at[7 * i + k - 1], device_id=(x, y, c), device_id_type=MESH).wait_recv()
        for i in range(n):
            acc = bufs[i][0]
            for k in range(1, 8):
                acc = acc + bufs[i][k]
            outs[i][...] = acc

    vm = pl.BlockSpec(memory_space=pltpu.VMEM)
    return pl.pallas_call(
        body, name="all_reduce_small", in_specs=[vm] * n, out_specs=[vm] * n, out_shape=[SDS(v.shape, F32) for v in vs],
        scratch_shapes=[pltpu.VMEM((8,) + v.shape, F32) for v in vs]
        + [pltpu.SemaphoreType.DMA((7 * n,)), pltpu.SemaphoreType.DMA((7 * n,))])(*vs)


def _reduce_begin(grads, tag):
    _, _, c = _position()
    cs = jnp.reshape(c, (1,)).astype(jnp.int32)
    g4 = [g.reshape(4, 2, g.shape[1] // 2, g.shape[2]) for g in grads]
    r1 = _swap_halves(g4, "rs_swap_" + tag)
    s1 = [_add_pair(g, r, cs, f"rs_add_pair_{tag}{i}") for i, (g, r) in enumerate(zip(g4, r1))]
    send_sems, recv_sems, s1, lands, token = _exchange_start(s1, "rs_exchange_start_" + tag)
    return (send_sems, recv_sems, s1, lands), token


def _reduce_end(state, after, tag):
    x, y, c = _position()
    send_sems, recv_sems, s1, lands = state
    s1, lands = _exchange_wait(send_sems, recv_sems, s1, lands, after, "rs_exchange_wait_" + tag)
    me_c = jnp.stack([2 * x + y, c]).astype(jnp.int32)
    f2 = [_add_four(s, l, me_c, f"rs_add_four_{tag}{i}") for i, (s, l) in enumerate(zip(s1, lands))]
    return [f.reshape(2 * f.shape[1], f.shape[2]) for f in _join_halves(f2, "rs_join_" + tag)]


def _halves(a):
    return a.reshape((2, a.shape[0] // 2) + a.shape[1:])


def _after(a, token):
    return a + token[0, 0]


def _in_proj_parts(xn, w, parts, z, name):
    M, K = xn.shape
    G, _, n = w.shape
    P = parts.shape[0]

    def body(p_ref, a_ref, b_ref, *rest):
        del p_ref
        rest[-1][...] = jnp.dot(a_ref[...], b_ref[...], preferred_element_type=F32).astype(BF16)

    in_specs = [pl.BlockSpec((MM_TM, K), lambda g, i, pr: (i, 0)),
                pl.BlockSpec((None, K, n), (lambda g, i, pr: (pr[g], 0, 0)) if G > 1 else (lambda g, i, pr: (0, 0, 0)))]
    args = [parts, xn, w]
    if z is not None:
        in_specs.append(pl.BlockSpec(memory_space=pl.ANY))
        args.append(z)
    return pl.pallas_call(
        body, name=name,
        grid_spec=pltpu.PrefetchScalarGridSpec(
            num_scalar_prefetch=1, grid=(P, M // MM_TM), in_specs=in_specs,
            out_specs=pl.BlockSpec((MM_TM, n), lambda g, i, pr: (i, pr[g]))),
        out_shape=SDS((M, 4 * n), BF16), input_output_aliases={3: 0} if z is not None else {})(*args)


def _local_step(x, mem, target, sp, ex):
    S, D = x.shape
    band, buckets = _bias_static()
    buckets = jnp.asarray(buckets)

    tok = ex.start_first()
    xn = _rmsnorm(x, _after(sp["attn_norm_w"], tok), "rms_in")
    own, me, others = ex.own_w_in()
    z = _in_proj_parts(xn, own, me, None, "in_proj_own")
    w_in, gathered_small = ex.first_weights(after=z)
    sp = {**sp, **gathered_small}
    n_in = 4 * w_in.shape[2]
    bw = {"w_in": w_in}
    z = _in_proj_parts(xn, w_in, others + ex.start_rest()[0, 0].astype(jnp.int32), z, "in_proj_rest")
    qh, kh, vh = _qkv_prep(z, sp["q_norm_w"], sp["k_norm_w"])
    tab = sp["rel_bias_table"].T.reshape(N_GROUPS, HPG, N_BUCKETS)
    bias = _bias_fwd(jnp.pad(tab, ((0, 0), (0, 8 - HPG), (0, 0))), buckets)
    biasm = jnp.where(jnp.asarray(band)[None, None], bias[:, :HPG].reshape(N_GROUPS, HPG, NQ, 2 * NQ), NEG)
    os_, lses = [], []
    for g, (_, dil) in enumerate(ATTN_GROUPS):
        o_g, lse_g = _attn_fwd(qh, kh, vh, biasm[g], g, dil)
        os_.append(o_g)
        lses.append(lse_g)
    attn = _merge_fwd(os_, lses)
    u1, u3 = _conv_fwd(z, sp["conv_dw_w"], sp["conv_dw_b"], sp["conv_ln_w"], sp["conv_ln_b"])
    bw.update(ex.rest_weights(after=attn))
    F2 = 4 * bw["w_up"].shape[2]
    mn, kv, mk, mv = _memkv_fwd(mem, sp["mem_norm_w"], bw["w_mem_kv"], sp["xk_norm_w"])
    oc = _cross_fwd(z, sp["xq_norm_w"], mk, mv)
    h1, hn = _outproj_fwd(x, z, sp["b_gate"], attn, u3, oc, bw["w_attn_o"], bw["w_conv_o"], bw["w_cross_o"], bw["w_out"],
                          sp["ffn_norm_w"])
    up = _mm_nn(hn, bw["w_up"], BF16, "ffn_up")
    dy, loss_tile, ac, gc = _ffn_fwd(up, sp["ffn_conv_w"], sp["ffn_conv_b"], bw["w_down"], h1, target)

    gs, gb = {}, {}
    dca, dcg, acca, accg, gb["w_down"] = _ffn_bwd_a(dy, bw["w_down"], ac, gc)
    cols = lambda acc: jnp.sum(acc, axis=1).reshape(1, F2 // 2)
    gs["ffn_conv_b"] = jnp.concatenate([cols(acca), cols(accg)], axis=1)
    dup, tacc = _ffn_bwd_b(dca, dcg, up, sp["ffn_conv_w"])
    gs["ffn_conv_w"] = jnp.transpose(jnp.sum(tacc, axis=2), (1, 0, 2)).reshape(FFN_K, F2)
    gb["w_up"] = _mm_tn(hn, dup, 4, "dw_up")
    tok = ex.reduce_begin("a", ("w_down", "w_up"), gb)
    dh1, gs["ffn_norm_w"] = _norm_in_bwd(dup, bw["w_up"], h1, _after(sp["ffn_norm_w"], tok), dy, "ffn_in_bwd")
    dz, dattn, du3, doc, gs["b_gate"], gb["w_out"], gb["w_attn_o"], gb["w_conv_o"], gb["w_cross_o"] = _outproj_bwd(
        dh1, z, sp["b_gate"], attn, u3, oc, bw["w_attn_o"], bw["w_conv_o"], bw["w_cross_o"], bw["w_out"], n_in)
    dz, dmk, dmv, gs["xq_norm_w"] = _cross_bwd(dz, doc, z, sp["xq_norm_w"], mk, mv)
    gb["w_mem_kv"], gs["xk_norm_w"], gs["mem_norm_w"] = _memkv_bwd(
        dmk, dmv, kv, mem, mn, sp["mem_norm_w"], bw["w_mem_kv"], sp["xk_norm_w"])
    ex.reduce_end("a", after=gs["mem_norm_w"])
    tok = ex.reduce_begin("b", ("w_out", "w_attn_o", "w_conv_o", "w_cross_o", "w_mem_kv"), gb)
    du1, cacc = _conv_bwd_a(du3, u1, z, _after(sp["conv_ln_w"], tok), sp["conv_ln_b"])
    gs["conv_dw_w"], gs["conv_dw_b"] = cacc[:CONV_K], cacc[32:33]
    gs["conv_ln_w"], gs["conv_ln_b"] = cacc[33:34], cacc[34:35]
    dz = _conv_bwd_b(dz, du1, z, sp["conv_dw_w"])
    wg, dah, dhb = _merge_bwd(dattn, os_, lses)
    dqs, dks, dvs, dsbs = [], [], [], []
    for g, (_, dil) in enumerate(ATTN_GROUPS):
        dq_g, dk_g, dv_g, dsb_g = _attn_bwd(qh, kh, vh, biasm[g], dah, wg[g], dhb, lses[g], g, dil)
        dqs.append(dq_g)
        dks.append(dk_g)
        dvs.append(dv_g)
        dsbs.append(dsb_g.reshape(HPG, NQ * 2 * NQ))
    dtab = _bias_bwd(jnp.pad(jnp.stack(dsbs), ((0, 0), (0, 8 - HPG), (0, 0))), buckets)
    gs["rel_bias_table"] = dtab[:, :HPG].reshape(N_GROUPS * HPG, N_BUCKETS).T
    dz, gs["q_norm_w"], gs["k_norm_w"] = _qkv_bwd(dz, z, dqs, dks, dvs, sp["q_norm_w"], sp["k_norm_w"])
    ex.reduce_end("b", after=gs["q_norm_w"])
    gb["w_in"] = _mm_tn(xn, dz, 4, "dw_in")
    tok = ex.reduce_begin("c", ("w_in",), gb)
    dx, gs["attn_norm_w"] = _norm_in_bwd(dz, bw["w_in"], x, _after(sp["attn_norm_w"], tok), dh1, "in_bwd")
    ex.reduce_end("c", after=gs["attn_norm_w"])
    return loss_tile, dx, gs, gb


SMALL = ("rel_bias_table", "attn_norm_w", "b_gate", "q_norm_w", "k_norm_w", "conv_dw_w", "conv_dw_b", "conv_ln_w", "conv_ln_b",
         "mem_norm_w", "xq_norm_w", "xk_norm_w", "ffn_norm_w", "ffn_conv_w", "ffn_conv_b")
SMALL_SHARDED = ("conv_dw_w", "ffn_conv_w")
BIG_COL = ("w_in", "w_attn_o", "w_conv_o", "w_cross_o", "w_up")
BIG_ROW = ("w_mem_kv", "w_out", "w_down")
BIG = BIG_COL + BIG_ROW
WEIGHTS = ("rel_bias_table", "attn_norm_w", "w_in", "b_gate", "q_norm_w", "k_norm_w", "w_attn_o", "conv_dw_w", "conv_dw_b",
           "conv_ln_w", "conv_ln_b", "w_conv_o", "mem_norm_w", "w_mem_kv", "xq_norm_w", "xk_norm_w", "w_cross_o", "w_out",
           "ffn_norm_w", "w_up", "ffn_conv_w", "ffn_conv_b", "w_down")


class _Exchanges:
    REST = tuple(k for k in BIG if k != "w_in")

    def __init__(self, w):
        self.w = w
        self.pending = {}
        self.reduced = {}

    def _whole(self, k, ga):
        ga = ga.reshape((4,) + self.w[k].shape)
        return ga if k in BIG_COL else ga.reshape((4 * self.w[k].shape[0],) + self.w[k].shape[1:])

    def start_first(self):
        self.w_in_local = self.w["w_in"].astype(BF16)
        local = [_halves(self.w_in_local)]
        for k in SMALL_SHARDED:
            flat = jnp.ravel(self.w[k])
            local.append(jnp.pad(flat, (0, (-flat.shape[0]) % 2048)).reshape(2, -1, 128))
        send_sems, recv_sems, ins, lands, token = _ag_ici_start(local, "gather_first_start")
        self.pending["first"] = (send_sems, recv_sems, ins, lands)
        return token

    def own_w_in(self):
        x, y, _ = _position()
        me = 2 * x + y
        others = jnp.stack([me ^ 1, me ^ 2, me ^ 3]).astype(jnp.int32)
        return self.w_in_local[None], jnp.reshape(me, (1,)).astype(jnp.int32), others

    def first_weights(self, after):
        send_sems, recv_sems, ins, lands = self.pending.pop("first")
        ins, lands = _ag_ici_wait(send_sems, recv_sems, ins, lands, after, "gather_first_wait")
        gathered = _ag_finish(ins, lands, "gather_first_finish")
        self.first = gathered[0]
        small = {}
        for k, ga in zip(SMALL_SHARDED, gathered[1:]):
            r, cdim = self.w[k].shape
            parts = ga.reshape(4, -1)[:, :r * cdim].reshape(4, r, cdim)
            small[k] = jnp.transpose(parts, (1, 0, 2)).reshape(r, 4 * cdim)
        return self._whole("w_in", gathered[0]), small

    def start_rest(self):
        local = [_halves(self.w[k].astype(BF16)) for k in self.REST]
        local, _ = lax.optimization_barrier((local, self.first))
        send_sems, recv_sems, ins, lands, token = _ag_ici_start(local, "gather_rest_start")
        self.pending["rest"] = (send_sems, recv_sems, ins, lands)
        return token

    def rest_weights(self, after):
        send_sems, recv_sems, ins, lands = self.pending.pop("rest")
        ins, lands = _ag_ici_wait(send_sems, recv_sems, ins, lands, after, "gather_rest_wait")
        gathered = _ag_finish(ins, lands, "gather_rest_finish")
        return {k: self._whole(k, ga) for k, ga in zip(self.REST, gathered)}

    def reduce_begin(self, tag, names, gb):
        parts = [gb[k].reshape((4,) + self.w[k].shape) for k in names]
        state, token = _reduce_begin(parts, tag)
        self.pending[tag] = (state, names)
        return token

    def reduce_end(self, tag, after):
        state, names = self.pending.pop(tag)
        self.reduced.update(zip(names, _reduce_end(state, after, tag)))


def _step(x, mem, target, w, m, v):
    xi, yi, _ = _position()
    shard = 2 * xi + yi
    ex = _Exchanges(w)
    sp = {k: w[k] for k in SMALL if k not in SMALL_SHARDED}
    loss_tile, dx, gs, _ = _local_step(x, mem, target, sp, ex)
    g_big = ex.reduced

    red = _all_reduce_small([loss_tile] + [gs[k] for k in SMALL])
    loss = red[0][0, 0]
    g_small = dict(zip(SMALL, red[1:]))
    for k in SMALL_SHARDED:
        cdim = w[k].shape[1]
        g_small[k] = lax.dynamic_slice_in_dim(g_small[k], shard * cdim, cdim, axis=1)

    grads, delta, new_m, new_v = {}, {}, {}, {}
    for k in BIG:
        grads[k] = g_big[k]
        delta[k], new_m[k], new_v[k] = _adamw(w[k], g_big[k], m[k], v[k], "adamw_" + k)
    outs = _adamw_small([w[k] for k in SMALL], [g_small[k] for k in SMALL], [m[k] for k in SMALL], [v[k] for k in SMALL])
    for dst, vals in zip((delta, new_m, new_v), outs):
        dst.update(zip(SMALL, vals))
    grads.update(g_small)
    return loss, dx, grads, delta, new_m, new_v


def kernel(x, mem, rel_bias_table, attn_norm_w, w_in, b_gate, q_norm_w, k_norm_w, w_attn_o, conv_dw_w, conv_dw_b, conv_ln_w, conv_ln_b, w_conv_o, mem_norm_w, w_mem_kv, xq_norm_w, xk_norm_w, w_cross_o, w_out, ffn_norm_w, w_up, ffn_conv_w, ffn_conv_b, w_down, loss_target, m_rel_bias_table, m_attn_norm_w, m_w_in, m_b_gate, m_q_norm_w, m_k_norm_w, m_w_attn_o, m_conv_dw_w, m_conv_dw_b, m_conv_ln_w, m_conv_ln_b, m_w_conv_o, m_mem_norm_w, m_w_mem_kv, m_xq_norm_w, m_xk_norm_w, m_w_cross_o, m_w_out, m_ffn_norm_w, m_w_up, m_ffn_conv_w, m_ffn_conv_b, m_w_down, v_rel_bias_table, v_attn_norm_w, v_w_in, v_b_gate, v_q_norm_w, v_k_norm_w, v_w_attn_o, v_conv_dw_w, v_conv_dw_b, v_conv_ln_w, v_conv_ln_b, v_w_conv_o, v_mem_norm_w, v_w_mem_kv, v_xq_norm_w, v_xk_norm_w, v_w_cross_o, v_w_out, v_ffn_norm_w, v_w_up, v_ffn_conv_w, v_ffn_conv_b, v_w_down):
    args = locals()
    def block(name, k):
        a = args[name] if k == "rel_bias_table" else args[name][0]
        return a.reshape(1, -1) if a.ndim == 1 else a

    w = {k: block(k, k) for k in WEIGHTS}
    m = {k: block("m_" + k, k) for k in WEIGHTS}
    v = {k: block("v_" + k, k) for k in WEIGHTS}
    loss, dx, grads, delta, new_m, new_v = _step(x[0], mem[0], loss_target[0], w, m, v)
    out = [loss, dx[None]]
    for d in (grads, delta, new_m, new_v):
        for k in WEIGHTS:
            out.append(d[k].reshape(args[k].shape))
    return tuple(out)
```

```python
import functools
import math

import numpy as np
import jax
import jax.numpy as jnp
from jax import lax
from jax.experimental import pallas as pl
from jax.experimental.pallas import tpu as pltpu

F32, BF16 = jnp.float32, jnp.bfloat16
SDS = jax.ShapeDtypeStruct
MESH = pl.DeviceIdType.MESH

HEAD = 128
N_GROUPS, HPG = 3, 4
ATTN_GROUPS = ((128, 1), (512, 4), (2048, 16))
NQ = 128
QKV_W = N_GROUPS * HPG * HEAD
CH = 512
CONV_K, FFN_K = 31, 3
N_BUCKETS, MAX_DIST = 32, 2048
RMS_EPS, LN_EPS = 1e-6, 1e-5
O_Q, O_K, O_V, O_CV, O_CG, O_XQ, O_G = 0, QKV_W, 2 * QKV_W, 3 * QKV_W, 3 * QKV_W + CH, 3 * QKV_W + 2 * CH, 3 * QKV_W + 3 * CH
ADAM_LR, ADAM_B1, ADAM_B2, ADAM_EPS, ADAM_WD, ADAM_STEP = 0.001, 0.9, 0.999, 1e-08, 0.01, 10
NEG = -1e30
SCALE = HEAD ** -0.5
TM = 512
MM_TM = 1024
ATT_RB = 2048
NT_DIMS = (((1,), (1,)), ((), ()))
TN_DIMS = (((0,), (0,)), ((), ()))


CONV_RC = 32
CONV_LB = 256
CONV_LANES = tuple(slice(l, l + CONV_LB) for l in range(0, CH, CONV_LB))
CONV_HALO = 32
FFN_RC = 32


def _conv_taps(buf, base, lanes, q_lo, q_hi, visit):
    span = CONV_RC + CONV_HALO
    xx = buf[pl.ds(base, span), lanes]
    for s in range(8):
        xs = xx if s == 0 else pltpu.roll(xx, span - s, 0)
        for q in range(s, q_hi + 1, 8):
            if q >= q_lo:
                visit(q, xs[q - s:q - s + CONV_RC])


def _sig(v):
    return 0.5 * jnp.tanh(0.5 * v) + 0.5


def _fold8(v):
    acc = v[0:8]
    for r in range(8, v.shape[0], 8):
        acc = acc + v[r:r + 8]
    return acc


def _row_tile(rows, cap, mult=8):
    best = None
    for t in range(mult, min(rows, cap) + 1, mult):
        if rows % t == 0:
            best = t
    return best if best is not None else rows


def _full(shape):
    n = len(shape)
    return pl.BlockSpec(shape, lambda *a: (0,) * n)


def _mm_nn(a, b, out_dtype, name):
    M, K = a.shape
    G, _, n = b.shape

    def body(a_ref, b_ref, o_ref):
        o_ref[...] = jnp.dot(a_ref[...].astype(BF16), b_ref[...], preferred_element_type=F32).astype(out_dtype)

    return pl.pallas_call(
        body, name=name, grid=(G, M // MM_TM),
        in_specs=[pl.BlockSpec((MM_TM, K), lambda g, i: (i, 0)), pl.BlockSpec((None, K, n), lambda g, i: (g, 0, 0))],
        out_specs=pl.BlockSpec((MM_TM, n), lambda g, i: (i, g)),
        out_shape=SDS((M, G * n), out_dtype))(a, b)


def _mm_tn(a, b, G, name):
    S, Ka = a.shape
    n = b.shape[1] // G
    tka = Ka
    while tka * n * 4 > 10 * 2 ** 20 and tka % 256 == 0:
        tka //= 2

    def body(a_ref, b_ref, o_ref):
        @pl.when(pl.program_id(2) == 0)
        def _():
            o_ref[...] = jnp.zeros_like(o_ref)
        o_ref[...] += lax.dot_general(a_ref[...].astype(BF16), b_ref[...].astype(BF16), TN_DIMS, preferred_element_type=F32)

    return pl.pallas_call(
        body, name=name, grid=(G, Ka // tka, S // MM_TM),
        in_specs=[pl.BlockSpec((MM_TM, tka), lambda g, i, k: (k, i)), pl.BlockSpec((MM_TM, n), lambda g, i, k: (k, g))],
        out_specs=pl.BlockSpec((None, tka, n), lambda g, i, k: (g, i, 0)),
        out_shape=SDS((G, Ka, n), F32))(a, b)


NORM_RC = 16


def _norm_in_bwd(a, w, xin, nw, resid, name):
    S, K = xin.shape
    G, _, n = w.shape
    tm = 1024 if S % 1024 == 0 and G * K * n * 2 <= 12 * 2 ** 20 else TM
    nt = S // tm

    def body(a_ref, w_ref, x_ref, nw_ref, r_ref, o_ref, dnw_ref, acc, part):
        i, g = pl.program_id(0), pl.program_id(1)
        cur = lax.rem(i, 2)

        def product():
            return lax.dot_general(a_ref[...], w_ref[g], NT_DIMS, preferred_element_type=F32)

        def epilogue():
            for r0 in range(0, tm, NORM_RC):
                rows = pl.ds(r0, NORM_RC)
                dn = acc[1 - cur, rows, :]
                xv = x_ref[rows, :]
                r = lax.rsqrt(jnp.mean(xv * xv, axis=-1, keepdims=True) + RMS_EPS)
                xhat = xv * r
                dyw = dn * nw_ref[...]
                o_ref[rows, :] = r_ref[rows, :] + r * (dyw - xhat * jnp.mean(dyw * xhat, axis=-1, keepdims=True))
                part[...] += _fold8(dn * xhat)

        @pl.when((g == 0) & (i == 0))
        def _():
            part[...] = jnp.zeros_like(part)
            acc[cur] = product()

        @pl.when((g == 0) & (i > 0) & (i < nt))
        def _():
            acc[cur] = product()
            epilogue()

        @pl.when((g == 0) & (i == nt))
        def _():
            epilogue()
            dnw_ref[...] = jnp.sum(part[...], axis=0, keepdims=True)

        @pl.when((g > 0) & (i < nt))
        def _():
            acc[cur] += product()

    lag = lambda i, g: (jnp.maximum(i - 1, 0), 0)
    return pl.pallas_call(
        body, name=name, grid=(nt + 1, G),
        in_specs=[pl.BlockSpec((tm, n), lambda i, g: (jnp.minimum(i, nt - 1), g)),
                  pl.BlockSpec((G, K, n), lambda i, g: (0, 0, 0), pipeline_mode=pl.Buffered(1)),
                  pl.BlockSpec((tm, K), lag), _full((1, K)), pl.BlockSpec((tm, K), lag)],
        out_specs=[pl.BlockSpec((tm, K), lag), _full((1, K))],
        out_shape=[SDS((S, K), F32), SDS((1, K), F32)],
        scratch_shapes=[pltpu.VMEM((2, tm, K), F32), pltpu.VMEM((8, K), F32)])(a, w, xin, nw, resid)


def _t5_bucket_np(dist):
    max_exact = N_BUCKETS // 2
    d = np.maximum(dist.astype(np.float32), np.float32(1.0))
    large = max_exact + (np.log(d / np.float32(max_exact)) / np.float32(math.log(MAX_DIST / max_exact))
                         * np.float32(N_BUCKETS - max_exact)).astype(np.int32)
    large = np.minimum(large, N_BUCKETS - 1)
    return np.where(dist < max_exact, dist, large).astype(np.int32)


def _bias_static():
    qi = np.arange(NQ)[:, None]
    kj = np.arange(2 * NQ)[None, :]
    step = qi + NQ - kj
    band = (step >= 0) & (step <= NQ)
    buckets = np.stack([_t5_bucket_np(np.clip(step, 0, None) * dil).reshape(1, -1) for _, dil in ATTN_GROUPS])
    return band, buckets


def _bias_fwd(table_t, buckets):
    nb = buckets.shape[-1]

    def body(t_ref, b_ref, o_ref):
        oh = (b_ref[...] == lax.broadcasted_iota(jnp.int32, (N_BUCKETS, nb), 0)).astype(F32)
        o_ref[...] = jnp.dot(t_ref[...], oh, preferred_element_type=F32, precision=lax.Precision.HIGHEST)

    return pl.pallas_call(
        body, name="bias_fwd", grid=(N_GROUPS,),
        in_specs=[pl.BlockSpec((None, 8, N_BUCKETS), lambda g: (g, 0, 0)), pl.BlockSpec((None, 1, nb), lambda g: (g, 0, 0))],
        out_specs=pl.BlockSpec((None, 8, nb), lambda g: (g, 0, 0)),
        out_shape=SDS((N_GROUPS, 8, nb), F32))(table_t, buckets)


def _bias_bwd(dsb, buckets):
    nb = buckets.shape[-1]

    def body(d_ref, b_ref, o_ref):
        oh = (b_ref[...] == lax.broadcasted_iota(jnp.int32, (N_BUCKETS, nb), 0)).astype(F32)
        o_ref[...] = lax.dot_general(d_ref[...], oh, NT_DIMS, preferred_element_type=F32, precision=lax.Precision.HIGHEST)

    return pl.pallas_call(
        body, name="bias_bwd", grid=(N_GROUPS,),
        in_specs=[pl.BlockSpec((None, 8, nb), lambda g: (g, 0, 0)), pl.BlockSpec((None, 1, nb), lambda g: (g, 0, 0))],
        out_specs=pl.BlockSpec((None, 8, N_BUCKETS), lambda g: (g, 0, 0)),
        out_shape=SDS((N_GROUPS, 8, N_BUCKETS), F32))(dsb, buckets)


def _qkv_prep(z, qw, kw):
    S = z.shape[0]
    nh = N_GROUPS * HPG

    def body(zq, zk, zv, qw_ref, kw_ref, qh, kh, vh):
        for h in range(nh):
            g = h // HPG
            sl = slice(h * HEAD, (h + 1) * HEAD)
            xq = zq[:, sl].astype(F32)
            qh[h] = xq * lax.rsqrt(jnp.mean(xq * xq, axis=-1, keepdims=True) + RMS_EPS) * qw_ref[g:g + 1, :]
            xk = zk[:, sl].astype(F32)
            kh[h] = xk * lax.rsqrt(jnp.mean(xk * xk, axis=-1, keepdims=True) + RMS_EPS) * kw_ref[g:g + 1, :]
            vh[h] = zv[:, sl].astype(F32)

    hm = pl.BlockSpec((nh, TM, HEAD), lambda i: (0, i, 0))
    return pl.pallas_call(
        body, name="qkv_prep", grid=(S // TM,),
        in_specs=[pl.BlockSpec((TM, QKV_W), lambda i: (i, 0)), pl.BlockSpec((TM, QKV_W), lambda i: (i, 1)),
                  pl.BlockSpec((TM, QKV_W), lambda i: (i, 2)), _full((N_GROUPS, HEAD)), _full((N_GROUPS, HEAD))],
        out_specs=[hm, hm, hm],
        out_shape=[SDS((nh, S, HEAD), F32)] * 3)(z, z, z, qw, kw)


def _rows(start, d):
    return pl.ds(start, NQ) if d == 1 else pl.ds(start, NQ, stride=d)


def _attn_fwd(qh, kh, vh, biasm, g, d):
    S = qh.shape[1]
    RB = ATT_RB
    nbk, nq = S // RB, RB // (NQ * d)

    def body(q_ref, k_ref, v_ref, bias_ref, o_ref, lse_ref, kbuf, vbuf):
        b = pl.program_id(1)

        @pl.when(b == 0)
        def _():
            kbuf[0:RB, :] = jnp.zeros((RB, HEAD), F32)
            vbuf[0:RB, :] = jnp.zeros((RB, HEAD), F32)

        @pl.when(b > 0)
        def _():
            kbuf[0:RB, :] = kbuf[RB:2 * RB, :]
            vbuf[0:RB, :] = vbuf[RB:2 * RB, :]

        kbuf[RB:2 * RB, :] = k_ref[...]
        vbuf[RB:2 * RB, :] = v_ref[...]
        bias = bias_ref[...]
        col = lax.broadcasted_iota(jnp.int32, (NQ, 2 * NQ), 1)

        for qb in range(nq):
            def unit(r, carry, qb=qb):
                qs = qb * NQ * d + r
                q = q_ref[_rows(qs, d), :].astype(BF16)
                kw = jnp.concatenate([kbuf[_rows(RB + qs - NQ * d, d), :], kbuf[_rows(RB + qs, d), :]], axis=0).astype(BF16)
                vw = jnp.concatenate([vbuf[_rows(RB + qs - NQ * d, d), :], vbuf[_rows(RB + qs, d), :]], axis=0).astype(BF16)
                s = lax.dot_general(q, kw, NT_DIMS, preferred_element_type=F32) * SCALE + bias
                if qb == 0:
                    s = jnp.where((col < NQ) & (b == 0), NEG, s)
                m = jnp.max(s, axis=-1, keepdims=True)
                p = jnp.exp(s - m)
                l = jnp.sum(p, axis=-1, keepdims=True)
                o = jnp.dot(p.astype(BF16), vw, preferred_element_type=F32) / l
                o_ref[_rows(qs, d), :] = o
                lse_ref[_rows(qs, d), :] = jnp.broadcast_to(m + jnp.log(l), (NQ, HEAD))
                return carry

            for r in range(d):
                unit(r, 0)

    blk = lambda f: pl.BlockSpec((None, RB, HEAD), f)
    return pl.pallas_call(
        body, name=f"attn_fwd_g{g}", grid=(HPG, nbk),
        in_specs=[blk(lambda h, b: (HPG * g + h, b, 0))] * 3 + [pl.BlockSpec((None, NQ, 2 * NQ), lambda h, b: (h, 0, 0))],
        out_specs=[blk(lambda h, b: (h, b, 0))] * 2,
        out_shape=[SDS((HPG, S, HEAD), F32)] * 2,
        scratch_shapes=[pltpu.VMEM((2 * RB, HEAD), F32)] * 2)(qh, kh, vh, biasm)


def _attn_bwd(qh, kh, vh, biasm, da, wg, dh, lse, g, d):
    S = qh.shape[1]
    RB = ATT_RB
    nbk, nq = S // RB, RB // (NQ * d)

    def body(q_ref, k_ref, v_ref, bias_ref, da_ref, wg_ref, dh_ref, lse_ref,
             dq_ref, dk_ref, dv_ref, dsb_ref, kbuf, vbuf, dkbuf, dvbuf):
        b = pl.program_id(1)
        zero = jnp.zeros((RB, HEAD), F32)

        @pl.when(b == 0)
        def _():
            kbuf[0:RB, :] = zero
            vbuf[0:RB, :] = zero
            dkbuf[0:RB, :] = zero
            dvbuf[0:RB, :] = zero
            dsb_ref[...] = jnp.zeros_like(dsb_ref)

        @pl.when(b > 0)
        def _():
            kbuf[0:RB, :] = kbuf[RB:2 * RB, :]
            vbuf[0:RB, :] = vbuf[RB:2 * RB, :]
            dkbuf[0:RB, :] = dkbuf[RB:2 * RB, :]
            dvbuf[0:RB, :] = dvbuf[RB:2 * RB, :]

        dkbuf[RB:2 * RB, :] = zero
        dvbuf[RB:2 * RB, :] = zero

        @pl.when(b < nbk)
        def _():
            kbuf[RB:2 * RB, :] = k_ref[...]
            vbuf[RB:2 * RB, :] = v_ref[...]
            bias = bias_ref[...]
            col = lax.broadcasted_iota(jnp.int32, (NQ, 2 * NQ), 1)

            for qb in range(nq):
                def unit(r, carry, qb=qb):
                    qs = qb * NQ * d + r
                    prev, cur = _rows(RB + qs - NQ * d, d), _rows(RB + qs, d)
                    q = q_ref[_rows(qs, d), :].astype(BF16)
                    kw = jnp.concatenate([kbuf[prev, :], kbuf[cur, :]], axis=0).astype(BF16)
                    vw = jnp.concatenate([vbuf[prev, :], vbuf[cur, :]], axis=0).astype(BF16)
                    s = lax.dot_general(q, kw, NT_DIMS, preferred_element_type=F32) * SCALE + bias
                    if qb == 0:
                        s = jnp.where((col < NQ) & (b == 0), NEG, s)
                    p = jnp.exp(s - lse_ref[_rows(qs, d), :][:, 0:1])
                    w = wg_ref[_rows(qs, d), :]
                    do = (da_ref[_rows(qs, d), :] * w).astype(BF16)
                    dp = lax.dot_general(do, vw, NT_DIMS, preferred_element_type=F32)
                    ds = p * (dp - w[:, 0:1] * dh_ref[_rows(qs, d), :][:, 0:1])
                    dsb_ref[...] += ds
                    dsb = ds.astype(BF16)
                    dq_ref[_rows(qs, d), :] = jnp.dot(dsb, kw, preferred_element_type=F32) * SCALE
                    dkw = lax.dot_general(dsb, q, TN_DIMS, preferred_element_type=F32) * SCALE
                    dvw = lax.dot_general(p.astype(BF16), do, TN_DIMS, preferred_element_type=F32)
                    dkbuf[prev, :] += dkw[0:NQ, :]
                    dkbuf[cur, :] += dkw[NQ:2 * NQ, :]
                    dvbuf[prev, :] += dvw[0:NQ, :]
                    dvbuf[cur, :] += dvw[NQ:2 * NQ, :]
                    return carry

                for r in range(d):
                    unit(r, 0)

        dk_ref[...] = dkbuf[0:RB, :]
        dv_ref[...] = dvbuf[0:RB, :]

    blk = lambda f: pl.BlockSpec((None, RB, HEAD), f)
    cur_g = blk(lambda h, b: (HPG * g + h, jnp.minimum(b, nbk - 1), 0))
    cur = blk(lambda h, b: (h, jnp.minimum(b, nbk - 1), 0))
    prv = blk(lambda h, b: (h, jnp.maximum(b - 1, 0), 0))
    sq = pl.BlockSpec((None, NQ, 2 * NQ), lambda h, b: (h, 0, 0))
    return pl.pallas_call(
        body, name=f"attn_bwd_g{g}", grid=(HPG, nbk + 1),
        in_specs=[cur_g, cur_g, cur_g, sq, cur, cur, cur, cur],
        out_specs=[cur, prv, prv, sq],
        out_shape=[SDS((HPG, S, HEAD), F32)] * 3 + [SDS((HPG, NQ, 2 * NQ), F32)],
        scratch_shapes=[pltpu.VMEM((2 * RB, HEAD), F32)] * 4)(qh, kh, vh, biasm, da, wg, dh, lse)


def _merge_weights(l0, l1, l2):
    m = jnp.maximum(jnp.maximum(l0, l1), l2)
    e0, e1, e2 = jnp.exp(l0 - m), jnp.exp(l1 - m), jnp.exp(l2 - m)
    inv = 1.0 / (e0 + e1 + e2)
    return e0 * inv, e1 * inv, e2 * inv


def _merge_fwd(os_, lses):
    S = os_[0].shape[1]

    def body(o0, o1, o2, l0, l1, l2, a_ref):
        for h in range(HPG):
            w0, w1, w2 = _merge_weights(l0[h], l1[h], l2[h])
            a_ref[:, h * HEAD:(h + 1) * HEAD] = (w0 * o0[h] + w1 * o1[h] + w2 * o2[h]).astype(BF16)

    hm = pl.BlockSpec((HPG, TM, HEAD), lambda i: (0, i, 0))
    return pl.pallas_call(
        body, name="merge_fwd", grid=(S // TM,), in_specs=[hm] * 6,
        out_specs=pl.BlockSpec((TM, CH), lambda i: (i, 0)),
        out_shape=SDS((S, CH), BF16))(*os_, *lses)


def _merge_bwd(dattn, os_, lses):
    S = dattn.shape[0]

    def body(da_ref, o0, o1, o2, l0, l1, l2, w0_ref, w1_ref, w2_ref, dah_ref, dh_ref):
        for h in range(HPG):
            w = _merge_weights(l0[h], l1[h], l2[h])
            attn = w[0] * o0[h] + w[1] * o1[h] + w[2] * o2[h]
            da = da_ref[:, h * HEAD:(h + 1) * HEAD]
            for w_ref, wv in zip((w0_ref, w1_ref, w2_ref), w):
                w_ref[h] = wv
            dah_ref[h] = da
            dh_ref[h] = jnp.broadcast_to(jnp.sum(da * attn, axis=-1, keepdims=True), (TM, HEAD))

    hm = pl.BlockSpec((HPG, TM, HEAD), lambda i: (0, i, 0))
    res = pl.pallas_call(
        body, name="merge_bwd", grid=(S // TM,),
        in_specs=[pl.BlockSpec((TM, CH), lambda i: (i, 0))] + [hm] * 6,
        out_specs=[hm] * 5, out_shape=[SDS((HPG, S, HEAD), F32)] * 5)(dattn, *os_, *lses)
    return res[0:3], res[3], res[4]


def _qkv_bwd(dz, z, dqs, dks, dvs, qw, kw):
    S = z.shape[0]
    nh = N_GROUPS * HPG

    def body(dz_in, zq, zk, *refs):
        del dz_in
        dq_refs, dk_refs, dv_refs = refs[0:3], refs[3:6], refs[6:9]
        qw_ref, kw_ref, dz_ref, dqw_ref, dkw_ref = refs[9:]

        @pl.when(pl.program_id(0) == 0)
        def _():
            dqw_ref[...] = jnp.zeros_like(dqw_ref)
            dkw_ref[...] = jnp.zeros_like(dkw_ref)

        def nbwd(xr, dy, wr, dwr, h, off):
            g = h // HPG
            x = xr[:, h * HEAD:(h + 1) * HEAD].astype(F32)
            r = lax.rsqrt(jnp.mean(x * x, axis=-1, keepdims=True) + RMS_EPS)
            xhat = x * r
            dyw = dy * wr[g:g + 1, :]
            dz_ref[:, off + h * HEAD:off + (h + 1) * HEAD] = (
                r * (dyw - xhat * jnp.mean(dyw * xhat, axis=-1, keepdims=True))).astype(BF16)
            dwr[g:g + 1, :] += jnp.sum(dy * xhat, axis=0, keepdims=True)

        for h in range(nh):
            g, hh = h // HPG, h % HPG
            nbwd(zq, dq_refs[g][hh], qw_ref, dqw_ref, h, O_Q)
            nbwd(zk, dk_refs[g][hh], kw_ref, dkw_ref, h, O_K)
            dz_ref[:, O_V + h * HEAD:O_V + (h + 1) * HEAD] = dv_refs[g][hh].astype(BF16)

    hm = pl.BlockSpec((HPG, TM, HEAD), lambda i: (0, i, 0))
    return pl.pallas_call(
        body, name="qkv_bwd", grid=(S // TM,),
        in_specs=[pl.BlockSpec(memory_space=pl.ANY), pl.BlockSpec((TM, QKV_W), lambda i: (i, 0)),
                  pl.BlockSpec((TM, QKV_W), lambda i: (i, 1))] + [hm] * 9 + [_full((N_GROUPS, HEAD)), _full((N_GROUPS, HEAD))],
        out_specs=[pl.BlockSpec((TM, 3 * QKV_W), lambda i: (i, 0)), _full((N_GROUPS, HEAD)), _full((N_GROUPS, HEAD))],
        out_shape=[SDS(dz.shape, BF16), SDS((N_GROUPS, HEAD), F32), SDS((N_GROUPS, HEAD), F32)],
        input_output_aliases={0: 0})(dz, z, z, *dqs, *dks, *dvs, qw, kw)


def _conv_fwd(z, cw, cb, lnw, lnb):
    S = z.shape[0]
    H = 32

    def body(zv, zg, cw_ref, cb_ref, lnw_ref, lnb_ref, u1_ref, u3_ref, xbuf):
        i = pl.program_id(0)

        @pl.when(i == 0)
        def _():
            xbuf[0:H, :] = jnp.zeros((H, CH), F32)

        @pl.when(i > 0)
        def _():
            xbuf[0:H, :] = xbuf[TM:TM + H, :]

        xbuf[H:H + TM, :] = zv[...].astype(F32) * _sig(zg[...].astype(F32))
        for r0 in range(0, TM, CONV_RC):
            rows = pl.ds(r0, CONV_RC)
            parts = []
            for lanes in CONV_LANES:
                part = [jnp.broadcast_to(cb_ref[:, lanes], (CONV_RC, CONV_LB))]

                def tap(q, view, part=part, lanes=lanes):
                    part[0] = part[0] + view * cw_ref[q - 2:q - 1, lanes]

                _conv_taps(xbuf, r0, lanes, 2, CONV_K + 1, tap)
                parts.append(part[0])
            acc = jnp.concatenate(parts, axis=1)
            u1_ref[rows, :] = acc
            mu = jnp.mean(acc, axis=-1, keepdims=True)
            xc = acc - mu
            yl = xc * lax.rsqrt(jnp.mean(xc * xc, axis=-1, keepdims=True) + LN_EPS) * lnw_ref[...] + lnb_ref[...]
            u3_ref[rows, :] = (yl * _sig(yl)).astype(BF16)

    row = pl.BlockSpec((TM, CH), lambda i: (i, 0))
    return pl.pallas_call(
        body, name="conv_fwd", grid=(S // TM,),
        in_specs=[pl.BlockSpec((TM, CH), lambda i: (i, O_CV // CH)), pl.BlockSpec((TM, CH), lambda i: (i, O_CG // CH)),
                  _full((CONV_K, CH)), _full((1, CH)), _full((1, CH)), _full((1, CH))],
        out_specs=[row, row], out_shape=[SDS((S, CH), F32), SDS((S, CH), BF16)],
        scratch_shapes=[pltpu.VMEM((TM + H, CH), F32)])(z, z, cw, cb, lnw, lnb)


def _conv_bwd_a(du3, u1, z, lnw, lnb):
    S = z.shape[0]
    H = 32

    nt = S // TM

    def body(du3_ref, u1_ref, zv, zg, lnw_ref, lnb_ref, du1_ref, acc_ref, xbuf, tacc):
        i = pl.program_id(0)

        @pl.when(i == 0)
        def _():
            xbuf[0:H, :] = jnp.zeros((H, CH), F32)
            tacc[...] = jnp.zeros_like(tacc)

        @pl.when(i > 0)
        def _():
            xbuf[0:H, :] = xbuf[TM:TM + H, :]

        xbuf[H:H + TM, :] = zv[...].astype(F32) * _sig(zg[...].astype(F32))
        for r0 in range(0, TM, CONV_RC):
            rows = pl.ds(r0, CONV_RC)
            u1 = u1_ref[rows, :]
            mu = jnp.mean(u1, axis=-1, keepdims=True)
            xc = u1 - mu
            r = lax.rsqrt(jnp.mean(xc * xc, axis=-1, keepdims=True) + LN_EPS)
            yhat = xc * r
            yl = yhat * lnw_ref[...] + lnb_ref[...]
            sg = _sig(yl)
            dyl = du3_ref[rows, :] * (sg * (1.0 + yl * (1.0 - sg)))
            dyh = dyl * lnw_ref[...]
            du1 = r * (dyh - jnp.mean(dyh, axis=-1, keepdims=True) - yhat * jnp.mean(dyh * yhat, axis=-1, keepdims=True))
            du1_ref[rows, :] = du1
            tacc[33] += _fold8(dyl * yhat)
            tacc[34] += _fold8(dyl)
            tacc[32] += _fold8(du1)
            for lanes in CONV_LANES:
                d = du1[:, lanes]

                def tap(q, view, d=d, lanes=lanes):
                    tacc[q - 2, :, lanes] += _fold8(d * view)

                _conv_taps(xbuf, r0, lanes, 2, CONV_K + 1, tap)

        @pl.when(i == nt - 1)
        def _():
            for k in range(40):
                acc_ref[k:k + 1, :] = jnp.sum(tacc[k], axis=0, keepdims=True)

    row = pl.BlockSpec((TM, CH), lambda i: (i, 0))
    return pl.pallas_call(
        body, name="conv_bwd_a", grid=(nt,),
        in_specs=[row, row, pl.BlockSpec((TM, CH), lambda i: (i, O_CV // CH)), pl.BlockSpec((TM, CH), lambda i: (i, O_CG // CH)),
                  _full((1, CH)), _full((1, CH))],
        out_specs=[row, _full((40, CH))], out_shape=[SDS((S, CH), F32), SDS((40, CH), F32)],
        scratch_shapes=[pltpu.VMEM((TM + H, CH), F32), pltpu.VMEM((40, 8, CH), F32)])(du3, u1, z, z, lnw, lnb)


def _conv_bwd_b(dz, du1, z, cw):
    S = z.shape[0]
    nt = S // TM
    H = 32

    def body(dz_in, du1_ref, zv, zg, cw_ref, dz_ref, ybuf, dgate):
        del dz_in
        i, p = pl.program_id(0), pl.program_id(1)

        @pl.when(p == 0)
        def _():
            @pl.when(i == 0)
            def _():
                ybuf[TM:TM + H, :] = jnp.zeros((H, CH), F32)

            @pl.when(i > 0)
            def _():
                ybuf[TM:TM + H, :] = ybuf[0:H, :]

            ybuf[0:TM, :] = du1_ref[...]
            for r0 in range(0, TM, CONV_RC):
                rows = pl.ds(r0, CONV_RC)
                parts = []
                for lanes in CONV_LANES:
                    part = [jnp.zeros((CONV_RC, CONV_LB), F32)]

                    def tap(q, view, part=part, lanes=lanes):
                        part[0] = part[0] + view * cw_ref[CONV_K - 1 - q:CONV_K - q, lanes]

                    _conv_taps(ybuf, r0, lanes, 0, CONV_K - 1, tap)
                    parts.append(part[0])
                acc = jnp.concatenate(parts, axis=1)
                val = zv[rows, :].astype(F32)
                sg = _sig(zg[rows, :].astype(F32))
                dz_ref[rows, :] = (acc * sg).astype(BF16)
                dgate[rows, :] = (acc * val * sg * (1.0 - sg)).astype(BF16)

        @pl.when(p == 1)
        def _():
            dz_ref[...] = dgate[...]

    rev = lambda c: pl.BlockSpec((TM, CH), lambda i, p: (nt - 1 - i, c))
    return pl.pallas_call(
        body, name="conv_bwd_b", grid=(nt, 2),
        in_specs=[pl.BlockSpec(memory_space=pl.ANY), rev(0), rev(O_CV // CH), rev(O_CG // CH), _full((CONV_K, CH))],
        out_specs=pl.BlockSpec((TM, CH), lambda i, p: (nt - 1 - i, O_CV // CH + p)),
        out_shape=SDS(dz.shape, BF16),
        scratch_shapes=[pltpu.VMEM((TM + H, CH), F32), pltpu.VMEM((TM, CH), BF16)],
        input_output_aliases={0: 0})(dz, du1, z, z, cw)


def _memkv_fwd(mem, mnw, wkv, xkw):
    M, D = mem.shape

    def body(mem_ref, mnw_ref, w_ref, xkw_ref, mn_ref, kv_ref, mk_ref, mv_ref):
        x = mem_ref[...]
        mn = (x * lax.rsqrt(jnp.mean(x * x, axis=-1, keepdims=True) + RMS_EPS) * mnw_ref[...]).astype(BF16)
        mn_ref[...] = mn
        kv = jnp.dot(mn, w_ref[...], preferred_element_type=F32)
        kv_ref[...] = kv
        for h in range(HPG):
            k = kv[:, h * HEAD:(h + 1) * HEAD]
            mk_ref[:, h * HEAD:(h + 1) * HEAD] = (
                k * lax.rsqrt(jnp.mean(k * k, axis=-1, keepdims=True) + RMS_EPS) * xkw_ref[...]).astype(BF16)
        mv_ref[...] = kv[:, CH:2 * CH].astype(BF16)

    return pl.pallas_call(
        body, name="memkv_fwd",
        out_shape=[SDS((M, D), BF16), SDS((M, 2 * CH), F32), SDS((M, CH), BF16), SDS((M, CH), BF16)])(mem, mnw, wkv, xkw)


def _cross_q(zx, xqw, h):
    x = zx[:, h * HEAD:(h + 1) * HEAD].astype(F32)
    r = lax.rsqrt(jnp.mean(x * x, axis=-1, keepdims=True) + RMS_EPS)
    xhat = x * r
    return xhat, r, xhat * xqw


def _cross_fwd(z, xqw, mk, mv):
    S = z.shape[0]
    M = mk.shape[0]

    def body(zx, xqw_ref, mk_ref, mv_ref, o_ref):
        for h in range(HPG):
            sl = slice(h * HEAD, (h + 1) * HEAD)
            _, _, q = _cross_q(zx, xqw_ref[...], h)
            s = lax.dot_general(q.astype(BF16), mk_ref[:, sl], NT_DIMS, preferred_element_type=F32) * SCALE
            e = jnp.exp(s - jnp.max(s, axis=-1, keepdims=True))
            p = e / jnp.sum(e, axis=-1, keepdims=True)
            o_ref[:, sl] = jnp.dot(p.astype(BF16), mv_ref[:, sl], preferred_element_type=F32).astype(BF16)

    return pl.pallas_call(
        body, name="cross_fwd", grid=(S // TM,),
        in_specs=[pl.BlockSpec((TM, CH), lambda i: (i, O_XQ // CH)), _full((1, HEAD)), _full((M, CH)), _full((M, CH))],
        out_specs=pl.BlockSpec((TM, CH), lambda i: (i, 0)), out_shape=SDS((S, CH), BF16))(z, xqw, mk, mv)


def _cross_bwd(dz, doc, z, xqw, mk, mv):
    S = z.shape[0]
    M = mk.shape[0]

    def body(dz_in, do_ref, zx, xqw_ref, mk_ref, mv_ref, dz_ref, dmk_ref, dmv_ref, dxw_ref):
        del dz_in

        @pl.when(pl.program_id(0) == 0)
        def _():
            dmk_ref[...] = jnp.zeros_like(dmk_ref)
            dmv_ref[...] = jnp.zeros_like(dmv_ref)
            dxw_ref[...] = jnp.zeros_like(dxw_ref)

        for h in range(HPG):
            sl = slice(h * HEAD, (h + 1) * HEAD)
            xhat, r, q = _cross_q(zx, xqw_ref[...], h)
            qb = q.astype(BF16)
            s = lax.dot_general(qb, mk_ref[:, sl], NT_DIMS, preferred_element_type=F32) * SCALE
            e = jnp.exp(s - jnp.max(s, axis=-1, keepdims=True))
            p = e / jnp.sum(e, axis=-1, keepdims=True)
            do = do_ref[:, sl].astype(BF16)
            dp = lax.dot_general(do, mv_ref[:, sl], NT_DIMS, preferred_element_type=F32)
            ds = (p * (dp - jnp.sum(p * dp, axis=-1, keepdims=True)) * SCALE).astype(BF16)
            dq = jnp.dot(ds, mk_ref[:, sl], preferred_element_type=F32)
            dmk_ref[:, sl] += lax.dot_general(ds, qb, TN_DIMS, preferred_element_type=F32)
            dmv_ref[:, sl] += lax.dot_general(p.astype(BF16), do, TN_DIMS, preferred_element_type=F32)
            dyw = dq * xqw_ref[...]
            dz_ref[:, sl] = (r * (dyw - xhat * jnp.mean(dyw * xhat, axis=-1, keepdims=True))).astype(BF16)
            dxw_ref[...] += jnp.sum(dq * xhat, axis=0, keepdims=True)

    return pl.pallas_call(
        body, name="cross_bwd", grid=(S // TM,),
        in_specs=[pl.BlockSpec(memory_space=pl.ANY), pl.BlockSpec((TM, CH), lambda i: (i, 0)),
                  pl.BlockSpec((TM, CH), lambda i: (i, O_XQ // CH)), _full((1, HEAD)), _full((M, CH)), _full((M, CH))],
        out_specs=[pl.BlockSpec((TM, CH), lambda i: (i, O_XQ // CH)), _full((M, CH)), _full((M, CH)), _full((1, HEAD))],
        out_shape=[SDS(dz.shape, BF16), SDS((M, CH), F32), SDS((M, CH), F32), SDS((1, HEAD), F32)],
        input_output_aliases={0: 0})(dz, doc, z, xqw, mk, mv)


def _memkv_bwd(dmk, dmv, kv, mem, mn, mnw, wkv, xkw):
    M, D = mem.shape

    def body(dmk_ref, dmv_ref, kv_ref, mem_ref, mn_ref, mnw_ref, w_ref, xkw_ref, dw_ref, dxk_ref, dmn_ref, dkv):
        dxk = jnp.zeros((1, HEAD), F32)
        for h in range(HPG):
            sl = slice(h * HEAD, (h + 1) * HEAD)
            k = kv_ref[:, sl]
            r = lax.rsqrt(jnp.mean(k * k, axis=-1, keepdims=True) + RMS_EPS)
            khat = k * r
            dy = dmk_ref[:, sl]
            dyw = dy * xkw_ref[...]
            dkv[:, sl] = (r * (dyw - khat * jnp.mean(dyw * khat, axis=-1, keepdims=True))).astype(BF16)
            dxk = dxk + jnp.sum(dy * khat, axis=0, keepdims=True)
        dxk_ref[...] = dxk
        dkv[:, CH:2 * CH] = dmv_ref[...].astype(BF16)
        dw_ref[...] = lax.dot_general(mn_ref[...], dkv[...], TN_DIMS, preferred_element_type=F32)
        dn = lax.dot_general(dkv[...], w_ref[...], NT_DIMS, preferred_element_type=F32)
        x = mem_ref[...]
        r = lax.rsqrt(jnp.mean(x * x, axis=-1, keepdims=True) + RMS_EPS)
        dmn_ref[...] = jnp.sum(dn * x * r, axis=0, keepdims=True)

    return pl.pallas_call(
        body, name="memkv_bwd",
        out_shape=[SDS((D, 2 * CH), F32), SDS((1, HEAD), F32), SDS((1, D), F32)],
        scratch_shapes=[pltpu.VMEM((M, 2 * CH), BF16)])(dmk, dmv, kv, mem, mn, mnw, wkv, xkw)


def _branch_proj(a_ref, w_ref, y_ref):
    G, _, n = w_ref.shape
    a = a_ref[...]
    for g in range(G):
        y_ref[:, g * n:(g + 1) * n] = jnp.dot(a, w_ref[g], preferred_element_type=F32)


OUT_RC = 16


def _gates(zg_ref, bg_ref, rows, k, D):
    return _sig(zg_ref[rows, k * D:(k + 1) * D].astype(F32) + bg_ref[:, k * D:(k + 1) * D])


def _outproj_fwd(x, z, bg, attn, u3, oc, wao, wco, wxo, wout, fnw):
    S, D = x.shape
    tm = TM

    def body(x_ref, zg_ref, bg_ref, a_ref, u_ref, c_ref, wa, wc, wx, wo, fnw_ref, h1_ref, hn_ref, ya, yc, yx, mg):
        _branch_proj(a_ref, wa, ya)
        _branch_proj(u_ref, wc, yc)
        _branch_proj(c_ref, wx, yx)
        for r0 in range(0, tm, OUT_RC):
            rows = pl.ds(r0, OUT_RC)
            mg[rows, :] = (_gates(zg_ref, bg_ref, rows, 0, D) * ya[rows, :] + _gates(zg_ref, bg_ref, rows, 1, D) * yc[rows, :]
                           + _gates(zg_ref, bg_ref, rows, 2, D) * yx[rows, :]).astype(BF16)
        ya[...] = jnp.dot(mg[...], wo[...], preferred_element_type=F32)
        for r0 in range(0, tm, OUT_RC):
            rows = pl.ds(r0, OUT_RC)
            h1 = x_ref[rows, :] + ya[rows, :]
            h1_ref[rows, :] = h1
            hn_ref[rows, :] = (h1 * lax.rsqrt(jnp.mean(h1 * h1, axis=-1, keepdims=True) + RMS_EPS) * fnw_ref[...]).astype(BF16)

    row = lambda w: pl.BlockSpec((tm, w), lambda i: (i, 0))
    return pl.pallas_call(
        body, name="outproj_fwd", grid=(S // tm,),
        in_specs=[row(D), pl.BlockSpec((tm, 3 * D), lambda i: (i, O_G // (3 * D))), _full((1, 3 * D)), row(CH), row(CH), row(CH),
                  _full(wao.shape), _full(wco.shape), _full(wxo.shape), _full((D, D)), _full((1, D))],
        out_specs=[row(D), row(D)], out_shape=[SDS((S, D), F32), SDS((S, D), BF16)],
        scratch_shapes=[pltpu.VMEM((tm, D), F32)] * 3 + [pltpu.VMEM((tm, D), BF16)])(x, z, bg, attn, u3, oc, wao, wco, wxo, wout, fnw)


def _outproj_bwd(dh1, z, bg, attn, u3, oc, wao, wco, wxo, wout, n_in):
    S, D = dh1.shape
    tm = 256
    nt = S // tm
    G, _, n = wao.shape

    def body(dh_ref, zg_ref, bg_ref, a_ref, u_ref, c_ref, wa, wc, wx, wo,
             dz_ref, da_ref, du_ref, dc_ref, dbg_ref, dwo_ref, dwa_ref, dwc_ref, dwx_ref,
             ya, yc, yx, dm, dy, mg, bacc, wacc):
        i = pl.program_id(0)

        @pl.when(i == 0)
        def _():
            bacc[...] = jnp.zeros_like(bacc)
            wacc[...] = jnp.zeros_like(wacc)
            dwo_ref[...] = jnp.zeros_like(dwo_ref)

        _branch_proj(a_ref, wa, ya)
        _branch_proj(u_ref, wc, yc)
        _branch_proj(c_ref, wx, yx)
        dhb = dh_ref[...].astype(BF16)
        dm[...] = lax.dot_general(dhb, wo[...], NT_DIMS, preferred_element_type=F32)
        for r0 in range(0, tm, OUT_RC):
            rows = pl.ds(r0, OUT_RC)
            dmv = dm[rows, :]
            merged = jnp.zeros((OUT_RC, D), F32)
            for k, y in enumerate((ya, yc, yx)):
                gk = _gates(zg_ref, bg_ref, rows, k, D)
                yk = y[rows, :]
                merged = merged + gk * yk
                dzg = dmv * yk * gk * (1.0 - gk)
                dz_ref[rows, k * D:(k + 1) * D] = dzg.astype(BF16)
                bacc[:, k * D:(k + 1) * D] += _fold8(dzg)
                dy[k, rows, :] = (dmv * gk).astype(BF16)
            mg[rows, :] = merged.astype(BF16)
        dwo_ref[...] += lax.dot_general(mg[...], dhb, TN_DIMS, preferred_element_type=F32)
        for k, (b_ref, w_ref, db_ref) in enumerate(((a_ref, wa, da_ref), (u_ref, wc, du_ref), (c_ref, wx, dc_ref))):
            dyk = dy[k]
            acc = jnp.zeros((tm, CH), F32)
            for g in range(G):
                acc = acc + lax.dot_general(dyk[:, g * n:(g + 1) * n], w_ref[g], NT_DIMS, preferred_element_type=F32)
            db_ref[...] = acc
            wacc[k] += lax.dot_general(b_ref[...], dyk, TN_DIMS, preferred_element_type=F32)

        @pl.when(i == nt - 1)
        def _():
            dbg_ref[...] = jnp.sum(bacc[...], axis=0, keepdims=True)
            for k, dw_ref in enumerate((dwa_ref, dwc_ref, dwx_ref)):
                for g in range(G):
                    dw_ref[g] = wacc[k, :, g * n:(g + 1) * n]

    row = lambda w: pl.BlockSpec((tm, w), lambda i: (i, 0))
    return pl.pallas_call(
        body, name="outproj_bwd", grid=(nt,),
        in_specs=[row(D), pl.BlockSpec((tm, 3 * D), lambda i: (i, O_G // (3 * D))), _full((1, 3 * D)), row(CH), row(CH), row(CH),
                  _full(wao.shape), _full(wco.shape), _full(wxo.shape), _full((D, D))],
        out_specs=[pl.BlockSpec((tm, 3 * D), lambda i: (i, O_G // (3 * D))), row(CH), row(CH), row(CH), _full((1, 3 * D)),
                   _full((D, D))] + [_full(wao.shape)] * 3,
        out_shape=[SDS((S, n_in), BF16)] + [SDS((S, CH), F32)] * 3 + [SDS((1, 3 * D), F32), SDS((D, D), F32)]
        + [SDS(wao.shape, F32)] * 3,
        scratch_shapes=[pltpu.VMEM((tm, D), F32)] * 4 + [pltpu.VMEM((3, tm, D), BF16), pltpu.VMEM((tm, D), BF16),
                                                        pltpu.VMEM((8, 3 * D), F32), pltpu.VMEM((3, CH, D), F32)],
    )(dh1, z, bg, attn, u3, oc, wao, wco, wxo, wout)


FFN_TC = 256
FFN_H = 8


def _ffn_taps(buf, r0):
    xx = buf[pl.ds(r0, FFN_RC + FFN_H), :]
    return xx[FFN_H:], pltpu.roll(xx, 1, 0)[FFN_H:], pltpu.roll(xx, 2, 0)[FFN_H:]


def _ffn_conv(taps, w_ref, b_ref):
    x0, x1, x2 = taps
    return b_ref[...] + x0 * w_ref[2:3, :] + x1 * w_ref[1:2, :] + x2 * w_ref[0:1, :]


def _ffn_fwd(up, cw, cb, wdown, h1, target):
    S, D = h1.shape
    F2 = up.shape[1]
    F = F2 // 2
    nj = F // FFN_TC
    tm = TM

    def body(up_ref, halo_ref, cw_ref, cb_ref, w_ref, h_ref, t_ref, dy_ref, loss_ref, ac_ref, gc_ref, abuf, gbuf, act_s):
        i = pl.program_id(0)

        @pl.when(i == 0)
        def _():
            loss_ref[...] = jnp.zeros_like(loss_ref)

        for j in range(nj):
            ca, cg = slice(j * FFN_TC, (j + 1) * FFN_TC), slice(F + j * FFN_TC, F + (j + 1) * FFN_TC)
            first = i == 0
            abuf[0:FFN_H, :] = jnp.where(first, 0.0, halo_ref[:, ca].astype(F32))
            gbuf[0:FFN_H, :] = jnp.where(first, 0.0, halo_ref[:, cg].astype(F32))
            abuf[FFN_H:FFN_H + tm, :] = up_ref[:, ca].astype(F32)
            gbuf[FFN_H:FFN_H + tm, :] = up_ref[:, cg].astype(F32)
            for r0 in range(0, tm, FFN_RC):
                rows = pl.ds(r0, FFN_RC)
                a = _ffn_conv(_ffn_taps(abuf, r0), cw_ref[:, ca], cb_ref[:, ca])
                gt = _ffn_conv(_ffn_taps(gbuf, r0), cw_ref[:, cg], cb_ref[:, cg])
                ac_ref[rows, ca] = a.astype(BF16)
                gc_ref[rows, ca] = gt.astype(BF16)
                act_s[rows, ca] = (gt * _sig(gt) * a).astype(BF16)
        err = h_ref[...] + jnp.dot(act_s[...], w_ref[...], preferred_element_type=F32) - t_ref[...]
        dy_ref[...] = err * (1.0 / D)
        loss_ref[...] += 0.5 * jnp.sum(jnp.mean(err * err, axis=-1, keepdims=True))

    row = lambda w: pl.BlockSpec((tm, w), lambda i: (i, 0))
    halo = pl.BlockSpec((FFN_H, F2), lambda i: (jnp.maximum(i * (tm // FFN_H) - 1, 0), 0))
    return pl.pallas_call(
        body, name="ffn_fwd", grid=(S // tm,),
        in_specs=[row(F2), halo, _full((FFN_K, F2)), _full((1, F2)), _full((F, D)), row(D), row(D)],
        out_specs=[row(D), _full((8, 128)), row(F), row(F)],
        out_shape=[SDS((S, D), F32), SDS((8, 128), F32), SDS((S, F), BF16), SDS((S, F), BF16)],
        scratch_shapes=[pltpu.VMEM((tm + FFN_H, FFN_TC), F32)] * 2 + [pltpu.VMEM((tm, F), BF16)])(up, up, cw, cb, wdown, h1, target)


def _ffn_bwd_a(dy, wdown, ac, gc):
    S, D = dy.shape
    F = ac.shape[1]
    nj = F // FFN_TC
    tm = 1024 if S % 1024 == 0 else TM

    def body(dy_ref, wd_ref, a_ref, g_ref, da_ref, dg_ref, acca_ref, accg_ref, dwd_ref, dact_s, act_s):
        i, j = pl.program_id(0), pl.program_id(1)

        @pl.when((i == 0) & (j == 0))
        def _():
            acca_ref[...] = jnp.zeros_like(acca_ref)
            accg_ref[...] = jnp.zeros_like(accg_ref)
            dwd_ref[...] = jnp.zeros_like(dwd_ref)

        dyb = dy_ref[...].astype(BF16)
        dact_s[...] = lax.dot_general(dyb, wd_ref[...], NT_DIMS, preferred_element_type=F32)
        pa = pg = jnp.zeros((8, FFN_TC), F32)
        for r0 in range(0, tm, FFN_RC):
            rows = pl.ds(r0, FFN_RC)
            a = a_ref[rows, :].astype(F32)
            gt = g_ref[rows, :].astype(F32)
            dact = dact_s[rows, :]
            sg = _sig(gt)
            silu = gt * sg
            act_s[rows, :] = (silu * a).astype(BF16)
            dac = dact * silu
            dgc = dact * a * (sg * (1.0 + gt * (1.0 - sg)))
            da_ref[rows, :] = dac.astype(BF16)
            dg_ref[rows, :] = dgc.astype(BF16)
            pa = pa + _fold8(dac)
            pg = pg + _fold8(dgc)
        acca_ref[j] += pa
        accg_ref[j] += pg
        dwd_ref[pl.ds(pl.multiple_of(j * FFN_TC, FFN_TC), FFN_TC), :] += lax.dot_general(
            act_s[...], dyb, TN_DIMS, preferred_element_type=F32)

    col = pl.BlockSpec((tm, FFN_TC), lambda i, j: (i, j))
    return pl.pallas_call(
        body, name="ffn_bwd_a", grid=(S // tm, nj),
        in_specs=[pl.BlockSpec((tm, D), lambda i, j: (i, 0)), pl.BlockSpec((FFN_TC, D), lambda i, j: (j, 0)), col, col],
        out_specs=[col, col] + [_full((nj, 8, FFN_TC))] * 2 + [_full((F, D))],
        out_shape=[SDS((S, F), BF16)] * 2 + [SDS((nj, 8, FFN_TC), F32)] * 2 + [SDS((F, D), F32)],
        scratch_shapes=[pltpu.VMEM((tm, FFN_TC), F32), pltpu.VMEM((tm, FFN_TC), BF16)])(dy, wdown, ac, gc)


def _ffn_bwd_b(dca, dcg, up, cw):
    S, F = dca.shape
    nj = F // FFN_TC
    tm = 4096 if S % 4096 == 0 else TM
    nt = S // tm
    span = FFN_RC + FFN_H

    def body(a_ref, g_ref, u_ref, w_ref, o_ref, tacc_ref, ybuf):
        j, i = pl.program_id(0), pl.program_id(1)

        @pl.when(i == 0)
        def _():
            ybuf[tm:tm + FFN_H, :] = jnp.zeros((FFN_H, FFN_TC), F32)
            tacc_ref[...] = jnp.zeros_like(tacc_ref)

        @pl.when(i > 0)
        def _():
            ybuf[tm:tm + FFN_H, :] = ybuf[0:FFN_H, :]

        ybuf[0:tm, :] = jnp.where(j < nj, a_ref[...], g_ref[...]).astype(F32)
        p = [jnp.zeros((8, FFN_TC), F32)] * FFN_K
        for r0 in range(0, tm, FFN_RC):
            rows = pl.ds(r0, FFN_RC)
            yy = ybuf[pl.ds(r0, span), :]
            ys = (yy[:FFN_RC], pltpu.roll(yy, span - 1, 0)[:FFN_RC], pltpu.roll(yy, span - 2, 0)[:FFN_RC])
            o_ref[rows, :] = (ys[0] * w_ref[2:3, :] + ys[1] * w_ref[1:2, :] + ys[2] * w_ref[0:1, :]).astype(BF16)
            u = u_ref[rows, :].astype(F32)
            for k in range(FFN_K):
                p[k] = p[k] + _fold8(ys[FFN_K - 1 - k] * u)
        for k in range(FFN_K):
            tacc_ref[k] += p[k]

    rev = lambda f: pl.BlockSpec((tm, FFN_TC), lambda j, i: (nt - 1 - i, f(j)))
    return pl.pallas_call(
        body, name="ffn_bwd_b", grid=(2 * nj, nt),
        in_specs=[rev(lambda j: jnp.minimum(j, nj - 1)), rev(lambda j: jnp.maximum(j - nj, 0)), rev(lambda j: j),
                  pl.BlockSpec((FFN_K, FFN_TC), lambda j, i: (0, j))],
        out_specs=[rev(lambda j: j), pl.BlockSpec((None, FFN_K, 8, FFN_TC), lambda j, i: (j, 0, 0, 0))],
        out_shape=[SDS((S, 2 * F), BF16), SDS((2 * nj, FFN_K, 8, FFN_TC), F32)],
        scratch_shapes=[pltpu.VMEM((tm + FFN_H, FFN_TC), F32)])(dca, dcg, up, cw)


def _adamw_update(w_ref, g_ref, m_ref, v_ref, d_ref, nm_ref, nv_ref):
    gv = g_ref[...]
    m2 = ADAM_B1 * m_ref[...] + (1.0 - ADAM_B1) * gv
    v2 = ADAM_B2 * v_ref[...] + (1.0 - ADAM_B2) * jnp.square(gv)
    m_hat = m2 / (1.0 - ADAM_B1 ** ADAM_STEP)
    v_hat = v2 / (1.0 - ADAM_B2 ** ADAM_STEP)
    d_ref[...] = -ADAM_LR * (m_hat / (jnp.sqrt(v_hat) + ADAM_EPS) + ADAM_WD * w_ref[...])
    nm_ref[...] = m2
    nv_ref[...] = v2


def _adamw_small(ws, gs, ms, vs):
    n = len(ws)

    def body(*refs):
        for i in range(n):
            _adamw_update(*[refs[k * n + i] for k in range(7)])

    shapes = [SDS(w.shape, F32) for w in ws]
    res = pl.pallas_call(body, name="adamw_small", out_shape=shapes * 3)(*ws, *gs, *ms, *vs)
    return res[:n], res[n:2 * n], res[2 * n:]


def _adamw(w, g, m, v, name):
    R, C = w.shape
    tr = _row_tile(R, max(8, (2 ** 20) // (4 * C) // 8 * 8))

    def body(w_ref, g_ref, m_ref, v_ref, d_ref, nm_ref, nv_ref):
        _adamw_update(w_ref, g_ref, m_ref, v_ref, d_ref, nm_ref, nv_ref)

    blk = pl.BlockSpec((tr, C), lambda i: (i, 0))
    return pl.pallas_call(
        body, name=name, grid=(R // tr,), in_specs=[blk] * 4, out_specs=[blk] * 3,
        out_shape=[SDS((R, C), F32)] * 3)(w, g, m, v)


HBM_SPEC = pl.BlockSpec(memory_space=pltpu.HBM)
SEM_SPEC = pl.BlockSpec(memory_space=pltpu.SEMAPHORE)
DATAFLOW_EFFECT = pltpu.SideEffectType.DATAFLOW_SIDE_EFFECTING


def _position():
    return lax.axis_index("x"), lax.axis_index("y"), lax.axis_index("c")


def _other_chips(x, y):
    return [(1 - x, y), (x, 1 - y), (1 - x, 1 - y)]


def _all_gather_xy(arrs, name):
    n = len(arrs)
    hbm = pl.BlockSpec(memory_space=pl.ANY)

    def body(*refs):
        ins, outs = refs[:n], refs[n:2 * n]
        send_sems, recv_sems = refs[2 * n:]
        x, y, c = _position()
        me = 2 * x + y
        chips = _other_chips(x, y)

        def rcopy(i, k, src, dst, to):
            return pltpu.make_async_remote_copy(src_ref=src, dst_ref=dst, send_sem=send_sems.at[i, k], recv_sem=recv_sems.at[i, k],
                                                device_id=to, device_id_type=MESH)

        sends = []
        for i in range(n):
            own = rcopy(i, 6, ins[i], outs[i].at[me], (x, y, 1 - c))
            own.start()
            sends.append(own)
        for i in range(n):
            for j, (px, py) in enumerate(chips):
                cp = rcopy(i, j, ins[i].at[c], outs[i].at[me, c], (px, py, c))
                cp.start()
                sends.append(cp)
        for i in range(n):
            for j, (px, py) in enumerate(chips):
                got = outs[i].at[2 * px + py, c]
                rcopy(i, j, ins[i].at[c], got, (x, y, c)).wait_recv()
                fwd = rcopy(i, 3 + j, got, got, (x, y, 1 - c))
                fwd.start()
                sends.append(fwd)
        for i in range(n):
            for j, (px, py) in enumerate(chips):
                theirs = outs[i].at[2 * px + py, 1 - c]
                rcopy(i, 3 + j, theirs, theirs, (x, y, c)).wait_recv()
        for i in range(n):
            rcopy(i, 6, ins[i], outs[i].at[me], (x, y, c)).wait_recv()
        for cp in sends:
            cp.wait_send()

    return pl.pallas_call(
        body, name=name, in_specs=[hbm] * n, out_specs=[hbm] * n,
        out_shape=[SDS((4,) + a.shape, a.dtype) for a in arrs],
        scratch_shapes=[pltpu.SemaphoreType.DMA((n, 7)), pltpu.SemaphoreType.DMA((n, 7))])(*arrs)


def _ag_ici_start(arrs, name):
    n = len(arrs)

    def body(*refs):
        ins, lands = refs[:n], refs[n:2 * n]
        send_sems, recv_sems = refs[2 * n:2 * n + 2]
        token = refs[-1]
        x, y, c = _position()
        for i in range(n):
            for j, (px, py) in enumerate(_other_chips(x, y)):
                pltpu.make_async_remote_copy(src_ref=ins[i].at[c], dst_ref=lands[i].at[2 * x + y, c], send_sem=send_sems.at[3 * i + j],
                                             recv_sem=recv_sems.at[3 * i + j], device_id=(px, py, c), device_id_type=MESH).start()
        token[...] = jnp.zeros_like(token)

    lands = [lax.empty((4,) + a.shape, a.dtype) for a in arrs]
    res = pl.pallas_call(
        body, name=name,
        out_shape=[pltpu.SemaphoreType.DMA((3 * n,)), pltpu.SemaphoreType.DMA((3 * n,))]
        + [pltpu.HBM(a.shape, a.dtype) for a in arrs] + [pltpu.HBM(l.shape, l.dtype) for l in lands] + [SDS((8, 128), F32)],
        in_specs=[HBM_SPEC] * (2 * n), out_specs=[SEM_SPEC, SEM_SPEC] + [HBM_SPEC] * (2 * n) + [pl.BlockSpec(memory_space=pltpu.VMEM)],
        input_output_aliases={i: 2 + i for i in range(2 * n)},
        compiler_params=pltpu.CompilerParams(has_side_effects=DATAFLOW_EFFECT),
    )(*[pltpu.with_memory_space_constraint(a, pltpu.HBM) for a in list(arrs) + lands])
    return res[0], res[1], list(res[2:2 + n]), list(res[2 + n:2 + 2 * n]), res[-1]


def _ag_ici_wait(send_sems, recv_sems, ins, lands, after, name):
    n = len(ins)

    def body(*refs):
        ins_r, lands_r = refs[:n], refs[n:2 * n]
        send_r, recv_r = refs[2 * n:2 * n + 2]
        x, y, c = _position()
        for i in range(n):
            for j, (px, py) in enumerate(_other_chips(x, y)):
                cp = pltpu.make_async_remote_copy(src_ref=ins_r[i].at[c], dst_ref=lands_r[i].at[2 * px + py, c], send_sem=send_r.at[3 * i + j],
                                                  recv_sem=recv_r.at[3 * i + j], device_id=(px, py, c), device_id_type=MESH)
                cp.wait_send()
                cp.wait_recv()

    res = pl.pallas_call(
        body, name=name,
        out_shape=[pltpu.HBM(a.shape, a.dtype) for a in list(ins) + list(lands)],
        in_specs=[HBM_SPEC] * (2 * n) + [SEM_SPEC, SEM_SPEC, pl.BlockSpec(memory_space=pl.ANY)], out_specs=[HBM_SPEC] * (2 * n),
        input_output_aliases={i: i for i in range(2 * n)},
        compiler_params=pltpu.CompilerParams(has_side_effects=DATAFLOW_EFFECT),
    )(*ins, *lands, send_sems, recv_sems, after)
    return list(res[:n]), list(res[n:])


def _ag_finish(arrs, lands, name):
    n = len(arrs)
    hbm = pl.BlockSpec(memory_space=pl.ANY)

    def body(*refs):
        ins, landed, outs = refs[:n], refs[n:2 * n], refs[2 * n:3 * n]
        send_sems, recv_sems = refs[3 * n:]
        x, y, c = _position()
        chips = _other_chips(x, y)
        sends = []
        for i in range(n):
            own = pltpu.make_async_remote_copy(src_ref=ins[i], dst_ref=outs[i].at[2 * x + y], send_sem=send_sems.at[i, 3],
                                               recv_sem=recv_sems.at[i, 3], device_id=(x, y, 1 - c), device_id_type=MESH)
            own.start()
            sends.append(own)
        for i in range(n):
            for j, (px, py) in enumerate(chips):
                fwd = pltpu.make_async_remote_copy(src_ref=landed[i].at[2 * px + py, c], dst_ref=outs[i].at[2 * px + py, c],
                                                   send_sem=send_sems.at[i, j], recv_sem=recv_sems.at[i, j],
                                                   device_id=(x, y, 1 - c), device_id_type=MESH)
                fwd.start()
                sends.append(fwd)
        for i in range(n):
            for j, (px, py) in enumerate(chips):
                theirs = outs[i].at[2 * px + py, 1 - c]
                pltpu.make_async_remote_copy(src_ref=theirs, dst_ref=theirs, send_sem=send_sems.at[i, j], recv_sem=recv_sems.at[i, j],
                                             device_id=(x, y, c), device_id_type=MESH).wait_recv()
        for i in range(n):
            pltpu.make_async_remote_copy(src_ref=ins[i], dst_ref=outs[i].at[2 * x + y], send_sem=send_sems.at[i, 3],
                                         recv_sem=recv_sems.at[i, 3], device_id=(x, y, c), device_id_type=MESH).wait_recv()
        for cp in sends:
            cp.wait_send()

    return pl.pallas_call(
        body, name=name, in_specs=[hbm] * (2 * n), out_specs=[hbm] * n,
        out_shape=[SDS(l.shape, l.dtype) for l in lands],
        input_output_aliases={n + i: i for i in range(n)},
        scratch_shapes=[pltpu.SemaphoreType.DMA((n, 4)), pltpu.SemaphoreType.DMA((n, 4))])(*arrs, *lands)


def _swap_halves(gs, name):
    n = len(gs)
    hbm = pl.BlockSpec(memory_space=pl.ANY)

    def body(*refs):
        ins, outs = refs[:n], refs[n:2 * n]
        send_sems, recv_sems = refs[2 * n:]
        x, y, c = _position()
        cps = []
        for i in range(n):
            for p in range(4):
                cp = pltpu.make_async_remote_copy(src_ref=ins[i].at[p, 1 - c], dst_ref=outs[i].at[p], send_sem=send_sems.at[4 * i + p],
                                                  recv_sem=recv_sems.at[4 * i + p], device_id=(x, y, 1 - c), device_id_type=MESH)
                cp.start()
                cps.append(cp)
        for cp in cps:
            cp.wait()

    return pl.pallas_call(
        body, name=name, in_specs=[hbm] * n, out_specs=[hbm] * n,
        out_shape=[SDS((4,) + g.shape[2:], g.dtype) for g in gs],
        scratch_shapes=[pltpu.SemaphoreType.DMA((4 * n,)), pltpu.SemaphoreType.DMA((4 * n,))])(*gs)


def _exchange_start(s1s, name):
    n = len(s1s)

    def body(*refs):
        srcs, lands = refs[:n], refs[n:2 * n]
        send_sems, recv_sems = refs[2 * n:2 * n + 2]
        token = refs[-1]
        x, y, c = _position()
        for i in range(n):
            for j, (px, py) in enumerate(_other_chips(x, y)):
                pltpu.make_async_remote_copy(src_ref=srcs[i].at[2 * px + py], dst_ref=lands[i].at[j], send_sem=send_sems.at[3 * i + j],
                                             recv_sem=recv_sems.at[3 * i + j], device_id=(px, py, c), device_id_type=MESH).start()
        token[...] = jnp.zeros_like(token)

    lands = [lax.empty((3,) + s.shape[1:], F32) for s in s1s]
    res = pl.pallas_call(
        body, name=name,
        out_shape=[pltpu.SemaphoreType.DMA((3 * n,)), pltpu.SemaphoreType.DMA((3 * n,))]
        + [pltpu.HBM(a.shape, F32) for a in list(s1s) + lands] + [SDS((8, 128), F32)],
        in_specs=[HBM_SPEC] * (2 * n), out_specs=[SEM_SPEC, SEM_SPEC] + [HBM_SPEC] * (2 * n) + [pl.BlockSpec(memory_space=pltpu.VMEM)],
        input_output_aliases={i: 2 + i for i in range(2 * n)},
        compiler_params=pltpu.CompilerParams(has_side_effects=DATAFLOW_EFFECT),
    )(*[pltpu.with_memory_space_constraint(a, pltpu.HBM) for a in list(s1s) + lands])
    return res[0], res[1], list(res[2:2 + n]), list(res[2 + n:2 + 2 * n]), res[-1]


def _exchange_wait(send_sems, recv_sems, s1s, lands, after, name):
    n = len(s1s)

    def body(*refs):
        srcs, lands_r = refs[:n], refs[n:2 * n]
        send_r, recv_r = refs[2 * n:2 * n + 2]
        x, y, c = _position()
        for i in range(n):
            for j, (px, py) in enumerate(_other_chips(x, y)):
                cp = pltpu.make_async_remote_copy(src_ref=srcs[i].at[2 * px + py], dst_ref=lands_r[i].at[j], send_sem=send_r.at[3 * i + j],
                                                  recv_sem=recv_r.at[3 * i + j], device_id=(px, py, c), device_id_type=MESH)
                cp.wait_send()
                cp.wait_recv()

    res = pl.pallas_call(
        body, name=name, out_shape=[pltpu.HBM(a.shape, F32) for a in list(s1s) + list(lands)],
        in_specs=[HBM_SPEC] * (2 * n) + [SEM_SPEC, SEM_SPEC, pl.BlockSpec(memory_space=pl.ANY)], out_specs=[HBM_SPEC] * (2 * n),
        input_output_aliases={i: i for i in range(2 * n)},
        compiler_params=pltpu.CompilerParams(has_side_effects=DATAFLOW_EFFECT),
    )(*s1s, *lands, send_sems, recv_sems, after)
    return list(res[:n]), list(res[n:])


def _join_halves(f2s, name):
    n = len(f2s)
    hbm = pl.BlockSpec(memory_space=pl.ANY)

    def body(*refs):
        ins, outs = refs[:n], refs[n:2 * n]
        send_sems, recv_sems = refs[2 * n:]
        x, y, c = _position()
        cps = []
        for i in range(n):
            cp = pltpu.make_async_remote_copy(src_ref=ins[i].at[c], dst_ref=outs[i].at[c], send_sem=send_sems.at[i],
                                              recv_sem=recv_sems.at[i], device_id=(x, y, 1 - c), device_id_type=MESH)
            cp.start()
            cps.append(cp)
        for i in range(n):
            pltpu.make_async_remote_copy(src_ref=ins[i].at[1 - c], dst_ref=outs[i].at[1 - c], send_sem=send_sems.at[i],
                                         recv_sem=recv_sems.at[i], device_id=(x, y, 1 - c), device_id_type=MESH).wait_recv()
        for cp in cps:
            cp.wait_send()

    return pl.pallas_call(
        body, name=name, in_specs=[hbm] * n, out_specs=[hbm] * n, out_shape=[SDS(f.shape, f.dtype) for f in f2s],
        input_output_aliases={i: i for i in range(n)},
        scratch_shapes=[pltpu.SemaphoreType.DMA((n,)), pltpu.SemaphoreType.DMA((n,))])(*f2s)


def _sum_tile(rows, cols):
    return _row_tile(rows, max(8, (2 ** 19 // cols) // 8 * 8))


def _add_pair(g, r1, c, name):
    _, R, C = r1.shape
    tr = _sum_tile(R, C)

    def body(c_ref, a_ref, b_ref, o_ref):
        del c_ref
        o_ref[...] = a_ref[...] + b_ref[...]

    blk = pl.BlockSpec((None, tr, C), lambda p, i, cr: (p, i, 0))
    return pl.pallas_call(
        body, name=name,
        grid_spec=pltpu.PrefetchScalarGridSpec(
            num_scalar_prefetch=1, grid=(4, R // tr),
            in_specs=[pl.BlockSpec((None, None, tr, C), lambda p, i, cr: (p, cr[0], i, 0)), blk], out_specs=blk),
        out_shape=SDS((4, R, C), F32))(c, g, r1)


def _add_four(s1, r2, me_c, name):
    _, R, C = s1.shape
    tr = _sum_tile(R, C)

    def body(m_ref, a_ref, b_ref, o_ref):
        del m_ref
        o_ref[...] = ((a_ref[...] + b_ref[0]) + b_ref[1]) + b_ref[2]

    return pl.pallas_call(
        body, name=name,
        grid_spec=pltpu.PrefetchScalarGridSpec(
            num_scalar_prefetch=1, grid=(R // tr,),
            in_specs=[pl.BlockSpec((None, tr, C), lambda i, mr: (mr[0], i, 0)), pl.BlockSpec((3, tr, C), lambda i, mr: (0, i, 0))],
            out_specs=pl.BlockSpec((None, tr, C), lambda i, mr: (mr[1], i, 0))),
        out_shape=SDS((2, R, C), F32))(me_c, s1, r2)


def _all_reduce_small(vs):
    n = len(vs)

    def body(*refs):
        ins, outs, bufs = refs[:n], refs[n:2 * n], refs[2 * n:3 * n]
        send_sems, recv_sems = refs[3 * n:]
        x, y, c = _position()
        me = 4 * x + 2 * y + c
        for i in range(n):
            bufs[i][me] = ins[i][...]
        cps = []
        for k in range(1, 8):
            to = (1 - x if k & 4 else x, 1 - y if k & 2 else y, 1 - c if k & 1 else c)
            for i in range(n):
                cp = pltpu.make_async_remote_copy(src_ref=bufs[i].at[me], dst_ref=bufs[i].at[me], send_sem=send_sems.at[7 * i + k - 1],
                                                  recv_sem=recv_sems.at[7 * i + k - 1], device_id=to, device_id_type=MESH)
                cp.start()
                cps.append(cp)
        for cp in cps:
            cp.wait_send()
        for k in range(1, 8):
            src = 4 * (1 - x if k & 4 else x) + 2 * (1 - y if k & 2 else y) + (1 - c if k & 1 else c)
            for i in range(n):
                pltpu.make_async_remote_copy(src_ref=bufs[i].at[src], dst_ref=bufs[i].at[src], send_sem=send_sems.at[7 * i + k - 1],
                                             recv_sem=recv_sems.at[7 * i + k - 1], device_id=(x, y, c), device_id_type=MESH).wait_recv()
        for i in range(n):
            acc = bufs[i][0]
            for k in range(1, 8):
                acc = acc + bufs[i][k]
            outs[i][...] = acc

    vm = pl.BlockSpec(memory_space=pltpu.VMEM)
    return pl.pallas_call(
        body, name="all_reduce_small", in_specs=[vm] * n, out_specs=[vm] * n, out_shape=[SDS(v.shape, F32) for v in vs],
        scratch_shapes=[pltpu.VMEM((8,) + v.shape, F32) for v in vs]
        + [pltpu.SemaphoreType.DMA((7 * n,)), pltpu.SemaphoreType.DMA((7 * n,))])(*vs)


def _reduce_begin(grads, tag):
    _, _, c = _position()
    cs = jnp.reshape(c, (1,)).astype(jnp.int32)
    g4 = [g.reshape(4, 2, g.shape[1] // 2, g.shape[2]) for g in grads]
    r1 = _swap_halves(g4, "rs_swap_" + tag)
    s1 = [_add_pair(g, r, cs, f"rs_add_pair_{tag}{i}") for i, (g, r) in enumerate(zip(g4, r1))]
    send_sems, recv_sems, s1, lands, token = _exchange_start(s1, "rs_exchange_start_" + tag)
    return (send_sems, recv_sems, s1, lands), token


def _reduce_end(state, after, tag):
    x, y, c = _position()
    send_sems, recv_sems, s1, lands = state
    s1, lands = _exchange_wait(send_sems, recv_sems, s1, lands, after, "rs_exchange_wait_" + tag)
    me_c = jnp.stack([2 * x + y, c]).astype(jnp.int32)
    f2 = [_add_four(s, l, me_c, f"rs_add_four_{tag}{i}") for i, (s, l) in enumerate(zip(s1, lands))]
    return [f.reshape(2 * f.shape[1], f.shape[2]) for f in _join_halves(f2, "rs_join_" + tag)]


def _halves(a):
    return a.reshape((2, a.shape[0] // 2) + a.shape[1:])


def _after(a, token):
    return a + token[0, 0]


def _in_proj_own(x, nw, w, part):
    M, K = x.shape
    n = w.shape[2]
    sub = 256

    def body(p_ref, x_ref, nw_ref, w_ref, xn_ref, z_ref):
        del p_ref
        for r0 in range(0, MM_TM, sub):
            rows = pl.ds(r0, sub)
            for c0 in range(r0, r0 + sub, NORM_RC):
                ch = pl.ds(c0, NORM_RC)
                xv = x_ref[ch, :]
                xn_ref[ch, :] = (xv * lax.rsqrt(jnp.mean(xv * xv, axis=-1, keepdims=True) + RMS_EPS) * nw_ref[...]).astype(BF16)
            z_ref[rows, :] = jnp.dot(xn_ref[rows, :], w_ref[...], preferred_element_type=F32).astype(BF16)

    row = pl.BlockSpec((MM_TM, K), lambda i, pr: (i, 0))
    return pl.pallas_call(
        body, name="in_proj_own",
        grid_spec=pltpu.PrefetchScalarGridSpec(
            num_scalar_prefetch=1, grid=(M // MM_TM,),
            in_specs=[row, pl.BlockSpec((1, K), lambda i, pr: (0, 0)), pl.BlockSpec((None, K, n), lambda i, pr: (0, 0, 0))],
            out_specs=[row, pl.BlockSpec((MM_TM, n), lambda i, pr: (i, pr[0]))]),
        out_shape=[SDS((M, K), BF16), SDS((M, 4 * n), BF16)])(part, x, nw, w)


def _in_proj_parts(xn, w, parts, z, name):
    M, K = xn.shape
    _, _, n = w.shape
    P = parts.shape[0]

    def body(p_ref, a_ref, b_ref, z_in, o_ref):
        del p_ref, z_in
        o_ref[...] = jnp.dot(a_ref[...], b_ref[...], preferred_element_type=F32).astype(BF16)

    return pl.pallas_call(
        body, name=name,
        grid_spec=pltpu.PrefetchScalarGridSpec(
            num_scalar_prefetch=1, grid=(P, M // MM_TM),
            in_specs=[pl.BlockSpec((MM_TM, K), lambda g, i, pr: (i, 0)), pl.BlockSpec((None, K, n), lambda g, i, pr: (pr[g], 0, 0)),
                      pl.BlockSpec(memory_space=pl.ANY)],
            out_specs=pl.BlockSpec((MM_TM, n), lambda g, i, pr: (i, pr[g]))),
        out_shape=SDS(z.shape, BF16), input_output_aliases={3: 0})(parts, xn, w, z)


def _local_step(x, mem, target, sp, ex):
    S, D = x.shape
    band, buckets = _bias_static()
    buckets = jnp.asarray(buckets)

    tok = ex.start_first()
    own, me, others = ex.own_w_in()
    xn, z = _in_proj_own(x, _after(sp["attn_norm_w"], tok), own, me)
    w_in, gathered_small = ex.first_weights(after=z)
    sp = {**sp, **gathered_small}
    n_in = 4 * w_in.shape[2]
    bw = {"w_in": w_in}
    z = _in_proj_parts(xn, w_in, others + ex.start_rest()[0, 0].astype(jnp.int32), z, "in_proj_rest")
    qh, kh, vh = _qkv_prep(z, sp["q_norm_w"], sp["k_norm_w"])
    tab = sp["rel_bias_table"].T.reshape(N_GROUPS, HPG, N_BUCKETS)
    bias = _bias_fwd(jnp.pad(tab, ((0, 0), (0, 8 - HPG), (0, 0))), buckets)
    biasm = jnp.where(jnp.asarray(band)[None, None], bias[:, :HPG].reshape(N_GROUPS, HPG, NQ, 2 * NQ), NEG)
    os_, lses = [], []
    for g, (_, dil) in enumerate(ATTN_GROUPS):
        o_g, lse_g = _attn_fwd(qh, kh, vh, biasm[g], g, dil)
        os_.append(o_g)
        lses.append(lse_g)
    attn = _merge_fwd(os_, lses)
    u1, u3 = _conv_fwd(z, sp["conv_dw_w"], sp["conv_dw_b"], sp["conv_ln_w"], sp["conv_ln_b"])
    bw.update(ex.rest_weights(after=attn))
    F2 = 4 * bw["w_up"].shape[2]
    mn, kv, mk, mv = _memkv_fwd(mem, sp["mem_norm_w"], bw["w_mem_kv"], sp["xk_norm_w"])
    oc = _cross_fwd(z, sp["xq_norm_w"], mk, mv)
    h1, hn = _outproj_fwd(x, z, sp["b_gate"], attn, u3, oc, bw["w_attn_o"], bw["w_conv_o"], bw["w_cross_o"], bw["w_out"],
                          sp["ffn_norm_w"])
    up = _mm_nn(hn, bw["w_up"], BF16, "ffn_up")
    dy, loss_tile, ac, gc = _ffn_fwd(up, sp["ffn_conv_w"], sp["ffn_conv_b"], bw["w_down"], h1, target)

    gs, gb = {}, {}
    dca, dcg, acca, accg, gb["w_down"] = _ffn_bwd_a(dy, bw["w_down"], ac, gc)
    cols = lambda acc: jnp.sum(acc, axis=1).reshape(1, F2 // 2)
    gs["ffn_conv_b"] = jnp.concatenate([cols(acca), cols(accg)], axis=1)
    dup, tacc = _ffn_bwd_b(dca, dcg, up, sp["ffn_conv_w"])
    gs["ffn_conv_w"] = jnp.transpose(jnp.sum(tacc, axis=2), (1, 0, 2)).reshape(FFN_K, F2)
    gb["w_up"] = _mm_tn(hn, dup, 4, "dw_up")
    tok = ex.reduce_begin("a", ("w_down", "w_up"), gb)
    dh1, gs["ffn_norm_w"] = _norm_in_bwd(dup, bw["w_up"], h1, _after(sp["ffn_norm_w"], tok), dy, "ffn_in_bwd")
    dz, dattn, du3, doc, gs["b_gate"], gb["w_out"], gb["w_attn_o"], gb["w_conv_o"], gb["w_cross_o"] = _outproj_bwd(
        dh1, z, sp["b_gate"], attn, u3, oc, bw["w_attn_o"], bw["w_conv_o"], bw["w_cross_o"], bw["w_out"], n_in)
    dz, dmk, dmv, gs["xq_norm_w"] = _cross_bwd(dz, doc, z, sp["xq_norm_w"], mk, mv)
    gb["w_mem_kv"], gs["xk_norm_w"], gs["mem_norm_w"] = _memkv_bwd(
        dmk, dmv, kv, mem, mn, sp["mem_norm_w"], bw["w_mem_kv"], sp["xk_norm_w"])
    ex.reduce_end("a", after=gs["mem_norm_w"])
    tok = ex.reduce_begin("b", ("w_out", "w_attn_o", "w_conv_o", "w_cross_o", "w_mem_kv"), gb)
    du1, cacc = _conv_bwd_a(du3, u1, z, _after(sp["conv_ln_w"], tok), sp["conv_ln_b"])
    gs["conv_dw_w"], gs["conv_dw_b"] = cacc[:CONV_K], cacc[32:33]
    gs["conv_ln_w"], gs["conv_ln_b"] = cacc[33:34], cacc[34:35]
    dz = _conv_bwd_b(dz, du1, z, sp["conv_dw_w"])
    wg, dah, dhb = _merge_bwd(dattn, os_, lses)
    dqs, dks, dvs, dsbs = [], [], [], []
    for g, (_, dil) in enumerate(ATTN_GROUPS):
        dq_g, dk_g, dv_g, dsb_g = _attn_bwd(qh, kh, vh, biasm[g], dah, wg[g], dhb, lses[g], g, dil)
        dqs.append(dq_g)
        dks.append(dk_g)
        dvs.append(dv_g)
        dsbs.append(dsb_g.reshape(HPG, NQ * 2 * NQ))
    dtab = _bias_bwd(jnp.pad(jnp.stack(dsbs), ((0, 0), (0, 8 - HPG), (0, 0))), buckets)
    gs["rel_bias_table"] = dtab[:, :HPG].reshape(N_GROUPS * HPG, N_BUCKETS).T
    dz, gs["q_norm_w"], gs["k_norm_w"] = _qkv_bwd(dz, z, dqs, dks, dvs, sp["q_norm_w"], sp["k_norm_w"])
    ex.reduce_end("b", after=gs["q_norm_w"])
    gb["w_in"] = _mm_tn(xn, dz, 4, "dw_in")
    tok = ex.reduce_begin("c", ("w_in",), gb)
    dx, gs["attn_norm_w"] = _norm_in_bwd(dz, bw["w_in"], x, _after(sp["attn_norm_w"], tok), dh1, "in_bwd")
    ex.reduce_end("c", after=gs["attn_norm_w"])
    return loss_tile, dx, gs, gb


SMALL = ("rel_bias_table", "attn_norm_w", "b_gate", "q_norm_w", "k_norm_w", "conv_dw_w", "conv_dw_b", "conv_ln_w", "conv_ln_b",
         "mem_norm_w", "xq_norm_w", "xk_norm_w", "ffn_norm_w", "ffn_conv_w", "ffn_conv_b")
SMALL_SHARDED = ("conv_dw_w", "ffn_conv_w")
BIG_COL = ("w_in", "w_attn_o", "w_conv_o", "w_cross_o", "w_up")
BIG_ROW = ("w_mem_kv", "w_out", "w_down")
BIG = BIG_COL + BIG_ROW
WEIGHTS = ("rel_bias_table", "attn_norm_w", "w_in", "b_gate", "q_norm_w", "k_norm_w", "w_attn_o", "conv_dw_w", "conv_dw_b",
           "conv_ln_w", "conv_ln_b", "w_conv_o", "mem_norm_w", "w_mem_kv", "xq_norm_w", "xk_norm_w", "w_cross_o", "w_out",
           "ffn_norm_w", "w_up", "ffn_conv_w", "ffn_conv_b", "w_down")


class _Exchanges:
    REST = tuple(k for k in BIG if k != "w_in")

    def __init__(self, w):
        self.w = w
        self.pending = {}
        self.reduced = {}

    def _whole(self, k, ga):
        ga = ga.reshape((4,) + self.w[k].shape)
        return ga if k in BIG_COL else ga.reshape((4 * self.w[k].shape[0],) + self.w[k].shape[1:])

    def start_first(self):
        self.w_in_local = self.w["w_in"].astype(BF16)
        local = [_halves(self.w_in_local)]
        for k in SMALL_SHARDED:
            flat = jnp.ravel(self.w[k])
            local.append(jnp.pad(flat, (0, (-flat.shape[0]) % 2048)).reshape(2, -1, 128))
        send_sems, recv_sems, ins, lands, token = _ag_ici_start(local, "gather_first_start")
        self.pending["first"] = (send_sems, recv_sems, ins, lands)
        return token

    def own_w_in(self):
        x, y, _ = _position()
        me = 2 * x + y
        others = jnp.stack([me ^ 1, me ^ 2, me ^ 3]).astype(jnp.int32)
        return self.w_in_local[None], jnp.reshape(me, (1,)).astype(jnp.int32), others

    def first_weights(self, after):
        send_sems, recv_sems, ins, lands = self.pending.pop("first")
        ins, lands = _ag_ici_wait(send_sems, recv_sems, ins, lands, after, "gather_first_wait")
        gathered = _ag_finish(ins, lands, "gather_first_finish")
        self.first = gathered[0]
        small = {}
        for k, ga in zip(SMALL_SHARDED, gathered[1:]):
            r, cdim = self.w[k].shape
            parts = ga.reshape(4, -1)[:, :r * cdim].reshape(4, r, cdim)
            small[k] = jnp.transpose(parts, (1, 0, 2)).reshape(r, 4 * cdim)
        return self._whole("w_in", gathered[0]), small

    def start_rest(self):
        local = [_halves(self.w[k].astype(BF16)) for k in self.REST]
        local, _ = lax.optimization_barrier((local, self.first))
        send_sems, recv_sems, ins, lands, token = _ag_ici_start(local, "gather_rest_start")
        self.pending["rest"] = (send_sems, recv_sems, ins, lands)
        return token

    def rest_weights(self, after):
        send_sems, recv_sems, ins, lands = self.pending.pop("rest")
        ins, lands = _ag_ici_wait(send_sems, recv_sems, ins, lands, after, "gather_rest_wait")
        gathered = _ag_finish(ins, lands, "gather_rest_finish")
        return {k: self._whole(k, ga) for k, ga in zip(self.REST, gathered)}

    def reduce_begin(self, tag, names, gb):
        parts = [gb[k].reshape((4,) + self.w[k].shape) for k in names]
        state, token = _reduce_begin(parts, tag)
        self.pending[tag] = (state, names)
        return token

    def reduce_end(self, tag, after):
        state, names = self.pending.pop(tag)
        self.reduced.update(zip(names, _reduce_end(state, after, tag)))


def _step(x, mem, target, w, m, v):
    xi, yi, _ = _position()
    shard = 2 * xi + yi
    ex = _Exchanges(w)
    sp = {k: w[k] for k in SMALL if k not in SMALL_SHARDED}
    loss_tile, dx, gs, _ = _local_step(x, mem, target, sp, ex)
    g_big = ex.reduced

    red = _all_reduce_small([loss_tile] + [gs[k] for k in SMALL])
    loss = red[0][0, 0]
    g_small = dict(zip(SMALL, red[1:]))
    for k in SMALL_SHARDED:
        cdim = w[k].shape[1]
        g_small[k] = lax.dynamic_slice_in_dim(g_small[k], shard * cdim, cdim, axis=1)

    grads, delta, new_m, new_v = {}, {}, {}, {}
    for k in BIG:
        grads[k] = g_big[k]
        delta[k], new_m[k], new_v[k] = _adamw(w[k], g_big[k], m[k], v[k], "adamw_" + k)
    outs = _adamw_small([w[k] for k in SMALL], [g_small[k] for k in SMALL], [m[k] for k in SMALL], [v[k] for k in SMALL])
    for dst, vals in zip((delta, new_m, new_v), outs):
        dst.update(zip(SMALL, vals))
    grads.update(g_small)
    return loss, dx, grads, delta, new_m, new_v


def kernel(x, mem, rel_bias_table, attn_norm_w, w_in, b_gate, q_norm_w, k_norm_w, w_attn_o, conv_dw_w, conv_dw_b, conv_ln_w, conv_ln_b, w_conv_o, mem_norm_w, w_mem_kv, xq_norm_w, xk_norm_w, w_cross_o, w_out, ffn_norm_w, w_up, ffn_conv_w, ffn_conv_b, w_down, loss_target, m_rel_bias_table, m_attn_norm_w, m_w_in, m_b_gate, m_q_norm_w, m_k_norm_w, m_w_attn_o, m_conv_dw_w, m_conv_dw_b, m_conv_ln_w, m_conv_ln_b, m_w_conv_o, m_mem_norm_w, m_w_mem_kv, m_xq_norm_w, m_xk_norm_w, m_w_cross_o, m_w_out, m_ffn_norm_w, m_w_up, m_ffn_conv_w, m_ffn_conv_b, m_w_down, v_rel_bias_table, v_attn_norm_w, v_w_in, v_b_gate, v_q_norm_w, v_k_norm_w, v_w_attn_o, v_conv_dw_w, v_conv_dw_b, v_conv_ln_w, v_conv_ln_b, v_w_conv_o, v_mem_norm_w, v_w_mem_kv, v_xq_norm_w, v_xk_norm_w, v_w_cross_o, v_w_out, v_ffn_norm_w, v_w_up, v_ffn_conv_w, v_ffn_conv_b, v_w_down):
    args = locals()
    def block(name, k):
        a = args[name] if k == "rel_bias_table" else args[name][0]
        return a.reshape(1, -1) if a.ndim == 1 else a

    w = {k: block(k, k) for k in WEIGHTS}
    m = {k: block("m_" + k, k) for k in WEIGHTS}
    v = {k: block("v_" + k, k) for k in WEIGHTS}
    loss, dx, grads, delta, new_m, new_v = _step(x[0], mem[0], loss_target[0], w, m, v)
    out = [loss, dx[None]]
    for d in (grads, delta, new_m, new_v):
        for k in WEIGHTS:
            out.append(d[k].reshape(args[k].shape))
    return tuple(out)
```

```python
import functools
import math

import numpy as np
import jax
import jax.numpy as jnp
from jax import lax
from jax.experimental import pallas as pl
from jax.experimental.pallas import tpu as pltpu

F32, BF16 = jnp.float32, jnp.bfloat16
SDS = jax.ShapeDtypeStruct
MESH = pl.DeviceIdType.MESH

HEAD = 128
N_GROUPS, HPG = 3, 4
ATTN_GROUPS = ((128, 1), (512, 4), (2048, 16))
NQ = 128
QKV_W = N_GROUPS * HPG * HEAD
CH = 512
CONV_K, FFN_K = 31, 3
N_BUCKETS, MAX_DIST = 32, 2048
RMS_EPS, LN_EPS = 1e-6, 1e-5
O_Q, O_K, O_V, O_CV, O_CG, O_XQ, O_G = 0, QKV_W, 2 * QKV_W, 3 * QKV_W, 3 * QKV_W + CH, 3 * QKV_W + 2 * CH, 3 * QKV_W + 3 * CH
ADAM_LR, ADAM_B1, ADAM_B2, ADAM_EPS, ADAM_WD, ADAM_STEP = 0.001, 0.9, 0.999, 1e-08, 0.01, 10
NEG = -1e30
SCALE = HEAD ** -0.5
TM = 512
MM_TM = 1024
ATT_RB = 2048
NT_DIMS = (((1,), (1,)), ((), ()))
TN_DIMS = (((0,), (0,)), ((), ()))


CONV_RC = 32
CONV_LB = 256
CONV_LANES = tuple(slice(l, l + CONV_LB) for l in range(0, CH, CONV_LB))
CONV_HALO = 32
FFN_RC = 32


def _conv_taps(buf, base, lanes, q_lo, q_hi, visit):
    span = CONV_RC + CONV_HALO
    xx = buf[pl.ds(base, span), lanes]
    for s in range(8):
        xs = xx if s == 0 else pltpu.roll(xx, span - s, 0)
        for q in range(s, q_hi + 1, 8):
            if q >= q_lo:
                visit(q, xs[q - s:q - s + CONV_RC])


def _sig(v):
    return 0.5 * jnp.tanh(0.5 * v) + 0.5


def _fold8(v):
    acc = v[0:8]
    for r in range(8, v.shape[0], 8):
        acc = acc + v[r:r + 8]
    return acc


def _row_tile(rows, cap, mult=8):
    best = None
    for t in range(mult, min(rows, cap) + 1, mult):
        if rows % t == 0:
            best = t
    return best if best is not None else rows


def _full(shape):
    n = len(shape)
    return pl.BlockSpec(shape, lambda *a: (0,) * n)


def _mm_nn(a, b, out_dtype, name):
    M, K = a.shape
    G, _, n = b.shape

    def body(a_ref, b_ref, o_ref):
        o_ref[...] = jnp.dot(a_ref[...].astype(BF16), b_ref[...], preferred_element_type=F32).astype(out_dtype)

    return pl.pallas_call(
        body, name=name, grid=(G, M // MM_TM),
        in_specs=[pl.BlockSpec((MM_TM, K), lambda g, i: (i, 0)), pl.BlockSpec((None, K, n), lambda g, i: (g, 0, 0))],
        out_specs=pl.BlockSpec((MM_TM, n), lambda g, i: (i, g)),
        out_shape=SDS((M, G * n), out_dtype))(a, b)


def _mm_tn(a, b, G, name):
    S, Ka = a.shape
    n = b.shape[1] // G
    tka = Ka
    while tka * n * 4 > 10 * 2 ** 20 and tka % 256 == 0:
        tka //= 2

    def body(a_ref, b_ref, o_ref):
        @pl.when(pl.program_id(2) == 0)
        def _():
            o_ref[...] = jnp.zeros_like(o_ref)
        o_ref[...] += lax.dot_general(a_ref[...].astype(BF16), b_ref[...].astype(BF16), TN_DIMS, preferred_element_type=F32)

    return pl.pallas_call(
        body, name=name, grid=(G, Ka // tka, S // MM_TM),
        in_specs=[pl.BlockSpec((MM_TM, tka), lambda g, i, k: (k, i)), pl.BlockSpec((MM_TM, n), lambda g, i, k: (k, g))],
        out_specs=pl.BlockSpec((None, tka, n), lambda g, i, k: (g, i, 0)),
        out_shape=SDS((G, Ka, n), F32))(a, b)


NORM_RC = 16


def _norm_in_bwd(a, w, xin, nw, resid, name):
    S, K = xin.shape
    G, _, n = w.shape
    tm = 1024 if S % 1024 == 0 and G * K * n * 2 <= 12 * 2 ** 20 else TM
    nt = S // tm

    def body(a_ref, w_ref, x_ref, nw_ref, r_ref, o_ref, dnw_ref, acc, part):
        i, g = pl.program_id(0), pl.program_id(1)
        cur = lax.rem(i, 2)

        def product():
            return lax.dot_general(a_ref[...], w_ref[g], NT_DIMS, preferred_element_type=F32)

        def epilogue():
            for r0 in range(0, tm, NORM_RC):
                rows = pl.ds(r0, NORM_RC)
                dn = acc[1 - cur, rows, :]
                xv = x_ref[rows, :]
                r = lax.rsqrt(jnp.mean(xv * xv, axis=-1, keepdims=True) + RMS_EPS)
                xhat = xv * r
                dyw = dn * nw_ref[...]
                o_ref[rows, :] = r_ref[rows, :] + r * (dyw - xhat * jnp.mean(dyw * xhat, axis=-1, keepdims=True))
                part[...] += _fold8(dn * xhat)

        @pl.when((g == 0) & (i == 0))
        def _():
            part[...] = jnp.zeros_like(part)
            acc[cur] = product()

        @pl.when((g == 0) & (i > 0) & (i < nt))
        def _():
            acc[cur] = product()
            epilogue()

        @pl.when((g == 0) & (i == nt))
        def _():
            epilogue()
            dnw_ref[...] = jnp.sum(part[...], axis=0, keepdims=True)

        @pl.when((g > 0) & (i < nt))
        def _():
            acc[cur] += product()

    lag = lambda i, g: (jnp.maximum(i - 1, 0), 0)
    return pl.pallas_call(
        body, name=name, grid=(nt + 1, G),
        in_specs=[pl.BlockSpec((tm, n), lambda i, g: (jnp.minimum(i, nt - 1), g)),
                  pl.BlockSpec((G, K, n), lambda i, g: (0, 0, 0), pipeline_mode=pl.Buffered(1)),
                  pl.BlockSpec((tm, K), lag), _full((1, K)), pl.BlockSpec((tm, K), lag)],
        out_specs=[pl.BlockSpec((tm, K), lag), _full((1, K))],
        out_shape=[SDS((S, K), F32), SDS((1, K), F32)],
        scratch_shapes=[pltpu.VMEM((2, tm, K), F32), pltpu.VMEM((8, K), F32)])(a, w, xin, nw, resid)


def _t5_bucket_np(dist):
    max_exact = N_BUCKETS // 2
    d = np.maximum(dist.astype(np.float32), np.float32(1.0))
    large = max_exact + (np.log(d / np.float32(max_exact)) / np.float32(math.log(MAX_DIST / max_exact))
                         * np.float32(N_BUCKETS - max_exact)).astype(np.int32)
    large = np.minimum(large, N_BUCKETS - 1)
    return np.where(dist < max_exact, dist, large).astype(np.int32)


def _bias_static():
    qi = np.arange(NQ)[:, None]
    kj = np.arange(2 * NQ)[None, :]
    step = qi + NQ - kj
    band = (step >= 0) & (step <= NQ)
    buckets = np.stack([_t5_bucket_np(np.clip(step, 0, None) * dil).reshape(1, -1) for _, dil in ATTN_GROUPS])
    return band, buckets


def _bias_fwd(table_t, buckets):
    nb = buckets.shape[-1]

    def body(t_ref, b_ref, o_ref):
        oh = (b_ref[...] == lax.broadcasted_iota(jnp.int32, (N_BUCKETS, nb), 0)).astype(F32)
        o_ref[...] = jnp.dot(t_ref[...], oh, preferred_element_type=F32, precision=lax.Precision.HIGHEST)

    return pl.pallas_call(
        body, name="bias_fwd", grid=(N_GROUPS,),
        in_specs=[pl.BlockSpec((None, 8, N_BUCKETS), lambda g: (g, 0, 0)), pl.BlockSpec((None, 1, nb), lambda g: (g, 0, 0))],
        out_specs=pl.BlockSpec((None, 8, nb), lambda g: (g, 0, 0)),
        out_shape=SDS((N_GROUPS, 8, nb), F32))(table_t, buckets)


def _bias_bwd(dsb, buckets):
    nb = buckets.shape[-1]

    def body(d_ref, b_ref, o_ref):
        oh = (b_ref[...] == lax.broadcasted_iota(jnp.int32, (N_BUCKETS, nb), 0)).astype(F32)
        o_ref[...] = lax.dot_general(d_ref[...], oh, NT_DIMS, preferred_element_type=F32, precision=lax.Precision.HIGHEST)

    return pl.pallas_call(
        body, name="bias_bwd", grid=(N_GROUPS,),
        in_specs=[pl.BlockSpec((None, 8, nb), lambda g: (g, 0, 0)), pl.BlockSpec((None, 1, nb), lambda g: (g, 0, 0))],
        out_specs=pl.BlockSpec((None, 8, N_BUCKETS), lambda g: (g, 0, 0)),
        out_shape=SDS((N_GROUPS, 8, N_BUCKETS), F32))(dsb, buckets)


P4 = 4


def _to_p4(dst_ref, h, val, scr):
    scr[...] = val
    for r in range(P4):
        dst_ref[h, r] = scr[pl.ds(r, TM // P4, stride=P4), :]


def _from_p4(src_ref, h, scr):
    for r in range(P4):
        scr[pl.ds(r, TM // P4, stride=P4), :] = src_ref[h, r]
    return scr[...]


def _qkv_prep(z, qw, kw):
    S = z.shape[0]
    nh = N_GROUPS * HPG
    n01 = 2 * HPG

    def body(zq, zk, zv, qw_ref, kw_ref, qh, kh, vh, q2, k2, v2, scr):
        for h in range(nh):
            g = h // HPG
            sl = slice(h * HEAD, (h + 1) * HEAD)
            xq = zq[:, sl].astype(F32)
            q = xq * lax.rsqrt(jnp.mean(xq * xq, axis=-1, keepdims=True) + RMS_EPS) * qw_ref[g:g + 1, :]
            xk = zk[:, sl].astype(F32)
            k = xk * lax.rsqrt(jnp.mean(xk * xk, axis=-1, keepdims=True) + RMS_EPS) * kw_ref[g:g + 1, :]
            v = zv[:, sl].astype(F32)
            if h < n01:
                qh[h], kh[h], vh[h] = q, k, v
            else:
                _to_p4(q2, h - n01, q, scr)
                _to_p4(k2, h - n01, k, scr)
                _to_p4(v2, h - n01, v, scr)

    hm = pl.BlockSpec((n01, TM, HEAD), lambda i: (0, i, 0))
    p4 = pl.BlockSpec((HPG, P4, TM // P4, HEAD), lambda i: (0, 0, i, 0))
    return pl.pallas_call(
        body, name="qkv_prep", grid=(S // TM,),
        in_specs=[pl.BlockSpec((TM, QKV_W), lambda i: (i, 0)), pl.BlockSpec((TM, QKV_W), lambda i: (i, 1)),
                  pl.BlockSpec((TM, QKV_W), lambda i: (i, 2)), _full((N_GROUPS, HEAD)), _full((N_GROUPS, HEAD))],
        out_specs=[hm, hm, hm, p4, p4, p4],
        out_shape=[SDS((n01, S, HEAD), F32)] * 3 + [SDS((HPG, P4, S // P4, HEAD), F32)] * 3,
        scratch_shapes=[pltpu.VMEM((TM, HEAD), F32)])(z, z, z, qw, kw)


def _rows(start, d):
    return pl.ds(start, NQ) if d == 1 else pl.ds(start, NQ, stride=d)


def _attn_fwd(qh, kh, vh, biasm, name, d, head0, nseq, bias_shift):
    S = qh.shape[1]
    RB = min(ATT_RB, S)
    nbk, nq = S // RB, RB // (NQ * d)

    def body(q_ref, k_ref, v_ref, bias_ref, o_ref, lse_ref, kbuf, vbuf):
        b = pl.program_id(1)

        @pl.when(b == 0)
        def _():
            kbuf[0:RB, :] = jnp.zeros((RB, HEAD), F32)
            vbuf[0:RB, :] = jnp.zeros((RB, HEAD), F32)

        @pl.when(b > 0)
        def _():
            kbuf[0:RB, :] = kbuf[RB:2 * RB, :]
            vbuf[0:RB, :] = vbuf[RB:2 * RB, :]

        kbuf[RB:2 * RB, :] = k_ref[...]
        vbuf[RB:2 * RB, :] = v_ref[...]
        bias = bias_ref[...]
        col = lax.broadcasted_iota(jnp.int32, (NQ, 2 * NQ), 1)

        for qb in range(nq):
            def unit(r, carry, qb=qb):
                qs = qb * NQ * d + r
                q = q_ref[_rows(qs, d), :].astype(BF16)
                kw = jnp.concatenate([kbuf[_rows(RB + qs - NQ * d, d), :], kbuf[_rows(RB + qs, d), :]], axis=0).astype(BF16)
                vw = jnp.concatenate([vbuf[_rows(RB + qs - NQ * d, d), :], vbuf[_rows(RB + qs, d), :]], axis=0).astype(BF16)
                s = lax.dot_general(q, kw, NT_DIMS, preferred_element_type=F32) * SCALE + bias
                if qb == 0:
                    s = jnp.where((col < NQ) & (b == 0), NEG, s)
                m = jnp.max(s, axis=-1, keepdims=True)
                p = jnp.exp(s - m)
                l = jnp.sum(p, axis=-1, keepdims=True)
                o = jnp.dot(p.astype(BF16), vw, preferred_element_type=F32) / l
                o_ref[_rows(qs, d), :] = o
                lse_ref[_rows(qs, d), :] = jnp.broadcast_to(m + jnp.log(l), (NQ, HEAD))
                return carry

            for r in range(d):
                unit(r, 0)

    blk = lambda f: pl.BlockSpec((None, RB, HEAD), f)
    return pl.pallas_call(
        body, name=name, grid=(nseq, nbk),
        in_specs=[blk(lambda h, b: (head0 + h, b, 0))] * 3
        + [pl.BlockSpec((None, NQ, 2 * NQ), lambda h, b: (jnp.right_shift(h, bias_shift), 0, 0))],
        out_specs=[blk(lambda h, b: (h, b, 0))] * 2,
        out_shape=[SDS((nseq, S, HEAD), F32)] * 2,
        scratch_shapes=[pltpu.VMEM((2 * RB, HEAD), F32)] * 2)(qh, kh, vh, biasm)


def _attn_bwd(qh, kh, vh, biasm, da, wg, dh, lse, name, d, head0, nseq, bias_shift):
    S = qh.shape[1]
    RB = min(ATT_RB, S)
    nbk, nq = S // RB, RB // (NQ * d)

    def body(q_ref, k_ref, v_ref, bias_ref, da_ref, wg_ref, dh_ref, lse_ref,
             dq_ref, dk_ref, dv_ref, dsb_ref, kbuf, vbuf, dkbuf, dvbuf):
        b = pl.program_id(1)
        zero = jnp.zeros((RB, HEAD), F32)

        @pl.when(b == 0)
        def _():
            kbuf[0:RB, :] = zero
            vbuf[0:RB, :] = zero
            dkbuf[0:RB, :] = zero
            dvbuf[0:RB, :] = zero
            dsb_ref[...] = jnp.zeros_like(dsb_ref)

        @pl.when(b > 0)
        def _():
            kbuf[0:RB, :] = kbuf[RB:2 * RB, :]
            vbuf[0:RB, :] = vbuf[RB:2 * RB, :]
            dkbuf[0:RB, :] = dkbuf[RB:2 * RB, :]
            dvbuf[0:RB, :] = dvbuf[RB:2 * RB, :]

        dkbuf[RB:2 * RB, :] = zero
        dvbuf[RB:2 * RB, :] = zero

        @pl.when(b < nbk)
        def _():
            kbuf[RB:2 * RB, :] = k_ref[...]
            vbuf[RB:2 * RB, :] = v_ref[...]
            bias = bias_ref[...]
            col = lax.broadcasted_iota(jnp.int32, (NQ, 2 * NQ), 1)

            for qb in range(nq):
                def unit(r, carry, qb=qb):
                    qs = qb * NQ * d + r
                    prev, cur = _rows(RB + qs - NQ * d, d), _rows(RB + qs, d)
                    q = q_ref[_rows(qs, d), :].astype(BF16)
                    kw = jnp.concatenate([kbuf[prev, :], kbuf[cur, :]], axis=0).astype(BF16)
                    vw = jnp.concatenate([vbuf[prev, :], vbuf[cur, :]], axis=0).astype(BF16)
                    s = lax.dot_general(q, kw, NT_DIMS, preferred_element_type=F32) * SCALE + bias
                    if qb == 0:
                        s = jnp.where((col < NQ) & (b == 0), NEG, s)
                    p = jnp.exp(s - lse_ref[_rows(qs, d), :][:, 0:1])
                    w = wg_ref[_rows(qs, d), :]
                    do = (da_ref[_rows(qs, d), :] * w).astype(BF16)
                    dp = lax.dot_general(do, vw, NT_DIMS, preferred_element_type=F32)
                    ds = p * (dp - w[:, 0:1] * dh_ref[_rows(qs, d), :][:, 0:1])
                    dsb_ref[...] += ds
                    dsb = ds.astype(BF16)
                    dq_ref[_rows(qs, d), :] = jnp.dot(dsb, kw, preferred_element_type=F32) * SCALE
                    dkw = lax.dot_general(dsb, q, TN_DIMS, preferred_element_type=F32) * SCALE
                    dvw = lax.dot_general(p.astype(BF16), do, TN_DIMS, preferred_element_type=F32)
                    dkbuf[prev, :] += dkw[0:NQ, :]
                    dkbuf[cur, :] += dkw[NQ:2 * NQ, :]
                    dvbuf[prev, :] += dvw[0:NQ, :]
                    dvbuf[cur, :] += dvw[NQ:2 * NQ, :]
                    return carry

                for r in range(d):
                    unit(r, 0)

        dk_ref[...] = dkbuf[0:RB, :]
        dv_ref[...] = dvbuf[0:RB, :]

    blk = lambda f: pl.BlockSpec((None, RB, HEAD), f)
    cur_g = blk(lambda h, b: (head0 + h, jnp.minimum(b, nbk - 1), 0))
    cur = blk(lambda h, b: (h, jnp.minimum(b, nbk - 1), 0))
    prv = blk(lambda h, b: (h, jnp.maximum(b - 1, 0), 0))
    sq = pl.BlockSpec((None, NQ, 2 * NQ), lambda h, b: (h, 0, 0))
    return pl.pallas_call(
        body, name=name, grid=(nseq, nbk + 1),
        in_specs=[cur_g, cur_g, cur_g, pl.BlockSpec((None, NQ, 2 * NQ), lambda h, b: (jnp.right_shift(h, bias_shift), 0, 0)),
                  cur, cur, cur, cur],
        out_specs=[cur, prv, prv, sq],
        out_shape=[SDS((nseq, S, HEAD), F32)] * 3 + [SDS((nseq, NQ, 2 * NQ), F32)],
        scratch_shapes=[pltpu.VMEM((2 * RB, HEAD), F32)] * 4)(qh, kh, vh, biasm, da, wg, dh, lse)


def _merge_weights(l0, l1, l2):
    m = jnp.maximum(jnp.maximum(l0, l1), l2)
    e0, e1, e2 = jnp.exp(l0 - m), jnp.exp(l1 - m), jnp.exp(l2 - m)
    inv = 1.0 / (e0 + e1 + e2)
    return e0 * inv, e1 * inv, e2 * inv


def _merge_fwd(os_, lses):
    S = os_[0].shape[1]

    def body(o0, o1, o2, l0, l1, l2, a_ref, so, sl):
        for h in range(HPG):
            w0, w1, w2 = _merge_weights(l0[h], l1[h], _from_p4(l2, h, sl))
            a_ref[:, h * HEAD:(h + 1) * HEAD] = (w0 * o0[h] + w1 * o1[h] + w2 * _from_p4(o2, h, so)).astype(BF16)

    hm = pl.BlockSpec((HPG, TM, HEAD), lambda i: (0, i, 0))
    p4 = pl.BlockSpec((HPG, P4, TM // P4, HEAD), lambda i: (0, 0, i, 0))
    return pl.pallas_call(
        body, name="merge_fwd", grid=(S // TM,), in_specs=[hm, hm, p4, hm, hm, p4],
        out_specs=pl.BlockSpec((TM, CH), lambda i: (i, 0)),
        out_shape=SDS((S, CH), BF16), scratch_shapes=[pltpu.VMEM((TM, HEAD), F32)] * 2)(*os_, *lses)


def _merge_bwd(dattn, os_, lses):
    S = dattn.shape[0]

    def body(da_ref, o0, o1, o2, l0, l1, l2, w0_ref, w1_ref, w2_ref, dah_ref, dh_ref, dah2_ref, dh2_ref, so, sl):
        for h in range(HPG):
            w = _merge_weights(l0[h], l1[h], _from_p4(l2, h, sl))
            attn = w[0] * o0[h] + w[1] * o1[h] + w[2] * _from_p4(o2, h, so)
            da = da_ref[:, h * HEAD:(h + 1) * HEAD]
            dh = jnp.broadcast_to(jnp.sum(da * attn, axis=-1, keepdims=True), (TM, HEAD))
            w0_ref[h], w1_ref[h] = w[0], w[1]
            dah_ref[h] = da
            dh_ref[h] = dh
            _to_p4(w2_ref, h, w[2], so)
            _to_p4(dah2_ref, h, da, so)
            _to_p4(dh2_ref, h, dh, so)

    hm = pl.BlockSpec((HPG, TM, HEAD), lambda i: (0, i, 0))
    p4 = pl.BlockSpec((HPG, P4, TM // P4, HEAD), lambda i: (0, 0, i, 0))
    nat, perm = SDS((HPG, S, HEAD), F32), SDS((HPG, P4, S // P4, HEAD), F32)
    w0, w1, w2, dah, dh, dah2, dh2 = pl.pallas_call(
        body, name="merge_bwd", grid=(S // TM,),
        in_specs=[pl.BlockSpec((TM, CH), lambda i: (i, 0)), hm, hm, p4, hm, hm, p4],
        out_specs=[hm, hm, p4, hm, hm, p4, p4], out_shape=[nat, nat, perm, nat, nat, perm, perm],
        scratch_shapes=[pltpu.VMEM((TM, HEAD), F32)] * 2)(dattn, *os_, *lses)
    return (w0, dah, dh), (w1, dah, dh), (w2, dah2, dh2)


def _qkv_bwd(dz, z, dqs, dks, dvs, qw, kw):
    S = z.shape[0]
    nh = N_GROUPS * HPG

    def body(dz_in, zq, zk, *refs):
        del dz_in
        dq_refs, dk_refs, dv_refs = refs[0:3], refs[3:6], refs[6:9]
        qw_ref, kw_ref, dz_ref, dqw_ref, dkw_ref, scr = refs[9:]

        @pl.when(pl.program_id(0) == 0)
        def _():
            dqw_ref[...] = jnp.zeros_like(dqw_ref)
            dkw_ref[...] = jnp.zeros_like(dkw_ref)

        def grad(refs3, g, hh):
            return _from_p4(refs3[g], hh, scr) if g == N_GROUPS - 1 else refs3[g][hh]

        def nbwd(xr, dy, wr, dwr, h, off):
            g = h // HPG
            x = xr[:, h * HEAD:(h + 1) * HEAD].astype(F32)
            r = lax.rsqrt(jnp.mean(x * x, axis=-1, keepdims=True) + RMS_EPS)
            xhat = x * r
            dyw = dy * wr[g:g + 1, :]
            dz_ref[:, off + h * HEAD:off + (h + 1) * HEAD] = (
                r * (dyw - xhat * jnp.mean(dyw * xhat, axis=-1, keepdims=True))).astype(BF16)
            dwr[g:g + 1, :] += jnp.sum(dy * xhat, axis=0, keepdims=True)

        for h in range(nh):
            g, hh = h // HPG, h % HPG
            nbwd(zq, grad(dq_refs, g, hh), qw_ref, dqw_ref, h, O_Q)
            nbwd(zk, grad(dk_refs, g, hh), kw_ref, dkw_ref, h, O_K)
            dz_ref[:, O_V + h * HEAD:O_V + (h + 1) * HEAD] = grad(dv_refs, g, hh).astype(BF16)

    hm = pl.BlockSpec((HPG, TM, HEAD), lambda i: (0, i, 0))
    p4 = pl.BlockSpec((HPG, P4, TM // P4, HEAD), lambda i: (0, 0, i, 0))
    return pl.pallas_call(
        body, name="qkv_bwd", grid=(S // TM,),
        in_specs=[pl.BlockSpec(memory_space=pl.ANY), pl.BlockSpec((TM, QKV_W), lambda i: (i, 0)),
                  pl.BlockSpec((TM, QKV_W), lambda i: (i, 1))] + [hm, hm, p4] * 3 + [_full((N_GROUPS, HEAD)), _full((N_GROUPS, HEAD))],
        out_specs=[pl.BlockSpec((TM, 3 * QKV_W), lambda i: (i, 0)), _full((N_GROUPS, HEAD)), _full((N_GROUPS, HEAD))],
        out_shape=[SDS(dz.shape, BF16), SDS((N_GROUPS, HEAD), F32), SDS((N_GROUPS, HEAD), F32)],
        scratch_shapes=[pltpu.VMEM((TM, HEAD), F32)],
        input_output_aliases={0: 0})(dz, z, z, *dqs, *dks, *dvs, qw, kw)


def _conv_fwd(z, cw, cb, lnw, lnb):
    S = z.shape[0]
    H = 32

    def body(zv, zg, cw_ref, cb_ref, lnw_ref, lnb_ref, u1_ref, u3_ref, xbuf):
        i = pl.program_id(0)

        @pl.when(i == 0)
        def _():
            xbuf[0:H, :] = jnp.zeros((H, CH), F32)

        @pl.when(i > 0)
        def _():
            xbuf[0:H, :] = xbuf[TM:TM + H, :]

        xbuf[H:H + TM, :] = zv[...].astype(F32) * _sig(zg[...].astype(F32))
        for r0 in range(0, TM, CONV_RC):
            rows = pl.ds(r0, CONV_RC)
            parts = []
            for lanes in CONV_LANES:
                part = [jnp.broadcast_to(cb_ref[:, lanes], (CONV_RC, CONV_LB))]

                def tap(q, view, part=part, lanes=lanes):
                    part[0] = part[0] + view * cw_ref[q - 2:q - 1, lanes]

                _conv_taps(xbuf, r0, lanes, 2, CONV_K + 1, tap)
                parts.append(part[0])
            acc = jnp.concatenate(parts, axis=1)
            u1_ref[rows, :] = acc
            mu = jnp.mean(acc, axis=-1, keepdims=True)
            xc = acc - mu
            yl = xc * lax.rsqrt(jnp.mean(xc * xc, axis=-1, keepdims=True) + LN_EPS) * lnw_ref[...] + lnb_ref[...]
            u3_ref[rows, :] = (yl * _sig(yl)).astype(BF16)

    row = pl.BlockSpec((TM, CH), lambda i: (i, 0))
    return pl.pallas_call(
        body, name="conv_fwd", grid=(S // TM,),
        in_specs=[pl.BlockSpec((TM, CH), lambda i: (i, O_CV // CH)), pl.BlockSpec((TM, CH), lambda i: (i, O_CG // CH)),
                  _full((CONV_K, CH)), _full((1, CH)), _full((1, CH)), _full((1, CH))],
        out_specs=[row, row], out_shape=[SDS((S, CH), F32), SDS((S, CH), BF16)],
        scratch_shapes=[pltpu.VMEM((TM + H, CH), F32)])(z, z, cw, cb, lnw, lnb)


def _conv_bwd_a(du3, u1, z, lnw, lnb):
    S = z.shape[0]
    H = 32

    nt = S // TM

    def body(du3_ref, u1_ref, zv, zg, lnw_ref, lnb_ref, du1_ref, acc_ref, xbuf, tacc):
        i = pl.program_id(0)

        @pl.when(i == 0)
        def _():
            xbuf[0:H, :] = jnp.zeros((H, CH), F32)
            tacc[...] = jnp.zeros_like(tacc)

        @pl.when(i > 0)
        def _():
            xbuf[0:H, :] = xbuf[TM:TM + H, :]

        xbuf[H:H + TM, :] = zv[...].astype(F32) * _sig(zg[...].astype(F32))
        for r0 in range(0, TM, CONV_RC):
            rows = pl.ds(r0, CONV_RC)
            u1 = u1_ref[rows, :]
            mu = jnp.mean(u1, axis=-1, keepdims=True)
            xc = u1 - mu
            r = lax.rsqrt(jnp.mean(xc * xc, axis=-1, keepdims=True) + LN_EPS)
            yhat = xc * r
            yl = yhat * lnw_ref[...] + lnb_ref[...]
            sg = _sig(yl)
            dyl = du3_ref[rows, :] * (sg * (1.0 + yl * (1.0 - sg)))
            dyh = dyl * lnw_ref[...]
            du1 = r * (dyh - jnp.mean(dyh, axis=-1, keepdims=True) - yhat * jnp.mean(dyh * yhat, axis=-1, keepdims=True))
            du1_ref[rows, :] = du1
            tacc[33] += _fold8(dyl * yhat)
            tacc[34] += _fold8(dyl)
            tacc[32] += _fold8(du1)
            for lanes in CONV_LANES:
                d = du1[:, lanes]

                def tap(q, view, d=d, lanes=lanes):
                    tacc[q - 2, :, lanes] += _fold8(d * view)

                _conv_taps(xbuf, r0, lanes, 2, CONV_K + 1, tap)

        @pl.when(i == nt - 1)
        def _():
            for k in range(40):
                acc_ref[k:k + 1, :] = jnp.sum(tacc[k], axis=0, keepdims=True)

    row = pl.BlockSpec((TM, CH), lambda i: (i, 0))
    return pl.pallas_call(
        body, name="conv_bwd_a", grid=(nt,),
        in_specs=[row, row, pl.BlockSpec((TM, CH), lambda i: (i, O_CV // CH)), pl.BlockSpec((TM, CH), lambda i: (i, O_CG // CH)),
                  _full((1, CH)), _full((1, CH))],
        out_specs=[row, _full((40, CH))], out_shape=[SDS((S, CH), F32), SDS((40, CH), F32)],
        scratch_shapes=[pltpu.VMEM((TM + H, CH), F32), pltpu.VMEM((40, 8, CH), F32)])(du3, u1, z, z, lnw, lnb)


def _conv_bwd_b(dz, du1, z, cw):
    S = z.shape[0]
    nt = S // TM
    H = 32

    def body(dz_in, du1_ref, zv, zg, cw_ref, dz_ref, ybuf, dgate):
        del dz_in
        i, p = pl.program_id(0), pl.program_id(1)

        @pl.when(p == 0)
        def _():
            @pl.when(i == 0)
            def _():
                ybuf[TM:TM + H, :] = jnp.zeros((H, CH), F32)

            @pl.when(i > 0)
            def _():
                ybuf[TM:TM + H, :] = ybuf[0:H, :]

            ybuf[0:TM, :] = du1_ref[...]
            for r0 in range(0, TM, CONV_RC):
                rows = pl.ds(r0, CONV_RC)
                parts = []
                for lanes in CONV_LANES:
                    part = [jnp.zeros((CONV_RC, CONV_LB), F32)]

                    def tap(q, view, part=part, lanes=lanes):
                        part[0] = part[0] + view * cw_ref[CONV_K - 1 - q:CONV_K - q, lanes]

                    _conv_taps(ybuf, r0, lanes, 0, CONV_K - 1, tap)
                    parts.append(part[0])
                acc = jnp.concatenate(parts, axis=1)
                val = zv[rows, :].astype(F32)
                sg = _sig(zg[rows, :].astype(F32))
                dz_ref[rows, :] = (acc * sg).astype(BF16)
                dgate[rows, :] = (acc * val * sg * (1.0 - sg)).astype(BF16)

        @pl.when(p == 1)
        def _():
            dz_ref[...] = dgate[...]

    rev = lambda c: pl.BlockSpec((TM, CH), lambda i, p: (nt - 1 - i, c))
    return pl.pallas_call(
        body, name="conv_bwd_b", grid=(nt, 2),
        in_specs=[pl.BlockSpec(memory_space=pl.ANY), rev(0), rev(O_CV // CH), rev(O_CG // CH), _full((CONV_K, CH))],
        out_specs=pl.BlockSpec((TM, CH), lambda i, p: (nt - 1 - i, O_CV // CH + p)),
        out_shape=SDS(dz.shape, BF16),
        scratch_shapes=[pltpu.VMEM((TM + H, CH), F32), pltpu.VMEM((TM, CH), BF16)],
        input_output_aliases={0: 0})(dz, du1, z, z, cw)


def _memkv_fwd(mem, mnw, wkv, xkw):
    M, D = mem.shape

    def body(mem_ref, mnw_ref, w_ref, xkw_ref, mn_ref, kv_ref, mk_ref, mv_ref):
        x = mem_ref[...]
        mn = (x * lax.rsqrt(jnp.mean(x * x, axis=-1, keepdims=True) + RMS_EPS) * mnw_ref[...]).astype(BF16)
        mn_ref[...] = mn
        kv = jnp.dot(mn, w_ref[...], preferred_element_type=F32)
        kv_ref[...] = kv
        for h in range(HPG):
            k = kv[:, h * HEAD:(h + 1) * HEAD]
            mk_ref[:, h * HEAD:(h + 1) * HEAD] = (
                k * lax.rsqrt(jnp.mean(k * k, axis=-1, keepdims=True) + RMS_EPS) * xkw_ref[...]).astype(BF16)
        mv_ref[...] = kv[:, CH:2 * CH].astype(BF16)

    return pl.pallas_call(
        body, name="memkv_fwd",
        out_shape=[SDS((M, D), BF16), SDS((M, 2 * CH), F32), SDS((M, CH), BF16), SDS((M, CH), BF16)])(mem, mnw, wkv, xkw)


def _cross_q(zx, xqw, h):
    x = zx[:, h * HEAD:(h + 1) * HEAD].astype(F32)
    r = lax.rsqrt(jnp.mean(x * x, axis=-1, keepdims=True) + RMS_EPS)
    xhat = x * r
    return xhat, r, xhat * xqw


def _cross_fwd(z, xqw, mk, mv):
    S = z.shape[0]
    M = mk.shape[0]

    def body(zx, xqw_ref, mk_ref, mv_ref, o_ref):
        for h in range(HPG):
            sl = slice(h * HEAD, (h + 1) * HEAD)
            _, _, q = _cross_q(zx, xqw_ref[...], h)
            s = lax.dot_general(q.astype(BF16), mk_ref[:, sl], NT_DIMS, preferred_element_type=F32) * SCALE
            e = jnp.exp(s - jnp.max(s, axis=-1, keepdims=True))
            p = e / jnp.sum(e, axis=-1, keepdims=True)
            o_ref[:, sl] = jnp.dot(p.astype(BF16), mv_ref[:, sl], preferred_element_type=F32).astype(BF16)

    return pl.pallas_call(
        body, name="cross_fwd", grid=(S // TM,),
        in_specs=[pl.BlockSpec((TM, CH), lambda i: (i, O_XQ // CH)), _full((1, HEAD)), _full((M, CH)), _full((M, CH))],
        out_specs=pl.BlockSpec((TM, CH), lambda i: (i, 0)), out_shape=SDS((S, CH), BF16))(z, xqw, mk, mv)


def _cross_bwd(dz, doc, z, xqw, mk, mv):
    S = z.shape[0]
    M = mk.shape[0]

    def body(dz_in, do_ref, zx, xqw_ref, mk_ref, mv_ref, dz_ref, dmk_ref, dmv_ref, dxw_ref):
        del dz_in

        @pl.when(pl.program_id(0) == 0)
        def _():
            dmk_ref[...] = jnp.zeros_like(dmk_ref)
            dmv_ref[...] = jnp.zeros_like(dmv_ref)
            dxw_ref[...] = jnp.zeros_like(dxw_ref)

        for h in range(HPG):
            sl = slice(h * HEAD, (h + 1) * HEAD)
            xhat, r, q = _cross_q(zx, xqw_ref[...], h)
            qb = q.astype(BF16)
            s = lax.dot_general(qb, mk_ref[:, sl], NT_DIMS, preferred_element_type=F32) * SCALE
            e = jnp.exp(s - jnp.max(s, axis=-1, keepdims=True))
            p = e / jnp.sum(e, axis=-1, keepdims=True)
            do = do_ref[:, sl].astype(BF16)
            dp = lax.dot_general(do, mv_ref[:, sl], NT_DIMS, preferred_element_type=F32)
            ds = (p * (dp - jnp.sum(p * dp, axis=-1, keepdims=True)) * SCALE).astype(BF16)
            dq = jnp.dot(ds, mk_ref[:, sl], preferred_element_type=F32)
            dmk_ref[:, sl] += lax.dot_general(ds, qb, TN_DIMS, preferred_element_type=F32)
            dmv_ref[:, sl] += lax.dot_general(p.astype(BF16), do, TN_DIMS, preferred_element_type=F32)
            dyw = dq * xqw_ref[...]
            dz_ref[:, sl] = (r * (dyw - xhat * jnp.mean(dyw * xhat, axis=-1, keepdims=True))).astype(BF16)
            dxw_ref[...] += jnp.sum(dq * xhat, axis=0, keepdims=True)

    return pl.pallas_call(
        body, name="cross_bwd", grid=(S // TM,),
        in_specs=[pl.BlockSpec(memory_space=pl.ANY), pl.BlockSpec((TM, CH), lambda i: (i, 0)),
                  pl.BlockSpec((TM, CH), lambda i: (i, O_XQ // CH)), _full((1, HEAD)), _full((M, CH)), _full((M, CH))],
        out_specs=[pl.BlockSpec((TM, CH), lambda i: (i, O_XQ // CH)), _full((M, CH)), _full((M, CH)), _full((1, HEAD))],
        out_shape=[SDS(dz.shape, BF16), SDS((M, CH), F32), SDS((M, CH), F32), SDS((1, HEAD), F32)],
        input_output_aliases={0: 0})(dz, doc, z, xqw, mk, mv)


def _memkv_bwd(dmk, dmv, kv, mem, mn, mnw, wkv, xkw):
    M, D = mem.shape

    def body(dmk_ref, dmv_ref, kv_ref, mem_ref, mn_ref, mnw_ref, w_ref, xkw_ref, dw_ref, dxk_ref, dmn_ref, dkv):
        dxk = jnp.zeros((1, HEAD), F32)
        for h in range(HPG):
            sl = slice(h * HEAD, (h + 1) * HEAD)
            k = kv_ref[:, sl]
            r = lax.rsqrt(jnp.mean(k * k, axis=-1, keepdims=True) + RMS_EPS)
            khat = k * r
            dy = dmk_ref[:, sl]
            dyw = dy * xkw_ref[...]
            dkv[:, sl] = (r * (dyw - khat * jnp.mean(dyw * khat, axis=-1, keepdims=True))).astype(BF16)
            dxk = dxk + jnp.sum(dy * khat, axis=0, keepdims=True)
        dxk_ref[...] = dxk
        dkv[:, CH:2 * CH] = dmv_ref[...].astype(BF16)
        dw_ref[...] = lax.dot_general(mn_ref[...], dkv[...], TN_DIMS, preferred_element_type=F32)
        dn = lax.dot_general(dkv[...], w_ref[...], NT_DIMS, preferred_element_type=F32)
        x = mem_ref[...]
        r = lax.rsqrt(jnp.mean(x * x, axis=-1, keepdims=True) + RMS_EPS)
        dmn_ref[...] = jnp.sum(dn * x * r, axis=0, keepdims=True)

    return pl.pallas_call(
        body, name="memkv_bwd",
        out_shape=[SDS((D, 2 * CH), F32), SDS((1, HEAD), F32), SDS((1, D), F32)],
        scratch_shapes=[pltpu.VMEM((M, 2 * CH), BF16)])(dmk, dmv, kv, mem, mn, mnw, wkv, xkw)


def _branch_proj(a_ref, w_ref, y_ref):
    G, _, n = w_ref.shape
    a = a_ref[...]
    for g in range(G):
        y_ref[:, g * n:(g + 1) * n] = jnp.dot(a, w_ref[g], preferred_element_type=F32)


OUT_RC = 16


def _gates(zg_ref, bg_ref, rows, k, D):
    return _sig(zg_ref[rows, k * D:(k + 1) * D].astype(F32) + bg_ref[:, k * D:(k + 1) * D])


def _outproj_fwd(x, z, bg, attn, u3, oc, wao, wco, wxo, wout, fnw):
    S, D = x.shape
    tm = TM

    def body(x_ref, zg_ref, bg_ref, a_ref, u_ref, c_ref, wa, wc, wx, wo, fnw_ref, h1_ref, hn_ref, ya, yc, yx, mg):
        _branch_proj(a_ref, wa, ya)
        _branch_proj(u_ref, wc, yc)
        _branch_proj(c_ref, wx, yx)
        for r0 in range(0, tm, OUT_RC):
            rows = pl.ds(r0, OUT_RC)
            mg[rows, :] = (_gates(zg_ref, bg_ref, rows, 0, D) * ya[rows, :] + _gates(zg_ref, bg_ref, rows, 1, D) * yc[rows, :]
                           + _gates(zg_ref, bg_ref, rows, 2, D) * yx[rows, :]).astype(BF16)
        ya[...] = jnp.dot(mg[...], wo[...], preferred_element_type=F32)
        for r0 in range(0, tm, OUT_RC):
            rows = pl.ds(r0, OUT_RC)
            h1 = x_ref[rows, :] + ya[rows, :]
            h1_ref[rows, :] = h1
            hn_ref[rows, :] = (h1 * lax.rsqrt(jnp.mean(h1 * h1, axis=-1, keepdims=True) + RMS_EPS) * fnw_ref[...]).astype(BF16)

    row = lambda w: pl.BlockSpec((tm, w), lambda i: (i, 0))
    return pl.pallas_call(
        body, name="outproj_fwd", grid=(S // tm,),
        in_specs=[row(D), pl.BlockSpec((tm, 3 * D), lambda i: (i, O_G // (3 * D))), _full((1, 3 * D)), row(CH), row(CH), row(CH),
                  _full(wao.shape), _full(wco.shape), _full(wxo.shape), _full((D, D)), _full((1, D))],
        out_specs=[row(D), row(D)], out_shape=[SDS((S, D), F32), SDS((S, D), BF16)],
        scratch_shapes=[pltpu.VMEM((tm, D), F32)] * 3 + [pltpu.VMEM((tm, D), BF16)])(x, z, bg, attn, u3, oc, wao, wco, wxo, wout, fnw)


def _outproj_bwd(dh1, z, bg, attn, u3, oc, wao, wco, wxo, wout, n_in):
    S, D = dh1.shape
    tm = 256
    nt = S // tm
    G, _, n = wao.shape

    def body(dh_ref, zg_ref, bg_ref, a_ref, u_ref, c_ref, wa, wc, wx, wo,
             dz_ref, da_ref, du_ref, dc_ref, dbg_ref, dwo_ref, dwa_ref, dwc_ref, dwx_ref,
             ya, yc, yx, dm, dy, mg, bacc, wacc):
        i = pl.program_id(0)

        @pl.when(i == 0)
        def _():
            bacc[...] = jnp.zeros_like(bacc)
            wacc[...] = jnp.zeros_like(wacc)
            dwo_ref[...] = jnp.zeros_like(dwo_ref)

        _branch_proj(a_ref, wa, ya)
        _branch_proj(u_ref, wc, yc)
        _branch_proj(c_ref, wx, yx)
        dhb = dh_ref[...].astype(BF16)
        dm[...] = lax.dot_general(dhb, wo[...], NT_DIMS, preferred_element_type=F32)
        for r0 in range(0, tm, OUT_RC):
            rows = pl.ds(r0, OUT_RC)
            dmv = dm[rows, :]
            merged = jnp.zeros((OUT_RC, D), F32)
            for k, y in enumerate((ya, yc, yx)):
                gk = _gates(zg_ref, bg_ref, rows, k, D)
                yk = y[rows, :]
                merged = merged + gk * yk
                dzg = dmv * yk * gk * (1.0 - gk)
                dz_ref[rows, k * D:(k + 1) * D] = dzg.astype(BF16)
                bacc[:, k * D:(k + 1) * D] += _fold8(dzg)
                dy[k, rows, :] = (dmv * gk).astype(BF16)
            mg[rows, :] = merged.astype(BF16)
        dwo_ref[...] += lax.dot_general(mg[...], dhb, TN_DIMS, preferred_element_type=F32)
        for k, (b_ref, w_ref, db_ref) in enumerate(((a_ref, wa, da_ref), (u_ref, wc, du_ref), (c_ref, wx, dc_ref))):
            dyk = dy[k]
            acc = jnp.zeros((tm, CH), F32)
            for g in range(G):
                acc = acc + lax.dot_general(dyk[:, g * n:(g + 1) * n], w_ref[g], NT_DIMS, preferred_element_type=F32)
            db_ref[...] = acc
            wacc[k] += lax.dot_general(b_ref[...], dyk, TN_DIMS, preferred_element_type=F32)

        @pl.when(i == nt - 1)
        def _():
            dbg_ref[...] = jnp.sum(bacc[...], axis=0, keepdims=True)
            for k, dw_ref in enumerate((dwa_ref, dwc_ref, dwx_ref)):
                for g in range(G):
                    dw_ref[g] = wacc[k, :, g * n:(g + 1) * n]

    row = lambda w: pl.BlockSpec((tm, w), lambda i: (i, 0))
    return pl.pallas_call(
        body, name="outproj_bwd", grid=(nt,),
        in_specs=[row(D), pl.BlockSpec((tm, 3 * D), lambda i: (i, O_G // (3 * D))), _full((1, 3 * D)), row(CH), row(CH), row(CH),
                  _full(wao.shape), _full(wco.shape), _full(wxo.shape), _full((D, D))],
        out_specs=[pl.BlockSpec((tm, 3 * D), lambda i: (i, O_G // (3 * D))), row(CH), row(CH), row(CH), _full((1, 3 * D)),
                   _full((D, D))] + [_full(wao.shape)] * 3,
        out_shape=[SDS((S, n_in), BF16)] + [SDS((S, CH), F32)] * 3 + [SDS((1, 3 * D), F32), SDS((D, D), F32)]
        + [SDS(wao.shape, F32)] * 3,
        scratch_shapes=[pltpu.VMEM((tm, D), F32)] * 4 + [pltpu.VMEM((3, tm, D), BF16), pltpu.VMEM((tm, D), BF16),
                                                        pltpu.VMEM((8, 3 * D), F32), pltpu.VMEM((3, CH, D), F32)],
    )(dh1, z, bg, attn, u3, oc, wao, wco, wxo, wout)


FFN_TC = 256
FFN_H = 8


def _ffn_taps(buf, r0):
    xx = buf[pl.ds(r0, FFN_RC + FFN_H), :]
    return xx[FFN_H:], pltpu.roll(xx, 1, 0)[FFN_H:], pltpu.roll(xx, 2, 0)[FFN_H:]


def _ffn_conv(taps, w_ref, b_ref):
    x0, x1, x2 = taps
    return b_ref[...] + x0 * w_ref[2:3, :] + x1 * w_ref[1:2, :] + x2 * w_ref[0:1, :]


def _ffn_fwd(up, cw, cb, wdown, h1, target):
    S, D = h1.shape
    F2 = up.shape[1]
    F = F2 // 2
    nj = F // FFN_TC
    tm = TM

    def body(up_ref, halo_ref, cw_ref, cb_ref, w_ref, h_ref, t_ref, dy_ref, loss_ref, ac_ref, gc_ref, abuf, gbuf, act_s):
        i = pl.program_id(0)

        @pl.when(i == 0)
        def _():
            loss_ref[...] = jnp.zeros_like(loss_ref)

        for j in range(nj):
            ca, cg = slice(j * FFN_TC, (j + 1) * FFN_TC), slice(F + j * FFN_TC, F + (j + 1) * FFN_TC)
            first = i == 0
            abuf[0:FFN_H, :] = jnp.where(first, 0.0, halo_ref[:, ca].astype(F32))
            gbuf[0:FFN_H, :] = jnp.where(first, 0.0, halo_ref[:, cg].astype(F32))
            abuf[FFN_H:FFN_H + tm, :] = up_ref[:, ca].astype(F32)
            gbuf[FFN_H:FFN_H + tm, :] = up_ref[:, cg].astype(F32)
            for r0 in range(0, tm, FFN_RC):
                rows = pl.ds(r0, FFN_RC)
                a = _ffn_conv(_ffn_taps(abuf, r0), cw_ref[:, ca], cb_ref[:, ca])
                gt = _ffn_conv(_ffn_taps(gbuf, r0), cw_ref[:, cg], cb_ref[:, cg])
                ac_ref[rows, ca] = a.astype(BF16)
                gc_ref[rows, ca] = gt.astype(BF16)
                act_s[rows, ca] = (gt * _sig(gt) * a).astype(BF16)
        err = h_ref[...] + jnp.dot(act_s[...], w_ref[...], preferred_element_type=F32) - t_ref[...]
        dy_ref[...] = err * (1.0 / D)
        loss_ref[...] += 0.5 * jnp.sum(jnp.mean(err * err, axis=-1, keepdims=True))

    row = lambda w: pl.BlockSpec((tm, w), lambda i: (i, 0))
    halo = pl.BlockSpec((FFN_H, F2), lambda i: (jnp.maximum(i * (tm // FFN_H) - 1, 0), 0))
    return pl.pallas_call(
        body, name="ffn_fwd", grid=(S // tm,),
        in_specs=[row(F2), halo, _full((FFN_K, F2)), _full((1, F2)), _full((F, D)), row(D), row(D)],
        out_specs=[row(D), _full((8, 128)), row(F), row(F)],
        out_shape=[SDS((S, D), F32), SDS((8, 128), F32), SDS((S, F), BF16), SDS((S, F), BF16)],
        scratch_shapes=[pltpu.VMEM((tm + FFN_H, FFN_TC), F32)] * 2 + [pltpu.VMEM((tm, F), BF16)])(up, up, cw, cb, wdown, h1, target)


def _ffn_bwd_a(dy, wdown, ac, gc):
    S, D = dy.shape
    F = ac.shape[1]
    nj = F // FFN_TC
    tm = 1024 if S % 1024 == 0 else TM

    def body(dy_ref, wd_ref, a_ref, g_ref, da_ref, dg_ref, acca_ref, accg_ref, dwd_ref, dact_s, act_s):
        i, j = pl.program_id(0), pl.program_id(1)

        @pl.when((i == 0) & (j == 0))
        def _():
            acca_ref[...] = jnp.zeros_like(acca_ref)
            accg_ref[...] = jnp.zeros_like(accg_ref)
            dwd_ref[...] = jnp.zeros_like(dwd_ref)

        dyb = dy_ref[...].astype(BF16)
        dact_s[...] = lax.dot_general(dyb, wd_ref[...], NT_DIMS, preferred_element_type=F32)
        pa = pg = jnp.zeros((8, FFN_TC), F32)
        for r0 in range(0, tm, FFN_RC):
            rows = pl.ds(r0, FFN_RC)
            a = a_ref[rows, :].astype(F32)
            gt = g_ref[rows, :].astype(F32)
            dact = dact_s[rows, :]
            sg = _sig(gt)
            silu = gt * sg
            act_s[rows, :] = (silu * a).astype(BF16)
            dac = dact * silu
            dgc = dact * a * (sg * (1.0 + gt * (1.0 - sg)))
            da_ref[rows, :] = dac.astype(BF16)
            dg_ref[rows, :] = dgc.astype(BF16)
            pa = pa + _fold8(dac)
            pg = pg + _fold8(dgc)
        acca_ref[j] += pa
        accg_ref[j] += pg
        dwd_ref[pl.ds(pl.multiple_of(j * FFN_TC, FFN_TC), FFN_TC), :] += lax.dot_general(
            act_s[...], dyb, TN_DIMS, preferred_element_type=F32)

    col = pl.BlockSpec((tm, FFN_TC), lambda i, j: (i, j))
    return pl.pallas_call(
        body, name="ffn_bwd_a", grid=(S // tm, nj),
        in_specs=[pl.BlockSpec((tm, D), lambda i, j: (i, 0)), pl.BlockSpec((FFN_TC, D), lambda i, j: (j, 0)), col, col],
        out_specs=[col, col] + [_full((nj, 8, FFN_TC))] * 2 + [_full((F, D))],
        out_shape=[SDS((S, F), BF16)] * 2 + [SDS((nj, 8, FFN_TC), F32)] * 2 + [SDS((F, D), F32)],
        scratch_shapes=[pltpu.VMEM((tm, FFN_TC), F32), pltpu.VMEM((tm, FFN_TC), BF16)])(dy, wdown, ac, gc)


def _ffn_bwd_b(dca, dcg, up, cw):
    S, F = dca.shape
    nj = F // FFN_TC
    tm = 4096 if S % 4096 == 0 else TM
    nt = S // tm
    span = FFN_RC + FFN_H

    def body(a_ref, g_ref, u_ref, w_ref, o_ref, tacc_ref, ybuf):
        j, i = pl.program_id(0), pl.program_id(1)

        @pl.when(i == 0)
        def _():
            ybuf[tm:tm + FFN_H, :] = jnp.zeros((FFN_H, FFN_TC), F32)
            tacc_ref[...] = jnp.zeros_like(tacc_ref)

        @pl.when(i > 0)
        def _():
            ybuf[tm:tm + FFN_H, :] = ybuf[0:FFN_H, :]

        ybuf[0:tm, :] = jnp.where(j < nj, a_ref[...], g_ref[...]).astype(F32)
        p = [jnp.zeros((8, FFN_TC), F32)] * FFN_K
        for r0 in range(0, tm, FFN_RC):
            rows = pl.ds(r0, FFN_RC)
            yy = ybuf[pl.ds(r0, span), :]
            ys = (yy[:FFN_RC], pltpu.roll(yy, span - 1, 0)[:FFN_RC], pltpu.roll(yy, span - 2, 0)[:FFN_RC])
            o_ref[rows, :] = (ys[0] * w_ref[2:3, :] + ys[1] * w_ref[1:2, :] + ys[2] * w_ref[0:1, :]).astype(BF16)
            u = u_ref[rows, :].astype(F32)
            for k in range(FFN_K):
                p[k] = p[k] + _fold8(ys[FFN_K - 1 - k] * u)
        for k in range(FFN_K):
            tacc_ref[k] += p[k]

    rev = lambda f: pl.BlockSpec((tm, FFN_TC), lambda j, i: (nt - 1 - i, f(j)))
    return pl.pallas_call(
        body, name="ffn_bwd_b", grid=(2 * nj, nt),
        in_specs=[rev(lambda j: jnp.minimum(j, nj - 1)), rev(lambda j: jnp.maximum(j - nj, 0)), rev(lambda j: j),
                  pl.BlockSpec((FFN_K, FFN_TC), lambda j, i: (0, j))],
        out_specs=[rev(lambda j: j), pl.BlockSpec((None, FFN_K, 8, FFN_TC), lambda j, i: (j, 0, 0, 0))],
        out_shape=[SDS((S, 2 * F), BF16), SDS((2 * nj, FFN_K, 8, FFN_TC), F32)],
        scratch_shapes=[pltpu.VMEM((tm + FFN_H, FFN_TC), F32)])(dca, dcg, up, cw)


def _adamw_update(w_ref, g_ref, m_ref, v_ref, d_ref, nm_ref, nv_ref):
    gv = g_ref[...]
    m2 = ADAM_B1 * m_ref[...] + (1.0 - ADAM_B1) * gv
    v2 = ADAM_B2 * v_ref[...] + (1.0 - ADAM_B2) * jnp.square(gv)
    m_hat = m2 / (1.0 - ADAM_B1 ** ADAM_STEP)
    v_hat = v2 / (1.0 - ADAM_B2 ** ADAM_STEP)
    d_ref[...] = -ADAM_LR * (m_hat / (jnp.sqrt(v_hat) + ADAM_EPS) + ADAM_WD * w_ref[...])
    nm_ref[...] = m2
    nv_ref[...] = v2


def _adamw_small(ws, gs, ms, vs):
    n = len(ws)

    def body(*refs):
        for i in range(n):
            _adamw_update(*[refs[k * n + i] for k in range(7)])

    shapes = [SDS(w.shape, F32) for w in ws]
    res = pl.pallas_call(body, name="adamw_small", out_shape=shapes * 3)(*ws, *gs, *ms, *vs)
    return res[:n], res[n:2 * n], res[2 * n:]


def _adamw(w, g, m, v, name):
    R, C = w.shape
    tr = _row_tile(R, max(8, (2 ** 20) // (4 * C) // 8 * 8))

    def body(w_ref, g_ref, m_ref, v_ref, d_ref, nm_ref, nv_ref):
        _adamw_update(w_ref, g_ref, m_ref, v_ref, d_ref, nm_ref, nv_ref)

    blk = pl.BlockSpec((tr, C), lambda i: (i, 0))
    return pl.pallas_call(
        body, name=name, grid=(R // tr,), in_specs=[blk] * 4, out_specs=[blk] * 3,
        out_shape=[SDS((R, C), F32)] * 3)(w, g, m, v)


HBM_SPEC = pl.BlockSpec(memory_space=pltpu.HBM)
SEM_SPEC = pl.BlockSpec(memory_space=pltpu.SEMAPHORE)
DATAFLOW_EFFECT = pltpu.SideEffectType.DATAFLOW_SIDE_EFFECTING


def _position():
    return lax.axis_index("x"), lax.axis_index("y"), lax.axis_index("c")


def _other_chips(x, y):
    return [(1 - x, y), (x, 1 - y), (1 - x, 1 - y)]


def _all_gather_xy(arrs, name):
    n = len(arrs)
    hbm = pl.BlockSpec(memory_space=pl.ANY)

    def body(*refs):
        ins, outs = refs[:n], refs[n:2 * n]
        send_sems, recv_sems = refs[2 * n:]
        x, y, c = _position()
        me = 2 * x + y
        chips = _other_chips(x, y)

        def rcopy(i, k, src, dst, to):
            return pltpu.make_async_remote_copy(src_ref=src, dst_ref=dst, send_sem=send_sems.at[i, k], recv_sem=recv_sems.at[i, k],
                                                device_id=to, device_id_type=MESH)

        sends = []
        for i in range(n):
            own = rcopy(i, 6, ins[i], outs[i].at[me], (x, y, 1 - c))
            own.start()
            sends.append(own)
        for i in range(n):
            for j, (px, py) in enumerate(chips):
                cp = rcopy(i, j, ins[i].at[c], outs[i].at[me, c], (px, py, c))
                cp.start()
                sends.append(cp)
        for i in range(n):
            for j, (px, py) in enumerate(chips):
                got = outs[i].at[2 * px + py, c]
                rcopy(i, j, ins[i].at[c], got, (x, y, c)).wait_recv()
                fwd = rcopy(i, 3 + j, got, got, (x, y, 1 - c))
                fwd.start()
                sends.append(fwd)
        for i in range(n):
            for j, (px, py) in enumerate(chips):
                theirs = outs[i].at[2 * px + py, 1 - c]
                rcopy(i, 3 + j, theirs, theirs, (x, y, c)).wait_recv()
        for i in range(n):
            rcopy(i, 6, ins[i], outs[i].at[me], (x, y, c)).wait_recv()
        for cp in sends:
            cp.wait_send()

    return pl.pallas_call(
        body, name=name, in_specs=[hbm] * n, out_specs=[hbm] * n,
        out_shape=[SDS((4,) + a.shape, a.dtype) for a in arrs],
        scratch_shapes=[pltpu.SemaphoreType.DMA((n, 7)), pltpu.SemaphoreType.DMA((n, 7))])(*arrs)


def _ag_ici_start(arrs, name):
    n = len(arrs)

    def body(*refs):
        ins, lands = refs[:n], refs[n:2 * n]
        send_sems, recv_sems = refs[2 * n:2 * n + 2]
        token = refs[-1]
        x, y, c = _position()
        for i in range(n):
            for j, (px, py) in enumerate(_other_chips(x, y)):
                pltpu.make_async_remote_copy(src_ref=ins[i].at[c], dst_ref=lands[i].at[2 * x + y, c], send_sem=send_sems.at[3 * i + j],
                                             recv_sem=recv_sems.at[3 * i + j], device_id=(px, py, c), device_id_type=MESH).start()
        token[...] = jnp.zeros_like(token)

    lands = [lax.empty((4,) + a.shape, a.dtype) for a in arrs]
    res = pl.pallas_call(
        body, name=name,
        out_shape=[pltpu.SemaphoreType.DMA((3 * n,)), pltpu.SemaphoreType.DMA((3 * n,))]
        + [pltpu.HBM(a.shape, a.dtype) for a in arrs] + [pltpu.HBM(l.shape, l.dtype) for l in lands] + [SDS((8, 128), F32)],
        in_specs=[HBM_SPEC] * (2 * n), out_specs=[SEM_SPEC, SEM_SPEC] + [HBM_SPEC] * (2 * n) + [pl.BlockSpec(memory_space=pltpu.VMEM)],
        input_output_aliases={i: 2 + i for i in range(2 * n)},
        compiler_params=pltpu.CompilerParams(has_side_effects=DATAFLOW_EFFECT),
    )(*[pltpu.with_memory_space_constraint(a, pltpu.HBM) for a in list(arrs) + lands])
    return res[0], res[1], list(res[2:2 + n]), list(res[2 + n:2 + 2 * n]), res[-1]


def _ag_ici_wait(send_sems, recv_sems, ins, lands, after, name):
    n = len(ins)

    def body(*refs):
        ins_r, lands_r = refs[:n], refs[n:2 * n]
        send_r, recv_r = refs[2 * n:2 * n + 2]
        x, y, c = _position()
        for i in range(n):
            for j, (px, py) in enumerate(_other_chips(x, y)):
                cp = pltpu.make_async_remote_copy(src_ref=ins_r[i].at[c], dst_ref=lands_r[i].at[2 * px + py, c], send_sem=send_r.at[3 * i + j],
                                                  recv_sem=recv_r.at[3 * i + j], device_id=(px, py, c), device_id_type=MESH)
                cp.wait_send()
                cp.wait_recv()

    res = pl.pallas_call(
        body, name=name,
        out_shape=[pltpu.HBM(a.shape, a.dtype) for a in list(ins) + list(lands)],
        in_specs=[HBM_SPEC] * (2 * n) + [SEM_SPEC, SEM_SPEC, pl.BlockSpec(memory_space=pl.ANY)], out_specs=[HBM_SPEC] * (2 * n),
        input_output_aliases={i: i for i in range(2 * n)},
        compiler_params=pltpu.CompilerParams(has_side_effects=DATAFLOW_EFFECT),
    )(*ins, *lands, send_sems, recv_sems, after)
    return list(res[:n]), list(res[n:])


def _ag_finish(arrs, lands, name):
    n = len(arrs)
    hbm = pl.BlockSpec(memory_space=pl.ANY)

    def body(*refs):
        ins, landed, outs = refs[:n], refs[n:2 * n], refs[2 * n:3 * n]
        send_sems, recv_sems = refs[3 * n:]
        x, y, c = _position()
        chips = _other_chips(x, y)
        sends = []
        for i in range(n):
            own = pltpu.make_async_remote_copy(src_ref=ins[i], dst_ref=outs[i].at[2 * x + y], send_sem=send_sems.at[i, 3],
                                               recv_sem=recv_sems.at[i, 3], device_id=(x, y, 1 - c), device_id_type=MESH)
            own.start()
            sends.append(own)
        for i in range(n):
            for j, (px, py) in enumerate(chips):
                fwd = pltpu.make_async_remote_copy(src_ref=landed[i].at[2 * px + py, c], dst_ref=outs[i].at[2 * px + py, c],
                                                   send_sem=send_sems.at[i, j], recv_sem=recv_sems.at[i, j],
                                                   device_id=(x, y, 1 - c), device_id_type=MESH)
                fwd.start()
                sends.append(fwd)
        for i in range(n):
            for j, (px, py) in enumerate(chips):
                theirs = outs[i].at[2 * px + py, 1 - c]
                pltpu.make_async_remote_copy(src_ref=theirs, dst_ref=theirs, send_sem=send_sems.at[i, j], recv_sem=recv_sems.at[i, j],
                                             device_id=(x, y, c), device_id_type=MESH).wait_recv()
        for i in range(n):
            pltpu.make_async_remote_copy(src_ref=ins[i], dst_ref=outs[i].at[2 * x + y], send_sem=send_sems.at[i, 3],
                                         recv_sem=recv_sems.at[i, 3], device_id=(x, y, c), device_id_type=MESH).wait_recv()
        for cp in sends:
            cp.wait_send()

    return pl.pallas_call(
        body, name=name, in_specs=[hbm] * (2 * n), out_specs=[hbm] * n,
        out_shape=[SDS(l.shape, l.dtype) for l in lands],
        input_output_aliases={n + i: i for i in range(n)},
        scratch_shapes=[pltpu.SemaphoreType.DMA((n, 4)), pltpu.SemaphoreType.DMA((n, 4))])(*arrs, *lands)


def _swap_halves(gs, name):
    n = len(gs)
    hbm = pl.BlockSpec(memory_space=pl.ANY)

    def body(*refs):
        ins, outs = refs[:n], refs[n:2 * n]
        send_sems, recv_sems = refs[2 * n:]
        x, y, c = _position()
        cps = []
        for i in range(n):
            for p in range(4):
                cp = pltpu.make_async_remote_copy(src_ref=ins[i].at[p, 1 - c], dst_ref=outs[i].at[p], send_sem=send_sems.at[4 * i + p],
                                                  recv_sem=recv_sems.at[4 * i + p], device_id=(x, y, 1 - c), device_id_type=MESH)
                cp.start()
                cps.append(cp)
        for cp in cps:
            cp.wait()

    return pl.pallas_call(
        body, name=name, in_specs=[hbm] * n, out_specs=[hbm] * n,
        out_shape=[SDS((4,) + g.shape[2:], g.dtype) for g in gs],
        scratch_shapes=[pltpu.SemaphoreType.DMA((4 * n,)), pltpu.SemaphoreType.DMA((4 * n,))])(*gs)


def _exchange_start(s1s, name):
    n = len(s1s)

    def body(*refs):
        srcs, lands = refs[:n], refs[n:2 * n]
        send_sems, recv_sems = refs[2 * n:2 * n + 2]
        token = refs[-1]
        x, y, c = _position()
        for i in range(n):
            for j, (px, py) in enumerate(_other_chips(x, y)):
                pltpu.make_async_remote_copy(src_ref=srcs[i].at[2 * px + py], dst_ref=lands[i].at[j], send_sem=send_sems.at[3 * i + j],
                                             recv_sem=recv_sems.at[3 * i + j], device_id=(px, py, c), device_id_type=MESH).start()
        token[...] = jnp.zeros_like(token)

    lands = [lax.empty((3,) + s.shape[1:], F32) for s in s1s]
    res = pl.pallas_call(
        body, name=name,
        out_shape=[pltpu.SemaphoreType.DMA((3 * n,)), pltpu.SemaphoreType.DMA((3 * n,))]
        + [pltpu.HBM(a.shape, F32) for a in list(s1s) + lands] + [SDS((8, 128), F32)],
        in_specs=[HBM_SPEC] * (2 * n), out_specs=[SEM_SPEC, SEM_SPEC] + [HBM_SPEC] * (2 * n) + [pl.BlockSpec(memory_space=pltpu.VMEM)],
        input_output_aliases={i: 2 + i for i in range(2 * n)},
        compiler_params=pltpu.CompilerParams(has_side_effects=DATAFLOW_EFFECT),
    )(*[pltpu.with_memory_space_constraint(a, pltpu.HBM) for a in list(s1s) + lands])
    return res[0], res[1], list(res[2:2 + n]), list(res[2 + n:2 + 2 * n]), res[-1]


def _exchange_wait(send_sems, recv_sems, s1s, lands, after, name):
    n = len(s1s)

    def body(*refs):
        srcs, lands_r = refs[:n], refs[n:2 * n]
        send_r, recv_r = refs[2 * n:2 * n + 2]
        x, y, c = _position()
        for i in range(n):
            for j, (px, py) in enumerate(_other_chips(x, y)):
                cp = pltpu.make_async_remote_copy(src_ref=srcs[i].at[2 * px + py], dst_ref=lands_r[i].at[j], send_sem=send_r.at[3 * i + j],
                                                  recv_sem=recv_r.at[3 * i + j], device_id=(px, py, c), device_id_type=MESH)
                cp.wait_send()
                cp.wait_recv()

    res = pl.pallas_call(
        body, name=name, out_shape=[pltpu.HBM(a.shape, F32) for a in list(s1s) + list(lands)],
        in_specs=[HBM_SPEC] * (2 * n) + [SEM_SPEC, SEM_SPEC, pl.BlockSpec(memory_space=pl.ANY)], out_specs=[HBM_SPEC] * (2 * n),
        input_output_aliases={i: i for i in range(2 * n)},
        compiler_params=pltpu.CompilerParams(has_side_effects=DATAFLOW_EFFECT),
    )(*s1s, *lands, send_sems, recv_sems, after)
    return list(res[:n]), list(res[n:])


def _join_halves(f2s, name):
    n = len(f2s)
    hbm = pl.BlockSpec(memory_space=pl.ANY)

    def body(*refs):
        ins, outs = refs[:n], refs[n:2 * n]
        send_sems, recv_sems = refs[2 * n:]
        x, y, c = _position()
        cps = []
        for i in range(n):
            cp = pltpu.make_async_remote_copy(src_ref=ins[i].at[c], dst_ref=outs[i].at[c], send_sem=send_sems.at[i],
                                              recv_sem=recv_sems.at[i], device_id=(x, y, 1 - c), device_id_type=MESH)
            cp.start()
            cps.append(cp)
        for i in range(n):
            pltpu.make_async_remote_copy(src_ref=ins[i].at[1 - c], dst_ref=outs[i].at[1 - c], send_sem=send_sems.at[i],
                                         recv_sem=recv_sems.at[i], device_id=(x, y, 1 - c), device_id_type=MESH).wait_recv()
        for cp in cps:
            cp.wait_send()

    return pl.pallas_call(
        body, name=name, in_specs=[hbm] * n, out_specs=[hbm] * n, out_shape=[SDS(f.shape, f.dtype) for f in f2s],
        input_output_aliases={i: i for i in range(n)},
        scratch_shapes=[pltpu.SemaphoreType.DMA((n,)), pltpu.SemaphoreType.DMA((n,))])(*f2s)


def _sum_tile(rows, cols):
    return _row_tile(rows, max(8, (2 ** 19 // cols) // 8 * 8))


def _add_pair(g, r1, c, name):
    _, R, C = r1.shape
    tr = _sum_tile(R, C)

    def body(c_ref, a_ref, b_ref, o_ref):
        del c_ref
        o_ref[...] = a_ref[...] + b_ref[...]

    blk = pl.BlockSpec((None, tr, C), lambda p, i, cr: (p, i, 0))
    return pl.pallas_call(
        body, name=name,
        grid_spec=pltpu.PrefetchScalarGridSpec(
            num_scalar_prefetch=1, grid=(4, R // tr),
            in_specs=[pl.BlockSpec((None, None, tr, C), lambda p, i, cr: (p, cr[0], i, 0)), blk], out_specs=blk),
        out_shape=SDS((4, R, C), F32))(c, g, r1)


def _add_four(s1, r2, me_c, name):
    _, R, C = s1.shape
    tr = _sum_tile(R, C)

    def body(m_ref, a_ref, b_ref, o_ref):
        del m_ref
        o_ref[...] = ((a_ref[...] + b_ref[0]) + b_ref[1]) + b_ref[2]

    return pl.pallas_call(
        body, name=name,
        grid_spec=pltpu.PrefetchScalarGridSpec(
            num_scalar_prefetch=1, grid=(R // tr,),
            in_specs=[pl.BlockSpec((None, tr, C), lambda i, mr: (mr[0], i, 0)), pl.BlockSpec((3, tr, C), lambda i, mr: (0, i, 0))],
            out_specs=pl.BlockSpec((None, tr, C), lambda i, mr: (mr[1], i, 0))),
        out_shape=SDS((2, R, C), F32))(me_c, s1, r2)


def _all_reduce_small(vs):
    n = len(vs)

    def body(*refs):
        ins, outs, bufs = refs[:n], refs[n:2 * n], refs[2 * n:3 * n]
        send_sems, recv_sems = refs[3 * n:]
        x, y, c = _position()
        me = 4 * x + 2 * y + c
        for i in range(n):
            bufs[i][me] = ins[i][...]
        cps = []
        for k in range(1, 8):
            to = (1 - x if k & 4 else x, 1 - y if k & 2 else y, 1 - c if k & 1 else c)
            for i in range(n):
                cp = pltpu.make_async_remote_copy(src_ref=bufs[i].at[me], dst_ref=bufs[i].at[me], send_sem=send_sems.at[7 * i + k - 1],
                                                  recv_sem=recv_sems.at[7 * i + k - 1], device_id=to, device_id_type=MESH)
                cp.start()
                cps.append(cp)
        for cp in cps:
            cp.wait_send()
        for k in range(1, 8):
            src = 4 * (1 - x if k & 4 else x) + 2 * (1 - y if k & 2 else y) + (1 - c if k & 1 else c)
            for i in range(n):
                pltpu.make_async_remote_copy(src_ref=bufs[i].at[src], dst_ref=bufs[i].at[src], send_sem=send_sems.at[7 * i + k - 1],
                                             recv_sem=recv_sems.at[7 * i + k - 1], device_id=(x, y, c), device_id_type=MESH).wait_recv()
        for i in range(n):
            acc = bufs[i][0]
            for k in range(1, 8):
                acc = acc + bufs[i][k]
            outs[i][...] = acc

    vm = pl.BlockSpec(memory_space=pltpu.VMEM)
    return pl.pallas_call(
        body, name="all_reduce_small", in_specs=[vm] * n, out_specs=[vm] * n, out_shape=[SDS(v.shape, F32) for v in vs],
        scratch_shapes=[pltpu.VMEM((8,) + v.shape, F32) for v in vs]
        + [pltpu.SemaphoreType.DMA((7 * n,)), pltpu.SemaphoreType.DMA((7 * n,))])(*vs)


def _reduce_begin(grads, tag):
    _, _, c = _position()
    cs = jnp.reshape(c, (1,)).astype(jnp.int32)
    g4 = [g.reshape(4, 2, g.shape[1] // 2, g.shape[2]) for g in grads]
    r1 = _swap_halves(g4, "rs_swap_" + tag)
    s1 = [_add_pair(g, r, cs, f"rs_add_pair_{tag}{i}") for i, (g, r) in enumerate(zip(g4, r1))]
    send_sems, recv_sems, s1, lands, token = _exchange_start(s1, "rs_exchange_start_" + tag)
    return (send_sems, recv_sems, s1, lands), token


def _reduce_end(state, after, tag):
    x, y, c = _position()
    send_sems, recv_sems, s1, lands = state
    s1, lands = _exchange_wait(send_sems, recv_sems, s1, lands, after, "rs_exchange_wait_" + tag)
    me_c = jnp.stack([2 * x + y, c]).astype(jnp.int32)
    f2 = [_add_four(s, l, me_c, f"rs_add_four_{tag}{i}") for i, (s, l) in enumerate(zip(s1, lands))]
    return [f.reshape(2 * f.shape[1], f.shape[2]) for f in _join_halves(f2, "rs_join_" + tag)]


def _halves(a):
    return a.reshape((2, a.shape[0] // 2) + a.shape[1:])


def _after(a, token):
    return a + token[0, 0]


def _in_proj_own(x, nw, w, part):
    M, K = x.shape
    n = w.shape[2]
    sub = 256

    def body(p_ref, x_ref, nw_ref, w_ref, xn_ref, z_ref):
        del p_ref
        for r0 in range(0, MM_TM, sub):
            rows = pl.ds(r0, sub)
            for c0 in range(r0, r0 + sub, NORM_RC):
                ch = pl.ds(c0, NORM_RC)
                xv = x_ref[ch, :]
                xn_ref[ch, :] = (xv * lax.rsqrt(jnp.mean(xv * xv, axis=-1, keepdims=True) + RMS_EPS) * nw_ref[...]).astype(BF16)
            z_ref[rows, :] = jnp.dot(xn_ref[rows, :], w_ref[...], preferred_element_type=F32).astype(BF16)

    row = pl.BlockSpec((MM_TM, K), lambda i, pr: (i, 0))
    return pl.pallas_call(
        body, name="in_proj_own",
        grid_spec=pltpu.PrefetchScalarGridSpec(
            num_scalar_prefetch=1, grid=(M // MM_TM,),
            in_specs=[row, pl.BlockSpec((1, K), lambda i, pr: (0, 0)), pl.BlockSpec((None, K, n), lambda i, pr: (0, 0, 0))],
            out_specs=[row, pl.BlockSpec((MM_TM, n), lambda i, pr: (i, pr[0]))]),
        out_shape=[SDS((M, K), BF16), SDS((M, 4 * n), BF16)])(part, x, nw, w)


def _in_proj_parts(xn, w, parts, z, name):
    M, K = xn.shape
    _, _, n = w.shape
    P = parts.shape[0]

    def body(p_ref, a_ref, b_ref, z_in, o_ref):
        del p_ref, z_in
        o_ref[...] = jnp.dot(a_ref[...], b_ref[...], preferred_element_type=F32).astype(BF16)

    return pl.pallas_call(
        body, name=name,
        grid_spec=pltpu.PrefetchScalarGridSpec(
            num_scalar_prefetch=1, grid=(P, M // MM_TM),
            in_specs=[pl.BlockSpec((MM_TM, K), lambda g, i, pr: (i, 0)), pl.BlockSpec((None, K, n), lambda g, i, pr: (pr[g], 0, 0)),
                      pl.BlockSpec(memory_space=pl.ANY)],
            out_specs=pl.BlockSpec((MM_TM, n), lambda g, i, pr: (i, pr[g]))),
        out_shape=SDS(z.shape, BF16), input_output_aliases={3: 0})(parts, xn, w, z)


def _local_step(x, mem, target, sp, ex):
    S, D = x.shape
    band, buckets = _bias_static()
    buckets = jnp.asarray(buckets)

    tok = ex.start_first()
    own, me, others = ex.own_w_in()
    xn, z = _in_proj_own(x, _after(sp["attn_norm_w"], tok), own, me)
    w_in, gathered_small = ex.first_weights(after=z)
    sp = {**sp, **gathered_small}
    n_in = 4 * w_in.shape[2]
    bw = {"w_in": w_in}
    z = _in_proj_parts(xn, w_in, others + ex.start_rest()[0, 0].astype(jnp.int32), z, "in_proj_rest")
    qh, kh, vh, q2, k2, v2 = _qkv_prep(z, sp["q_norm_w"], sp["k_norm_w"])
    tab = sp["rel_bias_table"].T.reshape(N_GROUPS, HPG, N_BUCKETS)
    bias = _bias_fwd(jnp.pad(tab, ((0, 0), (0, 8 - HPG), (0, 0))), buckets)
    biasm = jnp.where(jnp.asarray(band)[None, None], bias[:, :HPG].reshape(N_GROUPS, HPG, NQ, 2 * NQ), NEG)
    seqs = lambda a: a.reshape(HPG * P4, S // P4, HEAD)
    groups = ((qh, kh, vh, 1, 0, HPG, 0), (qh, kh, vh, 4, HPG, HPG, 0), (seqs(q2), seqs(k2), seqs(v2), 4, 0, HPG * P4, 2))
    os_, lses = [], []
    for g, (qg, kg, vg, dil, head0, nseq, shift) in enumerate(groups):
        o_g, lse_g = _attn_fwd(qg, kg, vg, biasm[g], f"attn_fwd_g{g}", dil, head0, nseq, shift)
        os_.append(o_g.reshape(HPG, -1, o_g.shape[1], HEAD) if g == 2 else o_g)
        lses.append(lse_g.reshape(HPG, -1, lse_g.shape[1], HEAD) if g == 2 else lse_g)
    attn = _merge_fwd(os_, lses)
    u1, u3 = _conv_fwd(z, sp["conv_dw_w"], sp["conv_dw_b"], sp["conv_ln_w"], sp["conv_ln_b"])
    bw.update(ex.rest_weights(after=attn))
    F2 = 4 * bw["w_up"].shape[2]
    mn, kv, mk, mv = _memkv_fwd(mem, sp["mem_norm_w"], bw["w_mem_kv"], sp["xk_norm_w"])
    oc = _cross_fwd(z, sp["xq_norm_w"], mk, mv)
    h1, hn = _outproj_fwd(x, z, sp["b_gate"], attn, u3, oc, bw["w_attn_o"], bw["w_conv_o"], bw["w_cross_o"], bw["w_out"],
                          sp["ffn_norm_w"])
    up = _mm_nn(hn, bw["w_up"], BF16, "ffn_up")
    dy, loss_tile, ac, gc = _ffn_fwd(up, sp["ffn_conv_w"], sp["ffn_conv_b"], bw["w_down"], h1, target)

    gs, gb = {}, {}
    dca, dcg, acca, accg, gb["w_down"] = _ffn_bwd_a(dy, bw["w_down"], ac, gc)
    cols = lambda acc: jnp.sum(acc, axis=1).reshape(1, F2 // 2)
    gs["ffn_conv_b"] = jnp.concatenate([cols(acca), cols(accg)], axis=1)
    dup, tacc = _ffn_bwd_b(dca, dcg, up, sp["ffn_conv_w"])
    gs["ffn_conv_w"] = jnp.transpose(jnp.sum(tacc, axis=2), (1, 0, 2)).reshape(FFN_K, F2)
    gb["w_up"] = _mm_tn(hn, dup, 4, "dw_up")
    tok = ex.reduce_begin("a", ("w_down", "w_up"), gb)
    dh1, gs["ffn_norm_w"] = _norm_in_bwd(dup, bw["w_up"], h1, _after(sp["ffn_norm_w"], tok), dy, "ffn_in_bwd")
    dz, dattn, du3, doc, gs["b_gate"], gb["w_out"], gb["w_attn_o"], gb["w_conv_o"], gb["w_cross_o"] = _outproj_bwd(
        dh1, z, sp["b_gate"], attn, u3, oc, bw["w_attn_o"], bw["w_conv_o"], bw["w_cross_o"], bw["w_out"], n_in)
    dz, dmk, dmv, gs["xq_norm_w"] = _cross_bwd(dz, doc, z, sp["xq_norm_w"], mk, mv)
    gb["w_mem_kv"], gs["xk_norm_w"], gs["mem_norm_w"] = _memkv_bwd(
        dmk, dmv, kv, mem, mn, sp["mem_norm_w"], bw["w_mem_kv"], sp["xk_norm_w"])
    ex.reduce_end("a", after=gs["mem_norm_w"])
    tok = ex.reduce_begin("b", ("w_out", "w_attn_o", "w_conv_o", "w_cross_o", "w_mem_kv"), gb)
    du1, cacc = _conv_bwd_a(du3, u1, z, _after(sp["conv_ln_w"], tok), sp["conv_ln_b"])
    gs["conv_dw_w"], gs["conv_dw_b"] = cacc[:CONV_K], cacc[32:33]
    gs["conv_ln_w"], gs["conv_ln_b"] = cacc[33:34], cacc[34:35]
    dz = _conv_bwd_b(dz, du1, z, sp["conv_dw_w"])
    merged_grads = _merge_bwd(dattn, os_, lses)
    dqs, dks, dvs, dsbs = [], [], [], []
    for g, (qg, kg, vg, dil, head0, nseq, shift) in enumerate(groups):
        wg_g, da_g, dh_g = merged_grads[g]
        lse_g = lses[g]
        if g == 2:
            wg_g, da_g, dh_g, lse_g = seqs(wg_g), seqs(da_g), seqs(dh_g), seqs(lse_g)
        dq_g, dk_g, dv_g, dsb_g = _attn_bwd(qg, kg, vg, biasm[g], da_g, wg_g, dh_g, lse_g, f"attn_bwd_g{g}", dil, head0, nseq, shift)
        if g == 2:
            dq_g, dk_g, dv_g = (t.reshape(HPG, P4, S // P4, HEAD) for t in (dq_g, dk_g, dv_g))
            dsb_g = jnp.sum(dsb_g.reshape(HPG, P4, NQ, 2 * NQ), axis=1)
        dqs.append(dq_g)
        dks.append(dk_g)
        dvs.append(dv_g)
        dsbs.append(dsb_g.reshape(HPG, NQ * 2 * NQ))
    dtab = _bias_bwd(jnp.pad(jnp.stack(dsbs), ((0, 0), (0, 8 - HPG), (0, 0))), buckets)
    gs["rel_bias_table"] = dtab[:, :HPG].reshape(N_GROUPS * HPG, N_BUCKETS).T
    dz, gs["q_norm_w"], gs["k_norm_w"] = _qkv_bwd(dz, z, dqs, dks, dvs, sp["q_norm_w"], sp["k_norm_w"])
    ex.reduce_end("b", after=gs["q_norm_w"])
    gb["w_in"] = _mm_tn(xn, dz, 4, "dw_in")
    tok = ex.reduce_begin("c", ("w_in",), gb)
    dx, gs["attn_norm_w"] = _norm_in_bwd(dz, bw["w_in"], x, _after(sp["attn_norm_w"], tok), dh1, "in_bwd")
    ex.reduce_end("c", after=gs["attn_norm_w"])
    return loss_tile, dx, gs, gb


SMALL = ("rel_bias_table", "attn_norm_w", "b_gate", "q_norm_w", "k_norm_w", "conv_dw_w", "conv_dw_b", "conv_ln_w", "conv_ln_b",
         "mem_norm_w", "xq_norm_w", "xk_norm_w", "ffn_norm_w", "ffn_conv_w", "ffn_conv_b")
SMALL_SHARDED = ("conv_dw_w", "ffn_conv_w")
BIG_COL = ("w_in", "w_attn_o", "w_conv_o", "w_cross_o", "w_up")
BIG_ROW = ("w_mem_kv", "w_out", "w_down")
BIG = BIG_COL + BIG_ROW
WEIGHTS = ("rel_bias_table", "attn_norm_w", "w_in", "b_gate", "q_norm_w", "k_norm_w", "w_attn_o", "conv_dw_w", "conv_dw_b",
           "conv_ln_w", "conv_ln_b", "w_conv_o", "mem_norm_w", "w_mem_kv", "xq_norm_w", "xk_norm_w", "w_cross_o", "w_out",
           "ffn_norm_w", "w_up", "ffn_conv_w", "ffn_conv_b", "w_down")


class _Exchanges:
    REST = tuple(k for k in BIG if k != "w_in")

    def __init__(self, w):
        self.w = w
        self.pending = {}
        self.reduced = {}

    def _whole(self, k, ga):
        ga = ga.reshape((4,) + self.w[k].shape)
        return ga if k in BIG_COL else ga.reshape((4 * self.w[k].shape[0],) + self.w[k].shape[1:])

    def start_first(self):
        self.w_in_local = self.w["w_in"].astype(BF16)
        local = [_halves(self.w_in_local)]
        for k in SMALL_SHARDED:
            flat = jnp.ravel(self.w[k])
            local.append(jnp.pad(flat, (0, (-flat.shape[0]) % 2048)).reshape(2, -1, 128))
        send_sems, recv_sems, ins, lands, token = _ag_ici_start(local, "gather_first_start")
        self.pending["first"] = (send_sems, recv_sems, ins, lands)
        return token

    def own_w_in(self):
        x, y, _ = _position()
        me = 2 * x + y
        others = jnp.stack([me ^ 1, me ^ 2, me ^ 3]).astype(jnp.int32)
        return self.w_in_local[None], jnp.reshape(me, (1,)).astype(jnp.int32), others

    def first_weights(self, after):
        send_sems, recv_sems, ins, lands = self.pending.pop("first")
        ins, lands = _ag_ici_wait(send_sems, recv_sems, ins, lands, after, "gather_first_wait")
        gathered = _ag_finish(ins, lands, "gather_first_finish")
        self.first = gathered[0]
        small = {}
        for k, ga in zip(SMALL_SHARDED, gathered[1:]):
            r, cdim = self.w[k].shape
            parts = ga.reshape(4, -1)[:, :r * cdim].reshape(4, r, cdim)
            small[k] = jnp.transpose(parts, (1, 0, 2)).reshape(r, 4 * cdim)
        return self._whole("w_in", gathered[0]), small

    def start_rest(self):
        local = [_halves(self.w[k].astype(BF16)) for k in self.REST]
        local, _ = lax.optimization_barrier((local, self.first))
        send_sems, recv_sems, ins, lands, token = _ag_ici_start(local, "gather_rest_start")
        self.pending["rest"] = (send_sems, recv_sems, ins, lands)
        return token

    def rest_weights(self, after):
        send_sems, recv_sems, ins, lands = self.pending.pop("rest")
        ins, lands = _ag_ici_wait(send_sems, recv_sems, ins, lands, after, "gather_rest_wait")
        gathered = _ag_finish(ins, lands, "gather_rest_finish")
        return {k: self._whole(k, ga) for k, ga in zip(self.REST, gathered)}

    def reduce_begin(self, tag, names, gb):
        parts = [gb[k].reshape((4,) + self.w[k].shape) for k in names]
        state, token = _reduce_begin(parts, tag)
        self.pending[tag] = (state, names)
        return token

    def reduce_end(self, tag, after):
        state, names = self.pending.pop(tag)
        self.reduced.update(zip(names, _reduce_end(state, after, tag)))


def _step(x, mem, target, w, m, v):
    xi, yi, _ = _position()
    shard = 2 * xi + yi
    ex = _Exchanges(w)
    sp = {k: w[k] for k in SMALL if k not in SMALL_SHARDED}
    loss_tile, dx, gs, _ = _local_step(x, mem, target, sp, ex)
    g_big = ex.reduced

    red = _all_reduce_small([loss_tile] + [gs[k] for k in SMALL])
    loss = red[0][0, 0]
    g_small = dict(zip(SMALL, red[1:]))
    for k in SMALL_SHARDED:
        cdim = w[k].shape[1]
        g_small[k] = lax.dynamic_slice_in_dim(g_small[k], shard * cdim, cdim, axis=1)

    grads, delta, new_m, new_v = {}, {}, {}, {}
    for k in BIG:
        grads[k] = g_big[k]
        delta[k], new_m[k], new_v[k] = _adamw(w[k], g_big[k], m[k], v[k], "adamw_" + k)
    outs = _adamw_small([w[k] for k in SMALL], [g_small[k] for k in SMALL], [m[k] for k in SMALL], [v[k] for k in SMALL])
    for dst, vals in zip((delta, new_m, new_v), outs):
        dst.update(zip(SMALL, vals))
    grads.update(g_small)
    return loss, dx, grads, delta, new_m, new_v


def kernel(x, mem, rel_bias_table, attn_norm_w, w_in, b_gate, q_norm_w, k_norm_w, w_attn_o, conv_dw_w, conv_dw_b, conv_ln_w, conv_ln_b, w_conv_o, mem_norm_w, w_mem_kv, xq_norm_w, xk_norm_w, w_cross_o, w_out, ffn_norm_w, w_up, ffn_conv_w, ffn_conv_b, w_down, loss_target, m_rel_bias_table, m_attn_norm_w, m_w_in, m_b_gate, m_q_norm_w, m_k_norm_w, m_w_attn_o, m_conv_dw_w, m_conv_dw_b, m_conv_ln_w, m_conv_ln_b, m_w_conv_o, m_mem_norm_w, m_w_mem_kv, m_xq_norm_w, m_xk_norm_w, m_w_cross_o, m_w_out, m_ffn_norm_w, m_w_up, m_ffn_conv_w, m_ffn_conv_b, m_w_down, v_rel_bias_table, v_attn_norm_w, v_w_in, v_b_gate, v_q_norm_w, v_k_norm_w, v_w_attn_o, v_conv_dw_w, v_conv_dw_b, v_conv_ln_w, v_conv_ln_b, v_w_conv_o, v_mem_norm_w, v_w_mem_kv, v_xq_norm_w, v_xk_norm_w, v_w_cross_o, v_w_out, v_ffn_norm_w, v_w_up, v_ffn_conv_w, v_ffn_conv_b, v_w_down):
    args = locals()
    def block(name, k):
        a = args[name] if k == "rel_bias_table" else args[name][0]
        return a.reshape(1, -1) if a.ndim == 1 else a

    w = {k: block(k, k) for k in WEIGHTS}
    m = {k: block("m_" + k, k) for k in WEIGHTS}
    v = {k: block("v_" + k, k) for k in WEIGHTS}
    loss, dx, grads, delta, new_m, new_v = _step(x[0], mem[0], loss_target[0], w, m, v)
    out = [loss, dx[None]]
    for d in (grads, delta, new_m, new_v):
        for k in WEIGHTS:
            out.append(d[k].reshape(args[k].shape))
    return tuple(out)
```

```python
import functools
import math

import numpy as np
import jax
import jax.numpy as jnp
from jax import lax
from jax.experimental import pallas as pl
from jax.experimental.pallas import tpu as pltpu

F32, BF16 = jnp.float32, jnp.bfloat16
SDS = jax.ShapeDtypeStruct
MESH = pl.DeviceIdType.MESH

HEAD = 128
N_GROUPS, HPG = 3, 4
ATTN_GROUPS = ((128, 1), (512, 4), (2048, 16))
NQ = 128
QKV_W = N_GROUPS * HPG * HEAD
CH = 512
CONV_K, FFN_K = 31, 3
N_BUCKETS, MAX_DIST = 32, 2048
RMS_EPS, LN_EPS = 1e-6, 1e-5
O_Q, O_K, O_V, O_CV, O_CG, O_XQ, O_G = 0, QKV_W, 2 * QKV_W, 3 * QKV_W, 3 * QKV_W + CH, 3 * QKV_W + 2 * CH, 3 * QKV_W + 3 * CH
ADAM_LR, ADAM_B1, ADAM_B2, ADAM_EPS, ADAM_WD, ADAM_STEP = 0.001, 0.9, 0.999, 1e-08, 0.01, 10
NEG = -1e30
SCALE = HEAD ** -0.5
TM = 512
MM_TM = 1024
ATT_RB = 2048
NT_DIMS = (((1,), (1,)), ((), ()))
TN_DIMS = (((0,), (0,)), ((), ()))


CONV_RC = 32
CONV_LB = 256
CONV_LANES = tuple(slice(l, l + CONV_LB) for l in range(0, CH, CONV_LB))
CONV_HALO = 32
FFN_RC = 32


def _conv_taps(buf, base, lanes, q_lo, q_hi, visit):
    span = CONV_RC + CONV_HALO
    xx = buf[pl.ds(base, span), lanes]
    for s in range(8):
        xs = xx if s == 0 else pltpu.roll(xx, span - s, 0)
        for q in range(s, q_hi + 1, 8):
            if q >= q_lo:
                visit(q, xs[q - s:q - s + CONV_RC])


def _sig(v):
    return 0.5 * jnp.tanh(0.5 * v) + 0.5


def _fold8(v):
    acc = v[0:8]
    for r in range(8, v.shape[0], 8):
        acc = acc + v[r:r + 8]
    return acc


def _row_tile(rows, cap, mult=8):
    best = None
    for t in range(mult, min(rows, cap) + 1, mult):
        if rows % t == 0:
            best = t
    return best if best is not None else rows


def _full(shape):
    n = len(shape)
    return pl.BlockSpec(shape, lambda *a: (0,) * n)


def _mm_nn(a, b, out_dtype, name):
    M, K = a.shape
    G, _, n = b.shape

    def body(a_ref, b_ref, o_ref):
        o_ref[...] = jnp.dot(a_ref[...].astype(BF16), b_ref[...], preferred_element_type=F32).astype(out_dtype)

    return pl.pallas_call(
        body, name=name, grid=(G, M // MM_TM),
        in_specs=[pl.BlockSpec((MM_TM, K), lambda g, i: (i, 0)), pl.BlockSpec((None, K, n), lambda g, i: (g, 0, 0))],
        out_specs=pl.BlockSpec((MM_TM, n), lambda g, i: (i, g)),
        out_shape=SDS((M, G * n), out_dtype))(a, b)


def _mm_tn(a, b, G, name):
    S, Ka = a.shape
    n = b.shape[1] // G
    tka = Ka
    while tka * n * 4 > 10 * 2 ** 20 and tka % 256 == 0:
        tka //= 2

    def body(a_ref, b_ref, o_ref):
        @pl.when(pl.program_id(2) == 0)
        def _():
            o_ref[...] = jnp.zeros_like(o_ref)
        o_ref[...] += lax.dot_general(a_ref[...].astype(BF16), b_ref[...].astype(BF16), TN_DIMS, preferred_element_type=F32)

    return pl.pallas_call(
        body, name=name, grid=(G, Ka // tka, S // MM_TM),
        in_specs=[pl.BlockSpec((MM_TM, tka), lambda g, i, k: (k, i)), pl.BlockSpec((MM_TM, n), lambda g, i, k: (k, g))],
        out_specs=pl.BlockSpec((None, tka, n), lambda g, i, k: (g, i, 0)),
        out_shape=SDS((G, Ka, n), F32))(a, b)


NORM_RC = 16


def _norm_in_bwd(a, w, xin, nw, resid, name):
    S, K = xin.shape
    G, _, n = w.shape
    tm = 1024 if S % 1024 == 0 and G * K * n * 2 <= 12 * 2 ** 20 else TM
    nt = S // tm

    def body(a_ref, w_ref, x_ref, nw_ref, r_ref, o_ref, dnw_ref, acc, part):
        i, g = pl.program_id(0), pl.program_id(1)

        @pl.when((i == 0) & (g == 0))
        def _():
            part[...] = jnp.zeros_like(part)

        @pl.when(g == 0)
        def _():
            acc[...] = jnp.zeros_like(acc)

        acc[...] += lax.dot_general(a_ref[...], w_ref[g], NT_DIMS, preferred_element_type=F32)

        @pl.when(g == G - 1)
        def _():
            for r0 in range(0, tm, NORM_RC):
                rows = pl.ds(r0, NORM_RC)
                dn = acc[rows, :]
                xv = x_ref[rows, :]
                r = lax.rsqrt(jnp.mean(xv * xv, axis=-1, keepdims=True) + RMS_EPS)
                xhat = xv * r
                dyw = dn * nw_ref[...]
                o_ref[rows, :] = r_ref[rows, :] + r * (dyw - xhat * jnp.mean(dyw * xhat, axis=-1, keepdims=True))
                part[...] += _fold8(dn * xhat)

        @pl.when((i == nt - 1) & (g == G - 1))
        def _():
            dnw_ref[...] = jnp.sum(part[...], axis=0, keepdims=True)

    row = pl.BlockSpec((tm, K), lambda i, g: (i, 0))
    return pl.pallas_call(
        body, name=name, grid=(nt, G),
        in_specs=[pl.BlockSpec((tm, n), lambda i, g: (i, g)),
                  pl.BlockSpec((G, K, n), lambda i, g: (0, 0, 0), pipeline_mode=pl.Buffered(1)),
                  row, _full((1, K)), row],
        out_specs=[row, _full((1, K))],
        out_shape=[SDS((S, K), F32), SDS((1, K), F32)],
        scratch_shapes=[pltpu.VMEM((tm, K), F32), pltpu.VMEM((8, K), F32)])(a, w, xin, nw, resid)


def _t5_bucket_np(dist):
    max_exact = N_BUCKETS // 2
    d = np.maximum(dist.astype(np.float32), np.float32(1.0))
    large = max_exact + (np.log(d / np.float32(max_exact)) / np.float32(math.log(MAX_DIST / max_exact))
                         * np.float32(N_BUCKETS - max_exact)).astype(np.int32)
    large = np.minimum(large, N_BUCKETS - 1)
    return np.where(dist < max_exact, dist, large).astype(np.int32)


def _bias_static():
    qi = np.arange(NQ)[:, None]
    kj = np.arange(2 * NQ)[None, :]
    step = qi + NQ - kj
    band = (step >= 0) & (step <= NQ)
    buckets = np.stack([_t5_bucket_np(np.clip(step, 0, None) * dil).reshape(1, -1) for _, dil in ATTN_GROUPS])
    return band, buckets


def _bias_fwd(table_t, buckets):
    nb = buckets.shape[-1]

    def body(t_ref, b_ref, o_ref):
        oh = (b_ref[...] == lax.broadcasted_iota(jnp.int32, (N_BUCKETS, nb), 0)).astype(F32)
        o_ref[...] = jnp.dot(t_ref[...], oh, preferred_element_type=F32, precision=lax.Precision.HIGHEST)

    return pl.pallas_call(
        body, name="bias_fwd", grid=(N_GROUPS,),
        in_specs=[pl.BlockSpec((None, 8, N_BUCKETS), lambda g: (g, 0, 0)), pl.BlockSpec((None, 1, nb), lambda g: (g, 0, 0))],
        out_specs=pl.BlockSpec((None, 8, nb), lambda g: (g, 0, 0)),
        out_shape=SDS((N_GROUPS, 8, nb), F32))(table_t, buckets)


def _bias_bwd(dsb, buckets):
    nb = buckets.shape[-1]

    def body(d_ref, b_ref, o_ref):
        oh = (b_ref[...] == lax.broadcasted_iota(jnp.int32, (N_BUCKETS, nb), 0)).astype(F32)
        o_ref[...] = lax.dot_general(d_ref[...], oh, NT_DIMS, preferred_element_type=F32, precision=lax.Precision.HIGHEST)

    return pl.pallas_call(
        body, name="bias_bwd", grid=(N_GROUPS,),
        in_specs=[pl.BlockSpec((None, 8, nb), lambda g: (g, 0, 0)), pl.BlockSpec((None, 1, nb), lambda g: (g, 0, 0))],
        out_specs=pl.BlockSpec((None, 8, N_BUCKETS), lambda g: (g, 0, 0)),
        out_shape=SDS((N_GROUPS, 8, N_BUCKETS), F32))(dsb, buckets)


P4 = 4


def _to_p4(dst_ref, h, val, scr):
    scr[...] = val
    for r in range(P4):
        dst_ref[h, r] = scr[pl.ds(r, TM // P4, stride=P4), :]


def _from_p4(src_ref, h, scr):
    for r in range(P4):
        scr[pl.ds(r, TM // P4, stride=P4), :] = src_ref[h, r]
    return scr[...]


def _qkv_prep(z, qw, kw):
    S = z.shape[0]
    nh = N_GROUPS * HPG
    n01 = 2 * HPG

    def body(zq, zk, zv, qw_ref, kw_ref, qh, kh, vh, q2, k2, v2, scr):
        for h in range(nh):
            g = h // HPG
            sl = slice(h * HEAD, (h + 1) * HEAD)
            xq = zq[:, sl].astype(F32)
            q = xq * lax.rsqrt(jnp.mean(xq * xq, axis=-1, keepdims=True) + RMS_EPS) * qw_ref[g:g + 1, :]
            xk = zk[:, sl].astype(F32)
            k = xk * lax.rsqrt(jnp.mean(xk * xk, axis=-1, keepdims=True) + RMS_EPS) * kw_ref[g:g + 1, :]
            v = zv[:, sl].astype(F32)
            if h < n01:
                qh[h], kh[h], vh[h] = q, k, v
            else:
                _to_p4(q2, h - n01, q, scr)
                _to_p4(k2, h - n01, k, scr)
                _to_p4(v2, h - n01, v, scr)

    hm = pl.BlockSpec((n01, TM, HEAD), lambda i: (0, i, 0))
    p4 = pl.BlockSpec((HPG, P4, TM // P4, HEAD), lambda i: (0, 0, i, 0))
    return pl.pallas_call(
        body, name="qkv_prep", grid=(S // TM,),
        in_specs=[pl.BlockSpec((TM, QKV_W), lambda i: (i, 0)), pl.BlockSpec((TM, QKV_W), lambda i: (i, 1)),
                  pl.BlockSpec((TM, QKV_W), lambda i: (i, 2)), _full((N_GROUPS, HEAD)), _full((N_GROUPS, HEAD))],
        out_specs=[hm, hm, hm, p4, p4, p4],
        out_shape=[SDS((n01, S, HEAD), F32)] * 3 + [SDS((HPG, P4, S // P4, HEAD), F32)] * 3,
        scratch_shapes=[pltpu.VMEM((TM, HEAD), F32)])(z, z, z, qw, kw)


def _rows(start, d):
    return pl.ds(start, NQ) if d == 1 else pl.ds(start, NQ, stride=d)


def _attn_fwd(qh, kh, vh, biasm, name, d, head0, nseq, bias_shift):
    S = qh.shape[1]
    RB = min(ATT_RB, S)
    nbk, nq = S // RB, RB // (NQ * d)

    def body(q_ref, k_ref, v_ref, bias_ref, o_ref, lse_ref, kbuf, vbuf):
        b = pl.program_id(1)

        @pl.when(b == 0)
        def _():
            kbuf[0:RB, :] = jnp.zeros((RB, HEAD), F32)
            vbuf[0:RB, :] = jnp.zeros((RB, HEAD), F32)

        @pl.when(b > 0)
        def _():
            kbuf[0:RB, :] = kbuf[RB:2 * RB, :]
            vbuf[0:RB, :] = vbuf[RB:2 * RB, :]

        kbuf[RB:2 * RB, :] = k_ref[...]
        vbuf[RB:2 * RB, :] = v_ref[...]
        bias = bias_ref[...]
        col = lax.broadcasted_iota(jnp.int32, (NQ, 2 * NQ), 1)

        for qb in range(nq):
            def unit(r, carry, qb=qb):
                qs = qb * NQ * d + r
                q = q_ref[_rows(qs, d), :].astype(BF16)
                kw = jnp.concatenate([kbuf[_rows(RB + qs - NQ * d, d), :], kbuf[_rows(RB + qs, d), :]], axis=0).astype(BF16)
                vw = jnp.concatenate([vbuf[_rows(RB + qs - NQ * d, d), :], vbuf[_rows(RB + qs, d), :]], axis=0).astype(BF16)
                s = lax.dot_general(q, kw, NT_DIMS, preferred_element_type=F32) * SCALE + bias
                if qb == 0:
                    s = jnp.where((col < NQ) & (b == 0), NEG, s)
                m = jnp.max(s, axis=-1, keepdims=True)
                p = jnp.exp(s - m)
                l = jnp.sum(p, axis=-1, keepdims=True)
                o = jnp.dot(p.astype(BF16), vw, preferred_element_type=F32) / l
                o_ref[_rows(qs, d), :] = o
                lse_ref[_rows(qs, d), :] = jnp.broadcast_to(m + jnp.log(l), (NQ, HEAD))
                return carry

            for r in range(d):
                unit(r, 0)

    blk = lambda f: pl.BlockSpec((None, RB, HEAD), f)
    return pl.pallas_call(
        body, name=name, grid=(nseq, nbk),
        in_specs=[blk(lambda h, b: (head0 + h, b, 0))] * 3
        + [pl.BlockSpec((None, NQ, 2 * NQ), lambda h, b: (jnp.right_shift(h, bias_shift), 0, 0))],
        out_specs=[blk(lambda h, b: (h, b, 0))] * 2,
        out_shape=[SDS((nseq, S, HEAD), F32)] * 2,
        scratch_shapes=[pltpu.VMEM((2 * RB, HEAD), F32)] * 2)(qh, kh, vh, biasm)


def _attn_bwd(qh, kh, vh, biasm, da, wg, dh, lse, name, d, head0, nseq, bias_shift):
    S = qh.shape[1]
    RB = min(ATT_RB, S)
    nbk, nq = S // RB, RB // (NQ * d)

    def body(q_ref, k_ref, v_ref, bias_ref, da_ref, wg_ref, dh_ref, lse_ref,
             dq_ref, dk_ref, dv_ref, dsb_ref, kbuf, vbuf, dkbuf, dvbuf):
        b = pl.program_id(1)
        zero = jnp.zeros((RB, HEAD), F32)

        @pl.when(b == 0)
        def _():
            kbuf[0:RB, :] = zero
            vbuf[0:RB, :] = zero
            dkbuf[0:RB, :] = zero
            dvbuf[0:RB, :] = zero
            dsb_ref[...] = jnp.zeros_like(dsb_ref)

        @pl.when(b > 0)
        def _():
            kbuf[0:RB, :] = kbuf[RB:2 * RB, :]
            vbuf[0:RB, :] = vbuf[RB:2 * RB, :]
            dkbuf[0:RB, :] = dkbuf[RB:2 * RB, :]
            dvbuf[0:RB, :] = dvbuf[RB:2 * RB, :]

        dkbuf[RB:2 * RB, :] = zero
        dvbuf[RB:2 * RB, :] = zero

        @pl.when(b < nbk)
        def _():
            kbuf[RB:2 * RB, :] = k_ref[...]
            vbuf[RB:2 * RB, :] = v_ref[...]
            bias = bias_ref[...]
            col = lax.broadcasted_iota(jnp.int32, (NQ, 2 * NQ), 1)

            for qb in range(nq):
                def unit(r, carry, qb=qb):
                    qs = qb * NQ * d + r
                    prev, cur = _rows(RB + qs - NQ * d, d), _rows(RB + qs, d)
                    q = q_ref[_rows(qs, d), :].astype(BF16)
                    kw = jnp.concatenate([kbuf[prev, :], kbuf[cur, :]], axis=0).astype(BF16)
                    vw = jnp.concatenate([vbuf[prev, :], vbuf[cur, :]], axis=0).astype(BF16)
                    s = lax.dot_general(q, kw, NT_DIMS, preferred_element_type=F32) * SCALE + bias
                    if qb == 0:
                        s = jnp.where((col < NQ) & (b == 0), NEG, s)
                    p = jnp.exp(s - lse_ref[_rows(qs, d), :][:, 0:1])
                    w = wg_ref[_rows(qs, d), :]
                    do = (da_ref[_rows(qs, d), :] * w).astype(BF16)
                    dp = lax.dot_general(do, vw, NT_DIMS, preferred_element_type=F32)
                    ds = p * (dp - w[:, 0:1] * dh_ref[_rows(qs, d), :][:, 0:1])
                    dsb_ref[...] += ds
                    dsb = ds.astype(BF16)
                    dq_ref[_rows(qs, d), :] = jnp.dot(dsb, kw, preferred_element_type=F32) * SCALE
                    dkw = lax.dot_general(dsb, q, TN_DIMS, preferred_element_type=F32) * SCALE
                    dvw = lax.dot_general(p.astype(BF16), do, TN_DIMS, preferred_element_type=F32)
                    dkbuf[prev, :] += dkw[0:NQ, :]
                    dkbuf[cur, :] += dkw[NQ:2 * NQ, :]
                    dvbuf[prev, :] += dvw[0:NQ, :]
                    dvbuf[cur, :] += dvw[NQ:2 * NQ, :]
                    return carry

                for r in range(d):
                    unit(r, 0)

        dk_ref[...] = dkbuf[done:done + RB, :]
        dv_ref[...] = dvbuf[done:done + RB, :]

    steps, done = (nbk + 1, 0) if nbk > 1 else (1, RB)
    blk = lambda f: pl.BlockSpec((None, RB, HEAD), f)
    cur_g = blk(lambda h, b: (head0 + h, jnp.minimum(b, nbk - 1), 0))
    cur = blk(lambda h, b: (h, jnp.minimum(b, nbk - 1), 0))
    prv = blk(lambda h, b: (h, jnp.maximum(b - 1, 0), 0))
    sq = pl.BlockSpec((None, NQ, 2 * NQ), lambda h, b: (h, 0, 0))
    return pl.pallas_call(
        body, name=name, grid=(nseq, steps),
        in_specs=[cur_g, cur_g, cur_g, pl.BlockSpec((None, NQ, 2 * NQ), lambda h, b: (jnp.right_shift(h, bias_shift), 0, 0)),
                  cur, cur, cur, cur],
        out_specs=[cur, prv, prv, sq],
        out_shape=[SDS((nseq, S, HEAD), F32)] * 3 + [SDS((nseq, NQ, 2 * NQ), F32)],
        scratch_shapes=[pltpu.VMEM((2 * RB, HEAD), F32)] * 4)(qh, kh, vh, biasm, da, wg, dh, lse)


def _merge_weights(l0, l1, l2):
    m = jnp.maximum(jnp.maximum(l0, l1), l2)
    e0, e1, e2 = jnp.exp(l0 - m), jnp.exp(l1 - m), jnp.exp(l2 - m)
    inv = 1.0 / (e0 + e1 + e2)
    return e0 * inv, e1 * inv, e2 * inv


def _merge_fwd(os_, lses):
    S = os_[0].shape[1]

    def body(o0, o1, o2, l0, l1, l2, a_ref, so, sl):
        for h in range(HPG):
            w0, w1, w2 = _merge_weights(l0[h], l1[h], _from_p4(l2, h, sl))
            a_ref[:, h * HEAD:(h + 1) * HEAD] = (w0 * o0[h] + w1 * o1[h] + w2 * _from_p4(o2, h, so)).astype(BF16)

    hm = pl.BlockSpec((HPG, TM, HEAD), lambda i: (0, i, 0))
    p4 = pl.BlockSpec((HPG, P4, TM // P4, HEAD), lambda i: (0, 0, i, 0))
    return pl.pallas_call(
        body, name="merge_fwd", grid=(S // TM,), in_specs=[hm, hm, p4, hm, hm, p4],
        out_specs=pl.BlockSpec((TM, CH), lambda i: (i, 0)),
        out_shape=SDS((S, CH), BF16), scratch_shapes=[pltpu.VMEM((TM, HEAD), F32)] * 2)(*os_, *lses)


def _merge_bwd(dattn, os_, lses):
    S = dattn.shape[0]

    def body(da_ref, o0, o1, o2, l0, l1, l2, w0_ref, w1_ref, w2_ref, dah_ref, dh_ref, dah2_ref, dh2_ref, so, sl):
        for h in range(HPG):
            w = _merge_weights(l0[h], l1[h], _from_p4(l2, h, sl))
            attn = w[0] * o0[h] + w[1] * o1[h] + w[2] * _from_p4(o2, h, so)
            da = da_ref[:, h * HEAD:(h + 1) * HEAD]
            dh = jnp.broadcast_to(jnp.sum(da * attn, axis=-1, keepdims=True), (TM, HEAD))
            w0_ref[h], w1_ref[h] = w[0], w[1]
            dah_ref[h] = da
            dh_ref[h] = dh
            _to_p4(w2_ref, h, w[2], so)
            _to_p4(dah2_ref, h, da, so)
            _to_p4(dh2_ref, h, dh, so)

    hm = pl.BlockSpec((HPG, TM, HEAD), lambda i: (0, i, 0))
    p4 = pl.BlockSpec((HPG, P4, TM // P4, HEAD), lambda i: (0, 0, i, 0))
    nat, perm = SDS((HPG, S, HEAD), F32), SDS((HPG, P4, S // P4, HEAD), F32)
    w0, w1, w2, dah, dh, dah2, dh2 = pl.pallas_call(
        body, name="merge_bwd", grid=(S // TM,),
        in_specs=[pl.BlockSpec((TM, CH), lambda i: (i, 0)), hm, hm, p4, hm, hm, p4],
        out_specs=[hm, hm, p4, hm, hm, p4, p4], out_shape=[nat, nat, perm, nat, nat, perm, perm],
        scratch_shapes=[pltpu.VMEM((TM, HEAD), F32)] * 2)(dattn, *os_, *lses)
    return (w0, dah, dh), (w1, dah, dh), (w2, dah2, dh2)


def _qkv_bwd(dz, z, dqs, dks, dvs, qw, kw):
    S = z.shape[0]
    nh = N_GROUPS * HPG

    def body(dz_in, zq, zk, *refs):
        del dz_in
        dq_refs, dk_refs, dv_refs = refs[0:3], refs[3:6], refs[6:9]
        qw_ref, kw_ref, dz_ref, dqw_ref, dkw_ref, scr = refs[9:]

        @pl.when(pl.program_id(0) == 0)
        def _():
            dqw_ref[...] = jnp.zeros_like(dqw_ref)
            dkw_ref[...] = jnp.zeros_like(dkw_ref)

        def grad(refs3, g, hh):
            return _from_p4(refs3[g], hh, scr) if g == N_GROUPS - 1 else refs3[g][hh]

        def nbwd(xr, dy, wr, dwr, h, off):
            g = h // HPG
            x = xr[:, h * HEAD:(h + 1) * HEAD].astype(F32)
            r = lax.rsqrt(jnp.mean(x * x, axis=-1, keepdims=True) + RMS_EPS)
            xhat = x * r
            dyw = dy * wr[g:g + 1, :]
            dz_ref[:, off + h * HEAD:off + (h + 1) * HEAD] = (
                r * (dyw - xhat * jnp.mean(dyw * xhat, axis=-1, keepdims=True))).astype(BF16)
            dwr[g:g + 1, :] += jnp.sum(dy * xhat, axis=0, keepdims=True)

        for h in range(nh):
            g, hh = h // HPG, h % HPG
            nbwd(zq, grad(dq_refs, g, hh), qw_ref, dqw_ref, h, O_Q)
            nbwd(zk, grad(dk_refs, g, hh), kw_ref, dkw_ref, h, O_K)
            dz_ref[:, O_V + h * HEAD:O_V + (h + 1) * HEAD] = grad(dv_refs, g, hh).astype(BF16)

    hm = pl.BlockSpec((HPG, TM, HEAD), lambda i: (0, i, 0))
    p4 = pl.BlockSpec((HPG, P4, TM // P4, HEAD), lambda i: (0, 0, i, 0))
    return pl.pallas_call(
        body, name="qkv_bwd", grid=(S // TM,),
        in_specs=[pl.BlockSpec(memory_space=pl.ANY), pl.BlockSpec((TM, QKV_W), lambda i: (i, 0)),
                  pl.BlockSpec((TM, QKV_W), lambda i: (i, 1))] + [hm, hm, p4] * 3 + [_full((N_GROUPS, HEAD)), _full((N_GROUPS, HEAD))],
        out_specs=[pl.BlockSpec((TM, 3 * QKV_W), lambda i: (i, 0)), _full((N_GROUPS, HEAD)), _full((N_GROUPS, HEAD))],
        out_shape=[SDS(dz.shape, BF16), SDS((N_GROUPS, HEAD), F32), SDS((N_GROUPS, HEAD), F32)],
        scratch_shapes=[pltpu.VMEM((TM, HEAD), F32)],
        input_output_aliases={0: 0})(dz, z, z, *dqs, *dks, *dvs, qw, kw)


def _conv_fwd(z, cw, cb, lnw, lnb):
    S = z.shape[0]
    H = 32

    def body(zv, zg, cw_ref, cb_ref, lnw_ref, lnb_ref, u1_ref, u3_ref, xbuf):
        i = pl.program_id(0)

        @pl.when(i == 0)
        def _():
            xbuf[0:H, :] = jnp.zeros((H, CH), F32)

        @pl.when(i > 0)
        def _():
            xbuf[0:H, :] = xbuf[TM:TM + H, :]

        xbuf[H:H + TM, :] = zv[...].astype(F32) * _sig(zg[...].astype(F32))
        for r0 in range(0, TM, CONV_RC):
            rows = pl.ds(r0, CONV_RC)
            parts = []
            for lanes in CONV_LANES:
                part = [jnp.broadcast_to(cb_ref[:, lanes], (CONV_RC, CONV_LB))]

                def tap(q, view, part=part, lanes=lanes):
                    part[0] = part[0] + view * cw_ref[q - 2:q - 1, lanes]

                _conv_taps(xbuf, r0, lanes, 2, CONV_K + 1, tap)
                parts.append(part[0])
            acc = jnp.concatenate(parts, axis=1)
            u1_ref[rows, :] = acc
            mu = jnp.mean(acc, axis=-1, keepdims=True)
            xc = acc - mu
            yl = xc * lax.rsqrt(jnp.mean(xc * xc, axis=-1, keepdims=True) + LN_EPS) * lnw_ref[...] + lnb_ref[...]
            u3_ref[rows, :] = (yl * _sig(yl)).astype(BF16)

    row = pl.BlockSpec((TM, CH), lambda i: (i, 0))
    return pl.pallas_call(
        body, name="conv_fwd", grid=(S // TM,),
        in_specs=[pl.BlockSpec((TM, CH), lambda i: (i, O_CV // CH)), pl.BlockSpec((TM, CH), lambda i: (i, O_CG // CH)),
                  _full((CONV_K, CH)), _full((1, CH)), _full((1, CH)), _full((1, CH))],
        out_specs=[row, row], out_shape=[SDS((S, CH), F32), SDS((S, CH), BF16)],
        scratch_shapes=[pltpu.VMEM((TM + H, CH), F32)])(z, z, cw, cb, lnw, lnb)


def _conv_bwd_a(du3, u1, z, lnw, lnb):
    S = z.shape[0]
    H = 32

    nt = S // TM

    def body(du3_ref, u1_ref, zv, zg, lnw_ref, lnb_ref, du1_ref, acc_ref, xbuf, tacc):
        i = pl.program_id(0)

        @pl.when(i == 0)
        def _():
            xbuf[0:H, :] = jnp.zeros((H, CH), F32)
            tacc[...] = jnp.zeros_like(tacc)

        @pl.when(i > 0)
        def _():
            xbuf[0:H, :] = xbuf[TM:TM + H, :]

        xbuf[H:H + TM, :] = zv[...].astype(F32) * _sig(zg[...].astype(F32))
        for r0 in range(0, TM, CONV_RC):
            rows = pl.ds(r0, CONV_RC)
            u1 = u1_ref[rows, :]
            mu = jnp.mean(u1, axis=-1, keepdims=True)
            xc = u1 - mu
            r = lax.rsqrt(jnp.mean(xc * xc, axis=-1, keepdims=True) + LN_EPS)
            yhat = xc * r
            yl = yhat * lnw_ref[...] + lnb_ref[...]
            sg = _sig(yl)
            dyl = du3_ref[rows, :] * (sg * (1.0 + yl * (1.0 - sg)))
            dyh = dyl * lnw_ref[...]
            du1 = r * (dyh - jnp.mean(dyh, axis=-1, keepdims=True) - yhat * jnp.mean(dyh * yhat, axis=-1, keepdims=True))
            du1_ref[rows, :] = du1
            tacc[33] += _fold8(dyl * yhat)
            tacc[34] += _fold8(dyl)
            tacc[32] += _fold8(du1)
            for lanes in CONV_LANES:
                d = du1[:, lanes]

                def tap(q, view, d=d, lanes=lanes):
                    tacc[q - 2, :, lanes] += _fold8(d * view)

                _conv_taps(xbuf, r0, lanes, 2, CONV_K + 1, tap)

        @pl.when(i == nt - 1)
        def _():
            for k in range(40):
                acc_ref[k:k + 1, :] = jnp.sum(tacc[k], axis=0, keepdims=True)

    row = pl.BlockSpec((TM, CH), lambda i: (i, 0))
    return pl.pallas_call(
        body, name="conv_bwd_a", grid=(nt,),
        in_specs=[row, row, pl.BlockSpec((TM, CH), lambda i: (i, O_CV // CH)), pl.BlockSpec((TM, CH), lambda i: (i, O_CG // CH)),
                  _full((1, CH)), _full((1, CH))],
        out_specs=[row, _full((40, CH))], out_shape=[SDS((S, CH), F32), SDS((40, CH), F32)],
        scratch_shapes=[pltpu.VMEM((TM + H, CH), F32), pltpu.VMEM((40, 8, CH), F32)])(du3, u1, z, z, lnw, lnb)


def _conv_bwd_b(dz, du1, z, cw):
    S = z.shape[0]
    nt = S // TM
    H = 32

    def body(dz_in, du1_ref, zv, zg, cw_ref, dz_ref, ybuf, dgate):
        del dz_in
        i, p = pl.program_id(0), pl.program_id(1)

        @pl.when(p == 0)
        def _():
            @pl.when(i == 0)
            def _():
                ybuf[TM:TM + H, :] = jnp.zeros((H, CH), F32)

            @pl.when(i > 0)
            def _():
                ybuf[TM:TM + H, :] = ybuf[0:H, :]

            ybuf[0:TM, :] = du1_ref[...]
            for r0 in range(0, TM, CONV_RC):
                rows = pl.ds(r0, CONV_RC)
                parts = []
                for lanes in CONV_LANES:
                    part = [jnp.zeros((CONV_RC, CONV_LB), F32)]

                    def tap(q, view, part=part, lanes=lanes):
                        part[0] = part[0] + view * cw_ref[CONV_K - 1 - q:CONV_K - q, lanes]

                    _conv_taps(ybuf, r0, lanes, 0, CONV_K - 1, tap)
                    parts.append(part[0])
                acc = jnp.concatenate(parts, axis=1)
                val = zv[rows, :].astype(F32)
                sg = _sig(zg[rows, :].astype(F32))
                dz_ref[rows, :] = (acc * sg).astype(BF16)
                dgate[rows, :] = (acc * val * sg * (1.0 - sg)).astype(BF16)

        @pl.when(p == 1)
        def _():
            dz_ref[...] = dgate[...]

    rev = lambda c: pl.BlockSpec((TM, CH), lambda i, p: (nt - 1 - i, c))
    return pl.pallas_call(
        body, name="conv_bwd_b", grid=(nt, 2),
        in_specs=[pl.BlockSpec(memory_space=pl.ANY), rev(0), rev(O_CV // CH), rev(O_CG // CH), _full((CONV_K, CH))],
        out_specs=pl.BlockSpec((TM, CH), lambda i, p: (nt - 1 - i, O_CV // CH + p)),
        out_shape=SDS(dz.shape, BF16),
        scratch_shapes=[pltpu.VMEM((TM + H, CH), F32), pltpu.VMEM((TM, CH), BF16)],
        input_output_aliases={0: 0})(dz, du1, z, z, cw)


def _memkv_fwd(mem, mnw, wkv, xkw):
    M, D = mem.shape

    def body(mem_ref, mnw_ref, w_ref, xkw_ref, mn_ref, kv_ref, mk_ref, mv_ref):
        x = mem_ref[...]
        mn = (x * lax.rsqrt(jnp.mean(x * x, axis=-1, keepdims=True) + RMS_EPS) * mnw_ref[...]).astype(BF16)
        mn_ref[...] = mn
        kv = jnp.dot(mn, w_ref[...], preferred_element_type=F32)
        kv_ref[...] = kv
        for h in range(HPG):
            k = kv[:, h * HEAD:(h + 1) * HEAD]
            mk_ref[:, h * HEAD:(h + 1) * HEAD] = (
                k * lax.rsqrt(jnp.mean(k * k, axis=-1, keepdims=True) + RMS_EPS) * xkw_ref[...]).astype(BF16)
        mv_ref[...] = kv[:, CH:2 * CH].astype(BF16)

    return pl.pallas_call(
        body, name="memkv_fwd",
        out_shape=[SDS((M, D), BF16), SDS((M, 2 * CH), F32), SDS((M, CH), BF16), SDS((M, CH), BF16)])(mem, mnw, wkv, xkw)


def _cross_q(zx, xqw, h):
    x = zx[:, h * HEAD:(h + 1) * HEAD].astype(F32)
    r = lax.rsqrt(jnp.mean(x * x, axis=-1, keepdims=True) + RMS_EPS)
    xhat = x * r
    return xhat, r, xhat * xqw


def _cross_fwd(z, xqw, mk, mv):
    S = z.shape[0]
    M = mk.shape[0]

    def body(zx, xqw_ref, mk_ref, mv_ref, o_ref):
        for h in range(HPG):
            sl = slice(h * HEAD, (h + 1) * HEAD)
            _, _, q = _cross_q(zx, xqw_ref[...], h)
            s = lax.dot_general(q.astype(BF16), mk_ref[:, sl], NT_DIMS, preferred_element_type=F32) * SCALE
            e = jnp.exp(s - jnp.max(s, axis=-1, keepdims=True))
            p = e / jnp.sum(e, axis=-1, keepdims=True)
            o_ref[:, sl] = jnp.dot(p.astype(BF16), mv_ref[:, sl], preferred_element_type=F32).astype(BF16)

    return pl.pallas_call(
        body, name="cross_fwd", grid=(S // TM,),
        in_specs=[pl.BlockSpec((TM, CH), lambda i: (i, O_XQ // CH)), _full((1, HEAD)), _full((M, CH)), _full((M, CH))],
        out_specs=pl.BlockSpec((TM, CH), lambda i: (i, 0)), out_shape=SDS((S, CH), BF16))(z, xqw, mk, mv)


def _cross_bwd(dz, doc, z, xqw, mk, mv):
    S = z.shape[0]
    M = mk.shape[0]

    def body(dz_in, do_ref, zx, xqw_ref, mk_ref, mv_ref, dz_ref, dmk_ref, dmv_ref, dxw_ref):
        del dz_in

        @pl.when(pl.program_id(0) == 0)
        def _():
            dmk_ref[...] = jnp.zeros_like(dmk_ref)
            dmv_ref[...] = jnp.zeros_like(dmv_ref)
            dxw_ref[...] = jnp.zeros_like(dxw_ref)

        for h in range(HPG):
            sl = slice(h * HEAD, (h + 1) * HEAD)
            xhat, r, q = _cross_q(zx, xqw_ref[...], h)
            qb = q.astype(BF16)
            s = lax.dot_general(qb, mk_ref[:, sl], NT_DIMS, preferred_element_type=F32) * SCALE
            e = jnp.exp(s - jnp.max(s, axis=-1, keepdims=True))
            p = e / jnp.sum(e, axis=-1, keepdims=True)
            do = do_ref[:, sl].astype(BF16)
            dp = lax.dot_general(do, mv_ref[:, sl], NT_DIMS, preferred_element_type=F32)
            ds = (p * (dp - jnp.sum(p * dp, axis=-1, keepdims=True)) * SCALE).astype(BF16)
            dq = jnp.dot(ds, mk_ref[:, sl], preferred_element_type=F32)
            dmk_ref[:, sl] += lax.dot_general(ds, qb, TN_DIMS, preferred_element_type=F32)
            dmv_ref[:, sl] += lax.dot_general(p.astype(BF16), do, TN_DIMS, preferred_element_type=F32)
            dyw = dq * xqw_ref[...]
            dz_ref[:, sl] = (r * (dyw - xhat * jnp.mean(dyw * xhat, axis=-1, keepdims=True))).astype(BF16)
            dxw_ref[...] += jnp.sum(dq * xhat, axis=0, keepdims=True)

    return pl.pallas_call(
        body, name="cross_bwd", grid=(S // TM,),
        in_specs=[pl.BlockSpec(memory_space=pl.ANY), pl.BlockSpec((TM, CH), lambda i: (i, 0)),
                  pl.BlockSpec((TM, CH), lambda i: (i, O_XQ // CH)), _full((1, HEAD)), _full((M, CH)), _full((M, CH))],
        out_specs=[pl.BlockSpec((TM, CH), lambda i: (i, O_XQ // CH)), _full((M, CH)), _full((M, CH)), _full((1, HEAD))],
        out_shape=[SDS(dz.shape, BF16), SDS((M, CH), F32), SDS((M, CH), F32), SDS((1, HEAD), F32)],
        input_output_aliases={0: 0})(dz, doc, z, xqw, mk, mv)


def _memkv_bwd(dmk, dmv, kv, mem, mn, mnw, wkv, xkw):
    M, D = mem.shape

    def body(dmk_ref, dmv_ref, kv_ref, mem_ref, mn_ref, mnw_ref, w_ref, xkw_ref, dw_ref, dxk_ref, dmn_ref, dkv):
        dxk = jnp.zeros((1, HEAD), F32)
        for h in range(HPG):
            sl = slice(h * HEAD, (h + 1) * HEAD)
            k = kv_ref[:, sl]
            r = lax.rsqrt(jnp.mean(k * k, axis=-1, keepdims=True) + RMS_EPS)
            khat = k * r
            dy = dmk_ref[:, sl]
            dyw = dy * xkw_ref[...]
            dkv[:, sl] = (r * (dyw - khat * jnp.mean(dyw * khat, axis=-1, keepdims=True))).astype(BF16)
            dxk = dxk + jnp.sum(dy * khat, axis=0, keepdims=True)
        dxk_ref[...] = dxk
        dkv[:, CH:2 * CH] = dmv_ref[...].astype(BF16)
        dw_ref[...] = lax.dot_general(mn_ref[...], dkv[...], TN_DIMS, preferred_element_type=F32)
        dn = lax.dot_general(dkv[...], w_ref[...], NT_DIMS, preferred_element_type=F32)
        x = mem_ref[...]
        r = lax.rsqrt(jnp.mean(x * x, axis=-1, keepdims=True) + RMS_EPS)
        dmn_ref[...] = jnp.sum(dn * x * r, axis=0, keepdims=True)

    return pl.pallas_call(
        body, name="memkv_bwd",
        out_shape=[SDS((D, 2 * CH), F32), SDS((1, HEAD), F32), SDS((1, D), F32)],
        scratch_shapes=[pltpu.VMEM((M, 2 * CH), BF16)])(dmk, dmv, kv, mem, mn, mnw, wkv, xkw)


def _branch_proj(a_ref, w_ref, y_ref):
    G, _, n = w_ref.shape
    a = a_ref[...]
    for g in range(G):
        y_ref[:, g * n:(g + 1) * n] = jnp.dot(a, w_ref[g], preferred_element_type=F32)


OUT_RC = 16


def _gates(zg_ref, bg_ref, rows, k, D):
    return _sig(zg_ref[rows, k * D:(k + 1) * D].astype(F32) + bg_ref[:, k * D:(k + 1) * D])


def _outproj_fwd(x, z, bg, attn, u3, oc, wao, wco, wxo, wout, fnw):
    S, D = x.shape
    tm = TM

    def body(x_ref, zg_ref, bg_ref, a_ref, u_ref, c_ref, wa, wc, wx, wo, fnw_ref, h1_ref, hn_ref, ya, yc, yx, mg):
        _branch_proj(a_ref, wa, ya)
        _branch_proj(u_ref, wc, yc)
        _branch_proj(c_ref, wx, yx)
        for r0 in range(0, tm, OUT_RC):
            rows = pl.ds(r0, OUT_RC)
            mg[rows, :] = (_gates(zg_ref, bg_ref, rows, 0, D) * ya[rows, :] + _gates(zg_ref, bg_ref, rows, 1, D) * yc[rows, :]
                           + _gates(zg_ref, bg_ref, rows, 2, D) * yx[rows, :]).astype(BF16)
        ya[...] = jnp.dot(mg[...], wo[...], preferred_element_type=F32)
        for r0 in range(0, tm, OUT_RC):
            rows = pl.ds(r0, OUT_RC)
            h1 = x_ref[rows, :] + ya[rows, :]
            h1_ref[rows, :] = h1
            hn_ref[rows, :] = (h1 * lax.rsqrt(jnp.mean(h1 * h1, axis=-1, keepdims=True) + RMS_EPS) * fnw_ref[...]).astype(BF16)

    row = lambda w: pl.BlockSpec((tm, w), lambda i: (i, 0))
    return pl.pallas_call(
        body, name="outproj_fwd", grid=(S // tm,),
        in_specs=[row(D), pl.BlockSpec((tm, 3 * D), lambda i: (i, O_G // (3 * D))), _full((1, 3 * D)), row(CH), row(CH), row(CH),
                  _full(wao.shape), _full(wco.shape), _full(wxo.shape), _full((D, D)), _full((1, D))],
        out_specs=[row(D), row(D)], out_shape=[SDS((S, D), F32), SDS((S, D), BF16)],
        scratch_shapes=[pltpu.VMEM((tm, D), F32)] * 3 + [pltpu.VMEM((tm, D), BF16)])(x, z, bg, attn, u3, oc, wao, wco, wxo, wout, fnw)


def _outproj_bwd(dh1, z, bg, attn, u3, oc, wao, wco, wxo, wout, n_in):
    S, D = dh1.shape
    tm = 256
    nt = S // tm
    G, _, n = wao.shape

    def body(dh_ref, zg_ref, bg_ref, a_ref, u_ref, c_ref, wa, wc, wx, wo,
             dz_ref, da_ref, du_ref, dc_ref, dbg_ref, dwo_ref, dwa_ref, dwc_ref, dwx_ref,
             ya, yc, yx, dm, dy, mg, bacc, wacc):
        i = pl.program_id(0)

        @pl.when(i == 0)
        def _():
            bacc[...] = jnp.zeros_like(bacc)
            wacc[...] = jnp.zeros_like(wacc)
            dwo_ref[...] = jnp.zeros_like(dwo_ref)

        _branch_proj(a_ref, wa, ya)
        _branch_proj(u_ref, wc, yc)
        _branch_proj(c_ref, wx, yx)
        dhb = dh_ref[...].astype(BF16)
        dm[...] = lax.dot_general(dhb, wo[...], NT_DIMS, preferred_element_type=F32)
        for r0 in range(0, tm, OUT_RC):
            rows = pl.ds(r0, OUT_RC)
            dmv = dm[rows, :]
            merged = jnp.zeros((OUT_RC, D), F32)
            for k, y in enumerate((ya, yc, yx)):
                gk = _gates(zg_ref, bg_ref, rows, k, D)
                yk = y[rows, :]
                merged = merged + gk * yk
                dzg = dmv * yk * gk * (1.0 - gk)
                dz_ref[rows, k * D:(k + 1) * D] = dzg.astype(BF16)
                bacc[:, k * D:(k + 1) * D] += _fold8(dzg)
                dy[k, rows, :] = (dmv * gk).astype(BF16)
            mg[rows, :] = merged.astype(BF16)
        dwo_ref[...] += lax.dot_general(mg[...], dhb, TN_DIMS, preferred_element_type=F32)
        for k, (b_ref, w_ref, db_ref) in enumerate(((a_ref, wa, da_ref), (u_ref, wc, du_ref), (c_ref, wx, dc_ref))):
            dyk = dy[k]
            acc = jnp.zeros((tm, CH), F32)
            for g in range(G):
                acc = acc + lax.dot_general(dyk[:, g * n:(g + 1) * n], w_ref[g], NT_DIMS, preferred_element_type=F32)
            db_ref[...] = acc
            wacc[k] += lax.dot_general(b_ref[...], dyk, TN_DIMS, preferred_element_type=F32)

        @pl.when(i == nt - 1)
        def _():
            dbg_ref[...] = jnp.sum(bacc[...], axis=0, keepdims=True)
            for k, dw_ref in enumerate((dwa_ref, dwc_ref, dwx_ref)):
                for g in range(G):
                    dw_ref[g] = wacc[k, :, g * n:(g + 1) * n]

    row = lambda w: pl.BlockSpec((tm, w), lambda i: (i, 0))
    return pl.pallas_call(
        body, name="outproj_bwd", grid=(nt,),
        in_specs=[row(D), pl.BlockSpec((tm, 3 * D), lambda i: (i, O_G // (3 * D))), _full((1, 3 * D)), row(CH), row(CH), row(CH),
                  _full(wao.shape), _full(wco.shape), _full(wxo.shape), _full((D, D))],
        out_specs=[pl.BlockSpec((tm, 3 * D), lambda i: (i, O_G // (3 * D))), row(CH), row(CH), row(CH), _full((1, 3 * D)),
                   _full((D, D))] + [_full(wao.shape)] * 3,
        out_shape=[SDS((S, n_in), BF16)] + [SDS((S, CH), F32)] * 3 + [SDS((1, 3 * D), F32), SDS((D, D), F32)]
        + [SDS(wao.shape, F32)] * 3,
        scratch_shapes=[pltpu.VMEM((tm, D), F32)] * 4 + [pltpu.VMEM((3, tm, D), BF16), pltpu.VMEM((tm, D), BF16),
                                                        pltpu.VMEM((8, 3 * D), F32), pltpu.VMEM((3, CH, D), F32)],
    )(dh1, z, bg, attn, u3, oc, wao, wco, wxo, wout)


FFN_TC = 256
FFN_H = 8


def _ffn_taps(buf, r0):
    xx = buf[pl.ds(r0, FFN_RC + FFN_H), :]
    return xx[FFN_H:], pltpu.roll(xx, 1, 0)[FFN_H:], pltpu.roll(xx, 2, 0)[FFN_H:]


def _ffn_conv(taps, w_ref, b_ref):
    x0, x1, x2 = taps
    return b_ref[...] + x0 * w_ref[2:3, :] + x1 * w_ref[1:2, :] + x2 * w_ref[0:1, :]


def _ffn_fwd(up, cw, cb, wdown, h1, target):
    S, D = h1.shape
    F2 = up.shape[1]
    F = F2 // 2
    nj = F // FFN_TC
    tm = TM

    def body(up_ref, halo_ref, cw_ref, cb_ref, w_ref, h_ref, t_ref, dy_ref, loss_ref, ac_ref, gc_ref, abuf, gbuf, act_s):
        i = pl.program_id(0)

        @pl.when(i == 0)
        def _():
            loss_ref[...] = jnp.zeros_like(loss_ref)

        for j in range(nj):
            ca, cg = slice(j * FFN_TC, (j + 1) * FFN_TC), slice(F + j * FFN_TC, F + (j + 1) * FFN_TC)
            first = i == 0
            abuf[0:FFN_H, :] = jnp.where(first, 0.0, halo_ref[:, ca].astype(F32))
            gbuf[0:FFN_H, :] = jnp.where(first, 0.0, halo_ref[:, cg].astype(F32))
            abuf[FFN_H:FFN_H + tm, :] = up_ref[:, ca].astype(F32)
            gbuf[FFN_H:FFN_H + tm, :] = up_ref[:, cg].astype(F32)
            for r0 in range(0, tm, FFN_RC):
                rows = pl.ds(r0, FFN_RC)
                a = _ffn_conv(_ffn_taps(abuf, r0), cw_ref[:, ca], cb_ref[:, ca])
                gt = _ffn_conv(_ffn_taps(gbuf, r0), cw_ref[:, cg], cb_ref[:, cg])
                ac_ref[rows, ca] = a.astype(BF16)
                gc_ref[rows, ca] = gt.astype(BF16)
                act_s[rows, ca] = (gt * _sig(gt) * a).astype(BF16)
        err = h_ref[...] + jnp.dot(act_s[...], w_ref[...], preferred_element_type=F32) - t_ref[...]
        dy_ref[...] = err * (1.0 / D)
        loss_ref[...] += 0.5 * jnp.sum(jnp.mean(err * err, axis=-1, keepdims=True))

    row = lambda w: pl.BlockSpec((tm, w), lambda i: (i, 0))
    halo = pl.BlockSpec((FFN_H, F2), lambda i: (jnp.maximum(i * (tm // FFN_H) - 1, 0), 0))
    return pl.pallas_call(
        body, name="ffn_fwd", grid=(S // tm,),
        in_specs=[row(F2), halo, _full((FFN_K, F2)), _full((1, F2)), _full((F, D)), row(D), row(D)],
        out_specs=[row(D), _full((8, 128)), row(F), row(F)],
        out_shape=[SDS((S, D), F32), SDS((8, 128), F32), SDS((S, F), BF16), SDS((S, F), BF16)],
        scratch_shapes=[pltpu.VMEM((tm + FFN_H, FFN_TC), F32)] * 2 + [pltpu.VMEM((tm, F), BF16)])(up, up, cw, cb, wdown, h1, target)


def _ffn_bwd_a(dy, wdown, ac, gc):
    S, D = dy.shape
    F = ac.shape[1]
    nj = F // FFN_TC
    tm = 1024 if S % 1024 == 0 else TM

    def body(dy_ref, wd_ref, a_ref, g_ref, da_ref, dg_ref, acca_ref, accg_ref, dwd_ref, dact_s, act_s):
        i, j = pl.program_id(0), pl.program_id(1)

        @pl.when((i == 0) & (j == 0))
        def _():
            acca_ref[...] = jnp.zeros_like(acca_ref)
            accg_ref[...] = jnp.zeros_like(accg_ref)
            dwd_ref[...] = jnp.zeros_like(dwd_ref)

        dyb = dy_ref[...].astype(BF16)
        dact_s[...] = lax.dot_general(dyb, wd_ref[...], NT_DIMS, preferred_element_type=F32)
        pa = pg = jnp.zeros((8, FFN_TC), F32)
        for r0 in range(0, tm, FFN_RC):
            rows = pl.ds(r0, FFN_RC)
            a = a_ref[rows, :].astype(F32)
            gt = g_ref[rows, :].astype(F32)
            dact = dact_s[rows, :]
            sg = _sig(gt)
            silu = gt * sg
            act_s[rows, :] = (silu * a).astype(BF16)
            dac = dact * silu
            dgc = dact * a * (sg * (1.0 + gt * (1.0 - sg)))
            da_ref[rows, :] = dac.astype(BF16)
            dg_ref[rows, :] = dgc.astype(BF16)
            pa = pa + _fold8(dac)
            pg = pg + _fold8(dgc)
        acca_ref[j] += pa
        accg_ref[j] += pg
        dwd_ref[pl.ds(pl.multiple_of(j * FFN_TC, FFN_TC), FFN_TC), :] += lax.dot_general(
            act_s[...], dyb, TN_DIMS, preferred_element_type=F32)

    col = pl.BlockSpec((tm, FFN_TC), lambda i, j: (i, j))
    return pl.pallas_call(
        body, name="ffn_bwd_a", grid=(S // tm, nj),
        in_specs=[pl.BlockSpec((tm, D), lambda i, j: (i, 0)), pl.BlockSpec((FFN_TC, D), lambda i, j: (j, 0)), col, col],
        out_specs=[col, col] + [_full((nj, 8, FFN_TC))] * 2 + [_full((F, D))],
        out_shape=[SDS((S, F), BF16)] * 2 + [SDS((nj, 8, FFN_TC), F32)] * 2 + [SDS((F, D), F32)],
        scratch_shapes=[pltpu.VMEM((tm, FFN_TC), F32), pltpu.VMEM((tm, FFN_TC), BF16)])(dy, wdown, ac, gc)


def _ffn_bwd_b(dca, dcg, up, cw):
    S, F = dca.shape
    nj = F // FFN_TC
    tm = 4096 if S % 4096 == 0 else TM
    nt = S // tm
    span = FFN_RC + FFN_H

    def body(a_ref, g_ref, u_ref, w_ref, o_ref, tacc_ref, ybuf):
        j, i = pl.program_id(0), pl.program_id(1)

        @pl.when(i == 0)
        def _():
            ybuf[tm:tm + FFN_H, :] = jnp.zeros((FFN_H, FFN_TC), F32)
            tacc_ref[...] = jnp.zeros_like(tacc_ref)

        @pl.when(i > 0)
        def _():
            ybuf[tm:tm + FFN_H, :] = ybuf[0:FFN_H, :]

        ybuf[0:tm, :] = jnp.where(j < nj, a_ref[...], g_ref[...]).astype(F32)
        p = [jnp.zeros((8, FFN_TC), F32)] * FFN_K
        for r0 in range(0, tm, FFN_RC):
            rows = pl.ds(r0, FFN_RC)
            yy = ybuf[pl.ds(r0, span), :]
            ys = (yy[:FFN_RC], pltpu.roll(yy, span - 1, 0)[:FFN_RC], pltpu.roll(yy, span - 2, 0)[:FFN_RC])
            o_ref[rows, :] = (ys[0] * w_ref[2:3, :] + ys[1] * w_ref[1:2, :] + ys[2] * w_ref[0:1, :]).astype(BF16)
            u = u_ref[rows, :].astype(F32)
            for k in range(FFN_K):
                p[k] = p[k] + _fold8(ys[FFN_K - 1 - k] * u)
        for k in range(FFN_K):
            tacc_ref[k] += p[k]

    rev = lambda f: pl.BlockSpec((tm, FFN_TC), lambda j, i: (nt - 1 - i, f(j)))
    return pl.pallas_call(
        body, name="ffn_bwd_b", grid=(2 * nj, nt),
        in_specs=[rev(lambda j: jnp.minimum(j, nj - 1)), rev(lambda j: jnp.maximum(j - nj, 0)), rev(lambda j: j),
                  pl.BlockSpec((FFN_K, FFN_TC), lambda j, i: (0, j))],
        out_specs=[rev(lambda j: j), pl.BlockSpec((None, FFN_K, 8, FFN_TC), lambda j, i: (j, 0, 0, 0))],
        out_shape=[SDS((S, 2 * F), BF16), SDS((2 * nj, FFN_K, 8, FFN_TC), F32)],
        scratch_shapes=[pltpu.VMEM((tm + FFN_H, FFN_TC), F32)])(dca, dcg, up, cw)


def _adamw_update(w_ref, g_ref, m_ref, v_ref, d_ref, nm_ref, nv_ref):
    gv = g_ref[...]
    m2 = ADAM_B1 * m_ref[...] + (1.0 - ADAM_B1) * gv
    v2 = ADAM_B2 * v_ref[...] + (1.0 - ADAM_B2) * jnp.square(gv)
    m_hat = m2 / (1.0 - ADAM_B1 ** ADAM_STEP)
    v_hat = v2 / (1.0 - ADAM_B2 ** ADAM_STEP)
    d_ref[...] = -ADAM_LR * (m_hat / (jnp.sqrt(v_hat) + ADAM_EPS) + ADAM_WD * w_ref[...])
    nm_ref[...] = m2
    nv_ref[...] = v2


def _adamw_small(ws, gs, ms, vs):
    n = len(ws)

    def body(*refs):
        for i in range(n):
            _adamw_update(*[refs[k * n + i] for k in range(7)])

    shapes = [SDS(w.shape, F32) for w in ws]
    res = pl.pallas_call(body, name="adamw_small", out_shape=shapes * 3)(*ws, *gs, *ms, *vs)
    return res[:n], res[n:2 * n], res[2 * n:]


def _adamw(w, g, m, v, name):
    R, C = w.shape
    tr = _row_tile(R, max(8, (2 ** 20) // (4 * C) // 8 * 8))

    def body(w_ref, g_ref, m_ref, v_ref, d_ref, nm_ref, nv_ref):
        _adamw_update(w_ref, g_ref, m_ref, v_ref, d_ref, nm_ref, nv_ref)

    blk = pl.BlockSpec((tr, C), lambda i: (i, 0))
    return pl.pallas_call(
        body, name=name, grid=(R // tr,), in_specs=[blk] * 4, out_specs=[blk] * 3,
        out_shape=[SDS((R, C), F32)] * 3)(w, g, m, v)


HBM_SPEC = pl.BlockSpec(memory_space=pltpu.HBM)
SEM_SPEC = pl.BlockSpec(memory_space=pltpu.SEMAPHORE)
DATAFLOW_EFFECT = pltpu.SideEffectType.DATAFLOW_SIDE_EFFECTING


def _position():
    return lax.axis_index("x"), lax.axis_index("y"), lax.axis_index("c")


def _other_chips(x, y):
    return [(1 - x, y), (x, 1 - y), (1 - x, 1 - y)]


def _all_gather_xy(arrs, name):
    n = len(arrs)
    hbm = pl.BlockSpec(memory_space=pl.ANY)

    def body(*refs):
        ins, outs = refs[:n], refs[n:2 * n]
        send_sems, recv_sems = refs[2 * n:]
        x, y, c = _position()
        me = 2 * x + y
        chips = _other_chips(x, y)

        def rcopy(i, k, src, dst, to):
            return pltpu.make_async_remote_copy(src_ref=src, dst_ref=dst, send_sem=send_sems.at[i, k], recv_sem=recv_sems.at[i, k],
                                                device_id=to, device_id_type=MESH)

        sends = []
        for i in range(n):
            own = rcopy(i, 6, ins[i], outs[i].at[me], (x, y, 1 - c))
            own.start()
            sends.append(own)
        for i in range(n):
            for j, (px, py) in enumerate(chips):
                cp = rcopy(i, j, ins[i].at[c], outs[i].at[me, c], (px, py, c))
                cp.start()
                sends.append(cp)
        for i in range(n):
            for j, (px, py) in enumerate(chips):
                got = outs[i].at[2 * px + py, c]
                rcopy(i, j, ins[i].at[c], got, (x, y, c)).wait_recv()
                fwd = rcopy(i, 3 + j, got, got, (x, y, 1 - c))
                fwd.start()
                sends.append(fwd)
        for i in range(n):
            for j, (px, py) in enumerate(chips):
                theirs = outs[i].at[2 * px + py, 1 - c]
                rcopy(i, 3 + j, theirs, theirs, (x, y, c)).wait_recv()
        for i in range(n):
            rcopy(i, 6, ins[i], outs[i].at[me], (x, y, c)).wait_recv()
        for cp in sends:
            cp.wait_send()

    return pl.pallas_call(
        body, name=name, in_specs=[hbm] * n, out_specs=[hbm] * n,
        out_shape=[SDS((4,) + a.shape, a.dtype) for a in arrs],
        scratch_shapes=[pltpu.SemaphoreType.DMA((n, 7)), pltpu.SemaphoreType.DMA((n, 7))])(*arrs)


def _ag_ici_start(arrs, name):
    n = len(arrs)

    def body(*refs):
        ins, lands = refs[:n], refs[n:2 * n]
        send_sems, recv_sems = refs[2 * n:2 * n + 2]
        token = refs[-1]
        x, y, c = _position()
        for i in range(n):
            for j, (px, py) in enumerate(_other_chips(x, y)):
                pltpu.make_async_remote_copy(src_ref=ins[i].at[c], dst_ref=lands[i].at[2 * x + y, c], send_sem=send_sems.at[3 * i + j],
                                             recv_sem=recv_sems.at[3 * i + j], device_id=(px, py, c), device_id_type=MESH).start()
        token[...] = jnp.zeros_like(token)

    lands = [lax.empty((4,) + a.shape, a.dtype) for a in arrs]
    res = pl.pallas_call(
        body, name=name,
        out_shape=[pltpu.SemaphoreType.DMA((3 * n,)), pltpu.SemaphoreType.DMA((3 * n,))]
        + [pltpu.HBM(a.shape, a.dtype) for a in arrs] + [pltpu.HBM(l.shape, l.dtype) for l in lands] + [SDS((8, 128), F32)],
        in_specs=[HBM_SPEC] * (2 * n), out_specs=[SEM_SPEC, SEM_SPEC] + [HBM_SPEC] * (2 * n) + [pl.BlockSpec(memory_space=pltpu.VMEM)],
        input_output_aliases={i: 2 + i for i in range(2 * n)},
        compiler_params=pltpu.CompilerParams(has_side_effects=DATAFLOW_EFFECT),
    )(*[pltpu.with_memory_space_constraint(a, pltpu.HBM) for a in list(arrs) + lands])
    return res[0], res[1], list(res[2:2 + n]), list(res[2 + n:2 + 2 * n]), res[-1]


def _ag_ici_wait(send_sems, recv_sems, ins, lands, after, name):
    n = len(ins)

    def body(*refs):
        ins_r, lands_r = refs[:n], refs[n:2 * n]
        send_r, recv_r = refs[2 * n:2 * n + 2]
        x, y, c = _position()
        for i in range(n):
            for j, (px, py) in enumerate(_other_chips(x, y)):
                cp = pltpu.make_async_remote_copy(src_ref=ins_r[i].at[c], dst_ref=lands_r[i].at[2 * px + py, c], send_sem=send_r.at[3 * i + j],
                                                  recv_sem=recv_r.at[3 * i + j], device_id=(px, py, c), device_id_type=MESH)
                cp.wait_send()
                cp.wait_recv()

    res = pl.pallas_call(
        body, name=name,
        out_shape=[pltpu.HBM(a.shape, a.dtype) for a in list(ins) + list(lands)],
        in_specs=[HBM_SPEC] * (2 * n) + [SEM_SPEC, SEM_SPEC, pl.BlockSpec(memory_space=pl.ANY)], out_specs=[HBM_SPEC] * (2 * n),
        input_output_aliases={i: i for i in range(2 * n)},
        compiler_params=pltpu.CompilerParams(has_side_effects=DATAFLOW_EFFECT),
    )(*ins, *lands, send_sems, recv_sems, after)
    return list(res[:n]), list(res[n:])


def _ag_finish_start(arrs, lands, name):
    n = len(arrs)

    def body(*refs):
        ins, bufs = refs[:n], refs[n:2 * n]
        send_sems, recv_sems = refs[2 * n:2 * n + 2]
        token = refs[-1]
        x, y, c = _position()
        for i in range(n):
            pltpu.make_async_remote_copy(src_ref=ins[i], dst_ref=bufs[i].at[2 * x + y], send_sem=send_sems.at[4 * i + 3],
                                         recv_sem=recv_sems.at[4 * i + 3], device_id=(x, y, 1 - c), device_id_type=MESH).start()
            for j, (px, py) in enumerate(_other_chips(x, y)):
                half = bufs[i].at[2 * px + py, c]
                pltpu.make_async_remote_copy(src_ref=half, dst_ref=half, send_sem=send_sems.at[4 * i + j],
                                             recv_sem=recv_sems.at[4 * i + j], device_id=(x, y, 1 - c), device_id_type=MESH).start()
        token[...] = jnp.zeros_like(token)

    res = pl.pallas_call(
        body, name=name,
        out_shape=[pltpu.SemaphoreType.DMA((4 * n,)), pltpu.SemaphoreType.DMA((4 * n,))]
        + [pltpu.HBM(a.shape, a.dtype) for a in list(arrs) + list(lands)] + [SDS((8, 128), F32)],
        in_specs=[HBM_SPEC] * (2 * n), out_specs=[SEM_SPEC, SEM_SPEC] + [HBM_SPEC] * (2 * n) + [pl.BlockSpec(memory_space=pltpu.VMEM)],
        input_output_aliases={i: 2 + i for i in range(2 * n)},
        compiler_params=pltpu.CompilerParams(has_side_effects=DATAFLOW_EFFECT),
    )(*arrs, *lands)
    return res[0], res[1], list(res[2:2 + n]), list(res[2 + n:2 + 2 * n]), res[-1]


def _ag_finish_wait(send_sems, recv_sems, arrs, lands, after, name):
    n = len(arrs)

    def body(*refs):
        ins, bufs = refs[:n], refs[n:2 * n]
        send_r, recv_r = refs[2 * n:2 * n + 2]
        x, y, c = _position()
        for i in range(n):
            own = pltpu.make_async_remote_copy(src_ref=ins[i], dst_ref=bufs[i].at[2 * x + y], send_sem=send_r.at[4 * i + 3],
                                               recv_sem=recv_r.at[4 * i + 3], device_id=(x, y, 1 - c), device_id_type=MESH)
            own.wait_send()
            own.wait_recv()
            for j, (px, py) in enumerate(_other_chips(x, y)):
                pltpu.make_async_remote_copy(src_ref=bufs[i].at[2 * px + py, c], dst_ref=bufs[i].at[2 * px + py, c],
                                             send_sem=send_r.at[4 * i + j], recv_sem=recv_r.at[4 * i + j],
                                             device_id=(x, y, 1 - c), device_id_type=MESH).wait_send()
                pltpu.make_async_remote_copy(src_ref=bufs[i].at[2 * px + py, 1 - c], dst_ref=bufs[i].at[2 * px + py, 1 - c],
                                             send_sem=send_r.at[4 * i + j], recv_sem=recv_r.at[4 * i + j],
                                             device_id=(x, y, 1 - c), device_id_type=MESH).wait_recv()

    res = pl.pallas_call(
        body, name=name, out_shape=[pltpu.HBM(a.shape, a.dtype) for a in list(arrs) + list(lands)],
        in_specs=[HBM_SPEC] * (2 * n) + [SEM_SPEC, SEM_SPEC, pl.BlockSpec(memory_space=pl.ANY)], out_specs=[HBM_SPEC] * (2 * n),
        input_output_aliases={i: i for i in range(2 * n)},
        compiler_params=pltpu.CompilerParams(has_side_effects=DATAFLOW_EFFECT),
    )(*arrs, *lands, send_sems, recv_sems, after)
    return list(res[n:])


def _ag_finish(arrs, lands, name):
    send_sems, recv_sems, arrs, lands, token = _ag_finish_start(arrs, lands, name + "_start")
    return _ag_finish_wait(send_sems, recv_sems, arrs, lands, token, name + "_wait")


def _swap_start(gs, name):
    n = len(gs)

    def body(*refs):
        ins, lands = refs[:n], refs[n:2 * n]
        send_sems, recv_sems = refs[2 * n:2 * n + 2]
        token = refs[-1]
        x, y, c = _position()
        for i in range(n):
            for p in range(4):
                pltpu.make_async_remote_copy(src_ref=ins[i].at[p, 1 - c], dst_ref=lands[i].at[p], send_sem=send_sems.at[4 * i + p],
                                             recv_sem=recv_sems.at[4 * i + p], device_id=(x, y, 1 - c), device_id_type=MESH).start()
        token[...] = jnp.zeros_like(token)

    lands = [lax.empty((4,) + g.shape[2:], F32) for g in gs]
    res = pl.pallas_call(
        body, name=name,
        out_shape=[pltpu.SemaphoreType.DMA((4 * n,)), pltpu.SemaphoreType.DMA((4 * n,))]
        + [pltpu.HBM(a.shape, F32) for a in list(gs) + lands] + [SDS((8, 128), F32)],
        in_specs=[HBM_SPEC] * (2 * n), out_specs=[SEM_SPEC, SEM_SPEC] + [HBM_SPEC] * (2 * n) + [pl.BlockSpec(memory_space=pltpu.VMEM)],
        input_output_aliases={i: 2 + i for i in range(2 * n)},
        compiler_params=pltpu.CompilerParams(has_side_effects=DATAFLOW_EFFECT),
    )(*[pltpu.with_memory_space_constraint(a, pltpu.HBM) for a in list(gs) + lands])
    return res[0], res[1], list(res[2:2 + n]), list(res[2 + n:2 + 2 * n]), res[-1]


def _swap_wait(send_sems, recv_sems, gs, lands, after, name):
    n = len(gs)

    def body(*refs):
        ins, lands_r = refs[:n], refs[n:2 * n]
        send_r, recv_r = refs[2 * n:2 * n + 2]
        x, y, c = _position()
        for i in range(n):
            for p in range(4):
                cp = pltpu.make_async_remote_copy(src_ref=ins[i].at[p, 1 - c], dst_ref=lands_r[i].at[p], send_sem=send_r.at[4 * i + p],
                                                  recv_sem=recv_r.at[4 * i + p], device_id=(x, y, 1 - c), device_id_type=MESH)
                cp.wait_send()
                cp.wait_recv()

    res = pl.pallas_call(
        body, name=name, out_shape=[pltpu.HBM(a.shape, F32) for a in list(gs) + list(lands)],
        in_specs=[HBM_SPEC] * (2 * n) + [SEM_SPEC, SEM_SPEC, pl.BlockSpec(memory_space=pl.ANY)], out_specs=[HBM_SPEC] * (2 * n),
        input_output_aliases={i: i for i in range(2 * n)},
        compiler_params=pltpu.CompilerParams(has_side_effects=DATAFLOW_EFFECT),
    )(*gs, *lands, send_sems, recv_sems, after)
    return list(res[:n]), list(res[n:])


def _exchange_start(s1s, name):
    n = len(s1s)

    def body(*refs):
        srcs, lands = refs[:n], refs[n:2 * n]
        send_sems, recv_sems = refs[2 * n:2 * n + 2]
        token = refs[-1]
        x, y, c = _position()
        for i in range(n):
            for j, (px, py) in enumerate(_other_chips(x, y)):
                pltpu.make_async_remote_copy(src_ref=srcs[i].at[2 * px + py], dst_ref=lands[i].at[j], send_sem=send_sems.at[3 * i + j],
                                             recv_sem=recv_sems.at[3 * i + j], device_id=(px, py, c), device_id_type=MESH).start()
        token[...] = jnp.zeros_like(token)

    lands = [lax.empty((3,) + s.shape[1:], F32) for s in s1s]
    res = pl.pallas_call(
        body, name=name,
        out_shape=[pltpu.SemaphoreType.DMA((3 * n,)), pltpu.SemaphoreType.DMA((3 * n,))]
        + [pltpu.HBM(a.shape, F32) for a in list(s1s) + lands] + [SDS((8, 128), F32)],
        in_specs=[HBM_SPEC] * (2 * n), out_specs=[SEM_SPEC, SEM_SPEC] + [HBM_SPEC] * (2 * n) + [pl.BlockSpec(memory_space=pltpu.VMEM)],
        input_output_aliases={i: 2 + i for i in range(2 * n)},
        compiler_params=pltpu.CompilerParams(has_side_effects=DATAFLOW_EFFECT),
    )(*[pltpu.with_memory_space_constraint(a, pltpu.HBM) for a in list(s1s) + lands])
    return res[0], res[1], list(res[2:2 + n]), list(res[2 + n:2 + 2 * n]), res[-1]


def _exchange_wait(send_sems, recv_sems, s1s, lands, after, name):
    n = len(s1s)

    def body(*refs):
        srcs, lands_r = refs[:n], refs[n:2 * n]
        send_r, recv_r = refs[2 * n:2 * n + 2]
        x, y, c = _position()
        for i in range(n):
            for j, (px, py) in enumerate(_other_chips(x, y)):
                cp = pltpu.make_async_remote_copy(src_ref=srcs[i].at[2 * px + py], dst_ref=lands_r[i].at[j], send_sem=send_r.at[3 * i + j],
                                                  recv_sem=recv_r.at[3 * i + j], device_id=(px, py, c), device_id_type=MESH)
                cp.wait_send()
                cp.wait_recv()

    res = pl.pallas_call(
        body, name=name, out_shape=[pltpu.HBM(a.shape, F32) for a in list(s1s) + list(lands)],
        in_specs=[HBM_SPEC] * (2 * n) + [SEM_SPEC, SEM_SPEC, pl.BlockSpec(memory_space=pl.ANY)], out_specs=[HBM_SPEC] * (2 * n),
        input_output_aliases={i: i for i in range(2 * n)},
        compiler_params=pltpu.CompilerParams(has_side_effects=DATAFLOW_EFFECT),
    )(*s1s, *lands, send_sems, recv_sems, after)
    return list(res[:n]), list(res[n:])


def _join_halves(f2s, name):
    n = len(f2s)
    hbm = pl.BlockSpec(memory_space=pl.ANY)

    def body(*refs):
        ins, outs = refs[:n], refs[n:2 * n]
        send_sems, recv_sems = refs[2 * n:]
        x, y, c = _position()
        cps = []
        for i in range(n):
            cp = pltpu.make_async_remote_copy(src_ref=ins[i].at[c], dst_ref=outs[i].at[c], send_sem=send_sems.at[i],
                                              recv_sem=recv_sems.at[i], device_id=(x, y, 1 - c), device_id_type=MESH)
            cp.start()
            cps.append(cp)
        for i in range(n):
            pltpu.make_async_remote_copy(src_ref=ins[i].at[1 - c], dst_ref=outs[i].at[1 - c], send_sem=send_sems.at[i],
                                         recv_sem=recv_sems.at[i], device_id=(x, y, 1 - c), device_id_type=MESH).wait_recv()
        for cp in cps:
            cp.wait_send()

    return pl.pallas_call(
        body, name=name, in_specs=[hbm] * n, out_specs=[hbm] * n, out_shape=[SDS(f.shape, f.dtype) for f in f2s],
        input_output_aliases={i: i for i in range(n)},
        scratch_shapes=[pltpu.SemaphoreType.DMA((n,)), pltpu.SemaphoreType.DMA((n,))])(*f2s)


def _sum_tile(rows, cols):
    return _row_tile(rows, max(8, (2 ** 19 // cols) // 8 * 8))


def _add_pair(g, r1, c, name):
    _, R, C = r1.shape
    tr = _sum_tile(R, C)

    def body(c_ref, a_ref, b_ref, o_ref):
        del c_ref
        o_ref[...] = a_ref[...] + b_ref[...]

    blk = pl.BlockSpec((None, tr, C), lambda p, i, cr: (p, i, 0))
    return pl.pallas_call(
        body, name=name,
        grid_spec=pltpu.PrefetchScalarGridSpec(
            num_scalar_prefetch=1, grid=(4, R // tr),
            in_specs=[pl.BlockSpec((None, None, tr, C), lambda p, i, cr: (p, cr[0], i, 0)), blk], out_specs=blk),
        out_shape=SDS((4, R, C), F32))(c, g, r1)


def _add_four(s1, r2, me_c, name):
    _, R, C = s1.shape
    tr = _sum_tile(R, C)

    def body(m_ref, a_ref, b_ref, o_ref):
        del m_ref
        o_ref[...] = ((a_ref[...] + b_ref[0]) + b_ref[1]) + b_ref[2]

    return pl.pallas_call(
        body, name=name,
        grid_spec=pltpu.PrefetchScalarGridSpec(
            num_scalar_prefetch=1, grid=(R // tr,),
            in_specs=[pl.BlockSpec((None, tr, C), lambda i, mr: (mr[0], i, 0)), pl.BlockSpec((3, tr, C), lambda i, mr: (0, i, 0))],
            out_specs=pl.BlockSpec((None, tr, C), lambda i, mr: (mr[1], i, 0))),
        out_shape=SDS((2, R, C), F32))(me_c, s1, r2)


def _all_reduce_small(vs):
    n = len(vs)

    def body(*refs):
        ins, outs, bufs = refs[:n], refs[n:2 * n], refs[2 * n:3 * n]
        send_sems, recv_sems = refs[3 * n:]
        x, y, c = _position()
        me = 4 * x + 2 * y + c
        for i in range(n):
            bufs[i][me] = ins[i][...]
        cps = []
        for k in range(1, 8):
            to = (1 - x if k & 4 else x, 1 - y if k & 2 else y, 1 - c if k & 1 else c)
            for i in range(n):
                cp = pltpu.make_async_remote_copy(src_ref=bufs[i].at[me], dst_ref=bufs[i].at[me], send_sem=send_sems.at[7 * i + k - 1],
                                                  recv_sem=recv_sems.at[7 * i + k - 1], device_id=to, device_id_type=MESH)
                cp.start()
                cps.append(cp)
        for cp in cps:
            cp.wait_send()
        for k in range(1, 8):
            src = 4 * (1 - x if k & 4 else x) + 2 * (1 - y if k & 2 else y) + (1 - c if k & 1 else c)
            for i in range(n):
                pltpu.make_async_remote_copy(src_ref=bufs[i].at[src], dst_ref=bufs[i].at[src], send_sem=send_sems.at[7 * i + k - 1],
                                             recv_sem=recv_sems.at[7 * i + k - 1], device_id=(x, y, c), device_id_type=MESH).wait_recv()
        for i in range(n):
            acc = bufs[i][0]
            for k in range(1, 8):
                acc = acc + bufs[i][k]
            outs[i][...] = acc

    vm = pl.BlockSpec(memory_space=pltpu.VMEM)
    return pl.pallas_call(
        body, name="all_reduce_small", in_specs=[vm] * n, out_specs=[vm] * n, out_shape=[SDS(v.shape, F32) for v in vs],
        scratch_shapes=[pltpu.VMEM((8,) + v.shape, F32) for v in vs]
        + [pltpu.SemaphoreType.DMA((7 * n,)), pltpu.SemaphoreType.DMA((7 * n,))])(*vs)


def _reduce_begin(grads, tag):
    g4 = [g.reshape(4, 2, g.shape[1] // 2, g.shape[2]) for g in grads]
    send_sems, recv_sems, g4, lands, token = _swap_start(g4, "rs_swap_start_" + tag)
    return (send_sems, recv_sems, g4, lands), token


def _reduce_mid(state, after, tag):
    _, _, c = _position()
    cs = jnp.reshape(c, (1,)).astype(jnp.int32)
    send_sems, recv_sems, g4, lands = state
    g4, r1 = _swap_wait(send_sems, recv_sems, g4, lands, after, "rs_swap_wait_" + tag)
    s1 = [_add_pair(g, r, cs, f"rs_add_pair_{tag}{i}") for i, (g, r) in enumerate(zip(g4, r1))]
    send_sems, recv_sems, s1, lands, token = _exchange_start(s1, "rs_exchange_start_" + tag)
    return (send_sems, recv_sems, s1, lands), token


def _reduce_end(state, after, tag):
    x, y, c = _position()
    send_sems, recv_sems, s1, lands = state
    s1, lands = _exchange_wait(send_sems, recv_sems, s1, lands, after, "rs_exchange_wait_" + tag)
    me_c = jnp.stack([2 * x + y, c]).astype(jnp.int32)
    f2 = [_add_four(s, l, me_c, f"rs_add_four_{tag}{i}") for i, (s, l) in enumerate(zip(s1, lands))]
    return [f.reshape(2 * f.shape[1], f.shape[2]) for f in _join_halves(f2, "rs_join_" + tag)]


def _halves(a):
    return a.reshape((2, a.shape[0] // 2) + a.shape[1:])


def _after(a, token):
    return a + token[0, 0]


def _in_proj_own(x, nw, w, part):
    M, K = x.shape
    n = w.shape[2]
    sub = 256

    def body(p_ref, x_ref, nw_ref, w_ref, xn_ref, z_ref):
        del p_ref
        for r0 in range(0, MM_TM, sub):
            rows = pl.ds(r0, sub)
            for c0 in range(r0, r0 + sub, NORM_RC):
                ch = pl.ds(c0, NORM_RC)
                xv = x_ref[ch, :]
                xn_ref[ch, :] = (xv * lax.rsqrt(jnp.mean(xv * xv, axis=-1, keepdims=True) + RMS_EPS) * nw_ref[...]).astype(BF16)
            z_ref[rows, :] = jnp.dot(xn_ref[rows, :], w_ref[...], preferred_element_type=F32).astype(BF16)

    row = pl.BlockSpec((MM_TM, K), lambda i, pr: (i, 0))
    return pl.pallas_call(
        body, name="in_proj_own",
        grid_spec=pltpu.PrefetchScalarGridSpec(
            num_scalar_prefetch=1, grid=(M // MM_TM,),
            in_specs=[row, pl.BlockSpec((1, K), lambda i, pr: (0, 0)), pl.BlockSpec((None, K, n), lambda i, pr: (0, 0, 0))],
            out_specs=[row, pl.BlockSpec((MM_TM, n), lambda i, pr: (i, pr[0]))]),
        out_shape=[SDS((M, K), BF16), SDS((M, 4 * n), BF16)])(part, x, nw, w)


def _in_proj_parts(xn, w, parts, z, name):
    M, K = xn.shape
    _, _, n = w.shape
    P = parts.shape[0]

    def body(p_ref, a_ref, b_ref, z_in, o_ref):
        del p_ref, z_in
        o_ref[...] = jnp.dot(a_ref[...], b_ref[...], preferred_element_type=F32).astype(BF16)

    return pl.pallas_call(
        body, name=name,
        grid_spec=pltpu.PrefetchScalarGridSpec(
            num_scalar_prefetch=1, grid=(P, M // MM_TM),
            in_specs=[pl.BlockSpec((MM_TM, K), lambda g, i, pr: (i, 0)), pl.BlockSpec((None, K, n), lambda g, i, pr: (pr[g], 0, 0)),
                      pl.BlockSpec(memory_space=pl.ANY)],
            out_specs=pl.BlockSpec((MM_TM, n), lambda g, i, pr: (i, pr[g]))),
        out_shape=SDS(z.shape, BF16), input_output_aliases={3: 0})(parts, xn, w, z)


def _local_step(x, mem, target, sp, ex):
    S, D = x.shape
    band, buckets = _bias_static()
    buckets = jnp.asarray(buckets)

    tok = ex.start_first()
    own, me, others = ex.own_w_in()
    xn, z = _in_proj_own(x, _after(sp["attn_norm_w"], tok), own, me)
    w_in, gathered_small = ex.first_weights(after=z)
    sp = {**sp, **gathered_small}
    n_in = 4 * w_in.shape[2]
    bw = {"w_in": w_in}
    z = _in_proj_parts(xn, w_in, others + ex.start_rest()[0, 0].astype(jnp.int32), z, "in_proj_rest")
    qh, kh, vh, q2, k2, v2 = _qkv_prep(z, sp["q_norm_w"], sp["k_norm_w"])
    tab = sp["rel_bias_table"].T.reshape(N_GROUPS, HPG, N_BUCKETS)
    bias = _bias_fwd(jnp.pad(tab, ((0, 0), (0, 8 - HPG), (0, 0))), buckets)
    biasm = jnp.where(jnp.asarray(band)[None, None], bias[:, :HPG].reshape(N_GROUPS, HPG, NQ, 2 * NQ), NEG)
    seqs = lambda a: a.reshape(HPG * P4, S // P4, HEAD)
    groups = ((qh, kh, vh, 1, 0, HPG, 0), (qh, kh, vh, 4, HPG, HPG, 0), (seqs(q2), seqs(k2), seqs(v2), 4, 0, HPG * P4, 2))
    os_, lses = [], []
    for g, (qg, kg, vg, dil, head0, nseq, shift) in enumerate(groups):
        o_g, lse_g = _attn_fwd(qg, kg, vg, biasm[g], f"attn_fwd_g{g}", dil, head0, nseq, shift)
        os_.append(o_g.reshape(HPG, -1, o_g.shape[1], HEAD) if g == 2 else o_g)
        lses.append(lse_g.reshape(HPG, -1, lse_g.shape[1], HEAD) if g == 2 else lse_g)
    attn = _merge_fwd(os_, lses)
    tok = ex.rest_arrived(after=attn)
    u1, u3 = _conv_fwd(z, sp["conv_dw_w"], sp["conv_dw_b"], _after(sp["conv_ln_w"], tok), sp["conv_ln_b"])
    bw.update(ex.rest_weights(after=u3))
    F2 = 4 * bw["w_up"].shape[2]
    mn, kv, mk, mv = _memkv_fwd(mem, sp["mem_norm_w"], bw["w_mem_kv"], sp["xk_norm_w"])
    oc = _cross_fwd(z, sp["xq_norm_w"], mk, mv)
    h1, hn = _outproj_fwd(x, z, sp["b_gate"], attn, u3, oc, bw["w_attn_o"], bw["w_conv_o"], bw["w_cross_o"], bw["w_out"],
                          sp["ffn_norm_w"])
    up = _mm_nn(hn, bw["w_up"], BF16, "ffn_up")
    dy, loss_tile, ac, gc = _ffn_fwd(up, sp["ffn_conv_w"], sp["ffn_conv_b"], bw["w_down"], h1, target)

    gs, gb = {}, {}
    dca, dcg, acca, accg, gb["w_down"] = _ffn_bwd_a(dy, bw["w_down"], ac, gc)
    cols = lambda acc: jnp.sum(acc, axis=1).reshape(1, F2 // 2)
    gs["ffn_conv_b"] = jnp.concatenate([cols(acca), cols(accg)], axis=1)
    dup, tacc = _ffn_bwd_b(dca, dcg, up, sp["ffn_conv_w"])
    gs["ffn_conv_w"] = jnp.transpose(jnp.sum(tacc, axis=2), (1, 0, 2)).reshape(FFN_K, F2)
    gb["w_up"] = _mm_tn(hn, dup, 4, "dw_up")
    tok = ex.reduce_begin("a", ("w_down", "w_up"), gb)
    dh1, gs["ffn_norm_w"] = _norm_in_bwd(dup, bw["w_up"], h1, _after(sp["ffn_norm_w"], tok), dy, "ffn_in_bwd")
    tok = ex.reduce_mid("a", after=gs["ffn_norm_w"])
    dz, dattn, du3, doc, gs["b_gate"], gb["w_out"], gb["w_attn_o"], gb["w_conv_o"], gb["w_cross_o"] = _outproj_bwd(
        dh1, z, _after(sp["b_gate"], tok), attn, u3, oc, bw["w_attn_o"], bw["w_conv_o"], bw["w_cross_o"], bw["w_out"], n_in)
    dz, dmk, dmv, gs["xq_norm_w"] = _cross_bwd(dz, doc, z, sp["xq_norm_w"], mk, mv)
    gb["w_mem_kv"], gs["xk_norm_w"], gs["mem_norm_w"] = _memkv_bwd(
        dmk, dmv, kv, mem, mn, sp["mem_norm_w"], bw["w_mem_kv"], sp["xk_norm_w"])
    ex.reduce_end("a", after=gs["mem_norm_w"])
    tok = ex.reduce_begin("b", ("w_out", "w_attn_o", "w_conv_o", "w_cross_o", "w_mem_kv"), gb)
    du1, cacc = _conv_bwd_a(du3, u1, z, _after(sp["conv_ln_w"], tok), sp["conv_ln_b"])
    gs["conv_dw_w"], gs["conv_dw_b"] = cacc[:CONV_K], cacc[32:33]
    gs["conv_ln_w"], gs["conv_ln_b"] = cacc[33:34], cacc[34:35]
    tok = ex.reduce_mid("b", after=cacc)
    dz = _conv_bwd_b(dz, du1, z, _after(sp["conv_dw_w"], tok))
    merged_grads = _merge_bwd(dattn, os_, lses)
    dqs, dks, dvs, dsbs = [], [], [], []
    for g, (qg, kg, vg, dil, head0, nseq, shift) in enumerate(groups):
        wg_g, da_g, dh_g = merged_grads[g]
        lse_g = lses[g]
        if g == 2:
            wg_g, da_g, dh_g, lse_g = seqs(wg_g), seqs(da_g), seqs(dh_g), seqs(lse_g)
        dq_g, dk_g, dv_g, dsb_g = _attn_bwd(qg, kg, vg, biasm[g], da_g, wg_g, dh_g, lse_g, f"attn_bwd_g{g}", dil, head0, nseq, shift)
        if g == 2:
            dq_g, dk_g, dv_g = (t.reshape(HPG, P4, S // P4, HEAD) for t in (dq_g, dk_g, dv_g))
            dsb_g = jnp.sum(dsb_g.reshape(HPG, P4, NQ, 2 * NQ), axis=1)
        dqs.append(dq_g)
        dks.append(dk_g)
        dvs.append(dv_g)
        dsbs.append(dsb_g.reshape(HPG, NQ * 2 * NQ))
    dtab = _bias_bwd(jnp.pad(jnp.stack(dsbs), ((0, 0), (0, 8 - HPG), (0, 0))), buckets)
    gs["rel_bias_table"] = dtab[:, :HPG].reshape(N_GROUPS * HPG, N_BUCKETS).T
    dz, gs["q_norm_w"], gs["k_norm_w"] = _qkv_bwd(dz, z, dqs, dks, dvs, sp["q_norm_w"], sp["k_norm_w"])
    ex.reduce_end("b", after=gs["q_norm_w"])
    gb["w_in"] = _mm_tn(xn, dz, 4, "dw_in")
    tok = ex.reduce_mid("c", after=ex.reduce_begin("c", ("w_in",), gb))
    dx, gs["attn_norm_w"] = _norm_in_bwd(dz, bw["w_in"], x, _after(sp["attn_norm_w"], tok), dh1, "in_bwd")
    ex.reduce_end("c", after=gs["attn_norm_w"])
    return loss_tile, dx, gs, gb


SMALL = ("rel_bias_table", "attn_norm_w", "b_gate", "q_norm_w", "k_norm_w", "conv_dw_w", "conv_dw_b", "conv_ln_w", "conv_ln_b",
         "mem_norm_w", "xq_norm_w", "xk_norm_w", "ffn_norm_w", "ffn_conv_w", "ffn_conv_b")
SMALL_SHARDED = ("conv_dw_w", "ffn_conv_w")
BIG_COL = ("w_in", "w_attn_o", "w_conv_o", "w_cross_o", "w_up")
BIG_ROW = ("w_mem_kv", "w_out", "w_down")
BIG = BIG_COL + BIG_ROW
WEIGHTS = ("rel_bias_table", "attn_norm_w", "w_in", "b_gate", "q_norm_w", "k_norm_w", "w_attn_o", "conv_dw_w", "conv_dw_b",
           "conv_ln_w", "conv_ln_b", "w_conv_o", "mem_norm_w", "w_mem_kv", "xq_norm_w", "xk_norm_w", "w_cross_o", "w_out",
           "ffn_norm_w", "w_up", "ffn_conv_w", "ffn_conv_b", "w_down")


class _Exchanges:
    REST = tuple(k for k in BIG if k != "w_in")

    def __init__(self, w):
        self.w = w
        self.pending = {}
        self.reduced = {}

    def _whole(self, k, ga):
        ga = ga.reshape((4,) + self.w[k].shape)
        return ga if k in BIG_COL else ga.reshape((4 * self.w[k].shape[0],) + self.w[k].shape[1:])

    def start_first(self):
        self.w_in_local = self.w["w_in"].astype(BF16)
        local = [_halves(self.w_in_local)]
        for k in SMALL_SHARDED:
            flat = jnp.ravel(self.w[k])
            local.append(jnp.pad(flat, (0, (-flat.shape[0]) % 2048)).reshape(2, -1, 128))
        send_sems, recv_sems, ins, lands, token = _ag_ici_start(local, "gather_first_start")
        self.pending["first"] = (send_sems, recv_sems, ins, lands)
        return token

    def own_w_in(self):
        x, y, _ = _position()
        me = 2 * x + y
        others = jnp.stack([me ^ 1, me ^ 2, me ^ 3]).astype(jnp.int32)
        return self.w_in_local[None], jnp.reshape(me, (1,)).astype(jnp.int32), others

    def first_weights(self, after):
        send_sems, recv_sems, ins, lands = self.pending.pop("first")
        ins, lands = _ag_ici_wait(send_sems, recv_sems, ins, lands, after, "gather_first_wait")
        gathered = _ag_finish(ins, lands, "gather_first_finish")
        self.first = gathered[0]
        small = {}
        for k, ga in zip(SMALL_SHARDED, gathered[1:]):
            r, cdim = self.w[k].shape
            parts = ga.reshape(4, -1)[:, :r * cdim].reshape(4, r, cdim)
            small[k] = jnp.transpose(parts, (1, 0, 2)).reshape(r, 4 * cdim)
        return self._whole("w_in", gathered[0]), small

    def start_rest(self):
        local = [_halves(self.w[k].astype(BF16)) for k in self.REST]
        local, _ = lax.optimization_barrier((local, self.first))
        send_sems, recv_sems, ins, lands, token = _ag_ici_start(local, "gather_rest_start")
        self.pending["rest"] = (send_sems, recv_sems, ins, lands)
        return token

    def rest_arrived(self, after):
        send_sems, recv_sems, ins, lands = self.pending.pop("rest")
        ins, lands = _ag_ici_wait(send_sems, recv_sems, ins, lands, after, "gather_rest_wait")
        send_sems, recv_sems, ins, lands, token = _ag_finish_start(ins, lands, "gather_rest_finish_start")
        self.pending["rest"] = (send_sems, recv_sems, ins, lands)
        return token

    def rest_weights(self, after):
        gathered = _ag_finish_wait(*self.pending.pop("rest"), after, "gather_rest_finish_wait")
        return {k: self._whole(k, ga) for k, ga in zip(self.REST, gathered)}

    def reduce_begin(self, tag, names, gb):
        parts = [gb[k].reshape((4,) + self.w[k].shape) for k in names]
        state, token = _reduce_begin(parts, tag)
        self.pending[tag] = (state, names)
        return token

    def reduce_mid(self, tag, after):
        state, names = self.pending.pop(tag)
        state, token = _reduce_mid(state, after, tag)
        self.pending[tag] = (state, names)
        return token

    def reduce_end(self, tag, after):
        state, names = self.pending.pop(tag)
        self.reduced.update(zip(names, _reduce_end(state, after, tag)))


def _step(x, mem, target, w, m, v):
    xi, yi, _ = _position()
    shard = 2 * xi + yi
    ex = _Exchanges(w)
    sp = {k: w[k] for k in SMALL if k not in SMALL_SHARDED}
    loss_tile, dx, gs, _ = _local_step(x, mem, target, sp, ex)
    g_big = ex.reduced

    red = _all_reduce_small([loss_tile] + [gs[k] for k in SMALL])
    loss = red[0][0, 0]
    g_small = dict(zip(SMALL, red[1:]))
    for k in SMALL_SHARDED:
        cdim = w[k].shape[1]
        g_small[k] = lax.dynamic_slice_in_dim(g_small[k], shard * cdim, cdim, axis=1)

    grads, delta, new_m, new_v = {}, {}, {}, {}
    for k in BIG:
        grads[k] = g_big[k]
        delta[k], new_m[k], new_v[k] = _adamw(w[k], g_big[k], m[k], v[k], "adamw_" + k)
    outs = _adamw_small([w[k] for k in SMALL], [g_small[k] for k in SMALL], [m[k] for k in SMALL], [v[k] for k in SMALL])
    for dst, vals in zip((delta, new_m, new_v), outs):
        dst.update(zip(SMALL, vals))
    grads.update(g_small)
    return loss, dx, grads, delta, new_m, new_v


def kernel(x, mem, rel_bias_table, attn_norm_w, w_in, b_gate, q_norm_w, k_norm_w, w_attn_o, conv_dw_w, conv_dw_b, conv_ln_w, conv_ln_b, w_conv_o, mem_norm_w, w_mem_kv, xq_norm_w, xk_norm_w, w_cross_o, w_out, ffn_norm_w, w_up, ffn_conv_w, ffn_conv_b, w_down, loss_target, m_rel_bias_table, m_attn_norm_w, m_w_in, m_b_gate, m_q_norm_w, m_k_norm_w, m_w_attn_o, m_conv_dw_w, m_conv_dw_b, m_conv_ln_w, m_conv_ln_b, m_w_conv_o, m_mem_norm_w, m_w_mem_kv, m_xq_norm_w, m_xk_norm_w, m_w_cross_o, m_w_out, m_ffn_norm_w, m_w_up, m_ffn_conv_w, m_ffn_conv_b, m_w_down, v_rel_bias_table, v_attn_norm_w, v_w_in, v_b_gate, v_q_norm_w, v_k_norm_w, v_w_attn_o, v_conv_dw_w, v_conv_dw_b, v_conv_ln_w, v_conv_ln_b, v_w_conv_o, v_mem_norm_w, v_w_mem_kv, v_xq_norm_w, v_xk_norm_w, v_w_cross_o, v_w_out, v_ffn_norm_w, v_w_up, v_ffn_conv_w, v_ffn_conv_b, v_w_down):
    args = locals()
    def block(name, k):
        a = args[name] if k == "rel_bias_table" else args[name][0]
        return a.reshape(1, -1) if a.ndim == 1 else a

    w = {k: block(k, k) for k in WEIGHTS}
    m = {k: block("m_" + k, k) for k in WEIGHTS}
    v = {k: block("v_" + k, k) for k in WEIGHTS}
    loss, dx, grads, delta, new_m, new_v = _step(x[0], mem[0], loss_target[0], w, m, v)
    out = [loss, dx[None]]
    for d in (grads, delta, new_m, new_v):
        for k in WEIGHTS:
            out.append(d[k].reshape(args[k].shape))
    return tuple(out)
```

```python
import functools
import math

import numpy as np
import jax
import jax.numpy as jnp
from jax import lax
from jax.experimental import pallas as pl
from jax.experimental.pallas import tpu as pltpu

F32, BF16 = jnp.float32, jnp.bfloat16
SDS = jax.ShapeDtypeStruct
MESH = pl.DeviceIdType.MESH

HEAD = 128
N_GROUPS, HPG = 3, 4
ATTN_GROUPS = ((128, 1), (512, 4), (2048, 16))
NQ = 128
QKV_W = N_GROUPS * HPG * HEAD
CH = 512
CONV_K, FFN_K = 31, 3
N_BUCKETS, MAX_DIST = 32, 2048
RMS_EPS, LN_EPS = 1e-6, 1e-5
O_Q, O_K, O_V, O_CV, O_CG, O_XQ, O_G = 0, QKV_W, 2 * QKV_W, 3 * QKV_W, 3 * QKV_W + CH, 3 * QKV_W + 2 * CH, 3 * QKV_W + 3 * CH
ADAM_LR, ADAM_B1, ADAM_B2, ADAM_EPS, ADAM_WD, ADAM_STEP = 0.001, 0.9, 0.999, 1e-08, 0.01, 10
NEG = -1e30
SCALE = HEAD ** -0.5
TM = 512
MM_TM = 1024
ATT_RB = 2048
NT_DIMS = (((1,), (1,)), ((), ()))
TN_DIMS = (((0,), (0,)), ((), ()))


CONV_RC = 32
CONV_LB = 256
CONV_LANES = tuple(slice(l, l + CONV_LB) for l in range(0, CH, CONV_LB))
CONV_HALO = 32
FFN_RC = 32


def _conv_taps(buf, base, lanes, q_lo, q_hi, visit):
    span = CONV_RC + CONV_HALO
    xx = buf[pl.ds(base, span), lanes]
    for s in range(8):
        xs = xx if s == 0 else pltpu.roll(xx, span - s, 0)
        for q in range(s, q_hi + 1, 8):
            if q >= q_lo:
                visit(q, xs[q - s:q - s + CONV_RC])


def _sig(v):
    return 0.5 * jnp.tanh(0.5 * v) + 0.5


def _fold8(v):
    acc = v[0:8]
    for r in range(8, v.shape[0], 8):
        acc = acc + v[r:r + 8]
    return acc


def _row_tile(rows, cap, mult=8):
    best = None
    for t in range(mult, min(rows, cap) + 1, mult):
        if rows % t == 0:
            best = t
    return best if best is not None else rows


def _full(shape):
    n = len(shape)
    return pl.BlockSpec(shape, lambda *a: (0,) * n)


def _mm_nn(a, b, out_dtype, name):
    M, K = a.shape
    G, _, n = b.shape

    def body(a_ref, b_ref, o_ref):
        o_ref[...] = jnp.dot(a_ref[...].astype(BF16), b_ref[...], preferred_element_type=F32).astype(out_dtype)

    return pl.pallas_call(
        body, name=name, grid=(G, M // MM_TM),
        in_specs=[pl.BlockSpec((MM_TM, K), lambda g, i: (i, 0)), pl.BlockSpec((None, K, n), lambda g, i: (g, 0, 0))],
        out_specs=pl.BlockSpec((MM_TM, n), lambda g, i: (i, g)),
        out_shape=SDS((M, G * n), out_dtype))(a, b)


def _mm_tn(a, b, G, name):
    S, Ka = a.shape
    n = b.shape[1] // G
    tka = Ka
    while tka * n * 4 > 10 * 2 ** 20 and tka % 256 == 0:
        tka //= 2

    def body(a_ref, b_ref, o_ref):
        @pl.when(pl.program_id(2) == 0)
        def _():
            o_ref[...] = jnp.zeros_like(o_ref)
        o_ref[...] += lax.dot_general(a_ref[...].astype(BF16), b_ref[...].astype(BF16), TN_DIMS, preferred_element_type=F32)

    return pl.pallas_call(
        body, name=name, grid=(G, Ka // tka, S // MM_TM),
        in_specs=[pl.BlockSpec((MM_TM, tka), lambda g, i, k: (k, i)), pl.BlockSpec((MM_TM, n), lambda g, i, k: (k, g))],
        out_specs=pl.BlockSpec((None, tka, n), lambda g, i, k: (g, i, 0)),
        out_shape=SDS((G, Ka, n), F32))(a, b)


NORM_RC = 16


def _norm_in_bwd(a, w, xin, nw, resid, name):
    S, K = xin.shape
    G, _, n = w.shape
    tm = 1024 if S % 1024 == 0 and G * K * n * 2 <= 12 * 2 ** 20 else TM
    nt = S // tm

    def body(a_ref, w_ref, x_ref, nw_ref, r_ref, o_ref, dnw_ref, acc, part):
        i, g = pl.program_id(0), pl.program_id(1)

        @pl.when((i == 0) & (g == 0))
        def _():
            part[...] = jnp.zeros_like(part)

        @pl.when(g == 0)
        def _():
            acc[...] = jnp.zeros_like(acc)

        acc[...] += lax.dot_general(a_ref[...], w_ref[g], NT_DIMS, preferred_element_type=F32)

        @pl.when(g == G - 1)
        def _():
            for r0 in range(0, tm, NORM_RC):
                rows = pl.ds(r0, NORM_RC)
                dn = acc[rows, :]
                xv = x_ref[rows, :]
                r = lax.rsqrt(jnp.mean(xv * xv, axis=-1, keepdims=True) + RMS_EPS)
                xhat = xv * r
                dyw = dn * nw_ref[...]
                o_ref[rows, :] = r_ref[rows, :] + r * (dyw - xhat * jnp.mean(dyw * xhat, axis=-1, keepdims=True))
                part[...] += _fold8(dn * xhat)

        @pl.when((i == nt - 1) & (g == G - 1))
        def _():
            dnw_ref[...] = jnp.sum(part[...], axis=0, keepdims=True)

    row = pl.BlockSpec((tm, K), lambda i, g: (i, 0))
    return pl.pallas_call(
        body, name=name, grid=(nt, G),
        in_specs=[pl.BlockSpec((tm, n), lambda i, g: (i, g)),
                  pl.BlockSpec((G, K, n), lambda i, g: (0, 0, 0), pipeline_mode=pl.Buffered(1)),
                  row, _full((1, K)), row],
        out_specs=[row, _full((1, K))],
        out_shape=[SDS((S, K), F32), SDS((1, K), F32)],
        scratch_shapes=[pltpu.VMEM((tm, K), F32), pltpu.VMEM((8, K), F32)])(a, w, xin, nw, resid)


def _t5_bucket_np(dist):
    max_exact = N_BUCKETS // 2
    d = np.maximum(dist.astype(np.float32), np.float32(1.0))
    large = max_exact + (np.log(d / np.float32(max_exact)) / np.float32(math.log(MAX_DIST / max_exact))
                         * np.float32(N_BUCKETS - max_exact)).astype(np.int32)
    large = np.minimum(large, N_BUCKETS - 1)
    return np.where(dist < max_exact, dist, large).astype(np.int32)


def _bias_static():
    qi = np.arange(NQ)[:, None]
    kj = np.arange(2 * NQ)[None, :]
    step = qi + NQ - kj
    band = (step >= 0) & (step <= NQ)
    buckets = np.stack([_t5_bucket_np(np.clip(step, 0, None) * dil).reshape(1, -1) for _, dil in ATTN_GROUPS])
    return band, buckets


def _bias_fwd(table_t, buckets):
    nb = buckets.shape[-1]

    def body(t_ref, b_ref, o_ref):
        oh = (b_ref[...] == lax.broadcasted_iota(jnp.int32, (N_BUCKETS, nb), 0)).astype(F32)
        o_ref[...] = jnp.dot(t_ref[...], oh, preferred_element_type=F32, precision=lax.Precision.HIGHEST)

    return pl.pallas_call(
        body, name="bias_fwd", grid=(N_GROUPS,),
        in_specs=[pl.BlockSpec((None, 8, N_BUCKETS), lambda g: (g, 0, 0)), pl.BlockSpec((None, 1, nb), lambda g: (g, 0, 0))],
        out_specs=pl.BlockSpec((None, 8, nb), lambda g: (g, 0, 0)),
        out_shape=SDS((N_GROUPS, 8, nb), F32))(table_t, buckets)


def _bias_bwd(dsb, buckets):
    nb = buckets.shape[-1]

    def body(d_ref, b_ref, o_ref):
        oh = (b_ref[...] == lax.broadcasted_iota(jnp.int32, (N_BUCKETS, nb), 0)).astype(F32)
        o_ref[...] = lax.dot_general(d_ref[...], oh, NT_DIMS, preferred_element_type=F32, precision=lax.Precision.HIGHEST)

    return pl.pallas_call(
        body, name="bias_bwd", grid=(N_GROUPS,),
        in_specs=[pl.BlockSpec((None, 8, nb), lambda g: (g, 0, 0)), pl.BlockSpec((None, 1, nb), lambda g: (g, 0, 0))],
        out_specs=pl.BlockSpec((None, 8, N_BUCKETS), lambda g: (g, 0, 0)),
        out_shape=SDS((N_GROUPS, 8, N_BUCKETS), F32))(dsb, buckets)


P4 = 4


def _to_p4(dst_ref, h, val, scr):
    scr[...] = val
    for r in range(P4):
        dst_ref[h, r] = scr[pl.ds(r, TM // P4, stride=P4), :]


def _from_p4(src_ref, h, scr):
    for r in range(P4):
        scr[pl.ds(r, TM // P4, stride=P4), :] = src_ref[h, r]
    return scr[...]


def _qkv_prep(z, qw, kw):
    S = z.shape[0]
    nh = N_GROUPS * HPG

    def body(zq, zk, zv, qw_ref, kw_ref, qh, kh, vh, q2, k2, v2, scr):
        for h in range(nh):
            g = h // HPG
            sl = slice(h * HEAD, (h + 1) * HEAD)
            xq = zq[:, sl].astype(F32)
            q = xq * lax.rsqrt(jnp.mean(xq * xq, axis=-1, keepdims=True) + RMS_EPS) * qw_ref[g:g + 1, :]
            xk = zk[:, sl].astype(F32)
            k = xk * lax.rsqrt(jnp.mean(xk * xk, axis=-1, keepdims=True) + RMS_EPS) * kw_ref[g:g + 1, :]
            v = zv[:, sl].astype(F32)
            if h < HPG:
                qh[h], kh[h], vh[h] = q, k, v
            else:
                _to_p4(q2, h - HPG, q, scr)
                _to_p4(k2, h - HPG, k, scr)
                _to_p4(v2, h - HPG, v, scr)

    hm = pl.BlockSpec((HPG, TM, HEAD), lambda i: (0, i, 0))
    p4 = pl.BlockSpec((2 * HPG, P4, TM // P4, HEAD), lambda i: (0, 0, i, 0))
    return pl.pallas_call(
        body, name="qkv_prep", grid=(S // TM,),
        in_specs=[pl.BlockSpec((TM, QKV_W), lambda i: (i, 0)), pl.BlockSpec((TM, QKV_W), lambda i: (i, 1)),
                  pl.BlockSpec((TM, QKV_W), lambda i: (i, 2)), _full((N_GROUPS, HEAD)), _full((N_GROUPS, HEAD))],
        out_specs=[hm, hm, hm, p4, p4, p4],
        out_shape=[SDS((HPG, S, HEAD), F32)] * 3 + [SDS((2 * HPG, P4, S // P4, HEAD), F32)] * 3,
        scratch_shapes=[pltpu.VMEM((TM, HEAD), F32)])(z, z, z, qw, kw)


def _rows(start, d):
    return pl.ds(start, NQ) if d == 1 else pl.ds(start, NQ, stride=d)


def _attn_fwd(qh, kh, vh, biasm, name, d, head0, nseq, bias_shift):
    S = qh.shape[1]
    RB = min(ATT_RB, S)
    nbk, nq = S // RB, RB // (NQ * d)

    def body(q_ref, k_ref, v_ref, bias_ref, o_ref, lse_ref, kbuf, vbuf):
        b = pl.program_id(1)

        @pl.when(b == 0)
        def _():
            kbuf[0:RB, :] = jnp.zeros((RB, HEAD), F32)
            vbuf[0:RB, :] = jnp.zeros((RB, HEAD), F32)

        @pl.when(b > 0)
        def _():
            kbuf[0:RB, :] = kbuf[RB:2 * RB, :]
            vbuf[0:RB, :] = vbuf[RB:2 * RB, :]

        kbuf[RB:2 * RB, :] = k_ref[...]
        vbuf[RB:2 * RB, :] = v_ref[...]
        bias = bias_ref[...]
        col = lax.broadcasted_iota(jnp.int32, (NQ, 2 * NQ), 1)

        for qb in range(nq):
            def unit(r, carry, qb=qb):
                qs = qb * NQ * d + r
                q = q_ref[_rows(qs, d), :].astype(BF16)
                kw = jnp.concatenate([kbuf[_rows(RB + qs - NQ * d, d), :], kbuf[_rows(RB + qs, d), :]], axis=0).astype(BF16)
                vw = jnp.concatenate([vbuf[_rows(RB + qs - NQ * d, d), :], vbuf[_rows(RB + qs, d), :]], axis=0).astype(BF16)
                s = lax.dot_general(q, kw, NT_DIMS, preferred_element_type=F32) * SCALE + bias
                if qb == 0:
                    s = jnp.where((col < NQ) & (b == 0), NEG, s)
                m = jnp.max(s, axis=-1, keepdims=True)
                p = jnp.exp(s - m)
                l = jnp.sum(p, axis=-1, keepdims=True)
                o = jnp.dot(p.astype(BF16), vw, preferred_element_type=F32) / l
                o_ref[_rows(qs, d), :] = o
                lse_ref[_rows(qs, d), :] = jnp.broadcast_to(m + jnp.log(l), (NQ, HEAD))
                return carry

            for r in range(d):
                unit(r, 0)

    blk = lambda f: pl.BlockSpec((None, RB, HEAD), f)
    return pl.pallas_call(
        body, name=name, grid=(nseq, nbk),
        in_specs=[blk(lambda h, b: (head0 + h, b, 0))] * 3
        + [pl.BlockSpec((None, NQ, 2 * NQ), lambda h, b: (jnp.right_shift(h, bias_shift), 0, 0))],
        out_specs=[blk(lambda h, b: (h, b, 0))] * 2,
        out_shape=[SDS((nseq, S, HEAD), F32)] * 2,
        scratch_shapes=[pltpu.VMEM((2 * RB, HEAD), F32)] * 2)(qh, kh, vh, biasm)


def _attn_bwd(qh, kh, vh, biasm, da, wg, dh, lse, name, d, head0, nseq, bias_shift):
    S = qh.shape[1]
    RB = min(ATT_RB, S)
    nbk, nq = S // RB, RB // (NQ * d)

    def body(q_ref, k_ref, v_ref, bias_ref, da_ref, wg_ref, dh_ref, lse_ref,
             dq_ref, dk_ref, dv_ref, dsb_ref, kbuf, vbuf, dkbuf, dvbuf):
        b = pl.program_id(1)
        zero = jnp.zeros((RB, HEAD), F32)

        @pl.when(b == 0)
        def _():
            kbuf[0:RB, :] = zero
            vbuf[0:RB, :] = zero
            dkbuf[0:RB, :] = zero
            dvbuf[0:RB, :] = zero
            dsb_ref[...] = jnp.zeros_like(dsb_ref)

        @pl.when(b > 0)
        def _():
            kbuf[0:RB, :] = kbuf[RB:2 * RB, :]
            vbuf[0:RB, :] = vbuf[RB:2 * RB, :]
            dkbuf[0:RB, :] = dkbuf[RB:2 * RB, :]
            dvbuf[0:RB, :] = dvbuf[RB:2 * RB, :]

        dkbuf[RB:2 * RB, :] = zero
        dvbuf[RB:2 * RB, :] = zero

        @pl.when(b < nbk)
        def _():
            kbuf[RB:2 * RB, :] = k_ref[...]
            vbuf[RB:2 * RB, :] = v_ref[...]
            bias = bias_ref[...]
            col = lax.broadcasted_iota(jnp.int32, (NQ, 2 * NQ), 1)

            for qb in range(nq):
                def unit(r, carry, qb=qb):
                    qs = qb * NQ * d + r
                    prev, cur = _rows(RB + qs - NQ * d, d), _rows(RB + qs, d)
                    q = q_ref[_rows(qs, d), :].astype(BF16)
                    kw = jnp.concatenate([kbuf[prev, :], kbuf[cur, :]], axis=0).astype(BF16)
                    vw = jnp.concatenate([vbuf[prev, :], vbuf[cur, :]], axis=0).astype(BF16)
                    s = lax.dot_general(q, kw, NT_DIMS, preferred_element_type=F32) * SCALE + bias
                    if qb == 0:
                        s = jnp.where((col < NQ) & (b == 0), NEG, s)
                    p = jnp.exp(s - lse_ref[_rows(qs, d), :][:, 0:1])
                    w = wg_ref[_rows(qs, d), :]
                    do = (da_ref[_rows(qs, d), :] * w).astype(BF16)
                    dp = lax.dot_general(do, vw, NT_DIMS, preferred_element_type=F32)
                    ds = p * (dp - w[:, 0:1] * dh_ref[_rows(qs, d), :][:, 0:1])
                    dsb_ref[...] += ds
                    dsb = ds.astype(BF16)
                    dq_ref[_rows(qs, d), :] = jnp.dot(dsb, kw, preferred_element_type=F32) * SCALE
                    dkw = lax.dot_general(dsb, q, TN_DIMS, preferred_element_type=F32) * SCALE
                    dvw = lax.dot_general(p.astype(BF16), do, TN_DIMS, preferred_element_type=F32)
                    dkbuf[prev, :] += dkw[0:NQ, :]
                    dkbuf[cur, :] += dkw[NQ:2 * NQ, :]
                    dvbuf[prev, :] += dvw[0:NQ, :]
                    dvbuf[cur, :] += dvw[NQ:2 * NQ, :]
                    return carry

                for r in range(d):
                    unit(r, 0)

        dk_ref[...] = dkbuf[done:done + RB, :]
        dv_ref[...] = dvbuf[done:done + RB, :]

    steps, done = (nbk + 1, 0) if nbk > 1 else (1, RB)
    blk = lambda f: pl.BlockSpec((None, RB, HEAD), f)
    cur_g = blk(lambda h, b: (head0 + h, jnp.minimum(b, nbk - 1), 0))
    cur = blk(lambda h, b: (h, jnp.minimum(b, nbk - 1), 0))
    prv = blk(lambda h, b: (h, jnp.maximum(b - 1, 0), 0))
    sq = pl.BlockSpec((None, NQ, 2 * NQ), lambda h, b: (h, 0, 0))
    return pl.pallas_call(
        body, name=name, grid=(nseq, steps),
        in_specs=[cur_g, cur_g, cur_g, pl.BlockSpec((None, NQ, 2 * NQ), lambda h, b: (jnp.right_shift(h, bias_shift), 0, 0)),
                  cur, cur, cur, cur],
        out_specs=[cur, prv, prv, sq],
        out_shape=[SDS((nseq, S, HEAD), F32)] * 3 + [SDS((nseq, NQ, 2 * NQ), F32)],
        scratch_shapes=[pltpu.VMEM((2 * RB, HEAD), F32)] * 4)(qh, kh, vh, biasm, da, wg, dh, lse)


def _merge_weights(l0, l1, l2):
    m = jnp.maximum(jnp.maximum(l0, l1), l2)
    e0, e1, e2 = jnp.exp(l0 - m), jnp.exp(l1 - m), jnp.exp(l2 - m)
    inv = 1.0 / (e0 + e1 + e2)
    return e0 * inv, e1 * inv, e2 * inv


def _merge_fwd(os_, lses):
    S = os_[0].shape[1]

    def body(o0, o1, o2, l0, l1, l2, a_ref, so, sl):
        for h in range(HPG):
            w0, w1, w2 = _merge_weights(l0[h], _from_p4(l1, h, so), _from_p4(l2, h, sl))
            a_ref[:, h * HEAD:(h + 1) * HEAD] = (w0 * o0[h] + w1 * _from_p4(o1, h, so) + w2 * _from_p4(o2, h, sl)).astype(BF16)

    hm = pl.BlockSpec((HPG, TM, HEAD), lambda i: (0, i, 0))
    p4 = pl.BlockSpec((HPG, P4, TM // P4, HEAD), lambda i: (0, 0, i, 0))
    return pl.pallas_call(
        body, name="merge_fwd", grid=(S // TM,), in_specs=[hm, p4, p4, hm, p4, p4],
        out_specs=pl.BlockSpec((TM, CH), lambda i: (i, 0)),
        out_shape=SDS((S, CH), BF16), scratch_shapes=[pltpu.VMEM((TM, HEAD), F32)] * 2)(*os_, *lses)


def _merge_bwd(dattn, os_, lses):
    S = dattn.shape[0]

    def body(da_ref, o0, o1, o2, l0, l1, l2, w0_ref, w1_ref, w2_ref, dah_ref, dh_ref, dah2_ref, dh2_ref, so, sl):
        for h in range(HPG):
            w = _merge_weights(l0[h], _from_p4(l1, h, so), _from_p4(l2, h, sl))
            attn = w[0] * o0[h] + w[1] * _from_p4(o1, h, so) + w[2] * _from_p4(o2, h, sl)
            da = da_ref[:, h * HEAD:(h + 1) * HEAD]
            dh = jnp.broadcast_to(jnp.sum(da * attn, axis=-1, keepdims=True), (TM, HEAD))
            w0_ref[h] = w[0]
            dah_ref[h] = da
            dh_ref[h] = dh
            _to_p4(w1_ref, h, w[1], so)
            _to_p4(w2_ref, h, w[2], so)
            _to_p4(dah2_ref, h, da, so)
            _to_p4(dh2_ref, h, dh, so)

    hm = pl.BlockSpec((HPG, TM, HEAD), lambda i: (0, i, 0))
    p4 = pl.BlockSpec((HPG, P4, TM // P4, HEAD), lambda i: (0, 0, i, 0))
    nat, perm = SDS((HPG, S, HEAD), F32), SDS((HPG, P4, S // P4, HEAD), F32)
    w0, w1, w2, dah, dh, dah2, dh2 = pl.pallas_call(
        body, name="merge_bwd", grid=(S // TM,),
        in_specs=[pl.BlockSpec((TM, CH), lambda i: (i, 0)), hm, p4, p4, hm, p4, p4],
        out_specs=[hm, p4, p4, hm, hm, p4, p4], out_shape=[nat, perm, perm, nat, nat, perm, perm],
        scratch_shapes=[pltpu.VMEM((TM, HEAD), F32)] * 2)(dattn, *os_, *lses)
    return (w0, dah, dh), (w1, dah2, dh2), (w2, dah2, dh2)


def _qkv_bwd(dz, z, dqs, dks, dvs, qw, kw):
    S = z.shape[0]
    nh = N_GROUPS * HPG

    def body(dz_in, zq, zk, *refs):
        del dz_in
        dq_refs, dk_refs, dv_refs = refs[0:3], refs[3:6], refs[6:9]
        qw_ref, kw_ref, dz_ref, dqw_ref, dkw_ref, scr = refs[9:]

        @pl.when(pl.program_id(0) == 0)
        def _():
            dqw_ref[...] = jnp.zeros_like(dqw_ref)
            dkw_ref[...] = jnp.zeros_like(dkw_ref)

        def grad(refs3, g, hh):
            return _from_p4(refs3[g], hh, scr) if g > 0 else refs3[g][hh]

        def nbwd(xr, dy, wr, dwr, h, off):
            g = h // HPG
            x = xr[:, h * HEAD:(h + 1) * HEAD].astype(F32)
            r = lax.rsqrt(jnp.mean(x * x, axis=-1, keepdims=True) + RMS_EPS)
            xhat = x * r
            dyw = dy * wr[g:g + 1, :]
            dz_ref[:, off + h * HEAD:off + (h + 1) * HEAD] = (
                r * (dyw - xhat * jnp.mean(dyw * xhat, axis=-1, keepdims=True))).astype(BF16)
            dwr[g:g + 1, :] += jnp.sum(dy * xhat, axis=0, keepdims=True)

        for h in range(nh):
            g, hh = h // HPG, h % HPG
            nbwd(zq, grad(dq_refs, g, hh), qw_ref, dqw_ref, h, O_Q)
            nbwd(zk, grad(dk_refs, g, hh), kw_ref, dkw_ref, h, O_K)
            dz_ref[:, O_V + h * HEAD:O_V + (h + 1) * HEAD] = grad(dv_refs, g, hh).astype(BF16)

    hm = pl.BlockSpec((HPG, TM, HEAD), lambda i: (0, i, 0))
    p4 = pl.BlockSpec((HPG, P4, TM // P4, HEAD), lambda i: (0, 0, i, 0))
    return pl.pallas_call(
        body, name="qkv_bwd", grid=(S // TM,),
        in_specs=[pl.BlockSpec(memory_space=pl.ANY), pl.BlockSpec((TM, QKV_W), lambda i: (i, 0)),
                  pl.BlockSpec((TM, QKV_W), lambda i: (i, 1))] + [hm, p4, p4] * 3 + [_full((N_GROUPS, HEAD)), _full((N_GROUPS, HEAD))],
        out_specs=[pl.BlockSpec((TM, 3 * QKV_W), lambda i: (i, 0)), _full((N_GROUPS, HEAD)), _full((N_GROUPS, HEAD))],
        out_shape=[SDS(dz.shape, BF16), SDS((N_GROUPS, HEAD), F32), SDS((N_GROUPS, HEAD), F32)],
        scratch_shapes=[pltpu.VMEM((TM, HEAD), F32)],
        input_output_aliases={0: 0})(dz, z, z, *dqs, *dks, *dvs, qw, kw)


def _conv_fwd(z, cw, cb, lnw, lnb):
    S = z.shape[0]
    H = 32

    def body(zv, zg, cw_ref, cb_ref, lnw_ref, lnb_ref, u1_ref, u3_ref, xbuf):
        i = pl.program_id(0)

        @pl.when(i == 0)
        def _():
            xbuf[0:H, :] = jnp.zeros((H, CH), F32)

        @pl.when(i > 0)
        def _():
            xbuf[0:H, :] = xbuf[TM:TM + H, :]

        xbuf[H:H + TM, :] = zv[...].astype(F32) * _sig(zg[...].astype(F32))
        for r0 in range(0, TM, CONV_RC):
            rows = pl.ds(r0, CONV_RC)
            parts = []
            for lanes in CONV_LANES:
                part = [jnp.broadcast_to(cb_ref[:, lanes], (CONV_RC, CONV_LB))]

                def tap(q, view, part=part, lanes=lanes):
                    part[0] = part[0] + view * cw_ref[q - 2:q - 1, lanes]

                _conv_taps(xbuf, r0, lanes, 2, CONV_K + 1, tap)
                parts.append(part[0])
            acc = jnp.concatenate(parts, axis=1)
            u1_ref[rows, :] = acc
            mu = jnp.mean(acc, axis=-1, keepdims=True)
            xc = acc - mu
            yl = xc * lax.rsqrt(jnp.mean(xc * xc, axis=-1, keepdims=True) + LN_EPS) * lnw_ref[...] + lnb_ref[...]
            u3_ref[rows, :] = (yl * _sig(yl)).astype(BF16)

    row = pl.BlockSpec((TM, CH), lambda i: (i, 0))
    return pl.pallas_call(
        body, name="conv_fwd", grid=(S // TM,),
        in_specs=[pl.BlockSpec((TM, CH), lambda i: (i, O_CV // CH)), pl.BlockSpec((TM, CH), lambda i: (i, O_CG // CH)),
                  _full((CONV_K, CH)), _full((1, CH)), _full((1, CH)), _full((1, CH))],
        out_specs=[row, row], out_shape=[SDS((S, CH), F32), SDS((S, CH), BF16)],
        scratch_shapes=[pltpu.VMEM((TM + H, CH), F32)])(z, z, cw, cb, lnw, lnb)


def _conv_bwd_a(du3, u1, z, lnw, lnb):
    S = z.shape[0]
    H = 32

    nt = S // TM

    def body(du3_ref, u1_ref, zv, zg, lnw_ref, lnb_ref, du1_ref, acc_ref, xbuf, tacc):
        i = pl.program_id(0)

        @pl.when(i == 0)
        def _():
            xbuf[0:H, :] = jnp.zeros((H, CH), F32)
            tacc[...] = jnp.zeros_like(tacc)

        @pl.when(i > 0)
        def _():
            xbuf[0:H, :] = xbuf[TM:TM + H, :]

        xbuf[H:H + TM, :] = zv[...].astype(F32) * _sig(zg[...].astype(F32))
        for r0 in range(0, TM, CONV_RC):
            rows = pl.ds(r0, CONV_RC)
            u1 = u1_ref[rows, :]
            mu = jnp.mean(u1, axis=-1, keepdims=True)
            xc = u1 - mu
            r = lax.rsqrt(jnp.mean(xc * xc, axis=-1, keepdims=True) + LN_EPS)
            yhat = xc * r
            yl = yhat * lnw_ref[...] + lnb_ref[...]
            sg = _sig(yl)
            dyl = du3_ref[rows, :] * (sg * (1.0 + yl * (1.0 - sg)))
            dyh = dyl * lnw_ref[...]
            du1 = r * (dyh - jnp.mean(dyh, axis=-1, keepdims=True) - yhat * jnp.mean(dyh * yhat, axis=-1, keepdims=True))
            du1_ref[rows, :] = du1
            tacc[33] += _fold8(dyl * yhat)
            tacc[34] += _fold8(dyl)
            tacc[32] += _fold8(du1)
            for lanes in CONV_LANES:
                d = du1[:, lanes]

                def tap(q, view, d=d, lanes=lanes):
                    tacc[q - 2, :, lanes] += _fold8(d * view)

                _conv_taps(xbuf, r0, lanes, 2, CONV_K + 1, tap)

        @pl.when(i == nt - 1)
        def _():
            for k in range(40):
                acc_ref[k:k + 1, :] = jnp.sum(tacc[k], axis=0, keepdims=True)

    row = pl.BlockSpec((TM, CH), lambda i: (i, 0))
    return pl.pallas_call(
        body, name="conv_bwd_a", grid=(nt,),
        in_specs=[row, row, pl.BlockSpec((TM, CH), lambda i: (i, O_CV // CH)), pl.BlockSpec((TM, CH), lambda i: (i, O_CG // CH)),
                  _full((1, CH)), _full((1, CH))],
        out_specs=[row, _full((40, CH))], out_shape=[SDS((S, CH), F32), SDS((40, CH), F32)],
        scratch_shapes=[pltpu.VMEM((TM + H, CH), F32), pltpu.VMEM((40, 8, CH), F32)])(du3, u1, z, z, lnw, lnb)


def _conv_bwd_b(dz, du1, z, cw):
    S = z.shape[0]
    nt = S // TM
    H = 32

    def body(dz_in, du1_ref, zv, zg, cw_ref, dz_ref, ybuf, dgate):
        del dz_in
        i, p = pl.program_id(0), pl.program_id(1)

        @pl.when(p == 0)
        def _():
            @pl.when(i == 0)
            def _():
                ybuf[TM:TM + H, :] = jnp.zeros((H, CH), F32)

            @pl.when(i > 0)
            def _():
                ybuf[TM:TM + H, :] = ybuf[0:H, :]

            ybuf[0:TM, :] = du1_ref[...]
            for r0 in range(0, TM, CONV_RC):
                rows = pl.ds(r0, CONV_RC)
                parts = []
                for lanes in CONV_LANES:
                    part = [jnp.zeros((CONV_RC, CONV_LB), F32)]

                    def tap(q, view, part=part, lanes=lanes):
                        part[0] = part[0] + view * cw_ref[CONV_K - 1 - q:CONV_K - q, lanes]

                    _conv_taps(ybuf, r0, lanes, 0, CONV_K - 1, tap)
                    parts.append(part[0])
                acc = jnp.concatenate(parts, axis=1)
                val = zv[rows, :].astype(F32)
                sg = _sig(zg[rows, :].astype(F32))
                dz_ref[rows, :] = (acc * sg).astype(BF16)
                dgate[rows, :] = (acc * val * sg * (1.0 - sg)).astype(BF16)

        @pl.when(p == 1)
        def _():
            dz_ref[...] = dgate[...]

    rev = lambda c: pl.BlockSpec((TM, CH), lambda i, p: (nt - 1 - i, c))
    return pl.pallas_call(
        body, name="conv_bwd_b", grid=(nt, 2),
        in_specs=[pl.BlockSpec(memory_space=pl.ANY), rev(0), rev(O_CV // CH), rev(O_CG // CH), _full((CONV_K, CH))],
        out_specs=pl.BlockSpec((TM, CH), lambda i, p: (nt - 1 - i, O_CV // CH + p)),
        out_shape=SDS(dz.shape, BF16),
        scratch_shapes=[pltpu.VMEM((TM + H, CH), F32), pltpu.VMEM((TM, CH), BF16)],
        input_output_aliases={0: 0})(dz, du1, z, z, cw)


def _memkv_fwd(mem, mnw, wkv, xkw):
    M, D = mem.shape

    def body(mem_ref, mnw_ref, w_ref, xkw_ref, mn_ref, kv_ref, mk_ref, mv_ref):
        x = mem_ref[...]
        mn = (x * lax.rsqrt(jnp.mean(x * x, axis=-1, keepdims=True) + RMS_EPS) * mnw_ref[...]).astype(BF16)
        mn_ref[...] = mn
        kv = jnp.dot(mn, w_ref[...], preferred_element_type=F32)
        kv_ref[...] = kv
        for h in range(HPG):
            k = kv[:, h * HEAD:(h + 1) * HEAD]
            mk_ref[:, h * HEAD:(h + 1) * HEAD] = (
                k * lax.rsqrt(jnp.mean(k * k, axis=-1, keepdims=True) + RMS_EPS) * xkw_ref[...]).astype(BF16)
        mv_ref[...] = kv[:, CH:2 * CH].astype(BF16)

    return pl.pallas_call(
        body, name="memkv_fwd",
        out_shape=[SDS((M, D), BF16), SDS((M, 2 * CH), F32), SDS((M, CH), BF16), SDS((M, CH), BF16)])(mem, mnw, wkv, xkw)


def _cross_q(zx, xqw, h):
    x = zx[:, h * HEAD:(h + 1) * HEAD].astype(F32)
    r = lax.rsqrt(jnp.mean(x * x, axis=-1, keepdims=True) + RMS_EPS)
    xhat = x * r
    return xhat, r, xhat * xqw


def _cross_fwd(z, xqw, mk, mv):
    S = z.shape[0]
    M = mk.shape[0]

    def body(zx, xqw_ref, mk_ref, mv_ref, o_ref):
        for h in range(HPG):
            sl = slice(h * HEAD, (h + 1) * HEAD)
            _, _, q = _cross_q(zx, xqw_ref[...], h)
            s = lax.dot_general(q.astype(BF16), mk_ref[:, sl], NT_DIMS, preferred_element_type=F32) * SCALE
            e = jnp.exp(s - jnp.max(s, axis=-1, keepdims=True))
            p = e / jnp.sum(e, axis=-1, keepdims=True)
            o_ref[:, sl] = jnp.dot(p.astype(BF16), mv_ref[:, sl], preferred_element_type=F32).astype(BF16)

    return pl.pallas_call(
        body, name="cross_fwd", grid=(S // TM,),
        in_specs=[pl.BlockSpec((TM, CH), lambda i: (i, O_XQ // CH)), _full((1, HEAD)), _full((M, CH)), _full((M, CH))],
        out_specs=pl.BlockSpec((TM, CH), lambda i: (i, 0)), out_shape=SDS((S, CH), BF16))(z, xqw, mk, mv)


def _cross_bwd(dz, doc, z, xqw, mk, mv):
    S = z.shape[0]
    M = mk.shape[0]

    def body(dz_in, do_ref, zx, xqw_ref, mk_ref, mv_ref, dz_ref, dmk_ref, dmv_ref, dxw_ref):
        del dz_in

        @pl.when(pl.program_id(0) == 0)
        def _():
            dmk_ref[...] = jnp.zeros_like(dmk_ref)
            dmv_ref[...] = jnp.zeros_like(dmv_ref)
            dxw_ref[...] = jnp.zeros_like(dxw_ref)

        for h in range(HPG):
            sl = slice(h * HEAD, (h + 1) * HEAD)
            xhat, r, q = _cross_q(zx, xqw_ref[...], h)
            qb = q.astype(BF16)
            s = lax.dot_general(qb, mk_ref[:, sl], NT_DIMS, preferred_element_type=F32) * SCALE
            e = jnp.exp(s - jnp.max(s, axis=-1, keepdims=True))
            p = e / jnp.sum(e, axis=-1, keepdims=True)
            do = do_ref[:, sl].astype(BF16)
            dp = lax.dot_general(do, mv_ref[:, sl], NT_DIMS, preferred_element_type=F32)
            ds = (p * (dp - jnp.sum(p * dp, axis=-1, keepdims=True)) * SCALE).astype(BF16)
            dq = jnp.dot(ds, mk_ref[:, sl], preferred_element_type=F32)
            dmk_ref[:, sl] += lax.dot_general(ds, qb, TN_DIMS, preferred_element_type=F32)
            dmv_ref[:, sl] += lax.dot_general(p.astype(BF16), do, TN_DIMS, preferred_element_type=F32)
            dyw = dq * xqw_ref[...]
            dz_ref[:, sl] = (r * (dyw - xhat * jnp.mean(dyw * xhat, axis=-1, keepdims=True))).astype(BF16)
            dxw_ref[...] += jnp.sum(dq * xhat, axis=0, keepdims=True)

    return pl.pallas_call(
        body, name="cross_bwd", grid=(S // TM,),
        in_specs=[pl.BlockSpec(memory_space=pl.ANY), pl.BlockSpec((TM, CH), lambda i: (i, 0)),
                  pl.BlockSpec((TM, CH), lambda i: (i, O_XQ // CH)), _full((1, HEAD)), _full((M, CH)), _full((M, CH))],
        out_specs=[pl.BlockSpec((TM, CH), lambda i: (i, O_XQ // CH)), _full((M, CH)), _full((M, CH)), _full((1, HEAD))],
        out_shape=[SDS(dz.shape, BF16), SDS((M, CH), F32), SDS((M, CH), F32), SDS((1, HEAD), F32)],
        input_output_aliases={0: 0})(dz, doc, z, xqw, mk, mv)


def _memkv_bwd(dmk, dmv, kv, mem, mn, mnw, wkv, xkw):
    M, D = mem.shape

    def body(dmk_ref, dmv_ref, kv_ref, mem_ref, mn_ref, mnw_ref, w_ref, xkw_ref, dw_ref, dxk_ref, dmn_ref, dkv):
        dxk = jnp.zeros((1, HEAD), F32)
        for h in range(HPG):
            sl = slice(h * HEAD, (h + 1) * HEAD)
            k = kv_ref[:, sl]
            r = lax.rsqrt(jnp.mean(k * k, axis=-1, keepdims=True) + RMS_EPS)
            khat = k * r
            dy = dmk_ref[:, sl]
            dyw = dy * xkw_ref[...]
            dkv[:, sl] = (r * (dyw - khat * jnp.mean(dyw * khat, axis=-1, keepdims=True))).astype(BF16)
            dxk = dxk + jnp.sum(dy * khat, axis=0, keepdims=True)
        dxk_ref[...] = dxk
        dkv[:, CH:2 * CH] = dmv_ref[...].astype(BF16)
        dw_ref[...] = lax.dot_general(mn_ref[...], dkv[...], TN_DIMS, preferred_element_type=F32)
        dn = lax.dot_general(dkv[...], w_ref[...], NT_DIMS, preferred_element_type=F32)
        x = mem_ref[...]
        r = lax.rsqrt(jnp.mean(x * x, axis=-1, keepdims=True) + RMS_EPS)
        dmn_ref[...] = jnp.sum(dn * x * r, axis=0, keepdims=True)

    return pl.pallas_call(
        body, name="memkv_bwd",
        out_shape=[SDS((D, 2 * CH), F32), SDS((1, HEAD), F32), SDS((1, D), F32)],
        scratch_shapes=[pltpu.VMEM((M, 2 * CH), BF16)])(dmk, dmv, kv, mem, mn, mnw, wkv, xkw)


def _branch_proj(a_ref, w_ref, y_ref):
    G, _, n = w_ref.shape
    a = a_ref[...]
    for g in range(G):
        y_ref[:, g * n:(g + 1) * n] = jnp.dot(a, w_ref[g], preferred_element_type=F32)


OUT_RC = 16


def _gates(zg_ref, bg_ref, rows, k, D):
    return _sig(zg_ref[rows, k * D:(k + 1) * D].astype(F32) + bg_ref[:, k * D:(k + 1) * D])


def _outproj_fwd(x, z, bg, attn, u3, oc, wao, wco, wxo, wout, fnw):
    S, D = x.shape
    tm = TM

    def body(x_ref, zg_ref, bg_ref, a_ref, u_ref, c_ref, wa, wc, wx, wo, fnw_ref, h1_ref, hn_ref, ya, yc, yx, mg):
        _branch_proj(a_ref, wa, ya)
        _branch_proj(u_ref, wc, yc)
        _branch_proj(c_ref, wx, yx)
        for r0 in range(0, tm, OUT_RC):
            rows = pl.ds(r0, OUT_RC)
            mg[rows, :] = (_gates(zg_ref, bg_ref, rows, 0, D) * ya[rows, :] + _gates(zg_ref, bg_ref, rows, 1, D) * yc[rows, :]
                           + _gates(zg_ref, bg_ref, rows, 2, D) * yx[rows, :]).astype(BF16)
        ya[...] = jnp.dot(mg[...], wo[...], preferred_element_type=F32)
        for r0 in range(0, tm, OUT_RC):
            rows = pl.ds(r0, OUT_RC)
            h1 = x_ref[rows, :] + ya[rows, :]
            h1_ref[rows, :] = h1
            hn_ref[rows, :] = (h1 * lax.rsqrt(jnp.mean(h1 * h1, axis=-1, keepdims=True) + RMS_EPS) * fnw_ref[...]).astype(BF16)

    row = lambda w: pl.BlockSpec((tm, w), lambda i: (i, 0))
    return pl.pallas_call(
        body, name="outproj_fwd", grid=(S // tm,),
        in_specs=[row(D), pl.BlockSpec((tm, 3 * D), lambda i: (i, O_G // (3 * D))), _full((1, 3 * D)), row(CH), row(CH), row(CH),
                  _full(wao.shape), _full(wco.shape), _full(wxo.shape), _full((D, D)), _full((1, D))],
        out_specs=[row(D), row(D)], out_shape=[SDS((S, D), F32), SDS((S, D), BF16)],
        scratch_shapes=[pltpu.VMEM((tm, D), F32)] * 3 + [pltpu.VMEM((tm, D), BF16)])(x, z, bg, attn, u3, oc, wao, wco, wxo, wout, fnw)


def _outproj_bwd(dh1, z, bg, attn, u3, oc, wao, wco, wxo, wout, n_in):
    S, D = dh1.shape
    tm = 256
    nt = S // tm
    G, _, n = wao.shape

    def body(dh_ref, zg_ref, bg_ref, a_ref, u_ref, c_ref, wa, wc, wx, wo,
             dz_ref, da_ref, du_ref, dc_ref, dbg_ref, dwo_ref, dwa_ref, dwc_ref, dwx_ref,
             ya, yc, yx, dm, dy, mg, bacc, wacc):
        i = pl.program_id(0)

        @pl.when(i == 0)
        def _():
            bacc[...] = jnp.zeros_like(bacc)
            wacc[...] = jnp.zeros_like(wacc)
            dwo_ref[...] = jnp.zeros_like(dwo_ref)

        _branch_proj(a_ref, wa, ya)
        _branch_proj(u_ref, wc, yc)
        _branch_proj(c_ref, wx, yx)
        dhb = dh_ref[...].astype(BF16)
        dm[...] = lax.dot_general(dhb, wo[...], NT_DIMS, preferred_element_type=F32)
        for r0 in range(0, tm, OUT_RC):
            rows = pl.ds(r0, OUT_RC)
            dmv = dm[rows, :]
            merged = jnp.zeros((OUT_RC, D), F32)
            for k, y in enumerate((ya, yc, yx)):
                gk = _gates(zg_ref, bg_ref, rows, k, D)
                yk = y[rows, :]
                merged = merged + gk * yk
                dzg = dmv * yk * gk * (1.0 - gk)
                dz_ref[rows, k * D:(k + 1) * D] = dzg.astype(BF16)
                bacc[:, k * D:(k + 1) * D] += _fold8(dzg)
                dy[k, rows, :] = (dmv * gk).astype(BF16)
            mg[rows, :] = merged.astype(BF16)
        dwo_ref[...] += lax.dot_general(mg[...], dhb, TN_DIMS, preferred_element_type=F32)
        for k, (b_ref, w_ref, db_ref) in enumerate(((a_ref, wa, da_ref), (u_ref, wc, du_ref), (c_ref, wx, dc_ref))):
            dyk = dy[k]
            acc = jnp.zeros((tm, CH), F32)
            for g in range(G):
                acc = acc + lax.dot_general(dyk[:, g * n:(g + 1) * n], w_ref[g], NT_DIMS, preferred_element_type=F32)
            db_ref[...] = acc
            wacc[k] += lax.dot_general(b_ref[...], dyk, TN_DIMS, preferred_element_type=F32)

        @pl.when(i == nt - 1)
        def _():
            dbg_ref[...] = jnp.sum(bacc[...], axis=0, keepdims=True)
            for k, dw_ref in enumerate((dwa_ref, dwc_ref, dwx_ref)):
                for g in range(G):
                    dw_ref[g] = wacc[k, :, g * n:(g + 1) * n]

    row = lambda w: pl.BlockSpec((tm, w), lambda i: (i, 0))
    return pl.pallas_call(
        body, name="outproj_bwd", grid=(nt,),
        in_specs=[row(D), pl.BlockSpec((tm, 3 * D), lambda i: (i, O_G // (3 * D))), _full((1, 3 * D)), row(CH), row(CH), row(CH),
                  _full(wao.shape), _full(wco.shape), _full(wxo.shape), _full((D, D))],
        out_specs=[pl.BlockSpec((tm, 3 * D), lambda i: (i, O_G // (3 * D))), row(CH), row(CH), row(CH), _full((1, 3 * D)),
                   _full((D, D))] + [_full(wao.shape)] * 3,
        out_shape=[SDS((S, n_in), BF16)] + [SDS((S, CH), F32)] * 3 + [SDS((1, 3 * D), F32), SDS((D, D), F32)]
        + [SDS(wao.shape, F32)] * 3,
        scratch_shapes=[pltpu.VMEM((tm, D), F32)] * 4 + [pltpu.VMEM((3, tm, D), BF16), pltpu.VMEM((tm, D), BF16),
                                                        pltpu.VMEM((8, 3 * D), F32), pltpu.VMEM((3, CH, D), F32)],
    )(dh1, z, bg, attn, u3, oc, wao, wco, wxo, wout)


FFN_TC = 256
FFN_H = 8


def _ffn_taps(buf, r0):
    xx = buf[pl.ds(r0, FFN_RC + FFN_H), :]
    return xx[FFN_H:], pltpu.roll(xx, 1, 0)[FFN_H:], pltpu.roll(xx, 2, 0)[FFN_H:]


def _ffn_conv(taps, w_ref, b_ref):
    x0, x1, x2 = taps
    return b_ref[...] + x0 * w_ref[2:3, :] + x1 * w_ref[1:2, :] + x2 * w_ref[0:1, :]


def _ffn_fwd(up, cw, cb, wdown, h1, target):
    S, D = h1.shape
    F2 = up.shape[1]
    F = F2 // 2
    nj = F // FFN_TC
    tm = TM

    def body(up_ref, halo_ref, cw_ref, cb_ref, w_ref, h_ref, t_ref, dy_ref, loss_ref, ac_ref, gc_ref, abuf, gbuf, act_s):
        i = pl.program_id(0)

        @pl.when(i == 0)
        def _():
            loss_ref[...] = jnp.zeros_like(loss_ref)

        for j in range(nj):
            ca, cg = slice(j * FFN_TC, (j + 1) * FFN_TC), slice(F + j * FFN_TC, F + (j + 1) * FFN_TC)
            first = i == 0
            abuf[0:FFN_H, :] = jnp.where(first, 0.0, halo_ref[:, ca].astype(F32))
            gbuf[0:FFN_H, :] = jnp.where(first, 0.0, halo_ref[:, cg].astype(F32))
            abuf[FFN_H:FFN_H + tm, :] = up_ref[:, ca].astype(F32)
            gbuf[FFN_H:FFN_H + tm, :] = up_ref[:, cg].astype(F32)
            for r0 in range(0, tm, FFN_RC):
                rows = pl.ds(r0, FFN_RC)
                a = _ffn_conv(_ffn_taps(abuf, r0), cw_ref[:, ca], cb_ref[:, ca])
                gt = _ffn_conv(_ffn_taps(gbuf, r0), cw_ref[:, cg], cb_ref[:, cg])
                ac_ref[rows, ca] = a.astype(BF16)
                gc_ref[rows, ca] = gt.astype(BF16)
                act_s[rows, ca] = (gt * _sig(gt) * a).astype(BF16)
        err = h_ref[...] + jnp.dot(act_s[...], w_ref[...], preferred_element_type=F32) - t_ref[...]
        dy_ref[...] = err * (1.0 / D)
        loss_ref[...] += 0.5 * jnp.sum(jnp.mean(err * err, axis=-1, keepdims=True))

    row = lambda w: pl.BlockSpec((tm, w), lambda i: (i, 0))
    halo = pl.BlockSpec((FFN_H, F2), lambda i: (jnp.maximum(i * (tm // FFN_H) - 1, 0), 0))
    return pl.pallas_call(
        body, name="ffn_fwd", grid=(S // tm,),
        in_specs=[row(F2), halo, _full((FFN_K, F2)), _full((1, F2)), _full((F, D)), row(D), row(D)],
        out_specs=[row(D), _full((8, 128)), row(F), row(F)],
        out_shape=[SDS((S, D), F32), SDS((8, 128), F32), SDS((S, F), BF16), SDS((S, F), BF16)],
        scratch_shapes=[pltpu.VMEM((tm + FFN_H, FFN_TC), F32)] * 2 + [pltpu.VMEM((tm, F), BF16)])(up, up, cw, cb, wdown, h1, target)


def _ffn_bwd_a(dy, wdown, ac, gc):
    S, D = dy.shape
    F = ac.shape[1]
    nj = F // FFN_TC
    tm = 1024 if S % 1024 == 0 else TM

    def body(dy_ref, wd_ref, a_ref, g_ref, da_ref, dg_ref, acca_ref, accg_ref, dwd_ref, dact_s, act_s):
        i, j = pl.program_id(0), pl.program_id(1)

        @pl.when((i == 0) & (j == 0))
        def _():
            acca_ref[...] = jnp.zeros_like(acca_ref)
            accg_ref[...] = jnp.zeros_like(accg_ref)
            dwd_ref[...] = jnp.zeros_like(dwd_ref)

        dyb = dy_ref[...].astype(BF16)
        dact_s[...] = lax.dot_general(dyb, wd_ref[...], NT_DIMS, preferred_element_type=F32)
        pa = pg = jnp.zeros((8, FFN_TC), F32)
        for r0 in range(0, tm, FFN_RC):
            rows = pl.ds(r0, FFN_RC)
            a = a_ref[rows, :].astype(F32)
            gt = g_ref[rows, :].astype(F32)
            dact = dact_s[rows, :]
            sg = _sig(gt)
            silu = gt * sg
            act_s[rows, :] = (silu * a).astype(BF16)
            dac = dact * silu
            dgc = dact * a * (sg * (1.0 + gt * (1.0 - sg)))
            da_ref[rows, :] = dac.astype(BF16)
            dg_ref[rows, :] = dgc.astype(BF16)
            pa = pa + _fold8(dac)
            pg = pg + _fold8(dgc)
        acca_ref[j] += pa
        accg_ref[j] += pg
        dwd_ref[pl.ds(pl.multiple_of(j * FFN_TC, FFN_TC), FFN_TC), :] += lax.dot_general(
            act_s[...], dyb, TN_DIMS, preferred_element_type=F32)

    col = pl.BlockSpec((tm, FFN_TC), lambda i, j: (i, j))
    return pl.pallas_call(
        body, name="ffn_bwd_a", grid=(S // tm, nj),
        in_specs=[pl.BlockSpec((tm, D), lambda i, j: (i, 0)), pl.BlockSpec((FFN_TC, D), lambda i, j: (j, 0)), col, col],
        out_specs=[col, col] + [_full((nj, 8, FFN_TC))] * 2 + [_full((F, D))],
        out_shape=[SDS((S, F), BF16)] * 2 + [SDS((nj, 8, FFN_TC), F32)] * 2 + [SDS((F, D), F32)],
        scratch_shapes=[pltpu.VMEM((tm, FFN_TC), F32), pltpu.VMEM((tm, FFN_TC), BF16)])(dy, wdown, ac, gc)


def _ffn_bwd_b(dca, dcg, up, cw):
    S, F = dca.shape
    nj = F // FFN_TC
    tm = 4096 if S % 4096 == 0 else TM
    nt = S // tm
    span = FFN_RC + FFN_H

    def body(a_ref, g_ref, u_ref, w_ref, o_ref, tacc_ref, ybuf):
        j, i = pl.program_id(0), pl.program_id(1)

        @pl.when(i == 0)
        def _():
            ybuf[tm:tm + FFN_H, :] = jnp.zeros((FFN_H, FFN_TC), F32)
            tacc_ref[...] = jnp.zeros_like(tacc_ref)

        @pl.when(i > 0)
        def _():
            ybuf[tm:tm + FFN_H, :] = ybuf[0:FFN_H, :]

        ybuf[0:tm, :] = jnp.where(j < nj, a_ref[...], g_ref[...]).astype(F32)
        p = [jnp.zeros((8, FFN_TC), F32)] * FFN_K
        for r0 in range(0, tm, FFN_RC):
            rows = pl.ds(r0, FFN_RC)
            yy = ybuf[pl.ds(r0, span), :]
            ys = (yy[:FFN_RC], pltpu.roll(yy, span - 1, 0)[:FFN_RC], pltpu.roll(yy, span - 2, 0)[:FFN_RC])
            o_ref[rows, :] = (ys[0] * w_ref[2:3, :] + ys[1] * w_ref[1:2, :] + ys[2] * w_ref[0:1, :]).astype(BF16)
            u = u_ref[rows, :].astype(F32)
            for k in range(FFN_K):
                p[k] = p[k] + _fold8(ys[FFN_K - 1 - k] * u)
        for k in range(FFN_K):
            tacc_ref[k] += p[k]

    rev = lambda f: pl.BlockSpec((tm, FFN_TC), lambda j, i: (nt - 1 - i, f(j)))
    return pl.pallas_call(
        body, name="ffn_bwd_b", grid=(2 * nj, nt),
        in_specs=[rev(lambda j: jnp.minimum(j, nj - 1)), rev(lambda j: jnp.maximum(j - nj, 0)), rev(lambda j: j),
                  pl.BlockSpec((FFN_K, FFN_TC), lambda j, i: (0, j))],
        out_specs=[rev(lambda j: j), pl.BlockSpec((None, FFN_K, 8, FFN_TC), lambda j, i: (j, 0, 0, 0))],
        out_shape=[SDS((S, 2 * F), BF16), SDS((2 * nj, FFN_K, 8, FFN_TC), F32)],
        scratch_shapes=[pltpu.VMEM((tm + FFN_H, FFN_TC), F32)])(dca, dcg, up, cw)


def _adamw_update(w_ref, g_ref, m_ref, v_ref, d_ref, nm_ref, nv_ref):
    gv = g_ref[...]
    m2 = ADAM_B1 * m_ref[...] + (1.0 - ADAM_B1) * gv
    v2 = ADAM_B2 * v_ref[...] + (1.0 - ADAM_B2) * jnp.square(gv)
    m_hat = m2 / (1.0 - ADAM_B1 ** ADAM_STEP)
    v_hat = v2 / (1.0 - ADAM_B2 ** ADAM_STEP)
    d_ref[...] = -ADAM_LR * (m_hat / (jnp.sqrt(v_hat) + ADAM_EPS) + ADAM_WD * w_ref[...])
    nm_ref[...] = m2
    nv_ref[...] = v2


def _adamw_small(ws, gs, ms, vs):
    n = len(ws)

    def body(*refs):
        for i in range(n):
            _adamw_update(*[refs[k * n + i] for k in range(7)])

    shapes = [SDS(w.shape, F32) for w in ws]
    res = pl.pallas_call(body, name="adamw_small", out_shape=shapes * 3)(*ws, *gs, *ms, *vs)
    return res[:n], res[n:2 * n], res[2 * n:]


def _adamw(w, g, m, v, name):
    R, C = w.shape
    tr = _row_tile(R, max(8, (2 ** 20) // (4 * C) // 8 * 8))

    def body(w_ref, g_ref, m_ref, v_ref, d_ref, nm_ref, nv_ref):
        _adamw_update(w_ref, g_ref, m_ref, v_ref, d_ref, nm_ref, nv_ref)

    blk = pl.BlockSpec((tr, C), lambda i: (i, 0))
    return pl.pallas_call(
        body, name=name, grid=(R // tr,), in_specs=[blk] * 4, out_specs=[blk] * 3,
        out_shape=[SDS((R, C), F32)] * 3)(w, g, m, v)


HBM_SPEC = pl.BlockSpec(memory_space=pltpu.HBM)
SEM_SPEC = pl.BlockSpec(memory_space=pltpu.SEMAPHORE)
DATAFLOW_EFFECT = pltpu.SideEffectType.DATAFLOW_SIDE_EFFECTING


def _position():
    return lax.axis_index("x"), lax.axis_index("y"), lax.axis_index("c")


def _other_chips(x, y):
    return [(1 - x, y), (x, 1 - y), (1 - x, 1 - y)]


def _all_gather_xy(arrs, name):
    n = len(arrs)
    hbm = pl.BlockSpec(memory_space=pl.ANY)

    def body(*refs):
        ins, outs = refs[:n], refs[n:2 * n]
        send_sems, recv_sems = refs[2 * n:]
        x, y, c = _position()
        me = 2 * x + y
        chips = _other_chips(x, y)

        def rcopy(i, k, src, dst, to):
            return pltpu.make_async_remote_copy(src_ref=src, dst_ref=dst, send_sem=send_sems.at[i, k], recv_sem=recv_sems.at[i, k],
                                                device_id=to, device_id_type=MESH)

        sends = []
        for i in range(n):
            own = rcopy(i, 6, ins[i], outs[i].at[me], (x, y, 1 - c))
            own.start()
            sends.append(own)
        for i in range(n):
            for j, (px, py) in enumerate(chips):
                cp = rcopy(i, j, ins[i].at[c], outs[i].at[me, c], (px, py, c))
                cp.start()
                sends.append(cp)
        for i in range(n):
            for j, (px, py) in enumerate(chips):
                got = outs[i].at[2 * px + py, c]
                rcopy(i, j, ins[i].at[c], got, (x, y, c)).wait_recv()
                fwd = rcopy(i, 3 + j, got, got, (x, y, 1 - c))
                fwd.start()
                sends.append(fwd)
        for i in range(n):
            for j, (px, py) in enumerate(chips):
                theirs = outs[i].at[2 * px + py, 1 - c]
                rcopy(i, 3 + j, theirs, theirs, (x, y, c)).wait_recv()
        for i in range(n):
            rcopy(i, 6, ins[i], outs[i].at[me], (x, y, c)).wait_recv()
        for cp in sends:
            cp.wait_send()

    return pl.pallas_call(
        body, name=name, in_specs=[hbm] * n, out_specs=[hbm] * n,
        out_shape=[SDS((4,) + a.shape, a.dtype) for a in arrs],
        scratch_shapes=[pltpu.SemaphoreType.DMA((n, 7)), pltpu.SemaphoreType.DMA((n, 7))])(*arrs)


def _ag_ici_start(arrs, name):
    n = len(arrs)

    def body(*refs):
        ins, lands = refs[:n], refs[n:2 * n]
        send_sems, recv_sems = refs[2 * n:2 * n + 2]
        token = refs[-1]
        x, y, c = _position()
        for i in range(n):
            for j, (px, py) in enumerate(_other_chips(x, y)):
                pltpu.make_async_remote_copy(src_ref=ins[i].at[c], dst_ref=lands[i].at[2 * x + y, c], send_sem=send_sems.at[3 * i + j],
                                             recv_sem=recv_sems.at[3 * i + j], device_id=(px, py, c), device_id_type=MESH).start()
        token[...] = jnp.zeros_like(token)

    lands = [lax.empty((4,) + a.shape, a.dtype) for a in arrs]
    res = pl.pallas_call(
        body, name=name,
        out_shape=[pltpu.SemaphoreType.DMA((3 * n,)), pltpu.SemaphoreType.DMA((3 * n,))]
        + [pltpu.HBM(a.shape, a.dtype) for a in arrs] + [pltpu.HBM(l.shape, l.dtype) for l in lands] + [SDS((8, 128), F32)],
        in_specs=[HBM_SPEC] * (2 * n), out_specs=[SEM_SPEC, SEM_SPEC] + [HBM_SPEC] * (2 * n) + [pl.BlockSpec(memory_space=pltpu.VMEM)],
        input_output_aliases={i: 2 + i for i in range(2 * n)},
        compiler_params=pltpu.CompilerParams(has_side_effects=DATAFLOW_EFFECT),
    )(*[pltpu.with_memory_space_constraint(a, pltpu.HBM) for a in list(arrs) + lands])
    return res[0], res[1], list(res[2:2 + n]), list(res[2 + n:2 + 2 * n]), res[-1]


def _ag_ici_wait(send_sems, recv_sems, ins, lands, after, name):
    n = len(ins)

    def body(*refs):
        ins_r, lands_r = refs[:n], refs[n:2 * n]
        send_r, recv_r = refs[2 * n:2 * n + 2]
        x, y, c = _position()
        for i in range(n):
            for j, (px, py) in enumerate(_other_chips(x, y)):
                cp = pltpu.make_async_remote_copy(src_ref=ins_r[i].at[c], dst_ref=lands_r[i].at[2 * px + py, c], send_sem=send_r.at[3 * i + j],
                                                  recv_sem=recv_r.at[3 * i + j], device_id=(px, py, c), device_id_type=MESH)
                cp.wait_send()
                cp.wait_recv()

    res = pl.pallas_call(
        body, name=name,
        out_shape=[pltpu.HBM(a.shape, a.dtype) for a in list(ins) + list(lands)],
        in_specs=[HBM_SPEC] * (2 * n) + [SEM_SPEC, SEM_SPEC, pl.BlockSpec(memory_space=pl.ANY)], out_specs=[HBM_SPEC] * (2 * n),
        input_output_aliases={i: i for i in range(2 * n)},
        compiler_params=pltpu.CompilerParams(has_side_effects=DATAFLOW_EFFECT),
    )(*ins, *lands, send_sems, recv_sems, after)
    return list(res[:n]), list(res[n:])


def _ag_finish_start(arrs, lands, name):
    n = len(arrs)

    def body(*refs):
        ins, bufs = refs[:n], refs[n:2 * n]
        send_sems, recv_sems = refs[2 * n:2 * n + 2]
        token = refs[-1]
        x, y, c = _position()
        for i in range(n):
            pltpu.make_async_remote_copy(src_ref=ins[i], dst_ref=bufs[i].at[2 * x + y], send_sem=send_sems.at[4 * i + 3],
                                         recv_sem=recv_sems.at[4 * i + 3], device_id=(x, y, 1 - c), device_id_type=MESH).start()
            for j, (px, py) in enumerate(_other_chips(x, y)):
                half = bufs[i].at[2 * px + py, c]
                pltpu.make_async_remote_copy(src_ref=half, dst_ref=half, send_sem=send_sems.at[4 * i + j],
                                             recv_sem=recv_sems.at[4 * i + j], device_id=(x, y, 1 - c), device_id_type=MESH).start()
        token[...] = jnp.zeros_like(token)

    res = pl.pallas_call(
        body, name=name,
        out_shape=[pltpu.SemaphoreType.DMA((4 * n,)), pltpu.SemaphoreType.DMA((4 * n,))]
        + [pltpu.HBM(a.shape, a.dtype) for a in list(arrs) + list(lands)] + [SDS((8, 128), F32)],
        in_specs=[HBM_SPEC] * (2 * n), out_specs=[SEM_SPEC, SEM_SPEC] + [HBM_SPEC] * (2 * n) + [pl.BlockSpec(memory_space=pltpu.VMEM)],
        input_output_aliases={i: 2 + i for i in range(2 * n)},
        compiler_params=pltpu.CompilerParams(has_side_effects=DATAFLOW_EFFECT),
    )(*arrs, *lands)
    return res[0], res[1], list(res[2:2 + n]), list(res[2 + n:2 + 2 * n]), res[-1]


def _ag_finish_wait(send_sems, recv_sems, arrs, lands, after, name):
    n = len(arrs)

    def body(*refs):
        ins, bufs = refs[:n], refs[n:2 * n]
        send_r, recv_r = refs[2 * n:2 * n + 2]
        x, y, c = _position()
        for i in range(n):
            own = pltpu.make_async_remote_copy(src_ref=ins[i], dst_ref=bufs[i].at[2 * x + y], send_sem=send_r.at[4 * i + 3],
                                               recv_sem=recv_r.at[4 * i + 3], device_id=(x, y, 1 - c), device_id_type=MESH)
            own.wait_send()
            own.wait_recv()
            for j, (px, py) in enumerate(_other_chips(x, y)):
                pltpu.make_async_remote_copy(src_ref=bufs[i].at[2 * px + py, c], dst_ref=bufs[i].at[2 * px + py, c],
                                             send_sem=send_r.at[4 * i + j], recv_sem=recv_r.at[4 * i + j],
                                             device_id=(x, y, 1 - c), device_id_type=MESH).wait_send()
                pltpu.make_async_remote_copy(src_ref=bufs[i].at[2 * px + py, 1 - c], dst_ref=bufs[i].at[2 * px + py, 1 - c],
                                             send_sem=send_r.at[4 * i + j], recv_sem=recv_r.at[4 * i + j],
                                             device_id=(x, y, 1 - c), device_id_type=MESH).wait_recv()

    res = pl.pallas_call(
        body, name=name, out_shape=[pltpu.HBM(a.shape, a.dtype) for a in list(arrs) + list(lands)],
        in_specs=[HBM_SPEC] * (2 * n) + [SEM_SPEC, SEM_SPEC, pl.BlockSpec(memory_space=pl.ANY)], out_specs=[HBM_SPEC] * (2 * n),
        input_output_aliases={i: i for i in range(2 * n)},
        compiler_params=pltpu.CompilerParams(has_side_effects=DATAFLOW_EFFECT),
    )(*arrs, *lands, send_sems, recv_sems, after)
    return list(res[n:])


def _ag_finish(arrs, lands, name):
    send_sems, recv_sems, arrs, lands, token = _ag_finish_start(arrs, lands, name + "_start")
    return _ag_finish_wait(send_sems, recv_sems, arrs, lands, token, name + "_wait")


def _swap_start(gs, name):
    n = len(gs)

    def body(*refs):
        ins, lands = refs[:n], refs[n:2 * n]
        send_sems, recv_sems = refs[2 * n:2 * n + 2]
        token = refs[-1]
        x, y, c = _position()
        for i in range(n):
            for p in range(4):
                pltpu.make_async_remote_copy(src_ref=ins[i].at[p, 1 - c], dst_ref=lands[i].at[p], send_sem=send_sems.at[4 * i + p],
                                             recv_sem=recv_sems.at[4 * i + p], device_id=(x, y, 1 - c), device_id_type=MESH).start()
        token[...] = jnp.zeros_like(token)

    lands = [lax.empty((4,) + g.shape[2:], F32) for g in gs]
    res = pl.pallas_call(
        body, name=name,
        out_shape=[pltpu.SemaphoreType.DMA((4 * n,)), pltpu.SemaphoreType.DMA((4 * n,))]
        + [pltpu.HBM(a.shape, F32) for a in list(gs) + lands] + [SDS((8, 128), F32)],
        in_specs=[HBM_SPEC] * (2 * n), out_specs=[SEM_SPEC, SEM_SPEC] + [HBM_SPEC] * (2 * n) + [pl.BlockSpec(memory_space=pltpu.VMEM)],
        input_output_aliases={i: 2 + i for i in range(2 * n)},
        compiler_params=pltpu.CompilerParams(has_side_effects=DATAFLOW_EFFECT),
    )(*[pltpu.with_memory_space_constraint(a, pltpu.HBM) for a in list(gs) + lands])
    return res[0], res[1], list(res[2:2 + n]), list(res[2 + n:2 + 2 * n]), res[-1]


def _swap_wait(send_sems, recv_sems, gs, lands, after, name):
    n = len(gs)

    def body(*refs):
        ins, lands_r = refs[:n], refs[n:2 * n]
        send_r, recv_r = refs[2 * n:2 * n + 2]
        x, y, c = _position()
        for i in range(n):
            for p in range(4):
                cp = pltpu.make_async_remote_copy(src_ref=ins[i].at[p, 1 - c], dst_ref=lands_r[i].at[p], send_sem=send_r.at[4 * i + p],
                                                  recv_sem=recv_r.at[4 * i + p], device_id=(x, y, 1 - c), device_id_type=MESH)
                cp.wait_send()
                cp.wait_recv()

    res = pl.pallas_call(
        body, name=name, out_shape=[pltpu.HBM(a.shape, F32) for a in list(gs) + list(lands)],
        in_specs=[HBM_SPEC] * (2 * n) + [SEM_SPEC, SEM_SPEC, pl.BlockSpec(memory_space=pl.ANY)], out_specs=[HBM_SPEC] * (2 * n),
        input_output_aliases={i: i for i in range(2 * n)},
        compiler_params=pltpu.CompilerParams(has_side_effects=DATAFLOW_EFFECT),
    )(*gs, *lands, send_sems, recv_sems, after)
    return list(res[:n]), list(res[n:])


def _exchange_start(s1s, name):
    n = len(s1s)

    def body(*refs):
        srcs, lands = refs[:n], refs[n:2 * n]
        send_sems, recv_sems = refs[2 * n:2 * n + 2]
        token = refs[-1]
        x, y, c = _position()
        for i in range(n):
            for j, (px, py) in enumerate(_other_chips(x, y)):
                pltpu.make_async_remote_copy(src_ref=srcs[i].at[2 * px + py], dst_ref=lands[i].at[j], send_sem=send_sems.at[3 * i + j],
                                             recv_sem=recv_sems.at[3 * i + j], device_id=(px, py, c), device_id_type=MESH).start()
        token[...] = jnp.zeros_like(token)

    lands = [lax.empty((3,) + s.shape[1:], F32) for s in s1s]
    res = pl.pallas_call(
        body, name=name,
        out_shape=[pltpu.SemaphoreType.DMA((3 * n,)), pltpu.SemaphoreType.DMA((3 * n,))]
        + [pltpu.HBM(a.shape, F32) for a in list(s1s) + lands] + [SDS((8, 128), F32)],
        in_specs=[HBM_SPEC] * (2 * n), out_specs=[SEM_SPEC, SEM_SPEC] + [HBM_SPEC] * (2 * n) + [pl.BlockSpec(memory_space=pltpu.VMEM)],
        input_output_aliases={i: 2 + i for i in range(2 * n)},
        compiler_params=pltpu.CompilerParams(has_side_effects=DATAFLOW_EFFECT),
    )(*[pltpu.with_memory_space_constraint(a, pltpu.HBM) for a in list(s1s) + lands])
    return res[0], res[1], list(res[2:2 + n]), list(res[2 + n:2 + 2 * n]), res[-1]


def _exchange_wait(send_sems, recv_sems, s1s, lands, after, name):
    n = len(s1s)

    def body(*refs):
        srcs, lands_r = refs[:n], refs[n:2 * n]
        send_r, recv_r = refs[2 * n:2 * n + 2]
        x, y, c = _position()
        for i in range(n):
            for j, (px, py) in enumerate(_other_chips(x, y)):
                cp = pltpu.make_async_remote_copy(src_ref=srcs[i].at[2 * px + py], dst_ref=lands_r[i].at[j], send_sem=send_r.at[3 * i + j],
                                                  recv_sem=recv_r.at[3 * i + j], device_id=(px, py, c), device_id_type=MESH)
                cp.wait_send()
                cp.wait_recv()

    res = pl.pallas_call(
        body, name=name, out_shape=[pltpu.HBM(a.shape, F32) for a in list(s1s) + list(lands)],
        in_specs=[HBM_SPEC] * (2 * n) + [SEM_SPEC, SEM_SPEC, pl.BlockSpec(memory_space=pl.ANY)], out_specs=[HBM_SPEC] * (2 * n),
        input_output_aliases={i: i for i in range(2 * n)},
        compiler_params=pltpu.CompilerParams(has_side_effects=DATAFLOW_EFFECT),
    )(*s1s, *lands, send_sems, recv_sems, after)
    return list(res[:n]), list(res[n:])


def _join_halves(f2s, name):
    n = len(f2s)
    hbm = pl.BlockSpec(memory_space=pl.ANY)

    def body(*refs):
        ins, outs = refs[:n], refs[n:2 * n]
        send_sems, recv_sems = refs[2 * n:]
        x, y, c = _position()
        cps = []
        for i in range(n):
            cp = pltpu.make_async_remote_copy(src_ref=ins[i].at[c], dst_ref=outs[i].at[c], send_sem=send_sems.at[i],
                                              recv_sem=recv_sems.at[i], device_id=(x, y, 1 - c), device_id_type=MESH)
            cp.start()
            cps.append(cp)
        for i in range(n):
            pltpu.make_async_remote_copy(src_ref=ins[i].at[1 - c], dst_ref=outs[i].at[1 - c], send_sem=send_sems.at[i],
                                         recv_sem=recv_sems.at[i], device_id=(x, y, 1 - c), device_id_type=MESH).wait_recv()
        for cp in cps:
            cp.wait_send()

    return pl.pallas_call(
        body, name=name, in_specs=[hbm] * n, out_specs=[hbm] * n, out_shape=[SDS(f.shape, f.dtype) for f in f2s],
        input_output_aliases={i: i for i in range(n)},
        scratch_shapes=[pltpu.SemaphoreType.DMA((n,)), pltpu.SemaphoreType.DMA((n,))])(*f2s)


def _sum_tile(rows, cols):
    return _row_tile(rows, max(8, (2 ** 19 // cols) // 8 * 8))


def _add_pair(g, r1, c, name):
    _, R, C = r1.shape
    tr = _sum_tile(R, C)

    def body(c_ref, a_ref, b_ref, o_ref):
        del c_ref
        o_ref[...] = a_ref[...] + b_ref[...]

    blk = pl.BlockSpec((None, tr, C), lambda p, i, cr: (p, i, 0))
    return pl.pallas_call(
        body, name=name,
        grid_spec=pltpu.PrefetchScalarGridSpec(
            num_scalar_prefetch=1, grid=(4, R // tr),
            in_specs=[pl.BlockSpec((None, None, tr, C), lambda p, i, cr: (p, cr[0], i, 0)), blk], out_specs=blk),
        out_shape=SDS((4, R, C), F32))(c, g, r1)


def _add_four(s1, r2, me_c, name):
    _, R, C = s1.shape
    tr = _sum_tile(R, C)

    def body(m_ref, a_ref, b_ref, o_ref):
        del m_ref
        o_ref[...] = ((a_ref[...] + b_ref[0]) + b_ref[1]) + b_ref[2]

    return pl.pallas_call(
        body, name=name,
        grid_spec=pltpu.PrefetchScalarGridSpec(
            num_scalar_prefetch=1, grid=(R // tr,),
            in_specs=[pl.BlockSpec((None, tr, C), lambda i, mr: (mr[0], i, 0)), pl.BlockSpec((3, tr, C), lambda i, mr: (0, i, 0))],
            out_specs=pl.BlockSpec((None, tr, C), lambda i, mr: (mr[1], i, 0))),
        out_shape=SDS((2, R, C), F32))(me_c, s1, r2)


def _all_reduce_small(vs):
    n = len(vs)

    def body(*refs):
        ins, outs, bufs = refs[:n], refs[n:2 * n], refs[2 * n:3 * n]
        send_sems, recv_sems = refs[3 * n:]
        x, y, c = _position()
        me = 4 * x + 2 * y + c
        for i in range(n):
            bufs[i][me] = ins[i][...]
        cps = []
        for k in range(1, 8):
            to = (1 - x if k & 4 else x, 1 - y if k & 2 else y, 1 - c if k & 1 else c)
            for i in range(n):
                cp = pltpu.make_async_remote_copy(src_ref=bufs[i].at[me], dst_ref=bufs[i].at[me], send_sem=send_sems.at[7 * i + k - 1],
                                                  recv_sem=recv_sems.at[7 * i + k - 1], device_id=to, device_id_type=MESH)
                cp.start()
                cps.append(cp)
        for cp in cps:
            cp.wait_send()
        for k in range(1, 8):
            src = 4 * (1 - x if k & 4 else x) + 2 * (1 - y if k & 2 else y) + (1 - c if k & 1 else c)
            for i in range(n):
                pltpu.make_async_remote_copy(src_ref=bufs[i].at[src], dst_ref=bufs[i].at[src], send_sem=send_sems.at[7 * i + k - 1],
                                             recv_sem=recv_sems.at[7 * i + k - 1], device_id=(x, y, c), device_id_type=MESH).wait_recv()
        for i in range(n):
            acc = bufs[i][0]
            for k in range(1, 8):
                acc = acc + bufs[i][k]
            outs[i][...] = acc

    vm = pl.BlockSpec(memory_space=pltpu.VMEM)
    return pl.pallas_call(
        body, name="all_reduce_small", in_specs=[vm] * n, out_specs=[vm] * n, out_shape=[SDS(v.shape, F32) for v in vs],
        scratch_shapes=[pltpu.VMEM((8,) + v.shape, F32) for v in vs]
        + [pltpu.SemaphoreType.DMA((7 * n,)), pltpu.SemaphoreType.DMA((7 * n,))])(*vs)


def _reduce_begin(grads, tag):
    g4 = [g.reshape(4, 2, g.shape[1] // 2, g.shape[2]) for g in grads]
    send_sems, recv_sems, g4, lands, token = _swap_start(g4, "rs_swap_start_" + tag)
    return (send_sems, recv_sems, g4, lands), token


def _reduce_mid(state, after, tag):
    _, _, c = _position()
    cs = jnp.reshape(c, (1,)).astype(jnp.int32)
    send_sems, recv_sems, g4, lands = state
    g4, r1 = _swap_wait(send_sems, recv_sems, g4, lands, after, "rs_swap_wait_" + tag)
    s1 = [_add_pair(g, r, cs, f"rs_add_pair_{tag}{i}") for i, (g, r) in enumerate(zip(g4, r1))]
    send_sems, recv_sems, s1, lands, token = _exchange_start(s1, "rs_exchange_start_" + tag)
    return (send_sems, recv_sems, s1, lands), token


def _reduce_end(state, after, tag):
    x, y, c = _position()
    send_sems, recv_sems, s1, lands = state
    s1, lands = _exchange_wait(send_sems, recv_sems, s1, lands, after, "rs_exchange_wait_" + tag)
    me_c = jnp.stack([2 * x + y, c]).astype(jnp.int32)
    f2 = [_add_four(s, l, me_c, f"rs_add_four_{tag}{i}") for i, (s, l) in enumerate(zip(s1, lands))]
    return [f.reshape(2 * f.shape[1], f.shape[2]) for f in _join_halves(f2, "rs_join_" + tag)]


def _halves(a):
    return a.reshape((2, a.shape[0] // 2) + a.shape[1:])


def _after(a, token):
    return a + token[0, 0]


def _in_proj_own(x, nw, w, part):
    M, K = x.shape
    n = w.shape[2]
    sub = 256

    def body(p_ref, x_ref, nw_ref, w_ref, xn_ref, z_ref):
        del p_ref
        for r0 in range(0, MM_TM, sub):
            rows = pl.ds(r0, sub)
            for c0 in range(r0, r0 + sub, NORM_RC):
                ch = pl.ds(c0, NORM_RC)
                xv = x_ref[ch, :]
                xn_ref[ch, :] = (xv * lax.rsqrt(jnp.mean(xv * xv, axis=-1, keepdims=True) + RMS_EPS) * nw_ref[...]).astype(BF16)
            z_ref[rows, :] = jnp.dot(xn_ref[rows, :], w_ref[...], preferred_element_type=F32).astype(BF16)

    row = pl.BlockSpec((MM_TM, K), lambda i, pr: (i, 0))
    return pl.pallas_call(
        body, name="in_proj_own",
        grid_spec=pltpu.PrefetchScalarGridSpec(
            num_scalar_prefetch=1, grid=(M // MM_TM,),
            in_specs=[row, pl.BlockSpec((1, K), lambda i, pr: (0, 0)), pl.BlockSpec((None, K, n), lambda i, pr: (0, 0, 0))],
            out_specs=[row, pl.BlockSpec((MM_TM, n), lambda i, pr: (i, pr[0]))]),
        out_shape=[SDS((M, K), BF16), SDS((M, 4 * n), BF16)])(part, x, nw, w)


def _in_proj_parts(xn, w, parts, z, name):
    M, K = xn.shape
    _, _, n = w.shape
    P = parts.shape[0]

    def body(p_ref, a_ref, b_ref, z_in, o_ref):
        del p_ref, z_in
        o_ref[...] = jnp.dot(a_ref[...], b_ref[...], preferred_element_type=F32).astype(BF16)

    return pl.pallas_call(
        body, name=name,
        grid_spec=pltpu.PrefetchScalarGridSpec(
            num_scalar_prefetch=1, grid=(P, M // MM_TM),
            in_specs=[pl.BlockSpec((MM_TM, K), lambda g, i, pr: (i, 0)), pl.BlockSpec((None, K, n), lambda g, i, pr: (pr[g], 0, 0)),
                      pl.BlockSpec(memory_space=pl.ANY)],
            out_specs=pl.BlockSpec((MM_TM, n), lambda g, i, pr: (i, pr[g]))),
        out_shape=SDS(z.shape, BF16), input_output_aliases={3: 0})(parts, xn, w, z)


def _local_step(x, mem, target, sp, ex):
    S, D = x.shape
    band, buckets = _bias_static()
    buckets = jnp.asarray(buckets)

    tok = ex.start_first()
    own, me, others = ex.own_w_in()
    xn, z = _in_proj_own(x, _after(sp["attn_norm_w"], tok), own, me)
    w_in, gathered_small = ex.first_weights(after=z)
    sp = {**sp, **gathered_small}
    n_in = 4 * w_in.shape[2]
    bw = {"w_in": w_in}
    z = _in_proj_parts(xn, w_in, others + ex.start_rest()[0, 0].astype(jnp.int32), z, "in_proj_rest")
    qh, kh, vh, q2, k2, v2 = _qkv_prep(z, sp["q_norm_w"], sp["k_norm_w"])
    tab = sp["rel_bias_table"].T.reshape(N_GROUPS, HPG, N_BUCKETS)
    bias = _bias_fwd(jnp.pad(tab, ((0, 0), (0, 8 - HPG), (0, 0))), buckets)
    biasm = jnp.where(jnp.asarray(band)[None, None], bias[:, :HPG].reshape(N_GROUPS, HPG, NQ, 2 * NQ), NEG)
    seqs = lambda a: a.reshape(-1, S // P4, HEAD)
    q12, k12, v12 = seqs(q2), seqs(k2), seqs(v2)
    groups = ((qh, kh, vh, 1, 0, HPG, 0), (q12, k12, v12, 1, 0, HPG * P4, 2), (q12, k12, v12, 4, HPG * P4, HPG * P4, 2))
    os_, lses = [], []
    for g, (qg, kg, vg, dil, head0, nseq, shift) in enumerate(groups):
        o_g, lse_g = _attn_fwd(qg, kg, vg, biasm[g], f"attn_fwd_g{g}", dil, head0, nseq, shift)
        os_.append(o_g.reshape(HPG, -1, o_g.shape[1], HEAD) if g > 0 else o_g)
        lses.append(lse_g.reshape(HPG, -1, lse_g.shape[1], HEAD) if g > 0 else lse_g)
    attn = _merge_fwd(os_, lses)
    tok = ex.rest_arrived(after=attn)
    u1, u3 = _conv_fwd(z, sp["conv_dw_w"], sp["conv_dw_b"], _after(sp["conv_ln_w"], tok), sp["conv_ln_b"])
    bw.update(ex.rest_weights(after=u3))
    F2 = 4 * bw["w_up"].shape[2]
    mn, kv, mk, mv = _memkv_fwd(mem, sp["mem_norm_w"], bw["w_mem_kv"], sp["xk_norm_w"])
    oc = _cross_fwd(z, sp["xq_norm_w"], mk, mv)
    h1, hn = _outproj_fwd(x, z, sp["b_gate"], attn, u3, oc, bw["w_attn_o"], bw["w_conv_o"], bw["w_cross_o"], bw["w_out"],
                          sp["ffn_norm_w"])
    up = _mm_nn(hn, bw["w_up"], BF16, "ffn_up")
    dy, loss_tile, ac, gc = _ffn_fwd(up, sp["ffn_conv_w"], sp["ffn_conv_b"], bw["w_down"], h1, target)

    gs, gb = {}, {}
    dca, dcg, acca, accg, gb["w_down"] = _ffn_bwd_a(dy, bw["w_down"], ac, gc)
    cols = lambda acc: jnp.sum(acc, axis=1).reshape(1, F2 // 2)
    gs["ffn_conv_b"] = jnp.concatenate([cols(acca), cols(accg)], axis=1)
    dup, tacc = _ffn_bwd_b(dca, dcg, up, sp["ffn_conv_w"])
    gs["ffn_conv_w"] = jnp.transpose(jnp.sum(tacc, axis=2), (1, 0, 2)).reshape(FFN_K, F2)
    gb["w_up"] = _mm_tn(hn, dup, 4, "dw_up")
    tok = ex.reduce_begin("a", ("w_down", "w_up"), gb)
    dh1, gs["ffn_norm_w"] = _norm_in_bwd(dup, bw["w_up"], h1, _after(sp["ffn_norm_w"], tok), dy, "ffn_in_bwd")
    tok = ex.reduce_mid("a", after=gs["ffn_norm_w"])
    dz, dattn, du3, doc, gs["b_gate"], gb["w_out"], gb["w_attn_o"], gb["w_conv_o"], gb["w_cross_o"] = _outproj_bwd(
        dh1, z, _after(sp["b_gate"], tok), attn, u3, oc, bw["w_attn_o"], bw["w_conv_o"], bw["w_cross_o"], bw["w_out"], n_in)
    dz, dmk, dmv, gs["xq_norm_w"] = _cross_bwd(dz, doc, z, sp["xq_norm_w"], mk, mv)
    gb["w_mem_kv"], gs["xk_norm_w"], gs["mem_norm_w"] = _memkv_bwd(
        dmk, dmv, kv, mem, mn, sp["mem_norm_w"], bw["w_mem_kv"], sp["xk_norm_w"])
    ex.reduce_end("a", after=gs["mem_norm_w"])
    tok = ex.reduce_begin("b", ("w_out", "w_attn_o", "w_conv_o", "w_cross_o", "w_mem_kv"), gb)
    du1, cacc = _conv_bwd_a(du3, u1, z, _after(sp["conv_ln_w"], tok), sp["conv_ln_b"])
    gs["conv_dw_w"], gs["conv_dw_b"] = cacc[:CONV_K], cacc[32:33]
    gs["conv_ln_w"], gs["conv_ln_b"] = cacc[33:34], cacc[34:35]
    tok = ex.reduce_mid("b", after=cacc)
    dz = _conv_bwd_b(dz, du1, z, _after(sp["conv_dw_w"], tok))
    merged_grads = _merge_bwd(dattn, os_, lses)
    dqs, dks, dvs, dsbs = [], [], [], []
    for g, (qg, kg, vg, dil, head0, nseq, shift) in enumerate(groups):
        wg_g, da_g, dh_g = merged_grads[g]
        lse_g = lses[g]
        if g > 0:
            wg_g, da_g, dh_g, lse_g = seqs(wg_g), seqs(da_g), seqs(dh_g), seqs(lse_g)
        dq_g, dk_g, dv_g, dsb_g = _attn_bwd(qg, kg, vg, biasm[g], da_g, wg_g, dh_g, lse_g, f"attn_bwd_g{g}", dil, head0, nseq, shift)
        if g > 0:
            dq_g, dk_g, dv_g = (t.reshape(HPG, P4, S // P4, HEAD) for t in (dq_g, dk_g, dv_g))
            dsb_g = jnp.sum(dsb_g.reshape(HPG, P4, NQ, 2 * NQ), axis=1)
        dqs.append(dq_g)
        dks.append(dk_g)
        dvs.append(dv_g)
        dsbs.append(dsb_g.reshape(HPG, NQ * 2 * NQ))
    dtab = _bias_bwd(jnp.pad(jnp.stack(dsbs), ((0, 0), (0, 8 - HPG), (0, 0))), buckets)
    gs["rel_bias_table"] = dtab[:, :HPG].reshape(N_GROUPS * HPG, N_BUCKETS).T
    dz, gs["q_norm_w"], gs["k_norm_w"] = _qkv_bwd(dz, z, dqs, dks, dvs, sp["q_norm_w"], sp["k_norm_w"])
    ex.reduce_end("b", after=gs["q_norm_w"])
    gb["w_in"] = _mm_tn(xn, dz, 4, "dw_in")
    tok = ex.reduce_mid("c", after=ex.reduce_begin("c", ("w_in",), gb))
    dx, gs["attn_norm_w"] = _norm_in_bwd(dz, bw["w_in"], x, _after(sp["attn_norm_w"], tok), dh1, "in_bwd")
    ex.reduce_end("c", after=gs["attn_norm_w"])
    return loss_tile, dx, gs, gb


SMALL = ("rel_bias_table", "attn_norm_w", "b_gate", "q_norm_w", "k_norm_w", "conv_dw_w", "conv_dw_b", "conv_ln_w", "conv_ln_b",
         "mem_norm_w", "xq_norm_w", "xk_norm_w", "ffn_norm_w", "ffn_conv_w", "ffn_conv_b")
SMALL_SHARDED = ("conv_dw_w", "ffn_conv_w")
BIG_COL = ("w_in", "w_attn_o", "w_conv_o", "w_cross_o", "w_up")
BIG_ROW = ("w_mem_kv", "w_out", "w_down")
BIG = BIG_COL + BIG_ROW
WEIGHTS = ("rel_bias_table", "attn_norm_w", "w_in", "b_gate", "q_norm_w", "k_norm_w", "w_attn_o", "conv_dw_w", "conv_dw_b",
           "conv_ln_w", "conv_ln_b", "w_conv_o", "mem_norm_w", "w_mem_kv", "xq_norm_w", "xk_norm_w", "w_cross_o", "w_out",
           "ffn_norm_w", "w_up", "ffn_conv_w", "ffn_conv_b", "w_down")


class _Exchanges:
    REST = tuple(k for k in BIG if k != "w_in")

    def __init__(self, w):
        self.w = w
        self.pending = {}
        self.reduced = {}

    def _whole(self, k, ga):
        ga = ga.reshape((4,) + self.w[k].shape)
        return ga if k in BIG_COL else ga.reshape((4 * self.w[k].shape[0],) + self.w[k].shape[1:])

    def start_first(self):
        self.w_in_local = self.w["w_in"].astype(BF16)
        local = [_halves(self.w_in_local)]
        for k in SMALL_SHARDED:
            flat = jnp.ravel(self.w[k])
            local.append(jnp.pad(flat, (0, (-flat.shape[0]) % 2048)).reshape(2, -1, 128))
        send_sems, recv_sems, ins, lands, token = _ag_ici_start(local, "gather_first_start")
        self.pending["first"] = (send_sems, recv_sems, ins, lands)
        return token

    def own_w_in(self):
        x, y, _ = _position()
        me = 2 * x + y
        others = jnp.stack([me ^ 1, me ^ 2, me ^ 3]).astype(jnp.int32)
        return self.w_in_local[None], jnp.reshape(me, (1,)).astype(jnp.int32), others

    def first_weights(self, after):
        send_sems, recv_sems, ins, lands = self.pending.pop("first")
        ins, lands = _ag_ici_wait(send_sems, recv_sems, ins, lands, after, "gather_first_wait")
        gathered = _ag_finish(ins, lands, "gather_first_finish")
        self.first = gathered[0]
        small = {}
        for k, ga in zip(SMALL_SHARDED, gathered[1:]):
            r, cdim = self.w[k].shape
            parts = ga.reshape(4, -1)[:, :r * cdim].reshape(4, r, cdim)
            small[k] = jnp.transpose(parts, (1, 0, 2)).reshape(r, 4 * cdim)
        return self._whole("w_in", gathered[0]), small

    def start_rest(self):
        local = [_halves(self.w[k].astype(BF16)) for k in self.REST]
        local, _ = lax.optimization_barrier((local, self.first))
        send_sems, recv_sems, ins, lands, token = _ag_ici_start(local, "gather_rest_start")
        self.pending["rest"] = (send_sems, recv_sems, ins, lands)
        return token

    def rest_arrived(self, after):
        send_sems, recv_sems, ins, lands = self.pending.pop("rest")
        ins, lands = _ag_ici_wait(send_sems, recv_sems, ins, lands, after, "gather_rest_wait")
        send_sems, recv_sems, ins, lands, token = _ag_finish_start(ins, lands, "gather_rest_finish_start")
        self.pending["rest"] = (send_sems, recv_sems, ins, lands)
        return token

    def rest_weights(self, after):
        gathered = _ag_finish_wait(*self.pending.pop("rest"), after, "gather_rest_finish_wait")
        return {k: self._whole(k, ga) for k, ga in zip(self.REST, gathered)}

    def reduce_begin(self, tag, names, gb):
        parts = [gb[k].reshape((4,) + self.w[k].shape) for k in names]
        state, token = _reduce_begin(parts, tag)
        self.pending[tag] = (state, names)
        return token

    def reduce_mid(self, tag, after):
        state, names = self.pending.pop(tag)
        state, token = _reduce_mid(state, after, tag)
        self.pending[tag] = (state, names)
        return token

    def reduce_end(self, tag, after):
        state, names = self.pending.pop(tag)
        self.reduced.update(zip(names, _reduce_end(state, after, tag)))


def _step(x, mem, target, w, m, v):
    xi, yi, _ = _position()
    shard = 2 * xi + yi
    ex = _Exchanges(w)
    sp = {k: w[k] for k in SMALL if k not in SMALL_SHARDED}
    loss_tile, dx, gs, _ = _local_step(x, mem, target, sp, ex)
    g_big = ex.reduced

    red = _all_reduce_small([loss_tile] + [gs[k] for k in SMALL])
    loss = red[0][0, 0]
    g_small = dict(zip(SMALL, red[1:]))
    for k in SMALL_SHARDED:
        cdim = w[k].shape[1]
        g_small[k] = lax.dynamic_slice_in_dim(g_small[k], shard * cdim, cdim, axis=1)

    grads, delta, new_m, new_v = {}, {}, {}, {}
    for k in BIG:
        grads[k] = g_big[k]
        delta[k], new_m[k], new_v[k] = _adamw(w[k], g_big[k], m[k], v[k], "adamw_" + k)
    outs = _adamw_small([w[k] for k in SMALL], [g_small[k] for k in SMALL], [m[k] for k in SMALL], [v[k] for k in SMALL])
    for dst, vals in zip((delta, new_m, new_v), outs):
        dst.update(zip(SMALL, vals))
    grads.update(g_small)
    return loss, dx, grads, delta, new_m, new_v


def kernel(x, mem, rel_bias_table, attn_norm_w, w_in, b_gate, q_norm_w, k_norm_w, w_attn_o, conv_dw_w, conv_dw_b, conv_ln_w, conv_ln_b, w_conv_o, mem_norm_w, w_mem_kv, xq_norm_w, xk_norm_w, w_cross_o, w_out, ffn_norm_w, w_up, ffn_conv_w, ffn_conv_b, w_down, loss_target, m_rel_bias_table, m_attn_norm_w, m_w_in, m_b_gate, m_q_norm_w, m_k_norm_w, m_w_attn_o, m_conv_dw_w, m_conv_dw_b, m_conv_ln_w, m_conv_ln_b, m_w_conv_o, m_mem_norm_w, m_w_mem_kv, m_xq_norm_w, m_xk_norm_w, m_w_cross_o, m_w_out, m_ffn_norm_w, m_w_up, m_ffn_conv_w, m_ffn_conv_b, m_w_down, v_rel_bias_table, v_attn_norm_w, v_w_in, v_b_gate, v_q_norm_w, v_k_norm_w, v_w_attn_o, v_conv_dw_w, v_conv_dw_b, v_conv_ln_w, v_conv_ln_b, v_w_conv_o, v_mem_norm_w, v_w_mem_kv, v_xq_norm_w, v_xk_norm_w, v_w_cross_o, v_w_out, v_ffn_norm_w, v_w_up, v_ffn_conv_w, v_ffn_conv_b, v_w_down):
    args = locals()
    def block(name, k):
        a = args[name] if k == "rel_bias_table" else args[name][0]
        return a.reshape(1, -1) if a.ndim == 1 else a

    w = {k: block(k, k) for k in WEIGHTS}
    m = {k: block("m_" + k, k) for k in WEIGHTS}
    v = {k: block("v_" + k, k) for k in WEIGHTS}
    loss, dx, grads, delta, new_m, new_v = _step(x[0], mem[0], loss_target[0], w, m, v)
    out = [loss, dx[None]]
    for d in (grads, delta, new_m, new_v):
        for k in WEIGHTS:
            out.append(d[k].reshape(args[k].shape))
    return tuple(out)
```

```python
import functools
import math

import numpy as np
import jax
import jax.numpy as jnp
from jax import lax
from jax.experimental import pallas as pl
from jax.experimental.pallas import tpu as pltpu

F32, BF16 = jnp.float32, jnp.bfloat16
SDS = jax.ShapeDtypeStruct
MESH = pl.DeviceIdType.MESH

HEAD = 128
N_GROUPS, HPG = 3, 4
ATTN_GROUPS = ((128, 1), (512, 4), (2048, 16))
NQ = 128
QKV_W = N_GROUPS * HPG * HEAD
CH = 512
CONV_K, FFN_K = 31, 3
N_BUCKETS, MAX_DIST = 32, 2048
RMS_EPS, LN_EPS = 1e-6, 1e-5
O_Q, O_K, O_V, O_CV, O_CG, O_XQ, O_G = 0, QKV_W, 2 * QKV_W, 3 * QKV_W, 3 * QKV_W + CH, 3 * QKV_W + 2 * CH, 3 * QKV_W + 3 * CH
ADAM_LR, ADAM_B1, ADAM_B2, ADAM_EPS, ADAM_WD, ADAM_STEP = 0.001, 0.9, 0.999, 1e-08, 0.01, 10
NEG = -1e30
SCALE = HEAD ** -0.5
TM = 512
MM_TM = 1024
ATT_RB = 2048
NT_DIMS = (((1,), (1,)), ((), ()))
TN_DIMS = (((0,), (0,)), ((), ()))


CONV_RC = 32
CONV_LB = 256
CONV_LANES = tuple(slice(l, l + CONV_LB) for l in range(0, CH, CONV_LB))
CONV_HALO = 32
FFN_RC = 32


def _conv_taps(buf, base, lanes, q_lo, q_hi, visit):
    span = CONV_RC + CONV_HALO
    xx = buf[pl.ds(base, span), lanes]
    for s in range(8):
        xs = xx if s == 0 else pltpu.roll(xx, span - s, 0)
        for q in range(s, q_hi + 1, 8):
            if q >= q_lo:
                visit(q, xs[q - s:q - s + CONV_RC])


def _sig(v):
    return 0.5 * jnp.tanh(0.5 * v) + 0.5


def _fold8(v):
    acc = v[0:8]
    for r in range(8, v.shape[0], 8):
        acc = acc + v[r:r + 8]
    return acc


def _row_tile(rows, cap, mult=8):
    best = None
    for t in range(mult, min(rows, cap) + 1, mult):
        if rows % t == 0:
            best = t
    return best if best is not None else rows


def _full(shape):
    n = len(shape)
    return pl.BlockSpec(shape, lambda *a: (0,) * n)


def _mm_nn(a, b, out_dtype, name):
    M, K = a.shape
    G, _, n = b.shape

    def body(a_ref, b_ref, o_ref):
        o_ref[...] = jnp.dot(a_ref[...].astype(BF16), b_ref[...], preferred_element_type=F32).astype(out_dtype)

    return pl.pallas_call(
        body, name=name, grid=(G, M // MM_TM),
        in_specs=[pl.BlockSpec((MM_TM, K), lambda g, i: (i, 0)), pl.BlockSpec((None, K, n), lambda g, i: (g, 0, 0))],
        out_specs=pl.BlockSpec((MM_TM, n), lambda g, i: (i, g)),
        out_shape=SDS((M, G * n), out_dtype))(a, b)


def _mm_tn(a, b, G, name):
    S, Ka = a.shape
    n = b.shape[1] // G
    tka = Ka
    while tka * n * 4 > 10 * 2 ** 20 and tka % 256 == 0:
        tka //= 2

    def body(a_ref, b_ref, o_ref):
        @pl.when(pl.program_id(2) == 0)
        def _():
            o_ref[...] = jnp.zeros_like(o_ref)
        o_ref[...] += lax.dot_general(a_ref[...].astype(BF16), b_ref[...].astype(BF16), TN_DIMS, preferred_element_type=F32)

    return pl.pallas_call(
        body, name=name, grid=(G, Ka // tka, S // MM_TM),
        in_specs=[pl.BlockSpec((MM_TM, tka), lambda g, i, k: (k, i)), pl.BlockSpec((MM_TM, n), lambda g, i, k: (k, g))],
        out_specs=pl.BlockSpec((None, tka, n), lambda g, i, k: (g, i, 0)),
        out_shape=SDS((G, Ka, n), F32))(a, b)


NORM_RC = 16


def _norm_in_bwd(a, w, xin, nw, resid, name):
    S, K = xin.shape
    G, _, n = w.shape
    tm = 1024 if S % 1024 == 0 and G * K * n * 2 <= 12 * 2 ** 20 else TM
    nt = S // tm

    def body(a_ref, w_ref, x_ref, nw_ref, r_ref, o_ref, dnw_ref, acc, part):
        i, g = pl.program_id(0), pl.program_id(1)

        @pl.when((i == 0) & (g == 0))
        def _():
            part[...] = jnp.zeros_like(part)

        @pl.when(g == 0)
        def _():
            acc[...] = jnp.zeros_like(acc)

        acc[...] += lax.dot_general(a_ref[...], w_ref[g], NT_DIMS, preferred_element_type=F32)

        @pl.when(g == G - 1)
        def _():
            for r0 in range(0, tm, NORM_RC):
                rows = pl.ds(r0, NORM_RC)
                dn = acc[rows, :]
                xv = x_ref[rows, :]
                r = lax.rsqrt(jnp.mean(xv * xv, axis=-1, keepdims=True) + RMS_EPS)
                xhat = xv * r
                dyw = dn * nw_ref[...]
                o_ref[rows, :] = r_ref[rows, :] + r * (dyw - xhat * jnp.mean(dyw * xhat, axis=-1, keepdims=True))
                part[...] += _fold8(dn * xhat)

        @pl.when((i == nt - 1) & (g == G - 1))
        def _():
            dnw_ref[...] = jnp.sum(part[...], axis=0, keepdims=True)

    row = pl.BlockSpec((tm, K), lambda i, g: (i, 0))
    return pl.pallas_call(
        body, name=name, grid=(nt, G),
        in_specs=[pl.BlockSpec((tm, n), lambda i, g: (i, g)),
                  pl.BlockSpec((G, K, n), lambda i, g: (0, 0, 0), pipeline_mode=pl.Buffered(1)),
                  row, _full((1, K)), row],
        out_specs=[row, _full((1, K))],
        out_shape=[SDS((S, K), F32), SDS((1, K), F32)],
        scratch_shapes=[pltpu.VMEM((tm, K), F32), pltpu.VMEM((8, K), F32)])(a, w, xin, nw, resid)


def _t5_bucket_np(dist):
    max_exact = N_BUCKETS // 2
    d = np.maximum(dist.astype(np.float32), np.float32(1.0))
    large = max_exact + (np.log(d / np.float32(max_exact)) / np.float32(math.log(MAX_DIST / max_exact))
                         * np.float32(N_BUCKETS - max_exact)).astype(np.int32)
    large = np.minimum(large, N_BUCKETS - 1)
    return np.where(dist < max_exact, dist, large).astype(np.int32)


def _bias_static():
    qi = np.arange(NQ)[:, None]
    kj = np.arange(2 * NQ)[None, :]
    step = qi + NQ - kj
    band = (step >= 0) & (step <= NQ)
    buckets = np.stack([_t5_bucket_np(np.clip(step, 0, None) * dil).reshape(1, -1) for _, dil in ATTN_GROUPS])
    return band, buckets


def _bias_fwd(table_t, buckets):
    nb = buckets.shape[-1]

    def body(t_ref, b_ref, o_ref):
        oh = (b_ref[...] == lax.broadcasted_iota(jnp.int32, (N_BUCKETS, nb), 0)).astype(F32)
        o_ref[...] = jnp.dot(t_ref[...], oh, preferred_element_type=F32, precision=lax.Precision.HIGHEST)

    return pl.pallas_call(
        body, name="bias_fwd", grid=(N_GROUPS,),
        in_specs=[pl.BlockSpec((None, 8, N_BUCKETS), lambda g: (g, 0, 0)), pl.BlockSpec((None, 1, nb), lambda g: (g, 0, 0))],
        out_specs=pl.BlockSpec((None, 8, nb), lambda g: (g, 0, 0)),
        out_shape=SDS((N_GROUPS, 8, nb), F32))(table_t, buckets)


def _bias_bwd(dsb, buckets):
    nb = buckets.shape[-1]

    def body(d_ref, b_ref, o_ref):
        oh = (b_ref[...] == lax.broadcasted_iota(jnp.int32, (N_BUCKETS, nb), 0)).astype(F32)
        o_ref[...] = lax.dot_general(d_ref[...], oh, NT_DIMS, preferred_element_type=F32, precision=lax.Precision.HIGHEST)

    return pl.pallas_call(
        body, name="bias_bwd", grid=(N_GROUPS,),
        in_specs=[pl.BlockSpec((None, 8, nb), lambda g: (g, 0, 0)), pl.BlockSpec((None, 1, nb), lambda g: (g, 0, 0))],
        out_specs=pl.BlockSpec((None, 8, N_BUCKETS), lambda g: (g, 0, 0)),
        out_shape=SDS((N_GROUPS, 8, N_BUCKETS), F32))(dsb, buckets)


P4 = 4


def _to_p4(dst_ref, h, val, scr):
    scr[...] = val
    for r in range(P4):
        dst_ref[h, r] = scr[pl.ds(r, TM // P4, stride=P4), :]


def _from_p4(src_ref, h, scr):
    for r in range(P4):
        scr[pl.ds(r, TM // P4, stride=P4), :] = src_ref[h, r]
    return scr[...]


def _qkv_prep(z, qw, kw):
    S = z.shape[0]
    nh = N_GROUPS * HPG

    def body(zq, zk, zv, qw_ref, kw_ref, qh, kh, vh, q2, k2, v2, scr):
        for h in range(nh):
            g = h // HPG
            sl = slice(h * HEAD, (h + 1) * HEAD)
            xq = zq[:, sl].astype(F32)
            q = xq * lax.rsqrt(jnp.mean(xq * xq, axis=-1, keepdims=True) + RMS_EPS) * qw_ref[g:g + 1, :]
            xk = zk[:, sl].astype(F32)
            k = xk * lax.rsqrt(jnp.mean(xk * xk, axis=-1, keepdims=True) + RMS_EPS) * kw_ref[g:g + 1, :]
            v = zv[:, sl].astype(F32)
            if h < HPG:
                qh[h], kh[h], vh[h] = q, k, v
            else:
                _to_p4(q2, h - HPG, q, scr)
                _to_p4(k2, h - HPG, k, scr)
                _to_p4(v2, h - HPG, v, scr)

    hm = pl.BlockSpec((HPG, TM, HEAD), lambda i: (0, i, 0))
    p4 = pl.BlockSpec((2 * HPG, P4, TM // P4, HEAD), lambda i: (0, 0, i, 0))
    return pl.pallas_call(
        body, name="qkv_prep", grid=(S // TM,),
        in_specs=[pl.BlockSpec((TM, QKV_W), lambda i: (i, 0)), pl.BlockSpec((TM, QKV_W), lambda i: (i, 1)),
                  pl.BlockSpec((TM, QKV_W), lambda i: (i, 2)), _full((N_GROUPS, HEAD)), _full((N_GROUPS, HEAD))],
        out_specs=[hm, hm, hm, p4, p4, p4],
        out_shape=[SDS((HPG, S, HEAD), F32)] * 3 + [SDS((2 * HPG, P4, S // P4, HEAD), F32)] * 3,
        scratch_shapes=[pltpu.VMEM((TM, HEAD), F32)])(z, z, z, qw, kw)


def _rows(start, d):
    return pl.ds(start, NQ) if d == 1 else pl.ds(start, NQ, stride=d)


def _attn_fwd(qh, kh, vh, biasm, name, d, head0, nseq, bias_shift):
    S = qh.shape[1]
    RB = min(ATT_RB, S)
    nbk, nq = S // RB, RB // (NQ * d)

    def body(q_ref, k_ref, v_ref, bias_ref, o_ref, lse_ref, kbuf, vbuf):
        b = pl.program_id(1)

        @pl.when(b == 0)
        def _():
            kbuf[0:RB, :] = jnp.zeros((RB, HEAD), F32)
            vbuf[0:RB, :] = jnp.zeros((RB, HEAD), F32)

        @pl.when(b > 0)
        def _():
            kbuf[0:RB, :] = kbuf[RB:2 * RB, :]
            vbuf[0:RB, :] = vbuf[RB:2 * RB, :]

        kbuf[RB:2 * RB, :] = k_ref[...]
        vbuf[RB:2 * RB, :] = v_ref[...]
        bias = bias_ref[...]
        col = lax.broadcasted_iota(jnp.int32, (NQ, 2 * NQ), 1)

        for qb in range(nq):
            def unit(r, carry, qb=qb):
                qs = qb * NQ * d + r
                q = q_ref[_rows(qs, d), :].astype(BF16)
                kw = jnp.concatenate([kbuf[_rows(RB + qs - NQ * d, d), :], kbuf[_rows(RB + qs, d), :]], axis=0).astype(BF16)
                vw = jnp.concatenate([vbuf[_rows(RB + qs - NQ * d, d), :], vbuf[_rows(RB + qs, d), :]], axis=0).astype(BF16)
                s = lax.dot_general(q, kw, NT_DIMS, preferred_element_type=F32) * SCALE + bias
                if qb == 0:
                    s = jnp.where((col < NQ) & (b == 0), NEG, s)
                m = jnp.max(s, axis=-1, keepdims=True)
                p = jnp.exp(s - m)
                l = jnp.sum(p, axis=-1, keepdims=True)
                o = jnp.dot(p.astype(BF16), vw, preferred_element_type=F32) / l
                o_ref[_rows(qs, d), :] = o
                lse_ref[_rows(qs, d), :] = jnp.broadcast_to(m + jnp.log(l), (NQ, HEAD))
                return carry

            for r in range(d):
                unit(r, 0)

    blk = lambda f: pl.BlockSpec((None, RB, HEAD), f)
    return pl.pallas_call(
        body, name=name, grid=(nseq, nbk),
        in_specs=[blk(lambda h, b: (head0 + h, b, 0))] * 3
        + [pl.BlockSpec((None, NQ, 2 * NQ), lambda h, b: (jnp.right_shift(h, bias_shift), 0, 0))],
        out_specs=[blk(lambda h, b: (h, b, 0))] * 2,
        out_shape=[SDS((nseq, S, HEAD), F32)] * 2,
        scratch_shapes=[pltpu.VMEM((2 * RB, HEAD), F32)] * 2)(qh, kh, vh, biasm)


def _attn_bwd(qh, kh, vh, biasm, da, wg, dh, lse, name, d, head0, nseq, bias_shift):
    S = qh.shape[1]
    RB = min(ATT_RB, S)
    nbk, nq = S // RB, RB // (NQ * d)

    def body(q_ref, k_ref, v_ref, bias_ref, da_ref, wg_ref, dh_ref, lse_ref,
             dq_ref, dk_ref, dv_ref, dsb_ref, kbuf, vbuf, dkbuf, dvbuf):
        b = pl.program_id(1)
        zero = jnp.zeros((RB, HEAD), F32)

        @pl.when(b == 0)
        def _():
            kbuf[0:RB, :] = zero
            vbuf[0:RB, :] = zero
            dkbuf[0:RB, :] = zero
            dvbuf[0:RB, :] = zero
            dsb_ref[...] = jnp.zeros_like(dsb_ref)

        @pl.when(b > 0)
        def _():
            kbuf[0:RB, :] = kbuf[RB:2 * RB, :]
            vbuf[0:RB, :] = vbuf[RB:2 * RB, :]
            dkbuf[0:RB, :] = dkbuf[RB:2 * RB, :]
            dvbuf[0:RB, :] = dvbuf[RB:2 * RB, :]

        dkbuf[RB:2 * RB, :] = zero
        dvbuf[RB:2 * RB, :] = zero

        @pl.when(b < nbk)
        def _():
            kbuf[RB:2 * RB, :] = k_ref[...]
            vbuf[RB:2 * RB, :] = v_ref[...]
            bias = bias_ref[...]
            col = lax.broadcasted_iota(jnp.int32, (NQ, 2 * NQ), 1)

            for qb in range(nq):
                def unit(r, carry, qb=qb):
                    qs = qb * NQ * d + r
                    prev, cur = _rows(RB + qs - NQ * d, d), _rows(RB + qs, d)
                    q = q_ref[_rows(qs, d), :].astype(BF16)
                    kw = jnp.concatenate([kbuf[prev, :], kbuf[cur, :]], axis=0).astype(BF16)
                    vw = jnp.concatenate([vbuf[prev, :], vbuf[cur, :]], axis=0).astype(BF16)
                    s = lax.dot_general(q, kw, NT_DIMS, preferred_element_type=F32) * SCALE + bias
                    if qb == 0:
                        s = jnp.where((col < NQ) & (b == 0), NEG, s)
                    p = jnp.exp(s - lse_ref[_rows(qs, d), :][:, 0:1])
                    w = wg_ref[_rows(qs, d), :]
                    do = (da_ref[_rows(qs, d), :] * w).astype(BF16)
                    dp = lax.dot_general(do, vw, NT_DIMS, preferred_element_type=F32)
                    ds = p * (dp - w[:, 0:1] * dh_ref[_rows(qs, d), :][:, 0:1])
                    dsb_ref[...] += ds
                    dsb = ds.astype(BF16)
                    dq_ref[_rows(qs, d), :] = jnp.dot(dsb, kw, preferred_element_type=F32) * SCALE
                    dkw = lax.dot_general(dsb, q, TN_DIMS, preferred_element_type=F32) * SCALE
                    dvw = lax.dot_general(p.astype(BF16), do, TN_DIMS, preferred_element_type=F32)
                    dkbuf[prev, :] += dkw[0:NQ, :]
                    dkbuf[cur, :] += dkw[NQ:2 * NQ, :]
                    dvbuf[prev, :] += dvw[0:NQ, :]
                    dvbuf[cur, :] += dvw[NQ:2 * NQ, :]
                    return carry

                for r in range(d):
                    unit(r, 0)

        dk_ref[...] = dkbuf[done:done + RB, :]
        dv_ref[...] = dvbuf[done:done + RB, :]

    steps, done = (nbk + 1, 0) if nbk > 1 else (1, RB)
    blk = lambda f: pl.BlockSpec((None, RB, HEAD), f)
    cur_g = blk(lambda h, b: (head0 + h, jnp.minimum(b, nbk - 1), 0))
    cur = blk(lambda h, b: (h, jnp.minimum(b, nbk - 1), 0))
    prv = blk(lambda h, b: (h, jnp.maximum(b - 1, 0), 0))
    sq = pl.BlockSpec((None, NQ, 2 * NQ), lambda h, b: (h, 0, 0))
    return pl.pallas_call(
        body, name=name, grid=(nseq, steps),
        in_specs=[cur_g, cur_g, cur_g, pl.BlockSpec((None, NQ, 2 * NQ), lambda h, b: (jnp.right_shift(h, bias_shift), 0, 0)),
                  cur, cur, cur, cur],
        out_specs=[cur, prv, prv, sq],
        out_shape=[SDS((nseq, S, HEAD), F32)] * 3 + [SDS((nseq, NQ, 2 * NQ), F32)],
        scratch_shapes=[pltpu.VMEM((2 * RB, HEAD), F32)] * 4)(qh, kh, vh, biasm, da, wg, dh, lse)


def _merge_weights(l0, l1, l2):
    m = jnp.maximum(jnp.maximum(l0, l1), l2)
    e0, e1, e2 = jnp.exp(l0 - m), jnp.exp(l1 - m), jnp.exp(l2 - m)
    inv = 1.0 / (e0 + e1 + e2)
    return e0 * inv, e1 * inv, e2 * inv


def _merge_fwd(os_, lses):
    S = os_[0].shape[1]

    def body(o0, o1, o2, l0, l1, l2, a_ref, so, sl):
        for h in range(HPG):
            w0, w1, w2 = _merge_weights(l0[h], _from_p4(l1, h, so), _from_p4(l2, h, sl))
            a_ref[:, h * HEAD:(h + 1) * HEAD] = (w0 * o0[h] + w1 * _from_p4(o1, h, so) + w2 * _from_p4(o2, h, sl)).astype(BF16)

    hm = pl.BlockSpec((HPG, TM, HEAD), lambda i: (0, i, 0))
    p4 = pl.BlockSpec((HPG, P4, TM // P4, HEAD), lambda i: (0, 0, i, 0))
    return pl.pallas_call(
        body, name="merge_fwd", grid=(S // TM,), in_specs=[hm, p4, p4, hm, p4, p4],
        out_specs=pl.BlockSpec((TM, CH), lambda i: (i, 0)),
        out_shape=SDS((S, CH), BF16), scratch_shapes=[pltpu.VMEM((TM, HEAD), F32)] * 2)(*os_, *lses)


def _merge_bwd(dattn, os_, lses):
    S = dattn.shape[0]

    def body(da_ref, o0, o1, o2, l0, l1, l2, w0_ref, w1_ref, w2_ref, dah_ref, dh_ref, dah2_ref, dh2_ref, so, sl):
        for h in range(HPG):
            w = _merge_weights(l0[h], _from_p4(l1, h, so), _from_p4(l2, h, sl))
            attn = w[0] * o0[h] + w[1] * _from_p4(o1, h, so) + w[2] * _from_p4(o2, h, sl)
            da = da_ref[:, h * HEAD:(h + 1) * HEAD]
            dh = jnp.broadcast_to(jnp.sum(da * attn, axis=-1, keepdims=True), (TM, HEAD))
            w0_ref[h] = w[0]
            dah_ref[h] = da
            dh_ref[h] = dh
            _to_p4(w1_ref, h, w[1], so)
            _to_p4(w2_ref, h, w[2], so)
            _to_p4(dah2_ref, h, da, so)
            _to_p4(dh2_ref, h, dh, so)

    hm = pl.BlockSpec((HPG, TM, HEAD), lambda i: (0, i, 0))
    p4 = pl.BlockSpec((HPG, P4, TM // P4, HEAD), lambda i: (0, 0, i, 0))
    nat, perm = SDS((HPG, S, HEAD), F32), SDS((HPG, P4, S // P4, HEAD), F32)
    w0, w1, w2, dah, dh, dah2, dh2 = pl.pallas_call(
        body, name="merge_bwd", grid=(S // TM,),
        in_specs=[pl.BlockSpec((TM, CH), lambda i: (i, 0)), hm, p4, p4, hm, p4, p4],
        out_specs=[hm, p4, p4, hm, hm, p4, p4], out_shape=[nat, perm, perm, nat, nat, perm, perm],
        scratch_shapes=[pltpu.VMEM((TM, HEAD), F32)] * 2)(dattn, *os_, *lses)
    return (w0, dah, dh), (w1, dah2, dh2), (w2, dah2, dh2)


def _qkv_bwd(dz, z, dqs, dks, dvs, qw, kw):
    S = z.shape[0]
    nh = N_GROUPS * HPG

    def body(dz_in, zq, zk, *refs):
        del dz_in
        dq_refs, dk_refs, dv_refs = refs[0:3], refs[3:6], refs[6:9]
        qw_ref, kw_ref, dz_ref, dqw_ref, dkw_ref, scr = refs[9:]

        @pl.when(pl.program_id(0) == 0)
        def _():
            dqw_ref[...] = jnp.zeros_like(dqw_ref)
            dkw_ref[...] = jnp.zeros_like(dkw_ref)

        def grad(refs3, g, hh):
            return _from_p4(refs3[g], hh, scr) if g > 0 else refs3[g][hh]

        def nbwd(xr, dy, wr, dwr, h, off):
            g = h // HPG
            x = xr[:, h * HEAD:(h + 1) * HEAD].astype(F32)
            r = lax.rsqrt(jnp.mean(x * x, axis=-1, keepdims=True) + RMS_EPS)
            xhat = x * r
            dyw = dy * wr[g:g + 1, :]
            dz_ref[:, off + h * HEAD:off + (h + 1) * HEAD] = (
                r * (dyw - xhat * jnp.mean(dyw * xhat, axis=-1, keepdims=True))).astype(BF16)
            dwr[g:g + 1, :] += jnp.sum(dy * xhat, axis=0, keepdims=True)

        for h in range(nh):
            g, hh = h // HPG, h % HPG
            nbwd(zq, grad(dq_refs, g, hh), qw_ref, dqw_ref, h, O_Q)
            nbwd(zk, grad(dk_refs, g, hh), kw_ref, dkw_ref, h, O_K)
            dz_ref[:, O_V + h * HEAD:O_V + (h + 1) * HEAD] = grad(dv_refs, g, hh).astype(BF16)

    hm = pl.BlockSpec((HPG, TM, HEAD), lambda i: (0, i, 0))
    p4 = pl.BlockSpec((HPG, P4, TM // P4, HEAD), lambda i: (0, 0, i, 0))
    return pl.pallas_call(
        body, name="qkv_bwd", grid=(S // TM,),
        in_specs=[pl.BlockSpec(memory_space=pl.ANY), pl.BlockSpec((TM, QKV_W), lambda i: (i, 0)),
                  pl.BlockSpec((TM, QKV_W), lambda i: (i, 1))] + [hm, p4, p4] * 3 + [_full((N_GROUPS, HEAD)), _full((N_GROUPS, HEAD))],
        out_specs=[pl.BlockSpec((TM, 3 * QKV_W), lambda i: (i, 0)), _full((N_GROUPS, HEAD)), _full((N_GROUPS, HEAD))],
        out_shape=[SDS(dz.shape, BF16), SDS((N_GROUPS, HEAD), F32), SDS((N_GROUPS, HEAD), F32)],
        scratch_shapes=[pltpu.VMEM((TM, HEAD), F32)],
        input_output_aliases={0: 0})(dz, z, z, *dqs, *dks, *dvs, qw, kw)


def _conv_fwd(z, cw, cb, lnw, lnb):
    S = z.shape[0]
    H = 32

    def body(zv, zg, cw_ref, cb_ref, lnw_ref, lnb_ref, u1_ref, u3_ref, xbuf):
        i = pl.program_id(0)

        @pl.when(i == 0)
        def _():
            xbuf[0:H, :] = jnp.zeros((H, CH), F32)

        @pl.when(i > 0)
        def _():
            xbuf[0:H, :] = xbuf[TM:TM + H, :]

        xbuf[H:H + TM, :] = zv[...].astype(F32) * _sig(zg[...].astype(F32))
        for r0 in range(0, TM, CONV_RC):
            rows = pl.ds(r0, CONV_RC)
            parts = []
            for lanes in CONV_LANES:
                part = [jnp.broadcast_to(cb_ref[:, lanes], (CONV_RC, CONV_LB))]

                def tap(q, view, part=part, lanes=lanes):
                    part[0] = part[0] + view * cw_ref[q - 2:q - 1, lanes]

                _conv_taps(xbuf, r0, lanes, 2, CONV_K + 1, tap)
                parts.append(part[0])
            acc = jnp.concatenate(parts, axis=1)
            u1_ref[rows, :] = acc
            mu = jnp.mean(acc, axis=-1, keepdims=True)
            xc = acc - mu
            yl = xc * lax.rsqrt(jnp.mean(xc * xc, axis=-1, keepdims=True) + LN_EPS) * lnw_ref[...] + lnb_ref[...]
            u3_ref[rows, :] = (yl * _sig(yl)).astype(BF16)

    row = pl.BlockSpec((TM, CH), lambda i: (i, 0))
    return pl.pallas_call(
        body, name="conv_fwd", grid=(S // TM,),
        in_specs=[pl.BlockSpec((TM, CH), lambda i: (i, O_CV // CH)), pl.BlockSpec((TM, CH), lambda i: (i, O_CG // CH)),
                  _full((CONV_K, CH)), _full((1, CH)), _full((1, CH)), _full((1, CH))],
        out_specs=[row, row], out_shape=[SDS((S, CH), F32), SDS((S, CH), BF16)],
        scratch_shapes=[pltpu.VMEM((TM + H, CH), F32)])(z, z, cw, cb, lnw, lnb)


def _conv_bwd_a(du3, u1, z, lnw, lnb):
    S = z.shape[0]
    H = 32

    nt = S // TM

    def body(du3_ref, u1_ref, zv, zg, lnw_ref, lnb_ref, du1_ref, acc_ref, xbuf, tacc):
        i = pl.program_id(0)

        @pl.when(i == 0)
        def _():
            xbuf[0:H, :] = jnp.zeros((H, CH), F32)
            tacc[...] = jnp.zeros_like(tacc)

        @pl.when(i > 0)
        def _():
            xbuf[0:H, :] = xbuf[TM:TM + H, :]

        xbuf[H:H + TM, :] = zv[...].astype(F32) * _sig(zg[...].astype(F32))
        for r0 in range(0, TM, CONV_RC):
            rows = pl.ds(r0, CONV_RC)
            u1 = u1_ref[rows, :]
            mu = jnp.mean(u1, axis=-1, keepdims=True)
            xc = u1 - mu
            r = lax.rsqrt(jnp.mean(xc * xc, axis=-1, keepdims=True) + LN_EPS)
            yhat = xc * r
            yl = yhat * lnw_ref[...] + lnb_ref[...]
            sg = _sig(yl)
            dyl = du3_ref[rows, :] * (sg * (1.0 + yl * (1.0 - sg)))
            dyh = dyl * lnw_ref[...]
            du1 = r * (dyh - jnp.mean(dyh, axis=-1, keepdims=True) - yhat * jnp.mean(dyh * yhat, axis=-1, keepdims=True))
            du1_ref[rows, :] = du1
            tacc[33] += _fold8(dyl * yhat)
            tacc[34] += _fold8(dyl)
            tacc[32] += _fold8(du1)
            for lanes in CONV_LANES:
                d = du1[:, lanes]

                def tap(q, view, d=d, lanes=lanes):
                    tacc[q - 2, :, lanes] += _fold8(d * view)

                _conv_taps(xbuf, r0, lanes, 2, CONV_K + 1, tap)

        @pl.when(i == nt - 1)
        def _():
            for k in range(40):
                acc_ref[k:k + 1, :] = jnp.sum(tacc[k], axis=0, keepdims=True)

    row = pl.BlockSpec((TM, CH), lambda i: (i, 0))
    return pl.pallas_call(
        body, name="conv_bwd_a", grid=(nt,),
        in_specs=[row, row, pl.BlockSpec((TM, CH), lambda i: (i, O_CV // CH)), pl.BlockSpec((TM, CH), lambda i: (i, O_CG // CH)),
                  _full((1, CH)), _full((1, CH))],
        out_specs=[row, _full((40, CH))], out_shape=[SDS((S, CH), F32), SDS((40, CH), F32)],
        scratch_shapes=[pltpu.VMEM((TM + H, CH), F32), pltpu.VMEM((40, 8, CH), F32)])(du3, u1, z, z, lnw, lnb)


def _conv_bwd_b(dz, du1, z, cw):
    S = z.shape[0]
    nt = S // TM
    H = 32

    def body(dz_in, du1_ref, zv, zg, cw_ref, dz_ref, ybuf, dgate):
        del dz_in
        i, p = pl.program_id(0), pl.program_id(1)

        @pl.when(p == 0)
        def _():
            @pl.when(i == 0)
            def _():
                ybuf[TM:TM + H, :] = jnp.zeros((H, CH), F32)

            @pl.when(i > 0)
            def _():
                ybuf[TM:TM + H, :] = ybuf[0:H, :]

            ybuf[0:TM, :] = du1_ref[...]
            for r0 in range(0, TM, CONV_RC):
                rows = pl.ds(r0, CONV_RC)
                parts = []
                for lanes in CONV_LANES:
                    part = [jnp.zeros((CONV_RC, CONV_LB), F32)]

                    def tap(q, view, part=part, lanes=lanes):
                        part[0] = part[0] + view * cw_ref[CONV_K - 1 - q:CONV_K - q, lanes]

                    _conv_taps(ybuf, r0, lanes, 0, CONV_K - 1, tap)
                    parts.append(part[0])
                acc = jnp.concatenate(parts, axis=1)
                val = zv[rows, :].astype(F32)
                sg = _sig(zg[rows, :].astype(F32))
                dz_ref[rows, :] = (acc * sg).astype(BF16)
                dgate[rows, :] = (acc * val * sg * (1.0 - sg)).astype(BF16)

        @pl.when(p == 1)
        def _():
            dz_ref[...] = dgate[...]

    rev = lambda c: pl.BlockSpec((TM, CH), lambda i, p: (nt - 1 - i, c))
    return pl.pallas_call(
        body, name="conv_bwd_b", grid=(nt, 2),
        in_specs=[pl.BlockSpec(memory_space=pl.ANY), rev(0), rev(O_CV // CH), rev(O_CG // CH), _full((CONV_K, CH))],
        out_specs=pl.BlockSpec((TM, CH), lambda i, p: (nt - 1 - i, O_CV // CH + p)),
        out_shape=SDS(dz.shape, BF16),
        scratch_shapes=[pltpu.VMEM((TM + H, CH), F32), pltpu.VMEM((TM, CH), BF16)],
        input_output_aliases={0: 0})(dz, du1, z, z, cw)


def _memkv_fwd(mem, mnw, wkv, xkw):
    M, D = mem.shape

    def body(mem_ref, mnw_ref, w_ref, xkw_ref, mn_ref, kv_ref, mk_ref, mv_ref):
        x = mem_ref[...]
        mn = (x * lax.rsqrt(jnp.mean(x * x, axis=-1, keepdims=True) + RMS_EPS) * mnw_ref[...]).astype(BF16)
        mn_ref[...] = mn
        kv = jnp.dot(mn, w_ref[...], preferred_element_type=F32)
        kv_ref[...] = kv
        for h in range(HPG):
            k = kv[:, h * HEAD:(h + 1) * HEAD]
            mk_ref[:, h * HEAD:(h + 1) * HEAD] = (
                k * lax.rsqrt(jnp.mean(k * k, axis=-1, keepdims=True) + RMS_EPS) * xkw_ref[...]).astype(BF16)
        mv_ref[...] = kv[:, CH:2 * CH].astype(BF16)

    return pl.pallas_call(
        body, name="memkv_fwd",
        out_shape=[SDS((M, D), BF16), SDS((M, 2 * CH), F32), SDS((M, CH), BF16), SDS((M, CH), BF16)])(mem, mnw, wkv, xkw)


def _cross_q(zx, xqw, h):
    x = zx[:, h * HEAD:(h + 1) * HEAD].astype(F32)
    r = lax.rsqrt(jnp.mean(x * x, axis=-1, keepdims=True) + RMS_EPS)
    xhat = x * r
    return xhat, r, xhat * xqw


def _cross_fwd(z, xqw, mk, mv):
    S = z.shape[0]
    M = mk.shape[0]

    def body(zx, xqw_ref, mk_ref, mv_ref, o_ref):
        for h in range(HPG):
            sl = slice(h * HEAD, (h + 1) * HEAD)
            _, _, q = _cross_q(zx, xqw_ref[...], h)
            s = lax.dot_general(q.astype(BF16), mk_ref[:, sl], NT_DIMS, preferred_element_type=F32) * SCALE
            e = jnp.exp(s - jnp.max(s, axis=-1, keepdims=True))
            p = e / jnp.sum(e, axis=-1, keepdims=True)
            o_ref[:, sl] = jnp.dot(p.astype(BF16), mv_ref[:, sl], preferred_element_type=F32).astype(BF16)

    return pl.pallas_call(
        body, name="cross_fwd", grid=(S // TM,),
        in_specs=[pl.BlockSpec((TM, CH), lambda i: (i, O_XQ // CH)), _full((1, HEAD)), _full((M, CH)), _full((M, CH))],
        out_specs=pl.BlockSpec((TM, CH), lambda i: (i, 0)), out_shape=SDS((S, CH), BF16))(z, xqw, mk, mv)


def _cross_bwd(dz, doc, z, xqw, mk, mv):
    S = z.shape[0]
    M = mk.shape[0]

    def body(dz_in, do_ref, zx, xqw_ref, mk_ref, mv_ref, dz_ref, dmk_ref, dmv_ref, dxw_ref):
        del dz_in

        @pl.when(pl.program_id(0) == 0)
        def _():
            dmk_ref[...] = jnp.zeros_like(dmk_ref)
            dmv_ref[...] = jnp.zeros_like(dmv_ref)
            dxw_ref[...] = jnp.zeros_like(dxw_ref)

        for h in range(HPG):
            sl = slice(h * HEAD, (h + 1) * HEAD)
            xhat, r, q = _cross_q(zx, xqw_ref[...], h)
            qb = q.astype(BF16)
            s = lax.dot_general(qb, mk_ref[:, sl], NT_DIMS, preferred_element_type=F32) * SCALE
            e = jnp.exp(s - jnp.max(s, axis=-1, keepdims=True))
            p = e / jnp.sum(e, axis=-1, keepdims=True)
            do = do_ref[:, sl].astype(BF16)
            dp = lax.dot_general(do, mv_ref[:, sl], NT_DIMS, preferred_element_type=F32)
            ds = (p * (dp - jnp.sum(p * dp, axis=-1, keepdims=True)) * SCALE).astype(BF16)
            dq = jnp.dot(ds, mk_ref[:, sl], preferred_element_type=F32)
            dmk_ref[:, sl] += lax.dot_general(ds, qb, TN_DIMS, preferred_element_type=F32)
            dmv_ref[:, sl] += lax.dot_general(p.astype(BF16), do, TN_DIMS, preferred_element_type=F32)
            dyw = dq * xqw_ref[...]
            dz_ref[:, sl] = (r * (dyw - xhat * jnp.mean(dyw * xhat, axis=-1, keepdims=True))).astype(BF16)
            dxw_ref[...] += jnp.sum(dq * xhat, axis=0, keepdims=True)

    return pl.pallas_call(
        body, name="cross_bwd", grid=(S // TM,),
        in_specs=[pl.BlockSpec(memory_space=pl.ANY), pl.BlockSpec((TM, CH), lambda i: (i, 0)),
                  pl.BlockSpec((TM, CH), lambda i: (i, O_XQ // CH)), _full((1, HEAD)), _full((M, CH)), _full((M, CH))],
        out_specs=[pl.BlockSpec((TM, CH), lambda i: (i, O_XQ // CH)), _full((M, CH)), _full((M, CH)), _full((1, HEAD))],
        out_shape=[SDS(dz.shape, BF16), SDS((M, CH), F32), SDS((M, CH), F32), SDS((1, HEAD), F32)],
        input_output_aliases={0: 0})(dz, doc, z, xqw, mk, mv)


def _memkv_bwd(dmk, dmv, kv, mem, mn, mnw, wkv, xkw):
    M, D = mem.shape

    def body(dmk_ref, dmv_ref, kv_ref, mem_ref, mn_ref, mnw_ref, w_ref, xkw_ref, dw_ref, dxk_ref, dmn_ref, dkv):
        dxk = jnp.zeros((1, HEAD), F32)
        for h in range(HPG):
            sl = slice(h * HEAD, (h + 1) * HEAD)
            k = kv_ref[:, sl]
            r = lax.rsqrt(jnp.mean(k * k, axis=-1, keepdims=True) + RMS_EPS)
            khat = k * r
            dy = dmk_ref[:, sl]
            dyw = dy * xkw_ref[...]
            dkv[:, sl] = (r * (dyw - khat * jnp.mean(dyw * khat, axis=-1, keepdims=True))).astype(BF16)
            dxk = dxk + jnp.sum(dy * khat, axis=0, keepdims=True)
        dxk_ref[...] = dxk
        dkv[:, CH:2 * CH] = dmv_ref[...].astype(BF16)
        dw_ref[...] = lax.dot_general(mn_ref[...], dkv[...], TN_DIMS, preferred_element_type=F32)
        dn = lax.dot_general(dkv[...], w_ref[...], NT_DIMS, preferred_element_type=F32)
        x = mem_ref[...]
        r = lax.rsqrt(jnp.mean(x * x, axis=-1, keepdims=True) + RMS_EPS)
        dmn_ref[...] = jnp.sum(dn * x * r, axis=0, keepdims=True)

    return pl.pallas_call(
        body, name="memkv_bwd",
        out_shape=[SDS((D, 2 * CH), F32), SDS((1, HEAD), F32), SDS((1, D), F32)],
        scratch_shapes=[pltpu.VMEM((M, 2 * CH), BF16)])(dmk, dmv, kv, mem, mn, mnw, wkv, xkw)


def _branch_proj(a_ref, w_ref, y_ref):
    G, _, n = w_ref.shape
    a = a_ref[...]
    for g in range(G):
        y_ref[:, g * n:(g + 1) * n] = jnp.dot(a, w_ref[g], preferred_element_type=F32)


OUT_RC = 16


def _gates(zg_ref, bg_ref, rows, k, D):
    return _sig(zg_ref[rows, k * D:(k + 1) * D].astype(F32) + bg_ref[:, k * D:(k + 1) * D])


def _outproj_fwd(x, z, bg, attn, u3, oc, wao, wco, wxo, wout, fnw):
    S, D = x.shape
    tm = TM

    def body(x_ref, zg_ref, bg_ref, a_ref, u_ref, c_ref, wa, wc, wx, wo, fnw_ref, h1_ref, hn_ref, ya, yc, yx, mg):
        _branch_proj(a_ref, wa, ya)
        _branch_proj(u_ref, wc, yc)
        _branch_proj(c_ref, wx, yx)
        for r0 in range(0, tm, OUT_RC):
            rows = pl.ds(r0, OUT_RC)
            mg[rows, :] = (_gates(zg_ref, bg_ref, rows, 0, D) * ya[rows, :] + _gates(zg_ref, bg_ref, rows, 1, D) * yc[rows, :]
                           + _gates(zg_ref, bg_ref, rows, 2, D) * yx[rows, :]).astype(BF16)
        ya[...] = jnp.dot(mg[...], wo[...], preferred_element_type=F32)
        for r0 in range(0, tm, OUT_RC):
            rows = pl.ds(r0, OUT_RC)
            h1 = x_ref[rows, :] + ya[rows, :]
            h1_ref[rows, :] = h1
            hn_ref[rows, :] = (h1 * lax.rsqrt(jnp.mean(h1 * h1, axis=-1, keepdims=True) + RMS_EPS) * fnw_ref[...]).astype(BF16)

    row = lambda w: pl.BlockSpec((tm, w), lambda i: (i, 0))
    return pl.pallas_call(
        body, name="outproj_fwd", grid=(S // tm,),
        in_specs=[row(D), pl.BlockSpec((tm, 3 * D), lambda i: (i, O_G // (3 * D))), _full((1, 3 * D)), row(CH), row(CH), row(CH),
                  _full(wao.shape), _full(wco.shape), _full(wxo.shape), _full((D, D)), _full((1, D))],
        out_specs=[row(D), row(D)], out_shape=[SDS((S, D), F32), SDS((S, D), BF16)],
        scratch_shapes=[pltpu.VMEM((tm, D), F32)] * 3 + [pltpu.VMEM((tm, D), BF16)])(x, z, bg, attn, u3, oc, wao, wco, wxo, wout, fnw)


def _outproj_bwd(dh1, z, bg, attn, u3, oc, wao, wco, wxo, wout, n_in):
    S, D = dh1.shape
    tm = 256
    nt = S // tm
    G, _, n = wao.shape

    def body(dh_ref, zg_ref, bg_ref, a_ref, u_ref, c_ref, wa, wc, wx, wo,
             dz_ref, da_ref, du_ref, dc_ref, dbg_ref, dwo_ref, dwa_ref, dwc_ref, dwx_ref,
             ya, yc, yx, dm, dy, mg, bacc, wacc):
        i = pl.program_id(0)

        @pl.when(i == 0)
        def _():
            bacc[...] = jnp.zeros_like(bacc)
            wacc[...] = jnp.zeros_like(wacc)
            dwo_ref[...] = jnp.zeros_like(dwo_ref)

        _branch_proj(a_ref, wa, ya)
        _branch_proj(u_ref, wc, yc)
        _branch_proj(c_ref, wx, yx)
        dhb = dh_ref[...].astype(BF16)
        dm[...] = lax.dot_general(dhb, wo[...], NT_DIMS, preferred_element_type=F32)
        for r0 in range(0, tm, OUT_RC):
            rows = pl.ds(r0, OUT_RC)
            dmv = dm[rows, :]
            merged = jnp.zeros((OUT_RC, D), F32)
            for k, y in enumerate((ya, yc, yx)):
                gk = _gates(zg_ref, bg_ref, rows, k, D)
                yk = y[rows, :]
                merged = merged + gk * yk
                dzg = dmv * yk * gk * (1.0 - gk)
                dz_ref[rows, k * D:(k + 1) * D] = dzg.astype(BF16)
                bacc[:, k * D:(k + 1) * D] += _fold8(dzg)
                dy[k, rows, :] = (dmv * gk).astype(BF16)
            mg[rows, :] = merged.astype(BF16)
        dwo_ref[...] += lax.dot_general(mg[...], dhb, TN_DIMS, preferred_element_type=F32)
        for k, (b_ref, w_ref, db_ref) in enumerate(((a_ref, wa, da_ref), (u_ref, wc, du_ref), (c_ref, wx, dc_ref))):
            dyk = dy[k]
            acc = jnp.zeros((tm, CH), F32)
            for g in range(G):
                acc = acc + lax.dot_general(dyk[:, g * n:(g + 1) * n], w_ref[g], NT_DIMS, preferred_element_type=F32)
            db_ref[...] = acc
            wacc[k] += lax.dot_general(b_ref[...], dyk, TN_DIMS, preferred_element_type=F32)

        @pl.when(i == nt - 1)
        def _():
            dbg_ref[...] = jnp.sum(bacc[...], axis=0, keepdims=True)
            for k, dw_ref in enumerate((dwa_ref, dwc_ref, dwx_ref)):
                for g in range(G):
                    dw_ref[g] = wacc[k, :, g * n:(g + 1) * n]

    row = lambda w: pl.BlockSpec((tm, w), lambda i: (i, 0))
    return pl.pallas_call(
        body, name="outproj_bwd", grid=(nt,),
        in_specs=[row(D), pl.BlockSpec((tm, 3 * D), lambda i: (i, O_G // (3 * D))), _full((1, 3 * D)), row(CH), row(CH), row(CH),
                  _full(wao.shape), _full(wco.shape), _full(wxo.shape), _full((D, D))],
        out_specs=[pl.BlockSpec((tm, 3 * D), lambda i: (i, O_G // (3 * D))), row(CH), row(CH), row(CH), _full((1, 3 * D)),
                   _full((D, D))] + [_full(wao.shape)] * 3,
        out_shape=[SDS((S, n_in), BF16)] + [SDS((S, CH), F32)] * 3 + [SDS((1, 3 * D), F32), SDS((D, D), F32)]
        + [SDS(wao.shape, F32)] * 3,
        scratch_shapes=[pltpu.VMEM((tm, D), F32)] * 4 + [pltpu.VMEM((3, tm, D), BF16), pltpu.VMEM((tm, D), BF16),
                                                        pltpu.VMEM((8, 3 * D), F32), pltpu.VMEM((3, CH, D), F32)],
    )(dh1, z, bg, attn, u3, oc, wao, wco, wxo, wout)


FFN_TC = 256
FFN_H = 8


def _ffn_taps(buf, r0):
    xx = buf[pl.ds(r0, FFN_RC + FFN_H), :]
    return xx[FFN_H:], pltpu.roll(xx, 1, 0)[FFN_H:], pltpu.roll(xx, 2, 0)[FFN_H:]


def _ffn_conv(taps, w_ref, b_ref):
    x0, x1, x2 = taps
    return b_ref[...] + x0 * w_ref[2:3, :] + x1 * w_ref[1:2, :] + x2 * w_ref[0:1, :]


def _ffn_fwd(up, cw, cb, wdown, h1, target):
    S, D = h1.shape
    F2 = up.shape[1]
    F = F2 // 2
    nj = F // FFN_TC
    tm = TM

    def body(up_ref, halo_ref, cw_ref, cb_ref, w_ref, h_ref, t_ref, dy_ref, loss_ref, ac_ref, gc_ref, abuf, gbuf, act_s):
        i = pl.program_id(0)

        @pl.when(i == 0)
        def _():
            loss_ref[...] = jnp.zeros_like(loss_ref)

        for j in range(nj):
            ca, cg = slice(j * FFN_TC, (j + 1) * FFN_TC), slice(F + j * FFN_TC, F + (j + 1) * FFN_TC)
            first = i == 0
            abuf[0:FFN_H, :] = jnp.where(first, 0.0, halo_ref[:, ca].astype(F32))
            gbuf[0:FFN_H, :] = jnp.where(first, 0.0, halo_ref[:, cg].astype(F32))
            abuf[FFN_H:FFN_H + tm, :] = up_ref[:, ca].astype(F32)
            gbuf[FFN_H:FFN_H + tm, :] = up_ref[:, cg].astype(F32)
            for r0 in range(0, tm, FFN_RC):
                rows = pl.ds(r0, FFN_RC)
                a = _ffn_conv(_ffn_taps(abuf, r0), cw_ref[:, ca], cb_ref[:, ca])
                gt = _ffn_conv(_ffn_taps(gbuf, r0), cw_ref[:, cg], cb_ref[:, cg])
                ac_ref[rows, ca] = a.astype(BF16)
                gc_ref[rows, ca] = gt.astype(BF16)
                act_s[rows, ca] = (gt * _sig(gt) * a).astype(BF16)
        err = h_ref[...] + jnp.dot(act_s[...], w_ref[...], preferred_element_type=F32) - t_ref[...]
        dy_ref[...] = err * (1.0 / D)
        loss_ref[...] += 0.5 * jnp.sum(jnp.mean(err * err, axis=-1, keepdims=True))

    row = lambda w: pl.BlockSpec((tm, w), lambda i: (i, 0))
    halo = pl.BlockSpec((FFN_H, F2), lambda i: (jnp.maximum(i * (tm // FFN_H) - 1, 0), 0))
    return pl.pallas_call(
        body, name="ffn_fwd", grid=(S // tm,),
        in_specs=[row(F2), halo, _full((FFN_K, F2)), _full((1, F2)), _full((F, D)), row(D), row(D)],
        out_specs=[row(D), _full((8, 128)), row(F), row(F)],
        out_shape=[SDS((S, D), F32), SDS((8, 128), F32), SDS((S, F), BF16), SDS((S, F), BF16)],
        scratch_shapes=[pltpu.VMEM((tm + FFN_H, FFN_TC), F32)] * 2 + [pltpu.VMEM((tm, F), BF16)])(up, up, cw, cb, wdown, h1, target)


def _ffn_bwd_a(dy, wdown, ac, gc):
    S, D = dy.shape
    F = ac.shape[1]
    nj = F // FFN_TC
    tm = 1024 if S % 1024 == 0 else TM

    def body(dy_ref, wd_ref, a_ref, g_ref, da_ref, dg_ref, acca_ref, accg_ref, dwd_ref, dact_s, act_s):
        i, j = pl.program_id(0), pl.program_id(1)

        @pl.when((i == 0) & (j == 0))
        def _():
            acca_ref[...] = jnp.zeros_like(acca_ref)
            accg_ref[...] = jnp.zeros_like(accg_ref)
            dwd_ref[...] = jnp.zeros_like(dwd_ref)

        dyb = dy_ref[...].astype(BF16)
        dact_s[...] = lax.dot_general(dyb, wd_ref[...], NT_DIMS, preferred_element_type=F32)
        pa = pg = jnp.zeros((8, FFN_TC), F32)
        for r0 in range(0, tm, FFN_RC):
            rows = pl.ds(r0, FFN_RC)
            a = a_ref[rows, :].astype(F32)
            gt = g_ref[rows, :].astype(F32)
            dact = dact_s[rows, :]
            sg = _sig(gt)
            silu = gt * sg
            act_s[rows, :] = (silu * a).astype(BF16)
            dac = dact * silu
            dgc = dact * a * (sg * (1.0 + gt * (1.0 - sg)))
            da_ref[rows, :] = dac.astype(BF16)
            dg_ref[rows, :] = dgc.astype(BF16)
            pa = pa + _fold8(dac)
            pg = pg + _fold8(dgc)
        acca_ref[j] += pa
        accg_ref[j] += pg
        dwd_ref[pl.ds(pl.multiple_of(j * FFN_TC, FFN_TC), FFN_TC), :] += lax.dot_general(
            act_s[...], dyb, TN_DIMS, preferred_element_type=F32)

    col = pl.BlockSpec((tm, FFN_TC), lambda i, j: (i, j))
    return pl.pallas_call(
        body, name="ffn_bwd_a", grid=(S // tm, nj),
        in_specs=[pl.BlockSpec((tm, D), lambda i, j: (i, 0)), pl.BlockSpec((FFN_TC, D), lambda i, j: (j, 0)), col, col],
        out_specs=[col, col] + [_full((nj, 8, FFN_TC))] * 2 + [_full((F, D))],
        out_shape=[SDS((S, F), BF16)] * 2 + [SDS((nj, 8, FFN_TC), F32)] * 2 + [SDS((F, D), F32)],
        scratch_shapes=[pltpu.VMEM((tm, FFN_TC), F32), pltpu.VMEM((tm, FFN_TC), BF16)])(dy, wdown, ac, gc)


def _ffn_bwd_b(dca, dcg, up, cw):
    S, F = dca.shape
    nj = F // FFN_TC
    tm = 4096 if S % 4096 == 0 else TM
    nt = S // tm
    span = FFN_RC + FFN_H

    def body(a_ref, g_ref, u_ref, w_ref, o_ref, tacc_ref, ybuf):
        j, i = pl.program_id(0), pl.program_id(1)

        @pl.when(i == 0)
        def _():
            ybuf[tm:tm + FFN_H, :] = jnp.zeros((FFN_H, FFN_TC), F32)
            tacc_ref[...] = jnp.zeros_like(tacc_ref)

        @pl.when(i > 0)
        def _():
            ybuf[tm:tm + FFN_H, :] = ybuf[0:FFN_H, :]

        ybuf[0:tm, :] = jnp.where(j < nj, a_ref[...], g_ref[...]).astype(F32)
        p = [jnp.zeros((8, FFN_TC), F32)] * FFN_K
        for r0 in range(0, tm, FFN_RC):
            rows = pl.ds(r0, FFN_RC)
            yy = ybuf[pl.ds(r0, span), :]
            ys = (yy[:FFN_RC], pltpu.roll(yy, span - 1, 0)[:FFN_RC], pltpu.roll(yy, span - 2, 0)[:FFN_RC])
            o_ref[rows, :] = (ys[0] * w_ref[2:3, :] + ys[1] * w_ref[1:2, :] + ys[2] * w_ref[0:1, :]).astype(BF16)
            u = u_ref[rows, :].astype(F32)
            for k in range(FFN_K):
                p[k] = p[k] + _fold8(ys[FFN_K - 1 - k] * u)
        for k in range(FFN_K):
            tacc_ref[k] += p[k]

    rev = lambda f: pl.BlockSpec((tm, FFN_TC), lambda j, i: (nt - 1 - i, f(j)))
    return pl.pallas_call(
        body, name="ffn_bwd_b", grid=(2 * nj, nt),
        in_specs=[rev(lambda j: jnp.minimum(j, nj - 1)), rev(lambda j: jnp.maximum(j - nj, 0)), rev(lambda j: j),
                  pl.BlockSpec((FFN_K, FFN_TC), lambda j, i: (0, j))],
        out_specs=[rev(lambda j: j), pl.BlockSpec((None, FFN_K, 8, FFN_TC), lambda j, i: (j, 0, 0, 0))],
        out_shape=[SDS((S, 2 * F), BF16), SDS((2 * nj, FFN_K, 8, FFN_TC), F32)],
        scratch_shapes=[pltpu.VMEM((tm + FFN_H, FFN_TC), F32)])(dca, dcg, up, cw)


def _adamw_update(w_ref, g_ref, m_ref, v_ref, d_ref, nm_ref, nv_ref):
    gv = g_ref[...]
    m2 = ADAM_B1 * m_ref[...] + (1.0 - ADAM_B1) * gv
    v2 = ADAM_B2 * v_ref[...] + (1.0 - ADAM_B2) * jnp.square(gv)
    m_hat = m2 / (1.0 - ADAM_B1 ** ADAM_STEP)
    v_hat = v2 / (1.0 - ADAM_B2 ** ADAM_STEP)
    d_ref[...] = -ADAM_LR * (m_hat / (jnp.sqrt(v_hat) + ADAM_EPS) + ADAM_WD * w_ref[...])
    nm_ref[...] = m2
    nv_ref[...] = v2


def _adamw_small(ws, gs, ms, vs):
    n = len(ws)

    def body(*refs):
        for i in range(n):
            _adamw_update(*[refs[k * n + i] for k in range(7)])

    shapes = [SDS(w.shape, F32) for w in ws]
    res = pl.pallas_call(body, name="adamw_small", out_shape=shapes * 3)(*ws, *gs, *ms, *vs)
    return res[:n], res[n:2 * n], res[2 * n:]


def _adamw(w, g, m, v, name):
    R, C = w.shape
    tr = _row_tile(R, max(8, (2 ** 20) // (4 * C) // 8 * 8))

    def body(w_ref, g_ref, m_ref, v_ref, d_ref, nm_ref, nv_ref):
        _adamw_update(w_ref, g_ref, m_ref, v_ref, d_ref, nm_ref, nv_ref)

    blk = pl.BlockSpec((tr, C), lambda i: (i, 0))
    return pl.pallas_call(
        body, name=name, grid=(R // tr,), in_specs=[blk] * 4, out_specs=[blk] * 3,
        out_shape=[SDS((R, C), F32)] * 3)(w, g, m, v)


HBM_SPEC = pl.BlockSpec(memory_space=pltpu.HBM)
SEM_SPEC = pl.BlockSpec(memory_space=pltpu.SEMAPHORE)
DATAFLOW_EFFECT = pltpu.SideEffectType.DATAFLOW_SIDE_EFFECTING


def _position():
    return lax.axis_index("x"), lax.axis_index("y"), lax.axis_index("c")


def _other_chips(x, y):
    return [(1 - x, y), (x, 1 - y), (1 - x, 1 - y)]


def _relations(x, y, js=(0, 1, 2)):
    return [(j, chip) for j, chip in enumerate(_other_chips(x, y)) if j in js]


def _all_gather_xy(arrs, name):
    n = len(arrs)
    hbm = pl.BlockSpec(memory_space=pl.ANY)

    def body(*refs):
        ins, outs = refs[:n], refs[n:2 * n]
        send_sems, recv_sems = refs[2 * n:]
        x, y, c = _position()
        me = 2 * x + y
        chips = _other_chips(x, y)

        def rcopy(i, k, src, dst, to):
            return pltpu.make_async_remote_copy(src_ref=src, dst_ref=dst, send_sem=send_sems.at[i, k], recv_sem=recv_sems.at[i, k],
                                                device_id=to, device_id_type=MESH)

        sends = []
        for i in range(n):
            own = rcopy(i, 6, ins[i], outs[i].at[me], (x, y, 1 - c))
            own.start()
            sends.append(own)
        for i in range(n):
            for j, (px, py) in enumerate(chips):
                cp = rcopy(i, j, ins[i].at[c], outs[i].at[me, c], (px, py, c))
                cp.start()
                sends.append(cp)
        for i in range(n):
            for j, (px, py) in enumerate(chips):
                got = outs[i].at[2 * px + py, c]
                rcopy(i, j, ins[i].at[c], got, (x, y, c)).wait_recv()
                fwd = rcopy(i, 3 + j, got, got, (x, y, 1 - c))
                fwd.start()
                sends.append(fwd)
        for i in range(n):
            for j, (px, py) in enumerate(chips):
                theirs = outs[i].at[2 * px + py, 1 - c]
                rcopy(i, 3 + j, theirs, theirs, (x, y, c)).wait_recv()
        for i in range(n):
            rcopy(i, 6, ins[i], outs[i].at[me], (x, y, c)).wait_recv()
        for cp in sends:
            cp.wait_send()

    return pl.pallas_call(
        body, name=name, in_specs=[hbm] * n, out_specs=[hbm] * n,
        out_shape=[SDS((4,) + a.shape, a.dtype) for a in arrs],
        scratch_shapes=[pltpu.SemaphoreType.DMA((n, 7)), pltpu.SemaphoreType.DMA((n, 7))])(*arrs)


def _ag_ici_start(arrs, name, js=(0, 1, 2), lands=None):
    n = len(arrs)

    def body(*refs):
        ins, lands = refs[:n], refs[n:2 * n]
        send_sems, recv_sems = refs[2 * n:2 * n + 2]
        token = refs[-1]
        x, y, c = _position()
        for i in range(n):
            for j, (px, py) in _relations(x, y, js):
                pltpu.make_async_remote_copy(src_ref=ins[i].at[c], dst_ref=lands[i].at[2 * x + y, c], send_sem=send_sems.at[3 * i + j],
                                             recv_sem=recv_sems.at[3 * i + j], device_id=(px, py, c), device_id_type=MESH).start()
        token[...] = jnp.zeros_like(token)

    if lands is None:
        lands = [pltpu.with_memory_space_constraint(lax.empty((4,) + a.shape, a.dtype), pltpu.HBM) for a in arrs]
        arrs = [pltpu.with_memory_space_constraint(a, pltpu.HBM) for a in arrs]
    res = pl.pallas_call(
        body, name=name,
        out_shape=[pltpu.SemaphoreType.DMA((3 * n,)), pltpu.SemaphoreType.DMA((3 * n,))]
        + [pltpu.HBM(a.shape, a.dtype) for a in arrs] + [pltpu.HBM(l.shape, l.dtype) for l in lands] + [SDS((8, 128), F32)],
        in_specs=[HBM_SPEC] * (2 * n), out_specs=[SEM_SPEC, SEM_SPEC] + [HBM_SPEC] * (2 * n) + [pl.BlockSpec(memory_space=pltpu.VMEM)],
        input_output_aliases={i: 2 + i for i in range(2 * n)},
        compiler_params=pltpu.CompilerParams(has_side_effects=DATAFLOW_EFFECT),
    )(*arrs, *lands)
    return res[0], res[1], list(res[2:2 + n]), list(res[2 + n:2 + 2 * n]), res[-1]


def _ag_ici_wait(send_sems, recv_sems, ins, lands, after, name, js=(0, 1, 2)):
    n = len(ins)

    def body(*refs):
        ins_r, lands_r = refs[:n], refs[n:2 * n]
        send_r, recv_r = refs[2 * n:2 * n + 2]
        x, y, c = _position()
        for i in range(n):
            for j, (px, py) in _relations(x, y, js):
                cp = pltpu.make_async_remote_copy(src_ref=ins_r[i].at[c], dst_ref=lands_r[i].at[2 * px + py, c], send_sem=send_r.at[3 * i + j],
                                                  recv_sem=recv_r.at[3 * i + j], device_id=(px, py, c), device_id_type=MESH)
                cp.wait_send()
                cp.wait_recv()

    res = pl.pallas_call(
        body, name=name,
        out_shape=[pltpu.HBM(a.shape, a.dtype) for a in list(ins) + list(lands)],
        in_specs=[HBM_SPEC] * (2 * n) + [SEM_SPEC, SEM_SPEC, pl.BlockSpec(memory_space=pl.ANY)], out_specs=[HBM_SPEC] * (2 * n),
        input_output_aliases={i: i for i in range(2 * n)},
        compiler_params=pltpu.CompilerParams(has_side_effects=DATAFLOW_EFFECT),
    )(*ins, *lands, send_sems, recv_sems, after)
    return list(res[:n]), list(res[n:])


def _ag_finish_start(arrs, lands, name, js=(0, 1, 2), own=True):
    n = len(arrs)

    def body(*refs):
        ins, bufs = refs[:n], refs[n:2 * n]
        send_sems, recv_sems = refs[2 * n:2 * n + 2]
        token = refs[-1]
        x, y, c = _position()
        for i in range(n):
            if own:
                pltpu.make_async_remote_copy(src_ref=ins[i], dst_ref=bufs[i].at[2 * x + y], send_sem=send_sems.at[4 * i + 3],
                                             recv_sem=recv_sems.at[4 * i + 3], device_id=(x, y, 1 - c), device_id_type=MESH).start()
            for j, (px, py) in _relations(x, y, js):
                half = bufs[i].at[2 * px + py, c]
                pltpu.make_async_remote_copy(src_ref=half, dst_ref=half, send_sem=send_sems.at[4 * i + j],
                                             recv_sem=recv_sems.at[4 * i + j], device_id=(x, y, 1 - c), device_id_type=MESH).start()
        token[...] = jnp.zeros_like(token)

    res = pl.pallas_call(
        body, name=name,
        out_shape=[pltpu.SemaphoreType.DMA((4 * n,)), pltpu.SemaphoreType.DMA((4 * n,))]
        + [pltpu.HBM(a.shape, a.dtype) for a in list(arrs) + list(lands)] + [SDS((8, 128), F32)],
        in_specs=[HBM_SPEC] * (2 * n), out_specs=[SEM_SPEC, SEM_SPEC] + [HBM_SPEC] * (2 * n) + [pl.BlockSpec(memory_space=pltpu.VMEM)],
        input_output_aliases={i: 2 + i for i in range(2 * n)},
        compiler_params=pltpu.CompilerParams(has_side_effects=DATAFLOW_EFFECT),
    )(*arrs, *lands)
    return res[0], res[1], list(res[2:2 + n]), list(res[2 + n:2 + 2 * n]), res[-1]


def _ag_finish_wait(send_sems, recv_sems, arrs, lands, after, name, js=(0, 1, 2), own=True):
    n = len(arrs)

    def body(*refs):
        ins, bufs = refs[:n], refs[n:2 * n]
        send_r, recv_r = refs[2 * n:2 * n + 2]
        x, y, c = _position()
        for i in range(n):
            if own:
                mine = pltpu.make_async_remote_copy(src_ref=ins[i], dst_ref=bufs[i].at[2 * x + y], send_sem=send_r.at[4 * i + 3],
                                                    recv_sem=recv_r.at[4 * i + 3], device_id=(x, y, 1 - c), device_id_type=MESH)
                mine.wait_send()
                mine.wait_recv()
            for j, (px, py) in _relations(x, y, js):
                pltpu.make_async_remote_copy(src_ref=bufs[i].at[2 * px + py, c], dst_ref=bufs[i].at[2 * px + py, c],
                                             send_sem=send_r.at[4 * i + j], recv_sem=recv_r.at[4 * i + j],
                                             device_id=(x, y, 1 - c), device_id_type=MESH).wait_send()
                pltpu.make_async_remote_copy(src_ref=bufs[i].at[2 * px + py, 1 - c], dst_ref=bufs[i].at[2 * px + py, 1 - c],
                                             send_sem=send_r.at[4 * i + j], recv_sem=recv_r.at[4 * i + j],
                                             device_id=(x, y, 1 - c), device_id_type=MESH).wait_recv()

    res = pl.pallas_call(
        body, name=name, out_shape=[pltpu.HBM(a.shape, a.dtype) for a in list(arrs) + list(lands)],
        in_specs=[HBM_SPEC] * (2 * n) + [SEM_SPEC, SEM_SPEC, pl.BlockSpec(memory_space=pl.ANY)], out_specs=[HBM_SPEC] * (2 * n),
        input_output_aliases={i: i for i in range(2 * n)},
        compiler_params=pltpu.CompilerParams(has_side_effects=DATAFLOW_EFFECT),
    )(*arrs, *lands, send_sems, recv_sems, after)
    return list(res[:n]), list(res[n:])


def _ag_finish(arrs, lands, name, js=(0, 1, 2), own=True):
    send_sems, recv_sems, arrs, lands, token = _ag_finish_start(arrs, lands, name + "_start", js, own)
    return _ag_finish_wait(send_sems, recv_sems, arrs, lands, token, name + "_wait", js, own)


def _swap_start(gs, name):
    n = len(gs)

    def body(*refs):
        ins, lands = refs[:n], refs[n:2 * n]
        send_sems, recv_sems = refs[2 * n:2 * n + 2]
        token = refs[-1]
        x, y, c = _position()
        for i in range(n):
            for p in range(4):
                pltpu.make_async_remote_copy(src_ref=ins[i].at[p, 1 - c], dst_ref=lands[i].at[p], send_sem=send_sems.at[4 * i + p],
                                             recv_sem=recv_sems.at[4 * i + p], device_id=(x, y, 1 - c), device_id_type=MESH).start()
        token[...] = jnp.zeros_like(token)

    lands = [lax.empty((4,) + g.shape[2:], F32) for g in gs]
    res = pl.pallas_call(
        body, name=name,
        out_shape=[pltpu.SemaphoreType.DMA((4 * n,)), pltpu.SemaphoreType.DMA((4 * n,))]
        + [pltpu.HBM(a.shape, F32) for a in list(gs) + lands] + [SDS((8, 128), F32)],
        in_specs=[HBM_SPEC] * (2 * n), out_specs=[SEM_SPEC, SEM_SPEC] + [HBM_SPEC] * (2 * n) + [pl.BlockSpec(memory_space=pltpu.VMEM)],
        input_output_aliases={i: 2 + i for i in range(2 * n)},
        compiler_params=pltpu.CompilerParams(has_side_effects=DATAFLOW_EFFECT),
    )(*[pltpu.with_memory_space_constraint(a, pltpu.HBM) for a in list(gs) + lands])
    return res[0], res[1], list(res[2:2 + n]), list(res[2 + n:2 + 2 * n]), res[-1]


def _swap_wait(send_sems, recv_sems, gs, lands, after, name):
    n = len(gs)

    def body(*refs):
        ins, lands_r = refs[:n], refs[n:2 * n]
        send_r, recv_r = refs[2 * n:2 * n + 2]
        x, y, c = _position()
        for i in range(n):
            for p in range(4):
                cp = pltpu.make_async_remote_copy(src_ref=ins[i].at[p, 1 - c], dst_ref=lands_r[i].at[p], send_sem=send_r.at[4 * i + p],
                                                  recv_sem=recv_r.at[4 * i + p], device_id=(x, y, 1 - c), device_id_type=MESH)
                cp.wait_send()
                cp.wait_recv()

    res = pl.pallas_call(
        body, name=name, out_shape=[pltpu.HBM(a.shape, F32) for a in list(gs) + list(lands)],
        in_specs=[HBM_SPEC] * (2 * n) + [SEM_SPEC, SEM_SPEC, pl.BlockSpec(memory_space=pl.ANY)], out_specs=[HBM_SPEC] * (2 * n),
        input_output_aliases={i: i for i in range(2 * n)},
        compiler_params=pltpu.CompilerParams(has_side_effects=DATAFLOW_EFFECT),
    )(*gs, *lands, send_sems, recv_sems, after)
    return list(res[:n]), list(res[n:])


def _exchange_start(s1s, name):
    n = len(s1s)

    def body(*refs):
        srcs, lands = refs[:n], refs[n:2 * n]
        send_sems, recv_sems = refs[2 * n:2 * n + 2]
        token = refs[-1]
        x, y, c = _position()
        for i in range(n):
            for j, (px, py) in enumerate(_other_chips(x, y)):
                pltpu.make_async_remote_copy(src_ref=srcs[i].at[2 * px + py], dst_ref=lands[i].at[j], send_sem=send_sems.at[3 * i + j],
                                             recv_sem=recv_sems.at[3 * i + j], device_id=(px, py, c), device_id_type=MESH).start()
        token[...] = jnp.zeros_like(token)

    lands = [lax.empty((3,) + s.shape[1:], F32) for s in s1s]
    res = pl.pallas_call(
        body, name=name,
        out_shape=[pltpu.SemaphoreType.DMA((3 * n,)), pltpu.SemaphoreType.DMA((3 * n,))]
        + [pltpu.HBM(a.shape, F32) for a in list(s1s) + lands] + [SDS((8, 128), F32)],
        in_specs=[HBM_SPEC] * (2 * n), out_specs=[SEM_SPEC, SEM_SPEC] + [HBM_SPEC] * (2 * n) + [pl.BlockSpec(memory_space=pltpu.VMEM)],
        input_output_aliases={i: 2 + i for i in range(2 * n)},
        compiler_params=pltpu.CompilerParams(has_side_effects=DATAFLOW_EFFECT),
    )(*[pltpu.with_memory_space_constraint(a, pltpu.HBM) for a in list(s1s) + lands])
    return res[0], res[1], list(res[2:2 + n]), list(res[2 + n:2 + 2 * n]), res[-1]


def _exchange_wait(send_sems, recv_sems, s1s, lands, after, name):
    n = len(s1s)

    def body(*refs):
        srcs, lands_r = refs[:n], refs[n:2 * n]
        send_r, recv_r = refs[2 * n:2 * n + 2]
        x, y, c = _position()
        for i in range(n):
            for j, (px, py) in enumerate(_other_chips(x, y)):
                cp = pltpu.make_async_remote_copy(src_ref=srcs[i].at[2 * px + py], dst_ref=lands_r[i].at[j], send_sem=send_r.at[3 * i + j],
                                                  recv_sem=recv_r.at[3 * i + j], device_id=(px, py, c), device_id_type=MESH)
                cp.wait_send()
                cp.wait_recv()

    res = pl.pallas_call(
        body, name=name, out_shape=[pltpu.HBM(a.shape, F32) for a in list(s1s) + list(lands)],
        in_specs=[HBM_SPEC] * (2 * n) + [SEM_SPEC, SEM_SPEC, pl.BlockSpec(memory_space=pl.ANY)], out_specs=[HBM_SPEC] * (2 * n),
        input_output_aliases={i: i for i in range(2 * n)},
        compiler_params=pltpu.CompilerParams(has_side_effects=DATAFLOW_EFFECT),
    )(*s1s, *lands, send_sems, recv_sems, after)
    return list(res[:n]), list(res[n:])


def _join_halves(f2s, name):
    n = len(f2s)
    hbm = pl.BlockSpec(memory_space=pl.ANY)

    def body(*refs):
        ins, outs = refs[:n], refs[n:2 * n]
        send_sems, recv_sems = refs[2 * n:]
        x, y, c = _position()
        cps = []
        for i in range(n):
            cp = pltpu.make_async_remote_copy(src_ref=ins[i].at[c], dst_ref=outs[i].at[c], send_sem=send_sems.at[i],
                                              recv_sem=recv_sems.at[i], device_id=(x, y, 1 - c), device_id_type=MESH)
            cp.start()
            cps.append(cp)
        for i in range(n):
            pltpu.make_async_remote_copy(src_ref=ins[i].at[1 - c], dst_ref=outs[i].at[1 - c], send_sem=send_sems.at[i],
                                         recv_sem=recv_sems.at[i], device_id=(x, y, 1 - c), device_id_type=MESH).wait_recv()
        for cp in cps:
            cp.wait_send()

    return pl.pallas_call(
        body, name=name, in_specs=[hbm] * n, out_specs=[hbm] * n, out_shape=[SDS(f.shape, f.dtype) for f in f2s],
        input_output_aliases={i: i for i in range(n)},
        scratch_shapes=[pltpu.SemaphoreType.DMA((n,)), pltpu.SemaphoreType.DMA((n,))])(*f2s)


def _sum_tile(rows, cols):
    return _row_tile(rows, max(8, (2 ** 19 // cols) // 8 * 8))


def _add_pair(g, r1, c, name):
    _, R, C = r1.shape
    tr = _sum_tile(R, C)

    def body(c_ref, a_ref, b_ref, o_ref):
        del c_ref
        o_ref[...] = a_ref[...] + b_ref[...]

    blk = pl.BlockSpec((None, tr, C), lambda p, i, cr: (p, i, 0))
    return pl.pallas_call(
        body, name=name,
        grid_spec=pltpu.PrefetchScalarGridSpec(
            num_scalar_prefetch=1, grid=(4, R // tr),
            in_specs=[pl.BlockSpec((None, None, tr, C), lambda p, i, cr: (p, cr[0], i, 0)), blk], out_specs=blk),
        out_shape=SDS((4, R, C), F32))(c, g, r1)


def _add_four(s1, r2, me_c, name):
    _, R, C = s1.shape
    tr = _sum_tile(R, C)

    def body(m_ref, a_ref, b_ref, o_ref):
        del m_ref
        o_ref[...] = ((a_ref[...] + b_ref[0]) + b_ref[1]) + b_ref[2]

    return pl.pallas_call(
        body, name=name,
        grid_spec=pltpu.PrefetchScalarGridSpec(
            num_scalar_prefetch=1, grid=(R // tr,),
            in_specs=[pl.BlockSpec((None, tr, C), lambda i, mr: (mr[0], i, 0)), pl.BlockSpec((3, tr, C), lambda i, mr: (0, i, 0))],
            out_specs=pl.BlockSpec((None, tr, C), lambda i, mr: (mr[1], i, 0))),
        out_shape=SDS((2, R, C), F32))(me_c, s1, r2)


def _all_reduce_small(vs):
    n = len(vs)

    def body(*refs):
        ins, outs, bufs = refs[:n], refs[n:2 * n], refs[2 * n:3 * n]
        send_sems, recv_sems = refs[3 * n:]
        x, y, c = _position()
        me = 4 * x + 2 * y + c
        for i in range(n):
            bufs[i][me] = ins[i][...]
        cps = []
        for k in range(1, 8):
            to = (1 - x if k & 4 else x, 1 - y if k & 2 else y, 1 - c if k & 1 else c)
            for i in range(n):
                cp = pltpu.make_async_remote_copy(src_ref=bufs[i].at[me], dst_ref=bufs[i].at[me], send_sem=send_sems.at[7 * i + k - 1],
                                                  recv_sem=recv_sems.at[7 * i + k - 1], device_id=to, device_id_type=MESH)
                cp.start()
                cps.append(cp)
        for cp in cps:
            cp.wait_send()
        for k in range(1, 8):
            src = 4 * (1 - x if k & 4 else x) + 2 * (1 - y if k & 2 else y) + (1 - c if k & 1 else c)
            for i in range(n):
                pltpu.make_async_remote_copy(src_ref=bufs[i].at[src], dst_ref=bufs[i].at[src], send_sem=send_sems.at[7 * i + k - 1],
                                             recv_sem=recv_sems.at[7 * i + k - 1], device_id=(x, y, c), device_id_type=MESH).wait_recv()
        for i in range(n):
            acc = bufs[i][0]
            for k in range(1, 8):
                acc = acc + bufs[i][k]
            outs[i][...] = acc

    vm = pl.BlockSpec(memory_space=pltpu.VMEM)
    return pl.pallas_call(
        body, name="all_reduce_small", in_specs=[vm] * n, out_specs=[vm] * n, out_shape=[SDS(v.shape, F32) for v in vs],
        scratch_shapes=[pltpu.VMEM((8,) + v.shape, F32) for v in vs]
        + [pltpu.SemaphoreType.DMA((7 * n,)), pltpu.SemaphoreType.DMA((7 * n,))])(*vs)


def _reduce_begin(grads, tag):
    g4 = [g.reshape(4, 2, g.shape[1] // 2, g.shape[2]) for g in grads]
    send_sems, recv_sems, g4, lands, token = _swap_start(g4, "rs_swap_start_" + tag)
    return (send_sems, recv_sems, g4, lands), token


def _reduce_mid(state, after, tag):
    _, _, c = _position()
    cs = jnp.reshape(c, (1,)).astype(jnp.int32)
    send_sems, recv_sems, g4, lands = state
    g4, r1 = _swap_wait(send_sems, recv_sems, g4, lands, after, "rs_swap_wait_" + tag)
    s1 = [_add_pair(g, r, cs, f"rs_add_pair_{tag}{i}") for i, (g, r) in enumerate(zip(g4, r1))]
    send_sems, recv_sems, s1, lands, token = _exchange_start(s1, "rs_exchange_start_" + tag)
    return (send_sems, recv_sems, s1, lands), token


def _reduce_end(state, after, tag):
    x, y, c = _position()
    send_sems, recv_sems, s1, lands = state
    s1, lands = _exchange_wait(send_sems, recv_sems, s1, lands, after, "rs_exchange_wait_" + tag)
    me_c = jnp.stack([2 * x + y, c]).astype(jnp.int32)
    f2 = [_add_four(s, l, me_c, f"rs_add_four_{tag}{i}") for i, (s, l) in enumerate(zip(s1, lands))]
    return [f.reshape(2 * f.shape[1], f.shape[2]) for f in _join_halves(f2, "rs_join_" + tag)]


def _halves(a):
    return a.reshape((2, a.shape[0] // 2) + a.shape[1:])


def _after(a, token):
    return a + token[0, 0]


def _in_proj_own(x, nw, w, part):
    M, K = x.shape
    n = w.shape[2]
    sub = 256

    def body(p_ref, x_ref, nw_ref, w_ref, xn_ref, z_ref):
        del p_ref
        for r0 in range(0, MM_TM, sub):
            rows = pl.ds(r0, sub)
            for c0 in range(r0, r0 + sub, NORM_RC):
                ch = pl.ds(c0, NORM_RC)
                xv = x_ref[ch, :]
                xn_ref[ch, :] = (xv * lax.rsqrt(jnp.mean(xv * xv, axis=-1, keepdims=True) + RMS_EPS) * nw_ref[...]).astype(BF16)
            z_ref[rows, :] = jnp.dot(xn_ref[rows, :], w_ref[...], preferred_element_type=F32).astype(BF16)

    row = pl.BlockSpec((MM_TM, K), lambda i, pr: (i, 0))
    return pl.pallas_call(
        body, name="in_proj_own",
        grid_spec=pltpu.PrefetchScalarGridSpec(
            num_scalar_prefetch=1, grid=(M // MM_TM,),
            in_specs=[row, pl.BlockSpec((1, K), lambda i, pr: (0, 0)), pl.BlockSpec((None, K, n), lambda i, pr: (0, 0, 0))],
            out_specs=[row, pl.BlockSpec((MM_TM, n), lambda i, pr: (i, pr[0]))]),
        out_shape=[SDS((M, K), BF16), SDS((M, 4 * n), BF16)])(part, x, nw, w)


def _in_proj_parts(xn, w, parts, z, name):
    M, K = xn.shape
    _, _, n = w.shape
    P = parts.shape[0]

    def body(p_ref, a_ref, b_ref, z_in, o_ref):
        del p_ref, z_in
        o_ref[...] = jnp.dot(a_ref[...], b_ref[...], preferred_element_type=F32).astype(BF16)

    return pl.pallas_call(
        body, name=name,
        grid_spec=pltpu.PrefetchScalarGridSpec(
            num_scalar_prefetch=1, grid=(P, M // MM_TM),
            in_specs=[pl.BlockSpec((MM_TM, K), lambda g, i, pr: (i, 0)), pl.BlockSpec((None, K, n), lambda g, i, pr: (pr[g], 0, 0)),
                      pl.BlockSpec(memory_space=pl.ANY)],
            out_specs=pl.BlockSpec((MM_TM, n), lambda g, i, pr: (i, pr[g]))),
        out_shape=SDS(z.shape, BF16), input_output_aliases={3: 0})(parts, xn, w, z)


def _local_step(x, mem, target, sp, ex):
    S, D = x.shape
    band, buckets = _bias_static()
    buckets = jnp.asarray(buckets)

    tok = ex.start_first()
    own, me, near, far = ex.own_w_in()
    xn, z = _in_proj_own(x, _after(sp["attn_norm_w"], tok), own, me)
    z = _in_proj_parts(xn, ex.near_weights(after=z), near, z, "in_proj_near")
    w_in, gathered_small = ex.first_weights(after=z)
    sp = {**sp, **gathered_small}
    n_in = 4 * w_in.shape[2]
    bw = {"w_in": w_in}
    z = _in_proj_parts(xn, w_in, far + ex.start_rest()[0, 0].astype(jnp.int32), z, "in_proj_far")
    qh, kh, vh, q2, k2, v2 = _qkv_prep(z, sp["q_norm_w"], sp["k_norm_w"])
    tab = sp["rel_bias_table"].T.reshape(N_GROUPS, HPG, N_BUCKETS)
    bias = _bias_fwd(jnp.pad(tab, ((0, 0), (0, 8 - HPG), (0, 0))), buckets)
    biasm = jnp.where(jnp.asarray(band)[None, None], bias[:, :HPG].reshape(N_GROUPS, HPG, NQ, 2 * NQ), NEG)
    seqs = lambda a: a.reshape(-1, S // P4, HEAD)
    q12, k12, v12 = seqs(q2), seqs(k2), seqs(v2)
    groups = ((qh, kh, vh, 1, 0, HPG, 0), (q12, k12, v12, 1, 0, HPG * P4, 2), (q12, k12, v12, 4, HPG * P4, HPG * P4, 2))
    os_, lses = [], []
    for g, (qg, kg, vg, dil, head0, nseq, shift) in enumerate(groups):
        o_g, lse_g = _attn_fwd(qg, kg, vg, biasm[g], f"attn_fwd_g{g}", dil, head0, nseq, shift)
        os_.append(o_g.reshape(HPG, -1, o_g.shape[1], HEAD) if g > 0 else o_g)
        lses.append(lse_g.reshape(HPG, -1, lse_g.shape[1], HEAD) if g > 0 else lse_g)
    attn = _merge_fwd(os_, lses)
    tok = ex.rest_arrived(after=attn)
    u1, u3 = _conv_fwd(z, sp["conv_dw_w"], sp["conv_dw_b"], _after(sp["conv_ln_w"], tok), sp["conv_ln_b"])
    bw.update(ex.rest_weights(after=u3))
    F2 = 4 * bw["w_up"].shape[2]
    mn, kv, mk, mv = _memkv_fwd(mem, sp["mem_norm_w"], bw["w_mem_kv"], sp["xk_norm_w"])
    oc = _cross_fwd(z, sp["xq_norm_w"], mk, mv)
    h1, hn = _outproj_fwd(x, z, sp["b_gate"], attn, u3, oc, bw["w_attn_o"], bw["w_conv_o"], bw["w_cross_o"], bw["w_out"],
                          sp["ffn_norm_w"])
    up = _mm_nn(hn, bw["w_up"], BF16, "ffn_up")
    dy, loss_tile, ac, gc = _ffn_fwd(up, sp["ffn_conv_w"], sp["ffn_conv_b"], bw["w_down"], h1, target)

    gs, gb = {}, {}
    dca, dcg, acca, accg, gb["w_down"] = _ffn_bwd_a(dy, bw["w_down"], ac, gc)
    cols = lambda acc: jnp.sum(acc, axis=1).reshape(1, F2 // 2)
    gs["ffn_conv_b"] = jnp.concatenate([cols(acca), cols(accg)], axis=1)
    dup, tacc = _ffn_bwd_b(dca, dcg, up, sp["ffn_conv_w"])
    gs["ffn_conv_w"] = jnp.transpose(jnp.sum(tacc, axis=2), (1, 0, 2)).reshape(FFN_K, F2)
    gb["w_up"] = _mm_tn(hn, dup, 4, "dw_up")
    tok = ex.reduce_begin("a", ("w_down", "w_up"), gb)
    dh1, gs["ffn_norm_w"] = _norm_in_bwd(dup, bw["w_up"], h1, _after(sp["ffn_norm_w"], tok), dy, "ffn_in_bwd")
    tok = ex.reduce_mid("a", after=gs["ffn_norm_w"])
    dz, dattn, du3, doc, gs["b_gate"], gb["w_out"], gb["w_attn_o"], gb["w_conv_o"], gb["w_cross_o"] = _outproj_bwd(
        dh1, z, _after(sp["b_gate"], tok), attn, u3, oc, bw["w_attn_o"], bw["w_conv_o"], bw["w_cross_o"], bw["w_out"], n_in)
    dz, dmk, dmv, gs["xq_norm_w"] = _cross_bwd(dz, doc, z, sp["xq_norm_w"], mk, mv)
    gb["w_mem_kv"], gs["xk_norm_w"], gs["mem_norm_w"] = _memkv_bwd(
        dmk, dmv, kv, mem, mn, sp["mem_norm_w"], bw["w_mem_kv"], sp["xk_norm_w"])
    ex.reduce_end("a", after=gs["mem_norm_w"])
    tok = ex.reduce_begin("b", ("w_out", "w_attn_o", "w_conv_o", "w_cross_o", "w_mem_kv"), gb)
    du1, cacc = _conv_bwd_a(du3, u1, z, _after(sp["conv_ln_w"], tok), sp["conv_ln_b"])
    gs["conv_dw_w"], gs["conv_dw_b"] = cacc[:CONV_K], cacc[32:33]
    gs["conv_ln_w"], gs["conv_ln_b"] = cacc[33:34], cacc[34:35]
    tok = ex.reduce_mid("b", after=cacc)
    dz = _conv_bwd_b(dz, du1, z, _after(sp["conv_dw_w"], tok))
    merged_grads = _merge_bwd(dattn, os_, lses)
    dqs, dks, dvs, dsbs = [], [], [], []
    for g, (qg, kg, vg, dil, head0, nseq, shift) in enumerate(groups):
        wg_g, da_g, dh_g = merged_grads[g]
        lse_g = lses[g]
        if g > 0:
            wg_g, da_g, dh_g, lse_g = seqs(wg_g), seqs(da_g), seqs(dh_g), seqs(lse_g)
        dq_g, dk_g, dv_g, dsb_g = _attn_bwd(qg, kg, vg, biasm[g], da_g, wg_g, dh_g, lse_g, f"attn_bwd_g{g}", dil, head0, nseq, shift)
        if g > 0:
            dq_g, dk_g, dv_g = (t.reshape(HPG, P4, S // P4, HEAD) for t in (dq_g, dk_g, dv_g))
            dsb_g = jnp.sum(dsb_g.reshape(HPG, P4, NQ, 2 * NQ), axis=1)
        dqs.append(dq_g)
        dks.append(dk_g)
        dvs.append(dv_g)
        dsbs.append(dsb_g.reshape(HPG, NQ * 2 * NQ))
    dtab = _bias_bwd(jnp.pad(jnp.stack(dsbs), ((0, 0), (0, 8 - HPG), (0, 0))), buckets)
    gs["rel_bias_table"] = dtab[:, :HPG].reshape(N_GROUPS * HPG, N_BUCKETS).T
    dz, gs["q_norm_w"], gs["k_norm_w"] = _qkv_bwd(dz, z, dqs, dks, dvs, sp["q_norm_w"], sp["k_norm_w"])
    ex.reduce_end("b", after=gs["q_norm_w"])
    gb["w_in"] = _mm_tn(xn, dz, 4, "dw_in")
    tok = ex.reduce_mid("c", after=ex.reduce_begin("c", ("w_in",), gb))
    dx, gs["attn_norm_w"] = _norm_in_bwd(dz, bw["w_in"], x, _after(sp["attn_norm_w"], tok), dh1, "in_bwd")
    ex.reduce_end("c", after=gs["attn_norm_w"])
    return loss_tile, dx, gs, gb


SMALL = ("rel_bias_table", "attn_norm_w", "b_gate", "q_norm_w", "k_norm_w", "conv_dw_w", "conv_dw_b", "conv_ln_w", "conv_ln_b",
         "mem_norm_w", "xq_norm_w", "xk_norm_w", "ffn_norm_w", "ffn_conv_w", "ffn_conv_b")
SMALL_SHARDED = ("conv_dw_w", "ffn_conv_w")
BIG_COL = ("w_in", "w_attn_o", "w_conv_o", "w_cross_o", "w_up")
BIG_ROW = ("w_mem_kv", "w_out", "w_down")
BIG = BIG_COL + BIG_ROW
WEIGHTS = ("rel_bias_table", "attn_norm_w", "w_in", "b_gate", "q_norm_w", "k_norm_w", "w_attn_o", "conv_dw_w", "conv_dw_b",
           "conv_ln_w", "conv_ln_b", "w_conv_o", "mem_norm_w", "w_mem_kv", "xq_norm_w", "xk_norm_w", "w_cross_o", "w_out",
           "ffn_norm_w", "w_up", "ffn_conv_w", "ffn_conv_b", "w_down")


class _Exchanges:
    REST = tuple(k for k in BIG if k != "w_in")

    def __init__(self, w):
        self.w = w
        self.pending = {}
        self.reduced = {}

    def _whole(self, k, ga):
        ga = ga.reshape((4,) + self.w[k].shape)
        return ga if k in BIG_COL else ga.reshape((4 * self.w[k].shape[0],) + self.w[k].shape[1:])

    def start_first(self):
        self.w_in_local = self.w["w_in"].astype(BF16)
        local = [_halves(self.w_in_local)]
        for k in SMALL_SHARDED:
            flat = jnp.ravel(self.w[k])
            local.append(jnp.pad(flat, (0, (-flat.shape[0]) % 2048)).reshape(2, -1, 128))
        near = _ag_ici_start(local, "gather_near_start", js=(0, 1))
        far = _ag_ici_start(near[2], "gather_far_start", js=(2,), lands=near[3])
        self.pending["first"] = (near[:2], far[:2], far[2], far[3])
        return far[4]

    def own_w_in(self):
        x, y, _ = _position()
        me = 2 * x + y
        ids = lambda *v: jnp.stack(v).astype(jnp.int32)
        return self.w_in_local[None], ids(me), ids(me ^ 2, me ^ 1), ids(me ^ 3)

    def near_weights(self, after):
        near, far, ins, lands = self.pending.pop("first")
        ins, lands = _ag_ici_wait(*near, ins, lands, after, "gather_near_wait", js=(0, 1))
        ins, lands = _ag_finish(ins, lands, "gather_near_finish", js=(0, 1))
        self.pending["first"] = (far, ins, lands)
        return self._whole("w_in", lands[0])

    def first_weights(self, after):
        far, ins, lands = self.pending.pop("first")
        ins, lands = _ag_ici_wait(*far, ins, lands, after, "gather_far_wait", js=(2,))
        _, gathered = _ag_finish(ins, lands, "gather_far_finish", js=(2,), own=False)
        self.first = gathered[0]
        small = {}
        for k, ga in zip(SMALL_SHARDED, gathered[1:]):
            r, cdim = self.w[k].shape
            parts = ga.reshape(4, -1)[:, :r * cdim].reshape(4, r, cdim)
            small[k] = jnp.transpose(parts, (1, 0, 2)).reshape(r, 4 * cdim)
        return self._whole("w_in", gathered[0]), small

    def start_rest(self):
        local = [_halves(self.w[k].astype(BF16)) for k in self.REST]
        local, _ = lax.optimization_barrier((local, self.first))
        send_sems, recv_sems, ins, lands, token = _ag_ici_start(local, "gather_rest_start")
        self.pending["rest"] = (send_sems, recv_sems, ins, lands)
        return token

    def rest_arrived(self, after):
        send_sems, recv_sems, ins, lands = self.pending.pop("rest")
        ins, lands = _ag_ici_wait(send_sems, recv_sems, ins, lands, after, "gather_rest_wait")
        send_sems, recv_sems, ins, lands, token = _ag_finish_start(ins, lands, "gather_rest_finish_start")
        self.pending["rest"] = (send_sems, recv_sems, ins, lands)
        return token

    def rest_weights(self, after):
        _, gathered = _ag_finish_wait(*self.pending.pop("rest"), after, "gather_rest_finish_wait")
        return {k: self._whole(k, ga) for k, ga in zip(self.REST, gathered)}

    def reduce_begin(self, tag, names, gb):
        parts = [gb[k].reshape((4,) + self.w[k].shape) for k in names]
        state, token = _reduce_begin(parts, tag)
        self.pending[tag] = (state, names)
        return token

    def reduce_mid(self, tag, after):
        state, names = self.pending.pop(tag)
        state, token = _reduce_mid(state, after, tag)
        self.pending[tag] = (state, names)
        return token

    def reduce_end(self, tag, after):
        state, names = self.pending.pop(tag)
        self.reduced.update(zip(names, _reduce_end(state, after, tag)))


def _step(x, mem, target, w, m, v):
    xi, yi, _ = _position()
    shard = 2 * xi + yi
    ex = _Exchanges(w)
    sp = {k: w[k] for k in SMALL if k not in SMALL_SHARDED}
    loss_tile, dx, gs, _ = _local_step(x, mem, target, sp, ex)
    g_big = ex.reduced

    red = _all_reduce_small([loss_tile] + [gs[k] for k in SMALL])
    loss = red[0][0, 0]
    g_small = dict(zip(SMALL, red[1:]))
    for k in SMALL_SHARDED:
        cdim = w[k].shape[1]
        g_small[k] = lax.dynamic_slice_in_dim(g_small[k], shard * cdim, cdim, axis=1)

    grads, delta, new_m, new_v = {}, {}, {}, {}
    for k in BIG:
        grads[k] = g_big[k]
        delta[k], new_m[k], new_v[k] = _adamw(w[k], g_big[k], m[k], v[k], "adamw_" + k)
    outs = _adamw_small([w[k] for k in SMALL], [g_small[k] for k in SMALL], [m[k] for k in SMALL], [v[k] for k in SMALL])
    for dst, vals in zip((delta, new_m, new_v), outs):
        dst.update(zip(SMALL, vals))
    grads.update(g_small)
    return loss, dx, grads, delta, new_m, new_v


def kernel(x, mem, rel_bias_table, attn_norm_w, w_in, b_gate, q_norm_w, k_norm_w, w_attn_o, conv_dw_w, conv_dw_b, conv_ln_w, conv_ln_b, w_conv_o, mem_norm_w, w_mem_kv, xq_norm_w, xk_norm_w, w_cross_o, w_out, ffn_norm_w, w_up, ffn_conv_w, ffn_conv_b, w_down, loss_target, m_rel_bias_table, m_attn_norm_w, m_w_in, m_b_gate, m_q_norm_w, m_k_norm_w, m_w_attn_o, m_conv_dw_w, m_conv_dw_b, m_conv_ln_w, m_conv_ln_b, m_w_conv_o, m_mem_norm_w, m_w_mem_kv, m_xq_norm_w, m_xk_norm_w, m_w_cross_o, m_w_out, m_ffn_norm_w, m_w_up, m_ffn_conv_w, m_ffn_conv_b, m_w_down, v_rel_bias_table, v_attn_norm_w, v_w_in, v_b_gate, v_q_norm_w, v_k_norm_w, v_w_attn_o, v_conv_dw_w, v_conv_dw_b, v_conv_ln_w, v_conv_ln_b, v_w_conv_o, v_mem_norm_w, v_w_mem_kv, v_xq_norm_w, v_xk_norm_w, v_w_cross_o, v_w_out, v_ffn_norm_w, v_w_up, v_ffn_conv_w, v_ffn_conv_b, v_w_down):
    args = locals()
    def block(name, k):
        a = args[name] if k == "rel_bias_table" else args[name][0]
        return a.reshape(1, -1) if a.ndim == 1 else a

    w = {k: block(k, k) for k in WEIGHTS}
    m = {k: block("m_" + k, k) for k in WEIGHTS}
    v = {k: block("v_" + k, k) for k in WEIGHTS}
    loss, dx, grads, delta, new_m, new_v = _step(x[0], mem[0], loss_target[0], w, m, v)
    out = [loss, dx[None]]
    for d in (grads, delta, new_m, new_v):
        for k in WEIGHTS:
            out.append(d[k].reshape(args[k].shape))
    return tuple(out)
```

```python
import functools
import math

import numpy as np
import jax
import jax.numpy as jnp
from jax import lax
from jax.experimental import pallas as pl
from jax.experimental.pallas import tpu as pltpu

F32, BF16 = jnp.float32, jnp.bfloat16
SDS = jax.ShapeDtypeStruct
MESH = pl.DeviceIdType.MESH

HEAD = 128
N_GROUPS, HPG = 3, 4
ATTN_GROUPS = ((128, 1), (512, 4), (2048, 16))
NQ = 128
QKV_W = N_GROUPS * HPG * HEAD
CH = 512
CONV_K, FFN_K = 31, 3
N_BUCKETS, MAX_DIST = 32, 2048
RMS_EPS, LN_EPS = 1e-6, 1e-5
O_Q, O_K, O_V, O_CV, O_CG, O_XQ, O_G = 0, QKV_W, 2 * QKV_W, 3 * QKV_W, 3 * QKV_W + CH, 3 * QKV_W + 2 * CH, 3 * QKV_W + 3 * CH
ADAM_LR, ADAM_B1, ADAM_B2, ADAM_EPS, ADAM_WD, ADAM_STEP = 0.001, 0.9, 0.999, 1e-08, 0.01, 10
NEG = -1e30
SCALE = HEAD ** -0.5
TM = 512
MM_TM = 1024
ATT_RB = 2048
NT_DIMS = (((1,), (1,)), ((), ()))
TN_DIMS = (((0,), (0,)), ((), ()))


CONV_RC = 32
CONV_LB = 256
CONV_LANES = tuple(slice(l, l + CONV_LB) for l in range(0, CH, CONV_LB))
CONV_HALO = 32
FFN_RC = 32


def _conv_taps(buf, base, lanes, q_lo, q_hi, visit):
    span = CONV_RC + CONV_HALO
    xx = buf[pl.ds(base, span), lanes]
    for s in range(8):
        xs = xx if s == 0 else pltpu.roll(xx, span - s, 0)
        for q in range(s, q_hi + 1, 8):
            if q >= q_lo:
                visit(q, xs[q - s:q - s + CONV_RC])


def _sig(v):
    return 0.5 * jnp.tanh(0.5 * v) + 0.5


def _fold8(v):
    acc = v[0:8]
    for r in range(8, v.shape[0], 8):
        acc = acc + v[r:r + 8]
    return acc


def _row_tile(rows, cap, mult=8):
    best = None
    for t in range(mult, min(rows, cap) + 1, mult):
        if rows % t == 0:
            best = t
    return best if best is not None else rows


def _full(shape):
    n = len(shape)
    return pl.BlockSpec(shape, lambda *a: (0,) * n)


def _mm_nn(a, b, out_dtype, name):
    M, K = a.shape
    G, _, n = b.shape

    def body(a_ref, b_ref, o_ref):
        o_ref[...] = jnp.dot(a_ref[...].astype(BF16), b_ref[...], preferred_element_type=F32).astype(out_dtype)

    return pl.pallas_call(
        body, name=name, grid=(G, M // MM_TM),
        in_specs=[pl.BlockSpec((MM_TM, K), lambda g, i: (i, 0)), pl.BlockSpec((None, K, n), lambda g, i: (g, 0, 0))],
        out_specs=pl.BlockSpec((MM_TM, n), lambda g, i: (i, g)),
        out_shape=SDS((M, G * n), out_dtype))(a, b)


def _mm_tn(a, b, G, name):
    S, Ka = a.shape
    n = b.shape[1] // G
    tka = Ka
    while tka * n * 4 > 10 * 2 ** 20 and tka % 256 == 0:
        tka //= 2

    def body(a_ref, b_ref, o_ref):
        @pl.when(pl.program_id(2) == 0)
        def _():
            o_ref[...] = jnp.zeros_like(o_ref)
        o_ref[...] += lax.dot_general(a_ref[...].astype(BF16), b_ref[...].astype(BF16), TN_DIMS, preferred_element_type=F32)

    return pl.pallas_call(
        body, name=name, grid=(G, Ka // tka, S // MM_TM),
        in_specs=[pl.BlockSpec((MM_TM, tka), lambda g, i, k: (k, i)), pl.BlockSpec((MM_TM, n), lambda g, i, k: (k, g))],
        out_specs=pl.BlockSpec((None, tka, n), lambda g, i, k: (g, i, 0)),
        out_shape=SDS((G, Ka, n), F32))(a, b)


NORM_RC = 16


def _norm_in_bwd(a, w, xin, nw, resid, name):
    S, K = xin.shape
    G, _, n = w.shape
    tm = 1024 if S % 1024 == 0 and G * K * n * 2 <= 12 * 2 ** 20 else TM
    nt = S // tm

    def body(a_ref, w_ref, x_ref, nw_ref, r_ref, o_ref, dnw_ref, acc, part):
        i, g = pl.program_id(0), pl.program_id(1)

        @pl.when((i == 0) & (g == 0))
        def _():
            part[...] = jnp.zeros_like(part)

        @pl.when(g == 0)
        def _():
            acc[...] = jnp.zeros_like(acc)

        acc[...] += lax.dot_general(a_ref[...], w_ref[g], NT_DIMS, preferred_element_type=F32)

        @pl.when(g == G - 1)
        def _():
            for r0 in range(0, tm, NORM_RC):
                rows = pl.ds(r0, NORM_RC)
                dn = acc[rows, :]
                xv = x_ref[rows, :]
                r = lax.rsqrt(jnp.mean(xv * xv, axis=-1, keepdims=True) + RMS_EPS)
                xhat = xv * r
                dyw = dn * nw_ref[...]
                o_ref[rows, :] = r_ref[rows, :] + r * (dyw - xhat * jnp.mean(dyw * xhat, axis=-1, keepdims=True))
                part[...] += _fold8(dn * xhat)

        @pl.when((i == nt - 1) & (g == G - 1))
        def _():
            dnw_ref[...] = jnp.sum(part[...], axis=0, keepdims=True)

    row = pl.BlockSpec((tm, K), lambda i, g: (i, 0))
    return pl.pallas_call(
        body, name=name, grid=(nt, G),
        in_specs=[pl.BlockSpec((tm, n), lambda i, g: (i, g)),
                  pl.BlockSpec((G, K, n), lambda i, g: (0, 0, 0), pipeline_mode=pl.Buffered(1)),
                  row, _full((1, K)), row],
        out_specs=[row, _full((1, K))],
        out_shape=[SDS((S, K), F32), SDS((1, K), F32)],
        scratch_shapes=[pltpu.VMEM((tm, K), F32), pltpu.VMEM((8, K), F32)])(a, w, xin, nw, resid)


def _t5_bucket_np(dist):
    max_exact = N_BUCKETS // 2
    d = np.maximum(dist.astype(np.float32), np.float32(1.0))
    large = max_exact + (np.log(d / np.float32(max_exact)) / np.float32(math.log(MAX_DIST / max_exact))
                         * np.float32(N_BUCKETS - max_exact)).astype(np.int32)
    large = np.minimum(large, N_BUCKETS - 1)
    return np.where(dist < max_exact, dist, large).astype(np.int32)


def _bias_static():
    qi = np.arange(NQ)[:, None]
    kj = np.arange(2 * NQ)[None, :]
    step = qi + NQ - kj
    band = (step >= 0) & (step <= NQ)
    buckets = np.stack([_t5_bucket_np(np.clip(step, 0, None) * dil).reshape(1, -1) for _, dil in ATTN_GROUPS])
    return band, buckets


def _bias_fwd(table_t, buckets):
    nb = buckets.shape[-1]

    def body(t_ref, b_ref, o_ref):
        oh = (b_ref[...] == lax.broadcasted_iota(jnp.int32, (N_BUCKETS, nb), 0)).astype(F32)
        o_ref[...] = jnp.dot(t_ref[...], oh, preferred_element_type=F32, precision=lax.Precision.HIGHEST)

    return pl.pallas_call(
        body, name="bias_fwd", grid=(N_GROUPS,),
        in_specs=[pl.BlockSpec((None, 8, N_BUCKETS), lambda g: (g, 0, 0)), pl.BlockSpec((None, 1, nb), lambda g: (g, 0, 0))],
        out_specs=pl.BlockSpec((None, 8, nb), lambda g: (g, 0, 0)),
        out_shape=SDS((N_GROUPS, 8, nb), F32))(table_t, buckets)


def _bias_bwd(dsb, buckets):
    nb = buckets.shape[-1]

    def body(d_ref, b_ref, o_ref):
        oh = (b_ref[...] == lax.broadcasted_iota(jnp.int32, (N_BUCKETS, nb), 0)).astype(F32)
        o_ref[...] = lax.dot_general(d_ref[...], oh, NT_DIMS, preferred_element_type=F32, precision=lax.Precision.HIGHEST)

    return pl.pallas_call(
        body, name="bias_bwd", grid=(N_GROUPS,),
        in_specs=[pl.BlockSpec((None, 8, nb), lambda g: (g, 0, 0)), pl.BlockSpec((None, 1, nb), lambda g: (g, 0, 0))],
        out_specs=pl.BlockSpec((None, 8, N_BUCKETS), lambda g: (g, 0, 0)),
        out_shape=SDS((N_GROUPS, 8, N_BUCKETS), F32))(dsb, buckets)


P4 = 4


def _to_p4(dst_ref, h, val, scr):
    scr[...] = val
    for r in range(P4):
        dst_ref[h, r] = scr[pl.ds(r, TM // P4, stride=P4), :]


def _from_p4(src_ref, h, scr):
    for r in range(P4):
        scr[pl.ds(r, TM // P4, stride=P4), :] = src_ref[h, r]
    return scr[...]


def _qkv_prep(z, qw, kw):
    S = z.shape[0]
    nh = N_GROUPS * HPG

    def body(zq, zk, zv, qw_ref, kw_ref, qh, kh, vh, q2, k2, v2, scr):
        for h in range(nh):
            g = h // HPG
            sl = slice(h * HEAD, (h + 1) * HEAD)
            xq = zq[:, sl].astype(F32)
            q = xq * lax.rsqrt(jnp.mean(xq * xq, axis=-1, keepdims=True) + RMS_EPS) * qw_ref[g:g + 1, :]
            xk = zk[:, sl].astype(F32)
            k = xk * lax.rsqrt(jnp.mean(xk * xk, axis=-1, keepdims=True) + RMS_EPS) * kw_ref[g:g + 1, :]
            v = zv[:, sl].astype(F32)
            if h < HPG:
                qh[h], kh[h], vh[h] = q, k, v
            else:
                _to_p4(q2, h - HPG, q, scr)
                _to_p4(k2, h - HPG, k, scr)
                _to_p4(v2, h - HPG, v, scr)

    hm = pl.BlockSpec((HPG, TM, HEAD), lambda i: (0, i, 0))
    p4 = pl.BlockSpec((2 * HPG, P4, TM // P4, HEAD), lambda i: (0, 0, i, 0))
    return pl.pallas_call(
        body, name="qkv_prep", grid=(S // TM,),
        in_specs=[pl.BlockSpec((TM, QKV_W), lambda i: (i, 0)), pl.BlockSpec((TM, QKV_W), lambda i: (i, 1)),
                  pl.BlockSpec((TM, QKV_W), lambda i: (i, 2)), _full((N_GROUPS, HEAD)), _full((N_GROUPS, HEAD))],
        out_specs=[hm, hm, hm, p4, p4, p4],
        out_shape=[SDS((HPG, S, HEAD), F32)] * 3 + [SDS((2 * HPG, P4, S // P4, HEAD), F32)] * 3,
        scratch_shapes=[pltpu.VMEM((TM, HEAD), F32)])(z, z, z, qw, kw)


def _rows(start, d):
    return pl.ds(start, NQ) if d == 1 else pl.ds(start, NQ, stride=d)


def _attn_fwd(qh, kh, vh, biasm, name, d, head0, nseq, bias_shift):
    S = qh.shape[1]
    RB = min(ATT_RB, S)
    nbk, nq = S // RB, RB // (NQ * d)

    def body(q_ref, k_ref, v_ref, bias_ref, o_ref, lse_ref, kbuf, vbuf):
        b = pl.program_id(1)

        @pl.when(b == 0)
        def _():
            kbuf[0:RB, :] = jnp.zeros((RB, HEAD), F32)
            vbuf[0:RB, :] = jnp.zeros((RB, HEAD), F32)

        @pl.when(b > 0)
        def _():
            kbuf[0:RB, :] = kbuf[RB:2 * RB, :]
            vbuf[0:RB, :] = vbuf[RB:2 * RB, :]

        kbuf[RB:2 * RB, :] = k_ref[...]
        vbuf[RB:2 * RB, :] = v_ref[...]
        bias = bias_ref[...]
        col = lax.broadcasted_iota(jnp.int32, (NQ, 2 * NQ), 1)

        for qb in range(nq):
            def unit(r, carry, qb=qb):
                qs = qb * NQ * d + r
                q = q_ref[_rows(qs, d), :].astype(BF16)
                kw = jnp.concatenate([kbuf[_rows(RB + qs - NQ * d, d), :], kbuf[_rows(RB + qs, d), :]], axis=0).astype(BF16)
                vw = jnp.concatenate([vbuf[_rows(RB + qs - NQ * d, d), :], vbuf[_rows(RB + qs, d), :]], axis=0).astype(BF16)
                s = lax.dot_general(q, kw, NT_DIMS, preferred_element_type=F32) * SCALE + bias
                if qb == 0:
                    s = jnp.where((col < NQ) & (b == 0), NEG, s)
                m = jnp.max(s, axis=-1, keepdims=True)
                p = jnp.exp(s - m)
                l = jnp.sum(p, axis=-1, keepdims=True)
                o = jnp.dot(p.astype(BF16), vw, preferred_element_type=F32) / l
                o_ref[_rows(qs, d), :] = o
                lse_ref[_rows(qs, d), :] = jnp.broadcast_to(m + jnp.log(l), (NQ, HEAD))
                return carry

            for r in range(d):
                unit(r, 0)

    blk = lambda f: pl.BlockSpec((None, RB, HEAD), f)
    return pl.pallas_call(
        body, name=name, grid=(nseq, nbk),
        in_specs=[blk(lambda h, b: (head0 + h, b, 0))] * 3
        + [pl.BlockSpec((None, NQ, 2 * NQ), lambda h, b: (jnp.right_shift(h, bias_shift), 0, 0))],
        out_specs=[blk(lambda h, b: (h, b, 0))] * 2,
        out_shape=[SDS((nseq, S, HEAD), F32)] * 2,
        scratch_shapes=[pltpu.VMEM((2 * RB, HEAD), F32)] * 2)(qh, kh, vh, biasm)


def _attn_bwd(qh, kh, vh, biasm, da, wg, dh, lse, name, d, head0, nseq, bias_shift):
    S = qh.shape[1]
    RB = min(ATT_RB, S)
    nbk, nq = S // RB, RB // (NQ * d)

    def body(q_ref, k_ref, v_ref, bias_ref, da_ref, wg_ref, dh_ref, lse_ref,
             dq_ref, dk_ref, dv_ref, dsb_ref, kbuf, vbuf, dkbuf, dvbuf):
        b = pl.program_id(1)
        zero = jnp.zeros((RB, HEAD), F32)

        @pl.when(b == 0)
        def _():
            kbuf[0:RB, :] = zero
            vbuf[0:RB, :] = zero
            dkbuf[0:RB, :] = zero
            dvbuf[0:RB, :] = zero
            dsb_ref[...] = jnp.zeros_like(dsb_ref)

        @pl.when(b > 0)
        def _():
            kbuf[0:RB, :] = kbuf[RB:2 * RB, :]
            vbuf[0:RB, :] = vbuf[RB:2 * RB, :]
            dkbuf[0:RB, :] = dkbuf[RB:2 * RB, :]
            dvbuf[0:RB, :] = dvbuf[RB:2 * RB, :]

        dkbuf[RB:2 * RB, :] = zero
        dvbuf[RB:2 * RB, :] = zero

        @pl.when(b < nbk)
        def _():
            kbuf[RB:2 * RB, :] = k_ref[...]
            vbuf[RB:2 * RB, :] = v_ref[...]
            bias = bias_ref[...]
            col = lax.broadcasted_iota(jnp.int32, (NQ, 2 * NQ), 1)

            for qb in range(nq):
                def unit(r, carry, qb=qb):
                    qs = qb * NQ * d + r
                    prev, cur = _rows(RB + qs - NQ * d, d), _rows(RB + qs, d)
                    q = q_ref[_rows(qs, d), :].astype(BF16)
                    kw = jnp.concatenate([kbuf[prev, :], kbuf[cur, :]], axis=0).astype(BF16)
                    vw = jnp.concatenate([vbuf[prev, :], vbuf[cur, :]], axis=0).astype(BF16)
                    s = lax.dot_general(q, kw, NT_DIMS, preferred_element_type=F32) * SCALE + bias
                    if qb == 0:
                        s = jnp.where((col < NQ) & (b == 0), NEG, s)
                    p = jnp.exp(s - lse_ref[_rows(qs, d), :][:, 0:1])
                    w = wg_ref[_rows(qs, d), :]
                    do = (da_ref[_rows(qs, d), :] * w).astype(BF16)
                    dp = lax.dot_general(do, vw, NT_DIMS, preferred_element_type=F32)
                    ds = p * (dp - w[:, 0:1] * dh_ref[_rows(qs, d), :][:, 0:1])
                    dsb_ref[...] += ds
                    dsb = ds.astype(BF16)
                    dq_ref[_rows(qs, d), :] = jnp.dot(dsb, kw, preferred_element_type=F32) * SCALE
                    dkw = lax.dot_general(dsb, q, TN_DIMS, preferred_element_type=F32) * SCALE
                    dvw = lax.dot_general(p.astype(BF16), do, TN_DIMS, preferred_element_type=F32)
                    dkbuf[prev, :] += dkw[0:NQ, :]
                    dkbuf[cur, :] += dkw[NQ:2 * NQ, :]
                    dvbuf[prev, :] += dvw[0:NQ, :]
                    dvbuf[cur, :] += dvw[NQ:2 * NQ, :]
                    return carry

                for r in range(d):
                    unit(r, 0)

        dk_ref[...] = dkbuf[done:done + RB, :]
        dv_ref[...] = dvbuf[done:done + RB, :]

    steps, done = (nbk + 1, 0) if nbk > 1 else (1, RB)
    blk = lambda f: pl.BlockSpec((None, RB, HEAD), f)
    cur_g = blk(lambda h, b: (head0 + h, jnp.minimum(b, nbk - 1), 0))
    cur = blk(lambda h, b: (h, jnp.minimum(b, nbk - 1), 0))
    prv = blk(lambda h, b: (h, jnp.maximum(b - 1, 0), 0))
    sq = pl.BlockSpec((None, NQ, 2 * NQ), lambda h, b: (h, 0, 0))
    return pl.pallas_call(
        body, name=name, grid=(nseq, steps),
        in_specs=[cur_g, cur_g, cur_g, pl.BlockSpec((None, NQ, 2 * NQ), lambda h, b: (jnp.right_shift(h, bias_shift), 0, 0)),
                  cur, cur, cur, cur],
        out_specs=[cur, prv, prv, sq],
        out_shape=[SDS((nseq, S, HEAD), F32)] * 3 + [SDS((nseq, NQ, 2 * NQ), F32)],
        scratch_shapes=[pltpu.VMEM((2 * RB, HEAD), F32)] * 4)(qh, kh, vh, biasm, da, wg, dh, lse)


def _merge_weights(l0, l1, l2):
    m = jnp.maximum(jnp.maximum(l0, l1), l2)
    e0, e1, e2 = jnp.exp(l0 - m), jnp.exp(l1 - m), jnp.exp(l2 - m)
    inv = 1.0 / (e0 + e1 + e2)
    return e0 * inv, e1 * inv, e2 * inv


def _merge_fwd(os_, lses):
    S = os_[0].shape[1]

    def body(o0, o1, o2, l0, l1, l2, a_ref, so, sl):
        for h in range(HPG):
            w0, w1, w2 = _merge_weights(l0[h], _from_p4(l1, h, so), _from_p4(l2, h, sl))
            a_ref[:, h * HEAD:(h + 1) * HEAD] = (w0 * o0[h] + w1 * _from_p4(o1, h, so) + w2 * _from_p4(o2, h, sl)).astype(BF16)

    hm = pl.BlockSpec((HPG, TM, HEAD), lambda i: (0, i, 0))
    p4 = pl.BlockSpec((HPG, P4, TM // P4, HEAD), lambda i: (0, 0, i, 0))
    return pl.pallas_call(
        body, name="merge_fwd", grid=(S // TM,), in_specs=[hm, p4, p4, hm, p4, p4],
        out_specs=pl.BlockSpec((TM, CH), lambda i: (i, 0)),
        out_shape=SDS((S, CH), BF16), scratch_shapes=[pltpu.VMEM((TM, HEAD), F32)] * 2)(*os_, *lses)


def _merge_bwd(dattn, os_, lses):
    S = dattn.shape[0]

    def body(da_ref, o0, o1, o2, l0, l1, l2, w0_ref, w1_ref, w2_ref, dah_ref, dh_ref, dah2_ref, dh2_ref, so, sl):
        for h in range(HPG):
            w = _merge_weights(l0[h], _from_p4(l1, h, so), _from_p4(l2, h, sl))
            attn = w[0] * o0[h] + w[1] * _from_p4(o1, h, so) + w[2] * _from_p4(o2, h, sl)
            da = da_ref[:, h * HEAD:(h + 1) * HEAD]
            dh = jnp.broadcast_to(jnp.sum(da * attn, axis=-1, keepdims=True), (TM, HEAD))
            w0_ref[h] = w[0]
            dah_ref[h] = da
            dh_ref[h] = dh
            _to_p4(w1_ref, h, w[1], so)
            _to_p4(w2_ref, h, w[2], so)
            _to_p4(dah2_ref, h, da, so)
            _to_p4(dh2_ref, h, dh, so)

    hm = pl.BlockSpec((HPG, TM, HEAD), lambda i: (0, i, 0))
    p4 = pl.BlockSpec((HPG, P4, TM // P4, HEAD), lambda i: (0, 0, i, 0))
    nat, perm = SDS((HPG, S, HEAD), F32), SDS((HPG, P4, S // P4, HEAD), F32)
    w0, w1, w2, dah, dh, dah2, dh2 = pl.pallas_call(
        body, name="merge_bwd", grid=(S // TM,),
        in_specs=[pl.BlockSpec((TM, CH), lambda i: (i, 0)), hm, p4, p4, hm, p4, p4],
        out_specs=[hm, p4, p4, hm, hm, p4, p4], out_shape=[nat, perm, perm, nat, nat, perm, perm],
        scratch_shapes=[pltpu.VMEM((TM, HEAD), F32)] * 2)(dattn, *os_, *lses)
    return (w0, dah, dh), (w1, dah2, dh2), (w2, dah2, dh2)


def _qkv_bwd(dz, z, dqs, dks, dvs, qw, kw):
    S = z.shape[0]
    nh = N_GROUPS * HPG

    def body(dz_in, zq, zk, *refs):
        del dz_in
        dq_refs, dk_refs, dv_refs = refs[0:3], refs[3:6], refs[6:9]
        qw_ref, kw_ref, dz_ref, dqw_ref, dkw_ref, scr = refs[9:]

        @pl.when(pl.program_id(0) == 0)
        def _():
            dqw_ref[...] = jnp.zeros_like(dqw_ref)
            dkw_ref[...] = jnp.zeros_like(dkw_ref)

        def grad(refs3, g, hh):
            return _from_p4(refs3[g], hh, scr) if g > 0 else refs3[g][hh]

        def nbwd(xr, dy, wr, dwr, h, off):
            g = h // HPG
            x = xr[:, h * HEAD:(h + 1) * HEAD].astype(F32)
            r = lax.rsqrt(jnp.mean(x * x, axis=-1, keepdims=True) + RMS_EPS)
            xhat = x * r
            dyw = dy * wr[g:g + 1, :]
            dz_ref[:, off + h * HEAD:off + (h + 1) * HEAD] = (
                r * (dyw - xhat * jnp.mean(dyw * xhat, axis=-1, keepdims=True))).astype(BF16)
            dwr[g:g + 1, :] += jnp.sum(dy * xhat, axis=0, keepdims=True)

        for h in range(nh):
            g, hh = h // HPG, h % HPG
            nbwd(zq, grad(dq_refs, g, hh), qw_ref, dqw_ref, h, O_Q)
            nbwd(zk, grad(dk_refs, g, hh), kw_ref, dkw_ref, h, O_K)
            dz_ref[:, O_V + h * HEAD:O_V + (h + 1) * HEAD] = grad(dv_refs, g, hh).astype(BF16)

    hm = pl.BlockSpec((HPG, TM, HEAD), lambda i: (0, i, 0))
    p4 = pl.BlockSpec((HPG, P4, TM // P4, HEAD), lambda i: (0, 0, i, 0))
    return pl.pallas_call(
        body, name="qkv_bwd", grid=(S // TM,),
        in_specs=[pl.BlockSpec(memory_space=pl.ANY), pl.BlockSpec((TM, QKV_W), lambda i: (i, 0)),
                  pl.BlockSpec((TM, QKV_W), lambda i: (i, 1))] + [hm, p4, p4] * 3 + [_full((N_GROUPS, HEAD)), _full((N_GROUPS, HEAD))],
        out_specs=[pl.BlockSpec((TM, 3 * QKV_W), lambda i: (i, 0)), _full((N_GROUPS, HEAD)), _full((N_GROUPS, HEAD))],
        out_shape=[SDS(dz.shape, BF16), SDS((N_GROUPS, HEAD), F32), SDS((N_GROUPS, HEAD), F32)],
        scratch_shapes=[pltpu.VMEM((TM, HEAD), F32)],
        input_output_aliases={0: 0})(dz, z, z, *dqs, *dks, *dvs, qw, kw)


def _conv_fwd(z, cw, cb, lnw, lnb):
    S = z.shape[0]
    H = 32

    def body(zv, zg, cw_ref, cb_ref, lnw_ref, lnb_ref, u1_ref, u3_ref, xbuf):
        i = pl.program_id(0)

        @pl.when(i == 0)
        def _():
            xbuf[0:H, :] = jnp.zeros((H, CH), F32)

        @pl.when(i > 0)
        def _():
            xbuf[0:H, :] = xbuf[TM:TM + H, :]

        xbuf[H:H + TM, :] = zv[...].astype(F32) * _sig(zg[...].astype(F32))
        for r0 in range(0, TM, CONV_RC):
            rows = pl.ds(r0, CONV_RC)
            parts = []
            for lanes in CONV_LANES:
                part = [jnp.broadcast_to(cb_ref[:, lanes], (CONV_RC, CONV_LB))]

                def tap(q, view, part=part, lanes=lanes):
                    part[0] = part[0] + view * cw_ref[q - 2:q - 1, lanes]

                _conv_taps(xbuf, r0, lanes, 2, CONV_K + 1, tap)
                parts.append(part[0])
            acc = jnp.concatenate(parts, axis=1)
            u1_ref[rows, :] = acc
            mu = jnp.mean(acc, axis=-1, keepdims=True)
            xc = acc - mu
            yl = xc * lax.rsqrt(jnp.mean(xc * xc, axis=-1, keepdims=True) + LN_EPS) * lnw_ref[...] + lnb_ref[...]
            u3_ref[rows, :] = (yl * _sig(yl)).astype(BF16)

    row = pl.BlockSpec((TM, CH), lambda i: (i, 0))
    return pl.pallas_call(
        body, name="conv_fwd", grid=(S // TM,),
        in_specs=[pl.BlockSpec((TM, CH), lambda i: (i, O_CV // CH)), pl.BlockSpec((TM, CH), lambda i: (i, O_CG // CH)),
                  _full((CONV_K, CH)), _full((1, CH)), _full((1, CH)), _full((1, CH))],
        out_specs=[row, row], out_shape=[SDS((S, CH), F32), SDS((S, CH), BF16)],
        scratch_shapes=[pltpu.VMEM((TM + H, CH), F32)])(z, z, cw, cb, lnw, lnb)


def _conv_bwd_a(du3, u1, z, lnw, lnb):
    S = z.shape[0]
    H = 32

    nt = S // TM

    def body(du3_ref, u1_ref, zv, zg, lnw_ref, lnb_ref, du1_ref, acc_ref, xbuf, tacc):
        i = pl.program_id(0)

        @pl.when(i == 0)
        def _():
            xbuf[0:H, :] = jnp.zeros((H, CH), F32)
            tacc[...] = jnp.zeros_like(tacc)

        @pl.when(i > 0)
        def _():
            xbuf[0:H, :] = xbuf[TM:TM + H, :]

        xbuf[H:H + TM, :] = zv[...].astype(F32) * _sig(zg[...].astype(F32))
        for r0 in range(0, TM, CONV_RC):
            rows = pl.ds(r0, CONV_RC)
            u1 = u1_ref[rows, :]
            mu = jnp.mean(u1, axis=-1, keepdims=True)
            xc = u1 - mu
            r = lax.rsqrt(jnp.mean(xc * xc, axis=-1, keepdims=True) + LN_EPS)
            yhat = xc * r
            yl = yhat * lnw_ref[...] + lnb_ref[...]
            sg = _sig(yl)
            dyl = du3_ref[rows, :] * (sg * (1.0 + yl * (1.0 - sg)))
            dyh = dyl * lnw_ref[...]
            du1 = r * (dyh - jnp.mean(dyh, axis=-1, keepdims=True) - yhat * jnp.mean(dyh * yhat, axis=-1, keepdims=True))
            du1_ref[rows, :] = du1
            tacc[33] += _fold8(dyl * yhat)
            tacc[34] += _fold8(dyl)
            tacc[32] += _fold8(du1)
            for lanes in CONV_LANES:
                d = du1[:, lanes]

                def tap(q, view, d=d, lanes=lanes):
                    tacc[q - 2, :, lanes] += _fold8(d * view)

                _conv_taps(xbuf, r0, lanes, 2, CONV_K + 1, tap)

        @pl.when(i == nt - 1)
        def _():
            for k in range(40):
                acc_ref[k:k + 1, :] = jnp.sum(tacc[k], axis=0, keepdims=True)

    row = pl.BlockSpec((TM, CH), lambda i: (i, 0))
    return pl.pallas_call(
        body, name="conv_bwd_a", grid=(nt,),
        in_specs=[row, row, pl.BlockSpec((TM, CH), lambda i: (i, O_CV // CH)), pl.BlockSpec((TM, CH), lambda i: (i, O_CG // CH)),
                  _full((1, CH)), _full((1, CH))],
        out_specs=[row, _full((40, CH))], out_shape=[SDS((S, CH), F32), SDS((40, CH), F32)],
        scratch_shapes=[pltpu.VMEM((TM + H, CH), F32), pltpu.VMEM((40, 8, CH), F32)])(du3, u1, z, z, lnw, lnb)


def _conv_bwd_b(dz, du1, z, cw):
    S = z.shape[0]
    nt = S // TM
    H = 32

    def body(dz_in, du1_ref, zv, zg, cw_ref, dz_ref, ybuf, dgate):
        del dz_in
        i, p = pl.program_id(0), pl.program_id(1)

        @pl.when(p == 0)
        def _():
            @pl.when(i == 0)
            def _():
                ybuf[TM:TM + H, :] = jnp.zeros((H, CH), F32)

            @pl.when(i > 0)
            def _():
                ybuf[TM:TM + H, :] = ybuf[0:H, :]

            ybuf[0:TM, :] = du1_ref[...]
            for r0 in range(0, TM, CONV_RC):
                rows = pl.ds(r0, CONV_RC)
                parts = []
                for lanes in CONV_LANES:
                    part = [jnp.zeros((CONV_RC, CONV_LB), F32)]

                    def tap(q, view, part=part, lanes=lanes):
                        part[0] = part[0] + view * cw_ref[CONV_K - 1 - q:CONV_K - q, lanes]

                    _conv_taps(ybuf, r0, lanes, 0, CONV_K - 1, tap)
                    parts.append(part[0])
                acc = jnp.concatenate(parts, axis=1)
                val = zv[rows, :].astype(F32)
                sg = _sig(zg[rows, :].astype(F32))
                dz_ref[rows, :] = (acc * sg).astype(BF16)
                dgate[rows, :] = (acc * val * sg * (1.0 - sg)).astype(BF16)

        @pl.when(p == 1)
        def _():
            dz_ref[...] = dgate[...]

    rev = lambda c: pl.BlockSpec((TM, CH), lambda i, p: (nt - 1 - i, c))
    return pl.pallas_call(
        body, name="conv_bwd_b", grid=(nt, 2),
        in_specs=[pl.BlockSpec(memory_space=pl.ANY), rev(0), rev(O_CV // CH), rev(O_CG // CH), _full((CONV_K, CH))],
        out_specs=pl.BlockSpec((TM, CH), lambda i, p: (nt - 1 - i, O_CV // CH + p)),
        out_shape=SDS(dz.shape, BF16),
        scratch_shapes=[pltpu.VMEM((TM + H, CH), F32), pltpu.VMEM((TM, CH), BF16)],
        input_output_aliases={0: 0})(dz, du1, z, z, cw)


def _memkv_fwd(mem, mnw, wkv, xkw):
    M, D = mem.shape

    def body(mem_ref, mnw_ref, w_ref, xkw_ref, mn_ref, kv_ref, mk_ref, mv_ref):
        x = mem_ref[...]
        mn = (x * lax.rsqrt(jnp.mean(x * x, axis=-1, keepdims=True) + RMS_EPS) * mnw_ref[...]).astype(BF16)
        mn_ref[...] = mn
        kv = jnp.dot(mn, w_ref[...], preferred_element_type=F32)
        kv_ref[...] = kv
        for h in range(HPG):
            k = kv[:, h * HEAD:(h + 1) * HEAD]
            mk_ref[:, h * HEAD:(h + 1) * HEAD] = (
                k * lax.rsqrt(jnp.mean(k * k, axis=-1, keepdims=True) + RMS_EPS) * xkw_ref[...]).astype(BF16)
        mv_ref[...] = kv[:, CH:2 * CH].astype(BF16)

    return pl.pallas_call(
        body, name="memkv_fwd",
        out_shape=[SDS((M, D), BF16), SDS((M, 2 * CH), F32), SDS((M, CH), BF16), SDS((M, CH), BF16)])(mem, mnw, wkv, xkw)


def _cross_q(zx, xqw, h):
    x = zx[:, h * HEAD:(h + 1) * HEAD].astype(F32)
    r = lax.rsqrt(jnp.mean(x * x, axis=-1, keepdims=True) + RMS_EPS)
    xhat = x * r
    return xhat, r, xhat * xqw


def _cross_fwd(z, xqw, mk, mv):
    S = z.shape[0]
    M = mk.shape[0]

    def body(zx, xqw_ref, mk_ref, mv_ref, o_ref):
        for h in range(HPG):
            sl = slice(h * HEAD, (h + 1) * HEAD)
            _, _, q = _cross_q(zx, xqw_ref[...], h)
            s = lax.dot_general(q.astype(BF16), mk_ref[:, sl], NT_DIMS, preferred_element_type=F32) * SCALE
            e = jnp.exp(s - jnp.max(s, axis=-1, keepdims=True))
            p = e / jnp.sum(e, axis=-1, keepdims=True)
            o_ref[:, sl] = jnp.dot(p.astype(BF16), mv_ref[:, sl], preferred_element_type=F32).astype(BF16)

    return pl.pallas_call(
        body, name="cross_fwd", grid=(S // TM,),
        in_specs=[pl.BlockSpec((TM, CH), lambda i: (i, O_XQ // CH)), _full((1, HEAD)), _full((M, CH)), _full((M, CH))],
        out_specs=pl.BlockSpec((TM, CH), lambda i: (i, 0)), out_shape=SDS((S, CH), BF16))(z, xqw, mk, mv)


def _cross_bwd(dz, doc, z, xqw, mk, mv):
    S = z.shape[0]
    M = mk.shape[0]

    def body(dz_in, do_ref, zx, xqw_ref, mk_ref, mv_ref, dz_ref, dmk_ref, dmv_ref, dxw_ref):
        del dz_in

        @pl.when(pl.program_id(0) == 0)
        def _():
            dmk_ref[...] = jnp.zeros_like(dmk_ref)
            dmv_ref[...] = jnp.zeros_like(dmv_ref)
            dxw_ref[...] = jnp.zeros_like(dxw_ref)

        for h in range(HPG):
            sl = slice(h * HEAD, (h + 1) * HEAD)
            xhat, r, q = _cross_q(zx, xqw_ref[...], h)
            qb = q.astype(BF16)
            s = lax.dot_general(qb, mk_ref[:, sl], NT_DIMS, preferred_element_type=F32) * SCALE
            e = jnp.exp(s - jnp.max(s, axis=-1, keepdims=True))
            p = e / jnp.sum(e, axis=-1, keepdims=True)
            do = do_ref[:, sl].astype(BF16)
            dp = lax.dot_general(do, mv_ref[:, sl], NT_DIMS, preferred_element_type=F32)
            ds = (p * (dp - jnp.sum(p * dp, axis=-1, keepdims=True)) * SCALE).astype(BF16)
            dq = jnp.dot(ds, mk_ref[:, sl], preferred_element_type=F32)
            dmk_ref[:, sl] += lax.dot_general(ds, qb, TN_DIMS, preferred_element_type=F32)
            dmv_ref[:, sl] += lax.dot_general(p.astype(BF16), do, TN_DIMS, preferred_element_type=F32)
            dyw = dq * xqw_ref[...]
            dz_ref[:, sl] = (r * (dyw - xhat * jnp.mean(dyw * xhat, axis=-1, keepdims=True))).astype(BF16)
            dxw_ref[...] += jnp.sum(dq * xhat, axis=0, keepdims=True)

    return pl.pallas_call(
        body, name="cross_bwd", grid=(S // TM,),
        in_specs=[pl.BlockSpec(memory_space=pl.ANY), pl.BlockSpec((TM, CH), lambda i: (i, 0)),
                  pl.BlockSpec((TM, CH), lambda i: (i, O_XQ // CH)), _full((1, HEAD)), _full((M, CH)), _full((M, CH))],
        out_specs=[pl.BlockSpec((TM, CH), lambda i: (i, O_XQ // CH)), _full((M, CH)), _full((M, CH)), _full((1, HEAD))],
        out_shape=[SDS(dz.shape, BF16), SDS((M, CH), F32), SDS((M, CH), F32), SDS((1, HEAD), F32)],
        input_output_aliases={0: 0})(dz, doc, z, xqw, mk, mv)


def _memkv_bwd(dmk, dmv, kv, mem, mn, mnw, wkv, xkw):
    M, D = mem.shape

    def body(dmk_ref, dmv_ref, kv_ref, mem_ref, mn_ref, mnw_ref, w_ref, xkw_ref, dw_ref, dxk_ref, dmn_ref, dkv):
        dxk = jnp.zeros((1, HEAD), F32)
        for h in range(HPG):
            sl = slice(h * HEAD, (h + 1) * HEAD)
            k = kv_ref[:, sl]
            r = lax.rsqrt(jnp.mean(k * k, axis=-1, keepdims=True) + RMS_EPS)
            khat = k * r
            dy = dmk_ref[:, sl]
            dyw = dy * xkw_ref[...]
            dkv[:, sl] = (r * (dyw - khat * jnp.mean(dyw * khat, axis=-1, keepdims=True))).astype(BF16)
            dxk = dxk + jnp.sum(dy * khat, axis=0, keepdims=True)
        dxk_ref[...] = dxk
        dkv[:, CH:2 * CH] = dmv_ref[...].astype(BF16)
        dw_ref[...] = lax.dot_general(mn_ref[...], dkv[...], TN_DIMS, preferred_element_type=F32)
        dn = lax.dot_general(dkv[...], w_ref[...], NT_DIMS, preferred_element_type=F32)
        x = mem_ref[...]
        r = lax.rsqrt(jnp.mean(x * x, axis=-1, keepdims=True) + RMS_EPS)
        dmn_ref[...] = jnp.sum(dn * x * r, axis=0, keepdims=True)

    return pl.pallas_call(
        body, name="memkv_bwd",
        out_shape=[SDS((D, 2 * CH), F32), SDS((1, HEAD), F32), SDS((1, D), F32)],
        scratch_shapes=[pltpu.VMEM((M, 2 * CH), BF16)])(dmk, dmv, kv, mem, mn, mnw, wkv, xkw)


def _branch_proj(a_ref, w_ref, y_ref):
    G, _, n = w_ref.shape
    a = a_ref[...]
    for g in range(G):
        y_ref[:, g * n:(g + 1) * n] = jnp.dot(a, w_ref[g], preferred_element_type=F32)


OUT_RC = 16


def _gates(zg_ref, bg_ref, rows, k, D):
    return _sig(zg_ref[rows, k * D:(k + 1) * D].astype(F32) + bg_ref[:, k * D:(k + 1) * D])


def _outproj_fwd(x, z, bg, attn, u3, oc, wao, wco, wxo, wout, fnw):
    S, D = x.shape
    tm = TM

    def body(x_ref, zg_ref, bg_ref, a_ref, u_ref, c_ref, wa, wc, wx, wo, fnw_ref, h1_ref, hn_ref, ya, yc, yx, mg):
        _branch_proj(a_ref, wa, ya)
        _branch_proj(u_ref, wc, yc)
        _branch_proj(c_ref, wx, yx)
        for r0 in range(0, tm, OUT_RC):
            rows = pl.ds(r0, OUT_RC)
            mg[rows, :] = (_gates(zg_ref, bg_ref, rows, 0, D) * ya[rows, :] + _gates(zg_ref, bg_ref, rows, 1, D) * yc[rows, :]
                           + _gates(zg_ref, bg_ref, rows, 2, D) * yx[rows, :]).astype(BF16)
        ya[...] = jnp.dot(mg[...], wo[...], preferred_element_type=F32)
        for r0 in range(0, tm, OUT_RC):
            rows = pl.ds(r0, OUT_RC)
            h1 = x_ref[rows, :] + ya[rows, :]
            h1_ref[rows, :] = h1
            hn_ref[rows, :] = (h1 * lax.rsqrt(jnp.mean(h1 * h1, axis=-1, keepdims=True) + RMS_EPS) * fnw_ref[...]).astype(BF16)

    row = lambda w: pl.BlockSpec((tm, w), lambda i: (i, 0))
    return pl.pallas_call(
        body, name="outproj_fwd", grid=(S // tm,),
        in_specs=[row(D), pl.BlockSpec((tm, 3 * D), lambda i: (i, O_G // (3 * D))), _full((1, 3 * D)), row(CH), row(CH), row(CH),
                  _full(wao.shape), _full(wco.shape), _full(wxo.shape), _full((D, D)), _full((1, D))],
        out_specs=[row(D), row(D)], out_shape=[SDS((S, D), F32), SDS((S, D), BF16)],
        scratch_shapes=[pltpu.VMEM((tm, D), F32)] * 3 + [pltpu.VMEM((tm, D), BF16)])(x, z, bg, attn, u3, oc, wao, wco, wxo, wout, fnw)


def _outproj_bwd(dh1, z, bg, attn, u3, oc, wao, wco, wxo, wout, n_in):
    S, D = dh1.shape
    tm = 256
    nt = S // tm
    G, _, n = wao.shape

    def body(dh_ref, zg_ref, bg_ref, a_ref, u_ref, c_ref, wa, wc, wx, wo,
             dz_ref, da_ref, du_ref, dc_ref, dbg_ref, dwo_ref, dwa_ref, dwc_ref, dwx_ref,
             ya, yc, yx, dm, dy, mg, bacc, wacc):
        i = pl.program_id(0)

        @pl.when(i == 0)
        def _():
            bacc[...] = jnp.zeros_like(bacc)
            wacc[...] = jnp.zeros_like(wacc)
            dwo_ref[...] = jnp.zeros_like(dwo_ref)

        _branch_proj(a_ref, wa, ya)
        _branch_proj(u_ref, wc, yc)
        _branch_proj(c_ref, wx, yx)
        dhb = dh_ref[...].astype(BF16)
        dm[...] = lax.dot_general(dhb, wo[...], NT_DIMS, preferred_element_type=F32)
        for r0 in range(0, tm, OUT_RC):
            rows = pl.ds(r0, OUT_RC)
            dmv = dm[rows, :]
            merged = jnp.zeros((OUT_RC, D), F32)
            for k, y in enumerate((ya, yc, yx)):
                gk = _gates(zg_ref, bg_ref, rows, k, D)
                yk = y[rows, :]
                merged = merged + gk * yk
                dzg = dmv * yk * gk * (1.0 - gk)
                dz_ref[rows, k * D:(k + 1) * D] = dzg.astype(BF16)
                bacc[:, k * D:(k + 1) * D] += _fold8(dzg)
                dy[k, rows, :] = (dmv * gk).astype(BF16)
            mg[rows, :] = merged.astype(BF16)
        dwo_ref[...] += lax.dot_general(mg[...], dhb, TN_DIMS, preferred_element_type=F32)
        for k, (b_ref, w_ref, db_ref) in enumerate(((a_ref, wa, da_ref), (u_ref, wc, du_ref), (c_ref, wx, dc_ref))):
            dyk = dy[k]
            acc = jnp.zeros((tm, CH), F32)
            for g in range(G):
                acc = acc + lax.dot_general(dyk[:, g * n:(g + 1) * n], w_ref[g], NT_DIMS, preferred_element_type=F32)
            db_ref[...] = acc
            wacc[k] += lax.dot_general(b_ref[...], dyk, TN_DIMS, preferred_element_type=F32)

        @pl.when(i == nt - 1)
        def _():
            dbg_ref[...] = jnp.sum(bacc[...], axis=0, keepdims=True)
            for k, dw_ref in enumerate((dwa_ref, dwc_ref, dwx_ref)):
                for g in range(G):
                    dw_ref[g] = wacc[k, :, g * n:(g + 1) * n]

    row = lambda w: pl.BlockSpec((tm, w), lambda i: (i, 0))
    return pl.pallas_call(
        body, name="outproj_bwd", grid=(nt,),
        in_specs=[row(D), pl.BlockSpec((tm, 3 * D), lambda i: (i, O_G // (3 * D))), _full((1, 3 * D)), row(CH), row(CH), row(CH),
                  _full(wao.shape), _full(wco.shape), _full(wxo.shape), _full((D, D))],
        out_specs=[pl.BlockSpec((tm, 3 * D), lambda i: (i, O_G // (3 * D))), row(CH), row(CH), row(CH), _full((1, 3 * D)),
                   _full((D, D))] + [_full(wao.shape)] * 3,
        out_shape=[SDS((S, n_in), BF16)] + [SDS((S, CH), F32)] * 3 + [SDS((1, 3 * D), F32), SDS((D, D), F32)]
        + [SDS(wao.shape, F32)] * 3,
        scratch_shapes=[pltpu.VMEM((tm, D), F32)] * 4 + [pltpu.VMEM((3, tm, D), BF16), pltpu.VMEM((tm, D), BF16),
                                                        pltpu.VMEM((8, 3 * D), F32), pltpu.VMEM((3, CH, D), F32)],
    )(dh1, z, bg, attn, u3, oc, wao, wco, wxo, wout)


FFN_TC = 256
FFN_H = 8


def _ffn_taps(buf, r0):
    xx = buf[pl.ds(r0, FFN_RC + FFN_H), :]
    return xx[FFN_H:], pltpu.roll(xx, 1, 0)[FFN_H:], pltpu.roll(xx, 2, 0)[FFN_H:]


def _ffn_conv(taps, w_ref, b_ref):
    x0, x1, x2 = taps
    return b_ref[...] + x0 * w_ref[2:3, :] + x1 * w_ref[1:2, :] + x2 * w_ref[0:1, :]


def _ffn_fwd(up, cw, cb, wdown, h1, target):
    S, D = h1.shape
    F2 = up.shape[1]
    F = F2 // 2
    nj = F // FFN_TC
    tm = TM

    def body(up_ref, halo_ref, cw_ref, cb_ref, w_ref, h_ref, t_ref, dy_ref, loss_ref, ac_ref, gc_ref, abuf, gbuf, act_s):
        i = pl.program_id(0)

        @pl.when(i == 0)
        def _():
            loss_ref[...] = jnp.zeros_like(loss_ref)

        for j in range(nj):
            ca, cg = slice(j * FFN_TC, (j + 1) * FFN_TC), slice(F + j * FFN_TC, F + (j + 1) * FFN_TC)
            first = i == 0
            abuf[0:FFN_H, :] = jnp.where(first, 0.0, halo_ref[:, ca].astype(F32))
            gbuf[0:FFN_H, :] = jnp.where(first, 0.0, halo_ref[:, cg].astype(F32))
            abuf[FFN_H:FFN_H + tm, :] = up_ref[:, ca].astype(F32)
            gbuf[FFN_H:FFN_H + tm, :] = up_ref[:, cg].astype(F32)
            for r0 in range(0, tm, FFN_RC):
                rows = pl.ds(r0, FFN_RC)
                a = _ffn_conv(_ffn_taps(abuf, r0), cw_ref[:, ca], cb_ref[:, ca])
                gt = _ffn_conv(_ffn_taps(gbuf, r0), cw_ref[:, cg], cb_ref[:, cg])
                ac_ref[rows, ca] = a.astype(BF16)
                gc_ref[rows, ca] = gt.astype(BF16)
                act_s[rows, ca] = (gt * _sig(gt) * a).astype(BF16)
        err = h_ref[...] + jnp.dot(act_s[...], w_ref[...], preferred_element_type=F32) - t_ref[...]
        dy_ref[...] = err * (1.0 / D)
        loss_ref[...] += 0.5 * jnp.sum(jnp.mean(err * err, axis=-1, keepdims=True))

    row = lambda w: pl.BlockSpec((tm, w), lambda i: (i, 0))
    halo = pl.BlockSpec((FFN_H, F2), lambda i: (jnp.maximum(i * (tm // FFN_H) - 1, 0), 0))
    return pl.pallas_call(
        body, name="ffn_fwd", grid=(S // tm,),
        in_specs=[row(F2), halo, _full((FFN_K, F2)), _full((1, F2)), _full((F, D)), row(D), row(D)],
        out_specs=[row(D), _full((8, 128)), row(F), row(F)],
        out_shape=[SDS((S, D), F32), SDS((8, 128), F32), SDS((S, F), BF16), SDS((S, F), BF16)],
        scratch_shapes=[pltpu.VMEM((tm + FFN_H, FFN_TC), F32)] * 2 + [pltpu.VMEM((tm, F), BF16)])(up, up, cw, cb, wdown, h1, target)


def _ffn_bwd_a(dy, wdown, ac, gc):
    S, D = dy.shape
    F = ac.shape[1]
    nj = F // FFN_TC
    tm = 1024 if S % 1024 == 0 else TM

    def body(dy_ref, wd_ref, a_ref, g_ref, da_ref, dg_ref, acca_ref, accg_ref, dwd_ref, dact_s, act_s):
        i, j = pl.program_id(0), pl.program_id(1)

        @pl.when((i == 0) & (j == 0))
        def _():
            acca_ref[...] = jnp.zeros_like(acca_ref)
            accg_ref[...] = jnp.zeros_like(accg_ref)
            dwd_ref[...] = jnp.zeros_like(dwd_ref)

        dyb = dy_ref[...].astype(BF16)
        dact_s[...] = lax.dot_general(dyb, wd_ref[...], NT_DIMS, preferred_element_type=F32)
        pa = pg = jnp.zeros((8, FFN_TC), F32)
        for r0 in range(0, tm, FFN_RC):
            rows = pl.ds(r0, FFN_RC)
            a = a_ref[rows, :].astype(F32)
            gt = g_ref[rows, :].astype(F32)
            dact = dact_s[rows, :]
            sg = _sig(gt)
            silu = gt * sg
            act_s[rows, :] = (silu * a).astype(BF16)
            dac = dact * silu
            dgc = dact * a * (sg * (1.0 + gt * (1.0 - sg)))
            da_ref[rows, :] = dac.astype(BF16)
            dg_ref[rows, :] = dgc.astype(BF16)
            pa = pa + _fold8(dac)
            pg = pg + _fold8(dgc)
        acca_ref[j] += pa
        accg_ref[j] += pg
        dwd_ref[pl.ds(pl.multiple_of(j * FFN_TC, FFN_TC), FFN_TC), :] += lax.dot_general(
            act_s[...], dyb, TN_DIMS, preferred_element_type=F32)

    col = pl.BlockSpec((tm, FFN_TC), lambda i, j: (i, j))
    return pl.pallas_call(
        body, name="ffn_bwd_a", grid=(S // tm, nj),
        in_specs=[pl.BlockSpec((tm, D), lambda i, j: (i, 0)), pl.BlockSpec((FFN_TC, D), lambda i, j: (j, 0)), col, col],
        out_specs=[col, col] + [_full((nj, 8, FFN_TC))] * 2 + [_full((F, D))],
        out_shape=[SDS((S, F), BF16)] * 2 + [SDS((nj, 8, FFN_TC), F32)] * 2 + [SDS((F, D), F32)],
        scratch_shapes=[pltpu.VMEM((tm, FFN_TC), F32), pltpu.VMEM((tm, FFN_TC), BF16)])(dy, wdown, ac, gc)


def _ffn_bwd_b(dca, dcg, up, cw):
    S, F = dca.shape
    nj = F // FFN_TC
    tm = 4096 if S % 4096 == 0 else TM
    nt = S // tm
    span = FFN_RC + FFN_H

    def body(a_ref, g_ref, u_ref, w_ref, o_ref, tacc_ref, ybuf):
        j, i = pl.program_id(0), pl.program_id(1)

        @pl.when(i == 0)
        def _():
            ybuf[tm:tm + FFN_H, :] = jnp.zeros((FFN_H, FFN_TC), F32)
            tacc_ref[...] = jnp.zeros_like(tacc_ref)

        @pl.when(i > 0)
        def _():
            ybuf[tm:tm + FFN_H, :] = ybuf[0:FFN_H, :]

        ybuf[0:tm, :] = jnp.where(j < nj, a_ref[...], g_ref[...]).astype(F32)
        p = [jnp.zeros((8, FFN_TC), F32)] * FFN_K
        for r0 in range(0, tm, FFN_RC):
            rows = pl.ds(r0, FFN_RC)
            yy = ybuf[pl.ds(r0, span), :]
            ys = (yy[:FFN_RC], pltpu.roll(yy, span - 1, 0)[:FFN_RC], pltpu.roll(yy, span - 2, 0)[:FFN_RC])
            o_ref[rows, :] = (ys[0] * w_ref[2:3, :] + ys[1] * w_ref[1:2, :] + ys[2] * w_ref[0:1, :]).astype(BF16)
            u = u_ref[rows, :].astype(F32)
            for k in range(FFN_K):
                p[k] = p[k] + _fold8(ys[FFN_K - 1 - k] * u)
        for k in range(FFN_K):
            tacc_ref[k] += p[k]

    rev = lambda f: pl.BlockSpec((tm, FFN_TC), lambda j, i: (nt - 1 - i, f(j)))
    return pl.pallas_call(
        body, name="ffn_bwd_b", grid=(2 * nj, nt),
        in_specs=[rev(lambda j: jnp.minimum(j, nj - 1)), rev(lambda j: jnp.maximum(j - nj, 0)), rev(lambda j: j),
                  pl.BlockSpec((FFN_K, FFN_TC), lambda j, i: (0, j))],
        out_specs=[rev(lambda j: j), pl.BlockSpec((None, FFN_K, 8, FFN_TC), lambda j, i: (j, 0, 0, 0))],
        out_shape=[SDS((S, 2 * F), BF16), SDS((2 * nj, FFN_K, 8, FFN_TC), F32)],
        scratch_shapes=[pltpu.VMEM((tm + FFN_H, FFN_TC), F32)])(dca, dcg, up, cw)


def _adamw_update(w_ref, g_ref, m_ref, v_ref, d_ref, nm_ref, nv_ref):
    gv = g_ref[...]
    m2 = ADAM_B1 * m_ref[...] + (1.0 - ADAM_B1) * gv
    v2 = ADAM_B2 * v_ref[...] + (1.0 - ADAM_B2) * jnp.square(gv)
    m_hat = m2 / (1.0 - ADAM_B1 ** ADAM_STEP)
    v_hat = v2 / (1.0 - ADAM_B2 ** ADAM_STEP)
    d_ref[...] = -ADAM_LR * (m_hat / (jnp.sqrt(v_hat) + ADAM_EPS) + ADAM_WD * w_ref[...])
    nm_ref[...] = m2
    nv_ref[...] = v2


def _adamw_small(ws, gs, ms, vs):
    n = len(ws)

    def body(*refs):
        for i in range(n):
            _adamw_update(*[refs[k * n + i] for k in range(7)])

    shapes = [SDS(w.shape, F32) for w in ws]
    res = pl.pallas_call(body, name="adamw_small", out_shape=shapes * 3)(*ws, *gs, *ms, *vs)
    return res[:n], res[n:2 * n], res[2 * n:]


def _adamw(w, g, m, v, name):
    R, C = w.shape
    tr = _row_tile(R, max(8, (2 ** 20) // (4 * C) // 8 * 8))

    def body(w_ref, g_ref, m_ref, v_ref, d_ref, nm_ref, nv_ref):
        _adamw_update(w_ref, g_ref, m_ref, v_ref, d_ref, nm_ref, nv_ref)

    blk = pl.BlockSpec((tr, C), lambda i: (i, 0))
    return pl.pallas_call(
        body, name=name, grid=(R // tr,), in_specs=[blk] * 4, out_specs=[blk] * 3,
        out_shape=[SDS((R, C), F32)] * 3)(w, g, m, v)


HBM_SPEC = pl.BlockSpec(memory_space=pltpu.HBM)
SEM_SPEC = pl.BlockSpec(memory_space=pltpu.SEMAPHORE)
DATAFLOW_EFFECT = pltpu.SideEffectType.DATAFLOW_SIDE_EFFECTING


def _position():
    return lax.axis_index("x"), lax.axis_index("y"), lax.axis_index("c")


def _other_chips(x, y):
    return [(1 - x, y), (x, 1 - y), (1 - x, 1 - y)]


SIBLING_BARRIER = 1


def _sibling_handshake(x, y, c):
    barrier = pltpu.get_barrier_semaphore()
    pl.semaphore_signal(barrier, inc=1, device_id=(x, y, 1 - c), device_id_type=MESH)
    pl.semaphore_wait(barrier, 1)


def _relations(x, y, js=(0, 1, 2)):
    return [(j, chip) for j, chip in enumerate(_other_chips(x, y)) if j in js]


def _all_gather_xy(arrs, name):
    n = len(arrs)
    hbm = pl.BlockSpec(memory_space=pl.ANY)

    def body(*refs):
        ins, outs = refs[:n], refs[n:2 * n]
        send_sems, recv_sems = refs[2 * n:]
        x, y, c = _position()
        me = 2 * x + y
        chips = _other_chips(x, y)

        def rcopy(i, k, src, dst, to):
            return pltpu.make_async_remote_copy(src_ref=src, dst_ref=dst, send_sem=send_sems.at[i, k], recv_sem=recv_sems.at[i, k],
                                                device_id=to, device_id_type=MESH)

        sends = []
        for i in range(n):
            own = rcopy(i, 6, ins[i], outs[i].at[me], (x, y, 1 - c))
            own.start()
            sends.append(own)
        for i in range(n):
            for j, (px, py) in enumerate(chips):
                cp = rcopy(i, j, ins[i].at[c], outs[i].at[me, c], (px, py, c))
                cp.start()
                sends.append(cp)
        for i in range(n):
            for j, (px, py) in enumerate(chips):
                got = outs[i].at[2 * px + py, c]
                rcopy(i, j, ins[i].at[c], got, (x, y, c)).wait_recv()
                fwd = rcopy(i, 3 + j, got, got, (x, y, 1 - c))
                fwd.start()
                sends.append(fwd)
        for i in range(n):
            for j, (px, py) in enumerate(chips):
                theirs = outs[i].at[2 * px + py, 1 - c]
                rcopy(i, 3 + j, theirs, theirs, (x, y, c)).wait_recv()
        for i in range(n):
            rcopy(i, 6, ins[i], outs[i].at[me], (x, y, c)).wait_recv()
        for cp in sends:
            cp.wait_send()

    return pl.pallas_call(
        body, name=name, in_specs=[hbm] * n, out_specs=[hbm] * n,
        out_shape=[SDS((4,) + a.shape, a.dtype) for a in arrs],
        scratch_shapes=[pltpu.SemaphoreType.DMA((n, 7)), pltpu.SemaphoreType.DMA((n, 7))])(*arrs)


def _ag_ici_start(arrs, name, js=(0, 1, 2), lands=None):
    n = len(arrs)

    def body(*refs):
        ins, lands = refs[:n], refs[n:2 * n]
        send_sems, recv_sems = refs[2 * n:2 * n + 2]
        token = refs[-1]
        x, y, c = _position()
        for i in range(n):
            for j, (px, py) in _relations(x, y, js):
                pltpu.make_async_remote_copy(src_ref=ins[i].at[c], dst_ref=lands[i].at[2 * x + y, c], send_sem=send_sems.at[3 * i + j],
                                             recv_sem=recv_sems.at[3 * i + j], device_id=(px, py, c), device_id_type=MESH).start()
        token[...] = jnp.zeros_like(token)

    if lands is None:
        lands = [pltpu.with_memory_space_constraint(lax.empty((4,) + a.shape, a.dtype), pltpu.HBM) for a in arrs]
        arrs = [pltpu.with_memory_space_constraint(a, pltpu.HBM) for a in arrs]
    res = pl.pallas_call(
        body, name=name,
        out_shape=[pltpu.SemaphoreType.DMA((3 * n,)), pltpu.SemaphoreType.DMA((3 * n,))]
        + [pltpu.HBM(a.shape, a.dtype) for a in arrs] + [pltpu.HBM(l.shape, l.dtype) for l in lands] + [SDS((8, 128), F32)],
        in_specs=[HBM_SPEC] * (2 * n), out_specs=[SEM_SPEC, SEM_SPEC] + [HBM_SPEC] * (2 * n) + [pl.BlockSpec(memory_space=pltpu.VMEM)],
        input_output_aliases={i: 2 + i for i in range(2 * n)},
        compiler_params=pltpu.CompilerParams(has_side_effects=DATAFLOW_EFFECT),
    )(*arrs, *lands)
    return res[0], res[1], list(res[2:2 + n]), list(res[2 + n:2 + 2 * n]), res[-1]


def _ag_ici_wait(send_sems, recv_sems, ins, lands, after, name, js=(0, 1, 2)):
    n = len(ins)

    def body(*refs):
        ins_r, lands_r = refs[:n], refs[n:2 * n]
        send_r, recv_r = refs[2 * n:2 * n + 2]
        x, y, c = _position()
        for i in range(n):
            for j, (px, py) in _relations(x, y, js):
                cp = pltpu.make_async_remote_copy(src_ref=ins_r[i].at[c], dst_ref=lands_r[i].at[2 * px + py, c], send_sem=send_r.at[3 * i + j],
                                                  recv_sem=recv_r.at[3 * i + j], device_id=(px, py, c), device_id_type=MESH)
                cp.wait_send()
                cp.wait_recv()

    res = pl.pallas_call(
        body, name=name,
        out_shape=[pltpu.HBM(a.shape, a.dtype) for a in list(ins) + list(lands)],
        in_specs=[HBM_SPEC] * (2 * n) + [SEM_SPEC, SEM_SPEC, pl.BlockSpec(memory_space=pl.ANY)], out_specs=[HBM_SPEC] * (2 * n),
        input_output_aliases={i: i for i in range(2 * n)},
        compiler_params=pltpu.CompilerParams(has_side_effects=DATAFLOW_EFFECT),
    )(*ins, *lands, send_sems, recv_sems, after)
    return list(res[:n]), list(res[n:])


def _ag_finish_start(arrs, lands, name, js=(0, 1, 2), own=True):
    n = len(arrs)

    def body(*refs):
        ins, bufs = refs[:n], refs[n:2 * n]
        send_sems, recv_sems = refs[2 * n:2 * n + 2]
        token = refs[-1]
        x, y, c = _position()
        _sibling_handshake(x, y, c)
        for i in range(n):
            if own:
                pltpu.make_async_remote_copy(src_ref=ins[i], dst_ref=bufs[i].at[2 * x + y], send_sem=send_sems.at[4 * i + 3],
                                             recv_sem=recv_sems.at[4 * i + 3], device_id=(x, y, 1 - c), device_id_type=MESH).start()
            for j, (px, py) in _relations(x, y, js):
                half = bufs[i].at[2 * px + py, c]
                pltpu.make_async_remote_copy(src_ref=half, dst_ref=half, send_sem=send_sems.at[4 * i + j],
                                             recv_sem=recv_sems.at[4 * i + j], device_id=(x, y, 1 - c), device_id_type=MESH).start()
        token[...] = jnp.zeros_like(token)

    res = pl.pallas_call(
        body, name=name,
        out_shape=[pltpu.SemaphoreType.DMA((4 * n,)), pltpu.SemaphoreType.DMA((4 * n,))]
        + [pltpu.HBM(a.shape, a.dtype) for a in list(arrs) + list(lands)] + [SDS((8, 128), F32)],
        in_specs=[HBM_SPEC] * (2 * n), out_specs=[SEM_SPEC, SEM_SPEC] + [HBM_SPEC] * (2 * n) + [pl.BlockSpec(memory_space=pltpu.VMEM)],
        input_output_aliases={i: 2 + i for i in range(2 * n)},
        compiler_params=pltpu.CompilerParams(has_side_effects=DATAFLOW_EFFECT, collective_id=SIBLING_BARRIER),
    )(*arrs, *lands)
    return res[0], res[1], list(res[2:2 + n]), list(res[2 + n:2 + 2 * n]), res[-1]


def _ag_finish_wait(send_sems, recv_sems, arrs, lands, after, name, js=(0, 1, 2), own=True):
    n = len(arrs)

    def body(*refs):
        ins, bufs = refs[:n], refs[n:2 * n]
        send_r, recv_r = refs[2 * n:2 * n + 2]
        x, y, c = _position()
        for i in range(n):
            if own:
                mine = pltpu.make_async_remote_copy(src_ref=ins[i], dst_ref=bufs[i].at[2 * x + y], send_sem=send_r.at[4 * i + 3],
                                                    recv_sem=recv_r.at[4 * i + 3], device_id=(x, y, 1 - c), device_id_type=MESH)
                mine.wait_send()
                mine.wait_recv()
            for j, (px, py) in _relations(x, y, js):
                pltpu.make_async_remote_copy(src_ref=bufs[i].at[2 * px + py, c], dst_ref=bufs[i].at[2 * px + py, c],
                                             send_sem=send_r.at[4 * i + j], recv_sem=recv_r.at[4 * i + j],
                                             device_id=(x, y, 1 - c), device_id_type=MESH).wait_send()
                pltpu.make_async_remote_copy(src_ref=bufs[i].at[2 * px + py, 1 - c], dst_ref=bufs[i].at[2 * px + py, 1 - c],
                                             send_sem=send_r.at[4 * i + j], recv_sem=recv_r.at[4 * i + j],
                                             device_id=(x, y, 1 - c), device_id_type=MESH).wait_recv()

    res = pl.pallas_call(
        body, name=name, out_shape=[pltpu.HBM(a.shape, a.dtype) for a in list(arrs) + list(lands)],
        in_specs=[HBM_SPEC] * (2 * n) + [SEM_SPEC, SEM_SPEC, pl.BlockSpec(memory_space=pl.ANY)], out_specs=[HBM_SPEC] * (2 * n),
        input_output_aliases={i: i for i in range(2 * n)},
        compiler_params=pltpu.CompilerParams(has_side_effects=DATAFLOW_EFFECT),
    )(*arrs, *lands, send_sems, recv_sems, after)
    return list(res[:n]), list(res[n:])


def _ag_finish(arrs, lands, name, js=(0, 1, 2), own=True):
    send_sems, recv_sems, arrs, lands, token = _ag_finish_start(arrs, lands, name + "_start", js, own)
    return _ag_finish_wait(send_sems, recv_sems, arrs, lands, token, name + "_wait", js, own)


def _swap_start(gs, name):
    n = len(gs)

    def body(*refs):
        ins, lands = refs[:n], refs[n:2 * n]
        send_sems, recv_sems = refs[2 * n:2 * n + 2]
        token = refs[-1]
        x, y, c = _position()
        _sibling_handshake(x, y, c)
        for i in range(n):
            for p in range(4):
                pltpu.make_async_remote_copy(src_ref=ins[i].at[p, 1 - c], dst_ref=lands[i].at[p], send_sem=send_sems.at[4 * i + p],
                                             recv_sem=recv_sems.at[4 * i + p], device_id=(x, y, 1 - c), device_id_type=MESH).start()
        token[...] = jnp.zeros_like(token)

    lands = [lax.empty((4,) + g.shape[2:], F32) for g in gs]
    res = pl.pallas_call(
        body, name=name,
        out_shape=[pltpu.SemaphoreType.DMA((4 * n,)), pltpu.SemaphoreType.DMA((4 * n,))]
        + [pltpu.HBM(a.shape, F32) for a in list(gs) + lands] + [SDS((8, 128), F32)],
        in_specs=[HBM_SPEC] * (2 * n), out_specs=[SEM_SPEC, SEM_SPEC] + [HBM_SPEC] * (2 * n) + [pl.BlockSpec(memory_space=pltpu.VMEM)],
        input_output_aliases={i: 2 + i for i in range(2 * n)},
        compiler_params=pltpu.CompilerParams(has_side_effects=DATAFLOW_EFFECT, collective_id=SIBLING_BARRIER),
    )(*[pltpu.with_memory_space_constraint(a, pltpu.HBM) for a in list(gs) + lands])
    return res[0], res[1], list(res[2:2 + n]), list(res[2 + n:2 + 2 * n]), res[-1]


def _swap_wait(send_sems, recv_sems, gs, lands, after, name):
    n = len(gs)

    def body(*refs):
        ins, lands_r = refs[:n], refs[n:2 * n]
        send_r, recv_r = refs[2 * n:2 * n + 2]
        x, y, c = _position()
        for i in range(n):
            for p in range(4):
                cp = pltpu.make_async_remote_copy(src_ref=ins[i].at[p, 1 - c], dst_ref=lands_r[i].at[p], send_sem=send_r.at[4 * i + p],
                                                  recv_sem=recv_r.at[4 * i + p], device_id=(x, y, 1 - c), device_id_type=MESH)
                cp.wait_send()
                cp.wait_recv()

    res = pl.pallas_call(
        body, name=name, out_shape=[pltpu.HBM(a.shape, F32) for a in list(gs) + list(lands)],
        in_specs=[HBM_SPEC] * (2 * n) + [SEM_SPEC, SEM_SPEC, pl.BlockSpec(memory_space=pl.ANY)], out_specs=[HBM_SPEC] * (2 * n),
        input_output_aliases={i: i for i in range(2 * n)},
        compiler_params=pltpu.CompilerParams(has_side_effects=DATAFLOW_EFFECT),
    )(*gs, *lands, send_sems, recv_sems, after)
    return list(res[:n]), list(res[n:])


def _exchange_start(s1s, name):
    n = len(s1s)

    def body(*refs):
        srcs, lands = refs[:n], refs[n:2 * n]
        send_sems, recv_sems = refs[2 * n:2 * n + 2]
        token = refs[-1]
        x, y, c = _position()
        for i in range(n):
            for j, (px, py) in enumerate(_other_chips(x, y)):
                pltpu.make_async_remote_copy(src_ref=srcs[i].at[2 * px + py], dst_ref=lands[i].at[j], send_sem=send_sems.at[3 * i + j],
                                             recv_sem=recv_sems.at[3 * i + j], device_id=(px, py, c), device_id_type=MESH).start()
        token[...] = jnp.zeros_like(token)

    lands = [lax.empty((3,) + s.shape[1:], F32) for s in s1s]
    res = pl.pallas_call(
        body, name=name,
        out_shape=[pltpu.SemaphoreType.DMA((3 * n,)), pltpu.SemaphoreType.DMA((3 * n,))]
        + [pltpu.HBM(a.shape, F32) for a in list(s1s) + lands] + [SDS((8, 128), F32)],
        in_specs=[HBM_SPEC] * (2 * n), out_specs=[SEM_SPEC, SEM_SPEC] + [HBM_SPEC] * (2 * n) + [pl.BlockSpec(memory_space=pltpu.VMEM)],
        input_output_aliases={i: 2 + i for i in range(2 * n)},
        compiler_params=pltpu.CompilerParams(has_side_effects=DATAFLOW_EFFECT),
    )(*[pltpu.with_memory_space_constraint(a, pltpu.HBM) for a in list(s1s) + lands])
    return res[0], res[1], list(res[2:2 + n]), list(res[2 + n:2 + 2 * n]), res[-1]


def _exchange_wait(send_sems, recv_sems, s1s, lands, after, name):
    n = len(s1s)

    def body(*refs):
        srcs, lands_r = refs[:n], refs[n:2 * n]
        send_r, recv_r = refs[2 * n:2 * n + 2]
        x, y, c = _position()
        for i in range(n):
            for j, (px, py) in enumerate(_other_chips(x, y)):
                cp = pltpu.make_async_remote_copy(src_ref=srcs[i].at[2 * px + py], dst_ref=lands_r[i].at[j], send_sem=send_r.at[3 * i + j],
                                                  recv_sem=recv_r.at[3 * i + j], device_id=(px, py, c), device_id_type=MESH)
                cp.wait_send()
                cp.wait_recv()

    res = pl.pallas_call(
        body, name=name, out_shape=[pltpu.HBM(a.shape, F32) for a in list(s1s) + list(lands)],
        in_specs=[HBM_SPEC] * (2 * n) + [SEM_SPEC, SEM_SPEC, pl.BlockSpec(memory_space=pl.ANY)], out_specs=[HBM_SPEC] * (2 * n),
        input_output_aliases={i: i for i in range(2 * n)},
        compiler_params=pltpu.CompilerParams(has_side_effects=DATAFLOW_EFFECT),
    )(*s1s, *lands, send_sems, recv_sems, after)
    return list(res[:n]), list(res[n:])


def _join_halves(f2s, name):
    n = len(f2s)
    hbm = pl.BlockSpec(memory_space=pl.ANY)

    def body(*refs):
        ins, outs = refs[:n], refs[n:2 * n]
        send_sems, recv_sems = refs[2 * n:]
        x, y, c = _position()
        _sibling_handshake(x, y, c)
        cps = []
        for i in range(n):
            cp = pltpu.make_async_remote_copy(src_ref=ins[i].at[c], dst_ref=outs[i].at[c], send_sem=send_sems.at[i],
                                              recv_sem=recv_sems.at[i], device_id=(x, y, 1 - c), device_id_type=MESH)
            cp.start()
            cps.append(cp)
        for i in range(n):
            pltpu.make_async_remote_copy(src_ref=ins[i].at[1 - c], dst_ref=outs[i].at[1 - c], send_sem=send_sems.at[i],
                                         recv_sem=recv_sems.at[i], device_id=(x, y, 1 - c), device_id_type=MESH).wait_recv()
        for cp in cps:
            cp.wait_send()

    return pl.pallas_call(
        body, name=name, in_specs=[hbm] * n, out_specs=[hbm] * n, out_shape=[SDS(f.shape, f.dtype) for f in f2s],
        input_output_aliases={i: i for i in range(n)},
        scratch_shapes=[pltpu.SemaphoreType.DMA((n,)), pltpu.SemaphoreType.DMA((n,))],
        compiler_params=pltpu.CompilerParams(collective_id=SIBLING_BARRIER))(*f2s)


def _sum_tile(rows, cols):
    return _row_tile(rows, max(8, (2 ** 19 // cols) // 8 * 8))


def _add_pair(g, r1, c, name):
    _, R, C = r1.shape
    tr = _sum_tile(R, C)

    def body(c_ref, a_ref, b_ref, o_ref):
        del c_ref
        o_ref[...] = a_ref[...] + b_ref[...]

    blk = pl.BlockSpec((None, tr, C), lambda p, i, cr: (p, i, 0))
    return pl.pallas_call(
        body, name=name,
        grid_spec=pltpu.PrefetchScalarGridSpec(
            num_scalar_prefetch=1, grid=(4, R // tr),
            in_specs=[pl.BlockSpec((None, None, tr, C), lambda p, i, cr: (p, cr[0], i, 0)), blk], out_specs=blk),
        out_shape=SDS((4, R, C), F32))(c, g, r1)


def _add_four(s1, r2, me_c, name):
    _, R, C = s1.shape
    tr = _sum_tile(R, C)

    def body(m_ref, a_ref, b_ref, o_ref):
        del m_ref
        o_ref[...] = ((a_ref[...] + b_ref[0]) + b_ref[1]) + b_ref[2]

    return pl.pallas_call(
        body, name=name,
        grid_spec=pltpu.PrefetchScalarGridSpec(
            num_scalar_prefetch=1, grid=(R // tr,),
            in_specs=[pl.BlockSpec((None, tr, C), lambda i, mr: (mr[0], i, 0)), pl.BlockSpec((3, tr, C), lambda i, mr: (0, i, 0))],
            out_specs=pl.BlockSpec((None, tr, C), lambda i, mr: (mr[1], i, 0))),
        out_shape=SDS((2, R, C), F32))(me_c, s1, r2)


def _all_reduce_small(vs):
    n = len(vs)

    def body(*refs):
        ins, outs, bufs = refs[:n], refs[n:2 * n], refs[2 * n:3 * n]
        send_sems, recv_sems = refs[3 * n:]
        x, y, c = _position()
        me = 4 * x + 2 * y + c
        for i in range(n):
            bufs[i][me] = ins[i][...]
        cps = []
        for k in range(1, 8):
            to = (1 - x if k & 4 else x, 1 - y if k & 2 else y, 1 - c if k & 1 else c)
            for i in range(n):
                cp = pltpu.make_async_remote_copy(src_ref=bufs[i].at[me], dst_ref=bufs[i].at[me], send_sem=send_sems.at[7 * i + k - 1],
                                                  recv_sem=recv_sems.at[7 * i + k - 1], device_id=to, device_id_type=MESH)
                cp.start()
                cps.append(cp)
        for cp in cps:
            cp.wait_send()
        for k in range(1, 8):
            src = 4 * (1 - x if k & 4 else x) + 2 * (1 - y if k & 2 else y) + (1 - c if k & 1 else c)
            for i in range(n):
                pltpu.make_async_remote_copy(src_ref=bufs[i].at[src], dst_ref=bufs[i].at[src], send_sem=send_sems.at[7 * i + k - 1],
                                             recv_sem=recv_sems.at[7 * i + k - 1], device_id=(x, y, c), device_id_type=MESH).wait_recv()
        for i in range(n):
            acc = bufs[i][0]
            for k in range(1, 8):
                acc = acc + bufs[i][k]
            outs[i][...] = acc

    vm = pl.BlockSpec(memory_space=pltpu.VMEM)
    return pl.pallas_call(
        body, name="all_reduce_small", in_specs=[vm] * n, out_specs=[vm] * n, out_shape=[SDS(v.shape, F32) for v in vs],
        scratch_shapes=[pltpu.VMEM((8,) + v.shape, F32) for v in vs]
        + [pltpu.SemaphoreType.DMA((7 * n,)), pltpu.SemaphoreType.DMA((7 * n,))])(*vs)


def _reduce_begin(grads, tag):
    g4 = [g.reshape(4, 2, g.shape[1] // 2, g.shape[2]) for g in grads]
    send_sems, recv_sems, g4, lands, token = _swap_start(g4, "rs_swap_start_" + tag)
    return (send_sems, recv_sems, g4, lands), token


def _reduce_mid(state, after, tag):
    _, _, c = _position()
    cs = jnp.reshape(c, (1,)).astype(jnp.int32)
    send_sems, recv_sems, g4, lands = state
    g4, r1 = _swap_wait(send_sems, recv_sems, g4, lands, after, "rs_swap_wait_" + tag)
    s1 = [_add_pair(g, r, cs, f"rs_add_pair_{tag}{i}") for i, (g, r) in enumerate(zip(g4, r1))]
    send_sems, recv_sems, s1, lands, token = _exchange_start(s1, "rs_exchange_start_" + tag)
    return (send_sems, recv_sems, s1, lands), token


def _reduce_end(state, after, tag):
    x, y, c = _position()
    send_sems, recv_sems, s1, lands = state
    s1, lands = _exchange_wait(send_sems, recv_sems, s1, lands, after, "rs_exchange_wait_" + tag)
    me_c = jnp.stack([2 * x + y, c]).astype(jnp.int32)
    f2 = [_add_four(s, l, me_c, f"rs_add_four_{tag}{i}") for i, (s, l) in enumerate(zip(s1, lands))]
    return [f.reshape(2 * f.shape[1], f.shape[2]) for f in _join_halves(f2, "rs_join_" + tag)]


def _halves(a):
    return a.reshape((2, a.shape[0] // 2) + a.shape[1:])


def _after(a, token):
    return a + token[0, 0]


def _in_proj_own(x, nw, w, part):
    M, K = x.shape
    n = w.shape[2]
    sub = 256

    def body(p_ref, x_ref, nw_ref, w_ref, xn_ref, z_ref):
        del p_ref
        for r0 in range(0, MM_TM, sub):
            rows = pl.ds(r0, sub)
            for c0 in range(r0, r0 + sub, NORM_RC):
                ch = pl.ds(c0, NORM_RC)
                xv = x_ref[ch, :]
                xn_ref[ch, :] = (xv * lax.rsqrt(jnp.mean(xv * xv, axis=-1, keepdims=True) + RMS_EPS) * nw_ref[...]).astype(BF16)
            z_ref[rows, :] = jnp.dot(xn_ref[rows, :], w_ref[...], preferred_element_type=F32).astype(BF16)

    row = pl.BlockSpec((MM_TM, K), lambda i, pr: (i, 0))
    return pl.pallas_call(
        body, name="in_proj_own",
        grid_spec=pltpu.PrefetchScalarGridSpec(
            num_scalar_prefetch=1, grid=(M // MM_TM,),
            in_specs=[row, pl.BlockSpec((1, K), lambda i, pr: (0, 0)), pl.BlockSpec((None, K, n), lambda i, pr: (0, 0, 0))],
            out_specs=[row, pl.BlockSpec((MM_TM, n), lambda i, pr: (i, pr[0]))]),
        out_shape=[SDS((M, K), BF16), SDS((M, 4 * n), BF16)])(part, x, nw, w)


def _in_proj_parts(xn, w, parts, z, name):
    M, K = xn.shape
    _, _, n = w.shape
    P = parts.shape[0]

    def body(p_ref, a_ref, b_ref, z_in, o_ref):
        del p_ref, z_in
        o_ref[...] = jnp.dot(a_ref[...], b_ref[...], preferred_element_type=F32).astype(BF16)

    return pl.pallas_call(
        body, name=name,
        grid_spec=pltpu.PrefetchScalarGridSpec(
            num_scalar_prefetch=1, grid=(P, M // MM_TM),
            in_specs=[pl.BlockSpec((MM_TM, K), lambda g, i, pr: (i, 0)), pl.BlockSpec((None, K, n), lambda g, i, pr: (pr[g], 0, 0)),
                      pl.BlockSpec(memory_space=pl.ANY)],
            out_specs=pl.BlockSpec((MM_TM, n), lambda g, i, pr: (i, pr[g]))),
        out_shape=SDS(z.shape, BF16), input_output_aliases={3: 0})(parts, xn, w, z)


def _local_step(x, mem, target, sp, ex):
    S, D = x.shape
    band, buckets = _bias_static()
    buckets = jnp.asarray(buckets)

    tok = ex.start_first()
    own, me, near, far = ex.own_w_in()
    xn, z = _in_proj_own(x, _after(sp["attn_norm_w"], tok), own, me)
    z = _in_proj_parts(xn, ex.near_weights(after=z), near, z, "in_proj_near")
    w_in, gathered_small = ex.first_weights(after=z)
    sp = {**sp, **gathered_small}
    n_in = 4 * w_in.shape[2]
    bw = {"w_in": w_in}
    z = _in_proj_parts(xn, w_in, far + ex.start_rest()[0, 0].astype(jnp.int32), z, "in_proj_far")
    qh, kh, vh, q2, k2, v2 = _qkv_prep(z, sp["q_norm_w"], sp["k_norm_w"])
    tab = sp["rel_bias_table"].T.reshape(N_GROUPS, HPG, N_BUCKETS)
    bias = _bias_fwd(jnp.pad(tab, ((0, 0), (0, 8 - HPG), (0, 0))), buckets)
    biasm = jnp.where(jnp.asarray(band)[None, None], bias[:, :HPG].reshape(N_GROUPS, HPG, NQ, 2 * NQ), NEG)
    seqs = lambda a: a.reshape(-1, S // P4, HEAD)
    q12, k12, v12 = seqs(q2), seqs(k2), seqs(v2)
    groups = ((qh, kh, vh, 1, 0, HPG, 0), (q12, k12, v12, 1, 0, HPG * P4, 2), (q12, k12, v12, 4, HPG * P4, HPG * P4, 2))
    os_, lses = [], []
    for g, (qg, kg, vg, dil, head0, nseq, shift) in enumerate(groups):
        o_g, lse_g = _attn_fwd(qg, kg, vg, biasm[g], f"attn_fwd_g{g}", dil, head0, nseq, shift)
        os_.append(o_g.reshape(HPG, -1, o_g.shape[1], HEAD) if g > 0 else o_g)
        lses.append(lse_g.reshape(HPG, -1, lse_g.shape[1], HEAD) if g > 0 else lse_g)
    attn = _merge_fwd(os_, lses)
    tok = ex.rest_arrived(after=attn)
    u1, u3 = _conv_fwd(z, sp["conv_dw_w"], sp["conv_dw_b"], _after(sp["conv_ln_w"], tok), sp["conv_ln_b"])
    bw.update(ex.rest_weights(after=u3))
    F2 = 4 * bw["w_up"].shape[2]
    mn, kv, mk, mv = _memkv_fwd(mem, sp["mem_norm_w"], bw["w_mem_kv"], sp["xk_norm_w"])
    oc = _cross_fwd(z, sp["xq_norm_w"], mk, mv)
    h1, hn = _outproj_fwd(x, z, sp["b_gate"], attn, u3, oc, bw["w_attn_o"], bw["w_conv_o"], bw["w_cross_o"], bw["w_out"],
                          sp["ffn_norm_w"])
    up = _mm_nn(hn, bw["w_up"], BF16, "ffn_up")
    dy, loss_tile, ac, gc = _ffn_fwd(up, sp["ffn_conv_w"], sp["ffn_conv_b"], bw["w_down"], h1, target)

    gs, gb = {}, {}
    dca, dcg, acca, accg, gb["w_down"] = _ffn_bwd_a(dy, bw["w_down"], ac, gc)
    cols = lambda acc: jnp.sum(acc, axis=1).reshape(1, F2 // 2)
    gs["ffn_conv_b"] = jnp.concatenate([cols(acca), cols(accg)], axis=1)
    dup, tacc = _ffn_bwd_b(dca, dcg, up, sp["ffn_conv_w"])
    gs["ffn_conv_w"] = jnp.transpose(jnp.sum(tacc, axis=2), (1, 0, 2)).reshape(FFN_K, F2)
    gb["w_up"] = _mm_tn(hn, dup, 4, "dw_up")
    tok = ex.reduce_begin("a", ("w_down", "w_up"), gb)
    dh1, gs["ffn_norm_w"] = _norm_in_bwd(dup, bw["w_up"], h1, _after(sp["ffn_norm_w"], tok), dy, "ffn_in_bwd")
    tok = ex.reduce_mid("a", after=gs["ffn_norm_w"])
    dz, dattn, du3, doc, gs["b_gate"], gb["w_out"], gb["w_attn_o"], gb["w_conv_o"], gb["w_cross_o"] = _outproj_bwd(
        dh1, z, _after(sp["b_gate"], tok), attn, u3, oc, bw["w_attn_o"], bw["w_conv_o"], bw["w_cross_o"], bw["w_out"], n_in)
    dz, dmk, dmv, gs["xq_norm_w"] = _cross_bwd(dz, doc, z, sp["xq_norm_w"], mk, mv)
    gb["w_mem_kv"], gs["xk_norm_w"], gs["mem_norm_w"] = _memkv_bwd(
        dmk, dmv, kv, mem, mn, sp["mem_norm_w"], bw["w_mem_kv"], sp["xk_norm_w"])
    ex.reduce_end("a", after=gs["mem_norm_w"])
    tok = ex.reduce_begin("b", ("w_out", "w_attn_o", "w_conv_o", "w_cross_o", "w_mem_kv"), gb)
    du1, cacc = _conv_bwd_a(du3, u1, z, _after(sp["conv_ln_w"], tok), sp["conv_ln_b"])
    gs["conv_dw_w"], gs["conv_dw_b"] = cacc[:CONV_K], cacc[32:33]
    gs["conv_ln_w"], gs["conv_ln_b"] = cacc[33:34], cacc[34:35]
    tok = ex.reduce_mid("b", after=cacc)
    dz = _conv_bwd_b(dz, du1, z, _after(sp["conv_dw_w"], tok))
    merged_grads = _merge_bwd(dattn, os_, lses)
    dqs, dks, dvs, dsbs = [], [], [], []
    for g, (qg, kg, vg, dil, head0, nseq, shift) in enumerate(groups):
        wg_g, da_g, dh_g = merged_grads[g]
        lse_g = lses[g]
        if g > 0:
            wg_g, da_g, dh_g, lse_g = seqs(wg_g), seqs(da_g), seqs(dh_g), seqs(lse_g)
        dq_g, dk_g, dv_g, dsb_g = _attn_bwd(qg, kg, vg, biasm[g], da_g, wg_g, dh_g, lse_g, f"attn_bwd_g{g}", dil, head0, nseq, shift)
        if g > 0:
            dq_g, dk_g, dv_g = (t.reshape(HPG, P4, S // P4, HEAD) for t in (dq_g, dk_g, dv_g))
            dsb_g = jnp.sum(dsb_g.reshape(HPG, P4, NQ, 2 * NQ), axis=1)
        dqs.append(dq_g)
        dks.append(dk_g)
        dvs.append(dv_g)
        dsbs.append(dsb_g.reshape(HPG, NQ * 2 * NQ))
    dtab = _bias_bwd(jnp.pad(jnp.stack(dsbs), ((0, 0), (0, 8 - HPG), (0, 0))), buckets)
    gs["rel_bias_table"] = dtab[:, :HPG].reshape(N_GROUPS * HPG, N_BUCKETS).T
    dz, gs["q_norm_w"], gs["k_norm_w"] = _qkv_bwd(dz, z, dqs, dks, dvs, sp["q_norm_w"], sp["k_norm_w"])
    ex.reduce_end("b", after=gs["q_norm_w"])
    gb["w_in"] = _mm_tn(xn, dz, 4, "dw_in")
    tok = ex.reduce_mid("c", after=ex.reduce_begin("c", ("w_in",), gb))
    dx, gs["attn_norm_w"] = _norm_in_bwd(dz, bw["w_in"], x, _after(sp["attn_norm_w"], tok), dh1, "in_bwd")
    ex.reduce_end("c", after=gs["attn_norm_w"])
    return loss_tile, dx, gs, gb


SMALL = ("rel_bias_table", "attn_norm_w", "b_gate", "q_norm_w", "k_norm_w", "conv_dw_w", "conv_dw_b", "conv_ln_w", "conv_ln_b",
         "mem_norm_w", "xq_norm_w", "xk_norm_w", "ffn_norm_w", "ffn_conv_w", "ffn_conv_b")
SMALL_SHARDED = ("conv_dw_w", "ffn_conv_w")
BIG_COL = ("w_in", "w_attn_o", "w_conv_o", "w_cross_o", "w_up")
BIG_ROW = ("w_mem_kv", "w_out", "w_down")
BIG = BIG_COL + BIG_ROW
WEIGHTS = ("rel_bias_table", "attn_norm_w", "w_in", "b_gate", "q_norm_w", "k_norm_w", "w_attn_o", "conv_dw_w", "conv_dw_b",
           "conv_ln_w", "conv_ln_b", "w_conv_o", "mem_norm_w", "w_mem_kv", "xq_norm_w", "xk_norm_w", "w_cross_o", "w_out",
           "ffn_norm_w", "w_up", "ffn_conv_w", "ffn_conv_b", "w_down")


class _Exchanges:
    REST = tuple(k for k in BIG if k != "w_in")

    def __init__(self, w):
        self.w = w
        self.pending = {}
        self.reduced = {}

    def _whole(self, k, ga):
        ga = ga.reshape((4,) + self.w[k].shape)
        return ga if k in BIG_COL else ga.reshape((4 * self.w[k].shape[0],) + self.w[k].shape[1:])

    def start_first(self):
        self.w_in_local = self.w["w_in"].astype(BF16)
        local = [_halves(self.w_in_local)]
        for k in SMALL_SHARDED:
            flat = jnp.ravel(self.w[k])
            local.append(jnp.pad(flat, (0, (-flat.shape[0]) % 2048)).reshape(2, -1, 128))
        near = _ag_ici_start(local, "gather_near_start", js=(0, 1))
        far = _ag_ici_start(near[2], "gather_far_start", js=(2,), lands=near[3])
        self.pending["first"] = (near[:2], far[:2], far[2], far[3])
        return far[4]

    def own_w_in(self):
        x, y, _ = _position()
        me = 2 * x + y
        ids = lambda *v: jnp.stack(v).astype(jnp.int32)
        return self.w_in_local[None], ids(me), ids(me ^ 2, me ^ 1), ids(me ^ 3)

    def near_weights(self, after):
        near, far, ins, lands = self.pending.pop("first")
        ins, lands = _ag_ici_wait(*near, ins, lands, after, "gather_near_wait", js=(0, 1))
        ins, lands = _ag_finish(ins, lands, "gather_near_finish", js=(0, 1))
        self.pending["first"] = (far, ins, lands)
        return self._whole("w_in", lands[0])

    def first_weights(self, after):
        far, ins, lands = self.pending.pop("first")
        ins, lands = _ag_ici_wait(*far, ins, lands, after, "gather_far_wait", js=(2,))
        _, gathered = _ag_finish(ins, lands, "gather_far_finish", js=(2,), own=False)
        self.first = gathered[0]
        small = {}
        for k, ga in zip(SMALL_SHARDED, gathered[1:]):
            r, cdim = self.w[k].shape
            parts = ga.reshape(4, -1)[:, :r * cdim].reshape(4, r, cdim)
            small[k] = jnp.transpose(parts, (1, 0, 2)).reshape(r, 4 * cdim)
        return self._whole("w_in", gathered[0]), small

    def start_rest(self):
        local = [_halves(self.w[k].astype(BF16)) for k in self.REST]
        local, _ = lax.optimization_barrier((local, self.first))
        send_sems, recv_sems, ins, lands, token = _ag_ici_start(local, "gather_rest_start")
        self.pending["rest"] = (send_sems, recv_sems, ins, lands)
        return token

    def rest_arrived(self, after):
        send_sems, recv_sems, ins, lands = self.pending.pop("rest")
        ins, lands = _ag_ici_wait(send_sems, recv_sems, ins, lands, after, "gather_rest_wait")
        send_sems, recv_sems, ins, lands, token = _ag_finish_start(ins, lands, "gather_rest_finish_start")
        self.pending["rest"] = (send_sems, recv_sems, ins, lands)
        return token

    def rest_weights(self, after):
        _, gathered = _ag_finish_wait(*self.pending.pop("rest"), after, "gather_rest_finish_wait")
        return {k: self._whole(k, ga) for k, ga in zip(self.REST, gathered)}

    def reduce_begin(self, tag, names, gb):
        parts = [gb[k].reshape((4,) + self.w[k].shape) for k in names]
        state, token = _reduce_begin(parts, tag)
        self.pending[tag] = (state, names)
        return token

    def reduce_mid(self, tag, after):
        state, names = self.pending.pop(tag)
        state, token = _reduce_mid(state, after, tag)
        self.pending[tag] = (state, names)
        return token

    def reduce_end(self, tag, after):
        state, names = self.pending.pop(tag)
        self.reduced.update(zip(names, _reduce_end(state, after, tag)))


def _step(x, mem, target, w, m, v):
    xi, yi, _ = _position()
    shard = 2 * xi + yi
    ex = _Exchanges(w)
    sp = {k: w[k] for k in SMALL if k not in SMALL_SHARDED}
    loss_tile, dx, gs, _ = _local_step(x, mem, target, sp, ex)
    g_big = ex.reduced

    red = _all_reduce_small([loss_tile] + [gs[k] for k in SMALL])
    loss = red[0][0, 0]
    g_small = dict(zip(SMALL, red[1:]))
    for k in SMALL_SHARDED:
        cdim = w[k].shape[1]
        g_small[k] = lax.dynamic_slice_in_dim(g_small[k], shard * cdim, cdim, axis=1)

    grads, delta, new_m, new_v = {}, {}, {}, {}
    for k in BIG:
        grads[k] = g_big[k]
        delta[k], new_m[k], new_v[k] = _adamw(w[k], g_big[k], m[k], v[k], "adamw_" + k)
    outs = _adamw_small([w[k] for k in SMALL], [g_small[k] for k in SMALL], [m[k] for k in SMALL], [v[k] for k in SMALL])
    for dst, vals in zip((delta, new_m, new_v), outs):
        dst.update(zip(SMALL, vals))
    grads.update(g_small)
    return loss, dx, grads, delta, new_m, new_v


def kernel(x, mem, rel_bias_table, attn_norm_w, w_in, b_gate, q_norm_w, k_norm_w, w_attn_o, conv_dw_w, conv_dw_b, conv_ln_w, conv_ln_b, w_conv_o, mem_norm_w, w_mem_kv, xq_norm_w, xk_norm_w, w_cross_o, w_out, ffn_norm_w, w_up, ffn_conv_w, ffn_conv_b, w_down, loss_target, m_rel_bias_table, m_attn_norm_w, m_w_in, m_b_gate, m_q_norm_w, m_k_norm_w, m_w_attn_o, m_conv_dw_w, m_conv_dw_b, m_conv_ln_w, m_conv_ln_b, m_w_conv_o, m_mem_norm_w, m_w_mem_kv, m_xq_norm_w, m_xk_norm_w, m_w_cross_o, m_w_out, m_ffn_norm_w, m_w_up, m_ffn_conv_w, m_ffn_conv_b, m_w_down, v_rel_bias_table, v_attn_norm_w, v_w_in, v_b_gate, v_q_norm_w, v_k_norm_w, v_w_attn_o, v_conv_dw_w, v_conv_dw_b, v_conv_ln_w, v_conv_ln_b, v_w_conv_o, v_mem_norm_w, v_w_mem_kv, v_xq_norm_w, v_xk_norm_w, v_w_cross_o, v_w_out, v_ffn_norm_w, v_w_up, v_ffn_conv_w, v_ffn_conv_b, v_w_down):
    args = locals()
    def block(name, k):
        a = args[name] if k == "rel_bias_table" else args[name][0]
        return a.reshape(1, -1) if a.ndim == 1 else a

    w = {k: block(k, k) for k in WEIGHTS}
    m = {k: block("m_" + k, k) for k in WEIGHTS}
    v = {k: block("v_" + k, k) for k in WEIGHTS}
    loss, dx, grads, delta, new_m, new_v = _step(x[0], mem[0], loss_target[0], w, m, v)
    out = [loss, dx[None]]
    for d in (grads, delta, new_m, new_v):
        for k in WEIGHTS:
            out.append(d[k].reshape(args[k].shape))
    return tuple(out)
```

```python
import math

import numpy as np
import jax
import jax.numpy as jnp
from jax import lax
from jax.experimental import pallas as pl
from jax.experimental.pallas import tpu as pltpu

F32, BF16 = jnp.float32, jnp.bfloat16
SDS = jax.ShapeDtypeStruct
MESH = pl.DeviceIdType.MESH

HEAD = 128
N_GROUPS, HPG = 3, 4
ATTN_GROUPS = ((128, 1), (512, 4), (2048, 16))
NQ = 128
QKV_W = N_GROUPS * HPG * HEAD
CH = 512
CONV_K, FFN_K = 31, 3
N_BUCKETS, MAX_DIST = 32, 2048
RMS_EPS, LN_EPS = 1e-6, 1e-5
O_Q, O_K, O_V, O_CV, O_CG, O_XQ, O_G = 0, QKV_W, 2 * QKV_W, 3 * QKV_W, 3 * QKV_W + CH, 3 * QKV_W + 2 * CH, 3 * QKV_W + 3 * CH
ADAM_LR, ADAM_B1, ADAM_B2, ADAM_EPS, ADAM_WD, ADAM_STEP = 0.001, 0.9, 0.999, 1e-08, 0.01, 10
NEG = -1e30
SCALE = HEAD ** -0.5
TM = 512
MM_TM = 1024
ATT_RB = 2048
NT_DIMS = (((1,), (1,)), ((), ()))
TN_DIMS = (((0,), (0,)), ((), ()))


CONV_RC = 32
CONV_LB = 256
CONV_LANES = tuple(slice(l, l + CONV_LB) for l in range(0, CH, CONV_LB))
CONV_HALO = 32
FFN_RC = 32


def _conv_taps(buf, base, lanes, q_lo, q_hi, visit):
    span = CONV_RC + CONV_HALO
    xx = buf[pl.ds(base, span), lanes]
    for s in range(8):
        xs = xx if s == 0 else pltpu.roll(xx, span - s, 0)
        for q in range(s, q_hi + 1, 8):
            if q >= q_lo:
                visit(q, xs[q - s:q - s + CONV_RC])


def _sig(v):
    return 0.5 * jnp.tanh(0.5 * v) + 0.5


def _fold8(v):
    acc = v[0:8]
    for r in range(8, v.shape[0], 8):
        acc = acc + v[r:r + 8]
    return acc


def _row_tile(rows, cap, mult=8):
    best = None
    for t in range(mult, min(rows, cap) + 1, mult):
        if rows % t == 0:
            best = t
    return best if best is not None else rows


def _full(shape):
    n = len(shape)
    return pl.BlockSpec(shape, lambda *a: (0,) * n)


def _mm_nn(a, b, out_dtype, name):
    M, K = a.shape
    G, _, n = b.shape

    def body(a_ref, b_ref, o_ref):
        o_ref[...] = jnp.dot(a_ref[...].astype(BF16), b_ref[...], preferred_element_type=F32).astype(out_dtype)

    return pl.pallas_call(
        body, name=name, grid=(G, M // MM_TM),
        in_specs=[pl.BlockSpec((MM_TM, K), lambda g, i: (i, 0)), pl.BlockSpec((None, K, n), lambda g, i: (g, 0, 0))],
        out_specs=pl.BlockSpec((MM_TM, n), lambda g, i: (i, g)),
        out_shape=SDS((M, G * n), out_dtype))(a, b)


def _mm_tn(a, b, G, name):
    S, Ka = a.shape
    n = b.shape[1] // G
    tka = Ka
    while tka * n * 4 > 10 * 2 ** 20 and tka % 256 == 0:
        tka //= 2

    def body(a_ref, b_ref, o_ref):
        @pl.when(pl.program_id(2) == 0)
        def _():
            o_ref[...] = jnp.zeros_like(o_ref)
        o_ref[...] += lax.dot_general(a_ref[...].astype(BF16), b_ref[...].astype(BF16), TN_DIMS, preferred_element_type=F32)

    return pl.pallas_call(
        body, name=name, grid=(G, Ka // tka, S // MM_TM),
        in_specs=[pl.BlockSpec((MM_TM, tka), lambda g, i, k: (k, i)), pl.BlockSpec((MM_TM, n), lambda g, i, k: (k, g))],
        out_specs=pl.BlockSpec((None, tka, n), lambda g, i, k: (g, i, 0)),
        out_shape=SDS((G, Ka, n), F32))(a, b)


NORM_RC = 16


def _norm_in_bwd(a, w, xin, nw, resid, name):
    S, K = xin.shape
    G, _, n = w.shape
    tm = 1024 if S % 1024 == 0 and G * K * n * 2 <= 12 * 2 ** 20 else TM
    nt = S // tm

    def body(a_ref, w_ref, x_ref, nw_ref, r_ref, o_ref, dnw_ref, acc, part):
        i, g = pl.program_id(0), pl.program_id(1)

        @pl.when((i == 0) & (g == 0))
        def _():
            part[...] = jnp.zeros_like(part)

        @pl.when(g == 0)
        def _():
            acc[...] = jnp.zeros_like(acc)

        acc[...] += lax.dot_general(a_ref[...], w_ref[g], NT_DIMS, preferred_element_type=F32)

        @pl.when(g == G - 1)
        def _():
            for r0 in range(0, tm, NORM_RC):
                rows = pl.ds(r0, NORM_RC)
                dn = acc[rows, :]
                xv = x_ref[rows, :]
                r = lax.rsqrt(jnp.mean(xv * xv, axis=-1, keepdims=True) + RMS_EPS)
                xhat = xv * r
                dyw = dn * nw_ref[...]
                o_ref[rows, :] = r_ref[rows, :] + r * (dyw - xhat * jnp.mean(dyw * xhat, axis=-1, keepdims=True))
                part[...] += _fold8(dn * xhat)

        @pl.when((i == nt - 1) & (g == G - 1))
        def _():
            dnw_ref[...] = jnp.sum(part[...], axis=0, keepdims=True)

    row = pl.BlockSpec((tm, K), lambda i, g: (i, 0))
    return pl.pallas_call(
        body, name=name, grid=(nt, G),
        in_specs=[pl.BlockSpec((tm, n), lambda i, g: (i, g)),
                  pl.BlockSpec((G, K, n), lambda i, g: (0, 0, 0), pipeline_mode=pl.Buffered(1)),
                  row, _full((1, K)), row],
        out_specs=[row, _full((1, K))],
        out_shape=[SDS((S, K), F32), SDS((1, K), F32)],
        scratch_shapes=[pltpu.VMEM((tm, K), F32), pltpu.VMEM((8, K), F32)])(a, w, xin, nw, resid)


def _t5_bucket_np(dist):
    max_exact = N_BUCKETS // 2
    d = np.maximum(dist.astype(np.float32), np.float32(1.0))
    large = max_exact + (np.log(d / np.float32(max_exact)) / np.float32(math.log(MAX_DIST / max_exact))
                         * np.float32(N_BUCKETS - max_exact)).astype(np.int32)
    large = np.minimum(large, N_BUCKETS - 1)
    return np.where(dist < max_exact, dist, large).astype(np.int32)


def _bias_static():
    qi = np.arange(NQ)[:, None]
    kj = np.arange(2 * NQ)[None, :]
    step = qi + NQ - kj
    band = (step >= 0) & (step <= NQ)
    buckets = np.stack([_t5_bucket_np(np.clip(step, 0, None) * dil).reshape(1, -1) for _, dil in ATTN_GROUPS])
    return band, buckets


def _bias_fwd(table_t, buckets):
    nb = buckets.shape[-1]

    def body(t_ref, b_ref, o_ref):
        oh = (b_ref[...] == lax.broadcasted_iota(jnp.int32, (N_BUCKETS, nb), 0)).astype(F32)
        o_ref[...] = jnp.dot(t_ref[...], oh, preferred_element_type=F32, precision=lax.Precision.HIGHEST)

    return pl.pallas_call(
        body, name="bias_fwd", grid=(N_GROUPS,),
        in_specs=[pl.BlockSpec((None, 8, N_BUCKETS), lambda g: (g, 0, 0)), pl.BlockSpec((None, 1, nb), lambda g: (g, 0, 0))],
        out_specs=pl.BlockSpec((None, 8, nb), lambda g: (g, 0, 0)),
        out_shape=SDS((N_GROUPS, 8, nb), F32))(table_t, buckets)


def _bias_bwd(dsb, buckets):
    nb = buckets.shape[-1]

    def body(d_ref, b_ref, o_ref):
        oh = (b_ref[...] == lax.broadcasted_iota(jnp.int32, (N_BUCKETS, nb), 0)).astype(F32)
        o_ref[...] = lax.dot_general(d_ref[...], oh, NT_DIMS, preferred_element_type=F32, precision=lax.Precision.HIGHEST)

    return pl.pallas_call(
        body, name="bias_bwd", grid=(N_GROUPS,),
        in_specs=[pl.BlockSpec((None, 8, nb), lambda g: (g, 0, 0)), pl.BlockSpec((None, 1, nb), lambda g: (g, 0, 0))],
        out_specs=pl.BlockSpec((None, 8, N_BUCKETS), lambda g: (g, 0, 0)),
        out_shape=SDS((N_GROUPS, 8, N_BUCKETS), F32))(dsb, buckets)


P4 = 4


def _to_p4(dst_ref, h, val, scr):
    scr[...] = val
    for r in range(P4):
        dst_ref[h, r] = scr[pl.ds(r, TM // P4, stride=P4), :]


def _from_p4(src_ref, h, scr):
    for r in range(P4):
        scr[pl.ds(r, TM // P4, stride=P4), :] = src_ref[h, r]
    return scr[...]


def _qkv_prep(z, qw, kw):
    S = z.shape[0]
    nh = N_GROUPS * HPG

    def body(zq, zk, zv, qw_ref, kw_ref, qh, kh, vh, q2, k2, v2, scr):
        for h in range(nh):
            g = h // HPG
            sl = slice(h * HEAD, (h + 1) * HEAD)
            xq = zq[:, sl].astype(F32)
            q = xq * lax.rsqrt(jnp.mean(xq * xq, axis=-1, keepdims=True) + RMS_EPS) * qw_ref[g:g + 1, :]
            xk = zk[:, sl].astype(F32)
            k = xk * lax.rsqrt(jnp.mean(xk * xk, axis=-1, keepdims=True) + RMS_EPS) * kw_ref[g:g + 1, :]
            v = zv[:, sl].astype(F32)
            if h < HPG:
                qh[h], kh[h], vh[h] = q, k, v
            else:
                _to_p4(q2, h - HPG, q, scr)
                _to_p4(k2, h - HPG, k, scr)
                _to_p4(v2, h - HPG, v, scr)

    hm = pl.BlockSpec((HPG, TM, HEAD), lambda i: (0, i, 0))
    p4 = pl.BlockSpec((2 * HPG, P4, TM // P4, HEAD), lambda i: (0, 0, i, 0))
    return pl.pallas_call(
        body, name="qkv_prep", grid=(S // TM,),
        in_specs=[pl.BlockSpec((TM, QKV_W), lambda i: (i, 0)), pl.BlockSpec((TM, QKV_W), lambda i: (i, 1)),
                  pl.BlockSpec((TM, QKV_W), lambda i: (i, 2)), _full((N_GROUPS, HEAD)), _full((N_GROUPS, HEAD))],
        out_specs=[hm, hm, hm, p4, p4, p4],
        out_shape=[SDS((HPG, S, HEAD), F32)] * 3 + [SDS((2 * HPG, P4, S // P4, HEAD), F32)] * 3,
        scratch_shapes=[pltpu.VMEM((TM, HEAD), F32)])(z, z, z, qw, kw)


def _rows(start, d):
    return pl.ds(start, NQ) if d == 1 else pl.ds(start, NQ, stride=d)


def _window(buf, ref, RB, start, d, single):
    if not single:
        return buf[_rows(start, d), :]
    at = start - RB
    return ref[_rows(at if at >= 0 else at + NQ * d, d), :]


def _attn_fwd(qh, kh, vh, biasm, name, d, head0, nseq, bias_shift):
    S = qh.shape[1]
    RB = min(ATT_RB, S)
    nbk, nq = S // RB, RB // (NQ * d)

    single = nbk == 1

    def body(q_ref, k_ref, v_ref, bias_ref, o_ref, lse_ref, kbuf, vbuf):
        b = pl.program_id(1)
        if not single:
            @pl.when(b == 0)
            def _():
                kbuf[0:RB, :] = jnp.zeros((RB, HEAD), F32)
                vbuf[0:RB, :] = jnp.zeros((RB, HEAD), F32)

            @pl.when(b > 0)
            def _():
                kbuf[0:RB, :] = kbuf[RB:2 * RB, :]
                vbuf[0:RB, :] = vbuf[RB:2 * RB, :]

            kbuf[RB:2 * RB, :] = k_ref[...]
            vbuf[RB:2 * RB, :] = v_ref[...]
        bias = bias_ref[...]
        col = lax.broadcasted_iota(jnp.int32, (NQ, 2 * NQ), 1)

        for qb in range(nq):
            def unit(r, carry, qb=qb):
                qs = qb * NQ * d + r
                q = q_ref[_rows(qs, d), :].astype(BF16)
                kw = jnp.concatenate([_window(kbuf, k_ref, RB, RB + qs - NQ * d, d, single),
                                      _window(kbuf, k_ref, RB, RB + qs, d, single)], axis=0).astype(BF16)
                vw = jnp.concatenate([_window(vbuf, v_ref, RB, RB + qs - NQ * d, d, single),
                                      _window(vbuf, v_ref, RB, RB + qs, d, single)], axis=0).astype(BF16)
                s = lax.dot_general(q, kw, NT_DIMS, preferred_element_type=F32) * SCALE + bias
                if qb == 0:
                    s = jnp.where((col < NQ) & (b == 0), NEG, s)
                m = jnp.max(s, axis=-1, keepdims=True)
                p = jnp.exp(s - m)
                l = jnp.sum(p, axis=-1, keepdims=True)
                o = jnp.dot(p.astype(BF16), vw, preferred_element_type=F32) / l
                o_ref[_rows(qs, d), :] = o
                lse_ref[_rows(qs, d), :] = jnp.broadcast_to(m + jnp.log(l), (NQ, HEAD))
                return carry

            for r in range(d):
                unit(r, 0)

    blk = lambda f: pl.BlockSpec((None, RB, HEAD), f)
    return pl.pallas_call(
        body, name=name, grid=(nseq, nbk),
        in_specs=[blk(lambda h, b: (head0 + h, b, 0))] * 3
        + [pl.BlockSpec((None, NQ, 2 * NQ), lambda h, b: (jnp.right_shift(h, bias_shift), 0, 0))],
        out_specs=[blk(lambda h, b: (h, b, 0))] * 2,
        out_shape=[SDS((nseq, S, HEAD), F32)] * 2,
        scratch_shapes=[pltpu.VMEM((2 * RB, HEAD), F32)] * 2)(qh, kh, vh, biasm)


def _attn_bwd(qh, kh, vh, biasm, da, wg, dh, lse, name, d, head0, nseq, bias_shift):
    S = qh.shape[1]
    RB = min(ATT_RB, S)
    nbk, nq = S // RB, RB // (NQ * d)
    single = nbk == 1

    def body(q_ref, k_ref, v_ref, bias_ref, da_ref, wg_ref, dh_ref, lse_ref,
             dq_ref, dk_ref, dv_ref, dsb_ref, kbuf, vbuf, dkbuf, dvbuf):
        b = pl.program_id(1)
        zero = jnp.zeros((RB, HEAD), F32)

        @pl.when(b == 0)
        def _():
            if not single:
                kbuf[0:RB, :] = zero
                vbuf[0:RB, :] = zero
            dkbuf[0:RB, :] = zero
            dvbuf[0:RB, :] = zero
            dsb_ref[...] = jnp.zeros_like(dsb_ref)

        @pl.when(b > 0)
        def _():
            kbuf[0:RB, :] = kbuf[RB:2 * RB, :]
            vbuf[0:RB, :] = vbuf[RB:2 * RB, :]
            dkbuf[0:RB, :] = dkbuf[RB:2 * RB, :]
            dvbuf[0:RB, :] = dvbuf[RB:2 * RB, :]

        dkbuf[RB:2 * RB, :] = zero
        dvbuf[RB:2 * RB, :] = zero

        @pl.when(b < nbk)
        def _():
            if not single:
                kbuf[RB:2 * RB, :] = k_ref[...]
                vbuf[RB:2 * RB, :] = v_ref[...]
            bias = bias_ref[...]
            col = lax.broadcasted_iota(jnp.int32, (NQ, 2 * NQ), 1)

            for qb in range(nq):
                def unit(r, carry, qb=qb):
                    qs = qb * NQ * d + r
                    prev, cur = _rows(RB + qs - NQ * d, d), _rows(RB + qs, d)
                    q = q_ref[_rows(qs, d), :].astype(BF16)
                    kw = jnp.concatenate([_window(kbuf, k_ref, RB, RB + qs - NQ * d, d, single),
                                          _window(kbuf, k_ref, RB, RB + qs, d, single)], axis=0).astype(BF16)
                    vw = jnp.concatenate([_window(vbuf, v_ref, RB, RB + qs - NQ * d, d, single),
                                          _window(vbuf, v_ref, RB, RB + qs, d, single)], axis=0).astype(BF16)
                    s = lax.dot_general(q, kw, NT_DIMS, preferred_element_type=F32) * SCALE + bias
                    if qb == 0:
                        s = jnp.where((col < NQ) & (b == 0), NEG, s)
                    p = jnp.exp(s - lse_ref[_rows(qs, d), :][:, 0:1])
                    w = wg_ref[_rows(qs, d), :]
                    do = (da_ref[_rows(qs, d), :] * w).astype(BF16)
                    dp = lax.dot_general(do, vw, NT_DIMS, preferred_element_type=F32)
                    ds = p * (dp - w[:, 0:1] * dh_ref[_rows(qs, d), :][:, 0:1])
                    dsb_ref[...] += ds
                    dsb = ds.astype(BF16)
                    dq_ref[_rows(qs, d), :] = jnp.dot(dsb, kw, preferred_element_type=F32) * SCALE
                    dkw = lax.dot_general(dsb, q, TN_DIMS, preferred_element_type=F32) * SCALE
                    dvw = lax.dot_general(p.astype(BF16), do, TN_DIMS, preferred_element_type=F32)
                    dkbuf[prev, :] += dkw[0:NQ, :]
                    dkbuf[cur, :] += dkw[NQ:2 * NQ, :]
                    dvbuf[prev, :] += dvw[0:NQ, :]
                    dvbuf[cur, :] += dvw[NQ:2 * NQ, :]
                    return carry

                for r in range(d):
                    unit(r, 0)

        dk_ref[...] = dkbuf[done:done + RB, :]
        dv_ref[...] = dvbuf[done:done + RB, :]

    steps, done = (nbk + 1, 0) if nbk > 1 else (1, RB)
    blk = lambda f: pl.BlockSpec((None, RB, HEAD), f)
    cur_g = blk(lambda h, b: (head0 + h, jnp.minimum(b, nbk - 1), 0))
    cur = blk(lambda h, b: (h, jnp.minimum(b, nbk - 1), 0))
    prv = blk(lambda h, b: (h, jnp.maximum(b - 1, 0), 0))
    sq = pl.BlockSpec((None, NQ, 2 * NQ), lambda h, b: (h, 0, 0))
    return pl.pallas_call(
        body, name=name, grid=(nseq, steps),
        in_specs=[cur_g, cur_g, cur_g, pl.BlockSpec((None, NQ, 2 * NQ), lambda h, b: (jnp.right_shift(h, bias_shift), 0, 0)),
                  cur, cur, cur, cur],
        out_specs=[cur, prv, prv, sq],
        out_shape=[SDS((nseq, S, HEAD), F32)] * 3 + [SDS((nseq, NQ, 2 * NQ), F32)],
        scratch_shapes=[pltpu.VMEM((2 * RB, HEAD), F32)] * 4)(qh, kh, vh, biasm, da, wg, dh, lse)


def _merge_weights(l0, l1, l2):
    m = jnp.maximum(jnp.maximum(l0, l1), l2)
    e0, e1, e2 = jnp.exp(l0 - m), jnp.exp(l1 - m), jnp.exp(l2 - m)
    inv = 1.0 / (e0 + e1 + e2)
    return e0 * inv, e1 * inv, e2 * inv


def _merge_fwd(os_, lses):
    S = os_[0].shape[1]

    def body(o0, o1, o2, l0, l1, l2, a_ref, so, sl):
        for h in range(HPG):
            w0, w1, w2 = _merge_weights(l0[h], _from_p4(l1, h, so), _from_p4(l2, h, sl))
            a_ref[:, h * HEAD:(h + 1) * HEAD] = (w0 * o0[h] + w1 * _from_p4(o1, h, so) + w2 * _from_p4(o2, h, sl)).astype(BF16)

    hm = pl.BlockSpec((HPG, TM, HEAD), lambda i: (0, i, 0))
    p4 = pl.BlockSpec((HPG, P4, TM // P4, HEAD), lambda i: (0, 0, i, 0))
    return pl.pallas_call(
        body, name="merge_fwd", grid=(S // TM,), in_specs=[hm, p4, p4, hm, p4, p4],
        out_specs=pl.BlockSpec((TM, CH), lambda i: (i, 0)),
        out_shape=SDS((S, CH), BF16), scratch_shapes=[pltpu.VMEM((TM, HEAD), F32)] * 2)(*os_, *lses)


def _merge_bwd(dattn, os_, lses):
    S = dattn.shape[0]

    def body(da_ref, o0, o1, o2, l0, l1, l2, w0_ref, w1_ref, w2_ref, dah_ref, dh_ref, dah2_ref, dh2_ref, so, sl):
        for h in range(HPG):
            w = _merge_weights(l0[h], _from_p4(l1, h, so), _from_p4(l2, h, sl))
            attn = w[0] * o0[h] + w[1] * _from_p4(o1, h, so) + w[2] * _from_p4(o2, h, sl)
            da = da_ref[:, h * HEAD:(h + 1) * HEAD]
            dh = jnp.broadcast_to(jnp.sum(da * attn, axis=-1, keepdims=True), (TM, HEAD))
            w0_ref[h] = w[0]
            dah_ref[h] = da
            dh_ref[h] = dh
            _to_p4(w1_ref, h, w[1], so)
            _to_p4(w2_ref, h, w[2], so)
            _to_p4(dah2_ref, h, da, so)
            _to_p4(dh2_ref, h, dh, so)

    hm = pl.BlockSpec((HPG, TM, HEAD), lambda i: (0, i, 0))
    p4 = pl.BlockSpec((HPG, P4, TM // P4, HEAD), lambda i: (0, 0, i, 0))
    nat, perm = SDS((HPG, S, HEAD), F32), SDS((HPG, P4, S // P4, HEAD), F32)
    w0, w1, w2, dah, dh, dah2, dh2 = pl.pallas_call(
        body, name="merge_bwd", grid=(S // TM,),
        in_specs=[pl.BlockSpec((TM, CH), lambda i: (i, 0)), hm, p4, p4, hm, p4, p4],
        out_specs=[hm, p4, p4, hm, hm, p4, p4], out_shape=[nat, perm, perm, nat, nat, perm, perm],
        scratch_shapes=[pltpu.VMEM((TM, HEAD), F32)] * 2)(dattn, *os_, *lses)
    return (w0, dah, dh), (w1, dah2, dh2), (w2, dah2, dh2)


def _qkv_bwd(dz, z, dqs, dks, dvs, qw, kw):
    S = z.shape[0]
    nh = N_GROUPS * HPG

    def body(dz_in, zq, zk, *refs):
        del dz_in
        dq_refs, dk_refs, dv_refs = refs[0:3], refs[3:6], refs[6:9]
        qw_ref, kw_ref, dz_ref, dqw_ref, dkw_ref, scr = refs[9:]

        @pl.when(pl.program_id(0) == 0)
        def _():
            dqw_ref[...] = jnp.zeros_like(dqw_ref)
            dkw_ref[...] = jnp.zeros_like(dkw_ref)

        def grad(refs3, g, hh):
            return _from_p4(refs3[g], hh, scr) if g > 0 else refs3[g][hh]

        def nbwd(xr, dy, wr, dwr, h, off):
            g = h // HPG
            x = xr[:, h * HEAD:(h + 1) * HEAD].astype(F32)
            r = lax.rsqrt(jnp.mean(x * x, axis=-1, keepdims=True) + RMS_EPS)
            xhat = x * r
            dyw = dy * wr[g:g + 1, :]
            dz_ref[:, off + h * HEAD:off + (h + 1) * HEAD] = (
                r * (dyw - xhat * jnp.mean(dyw * xhat, axis=-1, keepdims=True))).astype(BF16)
            dwr[g:g + 1, :] += jnp.sum(dy * xhat, axis=0, keepdims=True)

        for h in range(nh):
            g, hh = h // HPG, h % HPG
            nbwd(zq, grad(dq_refs, g, hh), qw_ref, dqw_ref, h, O_Q)
            nbwd(zk, grad(dk_refs, g, hh), kw_ref, dkw_ref, h, O_K)
            dz_ref[:, O_V + h * HEAD:O_V + (h + 1) * HEAD] = grad(dv_refs, g, hh).astype(BF16)

    hm = pl.BlockSpec((HPG, TM, HEAD), lambda i: (0, i, 0))
    p4 = pl.BlockSpec((HPG, P4, TM // P4, HEAD), lambda i: (0, 0, i, 0))
    return pl.pallas_call(
        body, name="qkv_bwd", grid=(S // TM,),
        in_specs=[pl.BlockSpec(memory_space=pl.ANY), pl.BlockSpec((TM, QKV_W), lambda i: (i, 0)),
                  pl.BlockSpec((TM, QKV_W), lambda i: (i, 1))] + [hm, p4, p4] * 3 + [_full((N_GROUPS, HEAD)), _full((N_GROUPS, HEAD))],
        out_specs=[pl.BlockSpec((TM, 3 * QKV_W), lambda i: (i, 0)), _full((N_GROUPS, HEAD)), _full((N_GROUPS, HEAD))],
        out_shape=[SDS(dz.shape, BF16), SDS((N_GROUPS, HEAD), F32), SDS((N_GROUPS, HEAD), F32)],
        scratch_shapes=[pltpu.VMEM((TM, HEAD), F32)],
        input_output_aliases={0: 0})(dz, z, z, *dqs, *dks, *dvs, qw, kw)


def _conv_fwd(z, cw, cb, lnw, lnb):
    S = z.shape[0]
    H = 32

    def body(zv, zg, cw_ref, cb_ref, lnw_ref, lnb_ref, u1_ref, u3_ref, xbuf):
        i = pl.program_id(0)

        @pl.when(i == 0)
        def _():
            xbuf[0:H, :] = jnp.zeros((H, CH), F32)

        @pl.when(i > 0)
        def _():
            xbuf[0:H, :] = xbuf[TM:TM + H, :]

        xbuf[H:H + TM, :] = zv[...].astype(F32) * _sig(zg[...].astype(F32))
        for r0 in range(0, TM, CONV_RC):
            rows = pl.ds(r0, CONV_RC)
            parts = []
            for lanes in CONV_LANES:
                part = [jnp.broadcast_to(cb_ref[:, lanes], (CONV_RC, CONV_LB))]

                def tap(q, view, part=part, lanes=lanes):
                    part[0] = part[0] + view * cw_ref[q - 2:q - 1, lanes]

                _conv_taps(xbuf, r0, lanes, 2, CONV_K + 1, tap)
                parts.append(part[0])
            acc = jnp.concatenate(parts, axis=1)
            u1_ref[rows, :] = acc
            mu = jnp.mean(acc, axis=-1, keepdims=True)
            xc = acc - mu
            yl = xc * lax.rsqrt(jnp.mean(xc * xc, axis=-1, keepdims=True) + LN_EPS) * lnw_ref[...] + lnb_ref[...]
            u3_ref[rows, :] = (yl * _sig(yl)).astype(BF16)

    row = pl.BlockSpec((TM, CH), lambda i: (i, 0))
    return pl.pallas_call(
        body, name="conv_fwd", grid=(S // TM,),
        in_specs=[pl.BlockSpec((TM, CH), lambda i: (i, O_CV // CH)), pl.BlockSpec((TM, CH), lambda i: (i, O_CG // CH)),
                  _full((CONV_K, CH)), _full((1, CH)), _full((1, CH)), _full((1, CH))],
        out_specs=[row, row], out_shape=[SDS((S, CH), F32), SDS((S, CH), BF16)],
        scratch_shapes=[pltpu.VMEM((TM + H, CH), F32)])(z, z, cw, cb, lnw, lnb)


def _conv_bwd_a(du3, u1, z, lnw, lnb):
    S = z.shape[0]
    H = 32

    nt = S // TM

    def body(du3_ref, u1_ref, zv, zg, lnw_ref, lnb_ref, du1_ref, acc_ref, xbuf, tacc):
        i = pl.program_id(0)

        @pl.when(i == 0)
        def _():
            xbuf[0:H, :] = jnp.zeros((H, CH), F32)
            tacc[...] = jnp.zeros_like(tacc)

        @pl.when(i > 0)
        def _():
            xbuf[0:H, :] = xbuf[TM:TM + H, :]

        xbuf[H:H + TM, :] = zv[...].astype(F32) * _sig(zg[...].astype(F32))
        for r0 in range(0, TM, CONV_RC):
            rows = pl.ds(r0, CONV_RC)
            u1 = u1_ref[rows, :]
            mu = jnp.mean(u1, axis=-1, keepdims=True)
            xc = u1 - mu
            r = lax.rsqrt(jnp.mean(xc * xc, axis=-1, keepdims=True) + LN_EPS)
            yhat = xc * r
            yl = yhat * lnw_ref[...] + lnb_ref[...]
            sg = _sig(yl)
            dyl = du3_ref[rows, :] * (sg * (1.0 + yl * (1.0 - sg)))
            dyh = dyl * lnw_ref[...]
            du1 = r * (dyh - jnp.mean(dyh, axis=-1, keepdims=True) - yhat * jnp.mean(dyh * yhat, axis=-1, keepdims=True))
            du1_ref[rows, :] = du1
            tacc[33] += _fold8(dyl * yhat)
            tacc[34] += _fold8(dyl)
            tacc[32] += _fold8(du1)
            for lanes in CONV_LANES:
                d = du1[:, lanes]

                def tap(q, view, d=d, lanes=lanes):
                    tacc[q - 2, :, lanes] += _fold8(d * view)

                _conv_taps(xbuf, r0, lanes, 2, CONV_K + 1, tap)

        @pl.when(i == nt - 1)
        def _():
            for k in range(40):
                acc_ref[k:k + 1, :] = jnp.sum(tacc[k], axis=0, keepdims=True)

    row = pl.BlockSpec((TM, CH), lambda i: (i, 0))
    return pl.pallas_call(
        body, name="conv_bwd_a", grid=(nt,),
        in_specs=[row, row, pl.BlockSpec((TM, CH), lambda i: (i, O_CV // CH)), pl.BlockSpec((TM, CH), lambda i: (i, O_CG // CH)),
                  _full((1, CH)), _full((1, CH))],
        out_specs=[row, _full((40, CH))], out_shape=[SDS((S, CH), F32), SDS((40, CH), F32)],
        scratch_shapes=[pltpu.VMEM((TM + H, CH), F32), pltpu.VMEM((40, 8, CH), F32)])(du3, u1, z, z, lnw, lnb)


def _conv_bwd_b(dz, du1, z, cw):
    S = z.shape[0]
    nt = S // TM
    H = 32

    def body(dz_in, du1_ref, zv, zg, cw_ref, dz_ref, ybuf, dgate):
        del dz_in
        i, p = pl.program_id(0), pl.program_id(1)

        @pl.when(p == 0)
        def _():
            @pl.when(i == 0)
            def _():
                ybuf[TM:TM + H, :] = jnp.zeros((H, CH), F32)

            @pl.when(i > 0)
            def _():
                ybuf[TM:TM + H, :] = ybuf[0:H, :]

            ybuf[0:TM, :] = du1_ref[...]
            for r0 in range(0, TM, CONV_RC):
                rows = pl.ds(r0, CONV_RC)
                parts = []
                for lanes in CONV_LANES:
                    part = [jnp.zeros((CONV_RC, CONV_LB), F32)]

                    def tap(q, view, part=part, lanes=lanes):
                        part[0] = part[0] + view * cw_ref[CONV_K - 1 - q:CONV_K - q, lanes]

                    _conv_taps(ybuf, r0, lanes, 0, CONV_K - 1, tap)
                    parts.append(part[0])
                acc = jnp.concatenate(parts, axis=1)
                val = zv[rows, :].astype(F32)
                sg = _sig(zg[rows, :].astype(F32))
                dz_ref[rows, :] = (acc * sg).astype(BF16)
                dgate[rows, :] = (acc * val * sg * (1.0 - sg)).astype(BF16)

        @pl.when(p == 1)
        def _():
            dz_ref[...] = dgate[...]

    rev = lambda c: pl.BlockSpec((TM, CH), lambda i, p: (nt - 1 - i, c))
    return pl.pallas_call(
        body, name="conv_bwd_b", grid=(nt, 2),
        in_specs=[pl.BlockSpec(memory_space=pl.ANY), rev(0), rev(O_CV // CH), rev(O_CG // CH), _full((CONV_K, CH))],
        out_specs=pl.BlockSpec((TM, CH), lambda i, p: (nt - 1 - i, O_CV // CH + p)),
        out_shape=SDS(dz.shape, BF16),
        scratch_shapes=[pltpu.VMEM((TM + H, CH), F32), pltpu.VMEM((TM, CH), BF16)],
        input_output_aliases={0: 0})(dz, du1, z, z, cw)


def _memkv_fwd(mem, mnw, wkv, xkw):
    M, D = mem.shape

    def body(mem_ref, mnw_ref, w_ref, xkw_ref, mn_ref, kv_ref, mk_ref, mv_ref):
        x = mem_ref[...]
        mn = (x * lax.rsqrt(jnp.mean(x * x, axis=-1, keepdims=True) + RMS_EPS) * mnw_ref[...]).astype(BF16)
        mn_ref[...] = mn
        kv = jnp.dot(mn, w_ref[...], preferred_element_type=F32)
        kv_ref[...] = kv
        for h in range(HPG):
            k = kv[:, h * HEAD:(h + 1) * HEAD]
            mk_ref[:, h * HEAD:(h + 1) * HEAD] = (
                k * lax.rsqrt(jnp.mean(k * k, axis=-1, keepdims=True) + RMS_EPS) * xkw_ref[...]).astype(BF16)
        mv_ref[...] = kv[:, CH:2 * CH].astype(BF16)

    return pl.pallas_call(
        body, name="memkv_fwd",
        out_shape=[SDS((M, D), BF16), SDS((M, 2 * CH), F32), SDS((M, CH), BF16), SDS((M, CH), BF16)])(mem, mnw, wkv, xkw)


def _cross_q(zx, xqw, h):
    x = zx[:, h * HEAD:(h + 1) * HEAD].astype(F32)
    r = lax.rsqrt(jnp.mean(x * x, axis=-1, keepdims=True) + RMS_EPS)
    xhat = x * r
    return xhat, r, xhat * xqw


def _cross_fwd(z, xqw, mk, mv):
    S = z.shape[0]
    M = mk.shape[0]

    def body(zx, xqw_ref, mk_ref, mv_ref, o_ref):
        for h in range(HPG):
            sl = slice(h * HEAD, (h + 1) * HEAD)
            _, _, q = _cross_q(zx, xqw_ref[...], h)
            s = lax.dot_general(q.astype(BF16), mk_ref[:, sl], NT_DIMS, preferred_element_type=F32) * SCALE
            e = jnp.exp(s - jnp.max(s, axis=-1, keepdims=True))
            p = e / jnp.sum(e, axis=-1, keepdims=True)
            o_ref[:, sl] = jnp.dot(p.astype(BF16), mv_ref[:, sl], preferred_element_type=F32).astype(BF16)

    return pl.pallas_call(
        body, name="cross_fwd", grid=(S // TM,),
        in_specs=[pl.BlockSpec((TM, CH), lambda i: (i, O_XQ // CH)), _full((1, HEAD)), _full((M, CH)), _full((M, CH))],
        out_specs=pl.BlockSpec((TM, CH), lambda i: (i, 0)), out_shape=SDS((S, CH), BF16))(z, xqw, mk, mv)


def _cross_bwd(dz, doc, z, xqw, mk, mv):
    S = z.shape[0]
    M = mk.shape[0]

    def body(dz_in, do_ref, zx, xqw_ref, mk_ref, mv_ref, dz_ref, dmk_ref, dmv_ref, dxw_ref):
        del dz_in

        @pl.when(pl.program_id(0) == 0)
        def _():
            dmk_ref[...] = jnp.zeros_like(dmk_ref)
            dmv_ref[...] = jnp.zeros_like(dmv_ref)
            dxw_ref[...] = jnp.zeros_like(dxw_ref)

        for h in range(HPG):
            sl = slice(h * HEAD, (h + 1) * HEAD)
            xhat, r, q = _cross_q(zx, xqw_ref[...], h)
            qb = q.astype(BF16)
            s = lax.dot_general(qb, mk_ref[:, sl], NT_DIMS, preferred_element_type=F32) * SCALE
            e = jnp.exp(s - jnp.max(s, axis=-1, keepdims=True))
            p = e / jnp.sum(e, axis=-1, keepdims=True)
            do = do_ref[:, sl].astype(BF16)
            dp = lax.dot_general(do, mv_ref[:, sl], NT_DIMS, preferred_element_type=F32)
            ds = (p * (dp - jnp.sum(p * dp, axis=-1, keepdims=True)) * SCALE).astype(BF16)
            dq = jnp.dot(ds, mk_ref[:, sl], preferred_element_type=F32)
            dmk_ref[:, sl] += lax.dot_general(ds, qb, TN_DIMS, preferred_element_type=F32)
            dmv_ref[:, sl] += lax.dot_general(p.astype(BF16), do, TN_DIMS, preferred_element_type=F32)
            dyw = dq * xqw_ref[...]
            dz_ref[:, sl] = (r * (dyw - xhat * jnp.mean(dyw * xhat, axis=-1, keepdims=True))).astype(BF16)
            dxw_ref[...] += jnp.sum(dq * xhat, axis=0, keepdims=True)

    return pl.pallas_call(
        body, name="cross_bwd", grid=(S // TM,),
        in_specs=[pl.BlockSpec(memory_space=pl.ANY), pl.BlockSpec((TM, CH), lambda i: (i, 0)),
                  pl.BlockSpec((TM, CH), lambda i: (i, O_XQ // CH)), _full((1, HEAD)), _full((M, CH)), _full((M, CH))],
        out_specs=[pl.BlockSpec((TM, CH), lambda i: (i, O_XQ // CH)), _full((M, CH)), _full((M, CH)), _full((1, HEAD))],
        out_shape=[SDS(dz.shape, BF16), SDS((M, CH), F32), SDS((M, CH), F32), SDS((1, HEAD), F32)],
        input_output_aliases={0: 0})(dz, doc, z, xqw, mk, mv)


def _memkv_bwd(dmk, dmv, kv, mem, mn, mnw, wkv, xkw):
    M, D = mem.shape

    def body(dmk_ref, dmv_ref, kv_ref, mem_ref, mn_ref, mnw_ref, w_ref, xkw_ref, dw_ref, dxk_ref, dmn_ref, dkv):
        dxk = jnp.zeros((1, HEAD), F32)
        for h in range(HPG):
            sl = slice(h * HEAD, (h + 1) * HEAD)
            k = kv_ref[:, sl]
            r = lax.rsqrt(jnp.mean(k * k, axis=-1, keepdims=True) + RMS_EPS)
            khat = k * r
            dy = dmk_ref[:, sl]
            dyw = dy * xkw_ref[...]
            dkv[:, sl] = (r * (dyw - khat * jnp.mean(dyw * khat, axis=-1, keepdims=True))).astype(BF16)
            dxk = dxk + jnp.sum(dy * khat, axis=0, keepdims=True)
        dxk_ref[...] = dxk
        dkv[:, CH:2 * CH] = dmv_ref[...].astype(BF16)
        dw_ref[...] = lax.dot_general(mn_ref[...], dkv[...], TN_DIMS, preferred_element_type=F32)
        dn = lax.dot_general(dkv[...], w_ref[...], NT_DIMS, preferred_element_type=F32)
        x = mem_ref[...]
        r = lax.rsqrt(jnp.mean(x * x, axis=-1, keepdims=True) + RMS_EPS)
        dmn_ref[...] = jnp.sum(dn * x * r, axis=0, keepdims=True)

    return pl.pallas_call(
        body, name="memkv_bwd",
        out_shape=[SDS((D, 2 * CH), F32), SDS((1, HEAD), F32), SDS((1, D), F32)],
        scratch_shapes=[pltpu.VMEM((M, 2 * CH), BF16)])(dmk, dmv, kv, mem, mn, mnw, wkv, xkw)


def _branch_proj(a_ref, w_ref, y_ref):
    G, _, n = w_ref.shape
    a = a_ref[...]
    for g in range(G):
        y_ref[:, g * n:(g + 1) * n] = jnp.dot(a, w_ref[g], preferred_element_type=F32)


OUT_RC = 16


def _gates(zg_ref, bg_ref, rows, k, D):
    return _sig(zg_ref[rows, k * D:(k + 1) * D].astype(F32) + bg_ref[:, k * D:(k + 1) * D])


def _outproj_fwd(x, z, bg, attn, u3, oc, wao, wco, wxo, wout, fnw):
    S, D = x.shape
    tm = TM

    def body(x_ref, zg_ref, bg_ref, a_ref, u_ref, c_ref, wa, wc, wx, wo, fnw_ref, h1_ref, hn_ref, ya, yc, yx, mg):
        _branch_proj(a_ref, wa, ya)
        _branch_proj(u_ref, wc, yc)
        _branch_proj(c_ref, wx, yx)
        for r0 in range(0, tm, OUT_RC):
            rows = pl.ds(r0, OUT_RC)
            mg[rows, :] = (_gates(zg_ref, bg_ref, rows, 0, D) * ya[rows, :] + _gates(zg_ref, bg_ref, rows, 1, D) * yc[rows, :]
                           + _gates(zg_ref, bg_ref, rows, 2, D) * yx[rows, :]).astype(BF16)
        ya[...] = jnp.dot(mg[...], wo[...], preferred_element_type=F32)
        for r0 in range(0, tm, OUT_RC):
            rows = pl.ds(r0, OUT_RC)
            h1 = x_ref[rows, :] + ya[rows, :]
            h1_ref[rows, :] = h1
            hn_ref[rows, :] = (h1 * lax.rsqrt(jnp.mean(h1 * h1, axis=-1, keepdims=True) + RMS_EPS) * fnw_ref[...]).astype(BF16)

    row = lambda w: pl.BlockSpec((tm, w), lambda i: (i, 0))
    return pl.pallas_call(
        body, name="outproj_fwd", grid=(S // tm,),
        in_specs=[row(D), pl.BlockSpec((tm, 3 * D), lambda i: (i, O_G // (3 * D))), _full((1, 3 * D)), row(CH), row(CH), row(CH),
                  _full(wao.shape), _full(wco.shape), _full(wxo.shape), _full((D, D)), _full((1, D))],
        out_specs=[row(D), row(D)], out_shape=[SDS((S, D), F32), SDS((S, D), BF16)],
        scratch_shapes=[pltpu.VMEM((tm, D), F32)] * 3 + [pltpu.VMEM((tm, D), BF16)])(x, z, bg, attn, u3, oc, wao, wco, wxo, wout, fnw)


def _outproj_bwd(dh1, z, bg, attn, u3, oc, wao, wco, wxo, wout, n_in):
    S, D = dh1.shape
    tm = 256
    nt = S // tm
    G, _, n = wao.shape

    def body(dh_ref, zg_ref, bg_ref, a_ref, u_ref, c_ref, wa, wc, wx, wo,
             dz_ref, da_ref, du_ref, dc_ref, dbg_ref, dwo_ref, dwa_ref, dwc_ref, dwx_ref,
             ya, yc, yx, dm, dy, mg, bacc, wacc):
        i = pl.program_id(0)

        @pl.when(i == 0)
        def _():
            bacc[...] = jnp.zeros_like(bacc)
            wacc[...] = jnp.zeros_like(wacc)
            dwo_ref[...] = jnp.zeros_like(dwo_ref)

        _branch_proj(a_ref, wa, ya)
        _branch_proj(u_ref, wc, yc)
        _branch_proj(c_ref, wx, yx)
        dhb = dh_ref[...].astype(BF16)
        dm[...] = lax.dot_general(dhb, wo[...], NT_DIMS, preferred_element_type=F32)
        for r0 in range(0, tm, OUT_RC):
            rows = pl.ds(r0, OUT_RC)
            dmv = dm[rows, :]
            merged = jnp.zeros((OUT_RC, D), F32)
            for k, y in enumerate((ya, yc, yx)):
                gk = _gates(zg_ref, bg_ref, rows, k, D)
                yk = y[rows, :]
                merged = merged + gk * yk
                dzg = dmv * yk * gk * (1.0 - gk)
                dz_ref[rows, k * D:(k + 1) * D] = dzg.astype(BF16)
                bacc[:, k * D:(k + 1) * D] += _fold8(dzg)
                dy[k, rows, :] = (dmv * gk).astype(BF16)
            mg[rows, :] = merged.astype(BF16)
        dwo_ref[...] += lax.dot_general(mg[...], dhb, TN_DIMS, preferred_element_type=F32)
        for k, (b_ref, w_ref, db_ref) in enumerate(((a_ref, wa, da_ref), (u_ref, wc, du_ref), (c_ref, wx, dc_ref))):
            dyk = dy[k]
            acc = jnp.zeros((tm, CH), F32)
            for g in range(G):
                acc = acc + lax.dot_general(dyk[:, g * n:(g + 1) * n], w_ref[g], NT_DIMS, preferred_element_type=F32)
            db_ref[...] = acc
            wacc[k] += lax.dot_general(b_ref[...], dyk, TN_DIMS, preferred_element_type=F32)

        @pl.when(i == nt - 1)
        def _():
            dbg_ref[...] = jnp.sum(bacc[...], axis=0, keepdims=True)
            for k, dw_ref in enumerate((dwa_ref, dwc_ref, dwx_ref)):
                for g in range(G):
                    dw_ref[g] = wacc[k, :, g * n:(g + 1) * n]

    row = lambda w: pl.BlockSpec((tm, w), lambda i: (i, 0))
    return pl.pallas_call(
        body, name="outproj_bwd", grid=(nt,),
        in_specs=[row(D), pl.BlockSpec((tm, 3 * D), lambda i: (i, O_G // (3 * D))), _full((1, 3 * D)), row(CH), row(CH), row(CH),
                  _full(wao.shape), _full(wco.shape), _full(wxo.shape), _full((D, D))],
        out_specs=[pl.BlockSpec((tm, 3 * D), lambda i: (i, O_G // (3 * D))), row(CH), row(CH), row(CH), _full((1, 3 * D)),
                   _full((D, D))] + [_full(wao.shape)] * 3,
        out_shape=[SDS((S, n_in), BF16)] + [SDS((S, CH), F32)] * 3 + [SDS((1, 3 * D), F32), SDS((D, D), F32)]
        + [SDS(wao.shape, F32)] * 3,
        scratch_shapes=[pltpu.VMEM((tm, D), F32)] * 4 + [pltpu.VMEM((3, tm, D), BF16), pltpu.VMEM((tm, D), BF16),
                                                        pltpu.VMEM((8, 3 * D), F32), pltpu.VMEM((3, CH, D), F32)],
    )(dh1, z, bg, attn, u3, oc, wao, wco, wxo, wout)


FFN_TC = 256
FFN_H = 8


def _ffn_taps(buf, r0):
    xx = buf[pl.ds(r0, FFN_RC + FFN_H), :]
    return xx[FFN_H:], pltpu.roll(xx, 1, 0)[FFN_H:], pltpu.roll(xx, 2, 0)[FFN_H:]


def _ffn_conv(taps, w_ref, b_ref):
    x0, x1, x2 = taps
    return b_ref[...] + x0 * w_ref[2:3, :] + x1 * w_ref[1:2, :] + x2 * w_ref[0:1, :]


def _ffn_fwd(up, cw, cb, wdown, h1, target):
    S, D = h1.shape
    F2 = up.shape[1]
    F = F2 // 2
    nj = F // FFN_TC
    tm = TM

    def body(up_ref, halo_ref, cw_ref, cb_ref, w_ref, h_ref, t_ref, dy_ref, loss_ref, ac_ref, gc_ref, abuf, gbuf, act_s):
        i = pl.program_id(0)

        @pl.when(i == 0)
        def _():
            loss_ref[...] = jnp.zeros_like(loss_ref)

        for j in range(nj):
            ca, cg = slice(j * FFN_TC, (j + 1) * FFN_TC), slice(F + j * FFN_TC, F + (j + 1) * FFN_TC)
            first = i == 0
            abuf[0:FFN_H, :] = jnp.where(first, 0.0, halo_ref[:, ca].astype(F32))
            gbuf[0:FFN_H, :] = jnp.where(first, 0.0, halo_ref[:, cg].astype(F32))
            abuf[FFN_H:FFN_H + tm, :] = up_ref[:, ca].astype(F32)
            gbuf[FFN_H:FFN_H + tm, :] = up_ref[:, cg].astype(F32)
            for r0 in range(0, tm, FFN_RC):
                rows = pl.ds(r0, FFN_RC)
                a = _ffn_conv(_ffn_taps(abuf, r0), cw_ref[:, ca], cb_ref[:, ca])
                gt = _ffn_conv(_ffn_taps(gbuf, r0), cw_ref[:, cg], cb_ref[:, cg])
                ac_ref[rows, ca] = a.astype(BF16)
                gc_ref[rows, ca] = gt.astype(BF16)
                act_s[rows, ca] = (gt * _sig(gt) * a).astype(BF16)
        err = h_ref[...] + jnp.dot(act_s[...], w_ref[...], preferred_element_type=F32) - t_ref[...]
        dy_ref[...] = err * (1.0 / D)
        loss_ref[...] += 0.5 * jnp.sum(jnp.mean(err * err, axis=-1, keepdims=True))

    row = lambda w: pl.BlockSpec((tm, w), lambda i: (i, 0))
    halo = pl.BlockSpec((FFN_H, F2), lambda i: (jnp.maximum(i * (tm // FFN_H) - 1, 0), 0))
    return pl.pallas_call(
        body, name="ffn_fwd", grid=(S // tm,),
        in_specs=[row(F2), halo, _full((FFN_K, F2)), _full((1, F2)), _full((F, D)), row(D), row(D)],
        out_specs=[row(D), _full((8, 128)), row(F), row(F)],
        out_shape=[SDS((S, D), F32), SDS((8, 128), F32), SDS((S, F), BF16), SDS((S, F), BF16)],
        scratch_shapes=[pltpu.VMEM((tm + FFN_H, FFN_TC), F32)] * 2 + [pltpu.VMEM((tm, F), BF16)])(up, up, cw, cb, wdown, h1, target)


def _ffn_bwd_a(dy, wdown, ac, gc):
    S, D = dy.shape
    F = ac.shape[1]
    nj = F // FFN_TC
    tm = 1024 if S % 1024 == 0 else TM

    def body(dy_ref, wd_ref, a_ref, g_ref, da_ref, dg_ref, acca_ref, accg_ref, dwd_ref, dact_s, act_s):
        i, j = pl.program_id(0), pl.program_id(1)

        @pl.when((i == 0) & (j == 0))
        def _():
            acca_ref[...] = jnp.zeros_like(acca_ref)
            accg_ref[...] = jnp.zeros_like(accg_ref)
            dwd_ref[...] = jnp.zeros_like(dwd_ref)

        dyb = dy_ref[...].astype(BF16)
        dact_s[...] = lax.dot_general(dyb, wd_ref[...], NT_DIMS, preferred_element_type=F32)
        pa = pg = jnp.zeros((8, FFN_TC), F32)
        for r0 in range(0, tm, FFN_RC):
            rows = pl.ds(r0, FFN_RC)
            a = a_ref[rows, :].astype(F32)
            gt = g_ref[rows, :].astype(F32)
            dact = dact_s[rows, :]
            sg = _sig(gt)
            silu = gt * sg
            act_s[rows, :] = (silu * a).astype(BF16)
            dac = dact * silu
            dgc = dact * a * (sg * (1.0 + gt * (1.0 - sg)))
            da_ref[rows, :] = dac.astype(BF16)
            dg_ref[rows, :] = dgc.astype(BF16)
            pa = pa + _fold8(dac)
            pg = pg + _fold8(dgc)
        acca_ref[j] += pa
        accg_ref[j] += pg
        dwd_ref[pl.ds(pl.multiple_of(j * FFN_TC, FFN_TC), FFN_TC), :] += lax.dot_general(
            act_s[...], dyb, TN_DIMS, preferred_element_type=F32)

    col = pl.BlockSpec((tm, FFN_TC), lambda i, j: (i, j))
    return pl.pallas_call(
        body, name="ffn_bwd_a", grid=(S // tm, nj),
        in_specs=[pl.BlockSpec((tm, D), lambda i, j: (i, 0)), pl.BlockSpec((FFN_TC, D), lambda i, j: (j, 0)), col, col],
        out_specs=[col, col] + [_full((nj, 8, FFN_TC))] * 2 + [_full((F, D))],
        out_shape=[SDS((S, F), BF16)] * 2 + [SDS((nj, 8, FFN_TC), F32)] * 2 + [SDS((F, D), F32)],
        scratch_shapes=[pltpu.VMEM((tm, FFN_TC), F32), pltpu.VMEM((tm, FFN_TC), BF16)])(dy, wdown, ac, gc)


def _ffn_bwd_b(dca, dcg, up, cw):
    S, F = dca.shape
    nj = F // FFN_TC
    tm = 4096 if S % 4096 == 0 else TM
    nt = S // tm
    span = FFN_RC + FFN_H

    def body(a_ref, g_ref, u_ref, w_ref, o_ref, tacc_ref, ybuf):
        j, i = pl.program_id(0), pl.program_id(1)

        @pl.when(i == 0)
        def _():
            ybuf[tm:tm + FFN_H, :] = jnp.zeros((FFN_H, FFN_TC), F32)
            tacc_ref[...] = jnp.zeros_like(tacc_ref)

        @pl.when(i > 0)
        def _():
            ybuf[tm:tm + FFN_H, :] = ybuf[0:FFN_H, :]

        ybuf[0:tm, :] = jnp.where(j < nj, a_ref[...], g_ref[...]).astype(F32)
        p = [jnp.zeros((8, FFN_TC), F32)] * FFN_K
        for r0 in range(0, tm, FFN_RC):
            rows = pl.ds(r0, FFN_RC)
            yy = ybuf[pl.ds(r0, span), :]
            ys = (yy[:FFN_RC], pltpu.roll(yy, span - 1, 0)[:FFN_RC], pltpu.roll(yy, span - 2, 0)[:FFN_RC])
            o_ref[rows, :] = (ys[0] * w_ref[2:3, :] + ys[1] * w_ref[1:2, :] + ys[2] * w_ref[0:1, :]).astype(BF16)
            u = u_ref[rows, :].astype(F32)
            for k in range(FFN_K):
                p[k] = p[k] + _fold8(ys[FFN_K - 1 - k] * u)
        for k in range(FFN_K):
            tacc_ref[k] += p[k]

    rev = lambda f: pl.BlockSpec((tm, FFN_TC), lambda j, i: (nt - 1 - i, f(j)))
    return pl.pallas_call(
        body, name="ffn_bwd_b", grid=(2 * nj, nt),
        in_specs=[rev(lambda j: jnp.minimum(j, nj - 1)), rev(lambda j: jnp.maximum(j - nj, 0)), rev(lambda j: j),
                  pl.BlockSpec((FFN_K, FFN_TC), lambda j, i: (0, j))],
        out_specs=[rev(lambda j: j), pl.BlockSpec((None, FFN_K, 8, FFN_TC), lambda j, i: (j, 0, 0, 0))],
        out_shape=[SDS((S, 2 * F), BF16), SDS((2 * nj, FFN_K, 8, FFN_TC), F32)],
        scratch_shapes=[pltpu.VMEM((tm + FFN_H, FFN_TC), F32)])(dca, dcg, up, cw)


def _adamw_update(w_ref, g_ref, m_ref, v_ref, d_ref, nm_ref, nv_ref):
    gv = g_ref[...]
    m2 = ADAM_B1 * m_ref[...] + (1.0 - ADAM_B1) * gv
    v2 = ADAM_B2 * v_ref[...] + (1.0 - ADAM_B2) * jnp.square(gv)
    m_hat = m2 / (1.0 - ADAM_B1 ** ADAM_STEP)
    v_hat = v2 / (1.0 - ADAM_B2 ** ADAM_STEP)
    d_ref[...] = -ADAM_LR * (m_hat / (jnp.sqrt(v_hat) + ADAM_EPS) + ADAM_WD * w_ref[...])
    nm_ref[...] = m2
    nv_ref[...] = v2


def _adamw_small(ws, gs, ms, vs):
    n = len(ws)

    def body(*refs):
        for i in range(n):
            _adamw_update(*[refs[k * n + i] for k in range(7)])

    shapes = [SDS(w.shape, F32) for w in ws]
    res = pl.pallas_call(body, name="adamw_small", out_shape=shapes * 3)(*ws, *gs, *ms, *vs)
    return res[:n], res[n:2 * n], res[2 * n:]


def _adamw(w, g, m, v, name):
    R, C = w.shape
    tr = _row_tile(R, max(8, (2 ** 20) // (4 * C) // 8 * 8))

    def body(w_ref, g_ref, m_ref, v_ref, d_ref, nm_ref, nv_ref):
        _adamw_update(w_ref, g_ref, m_ref, v_ref, d_ref, nm_ref, nv_ref)

    blk = pl.BlockSpec((tr, C), lambda i: (i, 0))
    return pl.pallas_call(
        body, name=name, grid=(R // tr,), in_specs=[blk] * 4, out_specs=[blk] * 3,
        out_shape=[SDS((R, C), F32)] * 3)(w, g, m, v)


HBM_SPEC = pl.BlockSpec(memory_space=pltpu.HBM)
SEM_SPEC = pl.BlockSpec(memory_space=pltpu.SEMAPHORE)
DATAFLOW_EFFECT = pltpu.SideEffectType.DATAFLOW_SIDE_EFFECTING


def _position():
    return lax.axis_index("x"), lax.axis_index("y"), lax.axis_index("c")


def _other_chips(x, y):
    return [(1 - x, y), (x, 1 - y), (1 - x, 1 - y)]


SIBLING_BARRIER = 1


def _sibling_handshake(x, y, c):
    barrier = pltpu.get_barrier_semaphore()
    pl.semaphore_signal(barrier, inc=1, device_id=(x, y, 1 - c), device_id_type=MESH)
    pl.semaphore_wait(barrier, 1)


def _relations(x, y, js=(0, 1, 2)):
    return [(j, chip) for j, chip in enumerate(_other_chips(x, y)) if j in js]


def _ag_ici_start(arrs, name, js=(0, 1, 2), lands=None):
    n = len(arrs)

    def body(*refs):
        ins, lands = refs[:n], refs[n:2 * n]
        send_sems, recv_sems = refs[2 * n:2 * n + 2]
        token = refs[-1]
        x, y, c = _position()
        for i in range(n):
            for j, (px, py) in _relations(x, y, js):
                pltpu.make_async_remote_copy(src_ref=ins[i].at[c], dst_ref=lands[i].at[2 * x + y, c], send_sem=send_sems.at[3 * i + j],
                                             recv_sem=recv_sems.at[3 * i + j], device_id=(px, py, c), device_id_type=MESH).start()
        token[...] = jnp.zeros_like(token)

    if lands is None:
        lands = [pltpu.with_memory_space_constraint(lax.empty((4,) + a.shape, a.dtype), pltpu.HBM) for a in arrs]
        arrs = [pltpu.with_memory_space_constraint(a, pltpu.HBM) for a in arrs]
    res = pl.pallas_call(
        body, name=name,
        out_shape=[pltpu.SemaphoreType.DMA((3 * n,)), pltpu.SemaphoreType.DMA((3 * n,))]
        + [pltpu.HBM(a.shape, a.dtype) for a in arrs] + [pltpu.HBM(l.shape, l.dtype) for l in lands] + [SDS((8, 128), F32)],
        in_specs=[HBM_SPEC] * (2 * n), out_specs=[SEM_SPEC, SEM_SPEC] + [HBM_SPEC] * (2 * n) + [pl.BlockSpec(memory_space=pltpu.VMEM)],
        input_output_aliases={i: 2 + i for i in range(2 * n)},
        compiler_params=pltpu.CompilerParams(has_side_effects=DATAFLOW_EFFECT),
    )(*arrs, *lands)
    return res[0], res[1], list(res[2:2 + n]), list(res[2 + n:2 + 2 * n]), res[-1]


def _ag_ici_wait(send_sems, recv_sems, ins, lands, after, name, js=(0, 1, 2)):
    n = len(ins)

    def body(*refs):
        ins_r, lands_r = refs[:n], refs[n:2 * n]
        send_r, recv_r = refs[2 * n:2 * n + 2]
        x, y, c = _position()
        for i in range(n):
            for j, (px, py) in _relations(x, y, js):
                cp = pltpu.make_async_remote_copy(src_ref=ins_r[i].at[c], dst_ref=lands_r[i].at[2 * px + py, c], send_sem=send_r.at[3 * i + j],
                                                  recv_sem=recv_r.at[3 * i + j], device_id=(px, py, c), device_id_type=MESH)
                cp.wait_send()
                cp.wait_recv()

    res = pl.pallas_call(
        body, name=name,
        out_shape=[pltpu.HBM(a.shape, a.dtype) for a in list(ins) + list(lands)],
        in_specs=[HBM_SPEC] * (2 * n) + [SEM_SPEC, SEM_SPEC, pl.BlockSpec(memory_space=pl.ANY)], out_specs=[HBM_SPEC] * (2 * n),
        input_output_aliases={i: i for i in range(2 * n)},
        compiler_params=pltpu.CompilerParams(has_side_effects=DATAFLOW_EFFECT),
    )(*ins, *lands, send_sems, recv_sems, after)
    return list(res[:n]), list(res[n:])


def _ag_finish_start(arrs, lands, name, js=(0, 1, 2), own=True):
    n = len(arrs)

    def body(*refs):
        ins, bufs = refs[:n], refs[n:2 * n]
        send_sems, recv_sems = refs[2 * n:2 * n + 2]
        token = refs[-1]
        x, y, c = _position()
        _sibling_handshake(x, y, c)
        for i in range(n):
            if own:
                pltpu.make_async_remote_copy(src_ref=ins[i], dst_ref=bufs[i].at[2 * x + y], send_sem=send_sems.at[4 * i + 3],
                                             recv_sem=recv_sems.at[4 * i + 3], device_id=(x, y, 1 - c), device_id_type=MESH).start()
            for j, (px, py) in _relations(x, y, js):
                half = bufs[i].at[2 * px + py, c]
                pltpu.make_async_remote_copy(src_ref=half, dst_ref=half, send_sem=send_sems.at[4 * i + j],
                                             recv_sem=recv_sems.at[4 * i + j], device_id=(x, y, 1 - c), device_id_type=MESH).start()
        token[...] = jnp.zeros_like(token)

    res = pl.pallas_call(
        body, name=name,
        out_shape=[pltpu.SemaphoreType.DMA((4 * n,)), pltpu.SemaphoreType.DMA((4 * n,))]
        + [pltpu.HBM(a.shape, a.dtype) for a in list(arrs) + list(lands)] + [SDS((8, 128), F32)],
        in_specs=[HBM_SPEC] * (2 * n), out_specs=[SEM_SPEC, SEM_SPEC] + [HBM_SPEC] * (2 * n) + [pl.BlockSpec(memory_space=pltpu.VMEM)],
        input_output_aliases={i: 2 + i for i in range(2 * n)},
        compiler_params=pltpu.CompilerParams(has_side_effects=DATAFLOW_EFFECT, collective_id=SIBLING_BARRIER),
    )(*arrs, *lands)
    return res[0], res[1], list(res[2:2 + n]), list(res[2 + n:2 + 2 * n]), res[-1]


def _ag_finish_wait(send_sems, recv_sems, arrs, lands, after, name, js=(0, 1, 2), own=True):
    n = len(arrs)

    def body(*refs):
        ins, bufs = refs[:n], refs[n:2 * n]
        send_r, recv_r = refs[2 * n:2 * n + 2]
        x, y, c = _position()
        for i in range(n):
            if own:
                mine = pltpu.make_async_remote_copy(src_ref=ins[i], dst_ref=bufs[i].at[2 * x + y], send_sem=send_r.at[4 * i + 3],
                                                    recv_sem=recv_r.at[4 * i + 3], device_id=(x, y, 1 - c), device_id_type=MESH)
                mine.wait_send()
                mine.wait_recv()
            for j, (px, py) in _relations(x, y, js):
                pltpu.make_async_remote_copy(src_ref=bufs[i].at[2 * px + py, c], dst_ref=bufs[i].at[2 * px + py, c],
                                             send_sem=send_r.at[4 * i + j], recv_sem=recv_r.at[4 * i + j],
                                             device_id=(x, y, 1 - c), device_id_type=MESH).wait_send()
                pltpu.make_async_remote_copy(src_ref=bufs[i].at[2 * px + py, 1 - c], dst_ref=bufs[i].at[2 * px + py, 1 - c],
                                             send_sem=send_r.at[4 * i + j], recv_sem=recv_r.at[4 * i + j],
                                             device_id=(x, y, 1 - c), device_id_type=MESH).wait_recv()

    res = pl.pallas_call(
        body, name=name, out_shape=[pltpu.HBM(a.shape, a.dtype) for a in list(arrs) + list(lands)],
        in_specs=[HBM_SPEC] * (2 * n) + [SEM_SPEC, SEM_SPEC, pl.BlockSpec(memory_space=pl.ANY)], out_specs=[HBM_SPEC] * (2 * n),
        input_output_aliases={i: i for i in range(2 * n)},
        compiler_params=pltpu.CompilerParams(has_side_effects=DATAFLOW_EFFECT),
    )(*arrs, *lands, send_sems, recv_sems, after)
    return list(res[:n]), list(res[n:])


def _ag_finish(arrs, lands, name, js=(0, 1, 2), own=True):
    send_sems, recv_sems, arrs, lands, token = _ag_finish_start(arrs, lands, name + "_start", js, own)
    return _ag_finish_wait(send_sems, recv_sems, arrs, lands, token, name + "_wait", js, own)


def _swap_start(gs, name):
    n = len(gs)

    def body(*refs):
        ins, lands = refs[:n], refs[n:2 * n]
        send_sems, recv_sems = refs[2 * n:2 * n + 2]
        token = refs[-1]
        x, y, c = _position()
        _sibling_handshake(x, y, c)
        for i in range(n):
            for p in range(4):
                pltpu.make_async_remote_copy(src_ref=ins[i].at[p, 1 - c], dst_ref=lands[i].at[p], send_sem=send_sems.at[4 * i + p],
                                             recv_sem=recv_sems.at[4 * i + p], device_id=(x, y, 1 - c), device_id_type=MESH).start()
        token[...] = jnp.zeros_like(token)

    lands = [lax.empty((4,) + g.shape[2:], F32) for g in gs]
    res = pl.pallas_call(
        body, name=name,
        out_shape=[pltpu.SemaphoreType.DMA((4 * n,)), pltpu.SemaphoreType.DMA((4 * n,))]
        + [pltpu.HBM(a.shape, F32) for a in list(gs) + lands] + [SDS((8, 128), F32)],
        in_specs=[HBM_SPEC] * (2 * n), out_specs=[SEM_SPEC, SEM_SPEC] + [HBM_SPEC] * (2 * n) + [pl.BlockSpec(memory_space=pltpu.VMEM)],
        input_output_aliases={i: 2 + i for i in range(2 * n)},
        compiler_params=pltpu.CompilerParams(has_side_effects=DATAFLOW_EFFECT, collective_id=SIBLING_BARRIER),
    )(*[pltpu.with_memory_space_constraint(a, pltpu.HBM) for a in list(gs) + lands])
    return res[0], res[1], list(res[2:2 + n]), list(res[2 + n:2 + 2 * n]), res[-1]


def _swap_wait(send_sems, recv_sems, gs, lands, after, name):
    n = len(gs)

    def body(*refs):
        ins, lands_r = refs[:n], refs[n:2 * n]
        send_r, recv_r = refs[2 * n:2 * n + 2]
        x, y, c = _position()
        for i in range(n):
            for p in range(4):
                cp = pltpu.make_async_remote_copy(src_ref=ins[i].at[p, 1 - c], dst_ref=lands_r[i].at[p], send_sem=send_r.at[4 * i + p],
                                                  recv_sem=recv_r.at[4 * i + p], device_id=(x, y, 1 - c), device_id_type=MESH)
                cp.wait_send()
                cp.wait_recv()

    res = pl.pallas_call(
        body, name=name, out_shape=[pltpu.HBM(a.shape, F32) for a in list(gs) + list(lands)],
        in_specs=[HBM_SPEC] * (2 * n) + [SEM_SPEC, SEM_SPEC, pl.BlockSpec(memory_space=pl.ANY)], out_specs=[HBM_SPEC] * (2 * n),
        input_output_aliases={i: i for i in range(2 * n)},
        compiler_params=pltpu.CompilerParams(has_side_effects=DATAFLOW_EFFECT),
    )(*gs, *lands, send_sems, recv_sems, after)
    return list(res[:n]), list(res[n:])


def _exchange_start(s1s, name):
    n = len(s1s)

    def body(*refs):
        srcs, lands = refs[:n], refs[n:2 * n]
        send_sems, recv_sems = refs[2 * n:2 * n + 2]
        token = refs[-1]
        x, y, c = _position()
        for i in range(n):
            for j, (px, py) in enumerate(_other_chips(x, y)):
                pltpu.make_async_remote_copy(src_ref=srcs[i].at[2 * px + py], dst_ref=lands[i].at[j], send_sem=send_sems.at[3 * i + j],
                                             recv_sem=recv_sems.at[3 * i + j], device_id=(px, py, c), device_id_type=MESH).start()
        token[...] = jnp.zeros_like(token)

    lands = [lax.empty((3,) + s.shape[1:], F32) for s in s1s]
    res = pl.pallas_call(
        body, name=name,
        out_shape=[pltpu.SemaphoreType.DMA((3 * n,)), pltpu.SemaphoreType.DMA((3 * n,))]
        + [pltpu.HBM(a.shape, F32) for a in list(s1s) + lands] + [SDS((8, 128), F32)],
        in_specs=[HBM_SPEC] * (2 * n), out_specs=[SEM_SPEC, SEM_SPEC] + [HBM_SPEC] * (2 * n) + [pl.BlockSpec(memory_space=pltpu.VMEM)],
        input_output_aliases={i: 2 + i for i in range(2 * n)},
        compiler_params=pltpu.CompilerParams(has_side_effects=DATAFLOW_EFFECT),
    )(*[pltpu.with_memory_space_constraint(a, pltpu.HBM) for a in list(s1s) + lands])
    return res[0], res[1], list(res[2:2 + n]), list(res[2 + n:2 + 2 * n]), res[-1]


def _exchange_wait(send_sems, recv_sems, s1s, lands, after, name):
    n = len(s1s)

    def body(*refs):
        srcs, lands_r = refs[:n], refs[n:2 * n]
        send_r, recv_r = refs[2 * n:2 * n + 2]
        x, y, c = _position()
        for i in range(n):
            for j, (px, py) in enumerate(_other_chips(x, y)):
                cp = pltpu.make_async_remote_copy(src_ref=srcs[i].at[2 * px + py], dst_ref=lands_r[i].at[j], send_sem=send_r.at[3 * i + j],
                                                  recv_sem=recv_r.at[3 * i + j], device_id=(px, py, c), device_id_type=MESH)
                cp.wait_send()
                cp.wait_recv()

    res = pl.pallas_call(
        body, name=name, out_shape=[pltpu.HBM(a.shape, F32) for a in list(s1s) + list(lands)],
        in_specs=[HBM_SPEC] * (2 * n) + [SEM_SPEC, SEM_SPEC, pl.BlockSpec(memory_space=pl.ANY)], out_specs=[HBM_SPEC] * (2 * n),
        input_output_aliases={i: i for i in range(2 * n)},
        compiler_params=pltpu.CompilerParams(has_side_effects=DATAFLOW_EFFECT),
    )(*s1s, *lands, send_sems, recv_sems, after)
    return list(res[:n]), list(res[n:])


def _join_halves(f2s, name):
    n = len(f2s)
    hbm = pl.BlockSpec(memory_space=pl.ANY)

    def body(*refs):
        ins, outs = refs[:n], refs[n:2 * n]
        send_sems, recv_sems = refs[2 * n:]
        x, y, c = _position()
        _sibling_handshake(x, y, c)
        cps = []
        for i in range(n):
            cp = pltpu.make_async_remote_copy(src_ref=ins[i].at[c], dst_ref=outs[i].at[c], send_sem=send_sems.at[i],
                                              recv_sem=recv_sems.at[i], device_id=(x, y, 1 - c), device_id_type=MESH)
            cp.start()
            cps.append(cp)
        for i in range(n):
            pltpu.make_async_remote_copy(src_ref=ins[i].at[1 - c], dst_ref=outs[i].at[1 - c], send_sem=send_sems.at[i],
                                         recv_sem=recv_sems.at[i], device_id=(x, y, 1 - c), device_id_type=MESH).wait_recv()
        for cp in cps:
            cp.wait_send()

    return pl.pallas_call(
        body, name=name, in_specs=[hbm] * n, out_specs=[hbm] * n, out_shape=[SDS(f.shape, f.dtype) for f in f2s],
        input_output_aliases={i: i for i in range(n)},
        scratch_shapes=[pltpu.SemaphoreType.DMA((n,)), pltpu.SemaphoreType.DMA((n,))],
        compiler_params=pltpu.CompilerParams(collective_id=SIBLING_BARRIER))(*f2s)


def _sum_tile(rows, cols):
    return _row_tile(rows, max(8, (2 ** 19 // cols) // 8 * 8))


def _add_pair(g, r1, c, name):
    _, R, C = r1.shape
    tr = _sum_tile(R, C)

    def body(c_ref, a_ref, b_ref, o_ref):
        del c_ref
        o_ref[...] = a_ref[...] + b_ref[...]

    blk = pl.BlockSpec((None, tr, C), lambda p, i, cr: (p, i, 0))
    return pl.pallas_call(
        body, name=name,
        grid_spec=pltpu.PrefetchScalarGridSpec(
            num_scalar_prefetch=1, grid=(4, R // tr),
            in_specs=[pl.BlockSpec((None, None, tr, C), lambda p, i, cr: (p, cr[0], i, 0)), blk], out_specs=blk),
        out_shape=SDS((4, R, C), F32))(c, g, r1)


def _add_four(s1, r2, me_c, name):
    _, R, C = s1.shape
    tr = _sum_tile(R, C)

    def body(m_ref, a_ref, b_ref, o_ref):
        del m_ref
        o_ref[...] = ((a_ref[...] + b_ref[0]) + b_ref[1]) + b_ref[2]

    return pl.pallas_call(
        body, name=name,
        grid_spec=pltpu.PrefetchScalarGridSpec(
            num_scalar_prefetch=1, grid=(R // tr,),
            in_specs=[pl.BlockSpec((None, tr, C), lambda i, mr: (mr[0], i, 0)), pl.BlockSpec((3, tr, C), lambda i, mr: (0, i, 0))],
            out_specs=pl.BlockSpec((None, tr, C), lambda i, mr: (mr[1], i, 0))),
        out_shape=SDS((2, R, C), F32))(me_c, s1, r2)


def _all_reduce_small(vs):
    n = len(vs)

    def body(*refs):
        ins, outs, bufs = refs[:n], refs[n:2 * n], refs[2 * n:3 * n]
        send_sems, recv_sems = refs[3 * n:]
        x, y, c = _position()
        me = 4 * x + 2 * y + c
        for i in range(n):
            bufs[i][me] = ins[i][...]
        cps = []
        for k in range(1, 8):
            to = (1 - x if k & 4 else x, 1 - y if k & 2 else y, 1 - c if k & 1 else c)
            for i in range(n):
                cp = pltpu.make_async_remote_copy(src_ref=bufs[i].at[me], dst_ref=bufs[i].at[me], send_sem=send_sems.at[7 * i + k - 1],
                                                  recv_sem=recv_sems.at[7 * i + k - 1], device_id=to, device_id_type=MESH)
                cp.start()
                cps.append(cp)
        for cp in cps:
            cp.wait_send()
        for k in range(1, 8):
            src = 4 * (1 - x if k & 4 else x) + 2 * (1 - y if k & 2 else y) + (1 - c if k & 1 else c)
            for i in range(n):
                pltpu.make_async_remote_copy(src_ref=bufs[i].at[src], dst_ref=bufs[i].at[src], send_sem=send_sems.at[7 * i + k - 1],
                                             recv_sem=recv_sems.at[7 * i + k - 1], device_id=(x, y, c), device_id_type=MESH).wait_recv()
        for i in range(n):
            acc = bufs[i][0]
            for k in range(1, 8):
                acc = acc + bufs[i][k]
            outs[i][...] = acc

    vm = pl.BlockSpec(memory_space=pltpu.VMEM)
    return pl.pallas_call(
        body, name="all_reduce_small", in_specs=[vm] * n, out_specs=[vm] * n, out_shape=[SDS(v.shape, F32) for v in vs],
        scratch_shapes=[pltpu.VMEM((8,) + v.shape, F32) for v in vs]
        + [pltpu.SemaphoreType.DMA((7 * n,)), pltpu.SemaphoreType.DMA((7 * n,))])(*vs)


def _reduce_begin(grads, tag):
    g4 = [g.reshape(4, 2, g.shape[1] // 2, g.shape[2]) for g in grads]
    send_sems, recv_sems, g4, lands, token = _swap_start(g4, "rs_swap_start_" + tag)
    return (send_sems, recv_sems, g4, lands), token


def _reduce_mid(state, after, tag):
    _, _, c = _position()
    cs = jnp.reshape(c, (1,)).astype(jnp.int32)
    send_sems, recv_sems, g4, lands = state
    g4, r1 = _swap_wait(send_sems, recv_sems, g4, lands, after, "rs_swap_wait_" + tag)
    s1 = [_add_pair(g, r, cs, f"rs_add_pair_{tag}{i}") for i, (g, r) in enumerate(zip(g4, r1))]
    send_sems, recv_sems, s1, lands, token = _exchange_start(s1, "rs_exchange_start_" + tag)
    return (send_sems, recv_sems, s1, lands), token


def _reduce_end(state, after, tag):
    x, y, c = _position()
    send_sems, recv_sems, s1, lands = state
    s1, lands = _exchange_wait(send_sems, recv_sems, s1, lands, after, "rs_exchange_wait_" + tag)
    me_c = jnp.stack([2 * x + y, c]).astype(jnp.int32)
    f2 = [_add_four(s, l, me_c, f"rs_add_four_{tag}{i}") for i, (s, l) in enumerate(zip(s1, lands))]
    return [f.reshape(2 * f.shape[1], f.shape[2]) for f in _join_halves(f2, "rs_join_" + tag)]


def _halves(a):
    return a.reshape((2, a.shape[0] // 2) + a.shape[1:])


def _after(a, token):
    return a + token[0, 0]


def _in_proj_own(x, nw, w, part):
    M, K = x.shape
    n = w.shape[2]
    sub = 256

    def body(p_ref, x_ref, nw_ref, w_ref, xn_ref, z_ref):
        del p_ref
        for r0 in range(0, MM_TM, sub):
            rows = pl.ds(r0, sub)
            for c0 in range(r0, r0 + sub, NORM_RC):
                ch = pl.ds(c0, NORM_RC)
                xv = x_ref[ch, :]
                xn_ref[ch, :] = (xv * lax.rsqrt(jnp.mean(xv * xv, axis=-1, keepdims=True) + RMS_EPS) * nw_ref[...]).astype(BF16)
            z_ref[rows, :] = jnp.dot(xn_ref[rows, :], w_ref[...], preferred_element_type=F32).astype(BF16)

    row = pl.BlockSpec((MM_TM, K), lambda i, pr: (i, 0))
    return pl.pallas_call(
        body, name="in_proj_own",
        grid_spec=pltpu.PrefetchScalarGridSpec(
            num_scalar_prefetch=1, grid=(M // MM_TM,),
            in_specs=[row, pl.BlockSpec((1, K), lambda i, pr: (0, 0)), pl.BlockSpec((None, K, n), lambda i, pr: (0, 0, 0))],
            out_specs=[row, pl.BlockSpec((MM_TM, n), lambda i, pr: (i, pr[0]))]),
        out_shape=[SDS((M, K), BF16), SDS((M, 4 * n), BF16)])(part, x, nw, w)


def _in_proj_parts(xn, w, parts, z, name):
    M, K = xn.shape
    _, _, n = w.shape
    P = parts.shape[0]

    def body(p_ref, a_ref, b_ref, z_in, o_ref):
        del p_ref, z_in
        o_ref[...] = jnp.dot(a_ref[...], b_ref[...], preferred_element_type=F32).astype(BF16)

    return pl.pallas_call(
        body, name=name,
        grid_spec=pltpu.PrefetchScalarGridSpec(
            num_scalar_prefetch=1, grid=(P, M // MM_TM),
            in_specs=[pl.BlockSpec((MM_TM, K), lambda g, i, pr: (i, 0)), pl.BlockSpec((None, K, n), lambda g, i, pr: (pr[g], 0, 0)),
                      pl.BlockSpec(memory_space=pl.ANY)],
            out_specs=pl.BlockSpec((MM_TM, n), lambda g, i, pr: (i, pr[g]))),
        out_shape=SDS(z.shape, BF16), input_output_aliases={3: 0})(parts, xn, w, z)


def _local_step(x, mem, target, sp, ex):
    S, D = x.shape
    band, buckets = _bias_static()
    buckets = jnp.asarray(buckets)

    tok = ex.start_first()
    own, me, near, far = ex.own_w_in()
    xn, z = _in_proj_own(x, _after(sp["attn_norm_w"], tok), own, me)
    z = _in_proj_parts(xn, ex.near_weights(after=z), near, z, "in_proj_near")
    w_in, gathered_small = ex.first_weights(after=z)
    sp = {**sp, **gathered_small}
    n_in = 4 * w_in.shape[2]
    bw = {"w_in": w_in}
    z = _in_proj_parts(xn, w_in, far + ex.start_rest()[0, 0].astype(jnp.int32), z, "in_proj_far")
    qh, kh, vh, q2, k2, v2 = _qkv_prep(z, sp["q_norm_w"], sp["k_norm_w"])
    tab = sp["rel_bias_table"].T.reshape(N_GROUPS, HPG, N_BUCKETS)
    bias = _bias_fwd(jnp.pad(tab, ((0, 0), (0, 8 - HPG), (0, 0))), buckets)
    biasm = jnp.where(jnp.asarray(band)[None, None], bias[:, :HPG].reshape(N_GROUPS, HPG, NQ, 2 * NQ), NEG)
    seqs = lambda a: a.reshape(-1, S // P4, HEAD)
    q12, k12, v12 = seqs(q2), seqs(k2), seqs(v2)
    groups = ((qh, kh, vh, 1, 0, HPG, 0), (q12, k12, v12, 1, 0, HPG * P4, 2), (q12, k12, v12, 4, HPG * P4, HPG * P4, 2))
    os_, lses = [], []
    for g, (qg, kg, vg, dil, head0, nseq, shift) in enumerate(groups):
        o_g, lse_g = _attn_fwd(qg, kg, vg, biasm[g], f"attn_fwd_g{g}", dil, head0, nseq, shift)
        os_.append(o_g.reshape(HPG, -1, o_g.shape[1], HEAD) if g > 0 else o_g)
        lses.append(lse_g.reshape(HPG, -1, lse_g.shape[1], HEAD) if g > 0 else lse_g)
    attn = _merge_fwd(os_, lses)
    tok = ex.rest_arrived(after=attn)
    u1, u3 = _conv_fwd(z, sp["conv_dw_w"], sp["conv_dw_b"], _after(sp["conv_ln_w"], tok), sp["conv_ln_b"])
    bw.update(ex.rest_weights(after=u3))
    F2 = 4 * bw["w_up"].shape[2]
    mn, kv, mk, mv = _memkv_fwd(mem, sp["mem_norm_w"], bw["w_mem_kv"], sp["xk_norm_w"])
    oc = _cross_fwd(z, sp["xq_norm_w"], mk, mv)
    h1, hn = _outproj_fwd(x, z, sp["b_gate"], attn, u3, oc, bw["w_attn_o"], bw["w_conv_o"], bw["w_cross_o"], bw["w_out"],
                          sp["ffn_norm_w"])
    up = _mm_nn(hn, bw["w_up"], BF16, "ffn_up")
    dy, loss_tile, ac, gc = _ffn_fwd(up, sp["ffn_conv_w"], sp["ffn_conv_b"], bw["w_down"], h1, target)

    gs, gb = {}, {}
    dca, dcg, acca, accg, gb["w_down"] = _ffn_bwd_a(dy, bw["w_down"], ac, gc)
    cols = lambda acc: jnp.sum(acc, axis=1).reshape(1, F2 // 2)
    gs["ffn_conv_b"] = jnp.concatenate([cols(acca), cols(accg)], axis=1)
    dup, tacc = _ffn_bwd_b(dca, dcg, up, sp["ffn_conv_w"])
    gs["ffn_conv_w"] = jnp.transpose(jnp.sum(tacc, axis=2), (1, 0, 2)).reshape(FFN_K, F2)
    gb["w_up"] = _mm_tn(hn, dup, 4, "dw_up")
    tok = ex.reduce_begin("a", ("w_down", "w_up"), gb)
    dh1, gs["ffn_norm_w"] = _norm_in_bwd(dup, bw["w_up"], h1, _after(sp["ffn_norm_w"], tok), dy, "ffn_in_bwd")
    tok = ex.reduce_mid("a", after=gs["ffn_norm_w"])
    dz, dattn, du3, doc, gs["b_gate"], gb["w_out"], gb["w_attn_o"], gb["w_conv_o"], gb["w_cross_o"] = _outproj_bwd(
        dh1, z, _after(sp["b_gate"], tok), attn, u3, oc, bw["w_attn_o"], bw["w_conv_o"], bw["w_cross_o"], bw["w_out"], n_in)
    dz, dmk, dmv, gs["xq_norm_w"] = _cross_bwd(dz, doc, z, sp["xq_norm_w"], mk, mv)
    gb["w_mem_kv"], gs["xk_norm_w"], gs["mem_norm_w"] = _memkv_bwd(
        dmk, dmv, kv, mem, mn, sp["mem_norm_w"], bw["w_mem_kv"], sp["xk_norm_w"])
    ex.reduce_end("a", after=gs["mem_norm_w"])
    tok = ex.reduce_begin("b", ("w_out", "w_attn_o", "w_conv_o", "w_cross_o", "w_mem_kv"), gb)
    du1, cacc = _conv_bwd_a(du3, u1, z, _after(sp["conv_ln_w"], tok), sp["conv_ln_b"])
    gs["conv_dw_w"], gs["conv_dw_b"] = cacc[:CONV_K], cacc[32:33]
    gs["conv_ln_w"], gs["conv_ln_b"] = cacc[33:34], cacc[34:35]
    tok = ex.reduce_mid("b", after=cacc)
    dz = _conv_bwd_b(dz, du1, z, _after(sp["conv_dw_w"], tok))
    merged_grads = _merge_bwd(dattn, os_, lses)
    dqs, dks, dvs, dsbs = [], [], [], []
    for g, (qg, kg, vg, dil, head0, nseq, shift) in enumerate(groups):
        wg_g, da_g, dh_g = merged_grads[g]
        lse_g = lses[g]
        if g > 0:
            wg_g, da_g, dh_g, lse_g = seqs(wg_g), seqs(da_g), seqs(dh_g), seqs(lse_g)
        dq_g, dk_g, dv_g, dsb_g = _attn_bwd(qg, kg, vg, biasm[g], da_g, wg_g, dh_g, lse_g, f"attn_bwd_g{g}", dil, head0, nseq, shift)
        if g > 0:
            dq_g, dk_g, dv_g = (t.reshape(HPG, P4, S // P4, HEAD) for t in (dq_g, dk_g, dv_g))
            dsb_g = jnp.sum(dsb_g.reshape(HPG, P4, NQ, 2 * NQ), axis=1)
        dqs.append(dq_g)
        dks.append(dk_g)
        dvs.append(dv_g)
        dsbs.append(dsb_g.reshape(HPG, NQ * 2 * NQ))
    dtab = _bias_bwd(jnp.pad(jnp.stack(dsbs), ((0, 0), (0, 8 - HPG), (0, 0))), buckets)
    gs["rel_bias_table"] = dtab[:, :HPG].reshape(N_GROUPS * HPG, N_BUCKETS).T
    dz, gs["q_norm_w"], gs["k_norm_w"] = _qkv_bwd(dz, z, dqs, dks, dvs, sp["q_norm_w"], sp["k_norm_w"])
    ex.reduce_end("b", after=gs["q_norm_w"])
    gb["w_in"] = _mm_tn(xn, dz, 4, "dw_in")
    tok = ex.reduce_mid("c", after=ex.reduce_begin("c", ("w_in",), gb))
    dx, gs["attn_norm_w"] = _norm_in_bwd(dz, bw["w_in"], x, _after(sp["attn_norm_w"], tok), dh1, "in_bwd")
    ex.reduce_end("c", after=gs["attn_norm_w"])
    return loss_tile, dx, gs, gb


SMALL = ("rel_bias_table", "attn_norm_w", "b_gate", "q_norm_w", "k_norm_w", "conv_dw_w", "conv_dw_b", "conv_ln_w", "conv_ln_b",
         "mem_norm_w", "xq_norm_w", "xk_norm_w", "ffn_norm_w", "ffn_conv_w", "ffn_conv_b")
SMALL_SHARDED = ("conv_dw_w", "ffn_conv_w")
BIG_COL = ("w_in", "w_attn_o", "w_conv_o", "w_cross_o", "w_up")
BIG_ROW = ("w_mem_kv", "w_out", "w_down")
BIG = BIG_COL + BIG_ROW
WEIGHTS = ("rel_bias_table", "attn_norm_w", "w_in", "b_gate", "q_norm_w", "k_norm_w", "w_attn_o", "conv_dw_w", "conv_dw_b",
           "conv_ln_w", "conv_ln_b", "w_conv_o", "mem_norm_w", "w_mem_kv", "xq_norm_w", "xk_norm_w", "w_cross_o", "w_out",
           "ffn_norm_w", "w_up", "ffn_conv_w", "ffn_conv_b", "w_down")


class _Exchanges:
    REST = tuple(k for k in BIG if k != "w_in")

    def __init__(self, w):
        self.w = w
        self.pending = {}
        self.reduced = {}

    def _whole(self, k, ga):
        ga = ga.reshape((4,) + self.w[k].shape)
        return ga if k in BIG_COL else ga.reshape((4 * self.w[k].shape[0],) + self.w[k].shape[1:])

    def start_first(self):
        self.w_in_local = self.w["w_in"].astype(BF16)
        local = [_halves(self.w_in_local)]
        for k in SMALL_SHARDED:
            flat = jnp.ravel(self.w[k])
            local.append(jnp.pad(flat, (0, (-flat.shape[0]) % 2048)).reshape(2, -1, 128))
        near = _ag_ici_start(local, "gather_near_start", js=(0, 1))
        far = _ag_ici_start(near[2], "gather_far_start", js=(2,), lands=near[3])
        self.pending["first"] = (near[:2], far[:2], far[2], far[3])
        return far[4]

    def own_w_in(self):
        x, y, _ = _position()
        me = 2 * x + y
        ids = lambda *v: jnp.stack(v).astype(jnp.int32)
        return self.w_in_local[None], ids(me), ids(me ^ 2, me ^ 1), ids(me ^ 3)

    def near_weights(self, after):
        near, far, ins, lands = self.pending.pop("first")
        ins, lands = _ag_ici_wait(*near, ins, lands, after, "gather_near_wait", js=(0, 1))
        ins, lands = _ag_finish(ins, lands, "gather_near_finish", js=(0, 1))
        self.pending["first"] = (far, ins, lands)
        return self._whole("w_in", lands[0])

    def first_weights(self, after):
        far, ins, lands = self.pending.pop("first")
        ins, lands = _ag_ici_wait(*far, ins, lands, after, "gather_far_wait", js=(2,))
        _, gathered = _ag_finish(ins, lands, "gather_far_finish", js=(2,), own=False)
        self.first = gathered[0]
        small = {}
        for k, ga in zip(SMALL_SHARDED, gathered[1:]):
            r, cdim = self.w[k].shape
            parts = ga.reshape(4, -1)[:, :r * cdim].reshape(4, r, cdim)
            small[k] = jnp.transpose(parts, (1, 0, 2)).reshape(r, 4 * cdim)
        return self._whole("w_in", gathered[0]), small

    def start_rest(self):
        local = [_halves(self.w[k].astype(BF16)) for k in self.REST]
        local, _ = lax.optimization_barrier((local, self.first))
        send_sems, recv_sems, ins, lands, token = _ag_ici_start(local, "gather_rest_start")
        self.pending["rest"] = (send_sems, recv_sems, ins, lands)
        return token

    def rest_arrived(self, after):
        send_sems, recv_sems, ins, lands = self.pending.pop("rest")
        ins, lands = _ag_ici_wait(send_sems, recv_sems, ins, lands, after, "gather_rest_wait")
        send_sems, recv_sems, ins, lands, token = _ag_finish_start(ins, lands, "gather_rest_finish_start")
        self.pending["rest"] = (send_sems, recv_sems, ins, lands)
        return token

    def rest_weights(self, after):
        _, gathered = _ag_finish_wait(*self.pending.pop("rest"), after, "gather_rest_finish_wait")
        return {k: self._whole(k, ga) for k, ga in zip(self.REST, gathered)}

    def reduce_begin(self, tag, names, gb):
        parts = [gb[k].reshape((4,) + self.w[k].shape) for k in names]
        state, token = _reduce_begin(parts, tag)
        self.pending[tag] = (state, names)
        return token

    def reduce_mid(self, tag, after):
        state, names = self.pending.pop(tag)
        state, token = _reduce_mid(state, after, tag)
        self.pending[tag] = (state, names)
        return token

    def reduce_end(self, tag, after):
        state, names = self.pending.pop(tag)
        self.reduced.update(zip(names, _reduce_end(state, after, tag)))


def _step(x, mem, target, w, m, v):
    xi, yi, _ = _position()
    shard = 2 * xi + yi
    ex = _Exchanges(w)
    sp = {k: w[k] for k in SMALL if k not in SMALL_SHARDED}
    loss_tile, dx, gs, _ = _local_step(x, mem, target, sp, ex)
    g_big = ex.reduced

    red = _all_reduce_small([loss_tile] + [gs[k] for k in SMALL])
    loss = red[0][0, 0]
    g_small = dict(zip(SMALL, red[1:]))
    for k in SMALL_SHARDED:
        cdim = w[k].shape[1]
        g_small[k] = lax.dynamic_slice_in_dim(g_small[k], shard * cdim, cdim, axis=1)

    grads, delta, new_m, new_v = {}, {}, {}, {}
    for k in BIG:
        grads[k] = g_big[k]
        delta[k], new_m[k], new_v[k] = _adamw(w[k], g_big[k], m[k], v[k], "adamw_" + k)
    outs = _adamw_small([w[k] for k in SMALL], [g_small[k] for k in SMALL], [m[k] for k in SMALL], [v[k] for k in SMALL])
    for dst, vals in zip((delta, new_m, new_v), outs):
        dst.update(zip(SMALL, vals))
    grads.update(g_small)
    return loss, dx, grads, delta, new_m, new_v


def kernel(x, mem, rel_bias_table, attn_norm_w, w_in, b_gate, q_norm_w, k_norm_w, w_attn_o, conv_dw_w, conv_dw_b, conv_ln_w, conv_ln_b, w_conv_o, mem_norm_w, w_mem_kv, xq_norm_w, xk_norm_w, w_cross_o, w_out, ffn_norm_w, w_up, ffn_conv_w, ffn_conv_b, w_down, loss_target, m_rel_bias_table, m_attn_norm_w, m_w_in, m_b_gate, m_q_norm_w, m_k_norm_w, m_w_attn_o, m_conv_dw_w, m_conv_dw_b, m_conv_ln_w, m_conv_ln_b, m_w_conv_o, m_mem_norm_w, m_w_mem_kv, m_xq_norm_w, m_xk_norm_w, m_w_cross_o, m_w_out, m_ffn_norm_w, m_w_up, m_ffn_conv_w, m_ffn_conv_b, m_w_down, v_rel_bias_table, v_attn_norm_w, v_w_in, v_b_gate, v_q_norm_w, v_k_norm_w, v_w_attn_o, v_conv_dw_w, v_conv_dw_b, v_conv_ln_w, v_conv_ln_b, v_w_conv_o, v_mem_norm_w, v_w_mem_kv, v_xq_norm_w, v_xk_norm_w, v_w_cross_o, v_w_out, v_ffn_norm_w, v_w_up, v_ffn_conv_w, v_ffn_conv_b, v_w_down):
    args = locals()
    def block(name, k):
        a = args[name] if k == "rel_bias_table" else args[name][0]
        return a.reshape(1, -1) if a.ndim == 1 else a

    w = {k: block(k, k) for k in WEIGHTS}
    m = {k: block("m_" + k, k) for k in WEIGHTS}
    v = {k: block("v_" + k, k) for k in WEIGHTS}
    loss, dx, grads, delta, new_m, new_v = _step(x[0], mem[0], loss_target[0], w, m, v)
    out = [loss, dx[None]]
    for d in (grads, delta, new_m, new_v):
        for k in WEIGHTS:
            out.append(d[k].reshape(args[k].shape))
    return tuple(out)
```

```python
import math

import numpy as np
import jax
import jax.numpy as jnp
from jax import lax
from jax.experimental import pallas as pl
from jax.experimental.pallas import tpu as pltpu

F32, BF16 = jnp.float32, jnp.bfloat16
SDS = jax.ShapeDtypeStruct
MESH = pl.DeviceIdType.MESH

HEAD = 128
N_GROUPS, HPG = 3, 4
ATTN_GROUPS = ((128, 1), (512, 4), (2048, 16))
NQ = 128
QKV_W = N_GROUPS * HPG * HEAD
CH = 512
CONV_K, FFN_K = 31, 3
N_BUCKETS, MAX_DIST = 32, 2048
RMS_EPS, LN_EPS = 1e-6, 1e-5
O_Q, O_K, O_V, O_CV, O_CG, O_XQ, O_G = 0, QKV_W, 2 * QKV_W, 3 * QKV_W, 3 * QKV_W + CH, 3 * QKV_W + 2 * CH, 3 * QKV_W + 3 * CH
ADAM_LR, ADAM_B1, ADAM_B2, ADAM_EPS, ADAM_WD, ADAM_STEP = 0.001, 0.9, 0.999, 1e-08, 0.01, 10
NEG = -1e30
SCALE = HEAD ** -0.5
TM = 512
MM_TM = 1024
ATT_RB = 2048
NT_DIMS = (((1,), (1,)), ((), ()))
TN_DIMS = (((0,), (0,)), ((), ()))


CONV_RC = 32
CONV_LB = 256
CONV_LANES = tuple(slice(l, l + CONV_LB) for l in range(0, CH, CONV_LB))
CONV_HALO = 32
FFN_RC = 32


def _conv_taps(buf, base, lanes, q_lo, q_hi, visit):
    span = CONV_RC + CONV_HALO
    xx = buf[pl.ds(base, span), lanes]
    for s in range(8):
        xs = xx if s == 0 else pltpu.roll(xx, span - s, 0)
        for q in range(s, q_hi + 1, 8):
            if q >= q_lo:
                visit(q, xs[q - s:q - s + CONV_RC])


def _sig(v):
    return 0.5 * jnp.tanh(0.5 * v) + 0.5


def _fold8(v):
    acc = v[0:8]
    for r in range(8, v.shape[0], 8):
        acc = acc + v[r:r + 8]
    return acc


def _row_tile(rows, cap, mult=8):
    best = None
    for t in range(mult, min(rows, cap) + 1, mult):
        if rows % t == 0:
            best = t
    return best if best is not None else rows


def _full(shape):
    n = len(shape)
    return pl.BlockSpec(shape, lambda *a: (0,) * n)


def _mm_nn(a, b, out_dtype, name):
    M, K = a.shape
    G, _, n = b.shape

    def body(a_ref, b_ref, o_ref):
        o_ref[...] = jnp.dot(a_ref[...].astype(BF16), b_ref[...], preferred_element_type=F32).astype(out_dtype)

    return pl.pallas_call(
        body, name=name, grid=(G, M // MM_TM),
        in_specs=[pl.BlockSpec((MM_TM, K), lambda g, i: (i, 0)), pl.BlockSpec((None, K, n), lambda g, i: (g, 0, 0))],
        out_specs=pl.BlockSpec((MM_TM, n), lambda g, i: (i, g)),
        out_shape=SDS((M, G * n), out_dtype))(a, b)


def _mm_tn(a, b, G, name, wire=False):
    S, Ka = a.shape
    n = b.shape[1] // G
    nk = S // MM_TM
    tka = Ka
    while tka * n * 4 > 10 * 2 ** 20 and tka % 256 == 0:
        tka //= 2

    def body(a_ref, b_ref, o_ref, *w_ref):
        @pl.when(pl.program_id(2) == 0)
        def _():
            o_ref[...] = jnp.zeros_like(o_ref)
        o_ref[...] += lax.dot_general(a_ref[...].astype(BF16), b_ref[...].astype(BF16), TN_DIMS, preferred_element_type=F32)
        if wire:
            @pl.when(pl.program_id(2) == nk - 1)
            def _():
                w_ref[0][...] = o_ref[...].astype(BF16)

    out = pl.BlockSpec((None, tka, n), lambda g, i, k: (g, i, 0))
    return pl.pallas_call(
        body, name=name, grid=(G, Ka // tka, nk),
        in_specs=[pl.BlockSpec((MM_TM, tka), lambda g, i, k: (k, i)), pl.BlockSpec((MM_TM, n), lambda g, i, k: (k, g))],
        out_specs=[out, out] if wire else out,
        out_shape=[SDS((G, Ka, n), F32), SDS((G, Ka, n), BF16)] if wire else SDS((G, Ka, n), F32))(a, b)


NORM_RC = 16


def _norm_in_bwd(a, w, xin, nw, resid, name):
    S, K = xin.shape
    G, _, n = w.shape
    tm = 1024 if S % 1024 == 0 and G * K * n * 2 <= 12 * 2 ** 20 else TM
    nt = S // tm

    def body(a_ref, w_ref, x_ref, nw_ref, r_ref, o_ref, dnw_ref, acc, part):
        i, g = pl.program_id(0), pl.program_id(1)

        @pl.when((i == 0) & (g == 0))
        def _():
            part[...] = jnp.zeros_like(part)

        @pl.when(g == 0)
        def _():
            acc[...] = jnp.zeros_like(acc)

        acc[...] += lax.dot_general(a_ref[...], w_ref[g], NT_DIMS, preferred_element_type=F32)

        @pl.when(g == G - 1)
        def _():
            for r0 in range(0, tm, NORM_RC):
                rows = pl.ds(r0, NORM_RC)
                dn = acc[rows, :]
                xv = x_ref[rows, :]
                r = lax.rsqrt(jnp.mean(xv * xv, axis=-1, keepdims=True) + RMS_EPS)
                xhat = xv * r
                dyw = dn * nw_ref[...]
                o_ref[rows, :] = r_ref[rows, :] + r * (dyw - xhat * jnp.mean(dyw * xhat, axis=-1, keepdims=True))
                part[...] += _fold8(dn * xhat)

        @pl.when((i == nt - 1) & (g == G - 1))
        def _():
            dnw_ref[...] = jnp.sum(part[...], axis=0, keepdims=True)

    row = pl.BlockSpec((tm, K), lambda i, g: (i, 0))
    return pl.pallas_call(
        body, name=name, grid=(nt, G),
        in_specs=[pl.BlockSpec((tm, n), lambda i, g: (i, g)),
                  pl.BlockSpec((G, K, n), lambda i, g: (0, 0, 0), pipeline_mode=pl.Buffered(1)),
                  row, _full((1, K)), row],
        out_specs=[row, _full((1, K))],
        out_shape=[SDS((S, K), F32), SDS((1, K), F32)],
        scratch_shapes=[pltpu.VMEM((tm, K), F32), pltpu.VMEM((8, K), F32)])(a, w, xin, nw, resid)


def _t5_bucket_np(dist):
    max_exact = N_BUCKETS // 2
    d = np.maximum(dist.astype(np.float32), np.float32(1.0))
    large = max_exact + (np.log(d / np.float32(max_exact)) / np.float32(math.log(MAX_DIST / max_exact))
                         * np.float32(N_BUCKETS - max_exact)).astype(np.int32)
    large = np.minimum(large, N_BUCKETS - 1)
    return np.where(dist < max_exact, dist, large).astype(np.int32)


def _bias_static():
    qi = np.arange(NQ)[:, None]
    kj = np.arange(2 * NQ)[None, :]
    step = qi + NQ - kj
    band = (step >= 0) & (step <= NQ)
    buckets = np.stack([_t5_bucket_np(np.clip(step, 0, None) * dil).reshape(1, -1) for _, dil in ATTN_GROUPS])
    return band, buckets


def _bias_fwd(table_t, buckets):
    nb = buckets.shape[-1]

    def body(t_ref, b_ref, o_ref):
        oh = (b_ref[...] == lax.broadcasted_iota(jnp.int32, (N_BUCKETS, nb), 0)).astype(F32)
        o_ref[...] = jnp.dot(t_ref[...], oh, preferred_element_type=F32, precision=lax.Precision.HIGHEST)

    return pl.pallas_call(
        body, name="bias_fwd", grid=(N_GROUPS,),
        in_specs=[pl.BlockSpec((None, 8, N_BUCKETS), lambda g: (g, 0, 0)), pl.BlockSpec((None, 1, nb), lambda g: (g, 0, 0))],
        out_specs=pl.BlockSpec((None, 8, nb), lambda g: (g, 0, 0)),
        out_shape=SDS((N_GROUPS, 8, nb), F32))(table_t, buckets)


def _bias_bwd(dsb, buckets):
    nb = buckets.shape[-1]

    def body(d_ref, b_ref, o_ref):
        oh = (b_ref[...] == lax.broadcasted_iota(jnp.int32, (N_BUCKETS, nb), 0)).astype(F32)
        o_ref[...] = lax.dot_general(d_ref[...], oh, NT_DIMS, preferred_element_type=F32, precision=lax.Precision.HIGHEST)

    return pl.pallas_call(
        body, name="bias_bwd", grid=(N_GROUPS,),
        in_specs=[pl.BlockSpec((None, 8, nb), lambda g: (g, 0, 0)), pl.BlockSpec((None, 1, nb), lambda g: (g, 0, 0))],
        out_specs=pl.BlockSpec((None, 8, N_BUCKETS), lambda g: (g, 0, 0)),
        out_shape=SDS((N_GROUPS, 8, N_BUCKETS), F32))(dsb, buckets)


P4 = 4


def _to_p4(dst_ref, h, val, scr):
    scr[...] = val
    for r in range(P4):
        dst_ref[h, r] = scr[pl.ds(r, TM // P4, stride=P4), :]


def _from_p4(src_ref, h, scr):
    for r in range(P4):
        scr[pl.ds(r, TM // P4, stride=P4), :] = src_ref[h, r]
    return scr[...]


def _qkv_prep(z, qw, kw):
    S = z.shape[0]
    nh = N_GROUPS * HPG

    def body(zq, zk, zv, qw_ref, kw_ref, qh, kh, vh, q2, k2, v2, scr):
        for h in range(nh):
            g = h // HPG
            sl = slice(h * HEAD, (h + 1) * HEAD)
            xq = zq[:, sl].astype(F32)
            q = xq * lax.rsqrt(jnp.mean(xq * xq, axis=-1, keepdims=True) + RMS_EPS) * qw_ref[g:g + 1, :]
            xk = zk[:, sl].astype(F32)
            k = xk * lax.rsqrt(jnp.mean(xk * xk, axis=-1, keepdims=True) + RMS_EPS) * kw_ref[g:g + 1, :]
            v = zv[:, sl].astype(F32)
            if h < HPG:
                qh[h], kh[h], vh[h] = q, k, v
            else:
                _to_p4(q2, h - HPG, q, scr)
                _to_p4(k2, h - HPG, k, scr)
                _to_p4(v2, h - HPG, v, scr)

    hm = pl.BlockSpec((HPG, TM, HEAD), lambda i: (0, i, 0))
    p4 = pl.BlockSpec((2 * HPG, P4, TM // P4, HEAD), lambda i: (0, 0, i, 0))
    return pl.pallas_call(
        body, name="qkv_prep", grid=(S // TM,),
        in_specs=[pl.BlockSpec((TM, QKV_W), lambda i: (i, 0)), pl.BlockSpec((TM, QKV_W), lambda i: (i, 1)),
                  pl.BlockSpec((TM, QKV_W), lambda i: (i, 2)), _full((N_GROUPS, HEAD)), _full((N_GROUPS, HEAD))],
        out_specs=[hm, hm, hm, p4, p4, p4],
        out_shape=[SDS((HPG, S, HEAD), F32)] * 3 + [SDS((2 * HPG, P4, S // P4, HEAD), F32)] * 3,
        scratch_shapes=[pltpu.VMEM((TM, HEAD), F32)])(z, z, z, qw, kw)


def _rows(start, d):
    return pl.ds(start, NQ) if d == 1 else pl.ds(start, NQ, stride=d)


def _window(buf, ref, RB, start, d, single):
    if not single:
        return buf[_rows(start, d), :]
    at = start - RB
    return ref[_rows(at if at >= 0 else at + NQ * d, d), :]


def _attn_fwd(qh, kh, vh, biasm, name, d, head0, nseq, bias_shift):
    S = qh.shape[1]
    RB = min(ATT_RB, S)
    nbk, nq = S // RB, RB // (NQ * d)

    single = nbk == 1

    def body(q_ref, k_ref, v_ref, bias_ref, o_ref, lse_ref, kbuf, vbuf):
        b = pl.program_id(1)
        if not single:
            @pl.when(b == 0)
            def _():
                kbuf[0:RB, :] = jnp.zeros((RB, HEAD), F32)
                vbuf[0:RB, :] = jnp.zeros((RB, HEAD), F32)

            @pl.when(b > 0)
            def _():
                kbuf[0:RB, :] = kbuf[RB:2 * RB, :]
                vbuf[0:RB, :] = vbuf[RB:2 * RB, :]

            kbuf[RB:2 * RB, :] = k_ref[...]
            vbuf[RB:2 * RB, :] = v_ref[...]
        bias = bias_ref[...]
        col = lax.broadcasted_iota(jnp.int32, (NQ, 2 * NQ), 1)

        for qb in range(nq):
            def unit(r, carry, qb=qb):
                qs = qb * NQ * d + r
                q = q_ref[_rows(qs, d), :].astype(BF16)
                kw = jnp.concatenate([_window(kbuf, k_ref, RB, RB + qs - NQ * d, d, single),
                                      _window(kbuf, k_ref, RB, RB + qs, d, single)], axis=0).astype(BF16)
                vw = jnp.concatenate([_window(vbuf, v_ref, RB, RB + qs - NQ * d, d, single),
                                      _window(vbuf, v_ref, RB, RB + qs, d, single)], axis=0).astype(BF16)
                s = lax.dot_general(q, kw, NT_DIMS, preferred_element_type=F32) * SCALE + bias
                if qb == 0:
                    s = jnp.where((col < NQ) & (b == 0), NEG, s)
                m = jnp.max(s, axis=-1, keepdims=True)
                p = jnp.exp(s - m)
                l = jnp.sum(p, axis=-1, keepdims=True)
                o = jnp.dot(p.astype(BF16), vw, preferred_element_type=F32) / l
                o_ref[_rows(qs, d), :] = o
                lse_ref[_rows(qs, d), :] = jnp.broadcast_to(m + jnp.log(l), (NQ, HEAD))
                return carry

            for r in range(d):
                unit(r, 0)

    blk = lambda f: pl.BlockSpec((None, RB, HEAD), f)
    return pl.pallas_call(
        body, name=name, grid=(nseq, nbk),
        in_specs=[blk(lambda h, b: (head0 + h, b, 0))] * 3
        + [pl.BlockSpec((None, NQ, 2 * NQ), lambda h, b: (jnp.right_shift(h, bias_shift), 0, 0))],
        out_specs=[blk(lambda h, b: (h, b, 0))] * 2,
        out_shape=[SDS((nseq, S, HEAD), F32)] * 2,
        scratch_shapes=[pltpu.VMEM((2 * RB, HEAD), F32)] * 2)(qh, kh, vh, biasm)


def _attn_bwd(qh, kh, vh, biasm, da, wg, dh, lse, name, d, head0, nseq, bias_shift):
    S = qh.shape[1]
    RB = min(ATT_RB, S)
    nbk, nq = S // RB, RB // (NQ * d)
    single = nbk == 1

    def body(q_ref, k_ref, v_ref, bias_ref, da_ref, wg_ref, dh_ref, lse_ref,
             dq_ref, dk_ref, dv_ref, dsb_ref, kbuf, vbuf, dkbuf, dvbuf):
        b = pl.program_id(1)
        zero = jnp.zeros((RB, HEAD), F32)

        @pl.when(b == 0)
        def _():
            if not single:
                kbuf[0:RB, :] = zero
                vbuf[0:RB, :] = zero
            dkbuf[0:RB, :] = zero
            dvbuf[0:RB, :] = zero
            dsb_ref[...] = jnp.zeros_like(dsb_ref)

        @pl.when(b > 0)
        def _():
            kbuf[0:RB, :] = kbuf[RB:2 * RB, :]
            vbuf[0:RB, :] = vbuf[RB:2 * RB, :]
            dkbuf[0:RB, :] = dkbuf[RB:2 * RB, :]
            dvbuf[0:RB, :] = dvbuf[RB:2 * RB, :]

        dkbuf[RB:2 * RB, :] = zero
        dvbuf[RB:2 * RB, :] = zero

        @pl.when(b < nbk)
        def _():
            if not single:
                kbuf[RB:2 * RB, :] = k_ref[...]
                vbuf[RB:2 * RB, :] = v_ref[...]
            bias = bias_ref[...]
            col = lax.broadcasted_iota(jnp.int32, (NQ, 2 * NQ), 1)

            for qb in range(nq):
                def unit(r, carry, qb=qb):
                    qs = qb * NQ * d + r
                    prev, cur = _rows(RB + qs - NQ * d, d), _rows(RB + qs, d)
                    q = q_ref[_rows(qs, d), :].astype(BF16)
                    kw = jnp.concatenate([_window(kbuf, k_ref, RB, RB + qs - NQ * d, d, single),
                                          _window(kbuf, k_ref, RB, RB + qs, d, single)], axis=0).astype(BF16)
                    vw = jnp.concatenate([_window(vbuf, v_ref, RB, RB + qs - NQ * d, d, single),
                                          _window(vbuf, v_ref, RB, RB + qs, d, single)], axis=0).astype(BF16)
                    s = lax.dot_general(q, kw, NT_DIMS, preferred_element_type=F32) * SCALE + bias
                    if qb == 0:
                        s = jnp.where((col < NQ) & (b == 0), NEG, s)
                    p = jnp.exp(s - lse_ref[_rows(qs, d), :][:, 0:1])
                    w = wg_ref[_rows(qs, d), :]
                    do = (da_ref[_rows(qs, d), :] * w).astype(BF16)
                    dp = lax.dot_general(do, vw, NT_DIMS, preferred_element_type=F32)
                    ds = p * (dp - w[:, 0:1] * dh_ref[_rows(qs, d), :][:, 0:1])
                    dsb_ref[...] += ds
                    dsb = ds.astype(BF16)
                    dq_ref[_rows(qs, d), :] = jnp.dot(dsb, kw, preferred_element_type=F32) * SCALE
                    dkw = lax.dot_general(dsb, q, TN_DIMS, preferred_element_type=F32) * SCALE
                    dvw = lax.dot_general(p.astype(BF16), do, TN_DIMS, preferred_element_type=F32)
                    dkbuf[prev, :] += dkw[0:NQ, :]
                    dkbuf[cur, :] += dkw[NQ:2 * NQ, :]
                    dvbuf[prev, :] += dvw[0:NQ, :]
                    dvbuf[cur, :] += dvw[NQ:2 * NQ, :]
                    return carry

                for r in range(d):
                    unit(r, 0)

        dk_ref[...] = dkbuf[done:done + RB, :]
        dv_ref[...] = dvbuf[done:done + RB, :]

    steps, done = (nbk + 1, 0) if nbk > 1 else (1, RB)
    blk = lambda f: pl.BlockSpec((None, RB, HEAD), f)
    cur_g = blk(lambda h, b: (head0 + h, jnp.minimum(b, nbk - 1), 0))
    cur = blk(lambda h, b: (h, jnp.minimum(b, nbk - 1), 0))
    prv = blk(lambda h, b: (h, jnp.maximum(b - 1, 0), 0))
    sq = pl.BlockSpec((None, NQ, 2 * NQ), lambda h, b: (h, 0, 0))
    return pl.pallas_call(
        body, name=name, grid=(nseq, steps),
        in_specs=[cur_g, cur_g, cur_g, pl.BlockSpec((None, NQ, 2 * NQ), lambda h, b: (jnp.right_shift(h, bias_shift), 0, 0)),
                  cur, cur, cur, cur],
        out_specs=[cur, prv, prv, sq],
        out_shape=[SDS((nseq, S, HEAD), F32)] * 3 + [SDS((nseq, NQ, 2 * NQ), F32)],
        scratch_shapes=[pltpu.VMEM((2 * RB, HEAD), F32)] * 4)(qh, kh, vh, biasm, da, wg, dh, lse)


def _merge_weights(l0, l1, l2):
    m = jnp.maximum(jnp.maximum(l0, l1), l2)
    e0, e1, e2 = jnp.exp(l0 - m), jnp.exp(l1 - m), jnp.exp(l2 - m)
    inv = 1.0 / (e0 + e1 + e2)
    return e0 * inv, e1 * inv, e2 * inv


def _merge_fwd(os_, lses):
    S = os_[0].shape[1]

    def body(o0, o1, o2, l0, l1, l2, a_ref, so, sl):
        for h in range(HPG):
            w0, w1, w2 = _merge_weights(l0[h], _from_p4(l1, h, so), _from_p4(l2, h, sl))
            a_ref[:, h * HEAD:(h + 1) * HEAD] = (w0 * o0[h] + w1 * _from_p4(o1, h, so) + w2 * _from_p4(o2, h, sl)).astype(BF16)

    hm = pl.BlockSpec((HPG, TM, HEAD), lambda i: (0, i, 0))
    p4 = pl.BlockSpec((HPG, P4, TM // P4, HEAD), lambda i: (0, 0, i, 0))
    return pl.pallas_call(
        body, name="merge_fwd", grid=(S // TM,), in_specs=[hm, p4, p4, hm, p4, p4],
        out_specs=pl.BlockSpec((TM, CH), lambda i: (i, 0)),
        out_shape=SDS((S, CH), BF16), scratch_shapes=[pltpu.VMEM((TM, HEAD), F32)] * 2)(*os_, *lses)


def _merge_bwd(dattn, os_, lses):
    S = dattn.shape[0]

    def body(da_ref, o0, o1, o2, l0, l1, l2, w0_ref, w1_ref, w2_ref, dah_ref, dh_ref, dah2_ref, dh2_ref, so, sl):
        for h in range(HPG):
            w = _merge_weights(l0[h], _from_p4(l1, h, so), _from_p4(l2, h, sl))
            attn = w[0] * o0[h] + w[1] * _from_p4(o1, h, so) + w[2] * _from_p4(o2, h, sl)
            da = da_ref[:, h * HEAD:(h + 1) * HEAD]
            dh = jnp.broadcast_to(jnp.sum(da * attn, axis=-1, keepdims=True), (TM, HEAD))
            w0_ref[h] = w[0]
            dah_ref[h] = da
            dh_ref[h] = dh
            _to_p4(w1_ref, h, w[1], so)
            _to_p4(w2_ref, h, w[2], so)
            _to_p4(dah2_ref, h, da, so)
            _to_p4(dh2_ref, h, dh, so)

    hm = pl.BlockSpec((HPG, TM, HEAD), lambda i: (0, i, 0))
    p4 = pl.BlockSpec((HPG, P4, TM // P4, HEAD), lambda i: (0, 0, i, 0))
    nat, perm = SDS((HPG, S, HEAD), F32), SDS((HPG, P4, S // P4, HEAD), F32)
    w0, w1, w2, dah, dh, dah2, dh2 = pl.pallas_call(
        body, name="merge_bwd", grid=(S // TM,),
        in_specs=[pl.BlockSpec((TM, CH), lambda i: (i, 0)), hm, p4, p4, hm, p4, p4],
        out_specs=[hm, p4, p4, hm, hm, p4, p4], out_shape=[nat, perm, perm, nat, nat, perm, perm],
        scratch_shapes=[pltpu.VMEM((TM, HEAD), F32)] * 2)(dattn, *os_, *lses)
    return (w0, dah, dh), (w1, dah2, dh2), (w2, dah2, dh2)


def _qkv_bwd(dz, z, dqs, dks, dvs, qw, kw):
    S = z.shape[0]
    nh = N_GROUPS * HPG

    def body(dz_in, zq, zk, *refs):
        del dz_in
        dq_refs, dk_refs, dv_refs = refs[0:3], refs[3:6], refs[6:9]
        qw_ref, kw_ref, dz_ref, dqw_ref, dkw_ref, scr = refs[9:]

        @pl.when(pl.program_id(0) == 0)
        def _():
            dqw_ref[...] = jnp.zeros_like(dqw_ref)
            dkw_ref[...] = jnp.zeros_like(dkw_ref)

        def grad(refs3, g, hh):
            return _from_p4(refs3[g], hh, scr) if g > 0 else refs3[g][hh]

        def nbwd(xr, dy, wr, dwr, h, off):
            g = h // HPG
            x = xr[:, h * HEAD:(h + 1) * HEAD].astype(F32)
            r = lax.rsqrt(jnp.mean(x * x, axis=-1, keepdims=True) + RMS_EPS)
            xhat = x * r
            dyw = dy * wr[g:g + 1, :]
            dz_ref[:, off + h * HEAD:off + (h + 1) * HEAD] = (
                r * (dyw - xhat * jnp.mean(dyw * xhat, axis=-1, keepdims=True))).astype(BF16)
            dwr[g:g + 1, :] += jnp.sum(dy * xhat, axis=0, keepdims=True)

        for h in range(nh):
            g, hh = h // HPG, h % HPG
            nbwd(zq, grad(dq_refs, g, hh), qw_ref, dqw_ref, h, O_Q)
            nbwd(zk, grad(dk_refs, g, hh), kw_ref, dkw_ref, h, O_K)
            dz_ref[:, O_V + h * HEAD:O_V + (h + 1) * HEAD] = grad(dv_refs, g, hh).astype(BF16)

    hm = pl.BlockSpec((HPG, TM, HEAD), lambda i: (0, i, 0))
    p4 = pl.BlockSpec((HPG, P4, TM // P4, HEAD), lambda i: (0, 0, i, 0))
    return pl.pallas_call(
        body, name="qkv_bwd", grid=(S // TM,),
        in_specs=[pl.BlockSpec(memory_space=pl.ANY), pl.BlockSpec((TM, QKV_W), lambda i: (i, 0)),
                  pl.BlockSpec((TM, QKV_W), lambda i: (i, 1))] + [hm, p4, p4] * 3 + [_full((N_GROUPS, HEAD)), _full((N_GROUPS, HEAD))],
        out_specs=[pl.BlockSpec((TM, 3 * QKV_W), lambda i: (i, 0)), _full((N_GROUPS, HEAD)), _full((N_GROUPS, HEAD))],
        out_shape=[SDS(dz.shape, BF16), SDS((N_GROUPS, HEAD), F32), SDS((N_GROUPS, HEAD), F32)],
        scratch_shapes=[pltpu.VMEM((TM, HEAD), F32)],
        input_output_aliases={0: 0})(dz, z, z, *dqs, *dks, *dvs, qw, kw)


def _conv_fwd(z, cw, cb, lnw, lnb):
    S = z.shape[0]
    H = 32

    def body(zv, zg, cw_ref, cb_ref, lnw_ref, lnb_ref, u1_ref, u3_ref, xbuf):
        i = pl.program_id(0)

        @pl.when(i == 0)
        def _():
            xbuf[0:H, :] = jnp.zeros((H, CH), F32)

        @pl.when(i > 0)
        def _():
            xbuf[0:H, :] = xbuf[TM:TM + H, :]

        xbuf[H:H + TM, :] = zv[...].astype(F32) * _sig(zg[...].astype(F32))
        for r0 in range(0, TM, CONV_RC):
            rows = pl.ds(r0, CONV_RC)
            parts = []
            for lanes in CONV_LANES:
                part = [jnp.broadcast_to(cb_ref[:, lanes], (CONV_RC, CONV_LB))]

                def tap(q, view, part=part, lanes=lanes):
                    part[0] = part[0] + view * cw_ref[q - 2:q - 1, lanes]

                _conv_taps(xbuf, r0, lanes, 2, CONV_K + 1, tap)
                parts.append(part[0])
            acc = jnp.concatenate(parts, axis=1)
            u1_ref[rows, :] = acc
            mu = jnp.mean(acc, axis=-1, keepdims=True)
            xc = acc - mu
            yl = xc * lax.rsqrt(jnp.mean(xc * xc, axis=-1, keepdims=True) + LN_EPS) * lnw_ref[...] + lnb_ref[...]
            u3_ref[rows, :] = (yl * _sig(yl)).astype(BF16)

    row = pl.BlockSpec((TM, CH), lambda i: (i, 0))
    return pl.pallas_call(
        body, name="conv_fwd", grid=(S // TM,),
        in_specs=[pl.BlockSpec((TM, CH), lambda i: (i, O_CV // CH)), pl.BlockSpec((TM, CH), lambda i: (i, O_CG // CH)),
                  _full((CONV_K, CH)), _full((1, CH)), _full((1, CH)), _full((1, CH))],
        out_specs=[row, row], out_shape=[SDS((S, CH), F32), SDS((S, CH), BF16)],
        scratch_shapes=[pltpu.VMEM((TM + H, CH), F32)])(z, z, cw, cb, lnw, lnb)


def _conv_bwd_a(du3, u1, z, lnw, lnb):
    S = z.shape[0]
    H = 32

    nt = S // TM

    def body(du3_ref, u1_ref, zv, zg, lnw_ref, lnb_ref, du1_ref, acc_ref, xbuf, tacc):
        i = pl.program_id(0)

        @pl.when(i == 0)
        def _():
            xbuf[0:H, :] = jnp.zeros((H, CH), F32)
            tacc[...] = jnp.zeros_like(tacc)

        @pl.when(i > 0)
        def _():
            xbuf[0:H, :] = xbuf[TM:TM + H, :]

        xbuf[H:H + TM, :] = zv[...].astype(F32) * _sig(zg[...].astype(F32))
        for r0 in range(0, TM, CONV_RC):
            rows = pl.ds(r0, CONV_RC)
            u1 = u1_ref[rows, :]
            mu = jnp.mean(u1, axis=-1, keepdims=True)
            xc = u1 - mu
            r = lax.rsqrt(jnp.mean(xc * xc, axis=-1, keepdims=True) + LN_EPS)
            yhat = xc * r
            yl = yhat * lnw_ref[...] + lnb_ref[...]
            sg = _sig(yl)
            dyl = du3_ref[rows, :] * (sg * (1.0 + yl * (1.0 - sg)))
            dyh = dyl * lnw_ref[...]
            du1 = r * (dyh - jnp.mean(dyh, axis=-1, keepdims=True) - yhat * jnp.mean(dyh * yhat, axis=-1, keepdims=True))
            du1_ref[rows, :] = du1
            tacc[33] += _fold8(dyl * yhat)
            tacc[34] += _fold8(dyl)
            tacc[32] += _fold8(du1)
            for lanes in CONV_LANES:
                d = du1[:, lanes]

                def tap(q, view, d=d, lanes=lanes):
                    tacc[q - 2, :, lanes] += _fold8(d * view)

                _conv_taps(xbuf, r0, lanes, 2, CONV_K + 1, tap)

        @pl.when(i == nt - 1)
        def _():
            for k in range(40):
                acc_ref[k:k + 1, :] = jnp.sum(tacc[k], axis=0, keepdims=True)

    row = pl.BlockSpec((TM, CH), lambda i: (i, 0))
    return pl.pallas_call(
        body, name="conv_bwd_a", grid=(nt,),
        in_specs=[row, row, pl.BlockSpec((TM, CH), lambda i: (i, O_CV // CH)), pl.BlockSpec((TM, CH), lambda i: (i, O_CG // CH)),
                  _full((1, CH)), _full((1, CH))],
        out_specs=[row, _full((40, CH))], out_shape=[SDS((S, CH), F32), SDS((40, CH), F32)],
        scratch_shapes=[pltpu.VMEM((TM + H, CH), F32), pltpu.VMEM((40, 8, CH), F32)])(du3, u1, z, z, lnw, lnb)


def _conv_bwd_b(dz, du1, z, cw):
    S = z.shape[0]
    nt = S // TM
    H = 32

    def body(dz_in, du1_ref, zv, zg, cw_ref, dz_ref, ybuf, dgate):
        del dz_in
        i, p = pl.program_id(0), pl.program_id(1)

        @pl.when(p == 0)
        def _():
            @pl.when(i == 0)
            def _():
                ybuf[TM:TM + H, :] = jnp.zeros((H, CH), F32)

            @pl.when(i > 0)
            def _():
                ybuf[TM:TM + H, :] = ybuf[0:H, :]

            ybuf[0:TM, :] = du1_ref[...]
            for r0 in range(0, TM, CONV_RC):
                rows = pl.ds(r0, CONV_RC)
                parts = []
                for lanes in CONV_LANES:
                    part = [jnp.zeros((CONV_RC, CONV_LB), F32)]

                    def tap(q, view, part=part, lanes=lanes):
                        part[0] = part[0] + view * cw_ref[CONV_K - 1 - q:CONV_K - q, lanes]

                    _conv_taps(ybuf, r0, lanes, 0, CONV_K - 1, tap)
                    parts.append(part[0])
                acc = jnp.concatenate(parts, axis=1)
                val = zv[rows, :].astype(F32)
                sg = _sig(zg[rows, :].astype(F32))
                dz_ref[rows, :] = (acc * sg).astype(BF16)
                dgate[rows, :] = (acc * val * sg * (1.0 - sg)).astype(BF16)

        @pl.when(p == 1)
        def _():
            dz_ref[...] = dgate[...]

    rev = lambda c: pl.BlockSpec((TM, CH), lambda i, p: (nt - 1 - i, c))
    return pl.pallas_call(
        body, name="conv_bwd_b", grid=(nt, 2),
        in_specs=[pl.BlockSpec(memory_space=pl.ANY), rev(0), rev(O_CV // CH), rev(O_CG // CH), _full((CONV_K, CH))],
        out_specs=pl.BlockSpec((TM, CH), lambda i, p: (nt - 1 - i, O_CV // CH + p)),
        out_shape=SDS(dz.shape, BF16),
        scratch_shapes=[pltpu.VMEM((TM + H, CH), F32), pltpu.VMEM((TM, CH), BF16)],
        input_output_aliases={0: 0})(dz, du1, z, z, cw)


def _memkv_fwd(mem, mnw, wkv, xkw):
    M, D = mem.shape

    def body(mem_ref, mnw_ref, w_ref, xkw_ref, mn_ref, kv_ref, mk_ref, mv_ref):
        x = mem_ref[...]
        mn = (x * lax.rsqrt(jnp.mean(x * x, axis=-1, keepdims=True) + RMS_EPS) * mnw_ref[...]).astype(BF16)
        mn_ref[...] = mn
        kv = jnp.dot(mn, w_ref[...], preferred_element_type=F32)
        kv_ref[...] = kv
        for h in range(HPG):
            k = kv[:, h * HEAD:(h + 1) * HEAD]
            mk_ref[:, h * HEAD:(h + 1) * HEAD] = (
                k * lax.rsqrt(jnp.mean(k * k, axis=-1, keepdims=True) + RMS_EPS) * xkw_ref[...]).astype(BF16)
        mv_ref[...] = kv[:, CH:2 * CH].astype(BF16)

    return pl.pallas_call(
        body, name="memkv_fwd",
        out_shape=[SDS((M, D), BF16), SDS((M, 2 * CH), F32), SDS((M, CH), BF16), SDS((M, CH), BF16)])(mem, mnw, wkv, xkw)


def _cross_q(zx, xqw, h):
    x = zx[:, h * HEAD:(h + 1) * HEAD].astype(F32)
    r = lax.rsqrt(jnp.mean(x * x, axis=-1, keepdims=True) + RMS_EPS)
    xhat = x * r
    return xhat, r, xhat * xqw


def _cross_fwd(z, xqw, mk, mv):
    S = z.shape[0]
    M = mk.shape[0]

    def body(zx, xqw_ref, mk_ref, mv_ref, o_ref):
        for h in range(HPG):
            sl = slice(h * HEAD, (h + 1) * HEAD)
            _, _, q = _cross_q(zx, xqw_ref[...], h)
            s = lax.dot_general(q.astype(BF16), mk_ref[:, sl], NT_DIMS, preferred_element_type=F32) * SCALE
            e = jnp.exp(s - jnp.max(s, axis=-1, keepdims=True))
            p = e / jnp.sum(e, axis=-1, keepdims=True)
            o_ref[:, sl] = jnp.dot(p.astype(BF16), mv_ref[:, sl], preferred_element_type=F32).astype(BF16)

    return pl.pallas_call(
        body, name="cross_fwd", grid=(S // TM,),
        in_specs=[pl.BlockSpec((TM, CH), lambda i: (i, O_XQ // CH)), _full((1, HEAD)), _full((M, CH)), _full((M, CH))],
        out_specs=pl.BlockSpec((TM, CH), lambda i: (i, 0)), out_shape=SDS((S, CH), BF16))(z, xqw, mk, mv)


def _cross_bwd(dz, doc, z, xqw, mk, mv):
    S = z.shape[0]
    M = mk.shape[0]

    def body(dz_in, do_ref, zx, xqw_ref, mk_ref, mv_ref, dz_ref, dmk_ref, dmv_ref, dxw_ref):
        del dz_in

        @pl.when(pl.program_id(0) == 0)
        def _():
            dmk_ref[...] = jnp.zeros_like(dmk_ref)
            dmv_ref[...] = jnp.zeros_like(dmv_ref)
            dxw_ref[...] = jnp.zeros_like(dxw_ref)

        for h in range(HPG):
            sl = slice(h * HEAD, (h + 1) * HEAD)
            xhat, r, q = _cross_q(zx, xqw_ref[...], h)
            qb = q.astype(BF16)
            s = lax.dot_general(qb, mk_ref[:, sl], NT_DIMS, preferred_element_type=F32) * SCALE
            e = jnp.exp(s - jnp.max(s, axis=-1, keepdims=True))
            p = e / jnp.sum(e, axis=-1, keepdims=True)
            do = do_ref[:, sl].astype(BF16)
            dp = lax.dot_general(do, mv_ref[:, sl], NT_DIMS, preferred_element_type=F32)
            ds = (p * (dp - jnp.sum(p * dp, axis=-1, keepdims=True)) * SCALE).astype(BF16)
            dq = jnp.dot(ds, mk_ref[:, sl], preferred_element_type=F32)
            dmk_ref[:, sl] += lax.dot_general(ds, qb, TN_DIMS, preferred_element_type=F32)
            dmv_ref[:, sl] += lax.dot_general(p.astype(BF16), do, TN_DIMS, preferred_element_type=F32)
            dyw = dq * xqw_ref[...]
            dz_ref[:, sl] = (r * (dyw - xhat * jnp.mean(dyw * xhat, axis=-1, keepdims=True))).astype(BF16)
            dxw_ref[...] += jnp.sum(dq * xhat, axis=0, keepdims=True)

    return pl.pallas_call(
        body, name="cross_bwd", grid=(S // TM,),
        in_specs=[pl.BlockSpec(memory_space=pl.ANY), pl.BlockSpec((TM, CH), lambda i: (i, 0)),
                  pl.BlockSpec((TM, CH), lambda i: (i, O_XQ // CH)), _full((1, HEAD)), _full((M, CH)), _full((M, CH))],
        out_specs=[pl.BlockSpec((TM, CH), lambda i: (i, O_XQ // CH)), _full((M, CH)), _full((M, CH)), _full((1, HEAD))],
        out_shape=[SDS(dz.shape, BF16), SDS((M, CH), F32), SDS((M, CH), F32), SDS((1, HEAD), F32)],
        input_output_aliases={0: 0})(dz, doc, z, xqw, mk, mv)


def _memkv_bwd(dmk, dmv, kv, mem, mn, mnw, wkv, xkw):
    M, D = mem.shape

    def body(dmk_ref, dmv_ref, kv_ref, mem_ref, mn_ref, mnw_ref, w_ref, xkw_ref, dw_ref, dxk_ref, dmn_ref, dkv):
        dxk = jnp.zeros((1, HEAD), F32)
        for h in range(HPG):
            sl = slice(h * HEAD, (h + 1) * HEAD)
            k = kv_ref[:, sl]
            r = lax.rsqrt(jnp.mean(k * k, axis=-1, keepdims=True) + RMS_EPS)
            khat = k * r
            dy = dmk_ref[:, sl]
            dyw = dy * xkw_ref[...]
            dkv[:, sl] = (r * (dyw - khat * jnp.mean(dyw * khat, axis=-1, keepdims=True))).astype(BF16)
            dxk = dxk + jnp.sum(dy * khat, axis=0, keepdims=True)
        dxk_ref[...] = dxk
        dkv[:, CH:2 * CH] = dmv_ref[...].astype(BF16)
        dw_ref[...] = lax.dot_general(mn_ref[...], dkv[...], TN_DIMS, preferred_element_type=F32)
        dn = lax.dot_general(dkv[...], w_ref[...], NT_DIMS, preferred_element_type=F32)
        x = mem_ref[...]
        r = lax.rsqrt(jnp.mean(x * x, axis=-1, keepdims=True) + RMS_EPS)
        dmn_ref[...] = jnp.sum(dn * x * r, axis=0, keepdims=True)

    return pl.pallas_call(
        body, name="memkv_bwd",
        out_shape=[SDS((D, 2 * CH), F32), SDS((1, HEAD), F32), SDS((1, D), F32)],
        scratch_shapes=[pltpu.VMEM((M, 2 * CH), BF16)])(dmk, dmv, kv, mem, mn, mnw, wkv, xkw)


def _branch_proj(a_ref, w_ref, y_ref):
    G, _, n = w_ref.shape
    a = a_ref[...]
    for g in range(G):
        y_ref[:, g * n:(g + 1) * n] = jnp.dot(a, w_ref[g], preferred_element_type=F32)


OUT_RC = 16


def _gates(zg_ref, bg_ref, rows, k, D):
    return _sig(zg_ref[rows, k * D:(k + 1) * D].astype(F32) + bg_ref[:, k * D:(k + 1) * D])


def _outproj_fwd(x, z, bg, attn, u3, oc, wao, wco, wxo, wout, fnw):
    S, D = x.shape
    tm = TM

    def body(x_ref, zg_ref, bg_ref, a_ref, u_ref, c_ref, wa, wc, wx, wo, fnw_ref, h1_ref, hn_ref, ya, yc, yx, mg):
        _branch_proj(a_ref, wa, ya)
        _branch_proj(u_ref, wc, yc)
        _branch_proj(c_ref, wx, yx)
        for r0 in range(0, tm, OUT_RC):
            rows = pl.ds(r0, OUT_RC)
            mg[rows, :] = (_gates(zg_ref, bg_ref, rows, 0, D) * ya[rows, :] + _gates(zg_ref, bg_ref, rows, 1, D) * yc[rows, :]
                           + _gates(zg_ref, bg_ref, rows, 2, D) * yx[rows, :]).astype(BF16)
        ya[...] = jnp.dot(mg[...], wo[...], preferred_element_type=F32)
        for r0 in range(0, tm, OUT_RC):
            rows = pl.ds(r0, OUT_RC)
            h1 = x_ref[rows, :] + ya[rows, :]
            h1_ref[rows, :] = h1
            hn_ref[rows, :] = (h1 * lax.rsqrt(jnp.mean(h1 * h1, axis=-1, keepdims=True) + RMS_EPS) * fnw_ref[...]).astype(BF16)

    row = lambda w: pl.BlockSpec((tm, w), lambda i: (i, 0))
    return pl.pallas_call(
        body, name="outproj_fwd", grid=(S // tm,),
        in_specs=[row(D), pl.BlockSpec((tm, 3 * D), lambda i: (i, O_G // (3 * D))), _full((1, 3 * D)), row(CH), row(CH), row(CH),
                  _full(wao.shape), _full(wco.shape), _full(wxo.shape), _full((D, D)), _full((1, D))],
        out_specs=[row(D), row(D)], out_shape=[SDS((S, D), F32), SDS((S, D), BF16)],
        scratch_shapes=[pltpu.VMEM((tm, D), F32)] * 3 + [pltpu.VMEM((tm, D), BF16)])(x, z, bg, attn, u3, oc, wao, wco, wxo, wout, fnw)


def _outproj_bwd(dh1, z, bg, attn, u3, oc, wao, wco, wxo, wout, n_in):
    S, D = dh1.shape
    tm = 256
    nt = S // tm
    G, _, n = wao.shape

    def body(dh_ref, zg_ref, bg_ref, a_ref, u_ref, c_ref, wa, wc, wx, wo,
             dz_ref, da_ref, du_ref, dc_ref, dbg_ref, dwo_ref, dwa_ref, dwc_ref, dwx_ref,
             ya, yc, yx, dm, dy, mg, bacc, wacc):
        i = pl.program_id(0)

        @pl.when(i == 0)
        def _():
            bacc[...] = jnp.zeros_like(bacc)
            wacc[...] = jnp.zeros_like(wacc)
            dwo_ref[...] = jnp.zeros_like(dwo_ref)

        _branch_proj(a_ref, wa, ya)
        _branch_proj(u_ref, wc, yc)
        _branch_proj(c_ref, wx, yx)
        dhb = dh_ref[...].astype(BF16)
        dm[...] = lax.dot_general(dhb, wo[...], NT_DIMS, preferred_element_type=F32)
        for r0 in range(0, tm, OUT_RC):
            rows = pl.ds(r0, OUT_RC)
            dmv = dm[rows, :]
            merged = jnp.zeros((OUT_RC, D), F32)
            for k, y in enumerate((ya, yc, yx)):
                gk = _gates(zg_ref, bg_ref, rows, k, D)
                yk = y[rows, :]
                merged = merged + gk * yk
                dzg = dmv * yk * gk * (1.0 - gk)
                dz_ref[rows, k * D:(k + 1) * D] = dzg.astype(BF16)
                bacc[:, k * D:(k + 1) * D] += _fold8(dzg)
                dy[k, rows, :] = (dmv * gk).astype(BF16)
            mg[rows, :] = merged.astype(BF16)
        dwo_ref[...] += lax.dot_general(mg[...], dhb, TN_DIMS, preferred_element_type=F32)
        for k, (b_ref, w_ref, db_ref) in enumerate(((a_ref, wa, da_ref), (u_ref, wc, du_ref), (c_ref, wx, dc_ref))):
            dyk = dy[k]
            acc = jnp.zeros((tm, CH), F32)
            for g in range(G):
                acc = acc + lax.dot_general(dyk[:, g * n:(g + 1) * n], w_ref[g], NT_DIMS, preferred_element_type=F32)
            db_ref[...] = acc
            wacc[k] += lax.dot_general(b_ref[...], dyk, TN_DIMS, preferred_element_type=F32)

        @pl.when(i == nt - 1)
        def _():
            dbg_ref[...] = jnp.sum(bacc[...], axis=0, keepdims=True)
            for k, dw_ref in enumerate((dwa_ref, dwc_ref, dwx_ref)):
                for g in range(G):
                    dw_ref[g] = wacc[k, :, g * n:(g + 1) * n]

    row = lambda w: pl.BlockSpec((tm, w), lambda i: (i, 0))
    return pl.pallas_call(
        body, name="outproj_bwd", grid=(nt,),
        in_specs=[row(D), pl.BlockSpec((tm, 3 * D), lambda i: (i, O_G // (3 * D))), _full((1, 3 * D)), row(CH), row(CH), row(CH),
                  _full(wao.shape), _full(wco.shape), _full(wxo.shape), _full((D, D))],
        out_specs=[pl.BlockSpec((tm, 3 * D), lambda i: (i, O_G // (3 * D))), row(CH), row(CH), row(CH), _full((1, 3 * D)),
                   _full((D, D))] + [_full(wao.shape)] * 3,
        out_shape=[SDS((S, n_in), BF16)] + [SDS((S, CH), F32)] * 3 + [SDS((1, 3 * D), F32), SDS((D, D), F32)]
        + [SDS(wao.shape, F32)] * 3,
        scratch_shapes=[pltpu.VMEM((tm, D), F32)] * 4 + [pltpu.VMEM((3, tm, D), BF16), pltpu.VMEM((tm, D), BF16),
                                                        pltpu.VMEM((8, 3 * D), F32), pltpu.VMEM((3, CH, D), F32)],
    )(dh1, z, bg, attn, u3, oc, wao, wco, wxo, wout)


FFN_TC = 256
FFN_H = 8


def _ffn_taps(buf, r0):
    xx = buf[pl.ds(r0, FFN_RC + FFN_H), :]
    return xx[FFN_H:], pltpu.roll(xx, 1, 0)[FFN_H:], pltpu.roll(xx, 2, 0)[FFN_H:]


def _ffn_conv(taps, w_ref, b_ref):
    x0, x1, x2 = taps
    return b_ref[...] + x0 * w_ref[2:3, :] + x1 * w_ref[1:2, :] + x2 * w_ref[0:1, :]


def _ffn_fwd(up, cw, cb, wdown, h1, target):
    S, D = h1.shape
    F2 = up.shape[1]
    F = F2 // 2
    nj = F // FFN_TC
    tm = TM

    def body(up_ref, halo_ref, cw_ref, cb_ref, w_ref, h_ref, t_ref, dy_ref, loss_ref, ac_ref, gc_ref, abuf, gbuf, act_s):
        i = pl.program_id(0)

        @pl.when(i == 0)
        def _():
            loss_ref[...] = jnp.zeros_like(loss_ref)

        for j in range(nj):
            ca, cg = slice(j * FFN_TC, (j + 1) * FFN_TC), slice(F + j * FFN_TC, F + (j + 1) * FFN_TC)
            first = i == 0
            abuf[0:FFN_H, :] = jnp.where(first, 0.0, halo_ref[:, ca].astype(F32))
            gbuf[0:FFN_H, :] = jnp.where(first, 0.0, halo_ref[:, cg].astype(F32))
            abuf[FFN_H:FFN_H + tm, :] = up_ref[:, ca].astype(F32)
            gbuf[FFN_H:FFN_H + tm, :] = up_ref[:, cg].astype(F32)
            for r0 in range(0, tm, FFN_RC):
                rows = pl.ds(r0, FFN_RC)
                a = _ffn_conv(_ffn_taps(abuf, r0), cw_ref[:, ca], cb_ref[:, ca])
                gt = _ffn_conv(_ffn_taps(gbuf, r0), cw_ref[:, cg], cb_ref[:, cg])
                ac_ref[rows, ca] = a.astype(BF16)
                gc_ref[rows, ca] = gt.astype(BF16)
                act_s[rows, ca] = (gt * _sig(gt) * a).astype(BF16)
        err = h_ref[...] + jnp.dot(act_s[...], w_ref[...], preferred_element_type=F32) - t_ref[...]
        dy_ref[...] = err * (1.0 / D)
        loss_ref[...] += 0.5 * jnp.sum(jnp.mean(err * err, axis=-1, keepdims=True))

    row = lambda w: pl.BlockSpec((tm, w), lambda i: (i, 0))
    halo = pl.BlockSpec((FFN_H, F2), lambda i: (jnp.maximum(i * (tm // FFN_H) - 1, 0), 0))
    return pl.pallas_call(
        body, name="ffn_fwd", grid=(S // tm,),
        in_specs=[row(F2), halo, _full((FFN_K, F2)), _full((1, F2)), _full((F, D)), row(D), row(D)],
        out_specs=[row(D), _full((8, 128)), row(F), row(F)],
        out_shape=[SDS((S, D), F32), SDS((8, 128), F32), SDS((S, F), BF16), SDS((S, F), BF16)],
        scratch_shapes=[pltpu.VMEM((tm + FFN_H, FFN_TC), F32)] * 2 + [pltpu.VMEM((tm, F), BF16)])(up, up, cw, cb, wdown, h1, target)


def _ffn_bwd_a(dy, wdown, ac, gc):
    S, D = dy.shape
    F = ac.shape[1]
    nj = F // FFN_TC
    tm = 1024 if S % 1024 == 0 else TM

    def body(dy_ref, wd_ref, a_ref, g_ref, da_ref, dg_ref, acca_ref, accg_ref, dwd_ref, dact_s, act_s):
        i, j = pl.program_id(0), pl.program_id(1)

        @pl.when((i == 0) & (j == 0))
        def _():
            acca_ref[...] = jnp.zeros_like(acca_ref)
            accg_ref[...] = jnp.zeros_like(accg_ref)
            dwd_ref[...] = jnp.zeros_like(dwd_ref)

        dyb = dy_ref[...].astype(BF16)
        dact_s[...] = lax.dot_general(dyb, wd_ref[...], NT_DIMS, preferred_element_type=F32)
        pa = pg = jnp.zeros((8, FFN_TC), F32)
        for r0 in range(0, tm, FFN_RC):
            rows = pl.ds(r0, FFN_RC)
            a = a_ref[rows, :].astype(F32)
            gt = g_ref[rows, :].astype(F32)
            dact = dact_s[rows, :]
            sg = _sig(gt)
            silu = gt * sg
            act_s[rows, :] = (silu * a).astype(BF16)
            dac = dact * silu
            dgc = dact * a * (sg * (1.0 + gt * (1.0 - sg)))
            da_ref[rows, :] = dac.astype(BF16)
            dg_ref[rows, :] = dgc.astype(BF16)
            pa = pa + _fold8(dac)
            pg = pg + _fold8(dgc)
        acca_ref[j] += pa
        accg_ref[j] += pg
        dwd_ref[pl.ds(pl.multiple_of(j * FFN_TC, FFN_TC), FFN_TC), :] += lax.dot_general(
            act_s[...], dyb, TN_DIMS, preferred_element_type=F32)

    col = pl.BlockSpec((tm, FFN_TC), lambda i, j: (i, j))
    return pl.pallas_call(
        body, name="ffn_bwd_a", grid=(S // tm, nj),
        in_specs=[pl.BlockSpec((tm, D), lambda i, j: (i, 0)), pl.BlockSpec((FFN_TC, D), lambda i, j: (j, 0)), col, col],
        out_specs=[col, col] + [_full((nj, 8, FFN_TC))] * 2 + [_full((F, D))],
        out_shape=[SDS((S, F), BF16)] * 2 + [SDS((nj, 8, FFN_TC), F32)] * 2 + [SDS((F, D), F32)],
        scratch_shapes=[pltpu.VMEM((tm, FFN_TC), F32), pltpu.VMEM((tm, FFN_TC), BF16)])(dy, wdown, ac, gc)


def _ffn_bwd_b(dca, dcg, up, cw):
    S, F = dca.shape
    nj = F // FFN_TC
    tm = 4096 if S % 4096 == 0 else TM
    nt = S // tm
    span = FFN_RC + FFN_H

    def body(a_ref, g_ref, u_ref, w_ref, o_ref, tacc_ref, ybuf):
        j, i = pl.program_id(0), pl.program_id(1)

        @pl.when(i == 0)
        def _():
            ybuf[tm:tm + FFN_H, :] = jnp.zeros((FFN_H, FFN_TC), F32)
            tacc_ref[...] = jnp.zeros_like(tacc_ref)

        @pl.when(i > 0)
        def _():
            ybuf[tm:tm + FFN_H, :] = ybuf[0:FFN_H, :]

        ybuf[0:tm, :] = jnp.where(j < nj, a_ref[...], g_ref[...]).astype(F32)
        p = [jnp.zeros((8, FFN_TC), F32)] * FFN_K
        for r0 in range(0, tm, FFN_RC):
            rows = pl.ds(r0, FFN_RC)
            yy = ybuf[pl.ds(r0, span), :]
            ys = (yy[:FFN_RC], pltpu.roll(yy, span - 1, 0)[:FFN_RC], pltpu.roll(yy, span - 2, 0)[:FFN_RC])
            o_ref[rows, :] = (ys[0] * w_ref[2:3, :] + ys[1] * w_ref[1:2, :] + ys[2] * w_ref[0:1, :]).astype(BF16)
            u = u_ref[rows, :].astype(F32)
            for k in range(FFN_K):
                p[k] = p[k] + _fold8(ys[FFN_K - 1 - k] * u)
        for k in range(FFN_K):
            tacc_ref[k] += p[k]

    rev = lambda f: pl.BlockSpec((tm, FFN_TC), lambda j, i: (nt - 1 - i, f(j)))
    return pl.pallas_call(
        body, name="ffn_bwd_b", grid=(2 * nj, nt),
        in_specs=[rev(lambda j: jnp.minimum(j, nj - 1)), rev(lambda j: jnp.maximum(j - nj, 0)), rev(lambda j: j),
                  pl.BlockSpec((FFN_K, FFN_TC), lambda j, i: (0, j))],
        out_specs=[rev(lambda j: j), pl.BlockSpec((None, FFN_K, 8, FFN_TC), lambda j, i: (j, 0, 0, 0))],
        out_shape=[SDS((S, 2 * F), BF16), SDS((2 * nj, FFN_K, 8, FFN_TC), F32)],
        scratch_shapes=[pltpu.VMEM((tm + FFN_H, FFN_TC), F32)])(dca, dcg, up, cw)


def _adamw_update(w_ref, g_ref, m_ref, v_ref, d_ref, nm_ref, nv_ref):
    gv = g_ref[...]
    m2 = ADAM_B1 * m_ref[...] + (1.0 - ADAM_B1) * gv
    v2 = ADAM_B2 * v_ref[...] + (1.0 - ADAM_B2) * jnp.square(gv)
    m_hat = m2 / (1.0 - ADAM_B1 ** ADAM_STEP)
    v_hat = v2 / (1.0 - ADAM_B2 ** ADAM_STEP)
    d_ref[...] = -ADAM_LR * (m_hat / (jnp.sqrt(v_hat) + ADAM_EPS) + ADAM_WD * w_ref[...])
    nm_ref[...] = m2
    nv_ref[...] = v2


def _adamw_small(ws, gs, ms, vs):
    n = len(ws)

    def body(*refs):
        for i in range(n):
            _adamw_update(*[refs[k * n + i] for k in range(7)])

    shapes = [SDS(w.shape, F32) for w in ws]
    res = pl.pallas_call(body, name="adamw_small", out_shape=shapes * 3)(*ws, *gs, *ms, *vs)
    return res[:n], res[n:2 * n], res[2 * n:]


def _adamw(w, g, m, v, name):
    R, C = w.shape
    tr = _row_tile(R, max(8, (2 ** 20) // (4 * C) // 8 * 8))

    def body(w_ref, g_ref, m_ref, v_ref, d_ref, nm_ref, nv_ref):
        _adamw_update(w_ref, g_ref, m_ref, v_ref, d_ref, nm_ref, nv_ref)

    blk = pl.BlockSpec((tr, C), lambda i: (i, 0))
    return pl.pallas_call(
        body, name=name, grid=(R // tr,), in_specs=[blk] * 4, out_specs=[blk] * 3,
        out_shape=[SDS((R, C), F32)] * 3)(w, g, m, v)


HBM_SPEC = pl.BlockSpec(memory_space=pltpu.HBM)
SEM_SPEC = pl.BlockSpec(memory_space=pltpu.SEMAPHORE)
DATAFLOW_EFFECT = pltpu.SideEffectType.DATAFLOW_SIDE_EFFECTING


def _position():
    return lax.axis_index("x"), lax.axis_index("y"), lax.axis_index("c")


def _other_chips(x, y):
    return [(1 - x, y), (x, 1 - y), (1 - x, 1 - y)]


SIBLING_BARRIER = 1


def _sibling_handshake(x, y, c):
    barrier = pltpu.get_barrier_semaphore()
    pl.semaphore_signal(barrier, inc=1, device_id=(x, y, 1 - c), device_id_type=MESH)
    pl.semaphore_wait(barrier, 1)


def _relations(x, y, js=(0, 1, 2)):
    return [(j, chip) for j, chip in enumerate(_other_chips(x, y)) if j in js]


def _ag_ici_start(arrs, name, js=(0, 1, 2), lands=None):
    n = len(arrs)

    def body(*refs):
        ins, lands = refs[:n], refs[n:2 * n]
        send_sems, recv_sems = refs[2 * n:2 * n + 2]
        token = refs[-1]
        x, y, c = _position()
        for i in range(n):
            for j, (px, py) in _relations(x, y, js):
                pltpu.make_async_remote_copy(src_ref=ins[i].at[c], dst_ref=lands[i].at[2 * x + y, c], send_sem=send_sems.at[3 * i + j],
                                             recv_sem=recv_sems.at[3 * i + j], device_id=(px, py, c), device_id_type=MESH).start()
        token[...] = jnp.zeros_like(token)

    if lands is None:
        lands = [pltpu.with_memory_space_constraint(lax.empty((4,) + a.shape, a.dtype), pltpu.HBM) for a in arrs]
        arrs = [pltpu.with_memory_space_constraint(a, pltpu.HBM) for a in arrs]
    res = pl.pallas_call(
        body, name=name,
        out_shape=[pltpu.SemaphoreType.DMA((3 * n,)), pltpu.SemaphoreType.DMA((3 * n,))]
        + [pltpu.HBM(a.shape, a.dtype) for a in arrs] + [pltpu.HBM(l.shape, l.dtype) for l in lands] + [SDS((8, 128), F32)],
        in_specs=[HBM_SPEC] * (2 * n), out_specs=[SEM_SPEC, SEM_SPEC] + [HBM_SPEC] * (2 * n) + [pl.BlockSpec(memory_space=pltpu.VMEM)],
        input_output_aliases={i: 2 + i for i in range(2 * n)},
        compiler_params=pltpu.CompilerParams(has_side_effects=DATAFLOW_EFFECT),
    )(*arrs, *lands)
    return res[0], res[1], list(res[2:2 + n]), list(res[2 + n:2 + 2 * n]), res[-1]


def _ag_ici_wait(send_sems, recv_sems, ins, lands, after, name, js=(0, 1, 2)):
    n = len(ins)

    def body(*refs):
        ins_r, lands_r = refs[:n], refs[n:2 * n]
        send_r, recv_r = refs[2 * n:2 * n + 2]
        x, y, c = _position()
        for i in range(n):
            for j, (px, py) in _relations(x, y, js):
                cp = pltpu.make_async_remote_copy(src_ref=ins_r[i].at[c], dst_ref=lands_r[i].at[2 * px + py, c], send_sem=send_r.at[3 * i + j],
                                                  recv_sem=recv_r.at[3 * i + j], device_id=(px, py, c), device_id_type=MESH)
                cp.wait_send()
                cp.wait_recv()

    res = pl.pallas_call(
        body, name=name,
        out_shape=[pltpu.HBM(a.shape, a.dtype) for a in list(ins) + list(lands)],
        in_specs=[HBM_SPEC] * (2 * n) + [SEM_SPEC, SEM_SPEC, pl.BlockSpec(memory_space=pl.ANY)], out_specs=[HBM_SPEC] * (2 * n),
        input_output_aliases={i: i for i in range(2 * n)},
        compiler_params=pltpu.CompilerParams(has_side_effects=DATAFLOW_EFFECT),
    )(*ins, *lands, send_sems, recv_sems, after)
    return list(res[:n]), list(res[n:])


def _ag_finish_start(arrs, lands, name, js=(0, 1, 2), own=True):
    n = len(arrs)

    def body(*refs):
        ins, bufs = refs[:n], refs[n:2 * n]
        send_sems, recv_sems = refs[2 * n:2 * n + 2]
        token = refs[-1]
        x, y, c = _position()
        _sibling_handshake(x, y, c)
        for i in range(n):
            if own:
                pltpu.make_async_remote_copy(src_ref=ins[i], dst_ref=bufs[i].at[2 * x + y], send_sem=send_sems.at[4 * i + 3],
                                             recv_sem=recv_sems.at[4 * i + 3], device_id=(x, y, 1 - c), device_id_type=MESH).start()
            for j, (px, py) in _relations(x, y, js):
                half = bufs[i].at[2 * px + py, c]
                pltpu.make_async_remote_copy(src_ref=half, dst_ref=half, send_sem=send_sems.at[4 * i + j],
                                             recv_sem=recv_sems.at[4 * i + j], device_id=(x, y, 1 - c), device_id_type=MESH).start()
        token[...] = jnp.zeros_like(token)

    res = pl.pallas_call(
        body, name=name,
        out_shape=[pltpu.SemaphoreType.DMA((4 * n,)), pltpu.SemaphoreType.DMA((4 * n,))]
        + [pltpu.HBM(a.shape, a.dtype) for a in list(arrs) + list(lands)] + [SDS((8, 128), F32)],
        in_specs=[HBM_SPEC] * (2 * n), out_specs=[SEM_SPEC, SEM_SPEC] + [HBM_SPEC] * (2 * n) + [pl.BlockSpec(memory_space=pltpu.VMEM)],
        input_output_aliases={i: 2 + i for i in range(2 * n)},
        compiler_params=pltpu.CompilerParams(has_side_effects=DATAFLOW_EFFECT, collective_id=SIBLING_BARRIER),
    )(*arrs, *lands)
    return res[0], res[1], list(res[2:2 + n]), list(res[2 + n:2 + 2 * n]), res[-1]


def _ag_finish_wait(send_sems, recv_sems, arrs, lands, after, name, js=(0, 1, 2), own=True):
    n = len(arrs)

    def body(*refs):
        ins, bufs = refs[:n], refs[n:2 * n]
        send_r, recv_r = refs[2 * n:2 * n + 2]
        x, y, c = _position()
        for i in range(n):
            if own:
                mine = pltpu.make_async_remote_copy(src_ref=ins[i], dst_ref=bufs[i].at[2 * x + y], send_sem=send_r.at[4 * i + 3],
                                                    recv_sem=recv_r.at[4 * i + 3], device_id=(x, y, 1 - c), device_id_type=MESH)
                mine.wait_send()
                mine.wait_recv()
            for j, (px, py) in _relations(x, y, js):
                pltpu.make_async_remote_copy(src_ref=bufs[i].at[2 * px + py, c], dst_ref=bufs[i].at[2 * px + py, c],
                                             send_sem=send_r.at[4 * i + j], recv_sem=recv_r.at[4 * i + j],
                                             device_id=(x, y, 1 - c), device_id_type=MESH).wait_send()
                pltpu.make_async_remote_copy(src_ref=bufs[i].at[2 * px + py, 1 - c], dst_ref=bufs[i].at[2 * px + py, 1 - c],
                                             send_sem=send_r.at[4 * i + j], recv_sem=recv_r.at[4 * i + j],
                                             device_id=(x, y, 1 - c), device_id_type=MESH).wait_recv()

    res = pl.pallas_call(
        body, name=name, out_shape=[pltpu.HBM(a.shape, a.dtype) for a in list(arrs) + list(lands)],
        in_specs=[HBM_SPEC] * (2 * n) + [SEM_SPEC, SEM_SPEC, pl.BlockSpec(memory_space=pl.ANY)], out_specs=[HBM_SPEC] * (2 * n),
        input_output_aliases={i: i for i in range(2 * n)},
        compiler_params=pltpu.CompilerParams(has_side_effects=DATAFLOW_EFFECT),
    )(*arrs, *lands, send_sems, recv_sems, after)
    return list(res[:n]), list(res[n:])


def _ag_finish(arrs, lands, name, js=(0, 1, 2), own=True):
    send_sems, recv_sems, arrs, lands, token = _ag_finish_start(arrs, lands, name + "_start", js, own)
    return _ag_finish_wait(send_sems, recv_sems, arrs, lands, token, name + "_wait", js, own)


def _swap_start(gs, name):
    n = len(gs)

    def body(*refs):
        ins, lands = refs[:n], refs[n:2 * n]
        send_sems, recv_sems = refs[2 * n:2 * n + 2]
        token = refs[-1]
        x, y, c = _position()
        _sibling_handshake(x, y, c)
        for i in range(n):
            for p in range(4):
                pltpu.make_async_remote_copy(src_ref=ins[i].at[p, 1 - c], dst_ref=lands[i].at[p], send_sem=send_sems.at[4 * i + p],
                                             recv_sem=recv_sems.at[4 * i + p], device_id=(x, y, 1 - c), device_id_type=MESH).start()
        token[...] = jnp.zeros_like(token)

    lands = [lax.empty((4,) + g.shape[2:], g.dtype) for g in gs]
    res = pl.pallas_call(
        body, name=name,
        out_shape=[pltpu.SemaphoreType.DMA((4 * n,)), pltpu.SemaphoreType.DMA((4 * n,))]
        + [pltpu.HBM(a.shape, a.dtype) for a in list(gs) + lands] + [SDS((8, 128), F32)],
        in_specs=[HBM_SPEC] * (2 * n), out_specs=[SEM_SPEC, SEM_SPEC] + [HBM_SPEC] * (2 * n) + [pl.BlockSpec(memory_space=pltpu.VMEM)],
        input_output_aliases={i: 2 + i for i in range(2 * n)},
        compiler_params=pltpu.CompilerParams(has_side_effects=DATAFLOW_EFFECT, collective_id=SIBLING_BARRIER),
    )(*[pltpu.with_memory_space_constraint(a, pltpu.HBM) for a in list(gs) + lands])
    return res[0], res[1], list(res[2:2 + n]), list(res[2 + n:2 + 2 * n]), res[-1]


def _swap_wait(send_sems, recv_sems, gs, lands, after, name):
    n = len(gs)

    def body(*refs):
        ins, lands_r = refs[:n], refs[n:2 * n]
        send_r, recv_r = refs[2 * n:2 * n + 2]
        x, y, c = _position()
        for i in range(n):
            for p in range(4):
                cp = pltpu.make_async_remote_copy(src_ref=ins[i].at[p, 1 - c], dst_ref=lands_r[i].at[p], send_sem=send_r.at[4 * i + p],
                                                  recv_sem=recv_r.at[4 * i + p], device_id=(x, y, 1 - c), device_id_type=MESH)
                cp.wait_send()
                cp.wait_recv()

    res = pl.pallas_call(
        body, name=name, out_shape=[pltpu.HBM(a.shape, a.dtype) for a in list(gs) + list(lands)],
        in_specs=[HBM_SPEC] * (2 * n) + [SEM_SPEC, SEM_SPEC, pl.BlockSpec(memory_space=pl.ANY)], out_specs=[HBM_SPEC] * (2 * n),
        input_output_aliases={i: i for i in range(2 * n)},
        compiler_params=pltpu.CompilerParams(has_side_effects=DATAFLOW_EFFECT),
    )(*gs, *lands, send_sems, recv_sems, after)
    return list(res[:n]), list(res[n:])


def _exchange_start(s1s, name):
    n = len(s1s)

    def body(*refs):
        srcs, lands = refs[:n], refs[n:2 * n]
        send_sems, recv_sems = refs[2 * n:2 * n + 2]
        token = refs[-1]
        x, y, c = _position()
        for i in range(n):
            for j, (px, py) in enumerate(_other_chips(x, y)):
                pltpu.make_async_remote_copy(src_ref=srcs[i].at[2 * px + py], dst_ref=lands[i].at[j], send_sem=send_sems.at[3 * i + j],
                                             recv_sem=recv_sems.at[3 * i + j], device_id=(px, py, c), device_id_type=MESH).start()
        token[...] = jnp.zeros_like(token)

    lands = [lax.empty((3,) + s.shape[1:], F32) for s in s1s]
    res = pl.pallas_call(
        body, name=name,
        out_shape=[pltpu.SemaphoreType.DMA((3 * n,)), pltpu.SemaphoreType.DMA((3 * n,))]
        + [pltpu.HBM(a.shape, F32) for a in list(s1s) + lands] + [SDS((8, 128), F32)],
        in_specs=[HBM_SPEC] * (2 * n), out_specs=[SEM_SPEC, SEM_SPEC] + [HBM_SPEC] * (2 * n) + [pl.BlockSpec(memory_space=pltpu.VMEM)],
        input_output_aliases={i: 2 + i for i in range(2 * n)},
        compiler_params=pltpu.CompilerParams(has_side_effects=DATAFLOW_EFFECT),
    )(*[pltpu.with_memory_space_constraint(a, pltpu.HBM) for a in list(s1s) + lands])
    return res[0], res[1], list(res[2:2 + n]), list(res[2 + n:2 + 2 * n]), res[-1]


def _exchange_wait(send_sems, recv_sems, s1s, lands, after, name):
    n = len(s1s)

    def body(*refs):
        srcs, lands_r = refs[:n], refs[n:2 * n]
        send_r, recv_r = refs[2 * n:2 * n + 2]
        x, y, c = _position()
        for i in range(n):
            for j, (px, py) in enumerate(_other_chips(x, y)):
                cp = pltpu.make_async_remote_copy(src_ref=srcs[i].at[2 * px + py], dst_ref=lands_r[i].at[j], send_sem=send_r.at[3 * i + j],
                                                  recv_sem=recv_r.at[3 * i + j], device_id=(px, py, c), device_id_type=MESH)
                cp.wait_send()
                cp.wait_recv()

    res = pl.pallas_call(
        body, name=name, out_shape=[pltpu.HBM(a.shape, F32) for a in list(s1s) + list(lands)],
        in_specs=[HBM_SPEC] * (2 * n) + [SEM_SPEC, SEM_SPEC, pl.BlockSpec(memory_space=pl.ANY)], out_specs=[HBM_SPEC] * (2 * n),
        input_output_aliases={i: i for i in range(2 * n)},
        compiler_params=pltpu.CompilerParams(has_side_effects=DATAFLOW_EFFECT),
    )(*s1s, *lands, send_sems, recv_sems, after)
    return list(res[:n]), list(res[n:])


def _join_halves(f2s, name):
    n = len(f2s)
    hbm = pl.BlockSpec(memory_space=pl.ANY)

    def body(*refs):
        ins, outs = refs[:n], refs[n:2 * n]
        send_sems, recv_sems = refs[2 * n:]
        x, y, c = _position()
        _sibling_handshake(x, y, c)
        cps = []
        for i in range(n):
            cp = pltpu.make_async_remote_copy(src_ref=ins[i].at[c], dst_ref=outs[i].at[c], send_sem=send_sems.at[i],
                                              recv_sem=recv_sems.at[i], device_id=(x, y, 1 - c), device_id_type=MESH)
            cp.start()
            cps.append(cp)
        for i in range(n):
            pltpu.make_async_remote_copy(src_ref=ins[i].at[1 - c], dst_ref=outs[i].at[1 - c], send_sem=send_sems.at[i],
                                         recv_sem=recv_sems.at[i], device_id=(x, y, 1 - c), device_id_type=MESH).wait_recv()
        for cp in cps:
            cp.wait_send()

    return pl.pallas_call(
        body, name=name, in_specs=[hbm] * n, out_specs=[hbm] * n, out_shape=[SDS(f.shape, f.dtype) for f in f2s],
        input_output_aliases={i: i for i in range(n)},
        scratch_shapes=[pltpu.SemaphoreType.DMA((n,)), pltpu.SemaphoreType.DMA((n,))],
        compiler_params=pltpu.CompilerParams(collective_id=SIBLING_BARRIER))(*f2s)


def _sum_tile(rows, cols):
    return _row_tile(rows, max(8, (2 ** 19 // cols) // 8 * 8))


def _add_pair(g, r1, c, name):
    _, R, C = r1.shape
    tr = _sum_tile(R, C)

    def body(c_ref, a_ref, b_ref, o_ref):
        del c_ref
        o_ref[...] = a_ref[...] + b_ref[...].astype(F32)

    blk = pl.BlockSpec((None, tr, C), lambda p, i, cr: (p, i, 0))
    return pl.pallas_call(
        body, name=name,
        grid_spec=pltpu.PrefetchScalarGridSpec(
            num_scalar_prefetch=1, grid=(4, R // tr),
            in_specs=[pl.BlockSpec((None, None, tr, C), lambda p, i, cr: (p, cr[0], i, 0)), blk], out_specs=blk),
        out_shape=SDS((4, R, C), F32))(c, g, r1)


def _add_four(s1, r2, me_c, name):
    _, R, C = s1.shape
    tr = _sum_tile(R, C)

    def body(m_ref, a_ref, b_ref, o_ref):
        del m_ref
        o_ref[...] = ((a_ref[...] + b_ref[0]) + b_ref[1]) + b_ref[2]

    return pl.pallas_call(
        body, name=name,
        grid_spec=pltpu.PrefetchScalarGridSpec(
            num_scalar_prefetch=1, grid=(R // tr,),
            in_specs=[pl.BlockSpec((None, tr, C), lambda i, mr: (mr[0], i, 0)), pl.BlockSpec((3, tr, C), lambda i, mr: (0, i, 0))],
            out_specs=pl.BlockSpec((None, tr, C), lambda i, mr: (mr[1], i, 0))),
        out_shape=SDS((2, R, C), F32))(me_c, s1, r2)


def _all_reduce_small(vs):
    n = len(vs)

    def body(*refs):
        ins, outs, bufs = refs[:n], refs[n:2 * n], refs[2 * n:3 * n]
        send_sems, recv_sems = refs[3 * n:]
        x, y, c = _position()
        me = 4 * x + 2 * y + c
        for i in range(n):
            bufs[i][me] = ins[i][...]
        cps = []
        for k in range(1, 8):
            to = (1 - x if k & 4 else x, 1 - y if k & 2 else y, 1 - c if k & 1 else c)
            for i in range(n):
                cp = pltpu.make_async_remote_copy(src_ref=bufs[i].at[me], dst_ref=bufs[i].at[me], send_sem=send_sems.at[7 * i + k - 1],
                                                  recv_sem=recv_sems.at[7 * i + k - 1], device_id=to, device_id_type=MESH)
                cp.start()
                cps.append(cp)
        for cp in cps:
            cp.wait_send()
        for k in range(1, 8):
            src = 4 * (1 - x if k & 4 else x) + 2 * (1 - y if k & 2 else y) + (1 - c if k & 1 else c)
            for i in range(n):
                pltpu.make_async_remote_copy(src_ref=bufs[i].at[src], dst_ref=bufs[i].at[src], send_sem=send_sems.at[7 * i + k - 1],
                                             recv_sem=recv_sems.at[7 * i + k - 1], device_id=(x, y, c), device_id_type=MESH).wait_recv()
        for i in range(n):
            acc = bufs[i][0]
            for k in range(1, 8):
                acc = acc + bufs[i][k]
            outs[i][...] = acc

    vm = pl.BlockSpec(memory_space=pltpu.VMEM)
    return pl.pallas_call(
        body, name="all_reduce_small", in_specs=[vm] * n, out_specs=[vm] * n, out_shape=[SDS(v.shape, F32) for v in vs],
        scratch_shapes=[pltpu.VMEM((8,) + v.shape, F32) for v in vs]
        + [pltpu.SemaphoreType.DMA((7 * n,)), pltpu.SemaphoreType.DMA((7 * n,))])(*vs)


def _reduce_begin(grads, tag, wires=None):
    halves = lambda g: g.reshape(4, 2, g.shape[1] // 2, g.shape[2])
    g4 = [halves(g) for g in grads]
    send_sems, recv_sems, sent, lands, token = _swap_start(g4 if wires is None else [halves(g) for g in wires], "rs_swap_start_" + tag)
    return (send_sems, recv_sems, sent, lands, None if wires is None else g4), token


def _reduce_mid(state, after, tag):
    _, _, c = _position()
    cs = jnp.reshape(c, (1,)).astype(jnp.int32)
    send_sems, recv_sems, sent, lands, g4 = state
    sent, r1 = _swap_wait(send_sems, recv_sems, sent, lands, after, "rs_swap_wait_" + tag)
    g4 = sent if g4 is None else g4
    s1 = [_add_pair(g, r, cs, f"rs_add_pair_{tag}{i}") for i, (g, r) in enumerate(zip(g4, r1))]
    send_sems, recv_sems, s1, lands, token = _exchange_start(s1, "rs_exchange_start_" + tag)
    return (send_sems, recv_sems, s1, lands), token


def _reduce_end(state, after, tag):
    x, y, c = _position()
    send_sems, recv_sems, s1, lands = state
    s1, lands = _exchange_wait(send_sems, recv_sems, s1, lands, after, "rs_exchange_wait_" + tag)
    me_c = jnp.stack([2 * x + y, c]).astype(jnp.int32)
    f2 = [_add_four(s, l, me_c, f"rs_add_four_{tag}{i}") for i, (s, l) in enumerate(zip(s1, lands))]
    return [f.reshape(2 * f.shape[1], f.shape[2]) for f in _join_halves(f2, "rs_join_" + tag)]


def _halves(a):
    return a.reshape((2, a.shape[0] // 2) + a.shape[1:])


def _after(a, token):
    return a + token[0, 0]


def _in_proj_own(x, nw, w, part):
    M, K = x.shape
    n = w.shape[2]
    sub = 256

    def body(p_ref, x_ref, nw_ref, w_ref, xn_ref, z_ref):
        del p_ref
        for r0 in range(0, MM_TM, sub):
            rows = pl.ds(r0, sub)
            for c0 in range(r0, r0 + sub, NORM_RC):
                ch = pl.ds(c0, NORM_RC)
                xv = x_ref[ch, :]
                xn_ref[ch, :] = (xv * lax.rsqrt(jnp.mean(xv * xv, axis=-1, keepdims=True) + RMS_EPS) * nw_ref[...]).astype(BF16)
            z_ref[rows, :] = jnp.dot(xn_ref[rows, :], w_ref[...], preferred_element_type=F32).astype(BF16)

    row = pl.BlockSpec((MM_TM, K), lambda i, pr: (i, 0))
    return pl.pallas_call(
        body, name="in_proj_own",
        grid_spec=pltpu.PrefetchScalarGridSpec(
            num_scalar_prefetch=1, grid=(M // MM_TM,),
            in_specs=[row, pl.BlockSpec((1, K), lambda i, pr: (0, 0)), pl.BlockSpec((None, K, n), lambda i, pr: (0, 0, 0))],
            out_specs=[row, pl.BlockSpec((MM_TM, n), lambda i, pr: (i, pr[0]))]),
        out_shape=[SDS((M, K), BF16), SDS((M, 4 * n), BF16)])(part, x, nw, w)


def _in_proj_parts(xn, w, parts, z, name):
    M, K = xn.shape
    _, _, n = w.shape
    P = parts.shape[0]

    def body(p_ref, a_ref, b_ref, z_in, o_ref):
        del p_ref, z_in
        o_ref[...] = jnp.dot(a_ref[...], b_ref[...], preferred_element_type=F32).astype(BF16)

    return pl.pallas_call(
        body, name=name,
        grid_spec=pltpu.PrefetchScalarGridSpec(
            num_scalar_prefetch=1, grid=(P, M // MM_TM),
            in_specs=[pl.BlockSpec((MM_TM, K), lambda g, i, pr: (i, 0)), pl.BlockSpec((None, K, n), lambda g, i, pr: (pr[g], 0, 0)),
                      pl.BlockSpec(memory_space=pl.ANY)],
            out_specs=pl.BlockSpec((MM_TM, n), lambda g, i, pr: (i, pr[g]))),
        out_shape=SDS(z.shape, BF16), input_output_aliases={3: 0})(parts, xn, w, z)


def _local_step(x, mem, target, sp, ex):
    S, D = x.shape
    band, buckets = _bias_static()
    buckets = jnp.asarray(buckets)

    tok = ex.start_first()
    own, me, near, far = ex.own_w_in()
    xn, z = _in_proj_own(x, _after(sp["attn_norm_w"], tok), own, me)
    z = _in_proj_parts(xn, ex.near_weights(after=z), near, z, "in_proj_near")
    w_in, gathered_small = ex.first_weights(after=z)
    sp = {**sp, **gathered_small}
    n_in = 4 * w_in.shape[2]
    bw = {"w_in": w_in}
    z = _in_proj_parts(xn, w_in, far + ex.start_rest()[0, 0].astype(jnp.int32), z, "in_proj_far")
    qh, kh, vh, q2, k2, v2 = _qkv_prep(z, sp["q_norm_w"], sp["k_norm_w"])
    tab = sp["rel_bias_table"].T.reshape(N_GROUPS, HPG, N_BUCKETS)
    bias = _bias_fwd(jnp.pad(tab, ((0, 0), (0, 8 - HPG), (0, 0))), buckets)
    biasm = jnp.where(jnp.asarray(band)[None, None], bias[:, :HPG].reshape(N_GROUPS, HPG, NQ, 2 * NQ), NEG)
    seqs = lambda a: a.reshape(-1, S // P4, HEAD)
    q12, k12, v12 = seqs(q2), seqs(k2), seqs(v2)
    groups = ((qh, kh, vh, 1, 0, HPG, 0), (q12, k12, v12, 1, 0, HPG * P4, 2), (q12, k12, v12, 4, HPG * P4, HPG * P4, 2))
    os_, lses = [], []
    for g, (qg, kg, vg, dil, head0, nseq, shift) in enumerate(groups):
        o_g, lse_g = _attn_fwd(qg, kg, vg, biasm[g], f"attn_fwd_g{g}", dil, head0, nseq, shift)
        os_.append(o_g.reshape(HPG, -1, o_g.shape[1], HEAD) if g > 0 else o_g)
        lses.append(lse_g.reshape(HPG, -1, lse_g.shape[1], HEAD) if g > 0 else lse_g)
    attn = _merge_fwd(os_, lses)
    tok = ex.rest_arrived(after=attn)
    u1, u3 = _conv_fwd(z, sp["conv_dw_w"], sp["conv_dw_b"], _after(sp["conv_ln_w"], tok), sp["conv_ln_b"])
    bw.update(ex.rest_weights(after=u3))
    F2 = 4 * bw["w_up"].shape[2]
    mn, kv, mk, mv = _memkv_fwd(mem, sp["mem_norm_w"], bw["w_mem_kv"], sp["xk_norm_w"])
    oc = _cross_fwd(z, sp["xq_norm_w"], mk, mv)
    h1, hn = _outproj_fwd(x, z, sp["b_gate"], attn, u3, oc, bw["w_attn_o"], bw["w_conv_o"], bw["w_cross_o"], bw["w_out"],
                          sp["ffn_norm_w"])
    up = _mm_nn(hn, bw["w_up"], BF16, "ffn_up")
    dy, loss_tile, ac, gc = _ffn_fwd(up, sp["ffn_conv_w"], sp["ffn_conv_b"], bw["w_down"], h1, target)

    gs, gb = {}, {}
    dca, dcg, acca, accg, gb["w_down"] = _ffn_bwd_a(dy, bw["w_down"], ac, gc)
    cols = lambda acc: jnp.sum(acc, axis=1).reshape(1, F2 // 2)
    gs["ffn_conv_b"] = jnp.concatenate([cols(acca), cols(accg)], axis=1)
    dup, tacc = _ffn_bwd_b(dca, dcg, up, sp["ffn_conv_w"])
    gs["ffn_conv_w"] = jnp.transpose(jnp.sum(tacc, axis=2), (1, 0, 2)).reshape(FFN_K, F2)
    gb["w_up"] = _mm_tn(hn, dup, 4, "dw_up")
    tok = ex.reduce_begin("a", ("w_down", "w_up"), gb)
    dh1, gs["ffn_norm_w"] = _norm_in_bwd(dup, bw["w_up"], h1, _after(sp["ffn_norm_w"], tok), dy, "ffn_in_bwd")
    tok = ex.reduce_mid("a", after=gs["ffn_norm_w"])
    dz, dattn, du3, doc, gs["b_gate"], gb["w_out"], gb["w_attn_o"], gb["w_conv_o"], gb["w_cross_o"] = _outproj_bwd(
        dh1, z, _after(sp["b_gate"], tok), attn, u3, oc, bw["w_attn_o"], bw["w_conv_o"], bw["w_cross_o"], bw["w_out"], n_in)
    dz, dmk, dmv, gs["xq_norm_w"] = _cross_bwd(dz, doc, z, sp["xq_norm_w"], mk, mv)
    gb["w_mem_kv"], gs["xk_norm_w"], gs["mem_norm_w"] = _memkv_bwd(
        dmk, dmv, kv, mem, mn, sp["mem_norm_w"], bw["w_mem_kv"], sp["xk_norm_w"])
    ex.reduce_end("a", after=gs["mem_norm_w"])
    tok = ex.reduce_begin("b", ("w_out", "w_attn_o", "w_conv_o", "w_cross_o", "w_mem_kv"), gb)
    du1, cacc = _conv_bwd_a(du3, u1, z, _after(sp["conv_ln_w"], tok), sp["conv_ln_b"])
    gs["conv_dw_w"], gs["conv_dw_b"] = cacc[:CONV_K], cacc[32:33]
    gs["conv_ln_w"], gs["conv_ln_b"] = cacc[33:34], cacc[34:35]
    tok = ex.reduce_mid("b", after=cacc)
    dz = _conv_bwd_b(dz, du1, z, _after(sp["conv_dw_w"], tok))
    merged_grads = _merge_bwd(dattn, os_, lses)
    dqs, dks, dvs, dsbs = [], [], [], []
    for g, (qg, kg, vg, dil, head0, nseq, shift) in enumerate(groups):
        wg_g, da_g, dh_g = merged_grads[g]
        lse_g = lses[g]
        if g > 0:
            wg_g, da_g, dh_g, lse_g = seqs(wg_g), seqs(da_g), seqs(dh_g), seqs(lse_g)
        dq_g, dk_g, dv_g, dsb_g = _attn_bwd(qg, kg, vg, biasm[g], da_g, wg_g, dh_g, lse_g, f"attn_bwd_g{g}", dil, head0, nseq, shift)
        if g > 0:
            dq_g, dk_g, dv_g = (t.reshape(HPG, P4, S // P4, HEAD) for t in (dq_g, dk_g, dv_g))
            dsb_g = jnp.sum(dsb_g.reshape(HPG, P4, NQ, 2 * NQ), axis=1)
        dqs.append(dq_g)
        dks.append(dk_g)
        dvs.append(dv_g)
        dsbs.append(dsb_g.reshape(HPG, NQ * 2 * NQ))
    dtab = _bias_bwd(jnp.pad(jnp.stack(dsbs), ((0, 0), (0, 8 - HPG), (0, 0))), buckets)
    gs["rel_bias_table"] = dtab[:, :HPG].reshape(N_GROUPS * HPG, N_BUCKETS).T
    dz, gs["q_norm_w"], gs["k_norm_w"] = _qkv_bwd(dz, z, dqs, dks, dvs, sp["q_norm_w"], sp["k_norm_w"])
    ex.reduce_end("b", after=gs["q_norm_w"])
    gb["w_in"], dw_in_wire = _mm_tn(xn, dz, 4, "dw_in", wire=True)
    tok = ex.reduce_mid("c", after=ex.reduce_begin("c", ("w_in",), gb, wires=[dw_in_wire]))
    dx, gs["attn_norm_w"] = _norm_in_bwd(dz, bw["w_in"], x, _after(sp["attn_norm_w"], tok), dh1, "in_bwd")
    ex.reduce_end("c", after=gs["attn_norm_w"])
    return loss_tile, dx, gs, gb


SMALL = ("rel_bias_table", "attn_norm_w", "b_gate", "q_norm_w", "k_norm_w", "conv_dw_w", "conv_dw_b", "conv_ln_w", "conv_ln_b",
         "mem_norm_w", "xq_norm_w", "xk_norm_w", "ffn_norm_w", "ffn_conv_w", "ffn_conv_b")
SMALL_SHARDED = ("conv_dw_w", "ffn_conv_w")
BIG_COL = ("w_in", "w_attn_o", "w_conv_o", "w_cross_o", "w_up")
BIG_ROW = ("w_mem_kv", "w_out", "w_down")
BIG = BIG_COL + BIG_ROW
WEIGHTS = ("rel_bias_table", "attn_norm_w", "w_in", "b_gate", "q_norm_w", "k_norm_w", "w_attn_o", "conv_dw_w", "conv_dw_b",
           "conv_ln_w", "conv_ln_b", "w_conv_o", "mem_norm_w", "w_mem_kv", "xq_norm_w", "xk_norm_w", "w_cross_o", "w_out",
           "ffn_norm_w", "w_up", "ffn_conv_w", "ffn_conv_b", "w_down")


class _Exchanges:
    REST = tuple(k for k in BIG if k != "w_in")

    def __init__(self, w):
        self.w = w
        self.pending = {}
        self.reduced = {}

    def _whole(self, k, ga):
        ga = ga.reshape((4,) + self.w[k].shape)
        return ga if k in BIG_COL else ga.reshape((4 * self.w[k].shape[0],) + self.w[k].shape[1:])

    def start_first(self):
        self.w_in_local = self.w["w_in"].astype(BF16)
        local = [_halves(self.w_in_local)]
        for k in SMALL_SHARDED:
            flat = jnp.ravel(self.w[k])
            local.append(jnp.pad(flat, (0, (-flat.shape[0]) % 2048)).reshape(2, -1, 128))
        near = _ag_ici_start(local, "gather_near_start", js=(0, 1))
        far = _ag_ici_start(near[2], "gather_far_start", js=(2,), lands=near[3])
        self.pending["first"] = (near[:2], far[:2], far[2], far[3])
        return far[4]

    def own_w_in(self):
        x, y, _ = _position()
        me = 2 * x + y
        ids = lambda *v: jnp.stack(v).astype(jnp.int32)
        return self.w_in_local[None], ids(me), ids(me ^ 2, me ^ 1), ids(me ^ 3)

    def near_weights(self, after):
        near, far, ins, lands = self.pending.pop("first")
        ins, lands = _ag_ici_wait(*near, ins, lands, after, "gather_near_wait", js=(0, 1))
        ins, lands = _ag_finish(ins, lands, "gather_near_finish", js=(0, 1))
        self.pending["first"] = (far, ins, lands)
        return self._whole("w_in", lands[0])

    def first_weights(self, after):
        far, ins, lands = self.pending.pop("first")
        ins, lands = _ag_ici_wait(*far, ins, lands, after, "gather_far_wait", js=(2,))
        _, gathered = _ag_finish(ins, lands, "gather_far_finish", js=(2,), own=False)
        self.first = gathered[0]
        small = {}
        for k, ga in zip(SMALL_SHARDED, gathered[1:]):
            r, cdim = self.w[k].shape
            parts = ga.reshape(4, -1)[:, :r * cdim].reshape(4, r, cdim)
            small[k] = jnp.transpose(parts, (1, 0, 2)).reshape(r, 4 * cdim)
        return self._whole("w_in", gathered[0]), small

    def start_rest(self):
        local = [_halves(self.w[k].astype(BF16)) for k in self.REST]
        local, _ = lax.optimization_barrier((local, self.first))
        send_sems, recv_sems, ins, lands, token = _ag_ici_start(local, "gather_rest_start")
        self.pending["rest"] = (send_sems, recv_sems, ins, lands)
        return token

    def rest_arrived(self, after):
        send_sems, recv_sems, ins, lands = self.pending.pop("rest")
        ins, lands = _ag_ici_wait(send_sems, recv_sems, ins, lands, after, "gather_rest_wait")
        send_sems, recv_sems, ins, lands, token = _ag_finish_start(ins, lands, "gather_rest_finish_start")
        self.pending["rest"] = (send_sems, recv_sems, ins, lands)
        return token

    def rest_weights(self, after):
        _, gathered = _ag_finish_wait(*self.pending.pop("rest"), after, "gather_rest_finish_wait")
        return {k: self._whole(k, ga) for k, ga in zip(self.REST, gathered)}

    def reduce_begin(self, tag, names, gb, wires=None):
        parts = [gb[k].reshape((4,) + self.w[k].shape) for k in names]
        state, token = _reduce_begin(parts, tag, wires)
        self.pending[tag] = (state, names)
        return token

    def reduce_mid(self, tag, after):
        state, names = self.pending.pop(tag)
        state, token = _reduce_mid(state, after, tag)
        self.pending[tag] = (state, names)
        return token

    def reduce_end(self, tag, after):
        state, names = self.pending.pop(tag)
        self.reduced.update(zip(names, _reduce_end(state, after, tag)))


def _step(x, mem, target, w, m, v):
    xi, yi, _ = _position()
    shard = 2 * xi + yi
    ex = _Exchanges(w)
    sp = {k: w[k] for k in SMALL if k not in SMALL_SHARDED}
    loss_tile, dx, gs, _ = _local_step(x, mem, target, sp, ex)
    g_big = ex.reduced

    red = _all_reduce_small([loss_tile] + [gs[k] for k in SMALL])
    loss = red[0][0, 0]
    g_small = dict(zip(SMALL, red[1:]))
    for k in SMALL_SHARDED:
        cdim = w[k].shape[1]
        g_small[k] = lax.dynamic_slice_in_dim(g_small[k], shard * cdim, cdim, axis=1)

    grads, delta, new_m, new_v = {}, {}, {}, {}
    for k in BIG:
        grads[k] = g_big[k]
        delta[k], new_m[k], new_v[k] = _adamw(w[k], g_big[k], m[k], v[k], "adamw_" + k)
    outs = _adamw_small([w[k] for k in SMALL], [g_small[k] for k in SMALL], [m[k] for k in SMALL], [v[k] for k in SMALL])
    for dst, vals in zip((delta, new_m, new_v), outs):
        dst.update(zip(SMALL, vals))
    grads.update(g_small)
    return loss, dx, grads, delta, new_m, new_v


def kernel(x, mem, rel_bias_table, attn_norm_w, w_in, b_gate, q_norm_w, k_norm_w, w_attn_o, conv_dw_w, conv_dw_b, conv_ln_w, conv_ln_b, w_conv_o, mem_norm_w, w_mem_kv, xq_norm_w, xk_norm_w, w_cross_o, w_out, ffn_norm_w, w_up, ffn_conv_w, ffn_conv_b, w_down, loss_target, m_rel_bias_table, m_attn_norm_w, m_w_in, m_b_gate, m_q_norm_w, m_k_norm_w, m_w_attn_o, m_conv_dw_w, m_conv_dw_b, m_conv_ln_w, m_conv_ln_b, m_w_conv_o, m_mem_norm_w, m_w_mem_kv, m_xq_norm_w, m_xk_norm_w, m_w_cross_o, m_w_out, m_ffn_norm_w, m_w_up, m_ffn_conv_w, m_ffn_conv_b, m_w_down, v_rel_bias_table, v_attn_norm_w, v_w_in, v_b_gate, v_q_norm_w, v_k_norm_w, v_w_attn_o, v_conv_dw_w, v_conv_dw_b, v_conv_ln_w, v_conv_ln_b, v_w_conv_o, v_mem_norm_w, v_w_mem_kv, v_xq_norm_w, v_xk_norm_w, v_w_cross_o, v_w_out, v_ffn_norm_w, v_w_up, v_ffn_conv_w, v_ffn_conv_b, v_w_down):
    args = locals()
    def block(name, k):
        a = args[name] if k == "rel_bias_table" else args[name][0]
        return a.reshape(1, -1) if a.ndim == 1 else a

    w = {k: block(k, k) for k in WEIGHTS}
    m = {k: block("m_" + k, k) for k in WEIGHTS}
    v = {k: block("v_" + k, k) for k in WEIGHTS}
    loss, dx, grads, delta, new_m, new_v = _step(x[0], mem[0], loss_target[0], w, m, v)
    out = [loss, dx[None]]
    for d in (grads, delta, new_m, new_v):
        for k in WEIGHTS:
            out.append(d[k].reshape(args[k].shape))
    return tuple(out)
```
